```python
import math
import jax, jax.numpy as jnp
from jax import lax
import numpy as np

D_MODEL = 2048
BATCH = 2
SEQ = 4096
DEPTH = 1
DEC_BATCH = 8
DEC_SEQ = 1
PAST_LEN = 16384
PAGE_SIZE = 128

D_MIX = D_MODEL
ML_HEADS = 4
ML_HEAD_DIM = D_MIX // (2 * ML_HEADS)
D_ML = ML_HEADS * ML_HEAD_DIM
ML_CHUNK = 64
CONV_W = 4
NSA_HEADS = 16
NSA_HEAD_DIM = (D_MIX - D_ML) // NSA_HEADS
D_NSA = NSA_HEADS * NSA_HEAD_DIM
NSA_KV_HEADS = 4
NSA_GROUP = NSA_HEADS // NSA_KV_HEADS
D_KV = NSA_KV_HEADS * NSA_HEAD_DIM
CMP_BLOCK = 32
CMP_STRIDE = 16
CMP_HIDDEN = 256
SLC_BLOCK = 64
SLC_TOPN = 16
WINDOW = 512
Q_BLOCK = 128
N_BUCKETS = 32
MAX_DISTANCE = 2048
N_GROUPS = 4
EXPERTS_PER_GROUP = 8
N_EXPERTS = N_GROUPS * EXPERTS_PER_GROUP
TOP_K_IN_GROUP = 2
D_EXPERT = 512
D_IN_PROJ = 4 * D_ML + 2 * ML_HEADS + D_NSA + 6 * D_KV + 3 * NSA_HEADS
EPS = 1e-6
NEG_BIG = -1e30
FORCE_SCORE = 1e4

kernel_name = 'hymba_mlstm_nsa_hmoe_step'


def rmsnorm(x, g):
    xf = x.astype(jnp.float32)
    y = xf * lax.rsqrt(jnp.mean(xf * xf, axis=-1, keepdims=True) + EPS)
    return (y * g.astype(jnp.float32)).astype(x.dtype)


def masked_softmax(s, valid):
    p = jax.nn.softmax(jnp.where(valid, s, NEG_BIG), axis=-1)
    return jnp.where(valid, p, 0.0)


def t5_bucket(dist):
    n = jnp.maximum(dist, 0)
    max_exact = N_BUCKETS // 2
    nf = jnp.maximum(n, 1).astype(jnp.float32)
    large = max_exact + (jnp.log(nf / max_exact) / math.log(MAX_DISTANCE / max_exact)
                         * (N_BUCKETS - max_exact)).astype(jnp.int32)
    return jnp.where(n < max_exact, n, jnp.minimum(large, N_BUCKETS - 1))


def split_projection(u):
    sizes = [D_ML, D_ML, D_ML, D_ML, ML_HEADS, ML_HEADS, D_NSA, 2 * D_KV, 2 * D_KV, 2 * D_KV, 3 * NSA_HEADS]
    return jnp.split(u, np.cumsum(sizes)[:-1].tolist(), axis=-1)


def mlstm_chunk(carry, xs):
    C, n, m = carry
    q, k, v, ig, lf = xs
    T = q.shape[1]
    b = jnp.cumsum(lf, axis=1)
    causal = jnp.tril(jnp.ones((T, T), dtype=bool))
    logd = b[:, :, None, :] - b[:, None, :, :] + ig[:, None, :, :]
    logd = jnp.where(causal[None, :, :, None], logd, -jnp.inf)
    inter = b + m[:, None, :]
    m_t = jnp.maximum(inter, jnp.max(logd, axis=2))
    w_intra = jnp.exp(logd - m_t[:, :, None, :])
    w_inter = jnp.exp(inter - m_t)
    sc = jnp.einsum('bthd,bshd->btsh', q, k) * w_intra
    num = jnp.einsum('btsh,bshd->bthd', sc, v) + w_inter[..., None] * jnp.einsum('bthk,bhkv->bthv', q, C)
    den = jnp.sum(sc, axis=2) + w_inter * jnp.einsum('bthk,bhk->bth', q, n)
    h = num / jnp.maximum(jnp.abs(den), jnp.exp(-m_t))[..., None]
    m_new = m_t[:, -1]
    w_s = jnp.exp(b[:, -1:, :] - b + ig - m_new[:, None, :])
    decay = jnp.exp(b[:, -1] + m - m_new)
    C_new = decay[..., None, None] * C + jnp.einsum('bsh,bshk,bshv->bhkv', w_s, k, v)
    n_new = decay[..., None] * n + jnp.einsum('bsh,bshk->bhk', w_s, k)
    return (C_new, n_new, m_new), h


def mlstm_mixer(q_pre, k_pre, v, o_pre, ig_pre, fg_pre, conv_buf, C0, n0, m0, conv_w, b_ig, b_fg, ml_norm_g):
    B, L, _ = v.shape
    f32 = jnp.float32
    qk_pre = jnp.concatenate([q_pre, k_pre], axis=-1)
    full = jnp.concatenate([conv_buf.astype(qk_pre.dtype), qk_pre], axis=1)
    conv = sum(full[:, j:j + L] * conv_w[j] for j in range(CONV_W))
    new_buf = full[:, L:]
    q, k = jnp.split(jax.nn.silu(conv), 2, axis=-1)
    hd = (B, L, ML_HEADS, ML_HEAD_DIM)
    q = q.reshape(hd).astype(f32)
    k = k.reshape(hd).astype(f32) * ML_HEAD_DIM ** -0.5
    v = v.reshape(hd).astype(f32)
    ig = ig_pre.astype(f32) + b_ig.astype(f32)
    lf = jax.nn.log_sigmoid(fg_pre.astype(f32) + b_fg.astype(f32))
    chunk = ML_CHUNK if L % ML_CHUNK == 0 else L
    nc = L // chunk

    def to_chunks(a):
        return a.reshape((B, nc, chunk) + a.shape[2:]).swapaxes(0, 1)

    (C, n, m), h = lax.scan(mlstm_chunk, (C0.astype(f32), n0.astype(f32), m0.astype(f32)),
                            (to_chunks(q), to_chunks(k), to_chunks(v), to_chunks(ig), to_chunks(lf)))
    h = h.swapaxes(0, 1).reshape(hd)
    h = h * lax.rsqrt(jnp.mean(h * h, axis=-1, keepdims=True) + EPS) * ml_norm_g.astype(f32).reshape(ML_HEADS, ML_HEAD_DIM)
    y = h.reshape(B, L, D_ML) * jax.nn.sigmoid(o_pre.astype(f32))
    return y.astype(v_dtype_of(q_pre)), new_buf, C, n, m


def v_dtype_of(a):
    return a.dtype


def gqa_attention(q, k, v, dist, valid, rel_table):
    Q, K = dist.shape
    s = jnp.einsum('bqhgd,bkhd->bhgqk', q, k).astype(jnp.float32) * NSA_HEAD_DIM ** -0.5
    bias = rel_table.astype(jnp.float32)[t5_bucket(dist)].reshape(Q, K, NSA_KV_HEADS, NSA_GROUP).transpose(2, 3, 0, 1)
    p = masked_softmax(s + bias, valid)
    o = jnp.einsum('bhgqk,bkhd->bqhgd', p.astype(v.dtype), v)
    return o, p


def compress_blocks(kv, cmp_pos, cmp_w1, cmp_w2):
    B, T = kv.shape[:2]
    n_chunks = T // CMP_STRIDE
    ratio = CMP_BLOCK // CMP_STRIDE
    ch = kv[:, :n_chunks * CMP_STRIDE].reshape(B, n_chunks, CMP_STRIDE, 2, NSA_KV_HEADS, NSA_HEAD_DIM)
    n_cmp = n_chunks - ratio + 1
    blocks = jnp.concatenate([ch[:, j:j + n_cmp] for j in range(ratio)], axis=2)
    blocks = blocks + cmp_pos.swapaxes(0, 1)[:, :, None, :]
    hid = jax.nn.gelu(jnp.einsum('bnlchd,cldf->bnchf', blocks, cmp_w1))
    return jnp.einsum('bnchf,cfd->bnchd', hid, cmp_w2)


def compressed_branch(q, q_pos, kv_ctx, cmp_pos, cmp_w1, cmp_w2, rel_table):
    kvc = compress_blocks(kv_ctx, cmp_pos, cmp_w1, cmp_w2)
    n_cmp = kvc.shape[1]
    end = jnp.arange(n_cmp, dtype=jnp.int32) * CMP_STRIDE + (CMP_BLOCK - 1)
    dist = q_pos[:, None] - end[None, :]
    o, p = gqa_attention(q, kvc[:, :, 0], kvc[:, :, 1], dist, dist >= 0, rel_table)
    return o, p.sum(axis=2)


def cmp_to_slc_overlap(n_cmp, n_slc):
    c0 = np.arange(n_cmp) * CMP_STRIDE
    s0 = np.arange(n_slc) * SLC_BLOCK
    ov = np.minimum(c0[:, None] + CMP_BLOCK, s0[None, :] + SLC_BLOCK) - np.maximum(c0[:, None], s0[None, :])
    return jnp.asarray(np.clip(ov, 0, None).astype(np.float32) / CMP_BLOCK)


def selected_branch(q, q_pos, kv_ctx, p_cmp, rel_table):
    B, T = kv_ctx.shape[:2]
    Q = q.shape[1]
    n_slc = -(-T // SLC_BLOCK)
    kv_pad = jnp.pad(kv_ctx, ((0, 0), (0, n_slc * SLC_BLOCK - T), (0, 0), (0, 0), (0, 0)))
    kvb = kv_pad.reshape(B, n_slc, SLC_BLOCK, 2, NSA_KV_HEADS, NSA_HEAD_DIM).transpose(0, 4, 1, 2, 3, 5)
    score = jnp.einsum('bhqn,ns->bhqs', p_cmp, cmp_to_slc_overlap(p_cmp.shape[-1], n_slc))
    blk = jnp.arange(n_slc, dtype=jnp.int32)[None, :]
    cur = (q_pos // SLC_BLOCK)[:, None]
    forced = (blk == 0) | (blk == cur) | (blk == cur - 1)
    score = jnp.where(forced, FORCE_SCORE, score)
    score = jnp.where(blk > cur, -FORCE_SCORE, score)
    top_n = min(SLC_TOPN, n_slc)
    _, idx = lax.top_k(score, top_n)
    qb = Q_BLOCK if Q % Q_BLOCK == 0 else Q
    nqb = Q // qb
    q_blocks = q.reshape(B, nqb, qb, NSA_KV_HEADS, NSA_GROUP, NSA_HEAD_DIM).swapaxes(0, 1)
    idx_blocks = idx.reshape(B, NSA_KV_HEADS, nqb, qb, top_n).transpose(2, 0, 1, 3, 4)
    pos_blocks = q_pos.reshape(nqb, qb)
    b_ix = jnp.arange(B)[:, None, None, None]
    h_ix = jnp.arange(NSA_KV_HEADS)[None, :, None, None]
    table = rel_table.astype(jnp.float32).reshape(N_BUCKETS, NSA_KV_HEADS, NSA_GROUP)

    def attend_block(args):
        qblk, iblk, pblk = args
        sel = kvb[b_ix, h_ix, iblk]
        kpos = iblk[..., None] * SLC_BLOCK + jnp.arange(SLC_BLOCK, dtype=jnp.int32)
        dist = pblk[None, None, :, None, None] - kpos
        s = jnp.einsum('bqhgd,bhqnld->bhgqnl', qblk, sel[..., 0, :]).astype(jnp.float32) * NSA_HEAD_DIM ** -0.5
        bias = table[t5_bucket(dist), h_ix[..., None]].transpose(0, 1, 5, 2, 3, 4)
        valid = jnp.broadcast_to((dist >= 0)[:, :, None], s.shape)
        flat = (B, NSA_KV_HEADS, NSA_GROUP, qb, top_n * SLC_BLOCK)
        p = masked_softmax((s + bias).reshape(flat), valid.reshape(flat)).reshape(s.shape)
        return jnp.einsum('bhgqnl,bhqnld->bqhgd', p.astype(sel.dtype), sel[..., 1, :])

    o = lax.map(attend_block, (q_blocks, idx_blocks, pos_blocks))
    return o.swapaxes(0, 1).reshape(B, Q, NSA_KV_HEADS, NSA_GROUP, NSA_HEAD_DIM)


def window_branch_prompt(q, kv, rel_table):
    B, S = q.shape[:2]
    nb = S // Q_BLOCK
    nback = WINDOW // Q_BLOCK
    kvp = jnp.pad(kv, ((0, 0), (nback * Q_BLOCK, 0), (0, 0), (0, 0), (0, 0)))
    kvp = kvp.reshape(B, nb + nback, Q_BLOCK, 2, NSA_KV_HEADS, NSA_HEAD_DIM)
    band = jnp.arange(nb)[:, None] + jnp.arange(nback + 1)[None, :]
    kv_band = kvp[:, band].reshape(B, nb, (nback + 1) * Q_BLOCK, 2, NSA_KV_HEADS, NSA_HEAD_DIM)
    q_pos = jnp.arange(S, dtype=jnp.int32).reshape(nb, Q_BLOCK)
    k_pos = (((band - nback) * Q_BLOCK)[:, :, None] + jnp.arange(Q_BLOCK)).reshape(nb, -1).astype(jnp.int32)
    dist = q_pos[:, :, None] - k_pos[:, None, :]
    valid = (dist >= 0) & (dist < WINDOW) & (k_pos[:, None, :] >= 0)
    qb = q.reshape(B, nb, Q_BLOCK, NSA_KV_HEADS, NSA_GROUP, NSA_HEAD_DIM)
    o, _ = jax.vmap(gqa_attention, in_axes=(1, 1, 1, 0, 0, None), out_axes=(1, 1))(
        qb, kv_band[:, :, :, 0], kv_band[:, :, :, 1], dist, valid, rel_table)
    return o.reshape(B, S, NSA_KV_HEADS, NSA_GROUP, NSA_HEAD_DIM)


def window_branch_sample(q, q_pos, kv_new, win_buf, rel_table):
    Wb, L = win_buf.shape[1], kv_new.shape[1]
    kv = jnp.concatenate([win_buf.astype(kv_new.dtype), kv_new], axis=1)
    k_pos = q_pos[0] - Wb + jnp.arange(Wb + L, dtype=jnp.int32)
    dist = q_pos[:, None] - k_pos[None, :]
    valid = (dist >= 0) & (dist < WINDOW)
    o, _ = gqa_attention(q, kv[:, :, 0], kv[:, :, 1], dist, valid, rel_table)
    return o, kv[:, L:]


def mixer_sublayer(h, q_pos, conv_buf, C0, n0, m0, past_cmp, past_slc, win_buf,
                   w_in, b_ig, b_fg, conv_w, ml_norm_g, cmp_pos, cmp_w1, cmp_w2, nsa_norm_g, w_out, rel_table):
    B, L, _ = h.shape
    (q_ml, k_ml, v_ml, o_ml, ig_pre, fg_pre, q_nsa, kv_c, kv_s, kv_w, gate_pre) = split_projection(h @ w_in)
    y_ml, new_conv, C, n, m = mlstm_mixer(q_ml, k_ml, v_ml, o_ml, ig_pre, fg_pre, conv_buf, C0, n0, m0,
                                          conv_w, b_ig, b_fg, ml_norm_g)
    kv_shape = (B, L, 2, NSA_KV_HEADS, NSA_HEAD_DIM)
    kv_c, kv_s, kv_w = kv_c.reshape(kv_shape), kv_s.reshape(kv_shape), kv_w.reshape(kv_shape)
    q = q_nsa.reshape(B, L, NSA_KV_HEADS, NSA_GROUP, NSA_HEAD_DIM)
    if past_cmp is None:
        ctx_c, ctx_s = kv_c, kv_s
    else:
        ctx_c = jnp.concatenate([past_cmp.astype(kv_c.dtype), kv_c], axis=1)
        ctx_s = jnp.concatenate([past_slc.astype(kv_s.dtype), kv_s], axis=1)
    o_c, p_cmp = compressed_branch(q, q_pos, ctx_c, cmp_pos, cmp_w1, cmp_w2, rel_table)
    o_s = selected_branch(q, q_pos, ctx_s, p_cmp, rel_table)
    if win_buf is None:
        o_w = window_branch_prompt(q, kv_w, rel_table)
        new_win = kv_w[:, L - min(WINDOW, L):]
    else:
        o_w, new_win = window_branch_sample(q, q_pos, kv_w, win_buf, rel_table)
    g = jax.nn.sigmoid(gate_pre.astype(jnp.float32)).reshape(B, L, 3, NSA_KV_HEADS, NSA_GROUP, 1)
    o_nsa = g[:, :, 0] * o_c + g[:, :, 1] * o_s + g[:, :, 2] * o_w
    o_nsa = rmsnorm(o_nsa.reshape(B, L, D_NSA), nsa_norm_g).astype(h.dtype)
    out = jnp.concatenate([y_ml, o_nsa], axis=-1) @ w_out
    return out, (kv_c, kv_s, new_win, C, n, m, new_conv)


def hier_moe(h, w_rg, b_rg, w_re, b_re, w_gate, w_up, w_down):
    B, L, D = h.shape
    f32 = jnp.float32
    x = h.reshape(B * L, D)
    lg = (x @ w_rg).astype(f32) + b_rg.astype(f32)
    g_top = jnp.argmax(lg, axis=-1)
    g_w = jnp.take_along_axis(jax.nn.softmax(lg, axis=-1), g_top[:, None], axis=-1)
    le = ((x @ w_re).astype(f32) + b_re.astype(f32)).reshape(-1, N_GROUPS, EXPERTS_PER_GROUP)
    le_g = jnp.take_along_axis(le, g_top[:, None, None], axis=1)[:, 0]
    top_v, top_i = lax.top_k(le_g, TOP_K_IN_GROUP)
    w_sel = jax.nn.softmax(top_v, axis=-1) * g_w
    expert_id = g_top[:, None] * EXPERTS_PER_GROUP + top_i
    gate = jnp.sum(jax.nn.one_hot(expert_id, N_EXPERTS, dtype=f32) * w_sel[..., None], axis=1)
    a = jnp.einsum('nd,edf->nef', x, w_gate)
    u = jnp.einsum('nd,edf->nef', x, w_up)
    hid = jax.nn.silu(a) * u * gate[..., None].astype(a.dtype)
    return jnp.einsum('nef,efd->nd', hid, w_down).reshape(B, L, D)


def setup_inputs(seed: int = 0) -> dict:
    key = jax.random.key(seed)
    keys = jax.random.split(key, 32)
    f32 = jnp.float32

    def nrm(i, shape, scale):
        return jax.random.normal(keys[i], shape, f32) * scale

    n_pages = PAST_LEN // PAGE_SIZE
    n_used = DEC_BATCH * n_pages
    n_pool = n_used + max(1, n_used // 4)
    win_len = min(WINDOW, PAST_LEN)
    kv_tail = (2, NSA_KV_HEADS, NSA_HEAD_DIM)
    page_table = jax.random.permutation(keys[9], n_pool)[:n_used].reshape(DEC_BATCH, n_pages).astype(jnp.int32)
    return {
        'x_prompt': nrm(0, (BATCH, SEQ, D_MODEL), 1.0),
        'x_sample': nrm(1, (DEC_BATCH, DEC_SEQ, D_MODEL), 1.0),
        'cache_cmp_kv': nrm(2, (DEPTH, n_pool, PAGE_SIZE) + kv_tail, 1.0),
        'cache_slc_kv': nrm(3, (DEPTH, n_pool, PAGE_SIZE) + kv_tail, 1.0),
        'cache_win_kv': nrm(4, (DEPTH, DEC_BATCH, win_len) + kv_tail, 1.0),
        'state_mlstm_C': nrm(5, (DEPTH, DEC_BATCH, ML_HEADS, ML_HEAD_DIM, ML_HEAD_DIM), 0.1),
        'state_mlstm_n': nrm(6, (DEPTH, DEC_BATCH, ML_HEADS, ML_HEAD_DIM), 0.1),
        'state_mlstm_m': nrm(7, (DEPTH, DEC_BATCH, ML_HEADS), 1.0),
        'state_conv': nrm(8, (DEPTH, DEC_BATCH, CONV_W - 1, 2 * D_ML), 1.0),
        'page_table': page_table,
        'rel_bias': nrm(10, (N_BUCKETS, NSA_HEADS), 0.5),
        'norm_mix_g': 1.0 + nrm(11, (DEPTH, D_MODEL), 0.02),
        'w_in': nrm(12, (DEPTH, D_MODEL, D_IN_PROJ), D_MODEL ** -0.5),
        'b_ig': nrm(13, (DEPTH, ML_HEADS), 0.1),
        'b_fg': jnp.linspace(3.0, 6.0, ML_HEADS, dtype=f32)[None, :] + nrm(14, (DEPTH, ML_HEADS), 0.1),
        'conv_w': nrm(15, (DEPTH, CONV_W, 2 * D_ML), 0.5),
        'ml_norm_g': 1.0 + nrm(16, (DEPTH, D_ML), 0.02),
        'cmp_pos': nrm(17, (DEPTH, 2, CMP_BLOCK, NSA_HEAD_DIM), 0.1),
        'cmp_w1': nrm(18, (DEPTH, 2, CMP_BLOCK, NSA_HEAD_DIM, CMP_HIDDEN), (CMP_BLOCK * NSA_HEAD_DIM) ** -0.5),
        'cmp_w2': nrm(19, (DEPTH, 2, CMP_HIDDEN, NSA_HEAD_DIM), CMP_HIDDEN ** -0.5),
        'nsa_norm_g': 1.0 + nrm(20, (DEPTH, D_NSA), 0.02),
        'w_out': nrm(21, (DEPTH, D_MIX, D_MODEL), D_MIX ** -0.5),
        'norm_ffn_g': 1.0 + nrm(22, (DEPTH, D_MODEL), 0.02),
        'w_router_grp': nrm(23, (DEPTH, D_MODEL, N_GROUPS), D_MODEL ** -0.5),
        'b_router_grp': nrm(24, (DEPTH, N_GROUPS), 0.01),
        'w_router_exp': nrm(25, (DEPTH, D_MODEL, N_EXPERTS), D_MODEL ** -0.5),
        'b_router_exp': nrm(26, (DEPTH, N_EXPERTS), 0.01),
        'w_gate': nrm(27, (DEPTH, N_EXPERTS, D_MODEL, D_EXPERT), D_MODEL ** -0.5),
        'w_up': nrm(28, (DEPTH, N_EXPERTS, D_MODEL, D_EXPERT), D_MODEL ** -0.5),
        'w_down': nrm(29, (DEPTH, N_EXPERTS, D_EXPERT, D_MODEL), D_EXPERT ** -0.5),
        'norm_final_g': 1.0 + nrm(30, (D_MODEL,), 0.02),
    }


def reference(x_prompt, x_sample, cache_cmp_kv, cache_slc_kv, cache_win_kv, state_mlstm_C, state_mlstm_n,
              state_mlstm_m, state_conv, page_table, rel_bias, norm_mix_g, w_in, b_ig, b_fg, conv_w, ml_norm_g,
              cmp_pos, cmp_w1, cmp_w2, nsa_norm_g, w_out, norm_ffn_g, w_router_grp, b_router_grp, w_router_exp,
              b_router_exp, w_gate, w_up, w_down, norm_final_g):
    f32 = jnp.float32
    B, S, _ = x_prompt.shape
    DB, L, _ = x_sample.shape
    past_len = page_table.shape[1] * PAGE_SIZE
    pos_p = jnp.arange(S, dtype=jnp.int32)
    pos_s = past_len + jnp.arange(L, dtype=jnp.int32)

    def gather_pages(pool):
        rows = pool[page_table]
        return rows.reshape(DB, past_len, 2, NSA_KV_HEADS, NSA_HEAD_DIM)

    yp, ys = x_prompt, x_sample
    st_p, st_s = [], []
    for l in range(DEPTH):
        lw = (w_in[l], b_ig[l], b_fg[l], conv_w[l], ml_norm_g[l], cmp_pos[l], cmp_w1[l], cmp_w2[l],
              nsa_norm_g[l], w_out[l], rel_bias)
        moe_w = (w_router_grp[l], b_router_grp[l], w_router_exp[l], b_router_exp[l], w_gate[l], w_up[l], w_down[l])
        mix_p, new_p = mixer_sublayer(
            rmsnorm(yp, norm_mix_g[l]), pos_p,
            jnp.zeros((B, CONV_W - 1, 2 * D_ML), x_prompt.dtype),
            jnp.zeros((B, ML_HEADS, ML_HEAD_DIM, ML_HEAD_DIM), f32),
            jnp.zeros((B, ML_HEADS, ML_HEAD_DIM), f32),
            jnp.full((B, ML_HEADS), -jnp.inf, f32),
            None, None, None, *lw)
        yp = yp + mix_p
        yp = yp + hier_moe(rmsnorm(yp, norm_ffn_g[l]), *moe_w)
        mix_s, new_s = mixer_sublayer(
            rmsnorm(ys, norm_mix_g[l]), pos_s, state_conv[l], state_mlstm_C[l], state_mlstm_n[l], state_mlstm_m[l],
            gather_pages(cache_cmp_kv[l]), gather_pages(cache_slc_kv[l]), cache_win_kv[l], *lw)
        ys = ys + mix_s
        ys = ys + hier_moe(rmsnorm(ys, norm_ffn_g[l]), *moe_w)
        st_p.append(new_p)
        st_s.append(new_s)

    def stacked(states, i):
        return jnp.stack([s[i] for s in states])

    y_prompt = rmsnorm(yp, norm_final_g)
    y_sample = rmsnorm(ys, norm_final_g)
    cmp_kv_prompt, cmp_kv_sample = stacked(st_p, 0), stacked(st_s, 0)
    slc_kv_prompt, slc_kv_sample = stacked(st_p, 1), stacked(st_s, 1)
    win_kv_prompt, win_kv_sample = stacked(st_p, 2), stacked(st_s, 2)
    mlstm_C_prompt, mlstm_C_sample = stacked(st_p, 3), stacked(st_s, 3)
    mlstm_n_prompt, mlstm_n_sample = stacked(st_p, 4), stacked(st_s, 4)
    mlstm_m_prompt, mlstm_m_sample = stacked(st_p, 5), stacked(st_s, 5)
    conv_prompt, conv_sample = stacked(st_p, 6), stacked(st_s, 6)
    return (y_prompt, y_sample, cmp_kv_prompt, cmp_kv_sample, slc_kv_prompt, slc_kv_sample,
            win_kv_prompt, win_kv_sample, mlstm_C_prompt, mlstm_C_sample, mlstm_n_prompt, mlstm_n_sample,
            mlstm_m_prompt, mlstm_m_sample, conv_prompt, conv_sample)
```

```python
import functools
import math

import numpy as np
import jax
import jax.numpy as jnp
from jax import lax
from jax.experimental import pallas as pl
from jax.experimental.pallas import tpu as pltpu

F32 = jnp.float32
BF16 = jnp.bfloat16

D_MODEL = 2048
ML_HEADS = 4
ML_HEAD_DIM = 256
D_ML = 1024
CONV_W = 4
NSA_HEADS = 16
NSA_HEAD_DIM = 64
D_NSA = 1024
NSA_KV_HEADS = 4
NSA_GROUP = 4
D_KV = 256
CMP_BLOCK = 32
CMP_STRIDE = 16
CMP_HIDDEN = 256
SLC_BLOCK = 64
SLC_TOPN = 16
WINDOW = 512
N_BUCKETS = 32
MAX_DISTANCE = 2048
N_GROUPS = 4
EXPERTS_PER_GROUP = 8
N_EXPERTS = 32
D_EXPERT = 512
PAGE_SIZE = 128
EPS = 1e-6
NEG_BIG = -1e30
FORCE_SCORE = 1e4

LANE = 128
COL_QML, COL_KML, COL_VML, COL_OML = 0, 1024, 2048, 3072
COL_QNSA = 4096
COL_KVC, COL_KVS, COL_KVW = 5120, 5632, 6144
COL_SMALL = 6656
N_PROJ = 7168
VMEM_LIMIT = 56 * 1024 * 1024


def _cparams(sem, vmem=VMEM_LIMIT):
    return pltpu.CompilerParams(dimension_semantics=sem, vmem_limit_bytes=vmem)


def _split2(x):
    hi = x.astype(BF16)
    lo = (x - hi.astype(F32)).astype(BF16)
    return hi, lo


def _split3(x):
    hi = x.astype(BF16)
    r = x - hi.astype(F32)
    mid = r.astype(BF16)
    lo = (r - mid.astype(F32)).astype(BF16)
    return hi, mid, lo


def _dot(a, b):
    return jnp.dot(a, b, preferred_element_type=F32)


def _dot_nt(a, b):
    return lax.dot_general(a, b, (((1,), (1,)), ((), ())), preferred_element_type=F32)


def _dot_tn(a, b):
    return lax.dot_general(a, b, (((0,), (0,)), ((), ())), preferred_element_type=F32)


def _proj_kernel(x_ref, g_ref, w_ref, o_ref, h_scr):
    @pl.when(pl.program_id(1) == 0)
    def _():
        x = x_ref[...]
        ms = jnp.mean(x * x, axis=-1, keepdims=True)
        h_scr[...] = (x * lax.rsqrt(ms + EPS) * g_ref[...]).astype(BF16)

    o_ref[...] = _dot(h_scr[...], w_ref[...])


def _proj(x2d, g, wb, tm, tn):
    n, d = x2d.shape
    nc = wb.shape[1]
    return pl.pallas_call(
        _proj_kernel,
        grid=(n // tm, nc // tn),
        in_specs=[pl.BlockSpec((tm, d), lambda i, j: (i, 0)),
                  pl.BlockSpec((1, d), lambda i, j: (0, 0)),
                  pl.BlockSpec((d, tn), lambda i, j: (0, j))],
        out_specs=pl.BlockSpec((tm, tn), lambda i, j: (i, j)),
        out_shape=jax.ShapeDtypeStruct((n, nc), F32),
        scratch_shapes=[pltpu.VMEM((tm, d), BF16)],
        compiler_params=_cparams(("arbitrary", "arbitrary")),
        name="proj",
    )(x2d, g, wb)


def _reorder_w_in(w_in):
    big = w_in[:, :4 * D_ML]
    small_a = w_in[:, 4 * D_ML:4 * D_ML + 2 * ML_HEADS]
    rest = w_in[:, 4 * D_ML + 2 * ML_HEADS:]
    nsa = rest[:, :D_NSA + 6 * D_KV]
    gate = rest[:, D_NSA + 6 * D_KV:]
    pad = jnp.zeros((w_in.shape[0], N_PROJ - COL_SMALL - 2 * ML_HEADS - 3 * NSA_HEADS), w_in.dtype)
    return jnp.concatenate([big, nsa, small_a, gate, pad], axis=1).astype(BF16)


def _log_sigmoid(x):
    return jnp.minimum(x, 0.0) - jnp.log1p(jnp.exp(-jnp.abs(x)))


def _mlstm_kernel(q_ref, k_ref, v_ref, o_ref, s_ref, cb_ref, c0_ref, n0_ref, m0_ref,
                  cw_ref, gb_ref, ng_ref,
                  y_ref, cbo_ref, co_ref, no_ref, mo_ref,
                  ext_scr, c_scr, n_scr, m_scr, *, T):
    c = pl.program_id(1)
    nc = pl.num_programs(1)

    @pl.when(c == 0)
    def _():
        ext_scr[0:8, :] = jnp.zeros((8, 2 * D_ML), F32)
        ext_scr[5:8, :] = cb_ref[0]
        c_scr[...] = c0_ref[0]
        n_scr[...] = n0_ref[0]
        m_scr[...] = m0_ref[0]

    ext_scr[8:8 + T, 0:D_ML] = q_ref[...]
    ext_scr[8:8 + T, D_ML:2 * D_ML] = k_ref[...]
    conv = ext_scr[5:5 + T, :] * cw_ref[0:1, :]
    for j in range(1, CONV_W):
        conv = conv + ext_scr[5 + j:5 + j + T, :] * cw_ref[j:j + 1, :]
    tail = ext_scr[8 + T - 3:8 + T, :]
    ext_scr[5:8, :] = tail
    cbo_ref[0] = tail
    qk = conv * jax.nn.sigmoid(conv)

    pre = s_ref[...] + gb_ref[...]
    col = lax.broadcasted_iota(jnp.int32, pre.shape, 1)
    padrow = s_ref[:, LANE - 1:LANE] > 0.5
    gates = jnp.where(col < ML_HEADS, pre, _log_sigmoid(pre))
    gates = jnp.where(padrow, jnp.where(col < ML_HEADS, NEG_BIG, 0.0), gates)
    g_r = gates.T
    ti = lax.broadcasted_iota(jnp.int32, (T, T), 0)
    si = lax.broadcasted_iota(jnp.int32, (T, T), 1)
    upper = (ti <= si).astype(BF16)
    g_fin = jnp.where(lax.broadcasted_iota(jnp.int32, g_r.shape, 0) < ML_HEADS, 0.0, g_r)
    hi, mid, lo = _split3(g_fin)
    cum_r = _dot(hi, upper) + _dot(mid, upper) + _dot(lo, upper)
    rowi = lax.broadcasted_iota(jnp.int32, g_r.shape, 0)
    a_r = jnp.where(rowi < ML_HEADS, g_r, cum_r)
    a_c = a_r.T
    causal = si <= ti

    for h in range(ML_HEADS):
        sl = slice(h * ML_HEAD_DIM, (h + 1) * ML_HEAD_DIM)
        q = qk[:, h * ML_HEAD_DIM:(h + 1) * ML_HEAD_DIM]
        k = qk[:, D_ML + h * ML_HEAD_DIM:D_ML + (h + 1) * ML_HEAD_DIM] * (ML_HEAD_DIM ** -0.5)
        v = v_ref[:, sl]
        ig_r = a_r[h:h + 1, :]
        b_r = a_r[ML_HEADS + h:ML_HEADS + h + 1, :]
        ig_c = a_c[:, h:h + 1]
        b_c = a_c[:, ML_HEADS + h:ML_HEADS + h + 1]
        m_prev = m_scr[h:h + 1, 0:1]
        logd = jnp.where(causal, b_c - b_r + ig_r, -jnp.inf)
        inter = b_c + m_prev
        m_t = jnp.maximum(inter, jnp.max(logd, axis=1, keepdims=True))
        w_intra = jnp.exp(logd - m_t)
        w_inter = jnp.exp(inter - m_t)
        qb = q.astype(BF16)
        kb = k.astype(BF16)
        vb = v.astype(BF16)
        sc = _dot_nt(qb, kb) * w_intra
        cmat = c_scr[h]
        nvec = n_scr[h:h + 1, :]
        num = _dot(sc.astype(BF16), vb) + w_inter * _dot(qb, cmat.astype(BF16))
        qn = jnp.sum(qb.astype(F32) * nvec.astype(BF16).astype(F32), axis=1, keepdims=True)
        den = jnp.sum(sc, axis=1, keepdims=True) + w_inter * qn
        hh = num / jnp.maximum(jnp.abs(den), jnp.exp(-m_t))
        m_new = m_t[T - 1:T, :]
        b_last = b_c[T - 1:T, :]
        w_s = jnp.exp(b_last - b_c + ig_c - m_new)
        decay = jnp.exp(b_last + m_prev - m_new)
        kw = k * w_s
        c_new = decay * cmat + _dot_tn(kw.astype(BF16), vb)
        n_new = decay * nvec + jnp.sum(kw, axis=0, keepdims=True)
        c_scr[h] = c_new
        n_scr[h:h + 1, :] = n_new
        m_scr[h:h + 1, :] = jnp.broadcast_to(m_new, (1, LANE))
        hn = hh * lax.rsqrt(jnp.mean(hh * hh, axis=1, keepdims=True) + EPS) * ng_ref[:, sl]
        y_ref[:, sl] = (hn * jax.nn.sigmoid(o_ref[:, sl])).astype(y_ref.dtype)

    @pl.when(c == nc - 1)
    def _():
        co_ref[0] = c_scr[...]
        no_ref[0] = n_scr[...]
        mo_ref[0] = m_scr[...]


def _mlstm(u, small, conv_buf, c0, n0, m0, conv_w, gate_bias, norm_g, batch, seq, T):
    nc = seq // T
    cb = D_ML // 1024
    m0b = jnp.broadcast_to(m0[:, :, None], (batch, ML_HEADS, LANE))
    m0b = jnp.concatenate([m0b, jnp.zeros((batch, 8 - ML_HEADS, LANE), F32)], axis=1)
    n0p = jnp.concatenate([n0, jnp.zeros((batch, 8 - ML_HEADS, ML_HEAD_DIM), F32)], axis=1)
    row = lambda b, c: (b * nc + c, 0)
    outs = pl.pallas_call(
        functools.partial(_mlstm_kernel, T=T),
        grid=(batch, nc),
        in_specs=[pl.BlockSpec((T, D_ML), lambda b, c: (b * nc + c, COL_QML // D_ML)),
                  pl.BlockSpec((T, D_ML), lambda b, c: (b * nc + c, COL_KML // D_ML)),
                  pl.BlockSpec((T, D_ML), lambda b, c: (b * nc + c, COL_VML // D_ML)),
                  pl.BlockSpec((T, D_ML), lambda b, c: (b * nc + c, COL_OML // D_ML)),
                  pl.BlockSpec((T, LANE), row),
                  pl.BlockSpec((1, CONV_W - 1, 2 * D_ML), lambda b, c: (b, 0, 0)),
                  pl.BlockSpec((1, ML_HEADS, ML_HEAD_DIM, ML_HEAD_DIM), lambda b, c: (b, 0, 0, 0)),
                  pl.BlockSpec((1, 8, ML_HEAD_DIM), lambda b, c: (b, 0, 0)),
                  pl.BlockSpec((1, 8, LANE), lambda b, c: (b, 0, 0)),
                  pl.BlockSpec((CONV_W, 2 * D_ML), lambda b, c: (0, 0)),
                  pl.BlockSpec((1, LANE), lambda b, c: (0, 0)),
                  pl.BlockSpec((1, D_ML), lambda b, c: (0, 0))],
        out_specs=[pl.BlockSpec((T, D_ML), row),
                   pl.BlockSpec((1, CONV_W - 1, 2 * D_ML), lambda b, c: (b, 0, 0)),
                   pl.BlockSpec((1, ML_HEADS, ML_HEAD_DIM, ML_HEAD_DIM), lambda b, c: (b, 0, 0, 0)),
                   pl.BlockSpec((1, 8, ML_HEAD_DIM), lambda b, c: (b, 0, 0)),
                   pl.BlockSpec((1, 8, LANE), lambda b, c: (b, 0, 0))],
        out_shape=[jax.ShapeDtypeStruct((batch * seq, D_ML), BF16),
                   jax.ShapeDtypeStruct((batch, CONV_W - 1, 2 * D_ML), F32),
                   jax.ShapeDtypeStruct((batch, ML_HEADS, ML_HEAD_DIM, ML_HEAD_DIM), F32),
                   jax.ShapeDtypeStruct((batch, 8, ML_HEAD_DIM), F32),
                   jax.ShapeDtypeStruct((batch, 8, LANE), F32)],
        scratch_shapes=[pltpu.VMEM((8 + T, 2 * D_ML), F32),
                        pltpu.VMEM((ML_HEADS, ML_HEAD_DIM, ML_HEAD_DIM), F32),
                        pltpu.VMEM((8, ML_HEAD_DIM), F32),
                        pltpu.VMEM((8, LANE), F32)],
        compiler_params=_cparams(("arbitrary", "arbitrary")),
        name="mlstm",
    )(u, u, u, u, small, conv_buf, c0, n0p, m0b, conv_w, gate_bias, norm_g)
    y, cbo, co, no, mo = outs
    return y, cbo, co, no[:, :ML_HEADS], mo[:, :ML_HEADS, 0]


def _bucket_np(dist):
    n = np.maximum(dist, 0)
    max_exact = N_BUCKETS // 2
    nf = np.maximum(n, 1).astype(np.float64)
    large = max_exact + (np.log(nf / max_exact) / math.log(MAX_DISTANCE / max_exact)
                         * (N_BUCKETS - max_exact)).astype(np.int64)
    return np.where(n < max_exact, n, np.minimum(large, N_BUCKETS - 1)).astype(np.int32)


def _bias_by_distance(rel_bias, n):
    return rel_bias.astype(F32)[_bucket_np(np.arange(n))].T


def _band_bias(rel_bias, tile, n_delta):
    lc = n_delta * tile
    a = _bias_by_distance(rel_bias, lc + 1)
    y = jnp.tile(a, (1, tile))[:, :tile * lc].reshape(NSA_HEADS, tile, lc)
    y = y.reshape(NSA_KV_HEADS, NSA_GROUP, tile, n_delta, tile).transpose(0, 3, 2, 1, 4)
    return y.reshape(NSA_KV_HEADS, n_delta, tile, NSA_GROUP * tile)


def _cmp_bias(rel_bias, seq, nch):
    a = _bias_by_distance(rel_bias, seq)
    a2 = jnp.concatenate([jnp.zeros((NSA_HEADS, CMP_BLOCK - 1), F32), a], axis=1)[:, :seq + CMP_STRIDE]
    y = jnp.tile(a2, (1, nch))[:, :nch * seq].reshape(NSA_HEADS, nch, seq)
    return y.transpose(0, 2, 1)


def _overlap_t(n_cmp, nch, n_slc):
    c0 = np.arange(nch) * CMP_STRIDE
    s0 = np.arange(n_slc) * SLC_BLOCK
    ov = np.minimum(c0[None, :] + CMP_BLOCK, s0[:, None] + SLC_BLOCK) - np.maximum(c0[None, :], s0[:, None])
    ov = np.clip(ov, 0, None).astype(np.float32) / CMP_BLOCK
    ov[:, n_cmp:] = 0.0
    return jnp.asarray(ov, BF16)


def _pq_kernel(x0_ref, x1_ref, x2_ref, x3_ref, pos_ref, w_ref, o_ref, *, rows):
    for pair, x_ref in enumerate((x0_ref, x1_ref, x2_ref, x3_ref)):
        toks = [x_ref[pl.ds(l, rows, stride=CMP_STRIDE), :] for l in range(CMP_STRIDE)]
        for half in range(2):
            ch = 2 * pair + half
            c = ch // NSA_KV_HEADS
            lanes = slice(half * NSA_HEAD_DIM, (half + 1) * NSA_HEAD_DIM)
            chunk = jnp.concatenate([t[:, lanes] for t in toks], axis=1)
            for part in range(2):
                lhs = (chunk + pos_ref[c, part:part + 1, :]).astype(BF16)
                o_ref[0, ch, :, part * CMP_HIDDEN:(part + 1) * CMP_HIDDEN] = _dot(lhs, w_ref[c, part])


def _hid_kernel(pq_ref, w2_ref, o_ref, *, nch):
    p = pq_ref[0, 0, :, 0:CMP_HIDDEN]
    q = pltpu.roll(pq_ref[0, 0, :, CMP_HIDDEN:2 * CMP_HIDDEN], nch - 1, 0)
    hid = jax.nn.gelu(p + q, approximate=True)
    o_ref[0, 0] = _dot(hid.astype(BF16), w2_ref[0])


def _compress(kv2d, col_block, bsz, seq, cmp_pos, cmp_w1, cmp_w2, tt):
    nch = seq // CMP_STRIDE
    kdim = CMP_STRIDE * NSA_HEAD_DIM
    nt = seq // tt
    tn = tt // CMP_STRIDE
    w1 = cmp_w1.reshape(2, 2, kdim, CMP_HIDDEN).astype(BF16)
    pos = cmp_pos.reshape(2, 2, kdim)
    pq = pl.pallas_call(
        functools.partial(_pq_kernel, rows=tn),
        grid=(bsz, nt),
        in_specs=[pl.BlockSpec((tt, LANE), functools.partial(lambda b, i, k: (b * nt + i, col_block * 4 + k), k=k))
                  for k in range(4)] +
                 [pl.BlockSpec((2, 2, kdim), lambda b, i: (0, 0, 0)),
                  pl.BlockSpec((2, 2, kdim, CMP_HIDDEN), lambda b, i: (0, 0, 0, 0))],
        out_specs=pl.BlockSpec((1, 8, tn, 2 * CMP_HIDDEN), lambda b, i: (b, 0, i, 0)),
        out_shape=jax.ShapeDtypeStruct((bsz, 8, nch, 2 * CMP_HIDDEN), F32),
        compiler_params=_cparams(("arbitrary", "arbitrary")),
        name="cmp_pq",
    )(kv2d, kv2d, kv2d, kv2d, pos, w1)
    return pl.pallas_call(
        functools.partial(_hid_kernel, nch=nch),
        grid=(bsz, 8),
        in_specs=[pl.BlockSpec((1, 1, nch, 2 * CMP_HIDDEN), lambda b, c: (b, c, 0, 0)),
                  pl.BlockSpec((1, CMP_HIDDEN, NSA_HEAD_DIM), lambda b, c: (c // NSA_KV_HEADS, 0, 0))],
        out_specs=pl.BlockSpec((1, 1, nch, NSA_HEAD_DIM), lambda b, c: (b, c, 0, 0)),
        out_shape=jax.ShapeDtypeStruct((bsz, 8, nch, NSA_HEAD_DIM), F32),
        compiler_params=_cparams(("arbitrary", "arbitrary")),
        name="cmp_hid",
    )(pq, cmp_w2.astype(BF16))


def _top_n_mask(score_t, blk, n_rows):
    rank = jnp.zeros(score_t.shape, jnp.int32)
    for i in range(n_rows):
        row = score_t[i:i + 1, :]
        beats = (row > score_t) | ((row == score_t) & (blk > i))
        rank = rank + beats.astype(jnp.int32)
    return (rank < min(SLC_TOPN, n_rows)).astype(F32)


def _cattn_kernel(q_ref, kc_ref, vc_ref, b_ref, ovt_ref, o_ref, sel_ref, *, tq, nch, n_cmp, n_slc):
    i = pl.program_id(2)
    kc = kc_ref[0, 0].astype(BF16)
    vc = vc_ref[0, 0].astype(BF16)
    t = i * tq + lax.broadcasted_iota(jnp.int32, (tq, nch), 0)
    n = lax.broadcasted_iota(jnp.int32, (tq, nch), 1)
    valid = (t - CMP_STRIDE * n - (CMP_BLOCK - 1) >= 0) & (n < n_cmp)
    pc = jnp.zeros((tq, nch), F32)
    for g in range(NSA_GROUP):
        sl = slice(g * NSA_HEAD_DIM, (g + 1) * NSA_HEAD_DIM)
        qg = (q_ref[:, sl] * (NSA_HEAD_DIM ** -0.5)).astype(BF16)
        s = _dot_nt(qg, kc) + b_ref[g]
        s = jnp.where(valid, s, NEG_BIG)
        e = jnp.exp(s - jnp.max(s, axis=1, keepdims=True))
        p = jnp.where(valid, e / jnp.sum(e, axis=1, keepdims=True), 0.0)
        o_ref[:, sl] = _dot(p.astype(BF16), vc)
        pc = pc + p
    score_t = _dot_nt(ovt_ref[...], pc.astype(BF16))
    blk = lax.broadcasted_iota(jnp.int32, (n_slc, tq), 0)
    cur = (i * tq + lax.broadcasted_iota(jnp.int32, (n_slc, tq), 1)) // SLC_BLOCK
    forced = (blk == 0) | (blk == cur) | (blk == cur - 1)
    score_t = jnp.where(forced, FORCE_SCORE, score_t)
    score_t = jnp.where(blk > cur, -FORCE_SCORE, score_t)
    sel_ref[0, 0] = _top_n_mask(score_t, blk, n_slc)


def _cattn(u, kvc, biasc, batch, seq, tq):
    nch = kvc.shape[2]
    n_cmp = nch - 1
    n_slc = seq // SLC_BLOCK
    ni = seq // tq
    ovt = _overlap_t(n_cmp, nch, n_slc)
    qcol = COL_QNSA // (NSA_GROUP * NSA_HEAD_DIM)
    return pl.pallas_call(
        functools.partial(_cattn_kernel, tq=tq, nch=nch, n_cmp=n_cmp, n_slc=n_slc),
        grid=(batch, NSA_KV_HEADS, ni),
        in_specs=[pl.BlockSpec((tq, NSA_GROUP * NSA_HEAD_DIM), lambda b, h, i: (b * ni + i, qcol + h)),
                  pl.BlockSpec((1, 1, nch, NSA_HEAD_DIM), lambda b, h, i: (b, h, 0, 0)),
                  pl.BlockSpec((1, 1, nch, NSA_HEAD_DIM), lambda b, h, i: (b, NSA_KV_HEADS + h, 0, 0)),
                  pl.BlockSpec((NSA_GROUP, tq, nch), lambda b, h, i: (h, i, 0)),
                  pl.BlockSpec((n_slc, nch), lambda b, h, i: (0, 0))],
        out_specs=[pl.BlockSpec((tq, NSA_GROUP * NSA_HEAD_DIM), lambda b, h, i: (b * ni + i, h)),
                   pl.BlockSpec((1, 1, n_slc, tq), lambda b, h, i: (b, h, 0, i))],
        out_shape=[jax.ShapeDtypeStruct((batch * seq, D_NSA), F32),
                   jax.ShapeDtypeStruct((batch, NSA_KV_HEADS, n_slc, seq), F32)],
        compiler_params=_cparams(("arbitrary", "arbitrary", "arbitrary")),
        name="cmp_attn",
    )(u, kvc, kvc, biasc, ovt)


TQ = 128


def _flash_kernel(q_ref, k_ref, vt_ref, b_ref, *rest, selected):
    if selected:
        sel_ref, o_ref, qs_scr, m_scr, l_scr, acc_scr = rest
    else:
        o_ref, qs_scr, m_scr, l_scr, acc_scr = rest
    i = pl.program_id(2)
    for g in range(NSA_GROUP):
        qs_scr[g * TQ:(g + 1) * TQ, :] = (
            q_ref[:, g * NSA_HEAD_DIM:(g + 1) * NSA_HEAD_DIM] * (NSA_HEAD_DIM ** -0.5)).astype(BF16)
    m_scr[...] = jnp.full(m_scr.shape, NEG_BIG, F32)
    l_scr[...] = jnp.zeros(l_scr.shape, F32)
    acc_scr[...] = jnp.zeros(acc_scr.shape, F32)
    wide = NSA_GROUP * TQ
    row = lax.broadcasted_iota(jnp.int32, (TQ, wide), 0)
    lane = lax.broadcasted_iota(jnp.int32, (TQ, wide), 1)
    base = (lane & (TQ - 1)) - row

    def body(j, carry):
        kj = k_ref[0, 0, pl.ds(pl.multiple_of(j * TQ, TQ), TQ), :].astype(BF16)
        s = _dot_nt(kj, qs_scr[...]) + b_ref[0, i - j]
        dist = base + (i - j) * TQ
        if selected:
            r = sel_ref[0, 0, pl.ds(2 * j, 2), :]
            r = jnp.concatenate([r] * NSA_GROUP, axis=1)
            picked = jnp.where(row < SLC_BLOCK, r[0:1, :], r[1:2, :]) > 0.5
            valid = (dist >= 0) & picked
        else:
            valid = (dist >= 0) & (dist < WINDOW)
        s = jnp.where(valid, s, NEG_BIG)
        m_old = m_scr[...]
        m_new = jnp.maximum(m_old, jnp.max(s, axis=0, keepdims=True))
        alpha = jnp.exp(m_old - m_new)
        p = jnp.where(valid, jnp.exp(s - m_new), 0.0)
        l_scr[...] = alpha * l_scr[...] + jnp.sum(p, axis=0, keepdims=True)
        acc_scr[...] = alpha * acc_scr[...] + _dot(vt_ref[0, 0, j].astype(BF16), p.astype(BF16))
        m_scr[...] = m_new
        return carry

    lo = 0 if selected else jnp.maximum(i - WINDOW // TQ, 0)
    lax.fori_loop(lo, i + 1, body, 0)
    o_ref[0, 0, 0] = acc_scr[...] / l_scr[...]


def _flash(u, k, vt, band, sel_t, batch, seq):
    ni = seq // TQ
    qcol = COL_QNSA // (NSA_GROUP * NSA_HEAD_DIM)
    selected = sel_t is not None
    n_delta = band.shape[1]
    in_specs = [pl.BlockSpec((TQ, NSA_GROUP * NSA_HEAD_DIM), lambda b, h, i: (b * ni + i, qcol + h)),
                pl.BlockSpec((1, 1, seq, NSA_HEAD_DIM), lambda b, h, i: (b, h, 0, 0)),
                pl.BlockSpec((1, 1, ni, NSA_HEAD_DIM, TQ), lambda b, h, i: (b, h, 0, 0, 0)),
                pl.BlockSpec((1, n_delta, TQ, NSA_GROUP * TQ), lambda b, h, i: (h, 0, 0, 0))]
    args = [u, k, vt, band]
    if selected:
        n_slc = sel_t.shape[2]
        in_specs.append(pl.BlockSpec((1, 1, n_slc, TQ), lambda b, h, i: (b, h, 0, i)))
        args.append(sel_t)
    o_t = pl.pallas_call(
        functools.partial(_flash_kernel, selected=selected),
        grid=(batch, NSA_KV_HEADS, ni),
        in_specs=in_specs,
        out_specs=pl.BlockSpec((1, 1, 1, NSA_HEAD_DIM, NSA_GROUP * TQ), lambda b, h, i: (b, h, i, 0, 0)),
        out_shape=jax.ShapeDtypeStruct((batch, NSA_KV_HEADS, ni, NSA_HEAD_DIM, NSA_GROUP * TQ), F32),
        scratch_shapes=[pltpu.VMEM((NSA_GROUP * TQ, NSA_HEAD_DIM), BF16),
                        pltpu.VMEM((1, NSA_GROUP * TQ), F32),
                        pltpu.VMEM((1, NSA_GROUP * TQ), F32),
                        pltpu.VMEM((NSA_HEAD_DIM, NSA_GROUP * TQ), F32)],
        compiler_params=_cparams(("arbitrary", "arbitrary", "arbitrary")),
        name="slc_attn" if selected else "win_attn",
    )(*args)
    o = o_t.reshape(batch, NSA_KV_HEADS, ni, NSA_HEAD_DIM, NSA_GROUP, TQ).transpose(0, 2, 5, 1, 4, 3)
    return o.reshape(batch * seq, D_NSA)


def _kv_layouts(kv2d, batch, seq):
    kv = kv2d.reshape(batch, seq, 2, NSA_KV_HEADS, NSA_HEAD_DIM)
    k = kv[:, :, 0].transpose(0, 2, 1, 3)
    vt = kv[:, :, 1].reshape(batch, seq // TQ, TQ, NSA_KV_HEADS, NSA_HEAD_DIM).transpose(0, 3, 1, 4, 2)
    return k, vt


def _gate_expand():
    e = np.zeros((3, LANE, D_NSA), np.float32)
    for br in range(3):
        for hd in range(NSA_HEADS):
            e[br, 2 * ML_HEADS + br * NSA_HEADS + hd, hd * NSA_HEAD_DIM:(hd + 1) * NSA_HEAD_DIM] = 1.0
    return jnp.asarray(e, BF16)


def _outproj_kernel(yml_ref, oc_ref, os_ref, ow_ref, s_ref, e_ref, ng_ref, w_ref, x_ref, o_ref, cat_scr):
    @pl.when(pl.program_id(1) == 0)
    def _():
        sig = jax.nn.sigmoid(s_ref[...])
        hi, lo = _split2(sig)
        o = jnp.zeros(oc_ref.shape, F32)
        for br, ref in enumerate((oc_ref, os_ref, ow_ref)):
            gexp = _dot(hi, e_ref[br]) + _dot(lo, e_ref[br])
            o = o + gexp * ref[...]
        o = o * lax.rsqrt(jnp.mean(o * o, axis=-1, keepdims=True) + EPS) * ng_ref[...]
        cat_scr[:, 0:D_ML] = yml_ref[...]
        cat_scr[:, D_ML:D_ML + D_NSA] = o.astype(BF16)

    o_ref[...] = x_ref[...] + _dot(cat_scr[...], w_ref[...])


def _outproj(y_ml, o_c, o_s, o_w, small, nsa_g, w_out_b, x2d, tm, tn):
    n, d = x2d.shape
    rows = lambda i, j: (i, 0)
    return pl.pallas_call(
        _outproj_kernel,
        grid=(n // tm, d // tn),
        in_specs=[pl.BlockSpec((tm, D_ML), rows), pl.BlockSpec((tm, D_NSA), rows),
                  pl.BlockSpec((tm, D_NSA), rows), pl.BlockSpec((tm, D_NSA), rows),
                  pl.BlockSpec((tm, LANE), rows),
                  pl.BlockSpec((3, LANE, D_NSA), lambda i, j: (0, 0, 0)),
                  pl.BlockSpec((1, D_NSA), lambda i, j: (0, 0)),
                  pl.BlockSpec((D_ML + D_NSA, tn), lambda i, j: (0, j)),
                  pl.BlockSpec((tm, tn), lambda i, j: (i, j))],
        out_specs=pl.BlockSpec((tm, tn), lambda i, j: (i, j)),
        out_shape=jax.ShapeDtypeStruct((n, d), F32),
        scratch_shapes=[pltpu.VMEM((tm, D_ML + D_NSA), BF16)],
        compiler_params=_cparams(("arbitrary", "arbitrary")),
        name="outproj",
    )(y_ml, o_c, o_s, o_w, small, _gate_expand(), nsa_g, w_out_b, x2d)


ROUTE_COL = N_GROUPS
BIG_COL = 1 << 20
SPLIT = D_MODEL // LANE


def _store_split(ref, val, n):
    for k in range(SPLIT):
        ref[pl.ds(k, n, stride=SPLIT), :] = val[:, k * LANE:(k + 1) * LANE]


def _load_split(ref, n):
    return jnp.concatenate([ref[pl.ds(k, n, stride=SPLIT), :] for k in range(SPLIT)], axis=1)


def _route_kernel(y_ref, g_ref, w_ref, b_ref, h_ref, info_ref, cnt_ref, carry_scr, *, tm):
    @pl.when(pl.program_id(0) == 0)
    def _():
        carry_scr[...] = jnp.zeros(carry_scr.shape, F32)

    x = y_ref[...]
    h = x * lax.rsqrt(jnp.mean(x * x, axis=-1, keepdims=True) + EPS) * g_ref[...]
    _store_split(h_ref, h, tm)
    logit = _dot(h.astype(BF16), w_ref[...]) + b_ref[...]
    col = lax.broadcasted_iota(jnp.int32, logit.shape, 1)
    is_grp = col < N_GROUPS
    gmax = jnp.max(jnp.where(is_grp, logit, -jnp.inf), axis=1, keepdims=True)
    gtop = jnp.min(jnp.where(is_grp & (logit == gmax), col, BIG_COL), axis=1, keepdims=True)
    gsum = jnp.sum(jnp.where(is_grp, jnp.exp(logit - gmax), 0.0), axis=1, keepdims=True)
    first = ROUTE_COL + gtop * EXPERTS_PER_GROUP
    in_grp = (col >= first) & (col < first + EXPERTS_PER_GROUP)
    v1 = jnp.max(jnp.where(in_grp, logit, -jnp.inf), axis=1, keepdims=True)
    i1 = jnp.min(jnp.where(in_grp & (logit == v1), col, BIG_COL), axis=1, keepdims=True)
    rest = in_grp & (col != i1)
    v2 = jnp.max(jnp.where(rest, logit, -jnp.inf), axis=1, keepdims=True)
    i2 = jnp.min(jnp.where(rest & (logit == v2), col, BIG_COL), axis=1, keepdims=True)
    e = jnp.exp(v2 - v1)
    w1 = 1.0 / ((1.0 + e) * gsum)
    w2 = e / ((1.0 + e) * gsum)
    pick1 = col == i1
    pick2 = col == i2
    both = (pick1 | pick2).astype(F32)
    ri = lax.broadcasted_iota(jnp.int32, (tm, tm), 0)
    ci = lax.broadcasted_iota(jnp.int32, (tm, tm), 1)
    before = (ci < ri).astype(BF16)
    cum = _dot(before, both.astype(BF16)) + carry_scr[...]
    r1 = jnp.sum(jnp.where(pick1, cum, 0.0), axis=1, keepdims=True)
    r2 = jnp.sum(jnp.where(pick2, cum, 0.0), axis=1, keepdims=True)
    carry_scr[...] = carry_scr[...] + jnp.sum(both, axis=0, keepdims=True)
    cnt_ref[...] = jnp.broadcast_to(carry_scr[...], cnt_ref.shape)
    info = jnp.where(col == 0, (i1 - ROUTE_COL).astype(F32), 0.0)
    info = jnp.where(col == 1, (i2 - ROUTE_COL).astype(F32), info)
    info = jnp.where(col == 2, w1, info)
    info = jnp.where(col == 3, w2, info)
    info = jnp.where(col == 4, r1, info)
    info_ref[...] = jnp.where(col == 5, r2, info)


def _route(y2d, g, w_rg, b_rg, w_re, b_re, tm):
    n, d = y2d.shape
    wr = jnp.concatenate([w_rg, w_re, jnp.zeros((d, LANE - N_GROUPS - N_EXPERTS), F32)], axis=1).astype(BF16)
    bias = jnp.concatenate([b_rg, b_re, jnp.zeros((LANE - N_GROUPS - N_EXPERTS,), F32)])[None, :]
    rows = lambda i: (i, 0)
    fixed = lambda i: (0, 0)
    return pl.pallas_call(
        functools.partial(_route_kernel, tm=tm),
        grid=(n // tm,),
        in_specs=[pl.BlockSpec((tm, d), rows), pl.BlockSpec((1, d), fixed),
                  pl.BlockSpec((d, LANE), fixed), pl.BlockSpec((1, LANE), fixed)],
        out_specs=[pl.BlockSpec((tm * SPLIT, LANE), rows), pl.BlockSpec((tm, LANE), rows),
                   pl.BlockSpec((8, LANE), fixed)],
        out_shape=[jax.ShapeDtypeStruct((n * SPLIT, LANE), F32), jax.ShapeDtypeStruct((n, LANE), F32),
                   jax.ShapeDtypeStruct((8, LANE), F32)],
        scratch_shapes=[pltpu.VMEM((1, LANE), F32)],
        compiler_params=_cparams(("arbitrary",)),
        name="moe_route",
    )(y2d, g, wr, bias)


GATHER_WINDOW = 32


def _row_gather_kernel(src_ref, x_hbm, o_hbm, sem, *, n_rows, span):
    def copy(r, src_row):
        return pltpu.make_async_copy(x_hbm.at[pl.ds(pl.multiple_of(src_row * span, span), span)],
                                     o_hbm.at[pl.ds(pl.multiple_of(r * span, span), span)], sem)

    def step(r, carry):
        copy(r, src_ref[r]).start()

        @pl.when(r >= GATHER_WINDOW)
        def _():
            copy(r - GATHER_WINDOW, 0).wait()
        return carry

    lax.fori_loop(0, n_rows, step, 0)

    def drain(r, carry):
        copy(r, 0).wait()
        return carry

    lax.fori_loop(n_rows - GATHER_WINDOW, n_rows, drain, 0)


def _row_gather(x, src, span):
    n_rows = src.shape[0]
    return pl.pallas_call(
        functools.partial(_row_gather_kernel, n_rows=n_rows, span=span),
        grid_spec=pltpu.PrefetchScalarGridSpec(
            num_scalar_prefetch=1, grid=(1,),
            in_specs=[pl.BlockSpec(memory_space=pl.ANY)],
            out_specs=pl.BlockSpec(memory_space=pl.ANY),
            scratch_shapes=[pltpu.SemaphoreType.DMA(())]),
        out_shape=jax.ShapeDtypeStruct((n_rows * span, x.shape[1]), x.dtype),
        compiler_params=pltpu.CompilerParams(dimension_semantics=("arbitrary",), has_side_effects=True),
        name="row_gather",
    )(src, x)


def _expert_kernel(te_ref, nu_ref, xs_ref, rw_ref, wg_ref, wu_ref, wd_ref, o_ref, wg_b, wu_b, wd_b, *, tme):
    i = pl.program_id(0)

    @pl.when(i < nu_ref[0])
    def _():
        prev = te_ref[jnp.maximum(i - 1, 0)]

        @pl.when((i == 0) | (te_ref[i] != prev))
        def _():
            wg_b[...] = wg_ref[0].astype(BF16)
            wu_b[...] = wu_ref[0].astype(BF16)
            wd_b[...] = wd_ref[0].astype(BF16)

        x = _load_split(xs_ref, tme).astype(BF16)
        a = _dot(x, wg_b[...])
        u = _dot(x, wu_b[...])
        hid = a * jax.nn.sigmoid(a) * u * rw_ref[...]
        _store_split(o_ref, _dot(hid.astype(BF16), wd_b[...]), tme)

    @pl.when(i >= nu_ref[0])
    def _():
        o_ref[...] = jnp.zeros(o_ref.shape, F32)


def _experts(xs, roww, tile_expert, n_used, w_gate, w_up, w_down, tme):
    p = xs.shape[0] // SPLIT
    d = D_MODEL
    return pl.pallas_call(
        functools.partial(_expert_kernel, tme=tme),
        grid_spec=pltpu.PrefetchScalarGridSpec(
            num_scalar_prefetch=2, grid=(p // tme,),
            in_specs=[pl.BlockSpec((tme * SPLIT, LANE), lambda i, te, nu: (i, 0)),
                      pl.BlockSpec((tme, 1), lambda i, te, nu: (i, 0)),
                      pl.BlockSpec((1, d, D_EXPERT), lambda i, te, nu: (te[i], 0, 0)),
                      pl.BlockSpec((1, d, D_EXPERT), lambda i, te, nu: (te[i], 0, 0)),
                      pl.BlockSpec((1, D_EXPERT, d), lambda i, te, nu: (te[i], 0, 0))],
            out_specs=pl.BlockSpec((tme * SPLIT, LANE), lambda i, te, nu: (i, 0)),
            scratch_shapes=[pltpu.VMEM((d, D_EXPERT), BF16), pltpu.VMEM((d, D_EXPERT), BF16),
                            pltpu.VMEM((D_EXPERT, d), BF16)]),
        out_shape=jax.ShapeDtypeStruct((p * SPLIT, LANE), F32),
        compiler_params=_cparams(("arbitrary",)),
        name="moe_experts",
    )(tile_expert, n_used, xs, roww, w_gate, w_up, w_down)


def _combine_kernel(d1_ref, d2_ref, y_ref, ys_hbm, g_ref, o_ref, a_buf, b_buf, sem, *, tm):
    base = pl.program_id(0) * tm

    def copy(r, row, buf):
        return pltpu.make_async_copy(ys_hbm.at[pl.ds(pl.multiple_of(row * SPLIT, SPLIT), SPLIT)],
                                     buf.at[pl.ds(pl.multiple_of(r * SPLIT, SPLIT), SPLIT)], sem)

    def issue(r, carry):
        copy(r, d1_ref[base + r], a_buf).start()
        copy(r, d2_ref[base + r], b_buf).start()
        return carry

    lax.fori_loop(0, tm, issue, 0)

    def drain(r, carry):
        copy(r, 0, a_buf).wait()
        copy(r, 0, b_buf).wait()
        return carry

    lax.fori_loop(0, tm, drain, 0)
    y = y_ref[...] + _load_split(a_buf, tm) + _load_split(b_buf, tm)
    o_ref[...] = y * lax.rsqrt(jnp.mean(y * y, axis=-1, keepdims=True) + EPS) * g_ref[...]


def _combine(y2d, ys, d1, d2, g, tm):
    n, d = y2d.shape
    return pl.pallas_call(
        functools.partial(_combine_kernel, tm=tm),
        grid_spec=pltpu.PrefetchScalarGridSpec(
            num_scalar_prefetch=2, grid=(n // tm,),
            in_specs=[pl.BlockSpec((tm, d), lambda i, a, b: (i, 0)),
                      pl.BlockSpec(memory_space=pl.ANY),
                      pl.BlockSpec((1, d), lambda i, a, b: (0, 0))],
            out_specs=pl.BlockSpec((tm, d), lambda i, a, b: (i, 0)),
            scratch_shapes=[pltpu.VMEM((tm * SPLIT, LANE), F32), pltpu.VMEM((tm * SPLIT, LANE), F32),
                            pltpu.SemaphoreType.DMA(())]),
        out_shape=jax.ShapeDtypeStruct((n, d), F32),
        compiler_params=_cparams(("arbitrary",)),
        name="moe_combine",
    )(d1, d2, y2d, ys, g)


def _moe_final(y2d, ffn_g, w_rg, b_rg, w_re, b_re, w_gate, w_up, w_down, final_g, tm, tme):
    n, d = y2d.shape
    h, info, cnt = _route(y2d, ffn_g, w_rg, b_rg, w_re, b_re, tm)
    e1 = info[:, 0].astype(jnp.int32)
    e2 = info[:, 1].astype(jnp.int32)
    counts = cnt[0, ROUTE_COL:ROUTE_COL + N_EXPERTS].astype(jnp.int32)
    padded = (counts + tme - 1) // tme * tme
    ends = jnp.cumsum(padded)
    offs = ends - padded
    d1 = offs[e1] + info[:, 4].astype(jnp.int32)
    d2 = offs[e2] + info[:, 5].astype(jnp.int32)
    p = 2 * n + N_EXPERTS * tme
    tok = jnp.arange(n, dtype=jnp.int32)
    src = jnp.zeros((p,), jnp.int32).at[d1].set(tok).at[d2].set(tok)
    roww = jnp.zeros((p,), F32).at[d1].set(info[:, 2]).at[d2].set(info[:, 3])
    n_used = (ends[-1] // tme).astype(jnp.int32).reshape(1)
    tile_start = jnp.arange(p // tme, dtype=jnp.int32) * tme
    tile_expert = jnp.minimum(jnp.searchsorted(ends, tile_start, side="right"), N_EXPERTS - 1).astype(jnp.int32)
    last = tile_expert[jnp.maximum(n_used[0] - 1, 0)]
    tile_expert = jnp.where(tile_start < ends[-1], tile_expert, last)
    xs = _row_gather(h, src, SPLIT)
    ys = _experts(xs, roww[:, None], tile_expert, n_used, w_gate, w_up, w_down, tme)
    return _combine(y2d, ys, d1, d2, final_g, tm)


GROUP_ROWS = 8
SLC_LANES = 384
SEL_KEYS = 1024
WIN_KEYS = 640


def _sample_cattn_kernel(q_ref, kc_ref, vc_ref, b_ref, ov_ref, o_ref, sel_ref, *, n_cmp, n_slc):
    q = (q_ref[0, 0] * (NSA_HEAD_DIM ** -0.5)).astype(BF16)
    s = _dot_nt(q, kc_ref[0, 0].astype(BF16)) + b_ref[0]
    valid = lax.broadcasted_iota(jnp.int32, s.shape, 1) < n_cmp
    s = jnp.where(valid, s, NEG_BIG)
    e = jnp.exp(s - jnp.max(s, axis=1, keepdims=True))
    p = jnp.where(valid, e / jnp.sum(e, axis=1, keepdims=True), 0.0)
    o_ref[0, 0] = _dot(p.astype(BF16), vc_ref[0, 0].astype(BF16))
    pg = jnp.where(lax.broadcasted_iota(jnp.int32, p.shape, 0) < NSA_GROUP, p, 0.0)
    pc = jnp.broadcast_to(jnp.sum(pg, axis=0, keepdims=True), pg.shape)
    score = _dot(pc.astype(BF16), ov_ref[...])[0:1, :]
    blk = lax.broadcasted_iota(jnp.int32, score.shape, 1)
    cur = n_slc - 1
    forced = (blk == 0) | (blk == cur) | (blk == cur - 1)
    score = jnp.where(forced, FORCE_SCORE, score)
    score = jnp.where(blk > cur, -jnp.inf, score)
    ri = lax.broadcasted_iota(jnp.int32, (SLC_LANES, SLC_LANES), 0)
    ci = lax.broadcasted_iota(jnp.int32, (SLC_LANES, SLC_LANES), 1)
    sb = jnp.broadcast_to(score, (SLC_LANES, SLC_LANES))
    col = jnp.sum(jnp.where(ri == ci, sb, 0.0), axis=1, keepdims=True)
    beats = (col > sb) | ((col == sb) & (ri < ci))
    rank = jnp.sum(beats.astype(jnp.int32), axis=0, keepdims=True)
    sel_ref[0, 0] = jnp.broadcast_to((rank < SLC_TOPN).astype(F32), (GROUP_ROWS, SLC_LANES))


def _sample_cattn(q8, kvc, bias_cs, n_cmp, n_slc):
    bsz, _, nch, _ = kvc.shape
    c0 = np.arange(nch) * CMP_STRIDE
    s0 = np.arange(SLC_LANES) * SLC_BLOCK
    ov = np.minimum(c0[:, None] + CMP_BLOCK, s0[None, :] + SLC_BLOCK) - np.maximum(c0[:, None], s0[None, :])
    ov = np.clip(ov, 0, None).astype(np.float32) / CMP_BLOCK
    ov[n_cmp:, :] = 0.0
    ov[:, n_slc:] = 0.0
    return pl.pallas_call(
        functools.partial(_sample_cattn_kernel, n_cmp=n_cmp, n_slc=n_slc),
        grid=(bsz, NSA_KV_HEADS),
        in_specs=[pl.BlockSpec((1, 1, GROUP_ROWS, NSA_HEAD_DIM), lambda b, h: (b, h, 0, 0)),
                  pl.BlockSpec((1, 1, nch, NSA_HEAD_DIM), lambda b, h: (b, h, 0, 0)),
                  pl.BlockSpec((1, 1, nch, NSA_HEAD_DIM), lambda b, h: (b, NSA_KV_HEADS + h, 0, 0)),
                  pl.BlockSpec((1, GROUP_ROWS, nch), lambda b, h: (h, 0, 0)),
                  pl.BlockSpec((nch, SLC_LANES), lambda b, h: (0, 0))],
        out_specs=[pl.BlockSpec((1, 1, GROUP_ROWS, NSA_HEAD_DIM), lambda b, h: (b, h, 0, 0)),
                   pl.BlockSpec((1, 1, GROUP_ROWS, SLC_LANES), lambda b, h: (b, h, 0, 0))],
        out_shape=[jax.ShapeDtypeStruct((bsz, NSA_KV_HEADS, GROUP_ROWS, NSA_HEAD_DIM), F32),
                   jax.ShapeDtypeStruct((bsz, NSA_KV_HEADS, GROUP_ROWS, SLC_LANES), F32)],
        compiler_params=_cparams(("arbitrary", "arbitrary")),
        name="sample_cmp_attn",
    )(q8, kvc, kvc, bias_cs, jnp.asarray(ov, BF16))


def _sample_attn_kernel(rs_ref, q_ref, cache_hbm, snew_ref, bs_ref, win_ref, wnew_ref, bw_ref,
                        os_ref, ow_ref, kbuf, wbuf, sem, *, n_gather):
    b = pl.program_id(0)

    def block_copy(h, s):
        start = pl.multiple_of(rs_ref[(b * NSA_KV_HEADS + h) * n_gather + s], SLC_BLOCK)
        return pltpu.make_async_copy(cache_hbm.at[pl.ds(start, SLC_BLOCK)],
                                     kbuf.at[h, pl.ds(s * SLC_BLOCK, SLC_BLOCK)], sem)

    for h in range(NSA_KV_HEADS):
        for s in range(n_gather):
            block_copy(h, s).start()

    first_row = lax.broadcasted_iota(jnp.int32, (8, 2 * D_KV), 0) == 0
    tail = n_gather * SLC_BLOCK
    for h in range(NSA_KV_HEADS):
        kbuf[h, tail:tail + 8, :] = jnp.where(first_row, snew_ref[0], 0.0)
        kbuf[h, tail + 8:SEL_KEYS, :] = jnp.zeros((SEL_KEYS - tail - 8, 2 * D_KV), F32)
    wlen = win_ref.shape[1]
    wbuf[0:wlen, :] = win_ref[0]
    wbuf[wlen:wlen + 8, :] = jnp.where(first_row, wnew_ref[0], 0.0)
    wbuf[wlen + 8:WIN_KEYS, :] = jnp.zeros((WIN_KEYS - wlen - 8, 2 * D_KV), F32)

    def attend(q, k, v, bias, valid):
        s = _dot_nt(q, k.astype(BF16)) + bias
        s = jnp.where(valid, s, NEG_BIG)
        e = jnp.exp(s - jnp.max(s, axis=1, keepdims=True))
        p = jnp.where(valid, e / jnp.sum(e, axis=1, keepdims=True), 0.0)
        return _dot(p.astype(BF16), v.astype(BF16))

    widx = lax.broadcasted_iota(jnp.int32, (GROUP_ROWS, WIN_KEYS), 1)
    wvalid = (widx >= wlen + 1 - WINDOW) & (widx <= wlen)
    for h in range(NSA_KV_HEADS):
        kl = slice(h * NSA_HEAD_DIM, (h + 1) * NSA_HEAD_DIM)
        vl = slice(D_KV + h * NSA_HEAD_DIM, D_KV + (h + 1) * NSA_HEAD_DIM)
        q = (q_ref[0, h] * (NSA_HEAD_DIM ** -0.5)).astype(BF16)
        ow_ref[0, h] = attend(q, wbuf[:, kl], wbuf[:, vl], bw_ref[h], wvalid)

    for h in range(NSA_KV_HEADS):
        for s in range(n_gather):
            block_copy(h, s).wait()

    svalid = lax.broadcasted_iota(jnp.int32, (GROUP_ROWS, SEL_KEYS), 1) <= tail
    for h in range(NSA_KV_HEADS):
        kl = slice(h * NSA_HEAD_DIM, (h + 1) * NSA_HEAD_DIM)
        vl = slice(D_KV + h * NSA_HEAD_DIM, D_KV + (h + 1) * NSA_HEAD_DIM)
        q = (q_ref[0, h] * (NSA_HEAD_DIM ** -0.5)).astype(BF16)
        os_ref[0, h] = attend(q, kbuf[h, :, kl], kbuf[h, :, vl], bs_ref[0, h], svalid)


def _sample_attn(row_start, q8, cache2d, kvs_new, bias_sel, win, kvw_new, bias_w, n_gather):
    bsz = q8.shape[0]
    wlen = win.shape[1]
    qspec = pl.BlockSpec((1, NSA_KV_HEADS, GROUP_ROWS, NSA_HEAD_DIM), lambda b, rs: (b, 0, 0, 0))
    return pl.pallas_call(
        functools.partial(_sample_attn_kernel, n_gather=n_gather),
        grid_spec=pltpu.PrefetchScalarGridSpec(
            num_scalar_prefetch=1, grid=(bsz,),
            in_specs=[qspec,
                      pl.BlockSpec(memory_space=pl.ANY),
                      pl.BlockSpec((1, 1, 2 * D_KV), lambda b, rs: (b, 0, 0)),
                      pl.BlockSpec((1, NSA_KV_HEADS, GROUP_ROWS, SEL_KEYS), lambda b, rs: (b, 0, 0, 0)),
                      pl.BlockSpec((1, wlen, 2 * D_KV), lambda b, rs: (b, 0, 0)),
                      pl.BlockSpec((1, 1, 2 * D_KV), lambda b, rs: (b, 0, 0)),
                      pl.BlockSpec((NSA_KV_HEADS, GROUP_ROWS, WIN_KEYS), lambda b, rs: (0, 0, 0))],
            out_specs=[qspec, qspec],
            scratch_shapes=[pltpu.VMEM((NSA_KV_HEADS, SEL_KEYS, 2 * D_KV), F32),
                            pltpu.VMEM((WIN_KEYS, 2 * D_KV), F32),
                            pltpu.SemaphoreType.DMA(())]),
        out_shape=[jax.ShapeDtypeStruct(q8.shape, F32), jax.ShapeDtypeStruct(q8.shape, F32)],
        compiler_params=_cparams(("arbitrary",)),
        name="sample_slc_win_attn",
    )(row_start, q8, cache2d, kvs_new, bias_sel, win, kvw_new, bias_w)


def _pad_rows(a, n):
    return jnp.concatenate([a, jnp.zeros((n - a.shape[0],) + a.shape[1:], a.dtype)], axis=0)


def _sample_mixer(x_sample, cache_cmp, cache_slc, cache_win, st_c, st_n, st_m, st_conv, page_table,
                  norm_g, wb, gate_bias, conv_w, ml_norm_g, cmp_pos, cmp_w1, cmp_w2, nsa_norm_g, w_out_b, rel_bias):
    db = x_sample.shape[0]
    n_pages = page_table.shape[1]
    past = n_pages * PAGE_SIZE
    tok = 16
    x16 = _pad_rows(x_sample.reshape(db, D_MODEL), tok)
    u = _proj(x16, norm_g, wb, tok, 512)
    small = u[:, COL_SMALL:COL_SMALL + LANE]

    T = 128
    useq = jnp.zeros((db, T, 4 * D_ML), F32)
    useq = useq.at[:, T - CONV_W:T - 1, 0:2 * D_ML].set(st_conv)
    useq = useq.at[:, T - 1, :].set(u[:db, 0:4 * D_ML])
    sseq = jnp.zeros((db, T, LANE), F32).at[:, :T - 1, LANE - 1].set(1.0)
    sseq = sseq.at[:, T - 1, :].set(small[:db])
    y_seq, conv_n, c_n, n_n, m_n = _mlstm(
        useq.reshape(db * T, 4 * D_ML), sseq.reshape(db * T, LANE), jnp.zeros((db, CONV_W - 1, 2 * D_ML), F32),
        st_c, st_n, st_m, conv_w, gate_bias, ml_norm_g, db, T, T)
    y_ml = y_seq.reshape(db, T, D_ML)[:, T - 1]

    n_pool = cache_cmp.shape[0]
    ctx = _row_gather(cache_cmp.reshape(n_pool * PAGE_SIZE, 2 * D_KV), page_table.reshape(-1), span=PAGE_SIZE)
    kvc = _compress(ctx, 0, db, past, cmp_pos, cmp_w1, cmp_w2, 4096)
    nch = past // CMP_STRIDE
    n_cmp = (past + 1) // CMP_STRIDE - CMP_BLOCK // CMP_STRIDE + 1
    n_slc = -(-(past + 1) // SLC_BLOCK)
    q = u[:db, COL_QNSA:COL_QNSA + D_NSA].reshape(db, NSA_KV_HEADS, NSA_GROUP, NSA_HEAD_DIM)
    q8 = jnp.concatenate([q, jnp.zeros_like(q)], axis=2)
    bd = _bias_by_distance(rel_bias, past + 1).reshape(NSA_KV_HEADS, NSA_GROUP, past + 1)
    pad_g = lambda a: jnp.concatenate([a, jnp.zeros_like(a)], axis=1)
    dist_c = np.clip(past - (np.arange(nch) * CMP_STRIDE + CMP_BLOCK - 1), 0, None)
    o_c8, sel = _sample_cattn(q8, kvc, pad_g(bd[:, :, dist_c]), n_cmp, n_slc)

    mask = sel[:, :, 0, :n_slc] > 0.5
    idx = jnp.sort(jnp.where(mask, jnp.arange(n_slc, dtype=jnp.int32), jnp.int32(1 << 20)), axis=-1)
    n_gather = SLC_TOPN - 1
    idx = idx[..., :n_gather]
    pages = jnp.take_along_axis(page_table[:, None, :], idx // 2, axis=2)
    row_start = (pages * PAGE_SIZE + (idx % 2) * SLC_BLOCK).reshape(-1).astype(jnp.int32)
    kpos = (idx[..., None] * SLC_BLOCK + jnp.arange(SLC_BLOCK, dtype=jnp.int32)).reshape(db, NSA_KV_HEADS, -1)
    hh = jnp.arange(NSA_KV_HEADS)[None, :, None, None]
    gg = jnp.arange(NSA_GROUP)[None, None, :, None]
    bias_sel = bd[hh, gg, (past - kpos)[:, :, None, :]]
    bias_sel = jnp.concatenate([bias_sel, jnp.broadcast_to(bd[None, :, :, 0:1], (db, NSA_KV_HEADS, NSA_GROUP, 1)),
                                jnp.zeros((db, NSA_KV_HEADS, NSA_GROUP, SEL_KEYS - n_gather * SLC_BLOCK - 1), F32)],
                               axis=-1)
    bias_sel = jnp.concatenate([bias_sel, jnp.zeros_like(bias_sel)], axis=2)
    wlen = cache_win.shape[1]
    dist_w = np.clip(wlen - np.arange(WIN_KEYS), 0, None)
    kvs_new = u[:db, COL_KVS:COL_KVS + 2 * D_KV]
    kvw_new = u[:db, COL_KVW:COL_KVW + 2 * D_KV]
    win2d = cache_win.reshape(db, wlen, 2 * D_KV)
    o_s8, o_w8 = _sample_attn(row_start, q8, cache_slc.reshape(n_pool * PAGE_SIZE, 2 * D_KV), kvs_new[:, None, :],
                              bias_sel, win2d, kvw_new[:, None, :], pad_g(bd[:, :, dist_w]), n_gather)

    heads = lambda o: _pad_rows(o[:, :, :NSA_GROUP, :].reshape(db, D_NSA), tok)
    y = _outproj(_pad_rows(y_ml, tok), heads(o_c8), heads(o_s8), heads(o_w8), small, nsa_norm_g, w_out_b, x16, tok, 512)
    kvshape = (1, db, 1, 2, NSA_KV_HEADS, NSA_HEAD_DIM)
    new_win = jnp.concatenate([win2d[:, 1:], kvw_new[:, None, :]], axis=1)
    states = (u[:db, COL_KVC:COL_KVC + 2 * D_KV].reshape(kvshape), kvs_new.reshape(kvshape),
              new_win.reshape((1, db, wlen, 2, NSA_KV_HEADS, NSA_HEAD_DIM)),
              c_n[None], n_n[None], m_n[None], conv_n[None])
    return y, states


def kernel(x_prompt, x_sample, cache_cmp_kv, cache_slc_kv, cache_win_kv, state_mlstm_C, state_mlstm_n,
           state_mlstm_m, state_conv, page_table, rel_bias, norm_mix_g, w_in, b_ig, b_fg, conv_w, ml_norm_g,
           cmp_pos, cmp_w1, cmp_w2, nsa_norm_g, w_out, norm_ffn_g, w_router_grp, b_router_grp, w_router_exp,
           b_router_exp, w_gate, w_up, w_down, norm_final_g):
    B, S, D = x_prompt.shape
    wb = _reorder_w_in(w_in[0])
    gate_bias = jnp.zeros((1, LANE), F32).at[0, 0:ML_HEADS].set(b_ig[0]).at[0, ML_HEADS:2 * ML_HEADS].set(b_fg[0])
    w_out_b = w_out[0].astype(BF16)
    yp, st_p = _prompt_mixer(x_prompt, norm_mix_g, wb, gate_bias, conv_w[0], ml_norm_g, cmp_pos[0], cmp_w1[0],
                             cmp_w2[0], nsa_norm_g, w_out_b, rel_bias)
    ys, st_s = _sample_mixer(x_sample, cache_cmp_kv[0], cache_slc_kv[0], cache_win_kv[0], state_mlstm_C[0],
                             state_mlstm_n[0], state_mlstm_m[0], state_conv[0], page_table, norm_mix_g, wb, gate_bias,
                             conv_w[0], ml_norm_g, cmp_pos[0], cmp_w1[0], cmp_w2[0], nsa_norm_g, w_out_b, rel_bias)
    moe_w = (norm_ffn_g, w_router_grp[0], b_router_grp[0], w_router_exp[0], b_router_exp[0],
             w_gate[0], w_up[0], w_down[0], norm_final_g[None, :])
    DB, L, _ = x_sample.shape
    out_p = _moe_final(yp.reshape(B * S, D), *moe_w, 256, 128).reshape(B, S, D)
    out_s = _moe_final(_pad_rows(ys, 128), *moe_w, 128, 16)[:DB].reshape(DB, L, D)
    outs = [out_p, out_s]
    for a, b in zip(st_p, st_s):
        outs += [a, b]
    return tuple(outs)


def _prompt_mixer(x_prompt, norm_g, wb, gate_bias, conv_w, ml_norm_g, cmp_pos, cmp_w1, cmp_w2, nsa_norm_g,
                  w_out_b, rel_bias):
    B, S, D = x_prompt.shape
    x2d = x_prompt.reshape(B * S, D)
    tm = min(1024, B * S)
    u = _proj(x2d, norm_g, wb, tm, 512)
    small = u[:, COL_SMALL:COL_SMALL + LANE]
    y_ml, conv_n, c_n, n_n, m_n = _mlstm(
        u, small, jnp.zeros((B, CONV_W - 1, 2 * D_ML), F32),
        jnp.zeros((B, ML_HEADS, ML_HEAD_DIM, ML_HEAD_DIM), F32), jnp.zeros((B, ML_HEADS, ML_HEAD_DIM), F32),
        jnp.full((B, ML_HEADS), -jnp.inf, F32), conv_w, gate_bias, ml_norm_g, B, S, 256)
    kv_c = u[:, COL_KVC:COL_KVC + 2 * D_KV]
    kv_s = u[:, COL_KVS:COL_KVS + 2 * D_KV]
    kv_w = u[:, COL_KVW:COL_KVW + 2 * D_KV]
    nch = S // CMP_STRIDE
    kvc = _compress(u, COL_KVC // (2 * D_KV), B, S, cmp_pos, cmp_w1, cmp_w2, min(S, 4096))
    o_c, sel_t = _cattn(u, kvc, _cmp_bias(rel_bias, S, nch), B, S, 256)
    band = _band_bias(rel_bias, TQ, S // TQ)
    k_s, vt_s = _kv_layouts(kv_s, B, S)
    k_w, vt_w = _kv_layouts(kv_w, B, S)
    o_s = _flash(u, k_s, vt_s, band, sel_t, B, S)
    o_w = _flash(u, k_w, vt_w, band, None, B, S)
    y = _outproj(y_ml, o_c, o_s, o_w, small, nsa_norm_g, w_out_b, x2d, tm, 512)
    kvshape = (1, B, S, 2, NSA_KV_HEADS, NSA_HEAD_DIM)
    win = min(WINDOW, S)
    states = (kv_c.reshape(kvshape), kv_s.reshape(kvshape), kv_w.reshape(kvshape)[:, :, S - win:],
              c_n[None], n_n[None], m_n[None], conv_n[None])
    return y.reshape(B, S, D), states
```

```python
import functools
import math

import numpy as np
import jax
import jax.numpy as jnp
from jax import lax
from jax.experimental import pallas as pl
from jax.experimental.pallas import tpu as pltpu

F32 = jnp.float32
BF16 = jnp.bfloat16

D_MODEL = 2048
ML_HEADS = 4
ML_HEAD_DIM = 256
D_ML = 1024
CONV_W = 4
NSA_HEADS = 16
NSA_HEAD_DIM = 64
D_NSA = 1024
NSA_KV_HEADS = 4
NSA_GROUP = 4
D_KV = 256
CMP_BLOCK = 32
CMP_STRIDE = 16
CMP_HIDDEN = 256
SLC_BLOCK = 64
SLC_TOPN = 16
WINDOW = 512
N_BUCKETS = 32
MAX_DISTANCE = 2048
N_GROUPS = 4
EXPERTS_PER_GROUP = 8
N_EXPERTS = 32
D_EXPERT = 512
PAGE_SIZE = 128
EPS = 1e-6
NEG_BIG = -1e30
FORCE_SCORE = 1e4

LANE = 128
COL_QML, COL_KML, COL_VML, COL_OML = 0, 1024, 2048, 3072
COL_QNSA = 4096
COL_KVC, COL_KVS, COL_KVW = 5120, 5632, 6144
COL_SMALL = 6656
N_PROJ = 7168
VMEM_LIMIT = 56 * 1024 * 1024


def _cparams(sem, vmem=VMEM_LIMIT):
    return pltpu.CompilerParams(dimension_semantics=sem, vmem_limit_bytes=vmem)


def _split2(x):
    hi = x.astype(BF16)
    lo = (x - hi.astype(F32)).astype(BF16)
    return hi, lo


def _split3(x):
    hi = x.astype(BF16)
    r = x - hi.astype(F32)
    mid = r.astype(BF16)
    lo = (r - mid.astype(F32)).astype(BF16)
    return hi, mid, lo


def _dot(a, b):
    return jnp.dot(a, b, preferred_element_type=F32)


def _dot_nt(a, b):
    return lax.dot_general(a, b, (((1,), (1,)), ((), ())), preferred_element_type=F32)


def _dot_tn(a, b):
    return lax.dot_general(a, b, (((0,), (0,)), ((), ())), preferred_element_type=F32)


def _proj_kernel(x_ref, g_ref, w_ref, o_ref, h_scr):
    @pl.when(pl.program_id(1) == 0)
    def _():
        x = x_ref[...]
        ms = jnp.mean(x * x, axis=-1, keepdims=True)
        h_scr[...] = (x * lax.rsqrt(ms + EPS) * g_ref[...]).astype(BF16)

    o_ref[...] = _dot(h_scr[...], w_ref[...])


def _proj(x2d, g, wb, tm, tn):
    n, d = x2d.shape
    nc = wb.shape[1]
    return pl.pallas_call(
        _proj_kernel,
        grid=(n // tm, nc // tn),
        in_specs=[pl.BlockSpec((tm, d), lambda i, j: (i, 0)),
                  pl.BlockSpec((1, d), lambda i, j: (0, 0)),
                  pl.BlockSpec((d, tn), lambda i, j: (0, j))],
        out_specs=pl.BlockSpec((tm, tn), lambda i, j: (i, j)),
        out_shape=jax.ShapeDtypeStruct((n, nc), F32),
        scratch_shapes=[pltpu.VMEM((tm, d), BF16)],
        compiler_params=_cparams(("arbitrary", "arbitrary")),
        name="proj",
    )(x2d, g, wb)


def _reorder_w_in(w_in):
    big = w_in[:, :4 * D_ML]
    small_a = w_in[:, 4 * D_ML:4 * D_ML + 2 * ML_HEADS]
    rest = w_in[:, 4 * D_ML + 2 * ML_HEADS:]
    nsa = rest[:, :D_NSA + 6 * D_KV]
    gate = rest[:, D_NSA + 6 * D_KV:]
    pad = jnp.zeros((w_in.shape[0], N_PROJ - COL_SMALL - 2 * ML_HEADS - 3 * NSA_HEADS), w_in.dtype)
    return jnp.concatenate([big, nsa, small_a, gate, pad], axis=1).astype(BF16)


def _log_sigmoid(x):
    return jnp.minimum(x, 0.0) - jnp.log1p(jnp.exp(-jnp.abs(x)))


def _mlstm_kernel(q_ref, k_ref, v_ref, o_ref, s_ref, cb_ref, c0_ref, n0_ref, m0_ref,
                  cw_ref, gb_ref, ng_ref,
                  y_ref, cbo_ref, co_ref, no_ref, mo_ref,
                  ext_scr, c_scr, n_scr, m_scr, *, T):
    c = pl.program_id(1)
    nc = pl.num_programs(1)

    @pl.when(c == 0)
    def _():
        ext_scr[0:8, :] = jnp.zeros((8, 2 * D_ML), F32)
        ext_scr[5:8, :] = cb_ref[0]
        c_scr[...] = c0_ref[0]
        n_scr[...] = n0_ref[0]
        m_scr[...] = m0_ref[0]

    ext_scr[8:8 + T, 0:D_ML] = q_ref[...]
    ext_scr[8:8 + T, D_ML:2 * D_ML] = k_ref[...]
    conv = ext_scr[5:5 + T, :] * cw_ref[0:1, :]
    for j in range(1, CONV_W):
        conv = conv + ext_scr[5 + j:5 + j + T, :] * cw_ref[j:j + 1, :]
    tail = ext_scr[8 + T - 3:8 + T, :]
    ext_scr[5:8, :] = tail
    cbo_ref[0] = tail
    qk = conv * jax.nn.sigmoid(conv)

    pre = s_ref[...] + gb_ref[...]
    col = lax.broadcasted_iota(jnp.int32, pre.shape, 1)
    padrow = s_ref[:, LANE - 1:LANE] > 0.5
    gates = jnp.where(col < ML_HEADS, pre, _log_sigmoid(pre))
    gates = jnp.where(padrow, jnp.where(col < ML_HEADS, NEG_BIG, 0.0), gates)
    g_r = gates.T
    ti = lax.broadcasted_iota(jnp.int32, (T, T), 0)
    si = lax.broadcasted_iota(jnp.int32, (T, T), 1)
    upper = (ti <= si).astype(BF16)
    g_fin = jnp.where(lax.broadcasted_iota(jnp.int32, g_r.shape, 0) < ML_HEADS, 0.0, g_r)
    hi, mid, lo = _split3(g_fin)
    cum_r = _dot(hi, upper) + _dot(mid, upper) + _dot(lo, upper)
    rowi = lax.broadcasted_iota(jnp.int32, g_r.shape, 0)
    a_r = jnp.where(rowi < ML_HEADS, g_r, cum_r)
    a_c = a_r.T
    causal = si <= ti

    for h in range(ML_HEADS):
        sl = slice(h * ML_HEAD_DIM, (h + 1) * ML_HEAD_DIM)
        q = qk[:, h * ML_HEAD_DIM:(h + 1) * ML_HEAD_DIM]
        k = qk[:, D_ML + h * ML_HEAD_DIM:D_ML + (h + 1) * ML_HEAD_DIM] * (ML_HEAD_DIM ** -0.5)
        v = v_ref[:, sl]
        ig_r = a_r[h:h + 1, :]
        b_r = a_r[ML_HEADS + h:ML_HEADS + h + 1, :]
        ig_c = a_c[:, h:h + 1]
        b_c = a_c[:, ML_HEADS + h:ML_HEADS + h + 1]
        m_prev = m_scr[h:h + 1, 0:1]
        logd = jnp.where(causal, b_c - b_r + ig_r, -jnp.inf)
        inter = b_c + m_prev
        m_t = jnp.maximum(inter, jnp.max(logd, axis=1, keepdims=True))
        w_intra = jnp.exp(logd - m_t)
        w_inter = jnp.exp(inter - m_t)
        qb = q.astype(BF16)
        kb = k.astype(BF16)
        vb = v.astype(BF16)
        sc = _dot_nt(qb, kb) * w_intra
        cmat = c_scr[h]
        nvec = n_scr[h:h + 1, :]
        num = _dot(sc.astype(BF16), vb) + w_inter * _dot(qb, cmat.astype(BF16))
        qn = jnp.sum(qb.astype(F32) * nvec.astype(BF16).astype(F32), axis=1, keepdims=True)
        den = jnp.sum(sc, axis=1, keepdims=True) + w_inter * qn
        hh = num / jnp.maximum(jnp.abs(den), jnp.exp(-m_t))
        m_new = m_t[T - 1:T, :]
        b_last = b_c[T - 1:T, :]
        w_s = jnp.exp(b_last - b_c + ig_c - m_new)
        decay = jnp.exp(b_last + m_prev - m_new)
        kw = k * w_s
        c_new = decay * cmat + _dot_tn(kw.astype(BF16), vb)
        n_new = decay * nvec + jnp.sum(kw, axis=0, keepdims=True)
        c_scr[h] = c_new
        n_scr[h:h + 1, :] = n_new
        m_scr[h:h + 1, :] = jnp.broadcast_to(m_new, (1, LANE))
        hn = hh * lax.rsqrt(jnp.mean(hh * hh, axis=1, keepdims=True) + EPS) * ng_ref[:, sl]
        y_ref[:, sl] = (hn * jax.nn.sigmoid(o_ref[:, sl])).astype(y_ref.dtype)

    @pl.when(c == nc - 1)
    def _():
        co_ref[0] = c_scr[...]
        no_ref[0] = n_scr[...]
        mo_ref[0] = m_scr[...]


def _mlstm(u, small, conv_buf, c0, n0, m0, conv_w, gate_bias, norm_g, batch, seq, T):
    nc = seq // T
    cb = D_ML // 1024
    m0b = jnp.broadcast_to(m0[:, :, None], (batch, ML_HEADS, LANE))
    m0b = jnp.concatenate([m0b, jnp.zeros((batch, 8 - ML_HEADS, LANE), F32)], axis=1)
    n0p = jnp.concatenate([n0, jnp.zeros((batch, 8 - ML_HEADS, ML_HEAD_DIM), F32)], axis=1)
    row = lambda b, c: (b * nc + c, 0)
    outs = pl.pallas_call(
        functools.partial(_mlstm_kernel, T=T),
        grid=(batch, nc),
        in_specs=[pl.BlockSpec((T, D_ML), lambda b, c: (b * nc + c, COL_QML // D_ML)),
                  pl.BlockSpec((T, D_ML), lambda b, c: (b * nc + c, COL_KML // D_ML)),
                  pl.BlockSpec((T, D_ML), lambda b, c: (b * nc + c, COL_VML // D_ML)),
                  pl.BlockSpec((T, D_ML), lambda b, c: (b * nc + c, COL_OML // D_ML)),
                  pl.BlockSpec((T, LANE), row),
                  pl.BlockSpec((1, CONV_W - 1, 2 * D_ML), lambda b, c: (b, 0, 0)),
                  pl.BlockSpec((1, ML_HEADS, ML_HEAD_DIM, ML_HEAD_DIM), lambda b, c: (b, 0, 0, 0)),
                  pl.BlockSpec((1, 8, ML_HEAD_DIM), lambda b, c: (b, 0, 0)),
                  pl.BlockSpec((1, 8, LANE), lambda b, c: (b, 0, 0)),
                  pl.BlockSpec((CONV_W, 2 * D_ML), lambda b, c: (0, 0)),
                  pl.BlockSpec((1, LANE), lambda b, c: (0, 0)),
                  pl.BlockSpec((1, D_ML), lambda b, c: (0, 0))],
        out_specs=[pl.BlockSpec((T, D_ML), row),
                   pl.BlockSpec((1, CONV_W - 1, 2 * D_ML), lambda b, c: (b, 0, 0)),
                   pl.BlockSpec((1, ML_HEADS, ML_HEAD_DIM, ML_HEAD_DIM), lambda b, c: (b, 0, 0, 0)),
                   pl.BlockSpec((1, 8, ML_HEAD_DIM), lambda b, c: (b, 0, 0)),
                   pl.BlockSpec((1, 8, LANE), lambda b, c: (b, 0, 0))],
        out_shape=[jax.ShapeDtypeStruct((batch * seq, D_ML), BF16),
                   jax.ShapeDtypeStruct((batch, CONV_W - 1, 2 * D_ML), F32),
                   jax.ShapeDtypeStruct((batch, ML_HEADS, ML_HEAD_DIM, ML_HEAD_DIM), F32),
                   jax.ShapeDtypeStruct((batch, 8, ML_HEAD_DIM), F32),
                   jax.ShapeDtypeStruct((batch, 8, LANE), F32)],
        scratch_shapes=[pltpu.VMEM((8 + T, 2 * D_ML), F32),
                        pltpu.VMEM((ML_HEADS, ML_HEAD_DIM, ML_HEAD_DIM), F32),
                        pltpu.VMEM((8, ML_HEAD_DIM), F32),
                        pltpu.VMEM((8, LANE), F32)],
        compiler_params=_cparams(("arbitrary", "arbitrary")),
        name="mlstm",
    )(u, u, u, u, small, conv_buf, c0, n0p, m0b, conv_w, gate_bias, norm_g)
    y, cbo, co, no, mo = outs
    return y, cbo, co, no[:, :ML_HEADS], mo[:, :ML_HEADS, 0]


def _bucket_np(dist):
    n = np.maximum(dist, 0)
    max_exact = N_BUCKETS // 2
    nf = np.maximum(n, 1).astype(np.float64)
    large = max_exact + (np.log(nf / max_exact) / math.log(MAX_DISTANCE / max_exact)
                         * (N_BUCKETS - max_exact)).astype(np.int64)
    return np.where(n < max_exact, n, np.minimum(large, N_BUCKETS - 1)).astype(np.int32)


def _bias_by_distance(rel_bias, n):
    return rel_bias.astype(F32)[_bucket_np(np.arange(n))].T


def _band_bias(rel_bias, tile, n_delta):
    lc = n_delta * tile
    a = _bias_by_distance(rel_bias, lc + 1)
    y = jnp.tile(a, (1, tile))[:, :tile * lc].reshape(NSA_HEADS, tile, lc)
    y = y.reshape(NSA_KV_HEADS, NSA_GROUP, tile, n_delta, tile).transpose(0, 3, 2, 1, 4)
    return y.reshape(NSA_KV_HEADS, n_delta, tile, NSA_GROUP * tile)


def _cmp_bias_table(rel_bias, seq, nch, tq):
    t = np.arange(tq)[:, None]
    m = np.arange(2 * nch)[None, :]
    dist = t - CMP_STRIDE * (m - (nch - 1)) - (CMP_BLOCK - 1)
    idx = np.clip(dist, 0, seq - 1).astype(np.int32)
    a = _bias_by_distance(rel_bias, seq)
    return jnp.take(a, jnp.asarray(idx.reshape(-1)), axis=1).reshape(NSA_HEADS, tq, 2 * nch)


def _overlap_t(n_cmp, nch, n_slc):
    c0 = np.arange(nch) * CMP_STRIDE
    s0 = np.arange(n_slc) * SLC_BLOCK
    ov = np.minimum(c0[None, :] + CMP_BLOCK, s0[:, None] + SLC_BLOCK) - np.maximum(c0[None, :], s0[:, None])
    ov = np.clip(ov, 0, None).astype(np.float32) / CMP_BLOCK
    ov[:, n_cmp:] = 0.0
    return jnp.asarray(ov, BF16)


def _pq_kernel(x0_ref, x1_ref, x2_ref, x3_ref, pos_ref, w_ref, o_ref, *, rows):
    for pair, x_ref in enumerate((x0_ref, x1_ref, x2_ref, x3_ref)):
        toks = [x_ref[pl.ds(l, rows, stride=CMP_STRIDE), :] for l in range(CMP_STRIDE)]
        for half in range(2):
            ch = 2 * pair + half
            c = ch // NSA_KV_HEADS
            lanes = slice(half * NSA_HEAD_DIM, (half + 1) * NSA_HEAD_DIM)
            chunk = jnp.concatenate([t[:, lanes] for t in toks], axis=1)
            for part in range(2):
                lhs = (chunk + pos_ref[c, part:part + 1, :]).astype(BF16)
                o_ref[0, ch, :, part * CMP_HIDDEN:(part + 1) * CMP_HIDDEN] = _dot(lhs, w_ref[c, part])


def _hid_kernel(pq_ref, w2_ref, o_ref, *, nch):
    p = pq_ref[0, 0, :, 0:CMP_HIDDEN]
    q = pltpu.roll(pq_ref[0, 0, :, CMP_HIDDEN:2 * CMP_HIDDEN], nch - 1, 0)
    hid = jax.nn.gelu(p + q, approximate=True)
    o_ref[0, 0] = _dot(hid.astype(BF16), w2_ref[0])


def _compress(kv2d, col_block, bsz, seq, cmp_pos, cmp_w1, cmp_w2, tt):
    nch = seq // CMP_STRIDE
    kdim = CMP_STRIDE * NSA_HEAD_DIM
    nt = seq // tt
    tn = tt // CMP_STRIDE
    w1 = cmp_w1.reshape(2, 2, kdim, CMP_HIDDEN).astype(BF16)
    pos = cmp_pos.reshape(2, 2, kdim)
    pq = pl.pallas_call(
        functools.partial(_pq_kernel, rows=tn),
        grid=(bsz, nt),
        in_specs=[pl.BlockSpec((tt, LANE), functools.partial(lambda b, i, k: (b * nt + i, col_block * 4 + k), k=k))
                  for k in range(4)] +
                 [pl.BlockSpec((2, 2, kdim), lambda b, i: (0, 0, 0)),
                  pl.BlockSpec((2, 2, kdim, CMP_HIDDEN), lambda b, i: (0, 0, 0, 0))],
        out_specs=pl.BlockSpec((1, 8, tn, 2 * CMP_HIDDEN), lambda b, i: (b, 0, i, 0)),
        out_shape=jax.ShapeDtypeStruct((bsz, 8, nch, 2 * CMP_HIDDEN), F32),
        compiler_params=_cparams(("arbitrary", "arbitrary")),
        name="cmp_pq",
    )(kv2d, kv2d, kv2d, kv2d, pos, w1)
    return pl.pallas_call(
        functools.partial(_hid_kernel, nch=nch),
        grid=(bsz, 8),
        in_specs=[pl.BlockSpec((1, 1, nch, 2 * CMP_HIDDEN), lambda b, c: (b, c, 0, 0)),
                  pl.BlockSpec((1, CMP_HIDDEN, NSA_HEAD_DIM), lambda b, c: (c // NSA_KV_HEADS, 0, 0))],
        out_specs=pl.BlockSpec((1, 1, nch, NSA_HEAD_DIM), lambda b, c: (b, c, 0, 0)),
        out_shape=jax.ShapeDtypeStruct((bsz, 8, nch, NSA_HEAD_DIM), F32),
        compiler_params=_cparams(("arbitrary", "arbitrary")),
        name="cmp_hid",
    )(pq, cmp_w2.astype(BF16))


def _top_n_mask(score_t, blk, n_rows):
    rank = jnp.zeros(score_t.shape, jnp.int32)
    for i in range(n_rows):
        row = score_t[i:i + 1, :]
        beats = (row > score_t) | ((row == score_t) & (blk > i))
        rank = rank + beats.astype(jnp.int32)
    return (rank < min(SLC_TOPN, n_rows)).astype(F32)


def _cattn_kernel(q_ref, kc_ref, vc_ref, b_ref, ovt_ref, o_ref, sel_ref, *, tq, nch, n_cmp, n_slc):
    i = pl.program_id(2)
    kc = kc_ref[0, 0].astype(BF16)
    vc = vc_ref[0, 0].astype(BF16)
    t = i * tq + lax.broadcasted_iota(jnp.int32, (tq, nch), 0)
    n = lax.broadcasted_iota(jnp.int32, (tq, nch), 1)
    valid = (t - CMP_STRIDE * n - (CMP_BLOCK - 1) >= 0) & (n < n_cmp)
    pc = jnp.zeros((tq, nch), F32)
    shift = (nch - 1) - i * (tq // CMP_STRIDE)
    amount = lax.rem(2 * nch - shift, 2 * nch)
    for g in range(NSA_GROUP):
        sl = slice(g * NSA_HEAD_DIM, (g + 1) * NSA_HEAD_DIM)
        qg = (q_ref[:, sl] * (NSA_HEAD_DIM ** -0.5)).astype(BF16)
        s = _dot_nt(qg, kc) + pltpu.roll(b_ref[g], amount, 1)[:, 0:nch]
        s = jnp.where(valid, s, NEG_BIG)
        e = jnp.exp(s - jnp.max(s, axis=1, keepdims=True))
        p = jnp.where(valid, e / jnp.sum(e, axis=1, keepdims=True), 0.0)
        o_ref[:, sl] = _dot(p.astype(BF16), vc)
        pc = pc + p
    score_t = _dot_nt(ovt_ref[...], pc.astype(BF16))
    blk = lax.broadcasted_iota(jnp.int32, (n_slc, tq), 0)
    cur = (i * tq + lax.broadcasted_iota(jnp.int32, (n_slc, tq), 1)) // SLC_BLOCK
    forced = (blk == 0) | (blk == cur) | (blk == cur - 1)
    score_t = jnp.where(forced, FORCE_SCORE, score_t)
    score_t = jnp.where(blk > cur, -FORCE_SCORE, score_t)
    sel_ref[0, 0] = _top_n_mask(score_t, blk, n_slc)


def _cattn(u, kvc, biasc, batch, seq, tq):
    nch = kvc.shape[2]
    n_cmp = nch - 1
    n_slc = seq // SLC_BLOCK
    ni = seq // tq
    ovt = _overlap_t(n_cmp, nch, n_slc)
    qcol = COL_QNSA // (NSA_GROUP * NSA_HEAD_DIM)
    return pl.pallas_call(
        functools.partial(_cattn_kernel, tq=tq, nch=nch, n_cmp=n_cmp, n_slc=n_slc),
        grid=(batch, NSA_KV_HEADS, ni),
        in_specs=[pl.BlockSpec((tq, NSA_GROUP * NSA_HEAD_DIM), lambda b, h, i: (b * ni + i, qcol + h)),
                  pl.BlockSpec((1, 1, nch, NSA_HEAD_DIM), lambda b, h, i: (b, h, 0, 0)),
                  pl.BlockSpec((1, 1, nch, NSA_HEAD_DIM), lambda b, h, i: (b, NSA_KV_HEADS + h, 0, 0)),
                  pl.BlockSpec((NSA_GROUP, tq, 2 * nch), lambda b, h, i: (h, 0, 0)),
                  pl.BlockSpec((n_slc, nch), lambda b, h, i: (0, 0))],
        out_specs=[pl.BlockSpec((tq, NSA_GROUP * NSA_HEAD_DIM), lambda b, h, i: (b * ni + i, h)),
                   pl.BlockSpec((1, 1, n_slc, tq), lambda b, h, i: (b, h, 0, i))],
        out_shape=[jax.ShapeDtypeStruct((batch * seq, D_NSA), F32),
                   jax.ShapeDtypeStruct((batch, NSA_KV_HEADS, n_slc, seq), F32)],
        compiler_params=_cparams(("arbitrary", "arbitrary", "arbitrary")),
        name="cmp_attn",
    )(u, kvc, kvc, biasc, ovt)


TQ = 128


def _flash_kernel(q_ref, k_ref, v_ref, b_ref, *rest, selected):
    if selected:
        sel_ref, o_ref, qs_scr, m_scr, l_scr, acc_scr = rest
    else:
        o_ref, qs_scr, m_scr, l_scr, acc_scr = rest
    i = pl.program_id(2)
    for g in range(NSA_GROUP):
        qs_scr[g * TQ:(g + 1) * TQ, :] = (
            q_ref[:, g * NSA_HEAD_DIM:(g + 1) * NSA_HEAD_DIM] * (NSA_HEAD_DIM ** -0.5)).astype(BF16)
    m_scr[...] = jnp.full(m_scr.shape, NEG_BIG, F32)
    l_scr[...] = jnp.zeros(l_scr.shape, F32)
    acc_scr[...] = jnp.zeros(acc_scr.shape, F32)
    wide = NSA_GROUP * TQ
    row = lax.broadcasted_iota(jnp.int32, (TQ, wide), 0)
    lane = lax.broadcasted_iota(jnp.int32, (TQ, wide), 1)
    base = (lane & (TQ - 1)) - row

    def body(j, carry):
        kj = k_ref[0, 0, pl.ds(pl.multiple_of(j * TQ, TQ), TQ), :].astype(BF16)
        s = _dot_nt(kj, qs_scr[...]) + b_ref[0, i - j]
        dist = base + (i - j) * TQ
        if selected:
            r = sel_ref[0, 0, pl.ds(2 * j, 2), :]
            r = jnp.concatenate([r] * NSA_GROUP, axis=1)
            picked = jnp.where(row < SLC_BLOCK, r[0:1, :], r[1:2, :]) > 0.5
            valid = (dist >= 0) & picked
        else:
            valid = (dist >= 0) & (dist < WINDOW)
        s = jnp.where(valid, s, NEG_BIG)
        m_old = m_scr[...]
        m_new = jnp.maximum(m_old, jnp.max(s, axis=0, keepdims=True))
        alpha = jnp.exp(m_old - m_new)
        p = jnp.where(valid, jnp.exp(s - m_new), 0.0)
        l_scr[...] = alpha * l_scr[...] + jnp.sum(p, axis=0, keepdims=True)
        vj = v_ref[0, 0, pl.ds(pl.multiple_of(j * TQ, TQ), TQ), :].astype(BF16)
        acc_scr[...] = alpha * acc_scr[...] + _dot_tn(vj, p.astype(BF16))
        m_scr[...] = m_new
        return carry

    lo = 0 if selected else jnp.maximum(i - WINDOW // TQ, 0)
    lax.fori_loop(lo, i + 1, body, 0)
    o_ref[0, 0, 0] = acc_scr[...] / l_scr[...]


def _flash(u, k, v, band, sel_t, batch, seq):
    ni = seq // TQ
    qcol = COL_QNSA // (NSA_GROUP * NSA_HEAD_DIM)
    selected = sel_t is not None
    n_delta = band.shape[1]
    in_specs = [pl.BlockSpec((TQ, NSA_GROUP * NSA_HEAD_DIM), lambda b, h, i: (b * ni + i, qcol + h)),
                pl.BlockSpec((1, 1, seq, NSA_HEAD_DIM), lambda b, h, i: (b, h, 0, 0)),
                pl.BlockSpec((1, 1, seq, NSA_HEAD_DIM), lambda b, h, i: (b, h, 0, 0)),
                pl.BlockSpec((1, n_delta, TQ, NSA_GROUP * TQ), lambda b, h, i: (h, 0, 0, 0))]
    args = [u, k, v, band]
    if selected:
        n_slc = sel_t.shape[2]
        in_specs.append(pl.BlockSpec((1, 1, n_slc, TQ), lambda b, h, i: (b, h, 0, i)))
        args.append(sel_t)
    o_t = pl.pallas_call(
        functools.partial(_flash_kernel, selected=selected),
        grid=(batch, NSA_KV_HEADS, ni),
        in_specs=in_specs,
        out_specs=pl.BlockSpec((1, 1, 1, NSA_HEAD_DIM, NSA_GROUP * TQ), lambda b, h, i: (b, h, i, 0, 0)),
        out_shape=jax.ShapeDtypeStruct((batch, NSA_KV_HEADS, ni, NSA_HEAD_DIM, NSA_GROUP * TQ), F32),
        scratch_shapes=[pltpu.VMEM((NSA_GROUP * TQ, NSA_HEAD_DIM), BF16),
                        pltpu.VMEM((1, NSA_GROUP * TQ), F32),
                        pltpu.VMEM((1, NSA_GROUP * TQ), F32),
                        pltpu.VMEM((NSA_HEAD_DIM, NSA_GROUP * TQ), F32)],
        compiler_params=_cparams(("arbitrary", "arbitrary", "arbitrary")),
        name="slc_attn" if selected else "win_attn",
    )(*args)
    o = o_t.reshape(batch, NSA_KV_HEADS, ni, NSA_HEAD_DIM, NSA_GROUP, TQ).transpose(0, 2, 5, 1, 4, 3)
    return o.reshape(batch * seq, D_NSA)


def _kv_layouts(kv2d, batch, seq):
    kv = kv2d.reshape(batch, seq, 2, NSA_KV_HEADS, NSA_HEAD_DIM)
    return kv[:, :, 0].transpose(0, 2, 1, 3), kv[:, :, 1].transpose(0, 2, 1, 3)


def _gate_expand():
    e = np.zeros((3, LANE, D_NSA), np.float32)
    for br in range(3):
        for hd in range(NSA_HEADS):
            e[br, 2 * ML_HEADS + br * NSA_HEADS + hd, hd * NSA_HEAD_DIM:(hd + 1) * NSA_HEAD_DIM] = 1.0
    return jnp.asarray(e, BF16)


def _outproj_kernel(yml_ref, oc_ref, os_ref, ow_ref, s_ref, e_ref, ng_ref, w_ref, x_ref, o_ref, cat_scr):
    @pl.when(pl.program_id(1) == 0)
    def _():
        sig = jax.nn.sigmoid(s_ref[...])
        hi, lo = _split2(sig)
        o = jnp.zeros(oc_ref.shape, F32)
        for br, ref in enumerate((oc_ref, os_ref, ow_ref)):
            gexp = _dot(hi, e_ref[br]) + _dot(lo, e_ref[br])
            o = o + gexp * ref[...]
        o = o * lax.rsqrt(jnp.mean(o * o, axis=-1, keepdims=True) + EPS) * ng_ref[...]
        cat_scr[:, 0:D_ML] = yml_ref[...]
        cat_scr[:, D_ML:D_ML + D_NSA] = o.astype(BF16)

    o_ref[...] = x_ref[...] + _dot(cat_scr[...], w_ref[...])


def _outproj(y_ml, o_c, o_s, o_w, small, nsa_g, w_out_b, x2d, tm, tn):
    n, d = x2d.shape
    rows = lambda i, j: (i, 0)
    return pl.pallas_call(
        _outproj_kernel,
        grid=(n // tm, d // tn),
        in_specs=[pl.BlockSpec((tm, D_ML), rows), pl.BlockSpec((tm, D_NSA), rows),
                  pl.BlockSpec((tm, D_NSA), rows), pl.BlockSpec((tm, D_NSA), rows),
                  pl.BlockSpec((tm, LANE), rows),
                  pl.BlockSpec((3, LANE, D_NSA), lambda i, j: (0, 0, 0)),
                  pl.BlockSpec((1, D_NSA), lambda i, j: (0, 0)),
                  pl.BlockSpec((D_ML + D_NSA, tn), lambda i, j: (0, j)),
                  pl.BlockSpec((tm, tn), lambda i, j: (i, j))],
        out_specs=pl.BlockSpec((tm, tn), lambda i, j: (i, j)),
        out_shape=jax.ShapeDtypeStruct((n, d), F32),
        scratch_shapes=[pltpu.VMEM((tm, D_ML + D_NSA), BF16)],
        compiler_params=_cparams(("arbitrary", "arbitrary")),
        name="outproj",
    )(y_ml, o_c, o_s, o_w, small, _gate_expand(), nsa_g, w_out_b, x2d)


ROUTE_COL = N_GROUPS
BIG_COL = 1 << 20
SPLIT = D_MODEL // LANE


def _store_split(ref, val, n):
    for k in range(SPLIT):
        ref[pl.ds(k, n, stride=SPLIT), :] = val[:, k * LANE:(k + 1) * LANE]


def _load_split(ref, n):
    return jnp.concatenate([ref[pl.ds(k, n, stride=SPLIT), :] for k in range(SPLIT)], axis=1)


def _route_kernel(y_ref, g_ref, w_ref, b_ref, h_ref, info_ref, cnt_ref, carry_scr, *, tm):
    @pl.when(pl.program_id(0) == 0)
    def _():
        carry_scr[...] = jnp.zeros(carry_scr.shape, F32)

    x = y_ref[...]
    h = x * lax.rsqrt(jnp.mean(x * x, axis=-1, keepdims=True) + EPS) * g_ref[...]
    _store_split(h_ref, h, tm)
    logit = _dot(h.astype(BF16), w_ref[...]) + b_ref[...]
    col = lax.broadcasted_iota(jnp.int32, logit.shape, 1)
    is_grp = col < N_GROUPS
    gmax = jnp.max(jnp.where(is_grp, logit, -jnp.inf), axis=1, keepdims=True)
    gtop = jnp.min(jnp.where(is_grp & (logit == gmax), col, BIG_COL), axis=1, keepdims=True)
    gsum = jnp.sum(jnp.where(is_grp, jnp.exp(logit - gmax), 0.0), axis=1, keepdims=True)
    first = ROUTE_COL + gtop * EXPERTS_PER_GROUP
    in_grp = (col >= first) & (col < first + EXPERTS_PER_GROUP)
    v1 = jnp.max(jnp.where(in_grp, logit, -jnp.inf), axis=1, keepdims=True)
    i1 = jnp.min(jnp.where(in_grp & (logit == v1), col, BIG_COL), axis=1, keepdims=True)
    rest = in_grp & (col != i1)
    v2 = jnp.max(jnp.where(rest, logit, -jnp.inf), axis=1, keepdims=True)
    i2 = jnp.min(jnp.where(rest & (logit == v2), col, BIG_COL), axis=1, keepdims=True)
    e = jnp.exp(v2 - v1)
    w1 = 1.0 / ((1.0 + e) * gsum)
    w2 = e / ((1.0 + e) * gsum)
    pick1 = col == i1
    pick2 = col == i2
    both = (pick1 | pick2).astype(F32)
    ri = lax.broadcasted_iota(jnp.int32, (tm, tm), 0)
    ci = lax.broadcasted_iota(jnp.int32, (tm, tm), 1)
    before = (ci < ri).astype(BF16)
    cum = _dot(before, both.astype(BF16)) + carry_scr[...]
    r1 = jnp.sum(jnp.where(pick1, cum, 0.0), axis=1, keepdims=True)
    r2 = jnp.sum(jnp.where(pick2, cum, 0.0), axis=1, keepdims=True)
    carry_scr[...] = carry_scr[...] + jnp.sum(both, axis=0, keepdims=True)
    cnt_ref[...] = jnp.broadcast_to(carry_scr[...], cnt_ref.shape)
    info = jnp.where(col == 0, (i1 - ROUTE_COL).astype(F32), 0.0)
    info = jnp.where(col == 1, (i2 - ROUTE_COL).astype(F32), info)
    info = jnp.where(col == 2, w1, info)
    info = jnp.where(col == 3, w2, info)
    info = jnp.where(col == 4, r1, info)
    info_ref[...] = jnp.where(col == 5, r2, info)


def _route(y2d, g, w_rg, b_rg, w_re, b_re, tm):
    n, d = y2d.shape
    wr = jnp.concatenate([w_rg, w_re, jnp.zeros((d, LANE - N_GROUPS - N_EXPERTS), F32)], axis=1).astype(BF16)
    bias = jnp.concatenate([b_rg, b_re, jnp.zeros((LANE - N_GROUPS - N_EXPERTS,), F32)])[None, :]
    rows = lambda i: (i, 0)
    fixed = lambda i: (0, 0)
    return pl.pallas_call(
        functools.partial(_route_kernel, tm=tm),
        grid=(n // tm,),
        in_specs=[pl.BlockSpec((tm, d), rows), pl.BlockSpec((1, d), fixed),
                  pl.BlockSpec((d, LANE), fixed), pl.BlockSpec((1, LANE), fixed)],
        out_specs=[pl.BlockSpec((tm * SPLIT, LANE), rows), pl.BlockSpec((tm, LANE), rows),
                   pl.BlockSpec((8, LANE), fixed)],
        out_shape=[jax.ShapeDtypeStruct((n * SPLIT, LANE), F32), jax.ShapeDtypeStruct((n, LANE), F32),
                   jax.ShapeDtypeStruct((8, LANE), F32)],
        scratch_shapes=[pltpu.VMEM((1, LANE), F32)],
        compiler_params=_cparams(("arbitrary",)),
        name="moe_route",
    )(y2d, g, wr, bias)


def _expert_kernel(te_ref, nu_ref, src_ref, h_hbm, rw_ref, wg_ref, wu_ref, wd_ref, o_ref,
                   xbuf, wg_b, wu_b, wd_b, sem, *, tme):
    i = pl.program_id(0)
    slot = i % 2

    def row_copy(tile, r, buf_slot):
        src_row = src_ref[tile * tme + r]
        return pltpu.make_async_copy(h_hbm.at[pl.ds(pl.multiple_of(src_row * SPLIT, SPLIT), SPLIT)],
                                     xbuf.at[buf_slot, pl.ds(pl.multiple_of(r * SPLIT, SPLIT), SPLIT)],
                                     sem.at[buf_slot])

    def fetch(tile, buf_slot):
        def body(r, carry):
            row_copy(tile, r, buf_slot).start()
            return carry
        lax.fori_loop(0, tme, body, 0)

    @pl.when(i == 0)
    def _():
        fetch(0, 0)

    @pl.when(i + 1 < nu_ref[0])
    def _():
        fetch(i + 1, 1 - slot)

    @pl.when(i < nu_ref[0])
    def _():
        prev = te_ref[jnp.maximum(i - 1, 0)]

        @pl.when((i == 0) | (te_ref[i] != prev))
        def _():
            wg_b[...] = wg_ref[0].astype(BF16)
            wu_b[...] = wu_ref[0].astype(BF16)
            wd_b[...] = wd_ref[0].astype(BF16)

        def wait_body(r, carry):
            row_copy(i, r, slot).wait()
            return carry
        lax.fori_loop(0, tme, wait_body, 0)

        x = _load_split(xbuf.at[slot], tme).astype(BF16)
        a = _dot(x, wg_b[...])
        u = _dot(x, wu_b[...])
        hid = a * jax.nn.sigmoid(a) * u * rw_ref[...]
        _store_split(o_ref, _dot(hid.astype(BF16), wd_b[...]), tme)

    @pl.when(i >= nu_ref[0])
    def _():
        o_ref[...] = jnp.zeros(o_ref.shape, F32)


def _experts(h, src, roww, tile_expert, n_used, w_gate, w_up, w_down, tme):
    p = src.shape[0]
    d = D_MODEL
    return pl.pallas_call(
        functools.partial(_expert_kernel, tme=tme),
        grid_spec=pltpu.PrefetchScalarGridSpec(
            num_scalar_prefetch=3, grid=(p // tme,),
            in_specs=[pl.BlockSpec(memory_space=pl.ANY),
                      pl.BlockSpec((tme, 1), lambda i, te, nu, sr: (i, 0)),
                      pl.BlockSpec((1, d, D_EXPERT), lambda i, te, nu, sr: (te[i], 0, 0)),
                      pl.BlockSpec((1, d, D_EXPERT), lambda i, te, nu, sr: (te[i], 0, 0)),
                      pl.BlockSpec((1, D_EXPERT, d), lambda i, te, nu, sr: (te[i], 0, 0))],
            out_specs=pl.BlockSpec((tme * SPLIT, LANE), lambda i, te, nu, sr: (i, 0)),
            scratch_shapes=[pltpu.VMEM((2, tme * SPLIT, LANE), F32),
                            pltpu.VMEM((d, D_EXPERT), BF16), pltpu.VMEM((d, D_EXPERT), BF16),
                            pltpu.VMEM((D_EXPERT, d), BF16), pltpu.SemaphoreType.DMA((2,))]),
        out_shape=jax.ShapeDtypeStruct((p * SPLIT, LANE), F32),
        compiler_params=_cparams(("arbitrary",)),
        name="moe_experts",
    )(tile_expert, n_used, src, h, roww, w_gate, w_up, w_down)


def _combine_kernel(d1_ref, d2_ref, y_ref, ys_hbm, g_ref, o_ref, a_buf, b_buf, sem, *, tm):
    base = pl.program_id(0) * tm

    def copy(r, row, buf):
        return pltpu.make_async_copy(ys_hbm.at[pl.ds(pl.multiple_of(row * SPLIT, SPLIT), SPLIT)],
                                     buf.at[pl.ds(pl.multiple_of(r * SPLIT, SPLIT), SPLIT)], sem)

    def issue(r, carry):
        copy(r, d1_ref[base + r], a_buf).start()
        copy(r, d2_ref[base + r], b_buf).start()
        return carry

    lax.fori_loop(0, tm, issue, 0)

    def drain(r, carry):
        copy(r, 0, a_buf).wait()
        copy(r, 0, b_buf).wait()
        return carry

    lax.fori_loop(0, tm, drain, 0)
    y = y_ref[...] + _load_split(a_buf, tm) + _load_split(b_buf, tm)
    o_ref[...] = y * lax.rsqrt(jnp.mean(y * y, axis=-1, keepdims=True) + EPS) * g_ref[...]


def _combine(y2d, ys, d1, d2, g, tm):
    n, d = y2d.shape
    return pl.pallas_call(
        functools.partial(_combine_kernel, tm=tm),
        grid_spec=pltpu.PrefetchScalarGridSpec(
            num_scalar_prefetch=2, grid=(n // tm,),
            in_specs=[pl.BlockSpec((tm, d), lambda i, a, b: (i, 0)),
                      pl.BlockSpec(memory_space=pl.ANY),
                      pl.BlockSpec((1, d), lambda i, a, b: (0, 0))],
            out_specs=pl.BlockSpec((tm, d), lambda i, a, b: (i, 0)),
            scratch_shapes=[pltpu.VMEM((tm * SPLIT, LANE), F32), pltpu.VMEM((tm * SPLIT, LANE), F32),
                            pltpu.SemaphoreType.DMA(())]),
        out_shape=jax.ShapeDtypeStruct((n, d), F32),
        compiler_params=_cparams(("arbitrary",)),
        name="moe_combine",
    )(d1, d2, y2d, ys, g)


def _moe_final(y2d, ffn_g, w_rg, b_rg, w_re, b_re, w_gate, w_up, w_down, final_g, tm, tme):
    n, d = y2d.shape
    h, info, cnt = _route(y2d, ffn_g, w_rg, b_rg, w_re, b_re, tm)
    e1 = info[:, 0].astype(jnp.int32)
    e2 = info[:, 1].astype(jnp.int32)
    counts = cnt[0, ROUTE_COL:ROUTE_COL + N_EXPERTS].astype(jnp.int32)
    padded = (counts + tme - 1) // tme * tme
    ends = jnp.cumsum(padded)
    offs = ends - padded
    d1 = offs[e1] + info[:, 4].astype(jnp.int32)
    d2 = offs[e2] + info[:, 5].astype(jnp.int32)
    p = 2 * n + N_EXPERTS * tme
    tok = jnp.arange(n, dtype=jnp.int32)
    src = jnp.zeros((p,), jnp.int32).at[d1].set(tok).at[d2].set(tok)
    roww = jnp.zeros((p,), F32).at[d1].set(info[:, 2]).at[d2].set(info[:, 3])
    n_used = (ends[-1] // tme).astype(jnp.int32).reshape(1)
    tile_start = jnp.arange(p // tme, dtype=jnp.int32) * tme
    tile_expert = jnp.sum((ends[None, :] <= tile_start[:, None]).astype(jnp.int32), axis=1)
    last = jnp.sum((ends <= ends[-1] - 1).astype(jnp.int32))
    tile_expert = jnp.minimum(tile_expert, last).astype(jnp.int32)
    ys = _experts(h, src, roww[:, None], tile_expert, n_used, w_gate, w_up, w_down, tme)
    return _combine(y2d, ys, d1, d2, final_g, tm)


GROUP_ROWS = 8
SLC_LANES = 384
SEL_KEYS = 1024
WIN_KEYS = 640


def _sample_cattn_kernel(q_ref, kc_ref, vc_ref, b_ref, ov_ref, o_ref, sel_ref, *, n_cmp, n_slc):
    q = (q_ref[0, 0] * (NSA_HEAD_DIM ** -0.5)).astype(BF16)
    s = _dot_nt(q, kc_ref[0, 0].astype(BF16)) + b_ref[0]
    valid = lax.broadcasted_iota(jnp.int32, s.shape, 1) < n_cmp
    s = jnp.where(valid, s, NEG_BIG)
    e = jnp.exp(s - jnp.max(s, axis=1, keepdims=True))
    p = jnp.where(valid, e / jnp.sum(e, axis=1, keepdims=True), 0.0)
    o_ref[0, 0] = _dot(p.astype(BF16), vc_ref[0, 0].astype(BF16))
    pg = jnp.where(lax.broadcasted_iota(jnp.int32, p.shape, 0) < NSA_GROUP, p, 0.0)
    pc = jnp.broadcast_to(jnp.sum(pg, axis=0, keepdims=True), pg.shape)
    score = _dot(pc.astype(BF16), ov_ref[...])[0:1, :]
    blk = lax.broadcasted_iota(jnp.int32, score.shape, 1)
    cur = n_slc - 1
    forced = (blk == 0) | (blk == cur) | (blk == cur - 1)
    score = jnp.where(forced, FORCE_SCORE, score)
    score = jnp.where(blk > cur, -jnp.inf, score)
    ri = lax.broadcasted_iota(jnp.int32, (SLC_LANES, SLC_LANES), 0)
    ci = lax.broadcasted_iota(jnp.int32, (SLC_LANES, SLC_LANES), 1)
    sb = jnp.broadcast_to(score, (SLC_LANES, SLC_LANES))
    col = jnp.sum(jnp.where(ri == ci, sb, 0.0), axis=1, keepdims=True)
    beats = (col > sb) | ((col == sb) & (ri < ci))
    rank = jnp.sum(beats.astype(jnp.int32), axis=0, keepdims=True)
    sel_ref[0, 0] = jnp.broadcast_to((rank < SLC_TOPN).astype(F32), (GROUP_ROWS, SLC_LANES))


def _sample_cattn(q8, kvc, bias_cs, n_cmp, n_slc):
    bsz, _, nch, _ = kvc.shape
    c0 = np.arange(nch) * CMP_STRIDE
    s0 = np.arange(SLC_LANES) * SLC_BLOCK
    ov = np.minimum(c0[:, None] + CMP_BLOCK, s0[None, :] + SLC_BLOCK) - np.maximum(c0[:, None], s0[None, :])
    ov = np.clip(ov, 0, None).astype(np.float32) / CMP_BLOCK
    ov[n_cmp:, :] = 0.0
    ov[:, n_slc:] = 0.0
    return pl.pallas_call(
        functools.partial(_sample_cattn_kernel, n_cmp=n_cmp, n_slc=n_slc),
        grid=(bsz, NSA_KV_HEADS),
        in_specs=[pl.BlockSpec((1, 1, GROUP_ROWS, NSA_HEAD_DIM), lambda b, h: (b, h, 0, 0)),
                  pl.BlockSpec((1, 1, nch, NSA_HEAD_DIM), lambda b, h: (b, h, 0, 0)),
                  pl.BlockSpec((1, 1, nch, NSA_HEAD_DIM), lambda b, h: (b, NSA_KV_HEADS + h, 0, 0)),
                  pl.BlockSpec((1, GROUP_ROWS, nch), lambda b, h: (h, 0, 0)),
                  pl.BlockSpec((nch, SLC_LANES), lambda b, h: (0, 0))],
        out_specs=[pl.BlockSpec((1, 1, GROUP_ROWS, NSA_HEAD_DIM), lambda b, h: (b, h, 0, 0)),
                   pl.BlockSpec((1, 1, GROUP_ROWS, SLC_LANES), lambda b, h: (b, h, 0, 0))],
        out_shape=[jax.ShapeDtypeStruct((bsz, NSA_KV_HEADS, GROUP_ROWS, NSA_HEAD_DIM), F32),
                   jax.ShapeDtypeStruct((bsz, NSA_KV_HEADS, GROUP_ROWS, SLC_LANES), F32)],
        compiler_params=_cparams(("arbitrary", "arbitrary")),
        name="sample_cmp_attn",
    )(q8, kvc, kvc, bias_cs, jnp.asarray(ov, BF16))


def _sample_attn_kernel(rs_ref, q_ref, cache_hbm, snew_ref, bs_ref, win_ref, wnew_ref, bw_ref,
                        os_ref, ow_ref, kbuf, wbuf, sem, *, n_gather):
    b = pl.program_id(0)

    def block_copy(h, s):
        start = pl.multiple_of(rs_ref[(b * NSA_KV_HEADS + h) * n_gather + s], SLC_BLOCK)
        return pltpu.make_async_copy(cache_hbm.at[pl.ds(start, SLC_BLOCK)],
                                     kbuf.at[h, pl.ds(s * SLC_BLOCK, SLC_BLOCK)], sem)

    for h in range(NSA_KV_HEADS):
        for s in range(n_gather):
            block_copy(h, s).start()

    first_row = lax.broadcasted_iota(jnp.int32, (8, 2 * D_KV), 0) == 0
    tail = n_gather * SLC_BLOCK
    for h in range(NSA_KV_HEADS):
        kbuf[h, tail:tail + 8, :] = jnp.where(first_row, snew_ref[0], 0.0)
        kbuf[h, tail + 8:SEL_KEYS, :] = jnp.zeros((SEL_KEYS - tail - 8, 2 * D_KV), F32)
    wlen = win_ref.shape[1]
    wbuf[0:wlen, :] = win_ref[0]
    wbuf[wlen:wlen + 8, :] = jnp.where(first_row, wnew_ref[0], 0.0)
    wbuf[wlen + 8:WIN_KEYS, :] = jnp.zeros((WIN_KEYS - wlen - 8, 2 * D_KV), F32)

    def attend(q, k, v, bias, valid):
        s = _dot_nt(q, k.astype(BF16)) + bias
        s = jnp.where(valid, s, NEG_BIG)
        e = jnp.exp(s - jnp.max(s, axis=1, keepdims=True))
        p = jnp.where(valid, e / jnp.sum(e, axis=1, keepdims=True), 0.0)
        return _dot(p.astype(BF16), v.astype(BF16))

    widx = lax.broadcasted_iota(jnp.int32, (GROUP_ROWS, WIN_KEYS), 1)
    wvalid = (widx >= wlen + 1 - WINDOW) & (widx <= wlen)
    for h in range(NSA_KV_HEADS):
        kl = slice(h * NSA_HEAD_DIM, (h + 1) * NSA_HEAD_DIM)
        vl = slice(D_KV + h * NSA_HEAD_DIM, D_KV + (h + 1) * NSA_HEAD_DIM)
        q = (q_ref[0, h] * (NSA_HEAD_DIM ** -0.5)).astype(BF16)
        ow_ref[0, h] = attend(q, wbuf[:, kl], wbuf[:, vl], bw_ref[h], wvalid)

    for h in range(NSA_KV_HEADS):
        for s in range(n_gather):
            block_copy(h, s).wait()

    svalid = lax.broadcasted_iota(jnp.int32, (GROUP_ROWS, SEL_KEYS), 1) <= tail
    for h in range(NSA_KV_HEADS):
        kl = slice(h * NSA_HEAD_DIM, (h + 1) * NSA_HEAD_DIM)
        vl = slice(D_KV + h * NSA_HEAD_DIM, D_KV + (h + 1) * NSA_HEAD_DIM)
        q = (q_ref[0, h] * (NSA_HEAD_DIM ** -0.5)).astype(BF16)
        os_ref[0, h] = attend(q, kbuf[h, :, kl], kbuf[h, :, vl], bs_ref[0, h], svalid)


def _sample_attn(row_start, q8, cache2d, kvs_new, bias_sel, win, kvw_new, bias_w, n_gather):
    bsz = q8.shape[0]
    wlen = win.shape[1]
    qspec = pl.BlockSpec((1, NSA_KV_HEADS, GROUP_ROWS, NSA_HEAD_DIM), lambda b, rs: (b, 0, 0, 0))
    return pl.pallas_call(
        functools.partial(_sample_attn_kernel, n_gather=n_gather),
        grid_spec=pltpu.PrefetchScalarGridSpec(
            num_scalar_prefetch=1, grid=(bsz,),
            in_specs=[qspec,
                      pl.BlockSpec(memory_space=pl.ANY),
                      pl.BlockSpec((1, 1, 2 * D_KV), lambda b, rs: (b, 0, 0)),
                      pl.BlockSpec((1, NSA_KV_HEADS, GROUP_ROWS, SEL_KEYS), lambda b, rs: (b, 0, 0, 0)),
                      pl.BlockSpec((1, wlen, 2 * D_KV), lambda b, rs: (b, 0, 0)),
                      pl.BlockSpec((1, 1, 2 * D_KV), lambda b, rs: (b, 0, 0)),
                      pl.BlockSpec((NSA_KV_HEADS, GROUP_ROWS, WIN_KEYS), lambda b, rs: (0, 0, 0))],
            out_specs=[qspec, qspec],
            scratch_shapes=[pltpu.VMEM((NSA_KV_HEADS, SEL_KEYS, 2 * D_KV), F32),
                            pltpu.VMEM((WIN_KEYS, 2 * D_KV), F32),
                            pltpu.SemaphoreType.DMA(())]),
        out_shape=[jax.ShapeDtypeStruct(q8.shape, F32), jax.ShapeDtypeStruct(q8.shape, F32)],
        compiler_params=_cparams(("arbitrary",)),
        name="sample_slc_win_attn",
    )(row_start, q8, cache2d, kvs_new, bias_sel, win, kvw_new, bias_w)


PAGES_PER_STEP = 4


def _page_copy_kernel(pt_ref, *refs):
    o_ref = refs[-1]
    for k, x_ref in enumerate(refs[:-1]):
        o_ref[k * PAGE_SIZE:(k + 1) * PAGE_SIZE, :] = x_ref[...]


def _gather_pages(cache2d, page_table):
    db, n_pages = page_table.shape
    w = cache2d.shape[1]
    steps = n_pages // PAGES_PER_STEP
    page_spec = lambda k: pl.BlockSpec((PAGE_SIZE, w), lambda b, p, pt: (pt[b, p * PAGES_PER_STEP + k], 0))
    return pl.pallas_call(
        _page_copy_kernel,
        grid_spec=pltpu.PrefetchScalarGridSpec(
            num_scalar_prefetch=1, grid=(db, steps),
            in_specs=[page_spec(k) for k in range(PAGES_PER_STEP)],
            out_specs=pl.BlockSpec((PAGES_PER_STEP * PAGE_SIZE, w), lambda b, p, pt: (b * steps + p, 0))),
        out_shape=jax.ShapeDtypeStruct((db * n_pages * PAGE_SIZE, w), cache2d.dtype),
        compiler_params=_cparams(("arbitrary", "arbitrary")),
        name="gather_pages",
    )(page_table, *([cache2d] * PAGES_PER_STEP))


def _pad_rows(a, n):
    return jnp.concatenate([a, jnp.zeros((n - a.shape[0],) + a.shape[1:], a.dtype)], axis=0)


def _sample_mixer(x_sample, cache_cmp, cache_slc, cache_win, st_c, st_n, st_m, st_conv, page_table,
                  norm_g, wb, gate_bias, conv_w, ml_norm_g, cmp_pos, cmp_w1, cmp_w2, nsa_norm_g, w_out_b, rel_bias):
    db = x_sample.shape[0]
    n_pages = page_table.shape[1]
    past = n_pages * PAGE_SIZE
    tok = 16
    x16 = _pad_rows(x_sample.reshape(db, D_MODEL), tok)
    u = _proj(x16, norm_g, wb, tok, 512)
    small = u[:, COL_SMALL:COL_SMALL + LANE]

    T = 128
    useq = jnp.zeros((db, T, 4 * D_ML), F32)
    useq = useq.at[:, T - CONV_W:T - 1, 0:2 * D_ML].set(st_conv)
    useq = useq.at[:, T - 1, :].set(u[:db, 0:4 * D_ML])
    sseq = jnp.zeros((db, T, LANE), F32).at[:, :T - 1, LANE - 1].set(1.0)
    sseq = sseq.at[:, T - 1, :].set(small[:db])
    y_seq, conv_n, c_n, n_n, m_n = _mlstm(
        useq.reshape(db * T, 4 * D_ML), sseq.reshape(db * T, LANE), jnp.zeros((db, CONV_W - 1, 2 * D_ML), F32),
        st_c, st_n, st_m, conv_w, gate_bias, ml_norm_g, db, T, T)
    y_ml = y_seq.reshape(db, T, D_ML)[:, T - 1]

    n_pool = cache_cmp.shape[0]
    ctx = _gather_pages(cache_cmp.reshape(n_pool * PAGE_SIZE, 2 * D_KV), page_table)
    kvc = _compress(ctx, 0, db, past, cmp_pos, cmp_w1, cmp_w2, 4096)
    nch = past // CMP_STRIDE
    n_cmp = (past + 1) // CMP_STRIDE - CMP_BLOCK // CMP_STRIDE + 1
    n_slc = -(-(past + 1) // SLC_BLOCK)
    q = u[:db, COL_QNSA:COL_QNSA + D_NSA].reshape(db, NSA_KV_HEADS, NSA_GROUP, NSA_HEAD_DIM)
    q8 = jnp.concatenate([q, jnp.zeros_like(q)], axis=2)
    bd = _bias_by_distance(rel_bias, past + 1).reshape(NSA_KV_HEADS, NSA_GROUP, past + 1)
    pad_g = lambda a: jnp.concatenate([a, jnp.zeros_like(a)], axis=1)
    dist_c = np.clip(past - (np.arange(nch) * CMP_STRIDE + CMP_BLOCK - 1), 0, None)
    o_c8, sel = _sample_cattn(q8, kvc, pad_g(bd[:, :, dist_c]), n_cmp, n_slc)

    mask = sel[:, :, 0, :n_slc] > 0.5
    idx = jnp.sort(jnp.where(mask, jnp.arange(n_slc, dtype=jnp.int32), jnp.int32(1 << 20)), axis=-1)
    n_gather = SLC_TOPN - 1
    idx = idx[..., :n_gather]
    pages = jnp.take_along_axis(page_table[:, None, :], idx // 2, axis=2)
    row_start = (pages * PAGE_SIZE + (idx % 2) * SLC_BLOCK).reshape(-1).astype(jnp.int32)
    kpos = (idx[..., None] * SLC_BLOCK + jnp.arange(SLC_BLOCK, dtype=jnp.int32)).reshape(db, NSA_KV_HEADS, -1)
    hh = jnp.arange(NSA_KV_HEADS)[None, :, None, None]
    gg = jnp.arange(NSA_GROUP)[None, None, :, None]
    bias_sel = bd[hh, gg, (past - kpos)[:, :, None, :]]
    bias_sel = jnp.concatenate([bias_sel, jnp.broadcast_to(bd[None, :, :, 0:1], (db, NSA_KV_HEADS, NSA_GROUP, 1)),
                                jnp.zeros((db, NSA_KV_HEADS, NSA_GROUP, SEL_KEYS - n_gather * SLC_BLOCK - 1), F32)],
                               axis=-1)
    bias_sel = jnp.concatenate([bias_sel, jnp.zeros_like(bias_sel)], axis=2)
    wlen = cache_win.shape[1]
    dist_w = np.clip(wlen - np.arange(WIN_KEYS), 0, None)
    kvs_new = u[:db, COL_KVS:COL_KVS + 2 * D_KV]
    kvw_new = u[:db, COL_KVW:COL_KVW + 2 * D_KV]
    win2d = cache_win.reshape(db, wlen, 2 * D_KV)
    o_s8, o_w8 = _sample_attn(row_start, q8, cache_slc.reshape(n_pool * PAGE_SIZE, 2 * D_KV), kvs_new[:, None, :],
                              bias_sel, win2d, kvw_new[:, None, :], pad_g(bd[:, :, dist_w]), n_gather)

    heads = lambda o: _pad_rows(o[:, :, :NSA_GROUP, :].reshape(db, D_NSA), tok)
    y = _outproj(_pad_rows(y_ml, tok), heads(o_c8), heads(o_s8), heads(o_w8), small, nsa_norm_g, w_out_b, x16, tok, 512)
    kvshape = (1, db, 1, 2, NSA_KV_HEADS, NSA_HEAD_DIM)
    new_win = jnp.concatenate([win2d[:, 1:], kvw_new[:, None, :]], axis=1)
    states = (u[:db, COL_KVC:COL_KVC + 2 * D_KV].reshape(kvshape), kvs_new.reshape(kvshape),
              new_win.reshape((1, db, wlen, 2, NSA_KV_HEADS, NSA_HEAD_DIM)),
              c_n[None], n_n[None], m_n[None], conv_n[None])
    return y, states


def kernel(x_prompt, x_sample, cache_cmp_kv, cache_slc_kv, cache_win_kv, state_mlstm_C, state_mlstm_n,
           state_mlstm_m, state_conv, page_table, rel_bias, norm_mix_g, w_in, b_ig, b_fg, conv_w, ml_norm_g,
           cmp_pos, cmp_w1, cmp_w2, nsa_norm_g, w_out, norm_ffn_g, w_router_grp, b_router_grp, w_router_exp,
           b_router_exp, w_gate, w_up, w_down, norm_final_g):
    B, S, D = x_prompt.shape
    wb = _reorder_w_in(w_in[0])
    gate_bias = jnp.zeros((1, LANE), F32).at[0, 0:ML_HEADS].set(b_ig[0]).at[0, ML_HEADS:2 * ML_HEADS].set(b_fg[0])
    w_out_b = w_out[0].astype(BF16)
    yp, st_p = _prompt_mixer(x_prompt, norm_mix_g, wb, gate_bias, conv_w[0], ml_norm_g, cmp_pos[0], cmp_w1[0],
                             cmp_w2[0], nsa_norm_g, w_out_b, rel_bias)
    ys, st_s = _sample_mixer(x_sample, cache_cmp_kv[0], cache_slc_kv[0], cache_win_kv[0], state_mlstm_C[0],
                             state_mlstm_n[0], state_mlstm_m[0], state_conv[0], page_table, norm_mix_g, wb, gate_bias,
                             conv_w[0], ml_norm_g, cmp_pos[0], cmp_w1[0], cmp_w2[0], nsa_norm_g, w_out_b, rel_bias)
    moe_w = (norm_ffn_g, w_router_grp[0], b_router_grp[0], w_router_exp[0], b_router_exp[0],
             w_gate[0], w_up[0], w_down[0], norm_final_g[None, :])
    DB, L, _ = x_sample.shape
    out_p = _moe_final(yp.reshape(B * S, D), *moe_w, 256, 128).reshape(B, S, D)
    out_s = _moe_final(_pad_rows(ys, 128), *moe_w, 128, 16)[:DB].reshape(DB, L, D)
    outs = [out_p, out_s]
    for a, b in zip(st_p, st_s):
        outs += [a, b]
    return tuple(outs)


def _prompt_mixer(x_prompt, norm_g, wb, gate_bias, conv_w, ml_norm_g, cmp_pos, cmp_w1, cmp_w2, nsa_norm_g,
                  w_out_b, rel_bias):
    B, S, D = x_prompt.shape
    x2d = x_prompt.reshape(B * S, D)
    tm = min(1024, B * S)
    u = _proj(x2d, norm_g, wb, tm, 512)
    small = u[:, COL_SMALL:COL_SMALL + LANE]
    y_ml, conv_n, c_n, n_n, m_n = _mlstm(
        u, small, jnp.zeros((B, CONV_W - 1, 2 * D_ML), F32),
        jnp.zeros((B, ML_HEADS, ML_HEAD_DIM, ML_HEAD_DIM), F32), jnp.zeros((B, ML_HEADS, ML_HEAD_DIM), F32),
        jnp.full((B, ML_HEADS), -jnp.inf, F32), conv_w, gate_bias, ml_norm_g, B, S, 256)
    kv_c = u[:, COL_KVC:COL_KVC + 2 * D_KV]
    kv_s = u[:, COL_KVS:COL_KVS + 2 * D_KV]
    kv_w = u[:, COL_KVW:COL_KVW + 2 * D_KV]
    nch = S // CMP_STRIDE
    kvc = _compress(u, COL_KVC // (2 * D_KV), B, S, cmp_pos, cmp_w1, cmp_w2, min(S, 4096))
    tq_c = min(S, 256)
    o_c, sel_t = _cattn(u, kvc, _cmp_bias_table(rel_bias, S, nch, tq_c), B, S, tq_c)
    band = _band_bias(rel_bias, TQ, S // TQ)
    k_s, vt_s = _kv_layouts(kv_s, B, S)
    k_w, vt_w = _kv_layouts(kv_w, B, S)
    o_s = _flash(u, k_s, vt_s, band, sel_t, B, S)
    o_w = _flash(u, k_w, vt_w, band, None, B, S)
    y = _outproj(y_ml, o_c, o_s, o_w, small, nsa_norm_g, w_out_b, x2d, tm, 512)
    kvshape = (1, B, S, 2, NSA_KV_HEADS, NSA_HEAD_DIM)
    win = min(WINDOW, S)
    states = (kv_c.reshape(kvshape), kv_s.reshape(kvshape), kv_w.reshape(kvshape)[:, :, S - win:],
              c_n[None], n_n[None], m_n[None], conv_n[None])
    return y.reshape(B, S, D), states
```

```python
import functools
import math

import numpy as np
import jax
import jax.numpy as jnp
from jax import lax
from jax.experimental import pallas as pl
from jax.experimental.pallas import tpu as pltpu

F32 = jnp.float32
BF16 = jnp.bfloat16

D_MODEL = 2048
ML_HEADS = 4
ML_HEAD_DIM = 256
D_ML = 1024
CONV_W = 4
NSA_HEADS = 16
NSA_HEAD_DIM = 64
D_NSA = 1024
NSA_KV_HEADS = 4
NSA_GROUP = 4
D_KV = 256
CMP_BLOCK = 32
CMP_STRIDE = 16
CMP_HIDDEN = 256
SLC_BLOCK = 64
SLC_TOPN = 16
WINDOW = 512
N_BUCKETS = 32
MAX_DISTANCE = 2048
N_GROUPS = 4
EXPERTS_PER_GROUP = 8
N_EXPERTS = 32
D_EXPERT = 512
PAGE_SIZE = 128
EPS = 1e-6
NEG_BIG = -1e30
FORCE_SCORE = 1e4

LANE = 128
COL_QML, COL_KML, COL_VML, COL_OML = 0, 1024, 2048, 3072
COL_QNSA = 4096
COL_KVC, COL_KVS, COL_KVW = 5120, 5632, 6144
COL_SMALL = 6656
N_PROJ = 7168
VMEM_LIMIT = 56 * 1024 * 1024


def _cparams(sem, vmem=VMEM_LIMIT):
    return pltpu.CompilerParams(dimension_semantics=sem, vmem_limit_bytes=vmem)


def _split2(x):
    hi = x.astype(BF16)
    lo = (x - hi.astype(F32)).astype(BF16)
    return hi, lo


def _split3(x):
    hi = x.astype(BF16)
    r = x - hi.astype(F32)
    mid = r.astype(BF16)
    lo = (r - mid.astype(F32)).astype(BF16)
    return hi, mid, lo


def _dot(a, b):
    return jnp.dot(a, b, preferred_element_type=F32)


def _dot_nt(a, b):
    return lax.dot_general(a, b, (((1,), (1,)), ((), ())), preferred_element_type=F32)


def _dot_tn(a, b):
    return lax.dot_general(a, b, (((0,), (0,)), ((), ())), preferred_element_type=F32)


def _proj_kernel(x_ref, g_ref, w_ref, o_ref, h_scr):
    @pl.when(pl.program_id(1) == 0)
    def _():
        x = x_ref[...]
        ms = jnp.mean(x * x, axis=-1, keepdims=True)
        h_scr[...] = (x * lax.rsqrt(ms + EPS) * g_ref[...]).astype(BF16)

    o_ref[...] = _dot(h_scr[...], w_ref[...])


def _proj(x2d, g, wb, tm, tn):
    n, d = x2d.shape
    nc = wb.shape[1]
    return pl.pallas_call(
        _proj_kernel,
        grid=(n // tm, nc // tn),
        in_specs=[pl.BlockSpec((tm, d), lambda i, j: (i, 0)),
                  pl.BlockSpec((1, d), lambda i, j: (0, 0)),
                  pl.BlockSpec((d, tn), lambda i, j: (0, j))],
        out_specs=pl.BlockSpec((tm, tn), lambda i, j: (i, j)),
        out_shape=jax.ShapeDtypeStruct((n, nc), F32),
        scratch_shapes=[pltpu.VMEM((tm, d), BF16)],
        compiler_params=_cparams(("arbitrary", "arbitrary")),
        name="proj",
    )(x2d, g, wb)


def _reorder_w_in(w_in):
    big = w_in[:, :4 * D_ML]
    small_a = w_in[:, 4 * D_ML:4 * D_ML + 2 * ML_HEADS]
    rest = w_in[:, 4 * D_ML + 2 * ML_HEADS:]
    nsa = rest[:, :D_NSA + 6 * D_KV]
    gate = rest[:, D_NSA + 6 * D_KV:]
    pad = jnp.zeros((w_in.shape[0], N_PROJ - COL_SMALL - 2 * ML_HEADS - 3 * NSA_HEADS), w_in.dtype)
    return jnp.concatenate([big, nsa, small_a, gate, pad], axis=1).astype(BF16)


def _log_sigmoid(x):
    return jnp.minimum(x, 0.0) - jnp.log1p(jnp.exp(-jnp.abs(x)))


def _mlstm_kernel(q_ref, k_ref, v_ref, o_ref, s_ref, cb_ref, c0_ref, n0_ref, m0_ref,
                  cw_ref, gb_ref, ng_ref,
                  y_ref, cbo_ref, co_ref, no_ref, mo_ref,
                  ext_scr, c_scr, n_scr, m_scr, *, T):
    c = pl.program_id(1)
    nc = pl.num_programs(1)

    @pl.when(c == 0)
    def _():
        ext_scr[0:8, :] = jnp.zeros((8, 2 * D_ML), F32)
        ext_scr[5:8, :] = cb_ref[0]
        c_scr[...] = c0_ref[0]
        n_scr[...] = n0_ref[0]
        m_scr[...] = m0_ref[0]

    ext_scr[8:8 + T, 0:D_ML] = q_ref[...]
    ext_scr[8:8 + T, D_ML:2 * D_ML] = k_ref[...]
    conv = ext_scr[5:5 + T, :] * cw_ref[0:1, :]
    for j in range(1, CONV_W):
        conv = conv + ext_scr[5 + j:5 + j + T, :] * cw_ref[j:j + 1, :]
    tail = ext_scr[8 + T - 3:8 + T, :]
    ext_scr[5:8, :] = tail
    cbo_ref[0] = tail
    qk = conv * jax.nn.sigmoid(conv)

    pre = s_ref[...] + gb_ref[...]
    col = lax.broadcasted_iota(jnp.int32, pre.shape, 1)
    padrow = s_ref[:, LANE - 1:LANE] > 0.5
    gates = jnp.where(col < ML_HEADS, pre, _log_sigmoid(pre))
    gates = jnp.where(padrow, jnp.where(col < ML_HEADS, NEG_BIG, 0.0), gates)
    g_r = gates.T
    ti = lax.broadcasted_iota(jnp.int32, (T, T), 0)
    si = lax.broadcasted_iota(jnp.int32, (T, T), 1)
    upper = (ti <= si).astype(BF16)
    g_fin = jnp.where(lax.broadcasted_iota(jnp.int32, g_r.shape, 0) < ML_HEADS, 0.0, g_r)
    hi, mid, lo = _split3(g_fin)
    cum_r = _dot(hi, upper) + _dot(mid, upper) + _dot(lo, upper)
    rowi = lax.broadcasted_iota(jnp.int32, g_r.shape, 0)
    a_r = jnp.where(rowi < ML_HEADS, g_r, cum_r)
    a_c = a_r.T
    causal = si <= ti

    for h in range(ML_HEADS):
        sl = slice(h * ML_HEAD_DIM, (h + 1) * ML_HEAD_DIM)
        q = qk[:, h * ML_HEAD_DIM:(h + 1) * ML_HEAD_DIM]
        k = qk[:, D_ML + h * ML_HEAD_DIM:D_ML + (h + 1) * ML_HEAD_DIM] * (ML_HEAD_DIM ** -0.5)
        v = v_ref[:, sl]
        ig_r = a_r[h:h + 1, :]
        b_r = a_r[ML_HEADS + h:ML_HEADS + h + 1, :]
        ig_c = a_c[:, h:h + 1]
        b_c = a_c[:, ML_HEADS + h:ML_HEADS + h + 1]
        m_prev = m_scr[h:h + 1, 0:1]
        logd = jnp.where(causal, b_c - b_r + ig_r, -jnp.inf)
        inter = b_c + m_prev
        m_t = jnp.maximum(inter, jnp.max(logd, axis=1, keepdims=True))
        w_intra = jnp.exp(logd - m_t)
        w_inter = jnp.exp(inter - m_t)
        qb = q.astype(BF16)
        kb = k.astype(BF16)
        vb = v.astype(BF16)
        sc = _dot_nt(qb, kb) * w_intra
        cmat = c_scr[h]
        nvec = n_scr[h:h + 1, :]
        num = _dot(sc.astype(BF16), vb) + w_inter * _dot(qb, cmat.astype(BF16))
        qn = jnp.sum(qb.astype(F32) * nvec.astype(BF16).astype(F32), axis=1, keepdims=True)
        den = jnp.sum(sc, axis=1, keepdims=True) + w_inter * qn
        hh = num / jnp.maximum(jnp.abs(den), jnp.exp(-m_t))
        m_new = m_t[T - 1:T, :]
        b_last = b_c[T - 1:T, :]
        w_s = jnp.exp(b_last - b_c + ig_c - m_new)
        decay = jnp.exp(b_last + m_prev - m_new)
        kw = k * w_s
        c_new = decay * cmat + _dot_tn(kw.astype(BF16), vb)
        n_new = decay * nvec + jnp.sum(kw, axis=0, keepdims=True)
        c_scr[h] = c_new
        n_scr[h:h + 1, :] = n_new
        m_scr[h:h + 1, :] = jnp.broadcast_to(m_new, (1, LANE))
        hn = hh * lax.rsqrt(jnp.mean(hh * hh, axis=1, keepdims=True) + EPS) * ng_ref[:, sl]
        y_ref[:, sl] = (hn * jax.nn.sigmoid(o_ref[:, sl])).astype(y_ref.dtype)

    @pl.when(c == nc - 1)
    def _():
        co_ref[0] = c_scr[...]
        no_ref[0] = n_scr[...]
        mo_ref[0] = m_scr[...]


def _mlstm(u, small, conv_buf, c0, n0, m0, conv_w, gate_bias, norm_g, batch, seq, T):
    nc = seq // T
    cb = D_ML // 1024
    m0b = jnp.broadcast_to(m0[:, :, None], (batch, ML_HEADS, LANE))
    m0b = jnp.concatenate([m0b, jnp.zeros((batch, 8 - ML_HEADS, LANE), F32)], axis=1)
    n0p = jnp.concatenate([n0, jnp.zeros((batch, 8 - ML_HEADS, ML_HEAD_DIM), F32)], axis=1)
    row = lambda b, c: (b * nc + c, 0)
    outs = pl.pallas_call(
        functools.partial(_mlstm_kernel, T=T),
        grid=(batch, nc),
        in_specs=[pl.BlockSpec((T, D_ML), lambda b, c: (b * nc + c, COL_QML // D_ML)),
                  pl.BlockSpec((T, D_ML), lambda b, c: (b * nc + c, COL_KML // D_ML)),
                  pl.BlockSpec((T, D_ML), lambda b, c: (b * nc + c, COL_VML // D_ML)),
                  pl.BlockSpec((T, D_ML), lambda b, c: (b * nc + c, COL_OML // D_ML)),
                  pl.BlockSpec((T, LANE), row),
                  pl.BlockSpec((1, CONV_W - 1, 2 * D_ML), lambda b, c: (b, 0, 0)),
                  pl.BlockSpec((1, ML_HEADS, ML_HEAD_DIM, ML_HEAD_DIM), lambda b, c: (b, 0, 0, 0)),
                  pl.BlockSpec((1, 8, ML_HEAD_DIM), lambda b, c: (b, 0, 0)),
                  pl.BlockSpec((1, 8, LANE), lambda b, c: (b, 0, 0)),
                  pl.BlockSpec((CONV_W, 2 * D_ML), lambda b, c: (0, 0)),
                  pl.BlockSpec((1, LANE), lambda b, c: (0, 0)),
                  pl.BlockSpec((1, D_ML), lambda b, c: (0, 0))],
        out_specs=[pl.BlockSpec((T, D_ML), row),
                   pl.BlockSpec((1, CONV_W - 1, 2 * D_ML), lambda b, c: (b, 0, 0)),
                   pl.BlockSpec((1, ML_HEADS, ML_HEAD_DIM, ML_HEAD_DIM), lambda b, c: (b, 0, 0, 0)),
                   pl.BlockSpec((1, 8, ML_HEAD_DIM), lambda b, c: (b, 0, 0)),
                   pl.BlockSpec((1, 8, LANE), lambda b, c: (b, 0, 0))],
        out_shape=[jax.ShapeDtypeStruct((batch * seq, D_ML), BF16),
                   jax.ShapeDtypeStruct((batch, CONV_W - 1, 2 * D_ML), F32),
                   jax.ShapeDtypeStruct((batch, ML_HEADS, ML_HEAD_DIM, ML_HEAD_DIM), F32),
                   jax.ShapeDtypeStruct((batch, 8, ML_HEAD_DIM), F32),
                   jax.ShapeDtypeStruct((batch, 8, LANE), F32)],
        scratch_shapes=[pltpu.VMEM((8 + T, 2 * D_ML), F32),
                        pltpu.VMEM((ML_HEADS, ML_HEAD_DIM, ML_HEAD_DIM), F32),
                        pltpu.VMEM((8, ML_HEAD_DIM), F32),
                        pltpu.VMEM((8, LANE), F32)],
        compiler_params=_cparams(("arbitrary", "arbitrary")),
        name="mlstm",
    )(u, u, u, u, small, conv_buf, c0, n0p, m0b, conv_w, gate_bias, norm_g)
    y, cbo, co, no, mo = outs
    return y, cbo, co, no[:, :ML_HEADS], mo[:, :ML_HEADS, 0]


def _bucket_np(dist):
    n = np.maximum(dist, 0)
    max_exact = N_BUCKETS // 2
    nf = np.maximum(n, 1).astype(np.float64)
    large = max_exact + (np.log(nf / max_exact) / math.log(MAX_DISTANCE / max_exact)
                         * (N_BUCKETS - max_exact)).astype(np.int64)
    return np.where(n < max_exact, n, np.minimum(large, N_BUCKETS - 1)).astype(np.int32)


def _bias_by_distance(rel_bias, n):
    return rel_bias.astype(F32)[_bucket_np(np.arange(n))].T


def _band_bias(rel_bias, tile, n_delta):
    lc = n_delta * tile
    a = _bias_by_distance(rel_bias, lc + 1)
    y = jnp.tile(a, (1, tile))[:, :tile * lc].reshape(NSA_HEADS, tile, lc)
    y = y.reshape(NSA_KV_HEADS, NSA_GROUP, tile, n_delta, tile).transpose(0, 3, 2, 1, 4)
    return y.reshape(NSA_KV_HEADS, n_delta, tile, NSA_GROUP * tile)


def _cmp_bias_table(rel_bias, seq, nch, tq):
    t = np.arange(tq)[:, None]
    m = np.arange(2 * nch)[None, :]
    dist = t - CMP_STRIDE * (m - (nch - 1)) - (CMP_BLOCK - 1)
    idx = np.clip(dist, 0, seq - 1).astype(np.int32)
    a = _bias_by_distance(rel_bias, seq)
    return jnp.take(a, jnp.asarray(idx.reshape(-1)), axis=1).reshape(NSA_HEADS, tq, 2 * nch)


def _overlap_t(n_cmp, nch, n_slc):
    c0 = np.arange(nch) * CMP_STRIDE
    s0 = np.arange(n_slc) * SLC_BLOCK
    ov = np.minimum(c0[None, :] + CMP_BLOCK, s0[:, None] + SLC_BLOCK) - np.maximum(c0[None, :], s0[:, None])
    ov = np.clip(ov, 0, None).astype(np.float32) / CMP_BLOCK
    ov[:, n_cmp:] = 0.0
    return jnp.asarray(ov, BF16)


def _pq_kernel(x0_ref, x1_ref, x2_ref, x3_ref, pos_ref, w_ref, o_ref, *, rows):
    for pair, x_ref in enumerate((x0_ref, x1_ref, x2_ref, x3_ref)):
        toks = [x_ref[pl.ds(l, rows, stride=CMP_STRIDE), :] for l in range(CMP_STRIDE)]
        for half in range(2):
            ch = 2 * pair + half
            c = ch // NSA_KV_HEADS
            lanes = slice(half * NSA_HEAD_DIM, (half + 1) * NSA_HEAD_DIM)
            chunk = jnp.concatenate([t[:, lanes] for t in toks], axis=1)
            for part in range(2):
                lhs = (chunk + pos_ref[c, part:part + 1, :]).astype(BF16)
                o_ref[0, ch, :, part * CMP_HIDDEN:(part + 1) * CMP_HIDDEN] = _dot(lhs, w_ref[c, part])


def _hid_kernel(pq_ref, w2_ref, o_ref, *, nch):
    p = pq_ref[0, 0, :, 0:CMP_HIDDEN]
    q = pltpu.roll(pq_ref[0, 0, :, CMP_HIDDEN:2 * CMP_HIDDEN], nch - 1, 0)
    hid = jax.nn.gelu(p + q, approximate=True)
    o_ref[0, 0] = _dot(hid.astype(BF16), w2_ref[0])


def _compress(kv2d, col_block, bsz, seq, cmp_pos, cmp_w1, cmp_w2, tt):
    nch = seq // CMP_STRIDE
    kdim = CMP_STRIDE * NSA_HEAD_DIM
    nt = seq // tt
    tn = tt // CMP_STRIDE
    w1 = cmp_w1.reshape(2, 2, kdim, CMP_HIDDEN).astype(BF16)
    pos = cmp_pos.reshape(2, 2, kdim)
    pq = pl.pallas_call(
        functools.partial(_pq_kernel, rows=tn),
        grid=(bsz, nt),
        in_specs=[pl.BlockSpec((tt, LANE), functools.partial(lambda b, i, k: (b * nt + i, col_block * 4 + k), k=k))
                  for k in range(4)] +
                 [pl.BlockSpec((2, 2, kdim), lambda b, i: (0, 0, 0)),
                  pl.BlockSpec((2, 2, kdim, CMP_HIDDEN), lambda b, i: (0, 0, 0, 0))],
        out_specs=pl.BlockSpec((1, 8, tn, 2 * CMP_HIDDEN), lambda b, i: (b, 0, i, 0)),
        out_shape=jax.ShapeDtypeStruct((bsz, 8, nch, 2 * CMP_HIDDEN), F32),
        compiler_params=_cparams(("arbitrary", "arbitrary")),
        name="cmp_pq",
    )(kv2d, kv2d, kv2d, kv2d, pos, w1)
    return _cmp_hid(pq, cmp_w2)


def _cmp_hid(pq, cmp_w2):
    bsz, _, nch, _ = pq.shape
    return pl.pallas_call(
        functools.partial(_hid_kernel, nch=nch),
        grid=(bsz, 8),
        in_specs=[pl.BlockSpec((1, 1, nch, 2 * CMP_HIDDEN), lambda b, c: (b, c, 0, 0)),
                  pl.BlockSpec((1, CMP_HIDDEN, NSA_HEAD_DIM), lambda b, c: (c // NSA_KV_HEADS, 0, 0))],
        out_specs=pl.BlockSpec((1, 1, nch, NSA_HEAD_DIM), lambda b, c: (b, c, 0, 0)),
        out_shape=jax.ShapeDtypeStruct((bsz, 8, nch, NSA_HEAD_DIM), F32),
        compiler_params=_cparams(("arbitrary", "arbitrary")),
        name="cmp_hid",
    )(pq, cmp_w2.astype(BF16))


def _top_n_mask(score_t, blk, n_rows):
    rank = jnp.zeros(score_t.shape, jnp.int32)
    for i in range(n_rows):
        row = score_t[i:i + 1, :]
        beats = (row > score_t) | ((row == score_t) & (blk > i))
        rank = rank + beats.astype(jnp.int32)
    return (rank < min(SLC_TOPN, n_rows)).astype(F32)


def _cattn_kernel(q_ref, kc_ref, vc_ref, b_ref, ovt_ref, o_ref, sel_ref, *, tq, nch, n_cmp, n_slc):
    i = pl.program_id(2)
    kc = kc_ref[0, 0].astype(BF16)
    vc = vc_ref[0, 0].astype(BF16)
    t = i * tq + lax.broadcasted_iota(jnp.int32, (tq, nch), 0)
    n = lax.broadcasted_iota(jnp.int32, (tq, nch), 1)
    valid = (t - CMP_STRIDE * n - (CMP_BLOCK - 1) >= 0) & (n < n_cmp)
    pc = jnp.zeros((tq, nch), F32)
    shift = (nch - 1) - i * (tq // CMP_STRIDE)
    amount = lax.rem(2 * nch - shift, 2 * nch)
    for g in range(NSA_GROUP):
        sl = slice(g * NSA_HEAD_DIM, (g + 1) * NSA_HEAD_DIM)
        qg = (q_ref[:, sl] * (NSA_HEAD_DIM ** -0.5)).astype(BF16)
        s = _dot_nt(qg, kc) + pltpu.roll(b_ref[g], amount, 1)[:, 0:nch]
        s = jnp.where(valid, s, NEG_BIG)
        e = jnp.exp(s - jnp.max(s, axis=1, keepdims=True))
        p = jnp.where(valid, e / jnp.sum(e, axis=1, keepdims=True), 0.0)
        o_ref[:, sl] = _dot(p.astype(BF16), vc)
        pc = pc + p
    score_t = _dot_nt(ovt_ref[...], pc.astype(BF16))
    blk = lax.broadcasted_iota(jnp.int32, (n_slc, tq), 0)
    cur = (i * tq + lax.broadcasted_iota(jnp.int32, (n_slc, tq), 1)) // SLC_BLOCK
    forced = (blk == 0) | (blk == cur) | (blk == cur - 1)
    score_t = jnp.where(forced, FORCE_SCORE, score_t)
    score_t = jnp.where(blk > cur, -FORCE_SCORE, score_t)
    sel_ref[0, 0] = _top_n_mask(score_t, blk, n_slc)


def _cattn(u, kvc, biasc, batch, seq, tq):
    nch = kvc.shape[2]
    n_cmp = nch - 1
    n_slc = seq // SLC_BLOCK
    ni = seq // tq
    ovt = _overlap_t(n_cmp, nch, n_slc)
    qcol = COL_QNSA // (NSA_GROUP * NSA_HEAD_DIM)
    return pl.pallas_call(
        functools.partial(_cattn_kernel, tq=tq, nch=nch, n_cmp=n_cmp, n_slc=n_slc),
        grid=(batch, NSA_KV_HEADS, ni),
        in_specs=[pl.BlockSpec((tq, NSA_GROUP * NSA_HEAD_DIM), lambda b, h, i: (b * ni + i, qcol + h)),
                  pl.BlockSpec((1, 1, nch, NSA_HEAD_DIM), lambda b, h, i: (b, h, 0, 0)),
                  pl.BlockSpec((1, 1, nch, NSA_HEAD_DIM), lambda b, h, i: (b, NSA_KV_HEADS + h, 0, 0)),
                  pl.BlockSpec((NSA_GROUP, tq, 2 * nch), lambda b, h, i: (h, 0, 0)),
                  pl.BlockSpec((n_slc, nch), lambda b, h, i: (0, 0))],
        out_specs=[pl.BlockSpec((tq, NSA_GROUP * NSA_HEAD_DIM), lambda b, h, i: (b * ni + i, h)),
                   pl.BlockSpec((1, 1, n_slc, tq), lambda b, h, i: (b, h, 0, i))],
        out_shape=[jax.ShapeDtypeStruct((batch * seq, D_NSA), F32),
                   jax.ShapeDtypeStruct((batch, NSA_KV_HEADS, n_slc, seq), F32)],
        compiler_params=_cparams(("arbitrary", "arbitrary", "arbitrary")),
        name="cmp_attn",
    )(u, kvc, kvc, biasc, ovt)


TQ = 128
FLASH_GROUP = 4


def _flash_kernel(q_ref, k_ref, v_ref, b_ref, *rest, selected):
    if selected:
        sel_ref, o_ref, qs_scr, m_scr, l_scr, acc_scr = rest
    else:
        o_ref, qs_scr, m_scr, l_scr, acc_scr = rest
    i = pl.program_id(2)
    for g in range(NSA_GROUP):
        qs_scr[g * TQ:(g + 1) * TQ, :] = (
            q_ref[:, g * NSA_HEAD_DIM:(g + 1) * NSA_HEAD_DIM] * (NSA_HEAD_DIM ** -0.5)).astype(BF16)
    m_scr[...] = jnp.full(m_scr.shape, 0.5 * NEG_BIG, F32)
    l_scr[...] = jnp.zeros(l_scr.shape, F32)
    acc_scr[...] = jnp.zeros(acc_scr.shape, F32)
    key = lax.broadcasted_iota(jnp.int32, (TQ, TQ), 0)
    qry = lax.broadcasted_iota(jnp.int32, (TQ, TQ), 1)
    n_back = WINDOW // TQ

    def scores(j, kind):
        kj = k_ref[0, 0, pl.ds(pl.multiple_of(j * TQ, TQ), TQ), :].astype(BF16)
        s = _dot_nt(kj, qs_scr[...]) + b_ref[0, i - j]
        mask = None
        if kind == "diag":
            mask = qry >= key
        elif kind == "far":
            mask = qry < key
        if selected:
            r = sel_ref[0, 0, pl.ds(2 * j, 2), :]
            picked = jnp.where(key < SLC_BLOCK, r[0:1, :], r[1:2, :]) > 0.5
            mask = picked if mask is None else (mask & picked)
        if mask is not None:
            s = jnp.concatenate([jnp.where(mask, s[:, g * TQ:(g + 1) * TQ], NEG_BIG) for g in range(NSA_GROUP)], axis=1)
        vj = v_ref[0, 0, pl.ds(pl.multiple_of(j * TQ, TQ), TQ), :].astype(BF16)
        return s, vj

    def update(tiles):
        m_old = m_scr[...]
        m_new = m_old
        for s, _ in tiles:
            m_new = jnp.maximum(m_new, jnp.max(s, axis=0, keepdims=True))
        alpha = jnp.exp(m_old - m_new)
        l_new = alpha * l_scr[...]
        acc = alpha * acc_scr[...]
        for s, vj in tiles:
            p = jnp.exp(s - m_new)
            l_new = l_new + jnp.sum(p, axis=0, keepdims=True)
            acc = acc + _dot_tn(vj, p.astype(BF16))
        m_scr[...] = m_new
        l_scr[...] = l_new
        acc_scr[...] = acc

    def single(j, carry):
        update([scores(j, "full")])
        return carry

    if selected:
        def group(t, carry):
            update([scores(FLASH_GROUP * t + k, "full") for k in range(FLASH_GROUP)])
            return carry

        n_groups = i // FLASH_GROUP
        lax.fori_loop(0, n_groups, group, 0)
        lax.fori_loop(n_groups * FLASH_GROUP, i, single, 0)
        update([scores(i, "diag")])
    else:
        @pl.when(i >= n_back)
        def _():
            update([scores(i - n_back, "far")] + [scores(i - n_back + k, "full") for k in range(1, n_back)]
                   + [scores(i, "diag")])

        @pl.when(i < n_back)
        def _():
            lax.fori_loop(0, i, single, 0)
            update([scores(i, "diag")])

    o_ref[0, 0, 0] = acc_scr[...] / l_scr[...]


def _flash(u, k, v, band, sel_t, batch, seq):
    ni = seq // TQ
    qcol = COL_QNSA // (NSA_GROUP * NSA_HEAD_DIM)
    selected = sel_t is not None
    n_delta = band.shape[1]
    in_specs = [pl.BlockSpec((TQ, NSA_GROUP * NSA_HEAD_DIM), lambda b, h, i: (b * ni + i, qcol + h)),
                pl.BlockSpec((1, 1, seq, NSA_HEAD_DIM), lambda b, h, i: (b, h, 0, 0)),
                pl.BlockSpec((1, 1, seq, NSA_HEAD_DIM), lambda b, h, i: (b, h, 0, 0)),
                pl.BlockSpec((1, n_delta, TQ, NSA_GROUP * TQ), lambda b, h, i: (h, 0, 0, 0))]
    args = [u, k, v, band]
    if selected:
        n_slc = sel_t.shape[2]
        in_specs.append(pl.BlockSpec((1, 1, n_slc, TQ), lambda b, h, i: (b, h, 0, i)))
        args.append(sel_t)
    o_t = pl.pallas_call(
        functools.partial(_flash_kernel, selected=selected),
        grid=(batch, NSA_KV_HEADS, ni),
        in_specs=in_specs,
        out_specs=pl.BlockSpec((1, 1, 1, NSA_HEAD_DIM, NSA_GROUP * TQ), lambda b, h, i: (b, h, i, 0, 0)),
        out_shape=jax.ShapeDtypeStruct((batch, NSA_KV_HEADS, ni, NSA_HEAD_DIM, NSA_GROUP * TQ), F32),
        scratch_shapes=[pltpu.VMEM((NSA_GROUP * TQ, NSA_HEAD_DIM), BF16),
                        pltpu.VMEM((1, NSA_GROUP * TQ), F32),
                        pltpu.VMEM((1, NSA_GROUP * TQ), F32),
                        pltpu.VMEM((NSA_HEAD_DIM, NSA_GROUP * TQ), F32)],
        compiler_params=_cparams(("arbitrary", "arbitrary", "arbitrary")),
        name="slc_attn" if selected else "win_attn",
    )(*args)
    o = o_t.reshape(batch, NSA_KV_HEADS, ni, NSA_HEAD_DIM, NSA_GROUP, TQ).transpose(0, 2, 5, 1, 4, 3)
    return o.reshape(batch * seq, D_NSA)


def _kv_layouts(kv2d, batch, seq):
    kv = kv2d.reshape(batch, seq, 2, NSA_KV_HEADS, NSA_HEAD_DIM)
    return kv[:, :, 0].transpose(0, 2, 1, 3), kv[:, :, 1].transpose(0, 2, 1, 3)


def _gate_expand():
    e = np.zeros((3, LANE, D_NSA), np.float32)
    for br in range(3):
        for hd in range(NSA_HEADS):
            e[br, 2 * ML_HEADS + br * NSA_HEADS + hd, hd * NSA_HEAD_DIM:(hd + 1) * NSA_HEAD_DIM] = 1.0
    return jnp.asarray(e, BF16)


def _outproj_kernel(yml_ref, oc_ref, os_ref, ow_ref, s_ref, e_ref, ng_ref, w_ref, x_ref, o_ref, cat_scr):
    @pl.when(pl.program_id(1) == 0)
    def _():
        sig = jax.nn.sigmoid(s_ref[...])
        hi, lo = _split2(sig)
        o = jnp.zeros(oc_ref.shape, F32)
        for br, ref in enumerate((oc_ref, os_ref, ow_ref)):
            gexp = _dot(hi, e_ref[br]) + _dot(lo, e_ref[br])
            o = o + gexp * ref[...]
        o = o * lax.rsqrt(jnp.mean(o * o, axis=-1, keepdims=True) + EPS) * ng_ref[...]
        cat_scr[:, 0:D_ML] = yml_ref[...]
        cat_scr[:, D_ML:D_ML + D_NSA] = o.astype(BF16)

    o_ref[...] = x_ref[...] + _dot(cat_scr[...], w_ref[...])


def _outproj(y_ml, o_c, o_s, o_w, small, nsa_g, w_out_b, x2d, tm, tn):
    n, d = x2d.shape
    rows = lambda i, j: (i, 0)
    return pl.pallas_call(
        _outproj_kernel,
        grid=(n // tm, d // tn),
        in_specs=[pl.BlockSpec((tm, D_ML), rows), pl.BlockSpec((tm, D_NSA), rows),
                  pl.BlockSpec((tm, D_NSA), rows), pl.BlockSpec((tm, D_NSA), rows),
                  pl.BlockSpec((tm, LANE), rows),
                  pl.BlockSpec((3, LANE, D_NSA), lambda i, j: (0, 0, 0)),
                  pl.BlockSpec((1, D_NSA), lambda i, j: (0, 0)),
                  pl.BlockSpec((D_ML + D_NSA, tn), lambda i, j: (0, j)),
                  pl.BlockSpec((tm, tn), lambda i, j: (i, j))],
        out_specs=pl.BlockSpec((tm, tn), lambda i, j: (i, j)),
        out_shape=jax.ShapeDtypeStruct((n, d), F32),
        scratch_shapes=[pltpu.VMEM((tm, D_ML + D_NSA), BF16)],
        compiler_params=_cparams(("arbitrary", "arbitrary")),
        name="outproj",
    )(y_ml, o_c, o_s, o_w, small, _gate_expand(), nsa_g, w_out_b, x2d)


ROUTE_COL = N_GROUPS
BIG_COL = 1 << 20
SPLIT = D_MODEL // LANE


def _store_split(ref, val, n):
    for k in range(SPLIT):
        ref[pl.ds(k, n, stride=SPLIT), :] = val[:, k * LANE:(k + 1) * LANE]


def _load_split(ref, n):
    return jnp.concatenate([ref[pl.ds(k, n, stride=SPLIT), :] for k in range(SPLIT)], axis=1)


def _route_kernel(y_ref, g_ref, w_ref, b_ref, h_ref, info_ref, cnt_ref, carry_scr, *, tm):
    @pl.when(pl.program_id(0) == 0)
    def _():
        carry_scr[...] = jnp.zeros(carry_scr.shape, F32)

    x = y_ref[...]
    h = x * lax.rsqrt(jnp.mean(x * x, axis=-1, keepdims=True) + EPS) * g_ref[...]
    _store_split(h_ref, h, tm)
    logit = _dot(h.astype(BF16), w_ref[...]) + b_ref[...]
    col = lax.broadcasted_iota(jnp.int32, logit.shape, 1)
    is_grp = col < N_GROUPS
    gmax = jnp.max(jnp.where(is_grp, logit, -jnp.inf), axis=1, keepdims=True)
    gtop = jnp.min(jnp.where(is_grp & (logit == gmax), col, BIG_COL), axis=1, keepdims=True)
    gsum = jnp.sum(jnp.where(is_grp, jnp.exp(logit - gmax), 0.0), axis=1, keepdims=True)
    first = ROUTE_COL + gtop * EXPERTS_PER_GROUP
    in_grp = (col >= first) & (col < first + EXPERTS_PER_GROUP)
    v1 = jnp.max(jnp.where(in_grp, logit, -jnp.inf), axis=1, keepdims=True)
    i1 = jnp.min(jnp.where(in_grp & (logit == v1), col, BIG_COL), axis=1, keepdims=True)
    rest = in_grp & (col != i1)
    v2 = jnp.max(jnp.where(rest, logit, -jnp.inf), axis=1, keepdims=True)
    i2 = jnp.min(jnp.where(rest & (logit == v2), col, BIG_COL), axis=1, keepdims=True)
    e = jnp.exp(v2 - v1)
    w1 = 1.0 / ((1.0 + e) * gsum)
    w2 = e / ((1.0 + e) * gsum)
    pick1 = col == i1
    pick2 = col == i2
    both = (pick1 | pick2).astype(F32)
    ri = lax.broadcasted_iota(jnp.int32, (tm, tm), 0)
    ci = lax.broadcasted_iota(jnp.int32, (tm, tm), 1)
    before = (ci < ri).astype(BF16)
    cum = _dot(before, both.astype(BF16)) + carry_scr[...]
    r1 = jnp.sum(jnp.where(pick1, cum, 0.0), axis=1, keepdims=True)
    r2 = jnp.sum(jnp.where(pick2, cum, 0.0), axis=1, keepdims=True)
    carry_scr[...] = carry_scr[...] + jnp.sum(both, axis=0, keepdims=True)
    cnt_ref[...] = jnp.broadcast_to(carry_scr[...], cnt_ref.shape)
    info = jnp.where(col == 0, (i1 - ROUTE_COL).astype(F32), 0.0)
    info = jnp.where(col == 1, (i2 - ROUTE_COL).astype(F32), info)
    info = jnp.where(col == 2, w1, info)
    info = jnp.where(col == 3, w2, info)
    info = jnp.where(col == 4, r1, info)
    info_ref[...] = jnp.where(col == 5, r2, info)


def _route(y2d, g, w_rg, b_rg, w_re, b_re, tm):
    n, d = y2d.shape
    wr = jnp.concatenate([w_rg, w_re, jnp.zeros((d, LANE - N_GROUPS - N_EXPERTS), F32)], axis=1).astype(BF16)
    bias = jnp.concatenate([b_rg, b_re, jnp.zeros((LANE - N_GROUPS - N_EXPERTS,), F32)])[None, :]
    rows = lambda i: (i, 0)
    fixed = lambda i: (0, 0)
    return pl.pallas_call(
        functools.partial(_route_kernel, tm=tm),
        grid=(n // tm,),
        in_specs=[pl.BlockSpec((tm, d), rows), pl.BlockSpec((1, d), fixed),
                  pl.BlockSpec((d, LANE), fixed), pl.BlockSpec((1, LANE), fixed)],
        out_specs=[pl.BlockSpec((tm * SPLIT, LANE), rows), pl.BlockSpec((tm, LANE), rows),
                   pl.BlockSpec((8, LANE), fixed)],
        out_shape=[jax.ShapeDtypeStruct((n * SPLIT, LANE), F32), jax.ShapeDtypeStruct((n, LANE), F32),
                   jax.ShapeDtypeStruct((8, LANE), F32)],
        scratch_shapes=[pltpu.VMEM((1, LANE), F32)],
        compiler_params=_cparams(("arbitrary",)),
        name="moe_route",
    )(y2d, g, wr, bias)


def _expert_kernel(te_ref, nu_ref, src_ref, h_hbm, rw_ref, wg_ref, wu_ref, wd_ref, o_ref,
                   xbuf, wg_b, wu_b, wd_b, sem, *, tme):
    i = pl.program_id(0)
    slot = i % 2

    def row_copy(tile, r, buf_slot):
        src_row = src_ref[tile * tme + r]
        return pltpu.make_async_copy(h_hbm.at[pl.ds(pl.multiple_of(src_row * SPLIT, SPLIT), SPLIT)],
                                     xbuf.at[buf_slot, pl.ds(pl.multiple_of(r * SPLIT, SPLIT), SPLIT)],
                                     sem.at[buf_slot])

    def fetch(tile, buf_slot):
        def body(r, carry):
            row_copy(tile, r, buf_slot).start()
            return carry
        lax.fori_loop(0, tme, body, 0)

    @pl.when(i == 0)
    def _():
        fetch(0, 0)

    @pl.when(i + 1 < nu_ref[0])
    def _():
        fetch(i + 1, 1 - slot)

    @pl.when(i < nu_ref[0])
    def _():
        prev = te_ref[jnp.maximum(i - 1, 0)]

        @pl.when((i == 0) | (te_ref[i] != prev))
        def _():
            wg_b[...] = wg_ref[0].astype(BF16)
            wu_b[...] = wu_ref[0].astype(BF16)
            wd_b[...] = wd_ref[0].astype(BF16)

        def wait_body(r, carry):
            row_copy(i, r, slot).wait()
            return carry
        lax.fori_loop(0, tme, wait_body, 0)

        x = _load_split(xbuf.at[slot], tme).astype(BF16)
        a = _dot(x, wg_b[...])
        u = _dot(x, wu_b[...])
        hid = a * jax.nn.sigmoid(a) * u * rw_ref[...]
        _store_split(o_ref, _dot(hid.astype(BF16), wd_b[...]), tme)

    @pl.when(i >= nu_ref[0])
    def _():
        o_ref[...] = jnp.zeros(o_ref.shape, F32)


def _experts(h, src, roww, tile_expert, n_used, w_gate, w_up, w_down, tme):
    p = src.shape[0]
    d = D_MODEL
    return pl.pallas_call(
        functools.partial(_expert_kernel, tme=tme),
        grid_spec=pltpu.PrefetchScalarGridSpec(
            num_scalar_prefetch=3, grid=(p // tme,),
            in_specs=[pl.BlockSpec(memory_space=pl.ANY),
                      pl.BlockSpec((tme, 1), lambda i, te, nu, sr: (i, 0)),
                      pl.BlockSpec((1, d, D_EXPERT), lambda i, te, nu, sr: (te[i], 0, 0)),
                      pl.BlockSpec((1, d, D_EXPERT), lambda i, te, nu, sr: (te[i], 0, 0)),
                      pl.BlockSpec((1, D_EXPERT, d), lambda i, te, nu, sr: (te[i], 0, 0))],
            out_specs=pl.BlockSpec((tme * SPLIT, LANE), lambda i, te, nu, sr: (i, 0)),
            scratch_shapes=[pltpu.VMEM((2, tme * SPLIT, LANE), F32),
                            pltpu.VMEM((d, D_EXPERT), BF16), pltpu.VMEM((d, D_EXPERT), BF16),
                            pltpu.VMEM((D_EXPERT, d), BF16), pltpu.SemaphoreType.DMA((2,))]),
        out_shape=jax.ShapeDtypeStruct((p * SPLIT, LANE), F32),
        compiler_params=_cparams(("arbitrary",)),
        name="moe_experts",
    )(tile_expert, n_used, src, h, roww, w_gate, w_up, w_down)


def _combine_kernel(d1_ref, d2_ref, y_ref, ys_hbm, g_ref, o_ref, a_buf, b_buf, sem, *, tm):
    base = pl.program_id(0) * tm

    def copy(r, row, buf):
        return pltpu.make_async_copy(ys_hbm.at[pl.ds(pl.multiple_of(row * SPLIT, SPLIT), SPLIT)],
                                     buf.at[pl.ds(pl.multiple_of(r * SPLIT, SPLIT), SPLIT)], sem)

    def issue(r, carry):
        copy(r, d1_ref[base + r], a_buf).start()
        copy(r, d2_ref[base + r], b_buf).start()
        return carry

    lax.fori_loop(0, tm, issue, 0)

    def drain(r, carry):
        copy(r, 0, a_buf).wait()
        copy(r, 0, b_buf).wait()
        return carry

    lax.fori_loop(0, tm, drain, 0)
    y = y_ref[...] + _load_split(a_buf, tm) + _load_split(b_buf, tm)
    o_ref[...] = y * lax.rsqrt(jnp.mean(y * y, axis=-1, keepdims=True) + EPS) * g_ref[...]


def _combine(y2d, ys, d1, d2, g, tm):
    n, d = y2d.shape
    return pl.pallas_call(
        functools.partial(_combine_kernel, tm=tm),
        grid_spec=pltpu.PrefetchScalarGridSpec(
            num_scalar_prefetch=2, grid=(n // tm,),
            in_specs=[pl.BlockSpec((tm, d), lambda i, a, b: (i, 0)),
                      pl.BlockSpec(memory_space=pl.ANY),
                      pl.BlockSpec((1, d), lambda i, a, b: (0, 0))],
            out_specs=pl.BlockSpec((tm, d), lambda i, a, b: (i, 0)),
            scratch_shapes=[pltpu.VMEM((tm * SPLIT, LANE), F32), pltpu.VMEM((tm * SPLIT, LANE), F32),
                            pltpu.SemaphoreType.DMA(())]),
        out_shape=jax.ShapeDtypeStruct((n, d), F32),
        compiler_params=_cparams(("arbitrary",)),
        name="moe_combine",
    )(d1, d2, y2d, ys, g)


def _moe_final(y2d, ffn_g, w_rg, b_rg, w_re, b_re, w_gate, w_up, w_down, final_g, tm, tme):
    n, d = y2d.shape
    h, info, cnt = _route(y2d, ffn_g, w_rg, b_rg, w_re, b_re, tm)
    e1 = info[:, 0].astype(jnp.int32)
    e2 = info[:, 1].astype(jnp.int32)
    counts = cnt[0, ROUTE_COL:ROUTE_COL + N_EXPERTS].astype(jnp.int32)
    padded = (counts + tme - 1) // tme * tme
    ends = jnp.cumsum(padded)
    offs = ends - padded
    d1 = offs[e1] + info[:, 4].astype(jnp.int32)
    d2 = offs[e2] + info[:, 5].astype(jnp.int32)
    p = 2 * n + N_EXPERTS * tme
    tok = jnp.arange(n, dtype=jnp.int32)
    src = jnp.zeros((p,), jnp.int32).at[d1].set(tok).at[d2].set(tok)
    roww = jnp.zeros((p,), F32).at[d1].set(info[:, 2]).at[d2].set(info[:, 3])
    n_used = (ends[-1] // tme).astype(jnp.int32).reshape(1)
    tile_start = jnp.arange(p // tme, dtype=jnp.int32) * tme
    tile_expert = jnp.sum((ends[None, :] <= tile_start[:, None]).astype(jnp.int32), axis=1)
    last = jnp.sum((ends <= ends[-1] - 1).astype(jnp.int32))
    tile_expert = jnp.minimum(tile_expert, last).astype(jnp.int32)
    ys = _experts(h, src, roww[:, None], tile_expert, n_used, w_gate, w_up, w_down, tme)
    return _combine(y2d, ys, d1, d2, final_g, tm)


GROUP_ROWS = 8
SLC_LANES = 384
SEL_KEYS = 1024
WIN_KEYS = 640


def _sample_cattn_kernel(q_ref, kc_ref, vc_ref, b_ref, ov_ref, o_ref, sel_ref, *, n_cmp, n_slc):
    q = (q_ref[0, 0] * (NSA_HEAD_DIM ** -0.5)).astype(BF16)
    s = _dot_nt(q, kc_ref[0, 0].astype(BF16)) + b_ref[0]
    valid = lax.broadcasted_iota(jnp.int32, s.shape, 1) < n_cmp
    s = jnp.where(valid, s, NEG_BIG)
    e = jnp.exp(s - jnp.max(s, axis=1, keepdims=True))
    p = jnp.where(valid, e / jnp.sum(e, axis=1, keepdims=True), 0.0)
    o_ref[0, 0] = _dot(p.astype(BF16), vc_ref[0, 0].astype(BF16))
    pg = jnp.where(lax.broadcasted_iota(jnp.int32, p.shape, 0) < NSA_GROUP, p, 0.0)
    pc = jnp.broadcast_to(jnp.sum(pg, axis=0, keepdims=True), pg.shape)
    score = _dot(pc.astype(BF16), ov_ref[...])[0:1, :]
    blk = lax.broadcasted_iota(jnp.int32, score.shape, 1)
    cur = n_slc - 1
    forced = (blk == 0) | (blk == cur) | (blk == cur - 1)
    score = jnp.where(forced, FORCE_SCORE, score)
    score = jnp.where(blk > cur, -jnp.inf, score)
    ri = lax.broadcasted_iota(jnp.int32, (SLC_LANES, SLC_LANES), 0)
    ci = lax.broadcasted_iota(jnp.int32, (SLC_LANES, SLC_LANES), 1)
    sb = jnp.broadcast_to(score, (SLC_LANES, SLC_LANES))
    col = jnp.sum(jnp.where(ri == ci, sb, 0.0), axis=1, keepdims=True)
    beats = (col > sb) | ((col == sb) & (ri < ci))
    rank = jnp.sum(beats.astype(jnp.int32), axis=0, keepdims=True)
    sel_ref[0, 0] = jnp.broadcast_to((rank < SLC_TOPN).astype(F32), (GROUP_ROWS, SLC_LANES))


def _sample_cattn(q8, kvc, bias_cs, n_cmp, n_slc):
    bsz, _, nch, _ = kvc.shape
    c0 = np.arange(nch) * CMP_STRIDE
    s0 = np.arange(SLC_LANES) * SLC_BLOCK
    ov = np.minimum(c0[:, None] + CMP_BLOCK, s0[None, :] + SLC_BLOCK) - np.maximum(c0[:, None], s0[None, :])
    ov = np.clip(ov, 0, None).astype(np.float32) / CMP_BLOCK
    ov[n_cmp:, :] = 0.0
    ov[:, n_slc:] = 0.0
    return pl.pallas_call(
        functools.partial(_sample_cattn_kernel, n_cmp=n_cmp, n_slc=n_slc),
        grid=(bsz, NSA_KV_HEADS),
        in_specs=[pl.BlockSpec((1, 1, GROUP_ROWS, NSA_HEAD_DIM), lambda b, h: (b, h, 0, 0)),
                  pl.BlockSpec((1, 1, nch, NSA_HEAD_DIM), lambda b, h: (b, h, 0, 0)),
                  pl.BlockSpec((1, 1, nch, NSA_HEAD_DIM), lambda b, h: (b, NSA_KV_HEADS + h, 0, 0)),
                  pl.BlockSpec((1, GROUP_ROWS, nch), lambda b, h: (h, 0, 0)),
                  pl.BlockSpec((nch, SLC_LANES), lambda b, h: (0, 0))],
        out_specs=[pl.BlockSpec((1, 1, GROUP_ROWS, NSA_HEAD_DIM), lambda b, h: (b, h, 0, 0)),
                   pl.BlockSpec((1, 1, GROUP_ROWS, SLC_LANES), lambda b, h: (b, h, 0, 0))],
        out_shape=[jax.ShapeDtypeStruct((bsz, NSA_KV_HEADS, GROUP_ROWS, NSA_HEAD_DIM), F32),
                   jax.ShapeDtypeStruct((bsz, NSA_KV_HEADS, GROUP_ROWS, SLC_LANES), F32)],
        compiler_params=_cparams(("arbitrary", "arbitrary")),
        name="sample_cmp_attn",
    )(q8, kvc, kvc, bias_cs, jnp.asarray(ov, BF16))


def _sample_attn_kernel(pg_ref, hf_ref, q_ref, cache_hbm, snew_ref, bs_ref, win_ref, wnew_ref, bw_ref,
                        os_ref, ow_ref, kt_buf, vt_buf, sem, *, n_gather, wlen):
    b = pl.program_id(0)
    rows_per_page = 2 * NSA_KV_HEADS * NSA_HEAD_DIM

    def page_copy(h, s, c, buf):
        page = pg_ref[(b * NSA_KV_HEADS + h) * n_gather + s]
        start = pl.multiple_of(page * rows_per_page + (c * NSA_KV_HEADS + h) * NSA_HEAD_DIM, NSA_HEAD_DIM)
        return pltpu.make_async_copy(cache_hbm.at[pl.ds(start, NSA_HEAD_DIM)],
                                     buf.at[h, :, pl.ds(s * PAGE_SIZE, PAGE_SIZE)], sem)

    for h in range(NSA_KV_HEADS):
        for s in range(n_gather):
            page_copy(h, s, 0, kt_buf).start()
            page_copy(h, s, 1, vt_buf).start()

    def attend(q, kts, vts, bias, valid):
        s = jnp.concatenate([_dot(q, kt.astype(BF16)) for kt in kts], axis=1) + bias
        s = jnp.where(valid, s, NEG_BIG)
        e = jnp.exp(s - jnp.max(s, axis=1, keepdims=True))
        p = jnp.where(valid, e / jnp.sum(e, axis=1, keepdims=True), 0.0).astype(BF16)
        out = None
        lo = 0
        for vt in vts:
            n = vt.shape[1]
            part = _dot_nt(p[:, lo:lo + n], vt.astype(BF16))
            out = part if out is None else out + part
            lo += n
        return out

    def head_rows(ref, c, h):
        r0 = (c * NSA_KV_HEADS + h) * NSA_HEAD_DIM
        return ref[0, r0:r0 + NSA_HEAD_DIM, :]

    widx = lax.broadcasted_iota(jnp.int32, (GROUP_ROWS, wlen + PAGE_SIZE), 1)
    wvalid = (widx >= wlen + 1 - WINDOW) & (widx <= wlen)
    for h in range(NSA_KV_HEADS):
        q = (q_ref[0, h] * (NSA_HEAD_DIM ** -0.5)).astype(BF16)
        ow_ref[0, h] = attend(q, [head_rows(win_ref, 0, h), head_rows(wnew_ref, 0, h)],
                              [head_rows(win_ref, 1, h), head_rows(wnew_ref, 1, h)], bw_ref[h], wvalid)

    for h in range(NSA_KV_HEADS):
        for s in range(n_gather):
            page_copy(h, s, 0, kt_buf).wait()
            page_copy(h, s, 1, vt_buf).wait()

    keys = n_gather * PAGE_SIZE
    lane = lax.broadcasted_iota(jnp.int32, (GROUP_ROWS, keys + PAGE_SIZE), 1)
    slot = lane >> 7
    lane_half = (lane >> 6) & 1
    for h in range(NSA_KV_HEADS):
        want = jnp.full(lane.shape, -1, jnp.int32)
        for s in range(n_gather):
            want = jnp.where(slot == s, hf_ref[(b * NSA_KV_HEADS + h) * n_gather + s], want)
        svalid = (lane_half == want) | (lane == keys)
        q = (q_ref[0, h] * (NSA_HEAD_DIM ** -0.5)).astype(BF16)
        os_ref[0, h] = attend(q, [kt_buf[h], head_rows(snew_ref, 0, h)],
                              [vt_buf[h], head_rows(snew_ref, 1, h)], bs_ref[0, h], svalid)


def _sample_attn(pages, halves, q8, cache_t, snew_t, bias_sel, win_t, wnew_t, bias_w, n_gather):
    bsz = q8.shape[0]
    wlen = win_t.shape[2]
    keys = n_gather * PAGE_SIZE
    rows = 2 * NSA_KV_HEADS * NSA_HEAD_DIM
    qspec = pl.BlockSpec((1, NSA_KV_HEADS, GROUP_ROWS, NSA_HEAD_DIM), lambda b, pg, hf: (b, 0, 0, 0))
    newspec = pl.BlockSpec((1, rows, PAGE_SIZE), lambda b, pg, hf: (b, 0, 0))
    return pl.pallas_call(
        functools.partial(_sample_attn_kernel, n_gather=n_gather, wlen=wlen),
        grid_spec=pltpu.PrefetchScalarGridSpec(
            num_scalar_prefetch=2, grid=(bsz,),
            in_specs=[qspec,
                      pl.BlockSpec(memory_space=pl.ANY),
                      newspec,
                      pl.BlockSpec((1, NSA_KV_HEADS, GROUP_ROWS, keys + PAGE_SIZE), lambda b, pg, hf: (b, 0, 0, 0)),
                      pl.BlockSpec((1, rows, wlen), lambda b, pg, hf: (b, 0, 0)),
                      newspec,
                      pl.BlockSpec((NSA_KV_HEADS, GROUP_ROWS, wlen + PAGE_SIZE), lambda b, pg, hf: (0, 0, 0))],
            out_specs=[qspec, qspec],
            scratch_shapes=[pltpu.VMEM((NSA_KV_HEADS, NSA_HEAD_DIM, keys), F32),
                            pltpu.VMEM((NSA_KV_HEADS, NSA_HEAD_DIM, keys), F32),
                            pltpu.SemaphoreType.DMA(())]),
        out_shape=[jax.ShapeDtypeStruct(q8.shape, F32), jax.ShapeDtypeStruct(q8.shape, F32)],
        compiler_params=_cparams(("arbitrary",)),
        name="sample_slc_win_attn",
    )(pages, halves, q8, cache_t, snew_t, bias_sel, win_t, wnew_t, bias_w)


PAGES_PER_STEP = 16


def _paged_pq_kernel(pt_ref, cache_hbm, pos_ref, w_ref, o_ref, pbuf, tok_scr, sem):
    b = pl.program_id(0)
    g = pl.program_id(1)
    ng = pl.num_programs(1)
    lin = b * ng + g
    slot = lin % 2
    rows_per_page = 2 * NSA_KV_HEADS * NSA_HEAD_DIM
    chunks = PAGES_PER_STEP * (PAGE_SIZE // CMP_STRIDE)

    def page_copy(step, p, buf_slot):
        sb = step // ng
        sg = step - sb * ng
        page = pt_ref[sb, sg * PAGES_PER_STEP + p]
        return pltpu.make_async_copy(cache_hbm.at[pl.ds(pl.multiple_of(page * rows_per_page, rows_per_page),
                                                        rows_per_page)],
                                     pbuf.at[buf_slot, p], sem.at[buf_slot])

    def fetch(step, buf_slot):
        for p in range(PAGES_PER_STEP):
            page_copy(step, p, buf_slot).start()

    @pl.when(lin == 0)
    def _():
        fetch(0, 0)

    @pl.when(lin + 1 < pl.num_programs(0) * ng)
    def _():
        fetch(lin + 1, 1 - slot)

    for p in range(PAGES_PER_STEP):
        page_copy(lin, p, slot).wait()

    for pair in range(NSA_KV_HEADS):
        for p in range(PAGES_PER_STEP):
            tok_scr[p * PAGE_SIZE:(p + 1) * PAGE_SIZE, :] = pbuf[slot, p, pair * LANE:(pair + 1) * LANE, :].T
        toks = [tok_scr[pl.ds(l, chunks, stride=CMP_STRIDE), :] for l in range(CMP_STRIDE)]
        for half in range(2):
            ch = 2 * pair + half
            c = ch // NSA_KV_HEADS
            lanes = slice(half * NSA_HEAD_DIM, (half + 1) * NSA_HEAD_DIM)
            chunk = jnp.concatenate([t[:, lanes] for t in toks], axis=1)
            for part in range(2):
                lhs = (chunk + pos_ref[c, part:part + 1, :]).astype(BF16)
                o_ref[0, ch, :, part * CMP_HIDDEN:(part + 1) * CMP_HIDDEN] = _dot(lhs, w_ref[c, part])


def _paged_pq(cache_t, page_table, cmp_pos, cmp_w1):
    db, n_pages = page_table.shape
    kdim = CMP_STRIDE * NSA_HEAD_DIM
    nch = n_pages * (PAGE_SIZE // CMP_STRIDE)
    tn = PAGES_PER_STEP * (PAGE_SIZE // CMP_STRIDE)
    rows_per_page = 2 * NSA_KV_HEADS * NSA_HEAD_DIM
    return pl.pallas_call(
        _paged_pq_kernel,
        grid_spec=pltpu.PrefetchScalarGridSpec(
            num_scalar_prefetch=1, grid=(db, n_pages // PAGES_PER_STEP),
            in_specs=[pl.BlockSpec(memory_space=pl.ANY),
                      pl.BlockSpec((2, 2, kdim), lambda b, g, pt: (0, 0, 0)),
                      pl.BlockSpec((2, 2, kdim, CMP_HIDDEN), lambda b, g, pt: (0, 0, 0, 0))],
            out_specs=pl.BlockSpec((1, 8, tn, 2 * CMP_HIDDEN), lambda b, g, pt: (b, 0, g, 0)),
            scratch_shapes=[pltpu.VMEM((2, PAGES_PER_STEP, rows_per_page, PAGE_SIZE), F32),
                            pltpu.VMEM((PAGES_PER_STEP * PAGE_SIZE, LANE), F32),
                            pltpu.SemaphoreType.DMA((2,))]),
        out_shape=jax.ShapeDtypeStruct((db, 8, nch, 2 * CMP_HIDDEN), F32),
        compiler_params=_cparams(("arbitrary", "arbitrary")),
        name="cmp_pq_paged",
    )(page_table, cache_t, cmp_pos.reshape(2, 2, kdim), cmp_w1.reshape(2, 2, kdim, CMP_HIDDEN).astype(BF16))


def _pad_rows(a, n):
    return jnp.concatenate([a, jnp.zeros((n - a.shape[0],) + a.shape[1:], a.dtype)], axis=0)


def _sample_mixer(x_sample, cache_cmp, cache_slc, cache_win, st_c, st_n, st_m, st_conv, page_table,
                  norm_g, wb, gate_bias, conv_w, ml_norm_g, cmp_pos, cmp_w1, cmp_w2, nsa_norm_g, w_out_b, rel_bias):
    db = x_sample.shape[0]
    n_pages = page_table.shape[1]
    past = n_pages * PAGE_SIZE
    tok = 16
    x16 = _pad_rows(x_sample.reshape(db, D_MODEL), tok)
    u = _proj(x16, norm_g, wb, tok, 512)
    small = u[:, COL_SMALL:COL_SMALL + LANE]

    T = 128
    useq = jnp.zeros((db, T, 4 * D_ML), F32)
    useq = useq.at[:, T - CONV_W:T - 1, 0:2 * D_ML].set(st_conv)
    useq = useq.at[:, T - 1, :].set(u[:db, 0:4 * D_ML])
    sseq = jnp.zeros((db, T, LANE), F32).at[:, :T - 1, LANE - 1].set(1.0)
    sseq = sseq.at[:, T - 1, :].set(small[:db])
    y_seq, conv_n, c_n, n_n, m_n = _mlstm(
        useq.reshape(db * T, 4 * D_ML), sseq.reshape(db * T, LANE), jnp.zeros((db, CONV_W - 1, 2 * D_ML), F32),
        st_c, st_n, st_m, conv_w, gate_bias, ml_norm_g, db, T, T)
    y_ml = y_seq.reshape(db, T, D_ML)[:, T - 1]

    n_pool = cache_cmp.shape[0]
    cmp_t = cache_cmp.transpose(0, 2, 3, 4, 1).reshape(n_pool * 2 * D_KV, PAGE_SIZE)
    kvc = _cmp_hid(_paged_pq(cmp_t, page_table, cmp_pos, cmp_w1), cmp_w2)
    nch = past // CMP_STRIDE
    n_cmp = (past + 1) // CMP_STRIDE - CMP_BLOCK // CMP_STRIDE + 1
    n_slc = -(-(past + 1) // SLC_BLOCK)
    q = u[:db, COL_QNSA:COL_QNSA + D_NSA].reshape(db, NSA_KV_HEADS, NSA_GROUP, NSA_HEAD_DIM)
    q8 = jnp.concatenate([q, jnp.zeros_like(q)], axis=2)
    bd = _bias_by_distance(rel_bias, past + 1).reshape(NSA_KV_HEADS, NSA_GROUP, past + 1)
    pad_g = lambda a: jnp.concatenate([a, jnp.zeros_like(a)], axis=1)
    dist_c = np.clip(past - (np.arange(nch) * CMP_STRIDE + CMP_BLOCK - 1), 0, None)
    o_c8, sel = _sample_cattn(q8, kvc, pad_g(bd[:, :, dist_c]), n_cmp, n_slc)

    mask = sel[:, :, 0, :n_slc] > 0.5
    idx = jnp.sort(jnp.where(mask, jnp.arange(n_slc, dtype=jnp.int32), jnp.int32(1 << 20)), axis=-1)
    n_gather = SLC_TOPN - 1
    idx = idx[..., :n_gather]
    pages_per_block = PAGE_SIZE // SLC_BLOCK
    logical_page = idx // pages_per_block
    pages = jnp.take_along_axis(page_table[:, None, :], logical_page, axis=2).reshape(-1).astype(jnp.int32)
    halves = (idx % pages_per_block).reshape(-1).astype(jnp.int32)
    kpos = (logical_page[..., None] * PAGE_SIZE + jnp.arange(PAGE_SIZE, dtype=jnp.int32)).reshape(db, NSA_KV_HEADS, -1)
    hh = jnp.arange(NSA_KV_HEADS)[None, :, None, None]
    gg = jnp.arange(NSA_GROUP)[None, None, :, None]
    bias_sel = bd[hh, gg, (past - kpos)[:, :, None, :]]
    bias_sel = jnp.concatenate([bias_sel, jnp.broadcast_to(bd[None, :, :, 0:1], (db, NSA_KV_HEADS, NSA_GROUP, 1)),
                                jnp.zeros((db, NSA_KV_HEADS, NSA_GROUP, PAGE_SIZE - 1), F32)], axis=-1)
    bias_sel = jnp.concatenate([bias_sel, jnp.zeros_like(bias_sel)], axis=2)
    wlen = cache_win.shape[1]
    dist_w = np.clip(wlen - np.arange(wlen + PAGE_SIZE), 0, None)
    kvs_new = u[:db, COL_KVS:COL_KVS + 2 * D_KV]
    kvw_new = u[:db, COL_KVW:COL_KVW + 2 * D_KV]
    lane_pad = lambda a: jnp.pad(a[:, :, None], ((0, 0), (0, 0), (0, PAGE_SIZE - 1)))
    slc_t = cache_slc.transpose(0, 2, 3, 4, 1).reshape(n_pool * 2 * D_KV, PAGE_SIZE)
    win_t = cache_win.transpose(0, 2, 3, 4, 1).reshape(db, 2 * D_KV, wlen)
    o_s8, o_w8 = _sample_attn(pages, halves, q8, slc_t, lane_pad(kvs_new), bias_sel, win_t, lane_pad(kvw_new),
                              pad_g(bd[:, :, dist_w]), n_gather)
    win2d = cache_win.reshape(db, wlen, 2 * D_KV)

    heads = lambda o: _pad_rows(o[:, :, :NSA_GROUP, :].reshape(db, D_NSA), tok)
    y = _outproj(_pad_rows(y_ml, tok), heads(o_c8), heads(o_s8), heads(o_w8), small, nsa_norm_g, w_out_b, x16, tok, 512)
    kvshape = (1, db, 1, 2, NSA_KV_HEADS, NSA_HEAD_DIM)
    new_win = jnp.concatenate([win2d[:, 1:], kvw_new[:, None, :]], axis=1)
    states = (u[:db, COL_KVC:COL_KVC + 2 * D_KV].reshape(kvshape), kvs_new.reshape(kvshape),
              new_win.reshape((1, db, wlen, 2, NSA_KV_HEADS, NSA_HEAD_DIM)),
              c_n[None], n_n[None], m_n[None], conv_n[None])
    return y, states


def kernel(x_prompt, x_sample, cache_cmp_kv, cache_slc_kv, cache_win_kv, state_mlstm_C, state_mlstm_n,
           state_mlstm_m, state_conv, page_table, rel_bias, norm_mix_g, w_in, b_ig, b_fg, conv_w, ml_norm_g,
           cmp_pos, cmp_w1, cmp_w2, nsa_norm_g, w_out, norm_ffn_g, w_router_grp, b_router_grp, w_router_exp,
           b_router_exp, w_gate, w_up, w_down, norm_final_g):
    B, S, D = x_prompt.shape
    wb = _reorder_w_in(w_in[0])
    gate_bias = jnp.zeros((1, LANE), F32).at[0, 0:ML_HEADS].set(b_ig[0]).at[0, ML_HEADS:2 * ML_HEADS].set(b_fg[0])
    w_out_b = w_out[0].astype(BF16)
    yp, st_p = _prompt_mixer(x_prompt, norm_mix_g, wb, gate_bias, conv_w[0], ml_norm_g, cmp_pos[0], cmp_w1[0],
                             cmp_w2[0], nsa_norm_g, w_out_b, rel_bias)
    ys, st_s = _sample_mixer(x_sample, cache_cmp_kv[0], cache_slc_kv[0], cache_win_kv[0], state_mlstm_C[0],
                             state_mlstm_n[0], state_mlstm_m[0], state_conv[0], page_table, norm_mix_g, wb, gate_bias,
                             conv_w[0], ml_norm_g, cmp_pos[0], cmp_w1[0], cmp_w2[0], nsa_norm_g, w_out_b, rel_bias)
    moe_w = (norm_ffn_g, w_router_grp[0], b_router_grp[0], w_router_exp[0], b_router_exp[0],
             w_gate[0], w_up[0], w_down[0], norm_final_g[None, :])
    DB, L, _ = x_sample.shape
    out_p = _moe_final(yp.reshape(B * S, D), *moe_w, 256, 128).reshape(B, S, D)
    out_s = _moe_final(_pad_rows(ys, 128), *moe_w, 128, 16)[:DB].reshape(DB, L, D)
    outs = [out_p, out_s]
    for a, b in zip(st_p, st_s):
        outs += [a, b]
    return tuple(outs)


def _prompt_mixer(x_prompt, norm_g, wb, gate_bias, conv_w, ml_norm_g, cmp_pos, cmp_w1, cmp_w2, nsa_norm_g,
                  w_out_b, rel_bias):
    B, S, D = x_prompt.shape
    x2d = x_prompt.reshape(B * S, D)
    tm = min(1024, B * S)
    u = _proj(x2d, norm_g, wb, tm, 512)
    small = u[:, COL_SMALL:COL_SMALL + LANE]
    y_ml, conv_n, c_n, n_n, m_n = _mlstm(
        u, small, jnp.zeros((B, CONV_W - 1, 2 * D_ML), F32),
        jnp.zeros((B, ML_HEADS, ML_HEAD_DIM, ML_HEAD_DIM), F32), jnp.zeros((B, ML_HEADS, ML_HEAD_DIM), F32),
        jnp.full((B, ML_HEADS), -jnp.inf, F32), conv_w, gate_bias, ml_norm_g, B, S, 256)
    kv_c = u[:, COL_KVC:COL_KVC + 2 * D_KV]
    kv_s = u[:, COL_KVS:COL_KVS + 2 * D_KV]
    kv_w = u[:, COL_KVW:COL_KVW + 2 * D_KV]
    nch = S // CMP_STRIDE
    kvc = _compress(u, COL_KVC // (2 * D_KV), B, S, cmp_pos, cmp_w1, cmp_w2, min(S, 4096))
    tq_c = min(S, 256)
    o_c, sel_t = _cattn(u, kvc, _cmp_bias_table(rel_bias, S, nch, tq_c), B, S, tq_c)
    band = _band_bias(rel_bias, TQ, S // TQ)
    k_s, vt_s = _kv_layouts(kv_s, B, S)
    k_w, vt_w = _kv_layouts(kv_w, B, S)
    o_s = _flash(u, k_s, vt_s, band, sel_t, B, S)
    o_w = _flash(u, k_w, vt_w, band, None, B, S)
    y = _outproj(y_ml, o_c, o_s, o_w, small, nsa_norm_g, w_out_b, x2d, tm, 512)
    kvshape = (1, B, S, 2, NSA_KV_HEADS, NSA_HEAD_DIM)
    win = min(WINDOW, S)
    states = (kv_c.reshape(kvshape), kv_s.reshape(kvshape), kv_w.reshape(kvshape)[:, :, S - win:],
              c_n[None], n_n[None], m_n[None], conv_n[None])
    return y.reshape(B, S, D), states
```

```python
import functools
import math

import numpy as np
import jax
import jax.numpy as jnp
from jax import lax
from jax.experimental import pallas as pl
from jax.experimental.pallas import tpu as pltpu

F32 = jnp.float32
BF16 = jnp.bfloat16

D_MODEL = 2048
ML_HEADS = 4
ML_HEAD_DIM = 256
D_ML = 1024
CONV_W = 4
NSA_HEADS = 16
NSA_HEAD_DIM = 64
D_NSA = 1024
NSA_KV_HEADS = 4
NSA_GROUP = 4
D_KV = 256
CMP_BLOCK = 32
CMP_STRIDE = 16
CMP_HIDDEN = 256
SLC_BLOCK = 64
SLC_TOPN = 16
WINDOW = 512
N_BUCKETS = 32
MAX_DISTANCE = 2048
N_GROUPS = 4
EXPERTS_PER_GROUP = 8
N_EXPERTS = 32
D_EXPERT = 512
PAGE_SIZE = 128
EPS = 1e-6
NEG_BIG = -1e30
FORCE_SCORE = 1e4

LANE = 128
COL_QML, COL_KML, COL_VML, COL_OML = 0, 1024, 2048, 3072
COL_QNSA = 4096
COL_KVC, COL_KVS, COL_KVW = 5120, 5632, 6144
COL_SMALL = 6656
N_PROJ = 7168
VMEM_LIMIT = 56 * 1024 * 1024


def _cparams(sem, vmem=VMEM_LIMIT):
    return pltpu.CompilerParams(dimension_semantics=sem, vmem_limit_bytes=vmem)


def _split2(x):
    hi = x.astype(BF16)
    lo = (x - hi.astype(F32)).astype(BF16)
    return hi, lo


def _split3(x):
    hi = x.astype(BF16)
    r = x - hi.astype(F32)
    mid = r.astype(BF16)
    lo = (r - mid.astype(F32)).astype(BF16)
    return hi, mid, lo


def _dot(a, b):
    return jnp.dot(a, b, preferred_element_type=F32)


def _dot_nt(a, b):
    return lax.dot_general(a, b, (((1,), (1,)), ((), ())), preferred_element_type=F32)


def _dot_tn(a, b):
    return lax.dot_general(a, b, (((0,), (0,)), ((), ())), preferred_element_type=F32)


def _proj_kernel(x_ref, g_ref, w_ref, o_ref, h_scr):
    @pl.when(pl.program_id(1) == 0)
    def _():
        x = x_ref[...]
        ms = jnp.mean(x * x, axis=-1, keepdims=True)
        h_scr[...] = (x * lax.rsqrt(ms + EPS) * g_ref[...]).astype(BF16)

    o_ref[...] = _dot(h_scr[...], w_ref[...])


def _proj(x2d, g, wb, tm, tn):
    n, d = x2d.shape
    nc = wb.shape[1]
    return pl.pallas_call(
        _proj_kernel,
        grid=(n // tm, nc // tn),
        in_specs=[pl.BlockSpec((tm, d), lambda i, j: (i, 0)),
                  pl.BlockSpec((1, d), lambda i, j: (0, 0)),
                  pl.BlockSpec((d, tn), lambda i, j: (0, j))],
        out_specs=pl.BlockSpec((tm, tn), lambda i, j: (i, j)),
        out_shape=jax.ShapeDtypeStruct((n, nc), F32),
        scratch_shapes=[pltpu.VMEM((tm, d), BF16)],
        compiler_params=_cparams(("arbitrary", "arbitrary")),
        name="proj",
    )(x2d, g, wb)


def _reorder_w_in(w_in):
    big = w_in[:, :4 * D_ML]
    small_a = w_in[:, 4 * D_ML:4 * D_ML + 2 * ML_HEADS]
    rest = w_in[:, 4 * D_ML + 2 * ML_HEADS:]
    nsa = rest[:, :D_NSA + 6 * D_KV]
    gate = rest[:, D_NSA + 6 * D_KV:]
    pad = jnp.zeros((w_in.shape[0], N_PROJ - COL_SMALL - 2 * ML_HEADS - 3 * NSA_HEADS), w_in.dtype)
    return jnp.concatenate([big, nsa, small_a, gate, pad], axis=1).astype(BF16)


def _log_sigmoid(x):
    return jnp.minimum(x, 0.0) - jnp.log1p(jnp.exp(-jnp.abs(x)))


def _mlstm_kernel(q_ref, k_ref, v_ref, o_ref, s_ref, cb_ref, c0_ref, n0_ref, m0_ref,
                  cw_ref, gb_ref, ng_ref,
                  y_ref, cbo_ref, co_ref, no_ref, mo_ref,
                  ext_scr, c_scr, n_scr, m_scr, *, T):
    c = pl.program_id(1)
    nc = pl.num_programs(1)

    @pl.when(c == 0)
    def _():
        ext_scr[0:8, :] = jnp.zeros((8, 2 * D_ML), F32)
        ext_scr[5:8, :] = cb_ref[0]
        c_scr[...] = c0_ref[0]
        n_scr[...] = n0_ref[0]
        m_scr[...] = m0_ref[0]

    ext_scr[8:8 + T, 0:D_ML] = q_ref[...]
    ext_scr[8:8 + T, D_ML:2 * D_ML] = k_ref[...]
    conv = ext_scr[5:5 + T, :] * cw_ref[0:1, :]
    for j in range(1, CONV_W):
        conv = conv + ext_scr[5 + j:5 + j + T, :] * cw_ref[j:j + 1, :]
    tail = ext_scr[8 + T - 3:8 + T, :]
    ext_scr[5:8, :] = tail
    cbo_ref[0] = tail
    qk = conv * jax.nn.sigmoid(conv)

    pre = s_ref[...] + gb_ref[...]
    col = lax.broadcasted_iota(jnp.int32, pre.shape, 1)
    padrow = s_ref[:, LANE - 1:LANE] > 0.5
    gates = jnp.where(col < ML_HEADS, pre, _log_sigmoid(pre))
    gates = jnp.where(padrow, jnp.where(col < ML_HEADS, NEG_BIG, 0.0), gates)
    g_r = gates.T
    ti = lax.broadcasted_iota(jnp.int32, (T, T), 0)
    si = lax.broadcasted_iota(jnp.int32, (T, T), 1)
    upper = (ti <= si).astype(BF16)
    g_fin = jnp.where(lax.broadcasted_iota(jnp.int32, g_r.shape, 0) < ML_HEADS, 0.0, g_r)
    hi, mid, lo = _split3(g_fin)
    cum_r = _dot(hi, upper) + _dot(mid, upper) + _dot(lo, upper)
    rowi = lax.broadcasted_iota(jnp.int32, g_r.shape, 0)
    a_r = jnp.where(rowi < ML_HEADS, g_r, cum_r)
    a_c = a_r.T
    causal = si <= ti

    for h in range(ML_HEADS):
        sl = slice(h * ML_HEAD_DIM, (h + 1) * ML_HEAD_DIM)
        q = qk[:, h * ML_HEAD_DIM:(h + 1) * ML_HEAD_DIM]
        k = qk[:, D_ML + h * ML_HEAD_DIM:D_ML + (h + 1) * ML_HEAD_DIM] * (ML_HEAD_DIM ** -0.5)
        v = v_ref[:, sl]
        ig_r = a_r[h:h + 1, :]
        b_r = a_r[ML_HEADS + h:ML_HEADS + h + 1, :]
        ig_c = a_c[:, h:h + 1]
        b_c = a_c[:, ML_HEADS + h:ML_HEADS + h + 1]
        m_prev = m_scr[h:h + 1, 0:1]
        logd = jnp.where(causal, b_c - b_r + ig_r, -jnp.inf)
        inter = b_c + m_prev
        m_t = jnp.maximum(inter, jnp.max(logd, axis=1, keepdims=True))
        w_intra = jnp.exp(logd - m_t)
        w_inter = jnp.exp(inter - m_t)
        qb = q.astype(BF16)
        kb = k.astype(BF16)
        vb = v.astype(BF16)
        sc = _dot_nt(qb, kb) * w_intra
        cmat = c_scr[h]
        nvec = n_scr[h:h + 1, :]
        num = _dot(sc.astype(BF16), vb) + w_inter * _dot(qb, cmat.astype(BF16))
        qn = jnp.sum(qb.astype(F32) * nvec.astype(BF16).astype(F32), axis=1, keepdims=True)
        den = jnp.sum(sc, axis=1, keepdims=True) + w_inter * qn
        hh = num / jnp.maximum(jnp.abs(den), jnp.exp(-m_t))
        m_new = m_t[T - 1:T, :]
        b_last = b_c[T - 1:T, :]
        w_s = jnp.exp(b_last - b_c + ig_c - m_new)
        decay = jnp.exp(b_last + m_prev - m_new)
        kw = k * w_s
        c_new = decay * cmat + _dot_tn(kw.astype(BF16), vb)
        n_new = decay * nvec + jnp.sum(kw, axis=0, keepdims=True)
        c_scr[h] = c_new
        n_scr[h:h + 1, :] = n_new
        m_scr[h:h + 1, :] = jnp.broadcast_to(m_new, (1, LANE))
        hn = hh * lax.rsqrt(jnp.mean(hh * hh, axis=1, keepdims=True) + EPS) * ng_ref[:, sl]
        y_ref[:, sl] = (hn * jax.nn.sigmoid(o_ref[:, sl])).astype(y_ref.dtype)

    @pl.when(c == nc - 1)
    def _():
        co_ref[0] = c_scr[...]
        no_ref[0] = n_scr[...]
        mo_ref[0] = m_scr[...]


def _mlstm(u, small, conv_buf, c0, n0, m0, conv_w, gate_bias, norm_g, batch, seq, T):
    nc = seq // T
    cb = D_ML // 1024
    m0b = jnp.broadcast_to(m0[:, :, None], (batch, ML_HEADS, LANE))
    m0b = jnp.concatenate([m0b, jnp.zeros((batch, 8 - ML_HEADS, LANE), F32)], axis=1)
    n0p = jnp.concatenate([n0, jnp.zeros((batch, 8 - ML_HEADS, ML_HEAD_DIM), F32)], axis=1)
    row = lambda b, c: (b * nc + c, 0)
    outs = pl.pallas_call(
        functools.partial(_mlstm_kernel, T=T),
        grid=(batch, nc),
        in_specs=[pl.BlockSpec((T, D_ML), lambda b, c: (b * nc + c, COL_QML // D_ML)),
                  pl.BlockSpec((T, D_ML), lambda b, c: (b * nc + c, COL_KML // D_ML)),
                  pl.BlockSpec((T, D_ML), lambda b, c: (b * nc + c, COL_VML // D_ML)),
                  pl.BlockSpec((T, D_ML), lambda b, c: (b * nc + c, COL_OML // D_ML)),
                  pl.BlockSpec((T, LANE), row),
                  pl.BlockSpec((1, CONV_W - 1, 2 * D_ML), lambda b, c: (b, 0, 0)),
                  pl.BlockSpec((1, ML_HEADS, ML_HEAD_DIM, ML_HEAD_DIM), lambda b, c: (b, 0, 0, 0)),
                  pl.BlockSpec((1, 8, ML_HEAD_DIM), lambda b, c: (b, 0, 0)),
                  pl.BlockSpec((1, 8, LANE), lambda b, c: (b, 0, 0)),
                  pl.BlockSpec((CONV_W, 2 * D_ML), lambda b, c: (0, 0)),
                  pl.BlockSpec((1, LANE), lambda b, c: (0, 0)),
                  pl.BlockSpec((1, D_ML), lambda b, c: (0, 0))],
        out_specs=[pl.BlockSpec((T, D_ML), row),
                   pl.BlockSpec((1, CONV_W - 1, 2 * D_ML), lambda b, c: (b, 0, 0)),
                   pl.BlockSpec((1, ML_HEADS, ML_HEAD_DIM, ML_HEAD_DIM), lambda b, c: (b, 0, 0, 0)),
                   pl.BlockSpec((1, 8, ML_HEAD_DIM), lambda b, c: (b, 0, 0)),
                   pl.BlockSpec((1, 8, LANE), lambda b, c: (b, 0, 0))],
        out_shape=[jax.ShapeDtypeStruct((batch * seq, D_ML), BF16),
                   jax.ShapeDtypeStruct((batch, CONV_W - 1, 2 * D_ML), F32),
                   jax.ShapeDtypeStruct((batch, ML_HEADS, ML_HEAD_DIM, ML_HEAD_DIM), F32),
                   jax.ShapeDtypeStruct((batch, 8, ML_HEAD_DIM), F32),
                   jax.ShapeDtypeStruct((batch, 8, LANE), F32)],
        scratch_shapes=[pltpu.VMEM((8 + T, 2 * D_ML), F32),
                        pltpu.VMEM((ML_HEADS, ML_HEAD_DIM, ML_HEAD_DIM), F32),
                        pltpu.VMEM((8, ML_HEAD_DIM), F32),
                        pltpu.VMEM((8, LANE), F32)],
        compiler_params=_cparams(("arbitrary", "arbitrary")),
        name="mlstm",
    )(u, u, u, u, small, conv_buf, c0, n0p, m0b, conv_w, gate_bias, norm_g)
    y, cbo, co, no, mo = outs
    return y, cbo, co, no[:, :ML_HEADS], mo[:, :ML_HEADS, 0]


def _bucket_np(dist):
    n = np.maximum(dist, 0)
    max_exact = N_BUCKETS // 2
    nf = np.maximum(n, 1).astype(np.float64)
    large = max_exact + (np.log(nf / max_exact) / math.log(MAX_DISTANCE / max_exact)
                         * (N_BUCKETS - max_exact)).astype(np.int64)
    return np.where(n < max_exact, n, np.minimum(large, N_BUCKETS - 1)).astype(np.int32)


def _bias_by_distance(rel_bias, n):
    return rel_bias.astype(F32)[_bucket_np(np.arange(n))].T


def _overlap_t(n_cmp, nch, n_slc):
    c0 = np.arange(nch) * CMP_STRIDE
    s0 = np.arange(n_slc) * SLC_BLOCK
    ov = np.minimum(c0[None, :] + CMP_BLOCK, s0[:, None] + SLC_BLOCK) - np.maximum(c0[None, :], s0[:, None])
    ov = np.clip(ov, 0, None).astype(np.float32) / CMP_BLOCK
    ov[:, n_cmp:] = 0.0
    return jnp.asarray(ov, BF16)


def _pq_kernel(x0_ref, x1_ref, x2_ref, x3_ref, pos_ref, w_ref, o_ref, *, rows):
    for pair, x_ref in enumerate((x0_ref, x1_ref, x2_ref, x3_ref)):
        toks = [x_ref[pl.ds(l, rows, stride=CMP_STRIDE), :] for l in range(CMP_STRIDE)]
        for half in range(2):
            ch = 2 * pair + half
            c = ch // NSA_KV_HEADS
            lanes = slice(half * NSA_HEAD_DIM, (half + 1) * NSA_HEAD_DIM)
            chunk = jnp.concatenate([t[:, lanes] for t in toks], axis=1)
            for part in range(2):
                lhs = (chunk + pos_ref[c, part:part + 1, :]).astype(BF16)
                o_ref[0, ch, :, part * CMP_HIDDEN:(part + 1) * CMP_HIDDEN] = _dot(lhs, w_ref[c, part])


def _hid_kernel(pq_ref, w2_ref, o_ref, *, nch):
    p = pq_ref[0, 0, :, 0:CMP_HIDDEN]
    q = pltpu.roll(pq_ref[0, 0, :, CMP_HIDDEN:2 * CMP_HIDDEN], nch - 1, 0)
    hid = jax.nn.gelu(p + q, approximate=True)
    o_ref[0, 0] = _dot(hid.astype(BF16), w2_ref[0])


def _compress(kv2d, col_block, bsz, seq, cmp_pos, cmp_w1, cmp_w2, tt):
    nch = seq // CMP_STRIDE
    kdim = CMP_STRIDE * NSA_HEAD_DIM
    nt = seq // tt
    tn = tt // CMP_STRIDE
    w1 = cmp_w1.reshape(2, 2, kdim, CMP_HIDDEN).astype(BF16)
    pos = cmp_pos.reshape(2, 2, kdim)
    pq = pl.pallas_call(
        functools.partial(_pq_kernel, rows=tn),
        grid=(bsz, nt),
        in_specs=[pl.BlockSpec((tt, LANE), functools.partial(lambda b, i, k: (b * nt + i, col_block * 4 + k), k=k))
                  for k in range(4)] +
                 [pl.BlockSpec((2, 2, kdim), lambda b, i: (0, 0, 0)),
                  pl.BlockSpec((2, 2, kdim, CMP_HIDDEN), lambda b, i: (0, 0, 0, 0))],
        out_specs=pl.BlockSpec((1, 8, tn, 2 * CMP_HIDDEN), lambda b, i: (b, 0, i, 0)),
        out_shape=jax.ShapeDtypeStruct((bsz, 8, nch, 2 * CMP_HIDDEN), F32),
        compiler_params=_cparams(("arbitrary", "arbitrary")),
        name="cmp_pq",
    )(kv2d, kv2d, kv2d, kv2d, pos, w1)
    return _cmp_hid(pq, cmp_w2)


def _cmp_hid(pq, cmp_w2):
    bsz, _, nch, _ = pq.shape
    return pl.pallas_call(
        functools.partial(_hid_kernel, nch=nch),
        grid=(bsz, 8),
        in_specs=[pl.BlockSpec((1, 1, nch, 2 * CMP_HIDDEN), lambda b, c: (b, c, 0, 0)),
                  pl.BlockSpec((1, CMP_HIDDEN, NSA_HEAD_DIM), lambda b, c: (c // NSA_KV_HEADS, 0, 0))],
        out_specs=pl.BlockSpec((1, 1, nch, NSA_HEAD_DIM), lambda b, c: (b, c, 0, 0)),
        out_shape=jax.ShapeDtypeStruct((bsz, 8, nch, NSA_HEAD_DIM), F32),
        compiler_params=_cparams(("arbitrary", "arbitrary")),
        name="cmp_hid",
    )(pq, cmp_w2.astype(BF16))


def _top_n_mask(score_t, blk, n_rows):
    rank = jnp.zeros(score_t.shape, jnp.int32)
    for i in range(n_rows):
        row = score_t[i:i + 1, :]
        beats = (row > score_t) | ((row == score_t) & (blk > i))
        rank = rank + beats.astype(jnp.int32)
    return (rank < min(SLC_TOPN, n_rows)).astype(F32)


def _bias_from_buckets(bucket, table_ref, head):
    bias = jnp.zeros(bucket.shape, F32)
    for k in range(N_BUCKETS):
        bias = jnp.where(bucket == k, table_ref[k, head], bias)
    return bias


def _cattn_kernel(tab_ref, q_ref, kc_ref, vc_ref, bk_ref, ovt_ref, o_ref, sel_ref, bias_scr,
                  *, tq, nch, n_cmp, n_slc):
    h = pl.program_id(0)
    i = pl.program_id(1)

    @pl.when(pl.program_id(2) == 0)
    def _():
        bucket = bk_ref[...]
        for g in range(NSA_GROUP):
            bias_scr[g] = _bias_from_buckets(bucket, tab_ref, h * NSA_GROUP + g)

    kc = kc_ref[0, 0].astype(BF16)
    vc = vc_ref[0, 0].astype(BF16)
    t = i * tq + lax.broadcasted_iota(jnp.int32, (tq, nch), 0)
    n = lax.broadcasted_iota(jnp.int32, (tq, nch), 1)
    valid = (t - CMP_STRIDE * n - (CMP_BLOCK - 1) >= 0) & (n < n_cmp)
    pc = jnp.zeros((tq, nch), F32)
    for g in range(NSA_GROUP):
        sl = slice(g * NSA_HEAD_DIM, (g + 1) * NSA_HEAD_DIM)
        qg = (q_ref[:, sl] * (NSA_HEAD_DIM ** -0.5)).astype(BF16)
        s = _dot_nt(qg, kc) + bias_scr[g]
        s = jnp.where(valid, s, NEG_BIG)
        e = jnp.exp(s - jnp.max(s, axis=1, keepdims=True))
        p = jnp.where(valid, e / jnp.sum(e, axis=1, keepdims=True), 0.0)
        o_ref[:, sl] = _dot(p.astype(BF16), vc)
        pc = pc + p
    score_t = _dot_nt(ovt_ref[...], pc.astype(BF16))
    blk = lax.broadcasted_iota(jnp.int32, (n_slc, tq), 0)
    cur = (i * tq + lax.broadcasted_iota(jnp.int32, (n_slc, tq), 1)) // SLC_BLOCK
    forced = (blk == 0) | (blk == cur) | (blk == cur - 1)
    score_t = jnp.where(forced, FORCE_SCORE, score_t)
    score_t = jnp.where(blk > cur, -FORCE_SCORE, score_t)
    sel_ref[0, 0] = _top_n_mask(score_t, blk, n_slc)


def _cattn(u, kvc, rel_bias, batch, seq, tq):
    nch = kvc.shape[2]
    n_cmp = nch - 1
    n_slc = seq // SLC_BLOCK
    ni = seq // tq
    ovt = _overlap_t(n_cmp, nch, n_slc)
    qcol = COL_QNSA // (NSA_GROUP * NSA_HEAD_DIM)
    dist = np.arange(seq)[:, None] - (CMP_STRIDE * np.arange(nch)[None, :] + CMP_BLOCK - 1)
    buckets = jnp.asarray(_bucket_np(dist))
    return pl.pallas_call(
        functools.partial(_cattn_kernel, tq=tq, nch=nch, n_cmp=n_cmp, n_slc=n_slc),
        grid=(NSA_KV_HEADS, ni, batch),
        in_specs=[pl.BlockSpec(memory_space=pltpu.SMEM),
                  pl.BlockSpec((tq, NSA_GROUP * NSA_HEAD_DIM), lambda h, i, b: (b * ni + i, qcol + h)),
                  pl.BlockSpec((1, 1, nch, NSA_HEAD_DIM), lambda h, i, b: (b, h, 0, 0)),
                  pl.BlockSpec((1, 1, nch, NSA_HEAD_DIM), lambda h, i, b: (b, NSA_KV_HEADS + h, 0, 0)),
                  pl.BlockSpec((tq, nch), lambda h, i, b: (i, 0)),
                  pl.BlockSpec((n_slc, nch), lambda h, i, b: (0, 0))],
        out_specs=[pl.BlockSpec((tq, NSA_GROUP * NSA_HEAD_DIM), lambda h, i, b: (b * ni + i, h)),
                   pl.BlockSpec((1, 1, n_slc, tq), lambda h, i, b: (b, h, 0, i))],
        out_shape=[jax.ShapeDtypeStruct((batch * seq, D_NSA), F32),
                   jax.ShapeDtypeStruct((batch, NSA_KV_HEADS, n_slc, seq), F32)],
        scratch_shapes=[pltpu.VMEM((NSA_GROUP, tq, nch), F32)],
        compiler_params=_cparams(("arbitrary", "arbitrary", "arbitrary")),
        name="cmp_attn",
    )(rel_bias.astype(F32), u, kvc, kvc, buckets, ovt)


TQ = 128
FLASH_GROUP = 4


def _flash_kernel(tab_ref, q_ref, k_ref, v_ref, bk_ref, *rest, selected, n_delta):
    if selected:
        sel_ref, o_ref, band_scr, qs_scr, m_scr, l_scr, acc_scr = rest
    else:
        o_ref, band_scr, qs_scr, m_scr, l_scr, acc_scr = rest
    h = pl.program_id(0)
    i = pl.program_id(2)
    odd = h % 2 == 1

    @pl.when((pl.program_id(1) == 0) & (i == 0))
    def _():
        def fill(d, carry):
            bucket = bk_ref[d]
            for g in range(NSA_GROUP):
                band_scr[d, :, g * TQ:(g + 1) * TQ] = _bias_from_buckets(bucket, tab_ref, h * NSA_GROUP + g)
            return carry
        lax.fori_loop(0, n_delta, fill, 0)

    qs_scr[...] = jnp.zeros(qs_scr.shape, BF16)
    for half, cond in ((0, jnp.logical_not(odd)), (1, odd)):
        @pl.when(cond)
        def _():
            for g in range(NSA_GROUP):
                qs_scr[g * TQ:(g + 1) * TQ, half * NSA_HEAD_DIM:(half + 1) * NSA_HEAD_DIM] = (
                    q_ref[:, g * NSA_HEAD_DIM:(g + 1) * NSA_HEAD_DIM] * (NSA_HEAD_DIM ** -0.5)).astype(BF16)
    m_scr[...] = jnp.full(m_scr.shape, 0.5 * NEG_BIG, F32)
    l_scr[...] = jnp.zeros(l_scr.shape, F32)
    acc_scr[...] = jnp.zeros(acc_scr.shape, F32)
    key = lax.broadcasted_iota(jnp.int32, (TQ, TQ), 0)
    qry = lax.broadcasted_iota(jnp.int32, (TQ, TQ), 1)
    n_back = WINDOW // TQ

    def scores(j, kind):
        kj = k_ref[pl.ds(pl.multiple_of(j * TQ, TQ), TQ), :].astype(BF16)
        s = _dot_nt(kj, qs_scr[...]) + band_scr[i - j]
        mask = None
        if kind == "diag":
            mask = qry >= key
        elif kind == "far":
            mask = qry < key
        if selected:
            r = sel_ref[0, 0, pl.ds(2 * j, 2), :]
            picked = jnp.where(key < SLC_BLOCK, r[0:1, :], r[1:2, :]) > 0.5
            mask = picked if mask is None else (mask & picked)
        if mask is not None:
            s = jnp.concatenate([jnp.where(mask, s[:, g * TQ:(g + 1) * TQ], NEG_BIG) for g in range(NSA_GROUP)], axis=1)
        vj = v_ref[pl.ds(pl.multiple_of(j * TQ, TQ), TQ), :].astype(BF16)
        return s, vj

    def update(tiles):
        m_old = m_scr[...]
        m_new = m_old
        for s, _ in tiles:
            m_new = jnp.maximum(m_new, jnp.max(s, axis=0, keepdims=True))
        alpha = jnp.exp(m_old - m_new)
        l_new = alpha * l_scr[...]
        acc = alpha * acc_scr[...]
        for s, vj in tiles:
            p = jnp.exp(s - m_new)
            l_new = l_new + jnp.sum(p, axis=0, keepdims=True)
            acc = acc + _dot_tn(vj, p.astype(BF16))
        m_scr[...] = m_new
        l_scr[...] = l_new
        acc_scr[...] = acc

    def full_tiles(first, count):
        return [scores(first + k, "full") for k in range(count)]

    if selected:
        def group(t, carry):
            update(full_tiles(FLASH_GROUP * t, FLASH_GROUP))
            return carry

        n_groups = i // FLASH_GROUP
        lax.fori_loop(0, n_groups, group, 0)
        for rem in range(FLASH_GROUP):
            @pl.when(i - n_groups * FLASH_GROUP == rem)
            def _():
                update(full_tiles(n_groups * FLASH_GROUP, rem) + [scores(i, "diag")])
    else:
        @pl.when(i >= n_back)
        def _():
            update([scores(i - n_back, "far")] + full_tiles(i - n_back + 1, n_back - 1) + [scores(i, "diag")])

        for rem in range(n_back):
            @pl.when(i == rem)
            def _():
                update(full_tiles(0, rem) + [scores(i, "diag")])

    for g in range(NSA_GROUP):
        cols = slice(g * TQ, (g + 1) * TQ)
        o_t = (acc_scr[:, cols] / l_scr[:, cols]).T
        for half, cond in ((0, jnp.logical_not(odd)), (1, odd)):
            @pl.when(cond)
            def _():
                o_ref[:, g * NSA_HEAD_DIM:(g + 1) * NSA_HEAD_DIM] = o_t[:, half * NSA_HEAD_DIM:(half + 1) * NSA_HEAD_DIM]


def _flash(u, kv_col, rel_bias, sel_t, batch, seq):
    ni = seq // TQ
    qcol = COL_QNSA // (NSA_GROUP * NSA_HEAD_DIM)
    selected = sel_t is not None
    n_delta = ni if selected else WINDOW // TQ + 1
    kblk = kv_col // LANE
    vblk = (kv_col + D_KV) // LANE
    delta = np.arange(n_delta)[:, None, None] * TQ + np.arange(TQ)[None, None, :] - np.arange(TQ)[None, :, None]
    buckets = jnp.asarray(_bucket_np(delta))
    in_specs = [pl.BlockSpec(memory_space=pltpu.SMEM),
                pl.BlockSpec((TQ, NSA_GROUP * NSA_HEAD_DIM), lambda h, b, i: (b * ni + i, qcol + h)),
                pl.BlockSpec((seq, LANE), lambda h, b, i: (b, kblk + h // 2)),
                pl.BlockSpec((seq, LANE), lambda h, b, i: (b, vblk + h // 2)),
                pl.BlockSpec((n_delta, TQ, TQ), lambda h, b, i: (0, 0, 0))]
    args = [rel_bias.astype(F32), u, u, u, buckets]
    if selected:
        n_slc = sel_t.shape[2]
        in_specs.append(pl.BlockSpec((1, 1, n_slc, TQ), lambda h, b, i: (b, h, 0, i)))
        args.append(sel_t)
    return pl.pallas_call(
        functools.partial(_flash_kernel, selected=selected, n_delta=n_delta),
        grid=(NSA_KV_HEADS, batch, ni),
        in_specs=in_specs,
        out_specs=pl.BlockSpec((TQ, NSA_GROUP * NSA_HEAD_DIM), lambda h, b, i: (b * ni + i, h)),
        out_shape=jax.ShapeDtypeStruct((batch * seq, D_NSA), F32),
        scratch_shapes=[pltpu.VMEM((n_delta, TQ, NSA_GROUP * TQ), F32),
                        pltpu.VMEM((NSA_GROUP * TQ, LANE), BF16),
                        pltpu.VMEM((1, NSA_GROUP * TQ), F32),
                        pltpu.VMEM((1, NSA_GROUP * TQ), F32),
                        pltpu.VMEM((LANE, NSA_GROUP * TQ), F32)],
        compiler_params=_cparams(("arbitrary", "arbitrary", "arbitrary")),
        name="slc_attn" if selected else "win_attn",
    )(*args)


def _gate_expand():
    e = np.zeros((3, LANE, D_NSA), np.float32)
    for br in range(3):
        for hd in range(NSA_HEADS):
            e[br, 2 * ML_HEADS + br * NSA_HEADS + hd, hd * NSA_HEAD_DIM:(hd + 1) * NSA_HEAD_DIM] = 1.0
    return jnp.asarray(e, BF16)


def _outproj_kernel(yml_ref, oc_ref, os_ref, ow_ref, s_ref, e_ref, ng_ref, w_ref, x_ref, o_ref, cat_scr):
    @pl.when(pl.program_id(1) == 0)
    def _():
        sig = jax.nn.sigmoid(s_ref[...])
        hi, lo = _split2(sig)
        o = jnp.zeros(oc_ref.shape, F32)
        for br, ref in enumerate((oc_ref, os_ref, ow_ref)):
            gexp = _dot(hi, e_ref[br]) + _dot(lo, e_ref[br])
            o = o + gexp * ref[...]
        o = o * lax.rsqrt(jnp.mean(o * o, axis=-1, keepdims=True) + EPS) * ng_ref[...]
        cat_scr[:, 0:D_ML] = yml_ref[...]
        cat_scr[:, D_ML:D_ML + D_NSA] = o.astype(BF16)

    o_ref[...] = x_ref[...] + _dot(cat_scr[...], w_ref[...])


def _outproj(y_ml, o_c, o_s, o_w, small, nsa_g, w_out_b, x2d, tm, tn):
    n, d = x2d.shape
    rows = lambda i, j: (i, 0)
    return pl.pallas_call(
        _outproj_kernel,
        grid=(n // tm, d // tn),
        in_specs=[pl.BlockSpec((tm, D_ML), rows), pl.BlockSpec((tm, D_NSA), rows),
                  pl.BlockSpec((tm, D_NSA), rows), pl.BlockSpec((tm, D_NSA), rows),
                  pl.BlockSpec((tm, LANE), rows),
                  pl.BlockSpec((3, LANE, D_NSA), lambda i, j: (0, 0, 0)),
                  pl.BlockSpec((1, D_NSA), lambda i, j: (0, 0)),
                  pl.BlockSpec((D_ML + D_NSA, tn), lambda i, j: (0, j)),
                  pl.BlockSpec((tm, tn), lambda i, j: (i, j))],
        out_specs=pl.BlockSpec((tm, tn), lambda i, j: (i, j)),
        out_shape=jax.ShapeDtypeStruct((n, d), F32),
        scratch_shapes=[pltpu.VMEM((tm, D_ML + D_NSA), BF16)],
        compiler_params=_cparams(("arbitrary", "arbitrary")),
        name="outproj",
    )(y_ml, o_c, o_s, o_w, small, _gate_expand(), nsa_g, w_out_b, x2d)


ROUTE_COL = N_GROUPS
BIG_COL = 1 << 20
SPLIT = D_MODEL // LANE
DMA_UNROLL = 8


def _store_split(ref, val, n):
    for k in range(SPLIT):
        ref[pl.ds(k, n, stride=SPLIT), :] = val[:, k * LANE:(k + 1) * LANE]


def _load_split(ref, n):
    return jnp.concatenate([ref[pl.ds(k, n, stride=SPLIT), :] for k in range(SPLIT)], axis=1)


def _route_kernel(y_ref, g_ref, w_ref, b_ref, h_ref, info_ref, cnt_ref, carry_scr, *, tm):
    @pl.when(pl.program_id(0) == 0)
    def _():
        carry_scr[...] = jnp.zeros(carry_scr.shape, F32)

    x = y_ref[...]
    h = x * lax.rsqrt(jnp.mean(x * x, axis=-1, keepdims=True) + EPS) * g_ref[...]
    _store_split(h_ref, h, tm)
    logit = _dot(h.astype(BF16), w_ref[...]) + b_ref[...]
    col = lax.broadcasted_iota(jnp.int32, logit.shape, 1)
    is_grp = col < N_GROUPS
    gmax = jnp.max(jnp.where(is_grp, logit, -jnp.inf), axis=1, keepdims=True)
    gtop = jnp.min(jnp.where(is_grp & (logit == gmax), col, BIG_COL), axis=1, keepdims=True)
    gsum = jnp.sum(jnp.where(is_grp, jnp.exp(logit - gmax), 0.0), axis=1, keepdims=True)
    first = ROUTE_COL + gtop * EXPERTS_PER_GROUP
    in_grp = (col >= first) & (col < first + EXPERTS_PER_GROUP)
    v1 = jnp.max(jnp.where(in_grp, logit, -jnp.inf), axis=1, keepdims=True)
    i1 = jnp.min(jnp.where(in_grp & (logit == v1), col, BIG_COL), axis=1, keepdims=True)
    rest = in_grp & (col != i1)
    v2 = jnp.max(jnp.where(rest, logit, -jnp.inf), axis=1, keepdims=True)
    i2 = jnp.min(jnp.where(rest & (logit == v2), col, BIG_COL), axis=1, keepdims=True)
    e = jnp.exp(v2 - v1)
    w1 = 1.0 / ((1.0 + e) * gsum)
    w2 = e / ((1.0 + e) * gsum)
    pick1 = col == i1
    pick2 = col == i2
    both = (pick1 | pick2).astype(F32)
    ri = lax.broadcasted_iota(jnp.int32, (tm, tm), 0)
    ci = lax.broadcasted_iota(jnp.int32, (tm, tm), 1)
    before = (ci < ri).astype(BF16)
    cum = _dot(before, both.astype(BF16)) + carry_scr[...]
    r1 = jnp.sum(jnp.where(pick1, cum, 0.0), axis=1, keepdims=True)
    r2 = jnp.sum(jnp.where(pick2, cum, 0.0), axis=1, keepdims=True)
    carry_scr[...] = carry_scr[...] + jnp.sum(both, axis=0, keepdims=True)
    cnt_ref[...] = jnp.broadcast_to(carry_scr[...], cnt_ref.shape)
    info = jnp.where(col == 0, (i1 - ROUTE_COL).astype(F32), 0.0)
    info = jnp.where(col == 1, (i2 - ROUTE_COL).astype(F32), info)
    info = jnp.where(col == 2, w1, info)
    info = jnp.where(col == 3, w2, info)
    info = jnp.where(col == 4, r1, info)
    info_ref[...] = jnp.where(col == 5, r2, info)


def _route(y2d, g, w_rg, b_rg, w_re, b_re, tm):
    n, d = y2d.shape
    wr = jnp.concatenate([w_rg, w_re, jnp.zeros((d, LANE - N_GROUPS - N_EXPERTS), F32)], axis=1).astype(BF16)
    bias = jnp.concatenate([b_rg, b_re, jnp.zeros((LANE - N_GROUPS - N_EXPERTS,), F32)])[None, :]
    rows = lambda i: (i, 0)
    fixed = lambda i: (0, 0)
    return pl.pallas_call(
        functools.partial(_route_kernel, tm=tm),
        grid=(n // tm,),
        in_specs=[pl.BlockSpec((tm, d), rows), pl.BlockSpec((1, d), fixed),
                  pl.BlockSpec((d, LANE), fixed), pl.BlockSpec((1, LANE), fixed)],
        out_specs=[pl.BlockSpec((tm * SPLIT, LANE), rows), pl.BlockSpec((tm, LANE), rows),
                   pl.BlockSpec((8, LANE), fixed)],
        out_shape=[jax.ShapeDtypeStruct((n * SPLIT, LANE), F32), jax.ShapeDtypeStruct((n, LANE), F32),
                   jax.ShapeDtypeStruct((8, LANE), F32)],
        scratch_shapes=[pltpu.VMEM((1, LANE), F32)],
        compiler_params=_cparams(("arbitrary",)),
        name="moe_route",
    )(y2d, g, wr, bias)


def _expert_kernel(te_ref, nu_ref, src_ref, h_hbm, rw_ref, wg_ref, wu_ref, wd_ref, o_ref,
                   xbuf, wg_b, wu_b, wd_b, sem, *, tme):
    i = pl.program_id(0)
    slot = i % 2

    def row_copy(tile, r, buf_slot):
        src_row = src_ref[tile * tme + r]
        return pltpu.make_async_copy(h_hbm.at[pl.ds(pl.multiple_of(src_row * SPLIT, SPLIT), SPLIT)],
                                     xbuf.at[buf_slot, pl.ds(pl.multiple_of(r * SPLIT, SPLIT), SPLIT)],
                                     sem.at[buf_slot])

    def fetch(tile, buf_slot):
        def body(r, carry):
            row_copy(tile, r, buf_slot).start()
            return carry
        lax.fori_loop(0, tme, body, 0, unroll=DMA_UNROLL)

    @pl.when(i == 0)
    def _():
        fetch(0, 0)

    @pl.when(i + 1 < nu_ref[0])
    def _():
        fetch(i + 1, 1 - slot)

    @pl.when(i < nu_ref[0])
    def _():
        prev = te_ref[jnp.maximum(i - 1, 0)]

        @pl.when((i == 0) | (te_ref[i] != prev))
        def _():
            wg_b[...] = wg_ref[0].astype(BF16)
            wu_b[...] = wu_ref[0].astype(BF16)
            wd_b[...] = wd_ref[0].astype(BF16)

        def wait_body(r, carry):
            row_copy(i, r, slot).wait()
            return carry
        lax.fori_loop(0, tme, wait_body, 0, unroll=DMA_UNROLL)

        x = _load_split(xbuf.at[slot], tme).astype(BF16)
        a = _dot(x, wg_b[...])
        u = _dot(x, wu_b[...])
        hid = a * jax.nn.sigmoid(a) * u * rw_ref[...]
        _store_split(o_ref, _dot(hid.astype(BF16), wd_b[...]), tme)

    @pl.when(i >= nu_ref[0])
    def _():
        o_ref[...] = jnp.zeros(o_ref.shape, F32)


def _experts(h, src, roww, tile_expert, n_used, w_gate, w_up, w_down, tme):
    p = src.shape[0]
    d = D_MODEL
    return pl.pallas_call(
        functools.partial(_expert_kernel, tme=tme),
        grid_spec=pltpu.PrefetchScalarGridSpec(
            num_scalar_prefetch=3, grid=(p // tme,),
            in_specs=[pl.BlockSpec(memory_space=pl.ANY),
                      pl.BlockSpec((tme, 1), lambda i, te, nu, sr: (i, 0)),
                      pl.BlockSpec((1, d, D_EXPERT), lambda i, te, nu, sr: (te[i], 0, 0)),
                      pl.BlockSpec((1, d, D_EXPERT), lambda i, te, nu, sr: (te[i], 0, 0)),
                      pl.BlockSpec((1, D_EXPERT, d), lambda i, te, nu, sr: (te[i], 0, 0))],
            out_specs=pl.BlockSpec((tme * SPLIT, LANE), lambda i, te, nu, sr: (i, 0)),
            scratch_shapes=[pltpu.VMEM((2, tme * SPLIT, LANE), F32),
                            pltpu.VMEM((d, D_EXPERT), BF16), pltpu.VMEM((d, D_EXPERT), BF16),
                            pltpu.VMEM((D_EXPERT, d), BF16), pltpu.SemaphoreType.DMA((2,))]),
        out_shape=jax.ShapeDtypeStruct((p * SPLIT, LANE), F32),
        compiler_params=_cparams(("arbitrary",)),
        name="moe_experts",
    )(tile_expert, n_used, src, h, roww, w_gate, w_up, w_down)


def _combine_kernel(d1_ref, d2_ref, y_ref, ys_hbm, g_ref, o_ref, a_buf, b_buf, sem, *, tm):
    base = pl.program_id(0) * tm

    def copy(r, row, buf):
        return pltpu.make_async_copy(ys_hbm.at[pl.ds(pl.multiple_of(row * SPLIT, SPLIT), SPLIT)],
                                     buf.at[pl.ds(pl.multiple_of(r * SPLIT, SPLIT), SPLIT)], sem)

    def issue(r, carry):
        copy(r, d1_ref[base + r], a_buf).start()
        copy(r, d2_ref[base + r], b_buf).start()
        return carry

    lax.fori_loop(0, tm, issue, 0, unroll=DMA_UNROLL)

    def drain(r, carry):
        copy(r, 0, a_buf).wait()
        copy(r, 0, b_buf).wait()
        return carry

    lax.fori_loop(0, tm, drain, 0, unroll=DMA_UNROLL)
    y = y_ref[...] + _load_split(a_buf, tm) + _load_split(b_buf, tm)
    o_ref[...] = y * lax.rsqrt(jnp.mean(y * y, axis=-1, keepdims=True) + EPS) * g_ref[...]


def _combine(y2d, ys, d1, d2, g, tm):
    n, d = y2d.shape
    return pl.pallas_call(
        functools.partial(_combine_kernel, tm=tm),
        grid_spec=pltpu.PrefetchScalarGridSpec(
            num_scalar_prefetch=2, grid=(n // tm,),
            in_specs=[pl.BlockSpec((tm, d), lambda i, a, b: (i, 0)),
                      pl.BlockSpec(memory_space=pl.ANY),
                      pl.BlockSpec((1, d), lambda i, a, b: (0, 0))],
            out_specs=pl.BlockSpec((tm, d), lambda i, a, b: (i, 0)),
            scratch_shapes=[pltpu.VMEM((tm * SPLIT, LANE), F32), pltpu.VMEM((tm * SPLIT, LANE), F32),
                            pltpu.SemaphoreType.DMA(())]),
        out_shape=jax.ShapeDtypeStruct((n, d), F32),
        compiler_params=_cparams(("arbitrary",)),
        name="moe_combine",
    )(d1, d2, y2d, ys, g)


def _moe_final(y2d, ffn_g, w_rg, b_rg, w_re, b_re, w_gate, w_up, w_down, final_g, tm, tme):
    n, d = y2d.shape
    h, info, cnt = _route(y2d, ffn_g, w_rg, b_rg, w_re, b_re, tm)
    e1 = info[:, 0].astype(jnp.int32)
    e2 = info[:, 1].astype(jnp.int32)
    counts = cnt[0, ROUTE_COL:ROUTE_COL + N_EXPERTS].astype(jnp.int32)
    padded = (counts + tme - 1) // tme * tme
    ends = jnp.cumsum(padded)
    offs = ends - padded
    d1 = offs[e1] + info[:, 4].astype(jnp.int32)
    d2 = offs[e2] + info[:, 5].astype(jnp.int32)
    p = 2 * n + N_EXPERTS * tme
    tok = jnp.arange(n, dtype=jnp.int32)
    src = jnp.zeros((p,), jnp.int32).at[d1].set(tok).at[d2].set(tok)
    roww = jnp.zeros((p,), F32).at[d1].set(info[:, 2]).at[d2].set(info[:, 3])
    n_used = (ends[-1] // tme).astype(jnp.int32).reshape(1)
    tile_start = jnp.arange(p // tme, dtype=jnp.int32) * tme
    tile_expert = jnp.sum((ends[None, :] <= tile_start[:, None]).astype(jnp.int32), axis=1)
    last = jnp.sum((ends <= ends[-1] - 1).astype(jnp.int32))
    tile_expert = jnp.minimum(tile_expert, last).astype(jnp.int32)
    ys = _experts(h, src, roww[:, None], tile_expert, n_used, w_gate, w_up, w_down, tme)
    return _combine(y2d, ys, d1, d2, final_g, tm)


GROUP_ROWS = 8
SLC_LANES = 384
SEL_KEYS = 1024
WIN_KEYS = 640


def _sample_cattn_kernel(q_ref, kc_ref, vc_ref, b_ref, ov_ref, o_ref, sel_ref, *, n_cmp, n_slc):
    q = (q_ref[0, 0] * (NSA_HEAD_DIM ** -0.5)).astype(BF16)
    s = _dot_nt(q, kc_ref[0, 0].astype(BF16)) + b_ref[0]
    valid = lax.broadcasted_iota(jnp.int32, s.shape, 1) < n_cmp
    s = jnp.where(valid, s, NEG_BIG)
    e = jnp.exp(s - jnp.max(s, axis=1, keepdims=True))
    p = jnp.where(valid, e / jnp.sum(e, axis=1, keepdims=True), 0.0)
    o_ref[0, 0] = _dot(p.astype(BF16), vc_ref[0, 0].astype(BF16))
    pg = jnp.where(lax.broadcasted_iota(jnp.int32, p.shape, 0) < NSA_GROUP, p, 0.0)
    pc = jnp.broadcast_to(jnp.sum(pg, axis=0, keepdims=True), pg.shape)
    score = _dot(pc.astype(BF16), ov_ref[...])[0:1, :]
    blk = lax.broadcasted_iota(jnp.int32, score.shape, 1)
    cur = n_slc - 1
    forced = (blk == 0) | (blk == cur) | (blk == cur - 1)
    score = jnp.where(forced, FORCE_SCORE, score)
    score = jnp.where(blk > cur, -jnp.inf, score)
    ri = lax.broadcasted_iota(jnp.int32, (SLC_LANES, SLC_LANES), 0)
    ci = lax.broadcasted_iota(jnp.int32, (SLC_LANES, SLC_LANES), 1)
    sb = jnp.broadcast_to(score, (SLC_LANES, SLC_LANES))
    col = jnp.sum(jnp.where(ri == ci, sb, 0.0), axis=1, keepdims=True)
    beats = (col > sb) | ((col == sb) & (ri < ci))
    rank = jnp.sum(beats.astype(jnp.int32), axis=0, keepdims=True)
    sel_ref[0, 0] = jnp.broadcast_to((rank < SLC_TOPN).astype(F32), (GROUP_ROWS, SLC_LANES))


def _sample_cattn(q8, kvc, bias_cs, n_cmp, n_slc):
    bsz, _, nch, _ = kvc.shape
    c0 = np.arange(nch) * CMP_STRIDE
    s0 = np.arange(SLC_LANES) * SLC_BLOCK
    ov = np.minimum(c0[:, None] + CMP_BLOCK, s0[None, :] + SLC_BLOCK) - np.maximum(c0[:, None], s0[None, :])
    ov = np.clip(ov, 0, None).astype(np.float32) / CMP_BLOCK
    ov[n_cmp:, :] = 0.0
    ov[:, n_slc:] = 0.0
    return pl.pallas_call(
        functools.partial(_sample_cattn_kernel, n_cmp=n_cmp, n_slc=n_slc),
        grid=(bsz, NSA_KV_HEADS),
        in_specs=[pl.BlockSpec((1, 1, GROUP_ROWS, NSA_HEAD_DIM), lambda b, h: (b, h, 0, 0)),
                  pl.BlockSpec((1, 1, nch, NSA_HEAD_DIM), lambda b, h: (b, h, 0, 0)),
                  pl.BlockSpec((1, 1, nch, NSA_HEAD_DIM), lambda b, h: (b, NSA_KV_HEADS + h, 0, 0)),
                  pl.BlockSpec((1, GROUP_ROWS, nch), lambda b, h: (h, 0, 0)),
                  pl.BlockSpec((nch, SLC_LANES), lambda b, h: (0, 0))],
        out_specs=[pl.BlockSpec((1, 1, GROUP_ROWS, NSA_HEAD_DIM), lambda b, h: (b, h, 0, 0)),
                   pl.BlockSpec((1, 1, GROUP_ROWS, SLC_LANES), lambda b, h: (b, h, 0, 0))],
        out_shape=[jax.ShapeDtypeStruct((bsz, NSA_KV_HEADS, GROUP_ROWS, NSA_HEAD_DIM), F32),
                   jax.ShapeDtypeStruct((bsz, NSA_KV_HEADS, GROUP_ROWS, SLC_LANES), F32)],
        compiler_params=_cparams(("arbitrary", "arbitrary")),
        name="sample_cmp_attn",
    )(q8, kvc, kvc, bias_cs, jnp.asarray(ov, BF16))


def _sample_attn_kernel(pg_ref, hf_ref, q_ref, cache_hbm, snew_ref, bs_ref, win_ref, wnew_ref, bw_ref,
                        os_ref, ow_ref, kt_buf, vt_buf, sem, *, n_gather, wlen):
    b = pl.program_id(0)
    rows_per_page = 2 * NSA_KV_HEADS * NSA_HEAD_DIM

    def page_copy(h, s, c, buf):
        page = pg_ref[(b * NSA_KV_HEADS + h) * n_gather + s]
        start = pl.multiple_of(page * rows_per_page + (c * NSA_KV_HEADS + h) * NSA_HEAD_DIM, NSA_HEAD_DIM)
        return pltpu.make_async_copy(cache_hbm.at[pl.ds(start, NSA_HEAD_DIM)],
                                     buf.at[h, :, pl.ds(s * PAGE_SIZE, PAGE_SIZE)], sem)

    for h in range(NSA_KV_HEADS):
        for s in range(n_gather):
            page_copy(h, s, 0, kt_buf).start()
            page_copy(h, s, 1, vt_buf).start()

    def attend(q, kts, vts, bias, valid):
        s = jnp.concatenate([_dot(q, kt.astype(BF16)) for kt in kts], axis=1) + bias
        s = jnp.where(valid, s, NEG_BIG)
        e = jnp.exp(s - jnp.max(s, axis=1, keepdims=True))
        p = jnp.where(valid, e / jnp.sum(e, axis=1, keepdims=True), 0.0).astype(BF16)
        out = None
        lo = 0
        for vt in vts:
            n = vt.shape[1]
            part = _dot_nt(p[:, lo:lo + n], vt.astype(BF16))
            out = part if out is None else out + part
            lo += n
        return out

    def head_rows(ref, c, h):
        r0 = (c * NSA_KV_HEADS + h) * NSA_HEAD_DIM
        return ref[0, r0:r0 + NSA_HEAD_DIM, :]

    widx = lax.broadcasted_iota(jnp.int32, (GROUP_ROWS, wlen + PAGE_SIZE), 1)
    wvalid = (widx >= wlen + 1 - WINDOW) & (widx <= wlen)
    for h in range(NSA_KV_HEADS):
        q = (q_ref[0, h] * (NSA_HEAD_DIM ** -0.5)).astype(BF16)
        ow_ref[0, h] = attend(q, [head_rows(win_ref, 0, h), head_rows(wnew_ref, 0, h)],
                              [head_rows(win_ref, 1, h), head_rows(wnew_ref, 1, h)], bw_ref[h], wvalid)

    for h in range(NSA_KV_HEADS):
        for s in range(n_gather):
            page_copy(h, s, 0, kt_buf).wait()
            page_copy(h, s, 1, vt_buf).wait()

    keys = n_gather * PAGE_SIZE
    lane = lax.broadcasted_iota(jnp.int32, (GROUP_ROWS, keys + PAGE_SIZE), 1)
    slot = lane >> 7
    lane_half = (lane >> 6) & 1
    for h in range(NSA_KV_HEADS):
        want = jnp.full(lane.shape, -1, jnp.int32)
        for s in range(n_gather):
            want = jnp.where(slot == s, hf_ref[(b * NSA_KV_HEADS + h) * n_gather + s], want)
        svalid = (lane_half == want) | (lane == keys)
        q = (q_ref[0, h] * (NSA_HEAD_DIM ** -0.5)).astype(BF16)
        os_ref[0, h] = attend(q, [kt_buf[h], head_rows(snew_ref, 0, h)],
                              [vt_buf[h], head_rows(snew_ref, 1, h)], bs_ref[0, h], svalid)


def _sample_attn(pages, halves, q8, cache_t, snew_t, bias_sel, win_t, wnew_t, bias_w, n_gather):
    bsz = q8.shape[0]
    wlen = win_t.shape[2]
    keys = n_gather * PAGE_SIZE
    rows = 2 * NSA_KV_HEADS * NSA_HEAD_DIM
    qspec = pl.BlockSpec((1, NSA_KV_HEADS, GROUP_ROWS, NSA_HEAD_DIM), lambda b, pg, hf: (b, 0, 0, 0))
    newspec = pl.BlockSpec((1, rows, PAGE_SIZE), lambda b, pg, hf: (b, 0, 0))
    return pl.pallas_call(
        functools.partial(_sample_attn_kernel, n_gather=n_gather, wlen=wlen),
        grid_spec=pltpu.PrefetchScalarGridSpec(
            num_scalar_prefetch=2, grid=(bsz,),
            in_specs=[qspec,
                      pl.BlockSpec(memory_space=pl.ANY),
                      newspec,
                      pl.BlockSpec((1, NSA_KV_HEADS, GROUP_ROWS, keys + PAGE_SIZE), lambda b, pg, hf: (b, 0, 0, 0)),
                      pl.BlockSpec((1, rows, wlen), lambda b, pg, hf: (b, 0, 0)),
                      newspec,
                      pl.BlockSpec((NSA_KV_HEADS, GROUP_ROWS, wlen + PAGE_SIZE), lambda b, pg, hf: (0, 0, 0))],
            out_specs=[qspec, qspec],
            scratch_shapes=[pltpu.VMEM((NSA_KV_HEADS, NSA_HEAD_DIM, keys), F32),
                            pltpu.VMEM((NSA_KV_HEADS, NSA_HEAD_DIM, keys), F32),
                            pltpu.SemaphoreType.DMA(())]),
        out_shape=[jax.ShapeDtypeStruct(q8.shape, F32), jax.ShapeDtypeStruct(q8.shape, F32)],
        compiler_params=_cparams(("arbitrary",)),
        name="sample_slc_win_attn",
    )(pages, halves, q8, cache_t, snew_t, bias_sel, win_t, wnew_t, bias_w)


PAGES_PER_STEP = 16


def _paged_pq_kernel(pt_ref, cache_hbm, pos_ref, w_ref, o_ref, pbuf, tok_scr, sem):
    b = pl.program_id(0)
    g = pl.program_id(1)
    ng = pl.num_programs(1)
    lin = b * ng + g
    slot = lin % 2
    rows_per_page = 2 * NSA_KV_HEADS * NSA_HEAD_DIM
    chunks = PAGES_PER_STEP * (PAGE_SIZE // CMP_STRIDE)

    def page_copy(step, p, buf_slot):
        sb = step // ng
        sg = step - sb * ng
        page = pt_ref[sb, sg * PAGES_PER_STEP + p]
        return pltpu.make_async_copy(cache_hbm.at[pl.ds(pl.multiple_of(page * rows_per_page, rows_per_page),
                                                        rows_per_page)],
                                     pbuf.at[buf_slot, p], sem.at[buf_slot])

    def fetch(step, buf_slot):
        for p in range(PAGES_PER_STEP):
            page_copy(step, p, buf_slot).start()

    @pl.when(lin == 0)
    def _():
        fetch(0, 0)

    @pl.when(lin + 1 < pl.num_programs(0) * ng)
    def _():
        fetch(lin + 1, 1 - slot)

    for p in range(PAGES_PER_STEP):
        page_copy(lin, p, slot).wait()

    for pair in range(NSA_KV_HEADS):
        for p in range(PAGES_PER_STEP):
            tok_scr[p * PAGE_SIZE:(p + 1) * PAGE_SIZE, :] = pbuf[slot, p, pair * LANE:(pair + 1) * LANE, :].T
        toks = [tok_scr[pl.ds(l, chunks, stride=CMP_STRIDE), :] for l in range(CMP_STRIDE)]
        for half in range(2):
            ch = 2 * pair + half
            c = ch // NSA_KV_HEADS
            lanes = slice(half * NSA_HEAD_DIM, (half + 1) * NSA_HEAD_DIM)
            chunk = jnp.concatenate([t[:, lanes] for t in toks], axis=1)
            for part in range(2):
                lhs = (chunk + pos_ref[c, part:part + 1, :]).astype(BF16)
                o_ref[0, ch, :, part * CMP_HIDDEN:(part + 1) * CMP_HIDDEN] = _dot(lhs, w_ref[c, part])


def _paged_pq(cache_t, page_table, cmp_pos, cmp_w1):
    db, n_pages = page_table.shape
    kdim = CMP_STRIDE * NSA_HEAD_DIM
    nch = n_pages * (PAGE_SIZE // CMP_STRIDE)
    tn = PAGES_PER_STEP * (PAGE_SIZE // CMP_STRIDE)
    rows_per_page = 2 * NSA_KV_HEADS * NSA_HEAD_DIM
    return pl.pallas_call(
        _paged_pq_kernel,
        grid_spec=pltpu.PrefetchScalarGridSpec(
            num_scalar_prefetch=1, grid=(db, n_pages // PAGES_PER_STEP),
            in_specs=[pl.BlockSpec(memory_space=pl.ANY),
                      pl.BlockSpec((2, 2, kdim), lambda b, g, pt: (0, 0, 0)),
                      pl.BlockSpec((2, 2, kdim, CMP_HIDDEN), lambda b, g, pt: (0, 0, 0, 0))],
            out_specs=pl.BlockSpec((1, 8, tn, 2 * CMP_HIDDEN), lambda b, g, pt: (b, 0, g, 0)),
            scratch_shapes=[pltpu.VMEM((2, PAGES_PER_STEP, rows_per_page, PAGE_SIZE), F32),
                            pltpu.VMEM((PAGES_PER_STEP * PAGE_SIZE, LANE), F32),
                            pltpu.SemaphoreType.DMA((2,))]),
        out_shape=jax.ShapeDtypeStruct((db, 8, nch, 2 * CMP_HIDDEN), F32),
        compiler_params=_cparams(("arbitrary", "arbitrary")),
        name="cmp_pq_paged",
    )(page_table, cache_t, cmp_pos.reshape(2, 2, kdim), cmp_w1.reshape(2, 2, kdim, CMP_HIDDEN).astype(BF16))


def _pad_rows(a, n):
    return jnp.concatenate([a, jnp.zeros((n - a.shape[0],) + a.shape[1:], a.dtype)], axis=0)


def _sample_mixer(x_sample, cache_cmp, cache_slc, cache_win, st_c, st_n, st_m, st_conv, page_table,
                  norm_g, wb, gate_bias, conv_w, ml_norm_g, cmp_pos, cmp_w1, cmp_w2, nsa_norm_g, w_out_b, rel_bias):
    db = x_sample.shape[0]
    n_pages = page_table.shape[1]
    past = n_pages * PAGE_SIZE
    tok = 16
    x16 = _pad_rows(x_sample.reshape(db, D_MODEL), tok)
    u = _proj(x16, norm_g, wb, tok, 512)
    small = u[:, COL_SMALL:COL_SMALL + LANE]

    T = 128
    useq = jnp.zeros((db, T, 4 * D_ML), F32)
    useq = useq.at[:, T - CONV_W:T - 1, 0:2 * D_ML].set(st_conv)
    useq = useq.at[:, T - 1, :].set(u[:db, 0:4 * D_ML])
    sseq = jnp.zeros((db, T, LANE), F32).at[:, :T - 1, LANE - 1].set(1.0)
    sseq = sseq.at[:, T - 1, :].set(small[:db])
    y_seq, conv_n, c_n, n_n, m_n = _mlstm(
        useq.reshape(db * T, 4 * D_ML), sseq.reshape(db * T, LANE), jnp.zeros((db, CONV_W - 1, 2 * D_ML), F32),
        st_c, st_n, st_m, conv_w, gate_bias, ml_norm_g, db, T, T)
    y_ml = y_seq.reshape(db, T, D_ML)[:, T - 1]

    n_pool = cache_cmp.shape[0]
    cmp_t = cache_cmp.transpose(0, 2, 3, 4, 1).reshape(n_pool * 2 * D_KV, PAGE_SIZE)
    kvc = _cmp_hid(_paged_pq(cmp_t, page_table, cmp_pos, cmp_w1), cmp_w2)
    nch = past // CMP_STRIDE
    n_cmp = (past + 1) // CMP_STRIDE - CMP_BLOCK // CMP_STRIDE + 1
    n_slc = -(-(past + 1) // SLC_BLOCK)
    q = u[:db, COL_QNSA:COL_QNSA + D_NSA].reshape(db, NSA_KV_HEADS, NSA_GROUP, NSA_HEAD_DIM)
    q8 = jnp.concatenate([q, jnp.zeros_like(q)], axis=2)
    bd = _bias_by_distance(rel_bias, past + 1).reshape(NSA_KV_HEADS, NSA_GROUP, past + 1)
    pad_g = lambda a: jnp.concatenate([a, jnp.zeros_like(a)], axis=1)
    dist_c = np.clip(past - (np.arange(nch) * CMP_STRIDE + CMP_BLOCK - 1), 0, None)
    o_c8, sel = _sample_cattn(q8, kvc, pad_g(bd[:, :, dist_c]), n_cmp, n_slc)

    mask = sel[:, :, 0, :n_slc] > 0.5
    idx = jnp.sort(jnp.where(mask, jnp.arange(n_slc, dtype=jnp.int32), jnp.int32(1 << 20)), axis=-1)
    n_gather = SLC_TOPN - 1
    idx = idx[..., :n_gather]
    pages_per_block = PAGE_SIZE // SLC_BLOCK
    logical_page = idx // pages_per_block
    pages = jnp.take_along_axis(page_table[:, None, :], logical_page, axis=2).reshape(-1).astype(jnp.int32)
    halves = (idx % pages_per_block).reshape(-1).astype(jnp.int32)
    kpos = (logical_page[..., None] * PAGE_SIZE + jnp.arange(PAGE_SIZE, dtype=jnp.int32)).reshape(db, NSA_KV_HEADS, -1)
    hh = jnp.arange(NSA_KV_HEADS)[None, :, None, None]
    gg = jnp.arange(NSA_GROUP)[None, None, :, None]
    bias_sel = bd[hh, gg, (past - kpos)[:, :, None, :]]
    bias_sel = jnp.concatenate([bias_sel, jnp.broadcast_to(bd[None, :, :, 0:1], (db, NSA_KV_HEADS, NSA_GROUP, 1)),
                                jnp.zeros((db, NSA_KV_HEADS, NSA_GROUP, PAGE_SIZE - 1), F32)], axis=-1)
    bias_sel = jnp.concatenate([bias_sel, jnp.zeros_like(bias_sel)], axis=2)
    wlen = cache_win.shape[1]
    dist_w = np.clip(wlen - np.arange(wlen + PAGE_SIZE), 0, None)
    kvs_new = u[:db, COL_KVS:COL_KVS + 2 * D_KV]
    kvw_new = u[:db, COL_KVW:COL_KVW + 2 * D_KV]
    lane_pad = lambda a: jnp.pad(a[:, :, None], ((0, 0), (0, 0), (0, PAGE_SIZE - 1)))
    slc_t = cache_slc.transpose(0, 2, 3, 4, 1).reshape(n_pool * 2 * D_KV, PAGE_SIZE)
    win_t = cache_win.transpose(0, 2, 3, 4, 1).reshape(db, 2 * D_KV, wlen)
    o_s8, o_w8 = _sample_attn(pages, halves, q8, slc_t, lane_pad(kvs_new), bias_sel, win_t, lane_pad(kvw_new),
                              pad_g(bd[:, :, dist_w]), n_gather)
    win2d = cache_win.reshape(db, wlen, 2 * D_KV)

    heads = lambda o: _pad_rows(o[:, :, :NSA_GROUP, :].reshape(db, D_NSA), tok)
    y = _outproj(_pad_rows(y_ml, tok), heads(o_c8), heads(o_s8), heads(o_w8), small, nsa_norm_g, w_out_b, x16, tok, 512)
    kvshape = (1, db, 1, 2, NSA_KV_HEADS, NSA_HEAD_DIM)
    new_win = jnp.concatenate([win2d[:, 1:], kvw_new[:, None, :]], axis=1)
    states = (u[:db, COL_KVC:COL_KVC + 2 * D_KV].reshape(kvshape), kvs_new.reshape(kvshape),
              new_win.reshape((1, db, wlen, 2, NSA_KV_HEADS, NSA_HEAD_DIM)),
              c_n[None], n_n[None], m_n[None], conv_n[None])
    return y, states


def kernel(x_prompt, x_sample, cache_cmp_kv, cache_slc_kv, cache_win_kv, state_mlstm_C, state_mlstm_n,
           state_mlstm_m, state_conv, page_table, rel_bias, norm_mix_g, w_in, b_ig, b_fg, conv_w, ml_norm_g,
           cmp_pos, cmp_w1, cmp_w2, nsa_norm_g, w_out, norm_ffn_g, w_router_grp, b_router_grp, w_router_exp,
           b_router_exp, w_gate, w_up, w_down, norm_final_g):
    B, S, D = x_prompt.shape
    wb = _reorder_w_in(w_in[0])
    gate_bias = jnp.zeros((1, LANE), F32).at[0, 0:ML_HEADS].set(b_ig[0]).at[0, ML_HEADS:2 * ML_HEADS].set(b_fg[0])
    w_out_b = w_out[0].astype(BF16)
    yp, st_p = _prompt_mixer(x_prompt, norm_mix_g, wb, gate_bias, conv_w[0], ml_norm_g, cmp_pos[0], cmp_w1[0],
                             cmp_w2[0], nsa_norm_g, w_out_b, rel_bias)
    ys, st_s = _sample_mixer(x_sample, cache_cmp_kv[0], cache_slc_kv[0], cache_win_kv[0], state_mlstm_C[0],
                             state_mlstm_n[0], state_mlstm_m[0], state_conv[0], page_table, norm_mix_g, wb, gate_bias,
                             conv_w[0], ml_norm_g, cmp_pos[0], cmp_w1[0], cmp_w2[0], nsa_norm_g, w_out_b, rel_bias)
    moe_w = (norm_ffn_g, w_router_grp[0], b_router_grp[0], w_router_exp[0], b_router_exp[0],
             w_gate[0], w_up[0], w_down[0], norm_final_g[None, :])
    DB, L, _ = x_sample.shape
    out_p = _moe_final(yp.reshape(B * S, D), *moe_w, 256, 128).reshape(B, S, D)
    out_s = _moe_final(_pad_rows(ys, 128), *moe_w, 128, 16)[:DB].reshape(DB, L, D)
    outs = [out_p, out_s]
    for a, b in zip(st_p, st_s):
        outs += [a, b]
    return tuple(outs)


def _prompt_mixer(x_prompt, norm_g, wb, gate_bias, conv_w, ml_norm_g, cmp_pos, cmp_w1, cmp_w2, nsa_norm_g,
                  w_out_b, rel_bias):
    B, S, D = x_prompt.shape
    x2d = x_prompt.reshape(B * S, D)
    tm = min(1024, B * S)
    u = _proj(x2d, norm_g, wb, tm, 512)
    small = u[:, COL_SMALL:COL_SMALL + LANE]
    y_ml, conv_n, c_n, n_n, m_n = _mlstm(
        u, small, jnp.zeros((B, CONV_W - 1, 2 * D_ML), F32),
        jnp.zeros((B, ML_HEADS, ML_HEAD_DIM, ML_HEAD_DIM), F32), jnp.zeros((B, ML_HEADS, ML_HEAD_DIM), F32),
        jnp.full((B, ML_HEADS), -jnp.inf, F32), conv_w, gate_bias, ml_norm_g, B, S, 256)
    kv_c = u[:, COL_KVC:COL_KVC + 2 * D_KV]
    kv_s = u[:, COL_KVS:COL_KVS + 2 * D_KV]
    kv_w = u[:, COL_KVW:COL_KVW + 2 * D_KV]
    nch = S // CMP_STRIDE
    kvc = _compress(u, COL_KVC // (2 * D_KV), B, S, cmp_pos, cmp_w1, cmp_w2, min(S, 4096))
    tq_c = min(S, 256)
    o_c, sel_t = _cattn(u, kvc, rel_bias, B, S, tq_c)
    o_s = _flash(u, COL_KVS, rel_bias, sel_t, B, S)
    o_w = _flash(u, COL_KVW, rel_bias, None, B, S)
    y = _outproj(y_ml, o_c, o_s, o_w, small, nsa_norm_g, w_out_b, x2d, tm, 512)
    kvshape = (1, B, S, 2, NSA_KV_HEADS, NSA_HEAD_DIM)
    win = min(WINDOW, S)
    states = (kv_c.reshape(kvshape), kv_s.reshape(kvshape), kv_w.reshape(kvshape)[:, :, S - win:],
              c_n[None], n_n[None], m_n[None], conv_n[None])
    return y.reshape(B, S, D), states
```

```python
import functools
import math

import numpy as np
import jax
import jax.numpy as jnp
from jax import lax
from jax.experimental import pallas as pl
from jax.experimental.pallas import tpu as pltpu

F32 = jnp.float32
BF16 = jnp.bfloat16

D_MODEL = 2048
ML_HEADS = 4
ML_HEAD_DIM = 256
D_ML = 1024
CONV_W = 4
NSA_HEADS = 16
NSA_HEAD_DIM = 64
D_NSA = 1024
NSA_KV_HEADS = 4
NSA_GROUP = 4
D_KV = 256
CMP_BLOCK = 32
CMP_STRIDE = 16
CMP_HIDDEN = 256
SLC_BLOCK = 64
SLC_TOPN = 16
WINDOW = 512
N_BUCKETS = 32
MAX_DISTANCE = 2048
N_GROUPS = 4
EXPERTS_PER_GROUP = 8
N_EXPERTS = 32
D_EXPERT = 512
PAGE_SIZE = 128
EPS = 1e-6
NEG_BIG = -1e30
FORCE_SCORE = 1e4

LANE = 128
COL_QML, COL_KML, COL_VML, COL_OML = 0, 1024, 2048, 3072
COL_QNSA = 4096
COL_KVC, COL_KVS, COL_KVW = 5120, 5632, 6144
COL_SMALL = 6656
N_PROJ = 7168
VMEM_LIMIT = 56 * 1024 * 1024


def _cparams(sem, vmem=VMEM_LIMIT):
    return pltpu.CompilerParams(dimension_semantics=sem, vmem_limit_bytes=vmem)


def _split2(x):
    hi = x.astype(BF16)
    lo = (x - hi.astype(F32)).astype(BF16)
    return hi, lo


def _split3(x):
    hi = x.astype(BF16)
    r = x - hi.astype(F32)
    mid = r.astype(BF16)
    lo = (r - mid.astype(F32)).astype(BF16)
    return hi, mid, lo


def _dot(a, b):
    return jnp.dot(a, b, preferred_element_type=F32)


def _dot_nt(a, b):
    return lax.dot_general(a, b, (((1,), (1,)), ((), ())), preferred_element_type=F32)


def _dot_tn(a, b):
    return lax.dot_general(a, b, (((0,), (0,)), ((), ())), preferred_element_type=F32)


def _proj_kernel(x_ref, g_ref, w_ref, o_ref, h_scr):
    @pl.when(pl.program_id(1) == 0)
    def _():
        x = x_ref[...]
        ms = jnp.mean(x * x, axis=-1, keepdims=True)
        h_scr[...] = (x * lax.rsqrt(ms + EPS) * g_ref[...]).astype(BF16)

    o_ref[...] = _dot(h_scr[...], w_ref[...])


def _proj(x2d, g, wb, tm, tn):
    n, d = x2d.shape
    nc = wb.shape[1]
    return pl.pallas_call(
        _proj_kernel,
        grid=(n // tm, nc // tn),
        in_specs=[pl.BlockSpec((tm, d), lambda i, j: (i, 0)),
                  pl.BlockSpec((1, d), lambda i, j: (0, 0)),
                  pl.BlockSpec((d, tn), lambda i, j: (0, j))],
        out_specs=pl.BlockSpec((tm, tn), lambda i, j: (i, j)),
        out_shape=jax.ShapeDtypeStruct((n, nc), F32),
        scratch_shapes=[pltpu.VMEM((tm, d), BF16)],
        compiler_params=_cparams(("arbitrary", "arbitrary")),
        name="proj",
    )(x2d, g, wb)


def _reorder_w_in(w_in):
    big = w_in[:, :4 * D_ML]
    small_a = w_in[:, 4 * D_ML:4 * D_ML + 2 * ML_HEADS]
    rest = w_in[:, 4 * D_ML + 2 * ML_HEADS:]
    q_and_cmp = rest[:, :D_NSA + 2 * D_KV]
    gate = rest[:, D_NSA + 6 * D_KV:]
    d = w_in.shape[0]

    def per_head(w):
        return w.reshape(d, 2, NSA_KV_HEADS, NSA_HEAD_DIM).transpose(0, 2, 1, 3).reshape(d, 2 * D_KV)

    kv_s = per_head(rest[:, D_NSA + 2 * D_KV:D_NSA + 4 * D_KV])
    kv_w = per_head(rest[:, D_NSA + 4 * D_KV:D_NSA + 6 * D_KV])
    pad = jnp.zeros((d, N_PROJ - COL_SMALL - 2 * ML_HEADS - 3 * NSA_HEADS), w_in.dtype)
    return jnp.concatenate([big, q_and_cmp, kv_s, kv_w, small_a, gate, pad], axis=1).astype(BF16)


def _kv_from_per_head(kv2d):
    n = kv2d.shape[0]
    return kv2d.reshape(n, NSA_KV_HEADS, 2, NSA_HEAD_DIM).transpose(0, 2, 1, 3).reshape(n, 2 * D_KV)


def _log_sigmoid(x):
    return jnp.minimum(x, 0.0) - jnp.log1p(jnp.exp(-jnp.abs(x)))


def _mlstm_kernel(q_ref, k_ref, v_ref, o_ref, s_ref, cb_ref, c0_ref, n0_ref, m0_ref,
                  cw_ref, gb_ref, ng_ref,
                  y_ref, cbo_ref, co_ref, no_ref, mo_ref,
                  ext_scr, c_scr, n_scr, m_scr, *, T):
    c = pl.program_id(1)
    nc = pl.num_programs(1)

    @pl.when(c == 0)
    def _():
        ext_scr[0:8, :] = jnp.zeros((8, 2 * D_ML), F32)
        ext_scr[5:8, :] = cb_ref[0]
        c_scr[...] = c0_ref[0]
        n_scr[...] = n0_ref[0]
        m_scr[...] = m0_ref[0]

    ext_scr[8:8 + T, 0:D_ML] = q_ref[...]
    ext_scr[8:8 + T, D_ML:2 * D_ML] = k_ref[...]
    conv = ext_scr[5:5 + T, :] * cw_ref[0:1, :]
    for j in range(1, CONV_W):
        conv = conv + ext_scr[5 + j:5 + j + T, :] * cw_ref[j:j + 1, :]
    tail = ext_scr[8 + T - 3:8 + T, :]
    ext_scr[5:8, :] = tail
    cbo_ref[0] = tail
    qk = conv * jax.nn.sigmoid(conv)

    pre = s_ref[...] + gb_ref[...]
    col = lax.broadcasted_iota(jnp.int32, pre.shape, 1)
    padrow = s_ref[:, LANE - 1:LANE] > 0.5
    gates = jnp.where(col < ML_HEADS, pre, _log_sigmoid(pre))
    gates = jnp.where(padrow, jnp.where(col < ML_HEADS, NEG_BIG, 0.0), gates)
    g_r = gates.T
    ti = lax.broadcasted_iota(jnp.int32, (T, T), 0)
    si = lax.broadcasted_iota(jnp.int32, (T, T), 1)
    upper = (ti <= si).astype(BF16)
    g_fin = jnp.where(lax.broadcasted_iota(jnp.int32, g_r.shape, 0) < ML_HEADS, 0.0, g_r)
    hi, mid, lo = _split3(g_fin)
    cum_r = _dot(hi, upper) + _dot(mid, upper) + _dot(lo, upper)
    rowi = lax.broadcasted_iota(jnp.int32, g_r.shape, 0)
    a_r = jnp.where(rowi < ML_HEADS, g_r, cum_r)
    a_c = a_r.T
    causal = si <= ti

    for h in range(ML_HEADS):
        sl = slice(h * ML_HEAD_DIM, (h + 1) * ML_HEAD_DIM)
        q = qk[:, h * ML_HEAD_DIM:(h + 1) * ML_HEAD_DIM]
        k = qk[:, D_ML + h * ML_HEAD_DIM:D_ML + (h + 1) * ML_HEAD_DIM] * (ML_HEAD_DIM ** -0.5)
        v = v_ref[:, sl]
        ig_r = a_r[h:h + 1, :]
        b_r = a_r[ML_HEADS + h:ML_HEADS + h + 1, :]
        ig_c = a_c[:, h:h + 1]
        b_c = a_c[:, ML_HEADS + h:ML_HEADS + h + 1]
        m_prev = m_scr[h:h + 1, 0:1]
        logd = jnp.where(causal, b_c - b_r + ig_r, -jnp.inf)
        inter = b_c + m_prev
        m_t = jnp.maximum(inter, jnp.max(logd, axis=1, keepdims=True))
        w_intra = jnp.exp(logd - m_t)
        w_inter = jnp.exp(inter - m_t)
        qb = q.astype(BF16)
        kb = k.astype(BF16)
        vb = v.astype(BF16)
        sc = _dot_nt(qb, kb) * w_intra
        cmat = c_scr[h]
        nvec = n_scr[h:h + 1, :]
        num = _dot(sc.astype(BF16), vb) + w_inter * _dot(qb, cmat.astype(BF16))
        qn = jnp.sum(qb.astype(F32) * nvec.astype(BF16).astype(F32), axis=1, keepdims=True)
        den = jnp.sum(sc, axis=1, keepdims=True) + w_inter * qn
        hh = num / jnp.maximum(jnp.abs(den), jnp.exp(-m_t))
        m_new = m_t[T - 1:T, :]
        b_last = b_c[T - 1:T, :]
        w_s = jnp.exp(b_last - b_c + ig_c - m_new)
        decay = jnp.exp(b_last + m_prev - m_new)
        kw = k * w_s
        c_new = decay * cmat + _dot_tn(kw.astype(BF16), vb)
        n_new = decay * nvec + jnp.sum(kw, axis=0, keepdims=True)
        c_scr[h] = c_new
        n_scr[h:h + 1, :] = n_new
        m_scr[h:h + 1, :] = jnp.broadcast_to(m_new, (1, LANE))
        hn = hh * lax.rsqrt(jnp.mean(hh * hh, axis=1, keepdims=True) + EPS) * ng_ref[:, sl]
        y_ref[:, sl] = (hn * jax.nn.sigmoid(o_ref[:, sl])).astype(y_ref.dtype)

    @pl.when(c == nc - 1)
    def _():
        co_ref[0] = c_scr[...]
        no_ref[0] = n_scr[...]
        mo_ref[0] = m_scr[...]


def _mlstm(u, small, conv_buf, c0, n0, m0, conv_w, gate_bias, norm_g, batch, seq, T):
    nc = seq // T
    cb = D_ML // 1024
    m0b = jnp.broadcast_to(m0[:, :, None], (batch, ML_HEADS, LANE))
    m0b = jnp.concatenate([m0b, jnp.zeros((batch, 8 - ML_HEADS, LANE), F32)], axis=1)
    n0p = jnp.concatenate([n0, jnp.zeros((batch, 8 - ML_HEADS, ML_HEAD_DIM), F32)], axis=1)
    row = lambda b, c: (b * nc + c, 0)
    outs = pl.pallas_call(
        functools.partial(_mlstm_kernel, T=T),
        grid=(batch, nc),
        in_specs=[pl.BlockSpec((T, D_ML), lambda b, c: (b * nc + c, COL_QML // D_ML)),
                  pl.BlockSpec((T, D_ML), lambda b, c: (b * nc + c, COL_KML // D_ML)),
                  pl.BlockSpec((T, D_ML), lambda b, c: (b * nc + c, COL_VML // D_ML)),
                  pl.BlockSpec((T, D_ML), lambda b, c: (b * nc + c, COL_OML // D_ML)),
                  pl.BlockSpec((T, LANE), row),
                  pl.BlockSpec((1, CONV_W - 1, 2 * D_ML), lambda b, c: (b, 0, 0)),
                  pl.BlockSpec((1, ML_HEADS, ML_HEAD_DIM, ML_HEAD_DIM), lambda b, c: (b, 0, 0, 0)),
                  pl.BlockSpec((1, 8, ML_HEAD_DIM), lambda b, c: (b, 0, 0)),
                  pl.BlockSpec((1, 8, LANE), lambda b, c: (b, 0, 0)),
                  pl.BlockSpec((CONV_W, 2 * D_ML), lambda b, c: (0, 0)),
                  pl.BlockSpec((1, LANE), lambda b, c: (0, 0)),
                  pl.BlockSpec((1, D_ML), lambda b, c: (0, 0))],
        out_specs=[pl.BlockSpec((T, D_ML), row),
                   pl.BlockSpec((1, CONV_W - 1, 2 * D_ML), lambda b, c: (b, 0, 0)),
                   pl.BlockSpec((1, ML_HEADS, ML_HEAD_DIM, ML_HEAD_DIM), lambda b, c: (b, 0, 0, 0)),
                   pl.BlockSpec((1, 8, ML_HEAD_DIM), lambda b, c: (b, 0, 0)),
                   pl.BlockSpec((1, 8, LANE), lambda b, c: (b, 0, 0))],
        out_shape=[jax.ShapeDtypeStruct((batch * seq, D_ML), BF16),
                   jax.ShapeDtypeStruct((batch, CONV_W - 1, 2 * D_ML), F32),
                   jax.ShapeDtypeStruct((batch, ML_HEADS, ML_HEAD_DIM, ML_HEAD_DIM), F32),
                   jax.ShapeDtypeStruct((batch, 8, ML_HEAD_DIM), F32),
                   jax.ShapeDtypeStruct((batch, 8, LANE), F32)],
        scratch_shapes=[pltpu.VMEM((8 + T, 2 * D_ML), F32),
                        pltpu.VMEM((ML_HEADS, ML_HEAD_DIM, ML_HEAD_DIM), F32),
                        pltpu.VMEM((8, ML_HEAD_DIM), F32),
                        pltpu.VMEM((8, LANE), F32)],
        compiler_params=_cparams(("arbitrary", "arbitrary")),
        name="mlstm",
    )(u, u, u, u, small, conv_buf, c0, n0p, m0b, conv_w, gate_bias, norm_g)
    y, cbo, co, no, mo = outs
    return y, cbo, co, no[:, :ML_HEADS], mo[:, :ML_HEADS, 0]


def _bucket_np(dist):
    n = np.maximum(dist, 0)
    max_exact = N_BUCKETS // 2
    nf = np.maximum(n, 1).astype(np.float64)
    large = max_exact + (np.log(nf / max_exact) / math.log(MAX_DISTANCE / max_exact)
                         * (N_BUCKETS - max_exact)).astype(np.int64)
    return np.where(n < max_exact, n, np.minimum(large, N_BUCKETS - 1)).astype(np.int32)


def _bias_by_distance(rel_bias, n):
    return rel_bias.astype(F32)[_bucket_np(np.arange(n))].T


def _overlap_t(n_cmp, nch, n_slc):
    c0 = np.arange(nch) * CMP_STRIDE
    s0 = np.arange(n_slc) * SLC_BLOCK
    ov = np.minimum(c0[None, :] + CMP_BLOCK, s0[:, None] + SLC_BLOCK) - np.maximum(c0[None, :], s0[:, None])
    ov = np.clip(ov, 0, None).astype(np.float32) / CMP_BLOCK
    ov[:, n_cmp:] = 0.0
    return jnp.asarray(ov, BF16)


def _pq_kernel(x0_ref, x1_ref, x2_ref, x3_ref, pos_ref, w_ref, o_ref, *, rows):
    for pair, x_ref in enumerate((x0_ref, x1_ref, x2_ref, x3_ref)):
        toks = [x_ref[pl.ds(l, rows, stride=CMP_STRIDE), :] for l in range(CMP_STRIDE)]
        for half in range(2):
            ch = 2 * pair + half
            c = ch // NSA_KV_HEADS
            lanes = slice(half * NSA_HEAD_DIM, (half + 1) * NSA_HEAD_DIM)
            chunk = jnp.concatenate([t[:, lanes] for t in toks], axis=1)
            for part in range(2):
                lhs = (chunk + pos_ref[c, part:part + 1, :]).astype(BF16)
                o_ref[0, ch, :, part * CMP_HIDDEN:(part + 1) * CMP_HIDDEN] = _dot(lhs, w_ref[c, part])


def _hid_kernel(pq_ref, w2_ref, o_ref, *, nch):
    p = pq_ref[0, 0, :, 0:CMP_HIDDEN]
    q = pltpu.roll(pq_ref[0, 0, :, CMP_HIDDEN:2 * CMP_HIDDEN], nch - 1, 0)
    hid = jax.nn.gelu(p + q, approximate=True)
    o_ref[0, 0] = _dot(hid.astype(BF16), w2_ref[0])


def _compress(kv2d, col_block, bsz, seq, cmp_pos, cmp_w1, cmp_w2, tt):
    nch = seq // CMP_STRIDE
    kdim = CMP_STRIDE * NSA_HEAD_DIM
    nt = seq // tt
    tn = tt // CMP_STRIDE
    w1 = cmp_w1.reshape(2, 2, kdim, CMP_HIDDEN).astype(BF16)
    pos = cmp_pos.reshape(2, 2, kdim)
    pq = pl.pallas_call(
        functools.partial(_pq_kernel, rows=tn),
        grid=(bsz, nt),
        in_specs=[pl.BlockSpec((tt, LANE), functools.partial(lambda b, i, k: (b * nt + i, col_block * 4 + k), k=k))
                  for k in range(4)] +
                 [pl.BlockSpec((2, 2, kdim), lambda b, i: (0, 0, 0)),
                  pl.BlockSpec((2, 2, kdim, CMP_HIDDEN), lambda b, i: (0, 0, 0, 0))],
        out_specs=pl.BlockSpec((1, 8, tn, 2 * CMP_HIDDEN), lambda b, i: (b, 0, i, 0)),
        out_shape=jax.ShapeDtypeStruct((bsz, 8, nch, 2 * CMP_HIDDEN), F32),
        compiler_params=_cparams(("arbitrary", "arbitrary")),
        name="cmp_pq",
    )(kv2d, kv2d, kv2d, kv2d, pos, w1)
    return _cmp_hid(pq, cmp_w2)


def _cmp_hid(pq, cmp_w2):
    bsz, _, nch, _ = pq.shape
    return pl.pallas_call(
        functools.partial(_hid_kernel, nch=nch),
        grid=(bsz, 8),
        in_specs=[pl.BlockSpec((1, 1, nch, 2 * CMP_HIDDEN), lambda b, c: (b, c, 0, 0)),
                  pl.BlockSpec((1, CMP_HIDDEN, NSA_HEAD_DIM), lambda b, c: (c // NSA_KV_HEADS, 0, 0))],
        out_specs=pl.BlockSpec((1, 1, nch, NSA_HEAD_DIM), lambda b, c: (b, c, 0, 0)),
        out_shape=jax.ShapeDtypeStruct((bsz, 8, nch, NSA_HEAD_DIM), F32),
        compiler_params=_cparams(("arbitrary", "arbitrary")),
        name="cmp_hid",
    )(pq, cmp_w2.astype(BF16))


def _top_n_mask(score_t, blk, n_rows):
    rank = jnp.zeros(score_t.shape, jnp.int32)
    for i in range(n_rows):
        row = score_t[i:i + 1, :]
        beats = (row > score_t) | ((row == score_t) & (blk > i))
        rank = rank + beats.astype(jnp.int32)
    return (rank < min(SLC_TOPN, n_rows)).astype(F32)


def _bias_from_buckets(bucket, table_ref, head):
    bias = jnp.zeros(bucket.shape, F32)
    for k in range(N_BUCKETS):
        bias = jnp.where(bucket == k, table_ref[k, head], bias)
    return bias


def _cattn_kernel(tab_ref, q_ref, kc_ref, vc_ref, bk_ref, ovt_ref, o_ref, sel_ref, bias_scr,
                  *, tq, nch, n_cmp, n_slc):
    h = pl.program_id(0)
    i = pl.program_id(1)

    @pl.when(pl.program_id(2) == 0)
    def _():
        bucket = bk_ref[...]
        for g in range(NSA_GROUP):
            bias_scr[g] = _bias_from_buckets(bucket, tab_ref, h * NSA_GROUP + g)

    kc = kc_ref[0, 0].astype(BF16)
    vc = vc_ref[0, 0].astype(BF16)
    t = i * tq + lax.broadcasted_iota(jnp.int32, (tq, nch), 0)
    n = lax.broadcasted_iota(jnp.int32, (tq, nch), 1)
    valid = (t - CMP_STRIDE * n - (CMP_BLOCK - 1) >= 0) & (n < n_cmp)
    pc = jnp.zeros((tq, nch), F32)
    for g in range(NSA_GROUP):
        sl = slice(g * NSA_HEAD_DIM, (g + 1) * NSA_HEAD_DIM)
        qg = (q_ref[:, sl] * (NSA_HEAD_DIM ** -0.5)).astype(BF16)
        s = _dot_nt(qg, kc) + bias_scr[g]
        s = jnp.where(valid, s, NEG_BIG)
        e = jnp.exp(s - jnp.max(s, axis=1, keepdims=True))
        p = jnp.where(valid, e / jnp.sum(e, axis=1, keepdims=True), 0.0)
        o_ref[:, sl] = _dot(p.astype(BF16), vc)
        pc = pc + p
    score_t = _dot_nt(ovt_ref[...], pc.astype(BF16))
    blk = lax.broadcasted_iota(jnp.int32, (n_slc, tq), 0)
    cur = (i * tq + lax.broadcasted_iota(jnp.int32, (n_slc, tq), 1)) // SLC_BLOCK
    forced = (blk == 0) | (blk == cur) | (blk == cur - 1)
    score_t = jnp.where(forced, FORCE_SCORE, score_t)
    score_t = jnp.where(blk > cur, -FORCE_SCORE, score_t)
    sel_ref[0, 0] = _top_n_mask(score_t, blk, n_slc)


def _cattn(u, kvc, rel_bias, batch, seq, tq):
    nch = kvc.shape[2]
    n_cmp = nch - 1
    n_slc = seq // SLC_BLOCK
    ni = seq // tq
    ovt = _overlap_t(n_cmp, nch, n_slc)
    qcol = COL_QNSA // (NSA_GROUP * NSA_HEAD_DIM)
    dist = np.arange(seq)[:, None] - (CMP_STRIDE * np.arange(nch)[None, :] + CMP_BLOCK - 1)
    buckets = jnp.asarray(_bucket_np(dist))
    return pl.pallas_call(
        functools.partial(_cattn_kernel, tq=tq, nch=nch, n_cmp=n_cmp, n_slc=n_slc),
        grid=(NSA_KV_HEADS, ni, batch),
        in_specs=[pl.BlockSpec(memory_space=pltpu.SMEM),
                  pl.BlockSpec((tq, NSA_GROUP * NSA_HEAD_DIM), lambda h, i, b: (b * ni + i, qcol + h)),
                  pl.BlockSpec((1, 1, nch, NSA_HEAD_DIM), lambda h, i, b: (b, h, 0, 0)),
                  pl.BlockSpec((1, 1, nch, NSA_HEAD_DIM), lambda h, i, b: (b, NSA_KV_HEADS + h, 0, 0)),
                  pl.BlockSpec((tq, nch), lambda h, i, b: (i, 0)),
                  pl.BlockSpec((n_slc, nch), lambda h, i, b: (0, 0))],
        out_specs=[pl.BlockSpec((tq, NSA_GROUP * NSA_HEAD_DIM), lambda h, i, b: (b * ni + i, h)),
                   pl.BlockSpec((1, 1, n_slc, tq), lambda h, i, b: (b, h, 0, i))],
        out_shape=[jax.ShapeDtypeStruct((batch * seq, D_NSA), F32),
                   jax.ShapeDtypeStruct((batch, NSA_KV_HEADS, n_slc, seq), F32)],
        scratch_shapes=[pltpu.VMEM((NSA_GROUP, tq, nch), F32)],
        compiler_params=_cparams(("arbitrary", "arbitrary", "arbitrary")),
        name="cmp_attn",
    )(rel_bias.astype(F32), u, kvc, kvc, buckets, ovt)


TQ = 128
FLASH_GROUP = 4


def _flash_kernel(tab_ref, q_ref, kv_ref, bk_ref, *rest, selected, n_delta):
    if selected:
        sel_ref, o_ref, band_scr, qs_scr, m_scr, l_scr, acc_scr = rest
    else:
        o_ref, band_scr, qs_scr, m_scr, l_scr, acc_scr = rest
    h = pl.program_id(0)
    i = pl.program_id(2)

    @pl.when((pl.program_id(1) == 0) & (i == 0))
    def _():
        def fill(d, carry):
            bucket = bk_ref[d]
            for g in range(NSA_GROUP):
                band_scr[d, :, g * TQ:(g + 1) * TQ] = _bias_from_buckets(bucket, tab_ref, h * NSA_GROUP + g)
            return carry
        lax.fori_loop(0, n_delta, fill, 0)

    for g in range(NSA_GROUP):
        qg = (q_ref[:, g * NSA_HEAD_DIM:(g + 1) * NSA_HEAD_DIM] * (NSA_HEAD_DIM ** -0.5)).astype(BF16)
        qs_scr[g * TQ:(g + 1) * TQ, :] = jnp.concatenate([qg, jnp.zeros_like(qg)], axis=1)
    is_key_lane = lax.broadcasted_iota(jnp.int32, (TQ, LANE), 1) < NSA_HEAD_DIM
    m_scr[...] = jnp.full(m_scr.shape, 0.5 * NEG_BIG, F32)
    l_scr[...] = jnp.zeros(l_scr.shape, F32)
    acc_scr[...] = jnp.zeros(acc_scr.shape, F32)
    key = lax.broadcasted_iota(jnp.int32, (TQ, TQ), 0)
    qry = lax.broadcasted_iota(jnp.int32, (TQ, TQ), 1)
    n_back = WINDOW // TQ

    def scores(j, kind):
        kvj = kv_ref[pl.ds(pl.multiple_of(j * TQ, TQ), TQ), :].astype(BF16)
        s = _dot_nt(kvj, qs_scr[...]) + band_scr[i - j]
        mask = None
        if kind == "diag":
            mask = qry >= key
        elif kind == "far":
            mask = qry < key
        if selected:
            r = sel_ref[0, 0, pl.ds(2 * j, 2), :]
            picked = jnp.where(key < SLC_BLOCK, r[0:1, :], r[1:2, :]) > 0.5
            mask = picked if mask is None else (mask & picked)
        if mask is not None:
            s = jnp.concatenate([jnp.where(mask, s[:, g * TQ:(g + 1) * TQ], NEG_BIG) for g in range(NSA_GROUP)], axis=1)
        ones_v = jnp.where(is_key_lane, jnp.ones_like(kvj), kvj)
        return s, ones_v

    def update(tiles):
        m_old = m_scr[...]
        m_new = m_old
        for s, _ in tiles:
            m_new = jnp.maximum(m_new, jnp.max(s, axis=0, keepdims=True))
        alpha = jnp.exp(m_old - m_new)
        l_new = alpha * l_scr[...]
        acc = alpha * acc_scr[...]
        for s, ones_v in tiles:
            p = jnp.exp(s - m_new)
            pv = _dot_tn(ones_v, p.astype(BF16))
            l_new = l_new + pv[0:1, :]
            acc = acc + pv[NSA_HEAD_DIM:2 * NSA_HEAD_DIM, :]
        m_scr[...] = m_new
        l_scr[...] = l_new
        acc_scr[...] = acc

    def full_tiles(first, count):
        return [scores(first + k, "full") for k in range(count)]

    if selected:
        def group(t, carry):
            update(full_tiles(FLASH_GROUP * t, FLASH_GROUP))
            return carry

        n_groups = i // FLASH_GROUP
        lax.fori_loop(0, n_groups, group, 0)
        for rem in range(FLASH_GROUP):
            @pl.when(i - n_groups * FLASH_GROUP == rem)
            def _():
                update(full_tiles(n_groups * FLASH_GROUP, rem) + [scores(i, "diag")])
    else:
        @pl.when(i >= n_back)
        def _():
            update([scores(i - n_back, "far")] + full_tiles(i - n_back + 1, n_back - 1) + [scores(i, "diag")])

        for rem in range(n_back):
            @pl.when(i == rem)
            def _():
                update(full_tiles(0, rem) + [scores(i, "diag")])

    for g in range(NSA_GROUP):
        cols = slice(g * TQ, (g + 1) * TQ)
        o_ref[:, g * NSA_HEAD_DIM:(g + 1) * NSA_HEAD_DIM] = (acc_scr[:, cols] / l_scr[:, cols]).T


def _flash(u, kv_col, rel_bias, sel_t, batch, seq):
    ni = seq // TQ
    qcol = COL_QNSA // (NSA_GROUP * NSA_HEAD_DIM)
    selected = sel_t is not None
    n_delta = ni if selected else WINDOW // TQ + 1
    kvblk = kv_col // LANE
    delta = np.arange(n_delta)[:, None, None] * TQ + np.arange(TQ)[None, None, :] - np.arange(TQ)[None, :, None]
    buckets = jnp.asarray(_bucket_np(delta))
    in_specs = [pl.BlockSpec(memory_space=pltpu.SMEM),
                pl.BlockSpec((TQ, NSA_GROUP * NSA_HEAD_DIM), lambda h, b, i: (b * ni + i, qcol + h)),
                pl.BlockSpec((seq, LANE), lambda h, b, i: (b, kvblk + h)),
                pl.BlockSpec((n_delta, TQ, TQ), lambda h, b, i: (0, 0, 0))]
    args = [rel_bias.astype(F32), u, u, buckets]
    if selected:
        n_slc = sel_t.shape[2]
        in_specs.append(pl.BlockSpec((1, 1, n_slc, TQ), lambda h, b, i: (b, h, 0, i)))
        args.append(sel_t)
    return pl.pallas_call(
        functools.partial(_flash_kernel, selected=selected, n_delta=n_delta),
        grid=(NSA_KV_HEADS, batch, ni),
        in_specs=in_specs,
        out_specs=pl.BlockSpec((TQ, NSA_GROUP * NSA_HEAD_DIM), lambda h, b, i: (b * ni + i, h)),
        out_shape=jax.ShapeDtypeStruct((batch * seq, D_NSA), F32),
        scratch_shapes=[pltpu.VMEM((n_delta, TQ, NSA_GROUP * TQ), F32),
                        pltpu.VMEM((NSA_GROUP * TQ, LANE), BF16),
                        pltpu.VMEM((1, NSA_GROUP * TQ), F32),
                        pltpu.VMEM((1, NSA_GROUP * TQ), F32),
                        pltpu.VMEM((NSA_HEAD_DIM, NSA_GROUP * TQ), F32)],
        compiler_params=_cparams(("arbitrary", "arbitrary", "arbitrary")),
        name="slc_attn" if selected else "win_attn",
    )(*args)


def _gate_expand():
    e = np.zeros((3, LANE, D_NSA), np.float32)
    for br in range(3):
        for hd in range(NSA_HEADS):
            e[br, 2 * ML_HEADS + br * NSA_HEADS + hd, hd * NSA_HEAD_DIM:(hd + 1) * NSA_HEAD_DIM] = 1.0
    return jnp.asarray(e, BF16)


def _outproj_kernel(yml_ref, oc_ref, os_ref, ow_ref, s_ref, e_ref, ng_ref, w_ref, x_ref, o_ref, cat_scr):
    @pl.when(pl.program_id(1) == 0)
    def _():
        sig = jax.nn.sigmoid(s_ref[...])
        hi, lo = _split2(sig)
        o = jnp.zeros(oc_ref.shape, F32)
        for br, ref in enumerate((oc_ref, os_ref, ow_ref)):
            gexp = _dot(hi, e_ref[br]) + _dot(lo, e_ref[br])
            o = o + gexp * ref[...]
        o = o * lax.rsqrt(jnp.mean(o * o, axis=-1, keepdims=True) + EPS) * ng_ref[...]
        cat_scr[:, 0:D_ML] = yml_ref[...]
        cat_scr[:, D_ML:D_ML + D_NSA] = o.astype(BF16)

    o_ref[...] = x_ref[...] + _dot(cat_scr[...], w_ref[...])


def _outproj(y_ml, o_c, o_s, o_w, small, nsa_g, w_out_b, x2d, tm, tn):
    n, d = x2d.shape
    rows = lambda i, j: (i, 0)
    return pl.pallas_call(
        _outproj_kernel,
        grid=(n // tm, d // tn),
        in_specs=[pl.BlockSpec((tm, D_ML), rows), pl.BlockSpec((tm, D_NSA), rows),
                  pl.BlockSpec((tm, D_NSA), rows), pl.BlockSpec((tm, D_NSA), rows),
                  pl.BlockSpec((tm, LANE), rows),
                  pl.BlockSpec((3, LANE, D_NSA), lambda i, j: (0, 0, 0)),
                  pl.BlockSpec((1, D_NSA), lambda i, j: (0, 0)),
                  pl.BlockSpec((D_ML + D_NSA, tn), lambda i, j: (0, j)),
                  pl.BlockSpec((tm, tn), lambda i, j: (i, j))],
        out_specs=pl.BlockSpec((tm, tn), lambda i, j: (i, j)),
        out_shape=jax.ShapeDtypeStruct((n, d), F32),
        scratch_shapes=[pltpu.VMEM((tm, D_ML + D_NSA), BF16)],
        compiler_params=_cparams(("arbitrary", "arbitrary")),
        name="outproj",
    )(y_ml, o_c, o_s, o_w, small, _gate_expand(), nsa_g, w_out_b, x2d)


ROUTE_COL = N_GROUPS
BIG_COL = 1 << 20
SPLIT = D_MODEL // LANE
DMA_UNROLL = 8


def _store_split(ref, val, n):
    for k in range(SPLIT):
        ref[pl.ds(k, n, stride=SPLIT), :] = val[:, k * LANE:(k + 1) * LANE]


def _load_split(ref, n):
    return jnp.concatenate([ref[pl.ds(k, n, stride=SPLIT), :] for k in range(SPLIT)], axis=1)


def _route_kernel(y_ref, g_ref, w_ref, b_ref, h_ref, info_ref, cnt_ref, carry_scr, *, tm):
    @pl.when(pl.program_id(0) == 0)
    def _():
        carry_scr[...] = jnp.zeros(carry_scr.shape, F32)

    x = y_ref[...]
    h = x * lax.rsqrt(jnp.mean(x * x, axis=-1, keepdims=True) + EPS) * g_ref[...]
    _store_split(h_ref, h, tm)
    logit = _dot(h.astype(BF16), w_ref[...]) + b_ref[...]
    col = lax.broadcasted_iota(jnp.int32, logit.shape, 1)
    is_grp = col < N_GROUPS
    gmax = jnp.max(jnp.where(is_grp, logit, -jnp.inf), axis=1, keepdims=True)
    gtop = jnp.min(jnp.where(is_grp & (logit == gmax), col, BIG_COL), axis=1, keepdims=True)
    gsum = jnp.sum(jnp.where(is_grp, jnp.exp(logit - gmax), 0.0), axis=1, keepdims=True)
    first = ROUTE_COL + gtop * EXPERTS_PER_GROUP
    in_grp = (col >= first) & (col < first + EXPERTS_PER_GROUP)
    v1 = jnp.max(jnp.where(in_grp, logit, -jnp.inf), axis=1, keepdims=True)
    i1 = jnp.min(jnp.where(in_grp & (logit == v1), col, BIG_COL), axis=1, keepdims=True)
    rest = in_grp & (col != i1)
    v2 = jnp.max(jnp.where(rest, logit, -jnp.inf), axis=1, keepdims=True)
    i2 = jnp.min(jnp.where(rest & (logit == v2), col, BIG_COL), axis=1, keepdims=True)
    e = jnp.exp(v2 - v1)
    w1 = 1.0 / ((1.0 + e) * gsum)
    w2 = e / ((1.0 + e) * gsum)
    pick1 = col == i1
    pick2 = col == i2
    both = (pick1 | pick2).astype(F32)
    ri = lax.broadcasted_iota(jnp.int32, (tm, tm), 0)
    ci = lax.broadcasted_iota(jnp.int32, (tm, tm), 1)
    before = (ci < ri).astype(BF16)
    cum = _dot(before, both.astype(BF16)) + carry_scr[...]
    r1 = jnp.sum(jnp.where(pick1, cum, 0.0), axis=1, keepdims=True)
    r2 = jnp.sum(jnp.where(pick2, cum, 0.0), axis=1, keepdims=True)
    carry_scr[...] = carry_scr[...] + jnp.sum(both, axis=0, keepdims=True)
    cnt_ref[...] = jnp.broadcast_to(carry_scr[...], cnt_ref.shape)
    info = jnp.where(col == 0, (i1 - ROUTE_COL).astype(F32), 0.0)
    info = jnp.where(col == 1, (i2 - ROUTE_COL).astype(F32), info)
    info = jnp.where(col == 2, w1, info)
    info = jnp.where(col == 3, w2, info)
    info = jnp.where(col == 4, r1, info)
    info_ref[...] = jnp.where(col == 5, r2, info)


def _route(y2d, g, w_rg, b_rg, w_re, b_re, tm):
    n, d = y2d.shape
    wr = jnp.concatenate([w_rg, w_re, jnp.zeros((d, LANE - N_GROUPS - N_EXPERTS), F32)], axis=1).astype(BF16)
    bias = jnp.concatenate([b_rg, b_re, jnp.zeros((LANE - N_GROUPS - N_EXPERTS,), F32)])[None, :]
    rows = lambda i: (i, 0)
    fixed = lambda i: (0, 0)
    return pl.pallas_call(
        functools.partial(_route_kernel, tm=tm),
        grid=(n // tm,),
        in_specs=[pl.BlockSpec((tm, d), rows), pl.BlockSpec((1, d), fixed),
                  pl.BlockSpec((d, LANE), fixed), pl.BlockSpec((1, LANE), fixed)],
        out_specs=[pl.BlockSpec((tm * SPLIT, LANE), rows), pl.BlockSpec((tm, LANE), rows),
                   pl.BlockSpec((8, LANE), fixed)],
        out_shape=[jax.ShapeDtypeStruct((n * SPLIT, LANE), F32), jax.ShapeDtypeStruct((n, LANE), F32),
                   jax.ShapeDtypeStruct((8, LANE), F32)],
        scratch_shapes=[pltpu.VMEM((1, LANE), F32)],
        compiler_params=_cparams(("arbitrary",)),
        name="moe_route",
    )(y2d, g, wr, bias)


W_CHUNKS = 4


def _expert_kernel(te_ref, nu_ref, src_ref, nx_ref, ws_ref, h_hbm, wg_hbm, wu_hbm, wd_hbm, o_ref,
                   xbuf, wg_f, wu_f, wd_f, wg_b, wu_b, wd_b, sem, wsem, *, tme):
    i = pl.program_id(0)
    slot = i % 2

    def weight_copies(expert, wslot):
        copies = []
        for hbm, buf in ((wg_hbm, wg_f), (wu_hbm, wu_f), (wd_hbm, wd_f)):
            step = hbm.shape[1] // W_CHUNKS
            for c in range(W_CHUNKS):
                copies.append(pltpu.make_async_copy(hbm.at[expert, pl.ds(c * step, step)],
                                                    buf.at[wslot, pl.ds(c * step, step)], wsem.at[wslot]))
        return copies

    def row_copy(tile, r, buf_slot):
        src_row = src_ref[tile * tme + r]
        return pltpu.make_async_copy(h_hbm.at[pl.ds(pl.multiple_of(src_row * SPLIT, SPLIT), SPLIT)],
                                     xbuf.at[buf_slot, pl.ds(pl.multiple_of(r * SPLIT, SPLIT), SPLIT)],
                                     sem.at[buf_slot])

    def fetch(tile, buf_slot):
        def body(r, carry):
            row_copy(tile, r, buf_slot).start()
            return carry
        lax.fori_loop(0, tme, body, 0, unroll=DMA_UNROLL)

    @pl.when(i == 0)
    def _():
        fetch(0, 0)
        for cp in weight_copies(te_ref[0], ws_ref[0]):
            cp.start()

    @pl.when(i + 1 < nu_ref[0])
    def _():
        fetch(i + 1, 1 - slot)

    @pl.when(i < nu_ref[0])
    def _():
        prev = te_ref[jnp.maximum(i - 1, 0)]

        @pl.when((i == 0) | (te_ref[i] != prev))
        def _():
            w = ws_ref[i]
            for cp in weight_copies(te_ref[i], w):
                cp.wait()
            wg_b[...] = wg_f[w].astype(BF16)
            wu_b[...] = wu_f[w].astype(BF16)
            wd_b[...] = wd_f[w].astype(BF16)

            @pl.when(nx_ref[i] >= 0)
            def _():
                for cp in weight_copies(nx_ref[i], 1 - w):
                    cp.start()

        def wait_body(r, carry):
            row_copy(i, r, slot).wait()
            return carry
        lax.fori_loop(0, tme, wait_body, 0, unroll=DMA_UNROLL)

        x = _load_split(xbuf.at[slot], tme).astype(BF16)
        a = _dot(x, wg_b[...])
        u = _dot(x, wu_b[...])
        hid = a * jax.nn.sigmoid(a) * u
        _store_split(o_ref, _dot(hid.astype(BF16), wd_b[...]), tme)

    @pl.when(i >= nu_ref[0])
    def _():
        o_ref[...] = jnp.zeros(o_ref.shape, F32)


def _experts(h, src, tile_expert, n_used, next_expert, weight_slot, w_gate, w_up, w_down, tme):
    p = src.shape[0]
    d = D_MODEL
    hbm = pl.BlockSpec(memory_space=pl.ANY)
    return pl.pallas_call(
        functools.partial(_expert_kernel, tme=tme),
        grid_spec=pltpu.PrefetchScalarGridSpec(
            num_scalar_prefetch=5, grid=(p // tme,),
            in_specs=[hbm, hbm, hbm, hbm],
            out_specs=pl.BlockSpec((tme * SPLIT, LANE), lambda i, *_: (i, 0)),
            scratch_shapes=[pltpu.VMEM((2, tme * SPLIT, LANE), F32),
                            pltpu.VMEM((2, d, D_EXPERT), F32), pltpu.VMEM((2, d, D_EXPERT), F32),
                            pltpu.VMEM((2, D_EXPERT, d), F32),
                            pltpu.VMEM((d, D_EXPERT), BF16), pltpu.VMEM((d, D_EXPERT), BF16),
                            pltpu.VMEM((D_EXPERT, d), BF16),
                            pltpu.SemaphoreType.DMA((2,)), pltpu.SemaphoreType.DMA((2,))]),
        out_shape=jax.ShapeDtypeStruct((p * SPLIT, LANE), F32),
        compiler_params=_cparams(("arbitrary",)),
        name="moe_experts",
    )(tile_expert, n_used, src, next_expert, weight_slot, h, w_gate, w_up, w_down)


def _combine_kernel(d1_ref, d2_ref, y_ref, info_ref, ys_hbm, g_ref, o_ref, a_buf, b_buf, sem, *, tm):
    i = pl.program_id(0)
    slot = i % 2

    def copy(r, row, buf, buf_slot):
        return pltpu.make_async_copy(ys_hbm.at[pl.ds(pl.multiple_of(row * SPLIT, SPLIT), SPLIT)],
                                     buf.at[buf_slot, pl.ds(pl.multiple_of(r * SPLIT, SPLIT), SPLIT)],
                                     sem.at[buf_slot])

    def fetch(tile, buf_slot):
        def issue(r, carry):
            copy(r, d1_ref[tile * tm + r], a_buf, buf_slot).start()
            copy(r, d2_ref[tile * tm + r], b_buf, buf_slot).start()
            return carry
        lax.fori_loop(0, tm, issue, 0, unroll=DMA_UNROLL)

    @pl.when(i == 0)
    def _():
        fetch(0, 0)

    @pl.when(i + 1 < pl.num_programs(0))
    def _():
        fetch(i + 1, 1 - slot)

    def drain(r, carry):
        copy(r, 0, a_buf, slot).wait()
        copy(r, 0, b_buf, slot).wait()
        return carry

    lax.fori_loop(0, tm, drain, 0, unroll=DMA_UNROLL)
    w1 = info_ref[:, 2:3]
    w2 = info_ref[:, 3:4]
    y = y_ref[...] + w1 * _load_split(a_buf.at[slot], tm) + w2 * _load_split(b_buf.at[slot], tm)
    o_ref[...] = y * lax.rsqrt(jnp.mean(y * y, axis=-1, keepdims=True) + EPS) * g_ref[...]


def _combine(y2d, info, ys, d1, d2, g, tm):
    n, d = y2d.shape
    return pl.pallas_call(
        functools.partial(_combine_kernel, tm=tm),
        grid_spec=pltpu.PrefetchScalarGridSpec(
            num_scalar_prefetch=2, grid=(n // tm,),
            in_specs=[pl.BlockSpec((tm, d), lambda i, a, b: (i, 0)),
                      pl.BlockSpec((tm, LANE), lambda i, a, b: (i, 0)),
                      pl.BlockSpec(memory_space=pl.ANY),
                      pl.BlockSpec((1, d), lambda i, a, b: (0, 0))],
            out_specs=pl.BlockSpec((tm, d), lambda i, a, b: (i, 0)),
            scratch_shapes=[pltpu.VMEM((2, tm * SPLIT, LANE), F32), pltpu.VMEM((2, tm * SPLIT, LANE), F32),
                            pltpu.SemaphoreType.DMA((2,))]),
        out_shape=jax.ShapeDtypeStruct((n, d), F32),
        compiler_params=_cparams(("arbitrary",)),
        name="moe_combine",
    )(d1, d2, y2d, info, ys, g)


def _moe_final(y2d, ffn_g, w_rg, b_rg, w_re, b_re, w_gate, w_up, w_down, final_g, tm, tme):
    n, d = y2d.shape
    h, info, cnt = _route(y2d, ffn_g, w_rg, b_rg, w_re, b_re, tm)
    e1 = info[:, 0].astype(jnp.int32)
    e2 = info[:, 1].astype(jnp.int32)
    counts = cnt[0, ROUTE_COL:ROUTE_COL + N_EXPERTS].astype(jnp.int32)
    padded = (counts + tme - 1) // tme * tme
    ends = jnp.cumsum(padded)
    offs = ends - padded
    d1 = offs[e1] + info[:, 4].astype(jnp.int32)
    d2 = offs[e2] + info[:, 5].astype(jnp.int32)
    p = 2 * n + N_EXPERTS * tme
    tok = jnp.arange(n, dtype=jnp.int32)
    src = jnp.zeros((p,), jnp.int32).at[jnp.concatenate([d1, d2])].set(jnp.concatenate([tok, tok]))
    n_tiles = p // tme
    n_used = (ends[-1] // tme).astype(jnp.int32).reshape(1)
    tile_start = jnp.arange(n_tiles, dtype=jnp.int32) * tme
    tile_expert = jnp.sum((ends[None, :] <= tile_start[:, None]).astype(jnp.int32), axis=1)
    last = jnp.sum((ends <= ends[-1] - 1).astype(jnp.int32))
    tile_expert = jnp.minimum(tile_expert, last).astype(jnp.int32)
    run = jnp.cumsum(jnp.concatenate([jnp.zeros((1,), jnp.int32),
                                      (tile_expert[1:] != tile_expert[:-1]).astype(jnp.int32)]))
    weight_slot = (run % 2).astype(jnp.int32)
    after = ends[tile_expert] // tme
    next_expert = jnp.where(after < n_used[0], tile_expert[jnp.minimum(after, n_tiles - 1)], -1).astype(jnp.int32)
    ys = _experts(h, src, tile_expert, n_used, next_expert, weight_slot, w_gate, w_up, w_down, tme)
    return _combine(y2d, info, ys, d1, d2, final_g, tm)


GROUP_ROWS = 8
SLC_LANES = 384
SEL_KEYS = 1024
WIN_KEYS = 640


def _sample_cattn_kernel(q_ref, kc_ref, vc_ref, b_ref, ov_ref, o_ref, sel_ref, *, n_cmp, n_slc):
    q = (q_ref[0, 0] * (NSA_HEAD_DIM ** -0.5)).astype(BF16)
    s = _dot_nt(q, kc_ref[0, 0].astype(BF16)) + b_ref[0]
    valid = lax.broadcasted_iota(jnp.int32, s.shape, 1) < n_cmp
    s = jnp.where(valid, s, NEG_BIG)
    e = jnp.exp(s - jnp.max(s, axis=1, keepdims=True))
    p = jnp.where(valid, e / jnp.sum(e, axis=1, keepdims=True), 0.0)
    o_ref[0, 0] = _dot(p.astype(BF16), vc_ref[0, 0].astype(BF16))
    pg = jnp.where(lax.broadcasted_iota(jnp.int32, p.shape, 0) < NSA_GROUP, p, 0.0)
    pc = jnp.broadcast_to(jnp.sum(pg, axis=0, keepdims=True), pg.shape)
    score = _dot(pc.astype(BF16), ov_ref[...])[0:1, :]
    blk = lax.broadcasted_iota(jnp.int32, score.shape, 1)
    cur = n_slc - 1
    forced = (blk == 0) | (blk == cur) | (blk == cur - 1)
    score = jnp.where(forced, FORCE_SCORE, score)
    score = jnp.where(blk > cur, -jnp.inf, score)
    ri = lax.broadcasted_iota(jnp.int32, (SLC_LANES, SLC_LANES), 0)
    ci = lax.broadcasted_iota(jnp.int32, (SLC_LANES, SLC_LANES), 1)
    sb = jnp.broadcast_to(score, (SLC_LANES, SLC_LANES))
    col = jnp.sum(jnp.where(ri == ci, sb, 0.0), axis=1, keepdims=True)
    beats = (col > sb) | ((col == sb) & (ri < ci))
    rank = jnp.sum(beats.astype(jnp.int32), axis=0, keepdims=True)
    sel_ref[0, 0] = jnp.broadcast_to((rank < SLC_TOPN).astype(F32), (GROUP_ROWS, SLC_LANES))


def _sample_cattn(q8, kvc, bias_cs, n_cmp, n_slc):
    bsz, _, nch, _ = kvc.shape
    c0 = np.arange(nch) * CMP_STRIDE
    s0 = np.arange(SLC_LANES) * SLC_BLOCK
    ov = np.minimum(c0[:, None] + CMP_BLOCK, s0[None, :] + SLC_BLOCK) - np.maximum(c0[:, None], s0[None, :])
    ov = np.clip(ov, 0, None).astype(np.float32) / CMP_BLOCK
    ov[n_cmp:, :] = 0.0
    ov[:, n_slc:] = 0.0
    return pl.pallas_call(
        functools.partial(_sample_cattn_kernel, n_cmp=n_cmp, n_slc=n_slc),
        grid=(bsz, NSA_KV_HEADS),
        in_specs=[pl.BlockSpec((1, 1, GROUP_ROWS, NSA_HEAD_DIM), lambda b, h: (b, h, 0, 0)),
                  pl.BlockSpec((1, 1, nch, NSA_HEAD_DIM), lambda b, h: (b, h, 0, 0)),
                  pl.BlockSpec((1, 1, nch, NSA_HEAD_DIM), lambda b, h: (b, NSA_KV_HEADS + h, 0, 0)),
                  pl.BlockSpec((1, GROUP_ROWS, nch), lambda b, h: (h, 0, 0)),
                  pl.BlockSpec((nch, SLC_LANES), lambda b, h: (0, 0))],
        out_specs=[pl.BlockSpec((1, 1, GROUP_ROWS, NSA_HEAD_DIM), lambda b, h: (b, h, 0, 0)),
                   pl.BlockSpec((1, 1, GROUP_ROWS, SLC_LANES), lambda b, h: (b, h, 0, 0))],
        out_shape=[jax.ShapeDtypeStruct((bsz, NSA_KV_HEADS, GROUP_ROWS, NSA_HEAD_DIM), F32),
                   jax.ShapeDtypeStruct((bsz, NSA_KV_HEADS, GROUP_ROWS, SLC_LANES), F32)],
        compiler_params=_cparams(("arbitrary", "arbitrary")),
        name="sample_cmp_attn",
    )(q8, kvc, kvc, bias_cs, jnp.asarray(ov, BF16))


def _sample_attn_kernel(pg_ref, hf_ref, q_ref, cache_hbm, snew_ref, bs_ref, win_ref, wnew_ref, bw_ref,
                        os_ref, ow_ref, kt_buf, vt_buf, sem, *, n_gather, wlen):
    b = pl.program_id(0)
    rows_per_page = 2 * NSA_KV_HEADS * NSA_HEAD_DIM

    def page_copy(h, s, c, buf):
        page = pg_ref[(b * NSA_KV_HEADS + h) * n_gather + s]
        start = pl.multiple_of(page * rows_per_page + (c * NSA_KV_HEADS + h) * NSA_HEAD_DIM, NSA_HEAD_DIM)
        return pltpu.make_async_copy(cache_hbm.at[pl.ds(start, NSA_HEAD_DIM)],
                                     buf.at[h, :, pl.ds(s * PAGE_SIZE, PAGE_SIZE)], sem)

    for h in range(NSA_KV_HEADS):
        for s in range(n_gather):
            page_copy(h, s, 0, kt_buf).start()
            page_copy(h, s, 1, vt_buf).start()

    def attend(q, kts, vts, bias, valid):
        s = jnp.concatenate([_dot(q, kt.astype(BF16)) for kt in kts], axis=1) + bias
        s = jnp.where(valid, s, NEG_BIG)
        e = jnp.exp(s - jnp.max(s, axis=1, keepdims=True))
        p = jnp.where(valid, e / jnp.sum(e, axis=1, keepdims=True), 0.0).astype(BF16)
        out = None
        lo = 0
        for vt in vts:
            n = vt.shape[1]
            part = _dot_nt(p[:, lo:lo + n], vt.astype(BF16))
            out = part if out is None else out + part
            lo += n
        return out

    def head_rows(ref, c, h):
        r0 = (c * NSA_KV_HEADS + h) * NSA_HEAD_DIM
        return ref[0, r0:r0 + NSA_HEAD_DIM, :]

    widx = lax.broadcasted_iota(jnp.int32, (GROUP_ROWS, wlen + PAGE_SIZE), 1)
    wvalid = (widx >= wlen + 1 - WINDOW) & (widx <= wlen)
    for h in range(NSA_KV_HEADS):
        q = (q_ref[0, h] * (NSA_HEAD_DIM ** -0.5)).astype(BF16)
        ow_ref[0, h] = attend(q, [head_rows(win_ref, 0, h), head_rows(wnew_ref, 0, h)],
                              [head_rows(win_ref, 1, h), head_rows(wnew_ref, 1, h)], bw_ref[h], wvalid)

    for h in range(NSA_KV_HEADS):
        for s in range(n_gather):
            page_copy(h, s, 0, kt_buf).wait()
            page_copy(h, s, 1, vt_buf).wait()

    keys = n_gather * PAGE_SIZE
    lane = lax.broadcasted_iota(jnp.int32, (GROUP_ROWS, keys + PAGE_SIZE), 1)
    slot = lane >> 7
    lane_half = (lane >> 6) & 1
    for h in range(NSA_KV_HEADS):
        want = jnp.full(lane.shape, -1, jnp.int32)
        for s in range(n_gather):
            want = jnp.where(slot == s, hf_ref[(b * NSA_KV_HEADS + h) * n_gather + s], want)
        svalid = (lane_half == want) | (lane == keys)
        q = (q_ref[0, h] * (NSA_HEAD_DIM ** -0.5)).astype(BF16)
        os_ref[0, h] = attend(q, [kt_buf[h], head_rows(snew_ref, 0, h)],
                              [vt_buf[h], head_rows(snew_ref, 1, h)], bs_ref[0, h], svalid)


def _sample_attn(pages, halves, q8, cache_t, snew_t, bias_sel, win_t, wnew_t, bias_w, n_gather):
    bsz = q8.shape[0]
    wlen = win_t.shape[2]
    keys = n_gather * PAGE_SIZE
    rows = 2 * NSA_KV_HEADS * NSA_HEAD_DIM
    qspec = pl.BlockSpec((1, NSA_KV_HEADS, GROUP_ROWS, NSA_HEAD_DIM), lambda b, pg, hf: (b, 0, 0, 0))
    newspec = pl.BlockSpec((1, rows, PAGE_SIZE), lambda b, pg, hf: (b, 0, 0))
    return pl.pallas_call(
        functools.partial(_sample_attn_kernel, n_gather=n_gather, wlen=wlen),
        grid_spec=pltpu.PrefetchScalarGridSpec(
            num_scalar_prefetch=2, grid=(bsz,),
            in_specs=[qspec,
                      pl.BlockSpec(memory_space=pl.ANY),
                      newspec,
                      pl.BlockSpec((1, NSA_KV_HEADS, GROUP_ROWS, keys + PAGE_SIZE), lambda b, pg, hf: (b, 0, 0, 0)),
                      pl.BlockSpec((1, rows, wlen), lambda b, pg, hf: (b, 0, 0)),
                      newspec,
                      pl.BlockSpec((NSA_KV_HEADS, GROUP_ROWS, wlen + PAGE_SIZE), lambda b, pg, hf: (0, 0, 0))],
            out_specs=[qspec, qspec],
            scratch_shapes=[pltpu.VMEM((NSA_KV_HEADS, NSA_HEAD_DIM, keys), F32),
                            pltpu.VMEM((NSA_KV_HEADS, NSA_HEAD_DIM, keys), F32),
                            pltpu.SemaphoreType.DMA(())]),
        out_shape=[jax.ShapeDtypeStruct(q8.shape, F32), jax.ShapeDtypeStruct(q8.shape, F32)],
        compiler_params=_cparams(("arbitrary",)),
        name="sample_slc_win_attn",
    )(pages, halves, q8, cache_t, snew_t, bias_sel, win_t, wnew_t, bias_w)


PAGES_PER_STEP = 16


def _paged_pq_kernel(pt_ref, cache_hbm, pos_ref, w_ref, o_ref, pbuf, tok_scr, sem):
    b = pl.program_id(0)
    g = pl.program_id(1)
    ng = pl.num_programs(1)
    lin = b * ng + g
    slot = lin % 2
    rows_per_page = 2 * NSA_KV_HEADS * NSA_HEAD_DIM
    chunks = PAGES_PER_STEP * (PAGE_SIZE // CMP_STRIDE)

    def page_copy(step, p, buf_slot):
        sb = step // ng
        sg = step - sb * ng
        page = pt_ref[sb, sg * PAGES_PER_STEP + p]
        return pltpu.make_async_copy(cache_hbm.at[pl.ds(pl.multiple_of(page * rows_per_page, rows_per_page),
                                                        rows_per_page)],
                                     pbuf.at[buf_slot, p], sem.at[buf_slot])

    def fetch(step, buf_slot):
        for p in range(PAGES_PER_STEP):
            page_copy(step, p, buf_slot).start()

    @pl.when(lin == 0)
    def _():
        fetch(0, 0)

    @pl.when(lin + 1 < pl.num_programs(0) * ng)
    def _():
        fetch(lin + 1, 1 - slot)

    for p in range(PAGES_PER_STEP):
        page_copy(lin, p, slot).wait()

    for pair in range(NSA_KV_HEADS):
        for p in range(PAGES_PER_STEP):
            tok_scr[p * PAGE_SIZE:(p + 1) * PAGE_SIZE, :] = pbuf[slot, p, pair * LANE:(pair + 1) * LANE, :].T
        toks = [tok_scr[pl.ds(l, chunks, stride=CMP_STRIDE), :] for l in range(CMP_STRIDE)]
        for half in range(2):
            ch = 2 * pair + half
            c = ch // NSA_KV_HEADS
            lanes = slice(half * NSA_HEAD_DIM, (half + 1) * NSA_HEAD_DIM)
            chunk = jnp.concatenate([t[:, lanes] for t in toks], axis=1)
            for part in range(2):
                lhs = (chunk + pos_ref[c, part:part + 1, :]).astype(BF16)
                o_ref[0, ch, :, part * CMP_HIDDEN:(part + 1) * CMP_HIDDEN] = _dot(lhs, w_ref[c, part])


def _paged_pq(cache_t, page_table, cmp_pos, cmp_w1):
    db, n_pages = page_table.shape
    kdim = CMP_STRIDE * NSA_HEAD_DIM
    nch = n_pages * (PAGE_SIZE // CMP_STRIDE)
    tn = PAGES_PER_STEP * (PAGE_SIZE // CMP_STRIDE)
    rows_per_page = 2 * NSA_KV_HEADS * NSA_HEAD_DIM
    return pl.pallas_call(
        _paged_pq_kernel,
        grid_spec=pltpu.PrefetchScalarGridSpec(
            num_scalar_prefetch=1, grid=(db, n_pages // PAGES_PER_STEP),
            in_specs=[pl.BlockSpec(memory_space=pl.ANY),
                      pl.BlockSpec((2, 2, kdim), lambda b, g, pt: (0, 0, 0)),
                      pl.BlockSpec((2, 2, kdim, CMP_HIDDEN), lambda b, g, pt: (0, 0, 0, 0))],
            out_specs=pl.BlockSpec((1, 8, tn, 2 * CMP_HIDDEN), lambda b, g, pt: (b, 0, g, 0)),
            scratch_shapes=[pltpu.VMEM((2, PAGES_PER_STEP, rows_per_page, PAGE_SIZE), F32),
                            pltpu.VMEM((PAGES_PER_STEP * PAGE_SIZE, LANE), F32),
                            pltpu.SemaphoreType.DMA((2,))]),
        out_shape=jax.ShapeDtypeStruct((db, 8, nch, 2 * CMP_HIDDEN), F32),
        compiler_params=_cparams(("arbitrary", "arbitrary")),
        name="cmp_pq_paged",
    )(page_table, cache_t, cmp_pos.reshape(2, 2, kdim), cmp_w1.reshape(2, 2, kdim, CMP_HIDDEN).astype(BF16))


def _pad_rows(a, n):
    return jnp.concatenate([a, jnp.zeros((n - a.shape[0],) + a.shape[1:], a.dtype)], axis=0)


def _sample_mixer(x_sample, cache_cmp, cache_slc, cache_win, st_c, st_n, st_m, st_conv, page_table,
                  norm_g, wb, gate_bias, conv_w, ml_norm_g, cmp_pos, cmp_w1, cmp_w2, nsa_norm_g, w_out_b, rel_bias):
    db = x_sample.shape[0]
    n_pages = page_table.shape[1]
    past = n_pages * PAGE_SIZE
    tok = 16
    x16 = _pad_rows(x_sample.reshape(db, D_MODEL), tok)
    u = _proj(x16, norm_g, wb, tok, 512)
    small = u[:, COL_SMALL:COL_SMALL + LANE]

    T = 128
    useq = jnp.zeros((db, T, 4 * D_ML), F32)
    useq = useq.at[:, T - CONV_W:T - 1, 0:2 * D_ML].set(st_conv)
    useq = useq.at[:, T - 1, :].set(u[:db, 0:4 * D_ML])
    sseq = jnp.zeros((db, T, LANE), F32).at[:, :T - 1, LANE - 1].set(1.0)
    sseq = sseq.at[:, T - 1, :].set(small[:db])
    y_seq, conv_n, c_n, n_n, m_n = _mlstm(
        useq.reshape(db * T, 4 * D_ML), sseq.reshape(db * T, LANE), jnp.zeros((db, CONV_W - 1, 2 * D_ML), F32),
        st_c, st_n, st_m, conv_w, gate_bias, ml_norm_g, db, T, T)
    y_ml = y_seq.reshape(db, T, D_ML)[:, T - 1]

    n_pool = cache_cmp.shape[0]
    cmp_t = cache_cmp.transpose(0, 2, 3, 4, 1).reshape(n_pool * 2 * D_KV, PAGE_SIZE)
    kvc = _cmp_hid(_paged_pq(cmp_t, page_table, cmp_pos, cmp_w1), cmp_w2)
    nch = past // CMP_STRIDE
    n_cmp = (past + 1) // CMP_STRIDE - CMP_BLOCK // CMP_STRIDE + 1
    n_slc = -(-(past + 1) // SLC_BLOCK)
    q = u[:db, COL_QNSA:COL_QNSA + D_NSA].reshape(db, NSA_KV_HEADS, NSA_GROUP, NSA_HEAD_DIM)
    q8 = jnp.concatenate([q, jnp.zeros_like(q)], axis=2)
    bd = _bias_by_distance(rel_bias, past + 1).reshape(NSA_KV_HEADS, NSA_GROUP, past + 1)
    pad_g = lambda a: jnp.concatenate([a, jnp.zeros_like(a)], axis=1)
    dist_c = np.clip(past - (np.arange(nch) * CMP_STRIDE + CMP_BLOCK - 1), 0, None)
    o_c8, sel = _sample_cattn(q8, kvc, pad_g(bd[:, :, dist_c]), n_cmp, n_slc)

    mask = sel[:, :, 0, :n_slc] > 0.5
    idx = jnp.sort(jnp.where(mask, jnp.arange(n_slc, dtype=jnp.int32), jnp.int32(1 << 20)), axis=-1)
    n_gather = SLC_TOPN - 1
    idx = idx[..., :n_gather]
    pages_per_block = PAGE_SIZE // SLC_BLOCK
    logical_page = idx // pages_per_block
    pages = jnp.take_along_axis(page_table[:, None, :], logical_page, axis=2).reshape(-1).astype(jnp.int32)
    halves = (idx % pages_per_block).reshape(-1).astype(jnp.int32)
    kpos = (logical_page[..., None] * PAGE_SIZE + jnp.arange(PAGE_SIZE, dtype=jnp.int32)).reshape(db, NSA_KV_HEADS, -1)
    hh = jnp.arange(NSA_KV_HEADS)[None, :, None, None]
    gg = jnp.arange(NSA_GROUP)[None, None, :, None]
    bias_sel = bd[hh, gg, (past - kpos)[:, :, None, :]]
    bias_sel = jnp.concatenate([bias_sel, jnp.broadcast_to(bd[None, :, :, 0:1], (db, NSA_KV_HEADS, NSA_GROUP, 1)),
                                jnp.zeros((db, NSA_KV_HEADS, NSA_GROUP, PAGE_SIZE - 1), F32)], axis=-1)
    bias_sel = jnp.concatenate([bias_sel, jnp.zeros_like(bias_sel)], axis=2)
    wlen = cache_win.shape[1]
    dist_w = np.clip(wlen - np.arange(wlen + PAGE_SIZE), 0, None)
    kvs_new = _kv_from_per_head(u[:db, COL_KVS:COL_KVS + 2 * D_KV])
    kvw_new = _kv_from_per_head(u[:db, COL_KVW:COL_KVW + 2 * D_KV])
    lane_pad = lambda a: jnp.pad(a[:, :, None], ((0, 0), (0, 0), (0, PAGE_SIZE - 1)))
    slc_t = cache_slc.transpose(0, 2, 3, 4, 1).reshape(n_pool * 2 * D_KV, PAGE_SIZE)
    win_t = cache_win.transpose(0, 2, 3, 4, 1).reshape(db, 2 * D_KV, wlen)
    o_s8, o_w8 = _sample_attn(pages, halves, q8, slc_t, lane_pad(kvs_new), bias_sel, win_t, lane_pad(kvw_new),
                              pad_g(bd[:, :, dist_w]), n_gather)
    win2d = cache_win.reshape(db, wlen, 2 * D_KV)

    heads = lambda o: _pad_rows(o[:, :, :NSA_GROUP, :].reshape(db, D_NSA), tok)
    y = _outproj(_pad_rows(y_ml, tok), heads(o_c8), heads(o_s8), heads(o_w8), small, nsa_norm_g, w_out_b, x16, tok, 512)
    kvshape = (1, db, 1, 2, NSA_KV_HEADS, NSA_HEAD_DIM)
    new_win = jnp.concatenate([win2d[:, 1:], kvw_new[:, None, :]], axis=1)
    states = (u[:db, COL_KVC:COL_KVC + 2 * D_KV].reshape(kvshape), kvs_new.reshape(kvshape),
              new_win.reshape((1, db, wlen, 2, NSA_KV_HEADS, NSA_HEAD_DIM)),
              c_n[None], n_n[None], m_n[None], conv_n[None])
    return y, states


def kernel(x_prompt, x_sample, cache_cmp_kv, cache_slc_kv, cache_win_kv, state_mlstm_C, state_mlstm_n,
           state_mlstm_m, state_conv, page_table, rel_bias, norm_mix_g, w_in, b_ig, b_fg, conv_w, ml_norm_g,
           cmp_pos, cmp_w1, cmp_w2, nsa_norm_g, w_out, norm_ffn_g, w_router_grp, b_router_grp, w_router_exp,
           b_router_exp, w_gate, w_up, w_down, norm_final_g):
    B, S, D = x_prompt.shape
    wb = _reorder_w_in(w_in[0])
    gate_bias = jnp.zeros((1, LANE), F32).at[0, 0:ML_HEADS].set(b_ig[0]).at[0, ML_HEADS:2 * ML_HEADS].set(b_fg[0])
    w_out_b = w_out[0].astype(BF16)
    yp, st_p = _prompt_mixer(x_prompt, norm_mix_g, wb, gate_bias, conv_w[0], ml_norm_g, cmp_pos[0], cmp_w1[0],
                             cmp_w2[0], nsa_norm_g, w_out_b, rel_bias)
    ys, st_s = _sample_mixer(x_sample, cache_cmp_kv[0], cache_slc_kv[0], cache_win_kv[0], state_mlstm_C[0],
                             state_mlstm_n[0], state_mlstm_m[0], state_conv[0], page_table, norm_mix_g, wb, gate_bias,
                             conv_w[0], ml_norm_g, cmp_pos[0], cmp_w1[0], cmp_w2[0], nsa_norm_g, w_out_b, rel_bias)
    moe_w = (norm_ffn_g, w_router_grp[0], b_router_grp[0], w_router_exp[0], b_router_exp[0],
             w_gate[0], w_up[0], w_down[0], norm_final_g[None, :])
    DB, L, _ = x_sample.shape
    out_p = _moe_final(yp.reshape(B * S, D), *moe_w, 256, 128).reshape(B, S, D)
    out_s = _moe_final(_pad_rows(ys, 128), *moe_w, 128, 16)[:DB].reshape(DB, L, D)
    outs = [out_p, out_s]
    for a, b in zip(st_p, st_s):
        outs += [a, b]
    return tuple(outs)


def _prompt_mixer(x_prompt, norm_g, wb, gate_bias, conv_w, ml_norm_g, cmp_pos, cmp_w1, cmp_w2, nsa_norm_g,
                  w_out_b, rel_bias):
    B, S, D = x_prompt.shape
    x2d = x_prompt.reshape(B * S, D)
    tm = min(1024, B * S)
    u = _proj(x2d, norm_g, wb, tm, 512)
    small = u[:, COL_SMALL:COL_SMALL + LANE]
    y_ml, conv_n, c_n, n_n, m_n = _mlstm(
        u, small, jnp.zeros((B, CONV_W - 1, 2 * D_ML), F32),
        jnp.zeros((B, ML_HEADS, ML_HEAD_DIM, ML_HEAD_DIM), F32), jnp.zeros((B, ML_HEADS, ML_HEAD_DIM), F32),
        jnp.full((B, ML_HEADS), -jnp.inf, F32), conv_w, gate_bias, ml_norm_g, B, S, 256)
    kv_c = u[:, COL_KVC:COL_KVC + 2 * D_KV]
    kv_s = _kv_from_per_head(u[:, COL_KVS:COL_KVS + 2 * D_KV])
    kv_w = _kv_from_per_head(u[:, COL_KVW:COL_KVW + 2 * D_KV])
    nch = S // CMP_STRIDE
    kvc = _compress(u, COL_KVC // (2 * D_KV), B, S, cmp_pos, cmp_w1, cmp_w2, min(S, 4096))
    tq_c = min(S, 256)
    o_c, sel_t = _cattn(u, kvc, rel_bias, B, S, tq_c)
    o_s = _flash(u, COL_KVS, rel_bias, sel_t, B, S)
    o_w = _flash(u, COL_KVW, rel_bias, None, B, S)
    y = _outproj(y_ml, o_c, o_s, o_w, small, nsa_norm_g, w_out_b, x2d, tm, 512)
    kvshape = (1, B, S, 2, NSA_KV_HEADS, NSA_HEAD_DIM)
    win = min(WINDOW, S)
    states = (kv_c.reshape(kvshape), kv_s.reshape(kvshape), kv_w.reshape(kvshape)[:, :, S - win:],
              c_n[None], n_n[None], m_n[None], conv_n[None])
    return y.reshape(B, S, D), states
```

```python
import functools
import math

import numpy as np
import jax
import jax.numpy as jnp
from jax import lax
from jax.experimental import pallas as pl
from jax.experimental.pallas import tpu as pltpu

F32 = jnp.float32
BF16 = jnp.bfloat16

D_MODEL = 2048
ML_HEADS = 4
ML_HEAD_DIM = 256
D_ML = 1024
CONV_W = 4
NSA_HEADS = 16
NSA_HEAD_DIM = 64
D_NSA = 1024
NSA_KV_HEADS = 4
NSA_GROUP = 4
D_KV = 256
CMP_BLOCK = 32
CMP_STRIDE = 16
CMP_HIDDEN = 256
SLC_BLOCK = 64
SLC_TOPN = 16
WINDOW = 512
N_BUCKETS = 32
MAX_DISTANCE = 2048
N_GROUPS = 4
EXPERTS_PER_GROUP = 8
N_EXPERTS = 32
D_EXPERT = 512
PAGE_SIZE = 128
EPS = 1e-6
NEG_BIG = -1e30
FORCE_SCORE = 1e4

LANE = 128
COL_QML, COL_KML, COL_VML, COL_OML = 0, 1024, 2048, 3072
COL_QNSA = 4096
COL_KVC, COL_KVS, COL_KVW = 5120, 5632, 6144
COL_SMALL = 6656
N_PROJ = 7168
VMEM_LIMIT = 56 * 1024 * 1024


def _cparams(sem, vmem=VMEM_LIMIT):
    return pltpu.CompilerParams(dimension_semantics=sem, vmem_limit_bytes=vmem)


def _split2(x):
    hi = x.astype(BF16)
    lo = (x - hi.astype(F32)).astype(BF16)
    return hi, lo


def _split3(x):
    hi = x.astype(BF16)
    r = x - hi.astype(F32)
    mid = r.astype(BF16)
    lo = (r - mid.astype(F32)).astype(BF16)
    return hi, mid, lo


def _dot(a, b):
    return jnp.dot(a, b, preferred_element_type=F32)


def _dot_nt(a, b):
    return lax.dot_general(a, b, (((1,), (1,)), ((), ())), preferred_element_type=F32)


def _dot_tn(a, b):
    return lax.dot_general(a, b, (((0,), (0,)), ((), ())), preferred_element_type=F32)


def _proj_kernel(x_ref, g_ref, w_ref, o_ref, h_scr):
    @pl.when(pl.program_id(1) == 0)
    def _():
        x = x_ref[...]
        ms = jnp.mean(x * x, axis=-1, keepdims=True)
        h_scr[...] = (x * lax.rsqrt(ms + EPS) * g_ref[...]).astype(BF16)

    o_ref[...] = _dot(h_scr[...], w_ref[...])


def _proj(x2d, g, wb, tm, tn):
    n, d = x2d.shape
    nc = wb.shape[1]
    return pl.pallas_call(
        _proj_kernel,
        grid=(n // tm, nc // tn),
        in_specs=[pl.BlockSpec((tm, d), lambda i, j: (i, 0)),
                  pl.BlockSpec((1, d), lambda i, j: (0, 0)),
                  pl.BlockSpec((d, tn), lambda i, j: (0, j))],
        out_specs=pl.BlockSpec((tm, tn), lambda i, j: (i, j)),
        out_shape=jax.ShapeDtypeStruct((n, nc), F32),
        scratch_shapes=[pltpu.VMEM((tm, d), BF16)],
        compiler_params=_cparams(("arbitrary", "arbitrary")),
        name="proj",
    )(x2d, g, wb)


def _reorder_w_in(w_in):
    big = w_in[:, :4 * D_ML]
    small_a = w_in[:, 4 * D_ML:4 * D_ML + 2 * ML_HEADS]
    rest = w_in[:, 4 * D_ML + 2 * ML_HEADS:]
    q_and_cmp = rest[:, :D_NSA + 2 * D_KV]
    gate = rest[:, D_NSA + 6 * D_KV:]
    d = w_in.shape[0]

    def per_head(w):
        return w.reshape(d, 2, NSA_KV_HEADS, NSA_HEAD_DIM).transpose(0, 2, 1, 3).reshape(d, 2 * D_KV)

    kv_s = per_head(rest[:, D_NSA + 2 * D_KV:D_NSA + 4 * D_KV])
    kv_w = per_head(rest[:, D_NSA + 4 * D_KV:D_NSA + 6 * D_KV])
    pad = jnp.zeros((d, N_PROJ - COL_SMALL - 2 * ML_HEADS - 3 * NSA_HEADS), w_in.dtype)
    return jnp.concatenate([big, q_and_cmp, kv_s, kv_w, small_a, gate, pad], axis=1).astype(BF16)


def _kv_from_per_head(kv2d):
    n = kv2d.shape[0]
    return kv2d.reshape(n, NSA_KV_HEADS, 2, NSA_HEAD_DIM).transpose(0, 2, 1, 3).reshape(n, 2 * D_KV)


def _log_sigmoid(x):
    return jnp.minimum(x, 0.0) - jnp.log1p(jnp.exp(-jnp.abs(x)))


def _mlstm_kernel(q_ref, k_ref, v_ref, o_ref, s_ref, cb_ref, c0_ref, n0_ref, m0_ref,
                  cw_ref, gb_ref, ng_ref,
                  y_ref, cbo_ref, co_ref, no_ref, mo_ref,
                  ext_scr, c_scr, n_scr, m_scr, *, T):
    c = pl.program_id(1)
    nc = pl.num_programs(1)

    @pl.when(c == 0)
    def _():
        ext_scr[0:8, :] = jnp.zeros((8, 2 * D_ML), F32)
        ext_scr[5:8, :] = cb_ref[0]
        c_scr[...] = c0_ref[0]
        n_scr[...] = n0_ref[0]
        m_scr[...] = m0_ref[0]

    ext_scr[8:8 + T, 0:D_ML] = q_ref[...]
    ext_scr[8:8 + T, D_ML:2 * D_ML] = k_ref[...]
    conv = ext_scr[5:5 + T, :] * cw_ref[0:1, :]
    for j in range(1, CONV_W):
        conv = conv + ext_scr[5 + j:5 + j + T, :] * cw_ref[j:j + 1, :]
    tail = ext_scr[8 + T - 3:8 + T, :]
    ext_scr[5:8, :] = tail
    cbo_ref[0] = tail
    qk = conv * jax.nn.sigmoid(conv)

    pre = s_ref[...] + gb_ref[...]
    col = lax.broadcasted_iota(jnp.int32, pre.shape, 1)
    padrow = s_ref[:, LANE - 1:LANE] > 0.5
    gates = jnp.where(col < ML_HEADS, pre, _log_sigmoid(pre))
    gates = jnp.where(padrow, jnp.where(col < ML_HEADS, NEG_BIG, 0.0), gates)
    g_r = gates.T
    ti = lax.broadcasted_iota(jnp.int32, (T, T), 0)
    si = lax.broadcasted_iota(jnp.int32, (T, T), 1)
    upper = (ti <= si).astype(BF16)
    g_fin = jnp.where(lax.broadcasted_iota(jnp.int32, g_r.shape, 0) < ML_HEADS, 0.0, g_r)
    hi, mid, lo = _split3(g_fin)
    cum_r = _dot(hi, upper) + _dot(mid, upper) + _dot(lo, upper)
    rowi = lax.broadcasted_iota(jnp.int32, g_r.shape, 0)
    a_r = jnp.where(rowi < ML_HEADS, g_r, cum_r)
    a_c = a_r.T
    causal = si <= ti

    for h in range(ML_HEADS):
        sl = slice(h * ML_HEAD_DIM, (h + 1) * ML_HEAD_DIM)
        q = qk[:, h * ML_HEAD_DIM:(h + 1) * ML_HEAD_DIM]
        k = qk[:, D_ML + h * ML_HEAD_DIM:D_ML + (h + 1) * ML_HEAD_DIM] * (ML_HEAD_DIM ** -0.5)
        v = v_ref[:, sl]
        ig_r = a_r[h:h + 1, :]
        b_r = a_r[ML_HEADS + h:ML_HEADS + h + 1, :]
        ig_c = a_c[:, h:h + 1]
        b_c = a_c[:, ML_HEADS + h:ML_HEADS + h + 1]
        m_prev = m_scr[h:h + 1, 0:1]
        logd = jnp.where(causal, b_c - b_r + ig_r, -jnp.inf)
        inter = b_c + m_prev
        m_t = jnp.maximum(inter, jnp.max(logd, axis=1, keepdims=True))
        w_intra = jnp.exp(logd - m_t)
        w_inter = jnp.exp(inter - m_t)
        qb = q.astype(BF16)
        kb = k.astype(BF16)
        vb = v.astype(BF16)
        sc = _dot_nt(qb, kb) * w_intra
        cmat = c_scr[h]
        nvec = n_scr[h:h + 1, :]
        num = _dot(sc.astype(BF16), vb) + w_inter * _dot(qb, cmat.astype(BF16))
        qn = jnp.sum(qb.astype(F32) * nvec.astype(BF16).astype(F32), axis=1, keepdims=True)
        den = jnp.sum(sc, axis=1, keepdims=True) + w_inter * qn
        hh = num / jnp.maximum(jnp.abs(den), jnp.exp(-m_t))
        m_new = m_t[T - 1:T, :]
        b_last = b_c[T - 1:T, :]
        w_s = jnp.exp(b_last - b_c + ig_c - m_new)
        decay = jnp.exp(b_last + m_prev - m_new)
        kw = k * w_s
        c_new = decay * cmat + _dot_tn(kw.astype(BF16), vb)
        n_new = decay * nvec + jnp.sum(kw, axis=0, keepdims=True)
        c_scr[h] = c_new
        n_scr[h:h + 1, :] = n_new
        m_scr[h:h + 1, :] = jnp.broadcast_to(m_new, (1, LANE))
        hn = hh * lax.rsqrt(jnp.mean(hh * hh, axis=1, keepdims=True) + EPS) * ng_ref[:, sl]
        y_ref[:, sl] = (hn * jax.nn.sigmoid(o_ref[:, sl])).astype(y_ref.dtype)

    @pl.when(c == nc - 1)
    def _():
        co_ref[0] = c_scr[...]
        no_ref[0] = n_scr[...]
        mo_ref[0] = m_scr[...]


def _mlstm(u, small, conv_buf, c0, n0, m0, conv_w, gate_bias, norm_g, batch, seq, T):
    nc = seq // T
    cb = D_ML // 1024
    m0b = jnp.broadcast_to(m0[:, :, None], (batch, ML_HEADS, LANE))
    m0b = jnp.concatenate([m0b, jnp.zeros((batch, 8 - ML_HEADS, LANE), F32)], axis=1)
    n0p = jnp.concatenate([n0, jnp.zeros((batch, 8 - ML_HEADS, ML_HEAD_DIM), F32)], axis=1)
    row = lambda b, c: (b * nc + c, 0)
    outs = pl.pallas_call(
        functools.partial(_mlstm_kernel, T=T),
        grid=(batch, nc),
        in_specs=[pl.BlockSpec((T, D_ML), lambda b, c: (b * nc + c, COL_QML // D_ML)),
                  pl.BlockSpec((T, D_ML), lambda b, c: (b * nc + c, COL_KML // D_ML)),
                  pl.BlockSpec((T, D_ML), lambda b, c: (b * nc + c, COL_VML // D_ML)),
                  pl.BlockSpec((T, D_ML), lambda b, c: (b * nc + c, COL_OML // D_ML)),
                  pl.BlockSpec((T, LANE), row),
                  pl.BlockSpec((1, CONV_W - 1, 2 * D_ML), lambda b, c: (b, 0, 0)),
                  pl.BlockSpec((1, ML_HEADS, ML_HEAD_DIM, ML_HEAD_DIM), lambda b, c: (b, 0, 0, 0)),
                  pl.BlockSpec((1, 8, ML_HEAD_DIM), lambda b, c: (b, 0, 0)),
                  pl.BlockSpec((1, 8, LANE), lambda b, c: (b, 0, 0)),
                  pl.BlockSpec((CONV_W, 2 * D_ML), lambda b, c: (0, 0)),
                  pl.BlockSpec((1, LANE), lambda b, c: (0, 0)),
                  pl.BlockSpec((1, D_ML), lambda b, c: (0, 0))],
        out_specs=[pl.BlockSpec((T, D_ML), row),
                   pl.BlockSpec((1, CONV_W - 1, 2 * D_ML), lambda b, c: (b, 0, 0)),
                   pl.BlockSpec((1, ML_HEADS, ML_HEAD_DIM, ML_HEAD_DIM), lambda b, c: (b, 0, 0, 0)),
                   pl.BlockSpec((1, 8, ML_HEAD_DIM), lambda b, c: (b, 0, 0)),
                   pl.BlockSpec((1, 8, LANE), lambda b, c: (b, 0, 0))],
        out_shape=[jax.ShapeDtypeStruct((batch * seq, D_ML), BF16),
                   jax.ShapeDtypeStruct((batch, CONV_W - 1, 2 * D_ML), F32),
                   jax.ShapeDtypeStruct((batch, ML_HEADS, ML_HEAD_DIM, ML_HEAD_DIM), F32),
                   jax.ShapeDtypeStruct((batch, 8, ML_HEAD_DIM), F32),
                   jax.ShapeDtypeStruct((batch, 8, LANE), F32)],
        scratch_shapes=[pltpu.VMEM((8 + T, 2 * D_ML), F32),
                        pltpu.VMEM((ML_HEADS, ML_HEAD_DIM, ML_HEAD_DIM), F32),
                        pltpu.VMEM((8, ML_HEAD_DIM), F32),
                        pltpu.VMEM((8, LANE), F32)],
        compiler_params=_cparams(("arbitrary", "arbitrary")),
        name="mlstm",
    )(u, u, u, u, small, conv_buf, c0, n0p, m0b, conv_w, gate_bias, norm_g)
    y, cbo, co, no, mo = outs
    return y, cbo, co, no[:, :ML_HEADS], mo[:, :ML_HEADS, 0]


def _bucket_np(dist):
    n = np.maximum(dist, 0)
    max_exact = N_BUCKETS // 2
    nf = np.maximum(n, 1).astype(np.float64)
    large = max_exact + (np.log(nf / max_exact) / math.log(MAX_DISTANCE / max_exact)
                         * (N_BUCKETS - max_exact)).astype(np.int64)
    return np.where(n < max_exact, n, np.minimum(large, N_BUCKETS - 1)).astype(np.int32)


def _bias_by_distance(rel_bias, n):
    return rel_bias.astype(F32)[_bucket_np(np.arange(n))].T


def _overlap_t(n_cmp, nch, n_slc):
    c0 = np.arange(nch) * CMP_STRIDE
    s0 = np.arange(n_slc) * SLC_BLOCK
    ov = np.minimum(c0[None, :] + CMP_BLOCK, s0[:, None] + SLC_BLOCK) - np.maximum(c0[None, :], s0[:, None])
    ov = np.clip(ov, 0, None).astype(np.float32) / CMP_BLOCK
    ov[:, n_cmp:] = 0.0
    return jnp.asarray(ov, BF16)


def _pq_kernel(x0_ref, x1_ref, x2_ref, x3_ref, pos_ref, w_ref, o_ref, *, rows):
    for pair, x_ref in enumerate((x0_ref, x1_ref, x2_ref, x3_ref)):
        toks = [x_ref[pl.ds(l, rows, stride=CMP_STRIDE), :] for l in range(CMP_STRIDE)]
        for half in range(2):
            ch = 2 * pair + half
            c = ch // NSA_KV_HEADS
            lanes = slice(half * NSA_HEAD_DIM, (half + 1) * NSA_HEAD_DIM)
            chunk = jnp.concatenate([t[:, lanes] for t in toks], axis=1)
            for part in range(2):
                lhs = (chunk + pos_ref[c, part:part + 1, :]).astype(BF16)
                o_ref[0, ch, :, part * CMP_HIDDEN:(part + 1) * CMP_HIDDEN] = _dot(lhs, w_ref[c, part])


def _hid_kernel(pq_ref, w2_ref, o_ref, *, nch):
    p = pq_ref[0, 0, :, 0:CMP_HIDDEN]
    q = pltpu.roll(pq_ref[0, 0, :, CMP_HIDDEN:2 * CMP_HIDDEN], nch - 1, 0)
    hid = jax.nn.gelu(p + q, approximate=True)
    o_ref[0, 0] = _dot(hid.astype(BF16), w2_ref[0])


def _compress(kv2d, col_block, bsz, seq, cmp_pos, cmp_w1, cmp_w2, tt):
    nch = seq // CMP_STRIDE
    kdim = CMP_STRIDE * NSA_HEAD_DIM
    nt = seq // tt
    tn = tt // CMP_STRIDE
    w1 = cmp_w1.reshape(2, 2, kdim, CMP_HIDDEN).astype(BF16)
    pos = cmp_pos.reshape(2, 2, kdim)
    pq = pl.pallas_call(
        functools.partial(_pq_kernel, rows=tn),
        grid=(bsz, nt),
        in_specs=[pl.BlockSpec((tt, LANE), functools.partial(lambda b, i, k: (b * nt + i, col_block * 4 + k), k=k))
                  for k in range(4)] +
                 [pl.BlockSpec((2, 2, kdim), lambda b, i: (0, 0, 0)),
                  pl.BlockSpec((2, 2, kdim, CMP_HIDDEN), lambda b, i: (0, 0, 0, 0))],
        out_specs=pl.BlockSpec((1, 8, tn, 2 * CMP_HIDDEN), lambda b, i: (b, 0, i, 0)),
        out_shape=jax.ShapeDtypeStruct((bsz, 8, nch, 2 * CMP_HIDDEN), F32),
        compiler_params=_cparams(("arbitrary", "arbitrary")),
        name="cmp_pq",
    )(kv2d, kv2d, kv2d, kv2d, pos, w1)
    return _cmp_hid(pq, cmp_w2)


def _cmp_hid(pq, cmp_w2):
    bsz, _, nch, _ = pq.shape
    return pl.pallas_call(
        functools.partial(_hid_kernel, nch=nch),
        grid=(bsz, 8),
        in_specs=[pl.BlockSpec((1, 1, nch, 2 * CMP_HIDDEN), lambda b, c: (b, c, 0, 0)),
                  pl.BlockSpec((1, CMP_HIDDEN, NSA_HEAD_DIM), lambda b, c: (c // NSA_KV_HEADS, 0, 0))],
        out_specs=pl.BlockSpec((1, 1, nch, NSA_HEAD_DIM), lambda b, c: (b, c, 0, 0)),
        out_shape=jax.ShapeDtypeStruct((bsz, 8, nch, NSA_HEAD_DIM), F32),
        compiler_params=_cparams(("arbitrary", "arbitrary")),
        name="cmp_hid",
    )(pq, cmp_w2.astype(BF16))


def _top_n_mask(score_t, blk, n_rows):
    rank = jnp.zeros(score_t.shape, jnp.int32)
    for i in range(n_rows):
        row = score_t[i:i + 1, :]
        beats = (row > score_t) | ((row == score_t) & (blk > i))
        rank = rank + beats.astype(jnp.int32)
    return (rank < min(SLC_TOPN, n_rows)).astype(F32)


def _bias_from_buckets(bucket, table_ref, head):
    bias = jnp.zeros(bucket.shape, F32)
    for k in range(N_BUCKETS):
        bias = jnp.where(bucket == k, table_ref[k, head], bias)
    return bias


def _cattn_kernel(tab_ref, q_ref, kc_ref, vc_ref, bk_ref, ovt_ref, o_ref, sel_ref, bias_scr,
                  *, tq, nch, n_cmp, n_slc):
    h = pl.program_id(0)
    i = pl.program_id(1)

    @pl.when(pl.program_id(2) == 0)
    def _():
        bucket = bk_ref[...]
        for g in range(NSA_GROUP):
            bias_scr[g] = _bias_from_buckets(bucket, tab_ref, h * NSA_GROUP + g)

    kc = kc_ref[0, 0].astype(BF16)
    vc = vc_ref[0, 0].astype(BF16)
    t = i * tq + lax.broadcasted_iota(jnp.int32, (tq, nch), 0)
    n = lax.broadcasted_iota(jnp.int32, (tq, nch), 1)
    valid = (t - CMP_STRIDE * n - (CMP_BLOCK - 1) >= 0) & (n < n_cmp)
    pc = jnp.zeros((tq, nch), F32)
    for g in range(NSA_GROUP):
        sl = slice(g * NSA_HEAD_DIM, (g + 1) * NSA_HEAD_DIM)
        qg = (q_ref[:, sl] * (NSA_HEAD_DIM ** -0.5)).astype(BF16)
        s = _dot_nt(qg, kc) + bias_scr[g]
        s = jnp.where(valid, s, NEG_BIG)
        e = jnp.exp(s - jnp.max(s, axis=1, keepdims=True))
        p = jnp.where(valid, e / jnp.sum(e, axis=1, keepdims=True), 0.0)
        o_ref[:, sl] = _dot(p.astype(BF16), vc)
        pc = pc + p
    score_t = _dot_nt(ovt_ref[...], pc.astype(BF16))
    blk = lax.broadcasted_iota(jnp.int32, (n_slc, tq), 0)
    cur = (i * tq + lax.broadcasted_iota(jnp.int32, (n_slc, tq), 1)) // SLC_BLOCK
    forced = (blk == 0) | (blk == cur) | (blk == cur - 1)
    score_t = jnp.where(forced, FORCE_SCORE, score_t)
    score_t = jnp.where(blk > cur, -FORCE_SCORE, score_t)
    sel_ref[0, 0] = _top_n_mask(score_t, blk, n_slc)


def _cattn(u, kvc, rel_bias, batch, seq, tq):
    nch = kvc.shape[2]
    n_cmp = nch - 1
    n_slc = seq // SLC_BLOCK
    ni = seq // tq
    ovt = _overlap_t(n_cmp, nch, n_slc)
    qcol = COL_QNSA // (NSA_GROUP * NSA_HEAD_DIM)
    dist = np.arange(seq)[:, None] - (CMP_STRIDE * np.arange(nch)[None, :] + CMP_BLOCK - 1)
    buckets = jnp.asarray(_bucket_np(dist))
    return pl.pallas_call(
        functools.partial(_cattn_kernel, tq=tq, nch=nch, n_cmp=n_cmp, n_slc=n_slc),
        grid=(NSA_KV_HEADS, ni, batch),
        in_specs=[pl.BlockSpec(memory_space=pltpu.SMEM),
                  pl.BlockSpec((tq, NSA_GROUP * NSA_HEAD_DIM), lambda h, i, b: (b * ni + i, qcol + h)),
                  pl.BlockSpec((1, 1, nch, NSA_HEAD_DIM), lambda h, i, b: (b, h, 0, 0)),
                  pl.BlockSpec((1, 1, nch, NSA_HEAD_DIM), lambda h, i, b: (b, NSA_KV_HEADS + h, 0, 0)),
                  pl.BlockSpec((tq, nch), lambda h, i, b: (i, 0)),
                  pl.BlockSpec((n_slc, nch), lambda h, i, b: (0, 0))],
        out_specs=[pl.BlockSpec((tq, NSA_GROUP * NSA_HEAD_DIM), lambda h, i, b: (b * ni + i, h)),
                   pl.BlockSpec((1, 1, n_slc, tq), lambda h, i, b: (b, h, 0, i))],
        out_shape=[jax.ShapeDtypeStruct((batch * seq, D_NSA), F32),
                   jax.ShapeDtypeStruct((batch, NSA_KV_HEADS, n_slc, seq), F32)],
        scratch_shapes=[pltpu.VMEM((NSA_GROUP, tq, nch), F32)],
        compiler_params=_cparams(("arbitrary", "arbitrary", "arbitrary")),
        name="cmp_attn",
    )(rel_bias.astype(F32), u, kvc, kvc, buckets, ovt)


TQ = 128
FLASH_GROUP = 8


def _flash_kernel(tab_ref, q_ref, kv_ref, bk_ref, *rest, selected, n_delta):
    if selected:
        sel_ref, o_ref, band_scr, qs_scr, m_scr, l_scr, acc_scr = rest
    else:
        o_ref, band_scr, qs_scr, m_scr, l_scr, acc_scr = rest
    h = pl.program_id(0)
    i = pl.program_id(2)

    @pl.when((pl.program_id(1) == 0) & (i == 0))
    def _():
        def fill(d, carry):
            bucket = bk_ref[d]
            for g in range(NSA_GROUP):
                band_scr[d, :, g * TQ:(g + 1) * TQ] = _bias_from_buckets(bucket, tab_ref, h * NSA_GROUP + g)
            return carry
        lax.fori_loop(0, n_delta, fill, 0)

    for g in range(NSA_GROUP):
        qg = (q_ref[:, g * NSA_HEAD_DIM:(g + 1) * NSA_HEAD_DIM] * (NSA_HEAD_DIM ** -0.5)).astype(BF16)
        qs_scr[g * TQ:(g + 1) * TQ, :] = jnp.concatenate([qg, jnp.zeros_like(qg)], axis=1)
    is_key_lane = lax.broadcasted_iota(jnp.int32, (TQ, LANE), 1) < NSA_HEAD_DIM
    m_scr[...] = jnp.full(m_scr.shape, 0.5 * NEG_BIG, F32)
    l_scr[...] = jnp.zeros(l_scr.shape, F32)
    acc_scr[...] = jnp.zeros(acc_scr.shape, F32)
    key = lax.broadcasted_iota(jnp.int32, (TQ, TQ), 0)
    qry = lax.broadcasted_iota(jnp.int32, (TQ, TQ), 1)
    n_back = WINDOW // TQ

    def scores(j, kind):
        kvj = kv_ref[pl.ds(pl.multiple_of(j * TQ, TQ), TQ), :].astype(BF16)
        s = _dot_nt(kvj, qs_scr[...]) + band_scr[i - j]
        mask = None
        if kind == "diag":
            mask = qry >= key
        elif kind == "far":
            mask = qry < key
        if selected:
            r = sel_ref[0, 0, pl.ds(2 * j, 2), :]
            picked = jnp.where(key < SLC_BLOCK, r[0:1, :], r[1:2, :]) > 0.5
            mask = picked if mask is None else (mask & picked)
        if mask is not None:
            s = jnp.concatenate([jnp.where(mask, s[:, g * TQ:(g + 1) * TQ], NEG_BIG) for g in range(NSA_GROUP)], axis=1)
        ones_v = jnp.where(is_key_lane, jnp.ones_like(kvj), kvj)
        return s, ones_v

    def update(tiles):
        m_old = m_scr[...]
        m_new = m_old
        for s, _ in tiles:
            m_new = jnp.maximum(m_new, jnp.max(s, axis=0, keepdims=True))
        alpha = jnp.exp(m_old - m_new)
        l_new = alpha * l_scr[...]
        acc = alpha * acc_scr[...]
        for s, ones_v in tiles:
            p = jnp.exp(s - m_new)
            pv = _dot_tn(ones_v, p.astype(BF16))
            l_new = l_new + pv[0:1, :]
            acc = acc + pv[NSA_HEAD_DIM:2 * NSA_HEAD_DIM, :]
        m_scr[...] = m_new
        l_scr[...] = l_new
        acc_scr[...] = acc

    def full_tiles(first, count):
        return [scores(first + k, "full") for k in range(count)]

    if selected:
        def group(t, carry):
            update(full_tiles(FLASH_GROUP * t, FLASH_GROUP))
            return carry

        n_groups = i // FLASH_GROUP
        lax.fori_loop(0, n_groups, group, 0)
        for rem in range(FLASH_GROUP):
            @pl.when(i - n_groups * FLASH_GROUP == rem)
            def _():
                update(full_tiles(n_groups * FLASH_GROUP, rem) + [scores(i, "diag")])
    else:
        @pl.when(i >= n_back)
        def _():
            update([scores(i - n_back, "far")] + full_tiles(i - n_back + 1, n_back - 1) + [scores(i, "diag")])

        for rem in range(n_back):
            @pl.when(i == rem)
            def _():
                update(full_tiles(0, rem) + [scores(i, "diag")])

    for g in range(NSA_GROUP):
        cols = slice(g * TQ, (g + 1) * TQ)
        o_ref[:, g * NSA_HEAD_DIM:(g + 1) * NSA_HEAD_DIM] = (acc_scr[:, cols] / l_scr[:, cols]).T


def _flash(u, kv_col, rel_bias, sel_t, batch, seq):
    ni = seq // TQ
    qcol = COL_QNSA // (NSA_GROUP * NSA_HEAD_DIM)
    selected = sel_t is not None
    n_delta = ni if selected else WINDOW // TQ + 1
    kvblk = kv_col // LANE
    delta = np.arange(n_delta)[:, None, None] * TQ + np.arange(TQ)[None, None, :] - np.arange(TQ)[None, :, None]
    buckets = jnp.asarray(_bucket_np(delta))
    in_specs = [pl.BlockSpec(memory_space=pltpu.SMEM),
                pl.BlockSpec((TQ, NSA_GROUP * NSA_HEAD_DIM), lambda h, b, i: (b * ni + i, qcol + h)),
                pl.BlockSpec((seq, LANE), lambda h, b, i: (b, kvblk + h)),
                pl.BlockSpec((n_delta, TQ, TQ), lambda h, b, i: (0, 0, 0))]
    args = [rel_bias.astype(F32), u, u, buckets]
    if selected:
        n_slc = sel_t.shape[2]
        in_specs.append(pl.BlockSpec((1, 1, n_slc, TQ), lambda h, b, i: (b, h, 0, i)))
        args.append(sel_t)
    return pl.pallas_call(
        functools.partial(_flash_kernel, selected=selected, n_delta=n_delta),
        grid=(NSA_KV_HEADS, batch, ni),
        in_specs=in_specs,
        out_specs=pl.BlockSpec((TQ, NSA_GROUP * NSA_HEAD_DIM), lambda h, b, i: (b * ni + i, h)),
        out_shape=jax.ShapeDtypeStruct((batch * seq, D_NSA), F32),
        scratch_shapes=[pltpu.VMEM((n_delta, TQ, NSA_GROUP * TQ), F32),
                        pltpu.VMEM((NSA_GROUP * TQ, LANE), BF16),
                        pltpu.VMEM((1, NSA_GROUP * TQ), F32),
                        pltpu.VMEM((1, NSA_GROUP * TQ), F32),
                        pltpu.VMEM((NSA_HEAD_DIM, NSA_GROUP * TQ), F32)],
        compiler_params=_cparams(("arbitrary", "arbitrary", "arbitrary")),
        name="slc_attn" if selected else "win_attn",
    )(*args)


def _gate_expand():
    e = np.zeros((3, LANE, D_NSA), np.float32)
    for br in range(3):
        for hd in range(NSA_HEADS):
            e[br, 2 * ML_HEADS + br * NSA_HEADS + hd, hd * NSA_HEAD_DIM:(hd + 1) * NSA_HEAD_DIM] = 1.0
    return jnp.asarray(e, BF16)


def _outproj_kernel(yml_ref, oc_ref, os_ref, ow_ref, s_ref, e_ref, ng_ref, w_ref, x_ref, o_ref, cat_scr):
    @pl.when(pl.program_id(1) == 0)
    def _():
        sig = jax.nn.sigmoid(s_ref[...])
        hi, lo = _split2(sig)
        o = jnp.zeros(oc_ref.shape, F32)
        for br, ref in enumerate((oc_ref, os_ref, ow_ref)):
            gexp = _dot(hi, e_ref[br]) + _dot(lo, e_ref[br])
            o = o + gexp * ref[...]
        o = o * lax.rsqrt(jnp.mean(o * o, axis=-1, keepdims=True) + EPS) * ng_ref[...]
        cat_scr[:, 0:D_ML] = yml_ref[...]
        cat_scr[:, D_ML:D_ML + D_NSA] = o.astype(BF16)

    o_ref[...] = x_ref[...] + _dot(cat_scr[...], w_ref[...])


def _outproj(y_ml, o_c, o_s, o_w, small, nsa_g, w_out_b, x2d, tm, tn):
    n, d = x2d.shape
    rows = lambda i, j: (i, 0)
    return pl.pallas_call(
        _outproj_kernel,
        grid=(n // tm, d // tn),
        in_specs=[pl.BlockSpec((tm, D_ML), rows), pl.BlockSpec((tm, D_NSA), rows),
                  pl.BlockSpec((tm, D_NSA), rows), pl.BlockSpec((tm, D_NSA), rows),
                  pl.BlockSpec((tm, LANE), rows),
                  pl.BlockSpec((3, LANE, D_NSA), lambda i, j: (0, 0, 0)),
                  pl.BlockSpec((1, D_NSA), lambda i, j: (0, 0)),
                  pl.BlockSpec((D_ML + D_NSA, tn), lambda i, j: (0, j)),
                  pl.BlockSpec((tm, tn), lambda i, j: (i, j))],
        out_specs=pl.BlockSpec((tm, tn), lambda i, j: (i, j)),
        out_shape=jax.ShapeDtypeStruct((n, d), F32),
        scratch_shapes=[pltpu.VMEM((tm, D_ML + D_NSA), BF16)],
        compiler_params=_cparams(("arbitrary", "arbitrary")),
        name="outproj",
    )(y_ml, o_c, o_s, o_w, small, _gate_expand(), nsa_g, w_out_b, x2d)


ROUTE_COL = N_GROUPS
BIG_COL = 1 << 20
SPLIT = D_MODEL // LANE
DMA_UNROLL = 8


def _store_split(ref, val, n):
    for k in range(SPLIT):
        ref[pl.ds(k, n, stride=SPLIT), :] = val[:, k * LANE:(k + 1) * LANE]


def _load_split(ref, n):
    return jnp.concatenate([ref[pl.ds(k, n, stride=SPLIT), :] for k in range(SPLIT)], axis=1)


def _route_kernel(y_ref, g_ref, w_ref, b_ref, h_ref, info_ref, cnt_ref, carry_scr, *, tm):
    @pl.when(pl.program_id(0) == 0)
    def _():
        carry_scr[...] = jnp.zeros(carry_scr.shape, F32)

    x = y_ref[...]
    h = x * lax.rsqrt(jnp.mean(x * x, axis=-1, keepdims=True) + EPS) * g_ref[...]
    _store_split(h_ref, h, tm)
    logit = _dot(h.astype(BF16), w_ref[...]) + b_ref[...]
    col = lax.broadcasted_iota(jnp.int32, logit.shape, 1)
    is_grp = col < N_GROUPS
    gmax = jnp.max(jnp.where(is_grp, logit, -jnp.inf), axis=1, keepdims=True)
    gtop = jnp.min(jnp.where(is_grp & (logit == gmax), col, BIG_COL), axis=1, keepdims=True)
    gsum = jnp.sum(jnp.where(is_grp, jnp.exp(logit - gmax), 0.0), axis=1, keepdims=True)
    first = ROUTE_COL + gtop * EXPERTS_PER_GROUP
    in_grp = (col >= first) & (col < first + EXPERTS_PER_GROUP)
    v1 = jnp.max(jnp.where(in_grp, logit, -jnp.inf), axis=1, keepdims=True)
    i1 = jnp.min(jnp.where(in_grp & (logit == v1), col, BIG_COL), axis=1, keepdims=True)
    rest = in_grp & (col != i1)
    v2 = jnp.max(jnp.where(rest, logit, -jnp.inf), axis=1, keepdims=True)
    i2 = jnp.min(jnp.where(rest & (logit == v2), col, BIG_COL), axis=1, keepdims=True)
    e = jnp.exp(v2 - v1)
    w1 = 1.0 / ((1.0 + e) * gsum)
    w2 = e / ((1.0 + e) * gsum)
    pick1 = col == i1
    pick2 = col == i2
    both = (pick1 | pick2).astype(F32)
    ri = lax.broadcasted_iota(jnp.int32, (tm, tm), 0)
    ci = lax.broadcasted_iota(jnp.int32, (tm, tm), 1)
    before = (ci < ri).astype(BF16)
    cum = _dot(before, both.astype(BF16)) + carry_scr[...]
    r1 = jnp.sum(jnp.where(pick1, cum, 0.0), axis=1, keepdims=True)
    r2 = jnp.sum(jnp.where(pick2, cum, 0.0), axis=1, keepdims=True)
    carry_scr[...] = carry_scr[...] + jnp.sum(both, axis=0, keepdims=True)
    cnt_ref[...] = jnp.broadcast_to(carry_scr[...], cnt_ref.shape)
    info = jnp.where(col == 0, (i1 - ROUTE_COL).astype(F32), 0.0)
    info = jnp.where(col == 1, (i2 - ROUTE_COL).astype(F32), info)
    info = jnp.where(col == 2, w1, info)
    info = jnp.where(col == 3, w2, info)
    info = jnp.where(col == 4, r1, info)
    info_ref[...] = jnp.where(col == 5, r2, info)


def _route(y2d, g, w_rg, b_rg, w_re, b_re, tm):
    n, d = y2d.shape
    wr = jnp.concatenate([w_rg, w_re, jnp.zeros((d, LANE - N_GROUPS - N_EXPERTS), F32)], axis=1).astype(BF16)
    bias = jnp.concatenate([b_rg, b_re, jnp.zeros((LANE - N_GROUPS - N_EXPERTS,), F32)])[None, :]
    rows = lambda i: (i, 0)
    fixed = lambda i: (0, 0)
    return pl.pallas_call(
        functools.partial(_route_kernel, tm=tm),
        grid=(n // tm,),
        in_specs=[pl.BlockSpec((tm, d), rows), pl.BlockSpec((1, d), fixed),
                  pl.BlockSpec((d, LANE), fixed), pl.BlockSpec((1, LANE), fixed)],
        out_specs=[pl.BlockSpec((tm * SPLIT, LANE), rows), pl.BlockSpec((tm, LANE), rows),
                   pl.BlockSpec((8, LANE), fixed)],
        out_shape=[jax.ShapeDtypeStruct((n * SPLIT, LANE), F32), jax.ShapeDtypeStruct((n, LANE), F32),
                   jax.ShapeDtypeStruct((8, LANE), F32)],
        scratch_shapes=[pltpu.VMEM((1, LANE), F32)],
        compiler_params=_cparams(("arbitrary",)),
        name="moe_route",
    )(y2d, g, wr, bias)


W_CHUNKS = 4


def _expert_kernel(te_ref, nu_ref, src_ref, nx_ref, ws_ref, h_hbm, wg_hbm, wu_hbm, wd_hbm, o_ref,
                   xbuf, wg_f, wu_f, wd_f, wg_b, wu_b, wd_b, sem, wsem, *, tme):
    i = pl.program_id(0)
    slot = i % 2

    def weight_copies(expert, wslot):
        copies = []
        for hbm, buf in ((wg_hbm, wg_f), (wu_hbm, wu_f), (wd_hbm, wd_f)):
            step = hbm.shape[1] // W_CHUNKS
            for c in range(W_CHUNKS):
                copies.append(pltpu.make_async_copy(hbm.at[expert, pl.ds(c * step, step)],
                                                    buf.at[wslot, pl.ds(c * step, step)], wsem.at[wslot]))
        return copies

    def row_copy(tile, r, buf_slot):
        src_row = src_ref[tile * tme + r]
        return pltpu.make_async_copy(h_hbm.at[pl.ds(pl.multiple_of(src_row * SPLIT, SPLIT), SPLIT)],
                                     xbuf.at[buf_slot, pl.ds(pl.multiple_of(r * SPLIT, SPLIT), SPLIT)],
                                     sem.at[buf_slot])

    def fetch(tile, buf_slot):
        def body(r, carry):
            row_copy(tile, r, buf_slot).start()
            return carry
        lax.fori_loop(0, tme, body, 0, unroll=DMA_UNROLL)

    @pl.when(i == 0)
    def _():
        fetch(0, 0)
        for cp in weight_copies(te_ref[0], ws_ref[0]):
            cp.start()

    @pl.when(i + 1 < nu_ref[0])
    def _():
        fetch(i + 1, 1 - slot)

    @pl.when(i < nu_ref[0])
    def _():
        prev = te_ref[jnp.maximum(i - 1, 0)]

        @pl.when((i == 0) | (te_ref[i] != prev))
        def _():
            w = ws_ref[i]
            for cp in weight_copies(te_ref[i], w):
                cp.wait()
            wg_b[...] = wg_f[w].astype(BF16)
            wu_b[...] = wu_f[w].astype(BF16)
            wd_b[...] = wd_f[w].astype(BF16)

            @pl.when(nx_ref[i] >= 0)
            def _():
                for cp in weight_copies(nx_ref[i], 1 - w):
                    cp.start()

        def wait_body(r, carry):
            row_copy(i, r, slot).wait()
            return carry
        lax.fori_loop(0, tme, wait_body, 0, unroll=DMA_UNROLL)

        x = _load_split(xbuf.at[slot], tme).astype(BF16)
        a = _dot(x, wg_b[...])
        u = _dot(x, wu_b[...])
        hid = a * jax.nn.sigmoid(a) * u
        _store_split(o_ref, _dot(hid.astype(BF16), wd_b[...]), tme)

    @pl.when(i >= nu_ref[0])
    def _():
        o_ref[...] = jnp.zeros(o_ref.shape, F32)


def _experts(h, src, tile_expert, n_used, next_expert, weight_slot, w_gate, w_up, w_down, tme):
    p = src.shape[0]
    d = D_MODEL
    hbm = pl.BlockSpec(memory_space=pl.ANY)
    return pl.pallas_call(
        functools.partial(_expert_kernel, tme=tme),
        grid_spec=pltpu.PrefetchScalarGridSpec(
            num_scalar_prefetch=5, grid=(p // tme,),
            in_specs=[hbm, hbm, hbm, hbm],
            out_specs=pl.BlockSpec((tme * SPLIT, LANE), lambda i, *_: (i, 0)),
            scratch_shapes=[pltpu.VMEM((2, tme * SPLIT, LANE), F32),
                            pltpu.VMEM((2, d, D_EXPERT), F32), pltpu.VMEM((2, d, D_EXPERT), F32),
                            pltpu.VMEM((2, D_EXPERT, d), F32),
                            pltpu.VMEM((d, D_EXPERT), BF16), pltpu.VMEM((d, D_EXPERT), BF16),
                            pltpu.VMEM((D_EXPERT, d), BF16),
                            pltpu.SemaphoreType.DMA((2,)), pltpu.SemaphoreType.DMA((2,))]),
        out_shape=jax.ShapeDtypeStruct((p * SPLIT, LANE), F32),
        compiler_params=_cparams(("arbitrary",)),
        name="moe_experts",
    )(tile_expert, n_used, src, next_expert, weight_slot, h, w_gate, w_up, w_down)


def _combine_kernel(d1_ref, d2_ref, y_ref, info_ref, ys_hbm, g_ref, o_ref, a_buf, b_buf, sem, *, tm):
    i = pl.program_id(0)
    slot = i % 2

    def copy(r, row, buf, buf_slot):
        return pltpu.make_async_copy(ys_hbm.at[pl.ds(pl.multiple_of(row * SPLIT, SPLIT), SPLIT)],
                                     buf.at[buf_slot, pl.ds(pl.multiple_of(r * SPLIT, SPLIT), SPLIT)],
                                     sem.at[buf_slot])

    def fetch(tile, buf_slot):
        def issue(r, carry):
            copy(r, d1_ref[tile * tm + r], a_buf, buf_slot).start()
            copy(r, d2_ref[tile * tm + r], b_buf, buf_slot).start()
            return carry
        lax.fori_loop(0, tm, issue, 0, unroll=DMA_UNROLL)

    @pl.when(i == 0)
    def _():
        fetch(0, 0)

    @pl.when(i + 1 < pl.num_programs(0))
    def _():
        fetch(i + 1, 1 - slot)

    def drain(r, carry):
        copy(r, 0, a_buf, slot).wait()
        copy(r, 0, b_buf, slot).wait()
        return carry

    lax.fori_loop(0, tm, drain, 0, unroll=DMA_UNROLL)
    w1 = info_ref[:, 2:3]
    w2 = info_ref[:, 3:4]
    y = y_ref[...] + w1 * _load_split(a_buf.at[slot], tm) + w2 * _load_split(b_buf.at[slot], tm)
    o_ref[...] = y * lax.rsqrt(jnp.mean(y * y, axis=-1, keepdims=True) + EPS) * g_ref[...]


def _combine(y2d, info, ys, d1, d2, g, tm):
    n, d = y2d.shape
    return pl.pallas_call(
        functools.partial(_combine_kernel, tm=tm),
        grid_spec=pltpu.PrefetchScalarGridSpec(
            num_scalar_prefetch=2, grid=(n // tm,),
            in_specs=[pl.BlockSpec((tm, d), lambda i, a, b: (i, 0)),
                      pl.BlockSpec((tm, LANE), lambda i, a, b: (i, 0)),
                      pl.BlockSpec(memory_space=pl.ANY),
                      pl.BlockSpec((1, d), lambda i, a, b: (0, 0))],
            out_specs=pl.BlockSpec((tm, d), lambda i, a, b: (i, 0)),
            scratch_shapes=[pltpu.VMEM((2, tm * SPLIT, LANE), F32), pltpu.VMEM((2, tm * SPLIT, LANE), F32),
                            pltpu.SemaphoreType.DMA((2,))]),
        out_shape=jax.ShapeDtypeStruct((n, d), F32),
        compiler_params=_cparams(("arbitrary",)),
        name="moe_combine",
    )(d1, d2, y2d, info, ys, g)


def _moe_final(y2d, ffn_g, w_rg, b_rg, w_re, b_re, w_gate, w_up, w_down, final_g, tm, tme):
    n, d = y2d.shape
    h, info, cnt = _route(y2d, ffn_g, w_rg, b_rg, w_re, b_re, tm)
    e1 = info[:, 0].astype(jnp.int32)
    e2 = info[:, 1].astype(jnp.int32)
    counts = cnt[0, ROUTE_COL:ROUTE_COL + N_EXPERTS].astype(jnp.int32)
    padded = (counts + tme - 1) // tme * tme
    ends = jnp.cumsum(padded)
    offs = ends - padded
    d1 = offs[e1] + info[:, 4].astype(jnp.int32)
    d2 = offs[e2] + info[:, 5].astype(jnp.int32)
    p = 2 * n + N_EXPERTS * tme
    tok = jnp.arange(n, dtype=jnp.int32)
    src = jnp.zeros((p,), jnp.int32).at[jnp.concatenate([d1, d2])].set(jnp.concatenate([tok, tok]))
    n_tiles = p // tme
    n_used = (ends[-1] // tme).astype(jnp.int32).reshape(1)
    tile_start = jnp.arange(n_tiles, dtype=jnp.int32) * tme
    tile_expert = jnp.sum((ends[None, :] <= tile_start[:, None]).astype(jnp.int32), axis=1)
    last = jnp.sum((ends <= ends[-1] - 1).astype(jnp.int32))
    tile_expert = jnp.minimum(tile_expert, last).astype(jnp.int32)
    run = jnp.cumsum(jnp.concatenate([jnp.zeros((1,), jnp.int32),
                                      (tile_expert[1:] != tile_expert[:-1]).astype(jnp.int32)]))
    weight_slot = (run % 2).astype(jnp.int32)
    after = ends[tile_expert] // tme
    next_expert = jnp.where(after < n_used[0], tile_expert[jnp.minimum(after, n_tiles - 1)], -1).astype(jnp.int32)
    ys = _experts(h, src, tile_expert, n_used, next_expert, weight_slot, w_gate, w_up, w_down, tme)
    return _combine(y2d, info, ys, d1, d2, final_g, tm)


GROUP_ROWS = 8
SLC_LANES = 384
SEL_KEYS = 1024
WIN_KEYS = 640


def _sample_cattn_kernel(q_ref, kc_ref, vc_ref, b_ref, ov_ref, o_ref, sel_ref, *, n_cmp, n_slc):
    q = (q_ref[0, 0] * (NSA_HEAD_DIM ** -0.5)).astype(BF16)
    s = _dot_nt(q, kc_ref[0, 0].astype(BF16)) + b_ref[0]
    valid = lax.broadcasted_iota(jnp.int32, s.shape, 1) < n_cmp
    s = jnp.where(valid, s, NEG_BIG)
    e = jnp.exp(s - jnp.max(s, axis=1, keepdims=True))
    p = jnp.where(valid, e / jnp.sum(e, axis=1, keepdims=True), 0.0)
    o_ref[0, 0] = _dot(p.astype(BF16), vc_ref[0, 0].astype(BF16))
    pg = jnp.where(lax.broadcasted_iota(jnp.int32, p.shape, 0) < NSA_GROUP, p, 0.0)
    pc = jnp.broadcast_to(jnp.sum(pg, axis=0, keepdims=True), pg.shape)
    score = _dot(pc.astype(BF16), ov_ref[...])[0:1, :]
    blk = lax.broadcasted_iota(jnp.int32, score.shape, 1)
    cur = n_slc - 1
    forced = (blk == 0) | (blk == cur) | (blk == cur - 1)
    score = jnp.where(forced, FORCE_SCORE, score)
    score = jnp.where(blk > cur, -jnp.inf, score)
    ri = lax.broadcasted_iota(jnp.int32, (SLC_LANES, SLC_LANES), 0)
    ci = lax.broadcasted_iota(jnp.int32, (SLC_LANES, SLC_LANES), 1)
    sb = jnp.broadcast_to(score, (SLC_LANES, SLC_LANES))
    col = jnp.sum(jnp.where(ri == ci, sb, 0.0), axis=1, keepdims=True)
    beats = (col > sb) | ((col == sb) & (ri < ci))
    rank = jnp.sum(beats.astype(jnp.int32), axis=0, keepdims=True)
    sel_ref[0, 0] = jnp.broadcast_to((rank < SLC_TOPN).astype(F32), (GROUP_ROWS, SLC_LANES))


def _sample_cattn(q8, kvc, bias_cs, n_cmp, n_slc):
    bsz, _, nch, _ = kvc.shape
    c0 = np.arange(nch) * CMP_STRIDE
    s0 = np.arange(SLC_LANES) * SLC_BLOCK
    ov = np.minimum(c0[:, None] + CMP_BLOCK, s0[None, :] + SLC_BLOCK) - np.maximum(c0[:, None], s0[None, :])
    ov = np.clip(ov, 0, None).astype(np.float32) / CMP_BLOCK
    ov[n_cmp:, :] = 0.0
    ov[:, n_slc:] = 0.0
    return pl.pallas_call(
        functools.partial(_sample_cattn_kernel, n_cmp=n_cmp, n_slc=n_slc),
        grid=(bsz, NSA_KV_HEADS),
        in_specs=[pl.BlockSpec((1, 1, GROUP_ROWS, NSA_HEAD_DIM), lambda b, h: (b, h, 0, 0)),
                  pl.BlockSpec((1, 1, nch, NSA_HEAD_DIM), lambda b, h: (b, h, 0, 0)),
                  pl.BlockSpec((1, 1, nch, NSA_HEAD_DIM), lambda b, h: (b, NSA_KV_HEADS + h, 0, 0)),
                  pl.BlockSpec((1, GROUP_ROWS, nch), lambda b, h: (h, 0, 0)),
                  pl.BlockSpec((nch, SLC_LANES), lambda b, h: (0, 0))],
        out_specs=[pl.BlockSpec((1, 1, GROUP_ROWS, NSA_HEAD_DIM), lambda b, h: (b, h, 0, 0)),
                   pl.BlockSpec((1, 1, GROUP_ROWS, SLC_LANES), lambda b, h: (b, h, 0, 0))],
        out_shape=[jax.ShapeDtypeStruct((bsz, NSA_KV_HEADS, GROUP_ROWS, NSA_HEAD_DIM), F32),
                   jax.ShapeDtypeStruct((bsz, NSA_KV_HEADS, GROUP_ROWS, SLC_LANES), F32)],
        compiler_params=_cparams(("arbitrary", "arbitrary")),
        name="sample_cmp_attn",
    )(q8, kvc, kvc, bias_cs, jnp.asarray(ov, BF16))


def _sample_attn_kernel(pg_ref, hf_ref, q_ref, cache_hbm, snew_ref, bs_ref, win_ref, wnew_ref, bw_ref,
                        os_ref, ow_ref, kt_buf, vt_buf, sem, *, n_gather, wlen):
    b = pl.program_id(0)
    rows_per_page = 2 * NSA_KV_HEADS * NSA_HEAD_DIM

    def page_copy(h, s, c, buf):
        page = pg_ref[(b * NSA_KV_HEADS + h) * n_gather + s]
        start = pl.multiple_of(page * rows_per_page + (c * NSA_KV_HEADS + h) * NSA_HEAD_DIM, NSA_HEAD_DIM)
        return pltpu.make_async_copy(cache_hbm.at[pl.ds(start, NSA_HEAD_DIM)],
                                     buf.at[h, :, pl.ds(s * PAGE_SIZE, PAGE_SIZE)], sem)

    for h in range(NSA_KV_HEADS):
        for s in range(n_gather):
            page_copy(h, s, 0, kt_buf).start()
            page_copy(h, s, 1, vt_buf).start()

    def attend(q, kts, vts, bias, valid):
        s = jnp.concatenate([_dot(q, kt.astype(BF16)) for kt in kts], axis=1) + bias
        s = jnp.where(valid, s, NEG_BIG)
        e = jnp.exp(s - jnp.max(s, axis=1, keepdims=True))
        p = jnp.where(valid, e / jnp.sum(e, axis=1, keepdims=True), 0.0).astype(BF16)
        out = None
        lo = 0
        for vt in vts:
            n = vt.shape[1]
            part = _dot_nt(p[:, lo:lo + n], vt.astype(BF16))
            out = part if out is None else out + part
            lo += n
        return out

    def head_rows(ref, c, h):
        r0 = (c * NSA_KV_HEADS + h) * NSA_HEAD_DIM
        return ref[0, r0:r0 + NSA_HEAD_DIM, :]

    widx = lax.broadcasted_iota(jnp.int32, (GROUP_ROWS, wlen + PAGE_SIZE), 1)
    wvalid = (widx >= wlen + 1 - WINDOW) & (widx <= wlen)
    for h in range(NSA_KV_HEADS):
        q = (q_ref[0, h] * (NSA_HEAD_DIM ** -0.5)).astype(BF16)
        ow_ref[0, h] = attend(q, [head_rows(win_ref, 0, h), head_rows(wnew_ref, 0, h)],
                              [head_rows(win_ref, 1, h), head_rows(wnew_ref, 1, h)], bw_ref[h], wvalid)

    for h in range(NSA_KV_HEADS):
        for s in range(n_gather):
            page_copy(h, s, 0, kt_buf).wait()
            page_copy(h, s, 1, vt_buf).wait()

    keys = n_gather * PAGE_SIZE
    lane = lax.broadcasted_iota(jnp.int32, (GROUP_ROWS, keys + PAGE_SIZE), 1)
    slot = lane >> 7
    lane_half = (lane >> 6) & 1
    for h in range(NSA_KV_HEADS):
        want = jnp.full(lane.shape, -1, jnp.int32)
        for s in range(n_gather):
            want = jnp.where(slot == s, hf_ref[(b * NSA_KV_HEADS + h) * n_gather + s], want)
        svalid = (lane_half == want) | (lane == keys)
        q = (q_ref[0, h] * (NSA_HEAD_DIM ** -0.5)).astype(BF16)
        os_ref[0, h] = attend(q, [kt_buf[h], head_rows(snew_ref, 0, h)],
                              [vt_buf[h], head_rows(snew_ref, 1, h)], bs_ref[0, h], svalid)


def _sample_attn(pages, halves, q8, cache_t, snew_t, bias_sel, win_t, wnew_t, bias_w, n_gather):
    bsz = q8.shape[0]
    wlen = win_t.shape[2]
    keys = n_gather * PAGE_SIZE
    rows = 2 * NSA_KV_HEADS * NSA_HEAD_DIM
    qspec = pl.BlockSpec((1, NSA_KV_HEADS, GROUP_ROWS, NSA_HEAD_DIM), lambda b, pg, hf: (b, 0, 0, 0))
    newspec = pl.BlockSpec((1, rows, PAGE_SIZE), lambda b, pg, hf: (b, 0, 0))
    return pl.pallas_call(
        functools.partial(_sample_attn_kernel, n_gather=n_gather, wlen=wlen),
        grid_spec=pltpu.PrefetchScalarGridSpec(
            num_scalar_prefetch=2, grid=(bsz,),
            in_specs=[qspec,
                      pl.BlockSpec(memory_space=pl.ANY),
                      newspec,
                      pl.BlockSpec((1, NSA_KV_HEADS, GROUP_ROWS, keys + PAGE_SIZE), lambda b, pg, hf: (b, 0, 0, 0)),
                      pl.BlockSpec((1, rows, wlen), lambda b, pg, hf: (b, 0, 0)),
                      newspec,
                      pl.BlockSpec((NSA_KV_HEADS, GROUP_ROWS, wlen + PAGE_SIZE), lambda b, pg, hf: (0, 0, 0))],
            out_specs=[qspec, qspec],
            scratch_shapes=[pltpu.VMEM((NSA_KV_HEADS, NSA_HEAD_DIM, keys), F32),
                            pltpu.VMEM((NSA_KV_HEADS, NSA_HEAD_DIM, keys), F32),
                            pltpu.SemaphoreType.DMA(())]),
        out_shape=[jax.ShapeDtypeStruct(q8.shape, F32), jax.ShapeDtypeStruct(q8.shape, F32)],
        compiler_params=_cparams(("arbitrary",)),
        name="sample_slc_win_attn",
    )(pages, halves, q8, cache_t, snew_t, bias_sel, win_t, wnew_t, bias_w)


PAGES_PER_STEP = 16


def _paged_pq_kernel(pt_ref, cache_hbm, pos_ref, w_ref, o_ref, pbuf, tok_scr, sem):
    b = pl.program_id(0)
    g = pl.program_id(1)
    ng = pl.num_programs(1)
    lin = b * ng + g
    slot = lin % 2
    rows_per_page = 2 * NSA_KV_HEADS * NSA_HEAD_DIM
    chunks = PAGES_PER_STEP * (PAGE_SIZE // CMP_STRIDE)

    def page_copy(step, p, buf_slot):
        sb = step // ng
        sg = step - sb * ng
        page = pt_ref[sb, sg * PAGES_PER_STEP + p]
        return pltpu.make_async_copy(cache_hbm.at[pl.ds(pl.multiple_of(page * rows_per_page, rows_per_page),
                                                        rows_per_page)],
                                     pbuf.at[buf_slot, p], sem.at[buf_slot])

    def fetch(step, buf_slot):
        for p in range(PAGES_PER_STEP):
            page_copy(step, p, buf_slot).start()

    @pl.when(lin == 0)
    def _():
        fetch(0, 0)

    @pl.when(lin + 1 < pl.num_programs(0) * ng)
    def _():
        fetch(lin + 1, 1 - slot)

    for p in range(PAGES_PER_STEP):
        page_copy(lin, p, slot).wait()

    for c in range(2):
        head_chunks = []
        for pair in (2 * c, 2 * c + 1):
            for p in range(PAGES_PER_STEP):
                tok_scr[p * PAGE_SIZE:(p + 1) * PAGE_SIZE, :] = pbuf[slot, p, pair * LANE:(pair + 1) * LANE, :].T
            toks = [tok_scr[pl.ds(l, chunks, stride=CMP_STRIDE), :] for l in range(CMP_STRIDE)]
            for half in range(2):
                lanes = slice(half * NSA_HEAD_DIM, (half + 1) * NSA_HEAD_DIM)
                head_chunks.append(jnp.concatenate([t[:, lanes] for t in toks], axis=1))
        stacked = jnp.concatenate(head_chunks, axis=0)
        for part in range(2):
            lhs = (stacked + pos_ref[c, part:part + 1, :]).astype(BF16)
            out = _dot(lhs, w_ref[c, part])
            for hh in range(NSA_KV_HEADS):
                o_ref[0, c * NSA_KV_HEADS + hh, :, part * CMP_HIDDEN:(part + 1) * CMP_HIDDEN] = (
                    out[hh * chunks:(hh + 1) * chunks, :])


def _paged_pq(cache_t, page_table, cmp_pos, cmp_w1):
    db, n_pages = page_table.shape
    kdim = CMP_STRIDE * NSA_HEAD_DIM
    nch = n_pages * (PAGE_SIZE // CMP_STRIDE)
    tn = PAGES_PER_STEP * (PAGE_SIZE // CMP_STRIDE)
    rows_per_page = 2 * NSA_KV_HEADS * NSA_HEAD_DIM
    return pl.pallas_call(
        _paged_pq_kernel,
        grid_spec=pltpu.PrefetchScalarGridSpec(
            num_scalar_prefetch=1, grid=(db, n_pages // PAGES_PER_STEP),
            in_specs=[pl.BlockSpec(memory_space=pl.ANY),
                      pl.BlockSpec((2, 2, kdim), lambda b, g, pt: (0, 0, 0)),
                      pl.BlockSpec((2, 2, kdim, CMP_HIDDEN), lambda b, g, pt: (0, 0, 0, 0))],
            out_specs=pl.BlockSpec((1, 8, tn, 2 * CMP_HIDDEN), lambda b, g, pt: (b, 0, g, 0)),
            scratch_shapes=[pltpu.VMEM((2, PAGES_PER_STEP, rows_per_page, PAGE_SIZE), F32),
                            pltpu.VMEM((PAGES_PER_STEP * PAGE_SIZE, LANE), F32),
                            pltpu.SemaphoreType.DMA((2,))]),
        out_shape=jax.ShapeDtypeStruct((db, 8, nch, 2 * CMP_HIDDEN), F32),
        compiler_params=_cparams(("arbitrary", "arbitrary")),
        name="cmp_pq_paged",
    )(page_table, cache_t, cmp_pos.reshape(2, 2, kdim), cmp_w1.reshape(2, 2, kdim, CMP_HIDDEN).astype(BF16))


def _pad_rows(a, n):
    return jnp.concatenate([a, jnp.zeros((n - a.shape[0],) + a.shape[1:], a.dtype)], axis=0)


def _sample_mixer(x_sample, cache_cmp, cache_slc, cache_win, st_c, st_n, st_m, st_conv, page_table,
                  norm_g, wb, gate_bias, conv_w, ml_norm_g, cmp_pos, cmp_w1, cmp_w2, nsa_norm_g, w_out_b, rel_bias):
    db = x_sample.shape[0]
    n_pages = page_table.shape[1]
    past = n_pages * PAGE_SIZE
    tok = 16
    x16 = _pad_rows(x_sample.reshape(db, D_MODEL), tok)
    u = _proj(x16, norm_g, wb, tok, 512)
    small = u[:, COL_SMALL:COL_SMALL + LANE]

    T = 128
    useq = jnp.zeros((db, T, 4 * D_ML), F32)
    useq = useq.at[:, T - CONV_W:T - 1, 0:2 * D_ML].set(st_conv)
    useq = useq.at[:, T - 1, :].set(u[:db, 0:4 * D_ML])
    sseq = jnp.zeros((db, T, LANE), F32).at[:, :T - 1, LANE - 1].set(1.0)
    sseq = sseq.at[:, T - 1, :].set(small[:db])
    y_seq, conv_n, c_n, n_n, m_n = _mlstm(
        useq.reshape(db * T, 4 * D_ML), sseq.reshape(db * T, LANE), jnp.zeros((db, CONV_W - 1, 2 * D_ML), F32),
        st_c, st_n, st_m, conv_w, gate_bias, ml_norm_g, db, T, T)
    y_ml = y_seq.reshape(db, T, D_ML)[:, T - 1]

    n_pool = cache_cmp.shape[0]
    cmp_t = cache_cmp.transpose(0, 2, 3, 4, 1).reshape(n_pool * 2 * D_KV, PAGE_SIZE)
    kvc = _cmp_hid(_paged_pq(cmp_t, page_table, cmp_pos, cmp_w1), cmp_w2)
    nch = past // CMP_STRIDE
    n_cmp = (past + 1) // CMP_STRIDE - CMP_BLOCK // CMP_STRIDE + 1
    n_slc = -(-(past + 1) // SLC_BLOCK)
    q = u[:db, COL_QNSA:COL_QNSA + D_NSA].reshape(db, NSA_KV_HEADS, NSA_GROUP, NSA_HEAD_DIM)
    q8 = jnp.concatenate([q, jnp.zeros_like(q)], axis=2)
    bd = _bias_by_distance(rel_bias, past + 1).reshape(NSA_KV_HEADS, NSA_GROUP, past + 1)
    pad_g = lambda a: jnp.concatenate([a, jnp.zeros_like(a)], axis=1)
    dist_c = np.clip(past - (np.arange(nch) * CMP_STRIDE + CMP_BLOCK - 1), 0, None)
    o_c8, sel = _sample_cattn(q8, kvc, pad_g(bd[:, :, dist_c]), n_cmp, n_slc)

    mask = sel[:, :, 0, :n_slc] > 0.5
    idx = jnp.sort(jnp.where(mask, jnp.arange(n_slc, dtype=jnp.int32), jnp.int32(1 << 20)), axis=-1)
    n_gather = SLC_TOPN - 1
    idx = idx[..., :n_gather]
    pages_per_block = PAGE_SIZE // SLC_BLOCK
    logical_page = idx // pages_per_block
    pages = jnp.take_along_axis(page_table[:, None, :], logical_page, axis=2).reshape(-1).astype(jnp.int32)
    halves = (idx % pages_per_block).reshape(-1).astype(jnp.int32)
    kpos = (logical_page[..., None] * PAGE_SIZE + jnp.arange(PAGE_SIZE, dtype=jnp.int32)).reshape(db, NSA_KV_HEADS, -1)
    hh = jnp.arange(NSA_KV_HEADS)[None, :, None, None]
    gg = jnp.arange(NSA_GROUP)[None, None, :, None]
    bias_sel = bd[hh, gg, (past - kpos)[:, :, None, :]]
    bias_sel = jnp.concatenate([bias_sel, jnp.broadcast_to(bd[None, :, :, 0:1], (db, NSA_KV_HEADS, NSA_GROUP, 1)),
                                jnp.zeros((db, NSA_KV_HEADS, NSA_GROUP, PAGE_SIZE - 1), F32)], axis=-1)
    bias_sel = jnp.concatenate([bias_sel, jnp.zeros_like(bias_sel)], axis=2)
    wlen = cache_win.shape[1]
    dist_w = np.clip(wlen - np.arange(wlen + PAGE_SIZE), 0, None)
    kvs_new = _kv_from_per_head(u[:db, COL_KVS:COL_KVS + 2 * D_KV])
    kvw_new = _kv_from_per_head(u[:db, COL_KVW:COL_KVW + 2 * D_KV])
    lane_pad = lambda a: jnp.pad(a[:, :, None], ((0, 0), (0, 0), (0, PAGE_SIZE - 1)))
    slc_t = cache_slc.transpose(0, 2, 3, 4, 1).reshape(n_pool * 2 * D_KV, PAGE_SIZE)
    win_t = cache_win.transpose(0, 2, 3, 4, 1).reshape(db, 2 * D_KV, wlen)
    o_s8, o_w8 = _sample_attn(pages, halves, q8, slc_t, lane_pad(kvs_new), bias_sel, win_t, lane_pad(kvw_new),
                              pad_g(bd[:, :, dist_w]), n_gather)
    win2d = cache_win.reshape(db, wlen, 2 * D_KV)

    heads = lambda o: _pad_rows(o[:, :, :NSA_GROUP, :].reshape(db, D_NSA), tok)
    y = _outproj(_pad_rows(y_ml, tok), heads(o_c8), heads(o_s8), heads(o_w8), small, nsa_norm_g, w_out_b, x16, tok, 512)
    kvshape = (1, db, 1, 2, NSA_KV_HEADS, NSA_HEAD_DIM)
    new_win = jnp.concatenate([win2d[:, 1:], kvw_new[:, None, :]], axis=1)
    states = (u[:db, COL_KVC:COL_KVC + 2 * D_KV].reshape(kvshape), kvs_new.reshape(kvshape),
              new_win.reshape((1, db, wlen, 2, NSA_KV_HEADS, NSA_HEAD_DIM)),
              c_n[None], n_n[None], m_n[None], conv_n[None])
    return y, states


def kernel(x_prompt, x_sample, cache_cmp_kv, cache_slc_kv, cache_win_kv, state_mlstm_C, state_mlstm_n,
           state_mlstm_m, state_conv, page_table, rel_bias, norm_mix_g, w_in, b_ig, b_fg, conv_w, ml_norm_g,
           cmp_pos, cmp_w1, cmp_w2, nsa_norm_g, w_out, norm_ffn_g, w_router_grp, b_router_grp, w_router_exp,
           b_router_exp, w_gate, w_up, w_down, norm_final_g):
    B, S, D = x_prompt.shape
    wb = _reorder_w_in(w_in[0])
    gate_bias = jnp.zeros((1, LANE), F32).at[0, 0:ML_HEADS].set(b_ig[0]).at[0, ML_HEADS:2 * ML_HEADS].set(b_fg[0])
    w_out_b = w_out[0].astype(BF16)
    yp, st_p = _prompt_mixer(x_prompt, norm_mix_g, wb, gate_bias, conv_w[0], ml_norm_g, cmp_pos[0], cmp_w1[0],
                             cmp_w2[0], nsa_norm_g, w_out_b, rel_bias)
    ys, st_s = _sample_mixer(x_sample, cache_cmp_kv[0], cache_slc_kv[0], cache_win_kv[0], state_mlstm_C[0],
                             state_mlstm_n[0], state_mlstm_m[0], state_conv[0], page_table, norm_mix_g, wb, gate_bias,
                             conv_w[0], ml_norm_g, cmp_pos[0], cmp_w1[0], cmp_w2[0], nsa_norm_g, w_out_b, rel_bias)
    moe_w = (norm_ffn_g, w_router_grp[0], b_router_grp[0], w_router_exp[0], b_router_exp[0],
             w_gate[0], w_up[0], w_down[0], norm_final_g[None, :])
    DB, L, _ = x_sample.shape
    out_p = _moe_final(yp.reshape(B * S, D), *moe_w, 256, 256).reshape(B, S, D)
    out_s = _moe_final(_pad_rows(ys, 128), *moe_w, 128, 16)[:DB].reshape(DB, L, D)
    outs = [out_p, out_s]
    for a, b in zip(st_p, st_s):
        outs += [a, b]
    return tuple(outs)


def _prompt_mixer(x_prompt, norm_g, wb, gate_bias, conv_w, ml_norm_g, cmp_pos, cmp_w1, cmp_w2, nsa_norm_g,
                  w_out_b, rel_bias):
    B, S, D = x_prompt.shape
    x2d = x_prompt.reshape(B * S, D)
    tm = min(1024, B * S)
    u = _proj(x2d, norm_g, wb, tm, 512)
    small = u[:, COL_SMALL:COL_SMALL + LANE]
    y_ml, conv_n, c_n, n_n, m_n = _mlstm(
        u, small, jnp.zeros((B, CONV_W - 1, 2 * D_ML), F32),
        jnp.zeros((B, ML_HEADS, ML_HEAD_DIM, ML_HEAD_DIM), F32), jnp.zeros((B, ML_HEADS, ML_HEAD_DIM), F32),
        jnp.full((B, ML_HEADS), -jnp.inf, F32), conv_w, gate_bias, ml_norm_g, B, S, 256)
    kv_c = u[:, COL_KVC:COL_KVC + 2 * D_KV]
    kv_s = _kv_from_per_head(u[:, COL_KVS:COL_KVS + 2 * D_KV])
    kv_w = _kv_from_per_head(u[:, COL_KVW:COL_KVW + 2 * D_KV])
    nch = S // CMP_STRIDE
    kvc = _compress(u, COL_KVC // (2 * D_KV), B, S, cmp_pos, cmp_w1, cmp_w2, min(S, 4096))
    tq_c = min(S, 256)
    o_c, sel_t = _cattn(u, kvc, rel_bias, B, S, tq_c)
    o_s = _flash(u, COL_KVS, rel_bias, sel_t, B, S)
    o_w = _flash(u, COL_KVW, rel_bias, None, B, S)
    y = _outproj(y_ml, o_c, o_s, o_w, small, nsa_norm_g, w_out_b, x2d, tm, 512)
    kvshape = (1, B, S, 2, NSA_KV_HEADS, NSA_HEAD_DIM)
    win = min(WINDOW, S)
    states = (kv_c.reshape(kvshape), kv_s.reshape(kvshape), kv_w.reshape(kvshape)[:, :, S - win:],
              c_n[None], n_n[None], m_n[None], conv_n[None])
    return y.reshape(B, S, D), states
```

```python
import functools
import math

import numpy as np
import jax
import jax.numpy as jnp
from jax import lax
from jax.experimental import pallas as pl
from jax.experimental.pallas import tpu as pltpu

F32 = jnp.float32
BF16 = jnp.bfloat16

D_MODEL = 2048
ML_HEADS = 4
ML_HEAD_DIM = 256
D_ML = 1024
CONV_W = 4
NSA_HEADS = 16
NSA_HEAD_DIM = 64
D_NSA = 1024
NSA_KV_HEADS = 4
NSA_GROUP = 4
D_KV = 256
CMP_BLOCK = 32
CMP_STRIDE = 16
CMP_HIDDEN = 256
SLC_BLOCK = 64
SLC_TOPN = 16
WINDOW = 512
N_BUCKETS = 32
MAX_DISTANCE = 2048
N_GROUPS = 4
EXPERTS_PER_GROUP = 8
N_EXPERTS = 32
D_EXPERT = 512
PAGE_SIZE = 128
EPS = 1e-6
NEG_BIG = -1e30
FORCE_SCORE = 1e4

LANE = 128
COL_QML, COL_KML, COL_VML, COL_OML = 0, 1024, 2048, 3072
COL_QNSA = 4096
COL_KVC, COL_KVS, COL_KVW = 5120, 5632, 6144
COL_SMALL = 6656
N_PROJ = 7168
VMEM_LIMIT = 56 * 1024 * 1024


def _cparams(sem, vmem=VMEM_LIMIT):
    return pltpu.CompilerParams(dimension_semantics=sem, vmem_limit_bytes=vmem)


def _split2(x):
    hi = x.astype(BF16)
    lo = (x - hi.astype(F32)).astype(BF16)
    return hi, lo


def _split3(x):
    hi = x.astype(BF16)
    r = x - hi.astype(F32)
    mid = r.astype(BF16)
    lo = (r - mid.astype(F32)).astype(BF16)
    return hi, mid, lo


def _dot(a, b):
    return jnp.dot(a, b, preferred_element_type=F32)


def _dot_nt(a, b):
    return lax.dot_general(a, b, (((1,), (1,)), ((), ())), preferred_element_type=F32)


def _dot_tn(a, b):
    return lax.dot_general(a, b, (((0,), (0,)), ((), ())), preferred_element_type=F32)


def _proj_kernel(x_ref, g_ref, w_ref, o_ref, h_scr):
    @pl.when(pl.program_id(1) == 0)
    def _():
        x = x_ref[...]
        ms = jnp.mean(x * x, axis=-1, keepdims=True)
        h_scr[...] = (x * lax.rsqrt(ms + EPS) * g_ref[...]).astype(BF16)

    o_ref[...] = _dot(h_scr[...], w_ref[...])


def _proj(x2d, g, wb, tm, tn):
    n, d = x2d.shape
    nc = wb.shape[1]
    return pl.pallas_call(
        _proj_kernel,
        grid=(n // tm, nc // tn),
        in_specs=[pl.BlockSpec((tm, d), lambda i, j: (i, 0)),
                  pl.BlockSpec((1, d), lambda i, j: (0, 0)),
                  pl.BlockSpec((d, tn), lambda i, j: (0, j))],
        out_specs=pl.BlockSpec((tm, tn), lambda i, j: (i, j)),
        out_shape=jax.ShapeDtypeStruct((n, nc), F32),
        scratch_shapes=[pltpu.VMEM((tm, d), BF16)],
        compiler_params=_cparams(("arbitrary", "arbitrary")),
        name="proj",
    )(x2d, g, wb)


def _reorder_w_in(w_in):
    big = w_in[:, :4 * D_ML]
    small_a = w_in[:, 4 * D_ML:4 * D_ML + 2 * ML_HEADS]
    rest = w_in[:, 4 * D_ML + 2 * ML_HEADS:]
    q_and_cmp = rest[:, :D_NSA + 2 * D_KV]
    gate = rest[:, D_NSA + 6 * D_KV:]
    d = w_in.shape[0]

    def per_head(w):
        return w.reshape(d, 2, NSA_KV_HEADS, NSA_HEAD_DIM).transpose(0, 2, 1, 3).reshape(d, 2 * D_KV)

    kv_s = per_head(rest[:, D_NSA + 2 * D_KV:D_NSA + 4 * D_KV])
    kv_w = per_head(rest[:, D_NSA + 4 * D_KV:D_NSA + 6 * D_KV])
    pad = jnp.zeros((d, N_PROJ - COL_SMALL - 2 * ML_HEADS - 3 * NSA_HEADS), w_in.dtype)
    return jnp.concatenate([big, q_and_cmp, kv_s, kv_w, small_a, gate, pad], axis=1).astype(BF16)


def _kv_from_per_head(kv2d):
    n = kv2d.shape[0]
    return kv2d.reshape(n, NSA_KV_HEADS, 2, NSA_HEAD_DIM).transpose(0, 2, 1, 3).reshape(n, 2 * D_KV)


def _log_sigmoid(x):
    return jnp.minimum(x, 0.0) - jnp.log1p(jnp.exp(-jnp.abs(x)))


def _mlstm_kernel(q_ref, k_ref, v_ref, o_ref, s_ref, cb_ref, c0_ref, n0_ref, m0_ref,
                  cw_ref, gb_ref, ng_ref,
                  y_ref, cbo_ref, co_ref, no_ref, mo_ref,
                  ext_scr, c_scr, n_scr, m_scr, *, T):
    c = pl.program_id(1)
    nc = pl.num_programs(1)

    @pl.when(c == 0)
    def _():
        ext_scr[0:8, :] = jnp.zeros((8, 2 * D_ML), F32)
        ext_scr[5:8, :] = cb_ref[0]
        c_scr[...] = c0_ref[0]
        n_scr[...] = n0_ref[0]
        m_scr[...] = m0_ref[0]

    ext_scr[8:8 + T, 0:D_ML] = q_ref[...]
    ext_scr[8:8 + T, D_ML:2 * D_ML] = k_ref[...]
    conv = ext_scr[5:5 + T, :] * cw_ref[0:1, :]
    for j in range(1, CONV_W):
        conv = conv + ext_scr[5 + j:5 + j + T, :] * cw_ref[j:j + 1, :]
    tail = ext_scr[8 + T - 3:8 + T, :]
    ext_scr[5:8, :] = tail
    cbo_ref[0] = tail
    qk = conv * jax.nn.sigmoid(conv)

    pre = s_ref[...] + gb_ref[...]
    col = lax.broadcasted_iota(jnp.int32, pre.shape, 1)
    padrow = s_ref[:, LANE - 1:LANE] > 0.5
    gates = jnp.where(col < ML_HEADS, pre, _log_sigmoid(pre))
    gates = jnp.where(padrow, jnp.where(col < ML_HEADS, NEG_BIG, 0.0), gates)
    g_r = gates.T
    ti = lax.broadcasted_iota(jnp.int32, (T, T), 0)
    si = lax.broadcasted_iota(jnp.int32, (T, T), 1)
    upper = (ti <= si).astype(BF16)
    g_fin = jnp.where(lax.broadcasted_iota(jnp.int32, g_r.shape, 0) < ML_HEADS, 0.0, g_r)
    hi, mid, lo = _split3(g_fin)
    cum_r = _dot(hi, upper) + _dot(mid, upper) + _dot(lo, upper)
    rowi = lax.broadcasted_iota(jnp.int32, g_r.shape, 0)
    a_r = jnp.where(rowi < ML_HEADS, g_r, cum_r)
    a_c = a_r.T
    causal = si <= ti

    for h in range(ML_HEADS):
        sl = slice(h * ML_HEAD_DIM, (h + 1) * ML_HEAD_DIM)
        q = qk[:, h * ML_HEAD_DIM:(h + 1) * ML_HEAD_DIM]
        k = qk[:, D_ML + h * ML_HEAD_DIM:D_ML + (h + 1) * ML_HEAD_DIM] * (ML_HEAD_DIM ** -0.5)
        v = v_ref[:, sl]
        ig_r = a_r[h:h + 1, :]
        b_r = a_r[ML_HEADS + h:ML_HEADS + h + 1, :]
        ig_c = a_c[:, h:h + 1]
        b_c = a_c[:, ML_HEADS + h:ML_HEADS + h + 1]
        m_prev = m_scr[h:h + 1, 0:1]
        logd = jnp.where(causal, b_c - b_r + ig_r, -jnp.inf)
        inter = b_c + m_prev
        m_t = jnp.maximum(inter, jnp.max(logd, axis=1, keepdims=True))
        w_intra = jnp.exp(logd - m_t)
        w_inter = jnp.exp(inter - m_t)
        qb = q.astype(BF16)
        kb = k.astype(BF16)
        vb = v.astype(BF16)
        sc = _dot_nt(qb, kb) * w_intra
        cmat = c_scr[h]
        nvec = n_scr[h:h + 1, :]
        num = _dot(sc.astype(BF16), vb) + w_inter * _dot(qb, cmat.astype(BF16))
        qn = jnp.sum(qb.astype(F32) * nvec.astype(BF16).astype(F32), axis=1, keepdims=True)
        den = jnp.sum(sc, axis=1, keepdims=True) + w_inter * qn
        hh = num / jnp.maximum(jnp.abs(den), jnp.exp(-m_t))
        m_new = m_t[T - 1:T, :]
        b_last = b_c[T - 1:T, :]
        w_s = jnp.exp(b_last - b_c + ig_c - m_new)
        decay = jnp.exp(b_last + m_prev - m_new)
        kw = k * w_s
        c_new = decay * cmat + _dot_tn(kw.astype(BF16), vb)
        n_new = decay * nvec + jnp.sum(kw, axis=0, keepdims=True)
        c_scr[h] = c_new
        n_scr[h:h + 1, :] = n_new
        m_scr[h:h + 1, :] = jnp.broadcast_to(m_new, (1, LANE))
        hn = hh * lax.rsqrt(jnp.mean(hh * hh, axis=1, keepdims=True) + EPS) * ng_ref[:, sl]
        y_ref[:, sl] = (hn * jax.nn.sigmoid(o_ref[:, sl])).astype(y_ref.dtype)

    @pl.when(c == nc - 1)
    def _():
        co_ref[0] = c_scr[...]
        no_ref[0] = n_scr[...]
        mo_ref[0] = m_scr[...]


def _mlstm(u, small, conv_buf, c0, n0, m0, conv_w, gate_bias, norm_g, batch, seq, T):
    nc = seq // T
    cb = D_ML // 1024
    m0b = jnp.broadcast_to(m0[:, :, None], (batch, ML_HEADS, LANE))
    m0b = jnp.concatenate([m0b, jnp.zeros((batch, 8 - ML_HEADS, LANE), F32)], axis=1)
    n0p = jnp.concatenate([n0, jnp.zeros((batch, 8 - ML_HEADS, ML_HEAD_DIM), F32)], axis=1)
    row = lambda b, c: (b * nc + c, 0)
    outs = pl.pallas_call(
        functools.partial(_mlstm_kernel, T=T),
        grid=(batch, nc),
        in_specs=[pl.BlockSpec((T, D_ML), lambda b, c: (b * nc + c, COL_QML // D_ML)),
                  pl.BlockSpec((T, D_ML), lambda b, c: (b * nc + c, COL_KML // D_ML)),
                  pl.BlockSpec((T, D_ML), lambda b, c: (b * nc + c, COL_VML // D_ML)),
                  pl.BlockSpec((T, D_ML), lambda b, c: (b * nc + c, COL_OML // D_ML)),
                  pl.BlockSpec((T, LANE), row),
                  pl.BlockSpec((1, CONV_W - 1, 2 * D_ML), lambda b, c: (b, 0, 0)),
                  pl.BlockSpec((1, ML_HEADS, ML_HEAD_DIM, ML_HEAD_DIM), lambda b, c: (b, 0, 0, 0)),
                  pl.BlockSpec((1, 8, ML_HEAD_DIM), lambda b, c: (b, 0, 0)),
                  pl.BlockSpec((1, 8, LANE), lambda b, c: (b, 0, 0)),
                  pl.BlockSpec((CONV_W, 2 * D_ML), lambda b, c: (0, 0)),
                  pl.BlockSpec((1, LANE), lambda b, c: (0, 0)),
                  pl.BlockSpec((1, D_ML), lambda b, c: (0, 0))],
        out_specs=[pl.BlockSpec((T, D_ML), row),
                   pl.BlockSpec((1, CONV_W - 1, 2 * D_ML), lambda b, c: (b, 0, 0)),
                   pl.BlockSpec((1, ML_HEADS, ML_HEAD_DIM, ML_HEAD_DIM), lambda b, c: (b, 0, 0, 0)),
                   pl.BlockSpec((1, 8, ML_HEAD_DIM), lambda b, c: (b, 0, 0)),
                   pl.BlockSpec((1, 8, LANE), lambda b, c: (b, 0, 0))],
        out_shape=[jax.ShapeDtypeStruct((batch * seq, D_ML), BF16),
                   jax.ShapeDtypeStruct((batch, CONV_W - 1, 2 * D_ML), F32),
                   jax.ShapeDtypeStruct((batch, ML_HEADS, ML_HEAD_DIM, ML_HEAD_DIM), F32),
                   jax.ShapeDtypeStruct((batch, 8, ML_HEAD_DIM), F32),
                   jax.ShapeDtypeStruct((batch, 8, LANE), F32)],
        scratch_shapes=[pltpu.VMEM((8 + T, 2 * D_ML), F32),
                        pltpu.VMEM((ML_HEADS, ML_HEAD_DIM, ML_HEAD_DIM), F32),
                        pltpu.VMEM((8, ML_HEAD_DIM), F32),
                        pltpu.VMEM((8, LANE), F32)],
        compiler_params=_cparams(("arbitrary", "arbitrary")),
        name="mlstm",
    )(u, u, u, u, small, conv_buf, c0, n0p, m0b, conv_w, gate_bias, norm_g)
    y, cbo, co, no, mo = outs
    return y, cbo, co, no[:, :ML_HEADS], mo[:, :ML_HEADS, 0]


def _bucket_np(dist):
    n = np.maximum(dist, 0)
    max_exact = N_BUCKETS // 2
    nf = np.maximum(n, 1).astype(np.float64)
    large = max_exact + (np.log(nf / max_exact) / math.log(MAX_DISTANCE / max_exact)
                         * (N_BUCKETS - max_exact)).astype(np.int64)
    return np.where(n < max_exact, n, np.minimum(large, N_BUCKETS - 1)).astype(np.int32)


def _bias_by_distance(rel_bias, n):
    return rel_bias.astype(F32)[_bucket_np(np.arange(n))].T


def _overlap_t(n_cmp, nch, n_slc):
    c0 = np.arange(nch) * CMP_STRIDE
    s0 = np.arange(n_slc) * SLC_BLOCK
    ov = np.minimum(c0[None, :] + CMP_BLOCK, s0[:, None] + SLC_BLOCK) - np.maximum(c0[None, :], s0[:, None])
    ov = np.clip(ov, 0, None).astype(np.float32) / CMP_BLOCK
    ov[:, n_cmp:] = 0.0
    return jnp.asarray(ov, BF16)


def _pq_kernel(x0_ref, x1_ref, x2_ref, x3_ref, pos_ref, w_ref, o_ref, *, rows):
    for pair, x_ref in enumerate((x0_ref, x1_ref, x2_ref, x3_ref)):
        toks = [x_ref[pl.ds(l, rows, stride=CMP_STRIDE), :] for l in range(CMP_STRIDE)]
        for half in range(2):
            ch = 2 * pair + half
            c = ch // NSA_KV_HEADS
            lanes = slice(half * NSA_HEAD_DIM, (half + 1) * NSA_HEAD_DIM)
            chunk = jnp.concatenate([t[:, lanes] for t in toks], axis=1)
            for part in range(2):
                lhs = (chunk + pos_ref[c, part:part + 1, :]).astype(BF16)
                o_ref[0, ch, :, part * CMP_HIDDEN:(part + 1) * CMP_HIDDEN] = _dot(lhs, w_ref[c, part])


def _hid_kernel(pq_ref, w2_ref, o_ref, *, nch):
    p = pq_ref[0, 0, :, 0:CMP_HIDDEN]
    q = pltpu.roll(pq_ref[0, 0, :, CMP_HIDDEN:2 * CMP_HIDDEN], nch - 1, 0)
    hid = jax.nn.gelu(p + q, approximate=True)
    o_ref[0, 0] = _dot(hid.astype(BF16), w2_ref[0])


def _compress(kv2d, col_block, bsz, seq, cmp_pos, cmp_w1, cmp_w2, tt):
    nch = seq // CMP_STRIDE
    kdim = CMP_STRIDE * NSA_HEAD_DIM
    nt = seq // tt
    tn = tt // CMP_STRIDE
    w1 = cmp_w1.reshape(2, 2, kdim, CMP_HIDDEN).astype(BF16)
    pos = cmp_pos.reshape(2, 2, kdim)
    pq = pl.pallas_call(
        functools.partial(_pq_kernel, rows=tn),
        grid=(bsz, nt),
        in_specs=[pl.BlockSpec((tt, LANE), functools.partial(lambda b, i, k: (b * nt + i, col_block * 4 + k), k=k))
                  for k in range(4)] +
                 [pl.BlockSpec((2, 2, kdim), lambda b, i: (0, 0, 0)),
                  pl.BlockSpec((2, 2, kdim, CMP_HIDDEN), lambda b, i: (0, 0, 0, 0))],
        out_specs=pl.BlockSpec((1, 8, tn, 2 * CMP_HIDDEN), lambda b, i: (b, 0, i, 0)),
        out_shape=jax.ShapeDtypeStruct((bsz, 8, nch, 2 * CMP_HIDDEN), F32),
        compiler_params=_cparams(("arbitrary", "arbitrary")),
        name="cmp_pq",
    )(kv2d, kv2d, kv2d, kv2d, pos, w1)
    return _cmp_hid(pq, cmp_w2)


def _cmp_hid(pq, cmp_w2):
    bsz, _, nch, _ = pq.shape
    return pl.pallas_call(
        functools.partial(_hid_kernel, nch=nch),
        grid=(bsz, 8),
        in_specs=[pl.BlockSpec((1, 1, nch, 2 * CMP_HIDDEN), lambda b, c: (b, c, 0, 0)),
                  pl.BlockSpec((1, CMP_HIDDEN, NSA_HEAD_DIM), lambda b, c: (c // NSA_KV_HEADS, 0, 0))],
        out_specs=pl.BlockSpec((1, 1, nch, NSA_HEAD_DIM), lambda b, c: (b, c, 0, 0)),
        out_shape=jax.ShapeDtypeStruct((bsz, 8, nch, NSA_HEAD_DIM), F32),
        compiler_params=_cparams(("arbitrary", "arbitrary")),
        name="cmp_hid",
    )(pq, cmp_w2.astype(BF16))


def _top_n_mask(score_t, blk, n_rows):
    rank = jnp.zeros(score_t.shape, jnp.int32)
    for i in range(n_rows):
        row = score_t[i:i + 1, :]
        beats = (row > score_t) | ((row == score_t) & (blk > i))
        rank = rank + beats.astype(jnp.int32)
    return (rank < min(SLC_TOPN, n_rows)).astype(F32)


def _bias_from_buckets(bucket, table_ref, first_head):
    biases = [jnp.zeros(bucket.shape, F32) for _ in range(NSA_GROUP)]
    for k in range(N_BUCKETS):
        hit = bucket == k
        biases = [jnp.where(hit, table_ref[k, first_head + g], biases[g]) for g in range(NSA_GROUP)]
    return biases


def _cattn_kernel(tab_ref, q_ref, kc_ref, vc_ref, bk_ref, ovt_ref, o_ref, sel_ref, bias_scr,
                  *, tq, nch, n_cmp, n_slc):
    h = pl.program_id(0)
    i = pl.program_id(1)

    @pl.when(pl.program_id(2) == 0)
    def _():
        for g, bias in enumerate(_bias_from_buckets(bk_ref[...], tab_ref, h * NSA_GROUP)):
            bias_scr[g] = bias

    kc = kc_ref[0, 0].astype(BF16)
    vc = vc_ref[0, 0].astype(BF16)
    t = i * tq + lax.broadcasted_iota(jnp.int32, (tq, nch), 0)
    n = lax.broadcasted_iota(jnp.int32, (tq, nch), 1)
    valid = (t - CMP_STRIDE * n - (CMP_BLOCK - 1) >= 0) & (n < n_cmp)
    pc = jnp.zeros((tq, nch), F32)
    for g in range(NSA_GROUP):
        sl = slice(g * NSA_HEAD_DIM, (g + 1) * NSA_HEAD_DIM)
        qg = (q_ref[:, sl] * (NSA_HEAD_DIM ** -0.5)).astype(BF16)
        s = _dot_nt(qg, kc) + bias_scr[g]
        s = jnp.where(valid, s, NEG_BIG)
        e = jnp.exp(s - jnp.max(s, axis=1, keepdims=True))
        p = jnp.where(valid, e / jnp.sum(e, axis=1, keepdims=True), 0.0)
        o_ref[:, sl] = _dot(p.astype(BF16), vc)
        pc = pc + p
    score_t = _dot_nt(ovt_ref[...], pc.astype(BF16))
    blk = lax.broadcasted_iota(jnp.int32, (n_slc, tq), 0)
    cur = (i * tq + lax.broadcasted_iota(jnp.int32, (n_slc, tq), 1)) // SLC_BLOCK
    forced = (blk == 0) | (blk == cur) | (blk == cur - 1)
    score_t = jnp.where(forced, FORCE_SCORE, score_t)
    score_t = jnp.where(blk > cur, -FORCE_SCORE, score_t)
    sel_ref[0, 0] = _top_n_mask(score_t, blk, n_slc)


def _cattn(u, kvc, rel_bias, batch, seq, tq):
    nch = kvc.shape[2]
    n_cmp = nch - 1
    n_slc = seq // SLC_BLOCK
    ni = seq // tq
    ovt = _overlap_t(n_cmp, nch, n_slc)
    qcol = COL_QNSA // (NSA_GROUP * NSA_HEAD_DIM)
    dist = np.arange(seq)[:, None] - (CMP_STRIDE * np.arange(nch)[None, :] + CMP_BLOCK - 1)
    buckets = jnp.asarray(_bucket_np(dist))
    return pl.pallas_call(
        functools.partial(_cattn_kernel, tq=tq, nch=nch, n_cmp=n_cmp, n_slc=n_slc),
        grid=(NSA_KV_HEADS, ni, batch),
        in_specs=[pl.BlockSpec(memory_space=pltpu.SMEM),
                  pl.BlockSpec((tq, NSA_GROUP * NSA_HEAD_DIM), lambda h, i, b: (b * ni + i, qcol + h)),
                  pl.BlockSpec((1, 1, nch, NSA_HEAD_DIM), lambda h, i, b: (b, h, 0, 0)),
                  pl.BlockSpec((1, 1, nch, NSA_HEAD_DIM), lambda h, i, b: (b, NSA_KV_HEADS + h, 0, 0)),
                  pl.BlockSpec((tq, nch), lambda h, i, b: (i, 0)),
                  pl.BlockSpec((n_slc, nch), lambda h, i, b: (0, 0))],
        out_specs=[pl.BlockSpec((tq, NSA_GROUP * NSA_HEAD_DIM), lambda h, i, b: (b * ni + i, h)),
                   pl.BlockSpec((1, 1, n_slc, tq), lambda h, i, b: (b, h, 0, i))],
        out_shape=[jax.ShapeDtypeStruct((batch * seq, D_NSA), F32),
                   jax.ShapeDtypeStruct((batch, NSA_KV_HEADS, n_slc, seq), F32)],
        scratch_shapes=[pltpu.VMEM((NSA_GROUP, tq, nch), F32)],
        compiler_params=_cparams(("arbitrary", "arbitrary", "arbitrary")),
        name="cmp_attn",
    )(rel_bias.astype(F32), u, kvc, kvc, buckets, ovt)


TQ = 128
FLASH_GROUP = 8


def _flash_kernel(tab_ref, q_ref, kv_ref, bk_ref, *rest, selected, n_delta):
    if selected:
        sel_ref, o_ref, band_scr, qs_scr, m_scr, l_scr, acc_scr = rest
    else:
        o_ref, band_scr, qs_scr, m_scr, l_scr, acc_scr = rest
    h = pl.program_id(0)
    i = pl.program_id(2)

    @pl.when((pl.program_id(1) == 0) & (i == 0))
    def _():
        def fill(d, carry):
            for g, bias in enumerate(_bias_from_buckets(bk_ref[d], tab_ref, h * NSA_GROUP)):
                band_scr[d, :, g * TQ:(g + 1) * TQ] = bias
            return carry
        lax.fori_loop(0, n_delta, fill, 0)

    for g in range(NSA_GROUP):
        qg = (q_ref[:, g * NSA_HEAD_DIM:(g + 1) * NSA_HEAD_DIM] * (NSA_HEAD_DIM ** -0.5)).astype(BF16)
        qs_scr[g * TQ:(g + 1) * TQ, :] = jnp.concatenate([qg, jnp.zeros_like(qg)], axis=1)
    is_key_lane = lax.broadcasted_iota(jnp.int32, (TQ, LANE), 1) < NSA_HEAD_DIM
    m_scr[...] = jnp.full(m_scr.shape, 0.5 * NEG_BIG, F32)
    l_scr[...] = jnp.zeros(l_scr.shape, F32)
    acc_scr[...] = jnp.zeros(acc_scr.shape, F32)
    key = lax.broadcasted_iota(jnp.int32, (TQ, TQ), 0)
    qry = lax.broadcasted_iota(jnp.int32, (TQ, TQ), 1)
    n_back = WINDOW // TQ

    def scores(j, kind):
        kvj = kv_ref[pl.ds(pl.multiple_of(j * TQ, TQ), TQ), :].astype(BF16)
        s = _dot_nt(kvj, qs_scr[...]) + band_scr[i - j]
        mask = None
        if kind == "diag":
            mask = qry >= key
        elif kind == "far":
            mask = qry < key
        if selected:
            r = sel_ref[0, 0, pl.ds(2 * j, 2), :]
            picked = jnp.where(key < SLC_BLOCK, r[0:1, :], r[1:2, :]) > 0.5
            mask = picked if mask is None else (mask & picked)
        if mask is not None:
            s = jnp.concatenate([jnp.where(mask, s[:, g * TQ:(g + 1) * TQ], NEG_BIG) for g in range(NSA_GROUP)], axis=1)
        return s, ones_and_values(j)

    def ones_and_values(j):
        kvj = kv_ref[pl.ds(pl.multiple_of(j * TQ, TQ), TQ), :].astype(BF16)
        return jnp.where(is_key_lane, jnp.ones_like(kvj), kvj)

    def update(tiles):
        m_old = m_scr[...]
        m_new = m_old
        for s, _ in tiles:
            m_new = jnp.maximum(m_new, jnp.max(s, axis=0, keepdims=True))
        alpha = jnp.exp(m_old - m_new)
        l_new = alpha * l_scr[...]
        acc = alpha * acc_scr[...]
        for s, ones_v in tiles:
            p = jnp.exp(s - m_new)
            pv = _dot_tn(ones_v, p.astype(BF16))
            l_new = l_new + pv[0:1, :]
            acc = acc + pv[NSA_HEAD_DIM:2 * NSA_HEAD_DIM, :]
        m_scr[...] = m_new
        l_scr[...] = l_new
        acc_scr[...] = acc

    def full_tiles(first, count):
        return [scores(first + k, "full") for k in range(count)]

    if selected:
        def group(t, carry):
            update(full_tiles(FLASH_GROUP * t, FLASH_GROUP))
            return carry

        n_groups = i // FLASH_GROUP
        lax.fori_loop(0, n_groups, group, 0)
        for rem in range(FLASH_GROUP):
            @pl.when(i - n_groups * FLASH_GROUP == rem)
            def _():
                update(full_tiles(n_groups * FLASH_GROUP, rem) + [scores(i, "diag")])
    else:
        @pl.when(i >= n_back)
        def _():
            update([scores(i - n_back, "far")] + full_tiles(i - n_back + 1, n_back - 1) + [scores(i, "diag")])

        for rem in range(n_back):
            @pl.when(i == rem)
            def _():
                update(full_tiles(0, rem) + [scores(i, "diag")])

    for g in range(NSA_GROUP):
        cols = slice(g * TQ, (g + 1) * TQ)
        o_ref[:, g * NSA_HEAD_DIM:(g + 1) * NSA_HEAD_DIM] = (acc_scr[:, cols] / l_scr[:, cols]).T


def _flash(u, kv_col, rel_bias, sel_t, batch, seq):
    ni = seq // TQ
    qcol = COL_QNSA // (NSA_GROUP * NSA_HEAD_DIM)
    selected = sel_t is not None
    n_delta = ni if selected else WINDOW // TQ + 1
    kvblk = kv_col // LANE
    delta = np.arange(n_delta)[:, None, None] * TQ + np.arange(TQ)[None, None, :] - np.arange(TQ)[None, :, None]
    buckets = jnp.asarray(_bucket_np(delta))
    in_specs = [pl.BlockSpec(memory_space=pltpu.SMEM),
                pl.BlockSpec((TQ, NSA_GROUP * NSA_HEAD_DIM), lambda h, b, i: (b * ni + i, qcol + h)),
                pl.BlockSpec((seq, LANE), lambda h, b, i: (b, kvblk + h)),
                pl.BlockSpec((n_delta, TQ, TQ), lambda h, b, i: (0, 0, 0))]
    args = [rel_bias.astype(F32), u, u, buckets]
    if selected:
        n_slc = sel_t.shape[2]
        in_specs.append(pl.BlockSpec((1, 1, n_slc, TQ), lambda h, b, i: (b, h, 0, i)))
        args.append(sel_t)
    scratch = [pltpu.VMEM((n_delta, TQ, NSA_GROUP * TQ), F32),
               pltpu.VMEM((NSA_GROUP * TQ, LANE), BF16),
               pltpu.VMEM((1, NSA_GROUP * TQ), F32),
               pltpu.VMEM((1, NSA_GROUP * TQ), F32),
               pltpu.VMEM((NSA_HEAD_DIM, NSA_GROUP * TQ), F32)]
    return pl.pallas_call(
        functools.partial(_flash_kernel, selected=selected, n_delta=n_delta),
        grid=(NSA_KV_HEADS, batch, ni),
        in_specs=in_specs,
        out_specs=pl.BlockSpec((TQ, NSA_GROUP * NSA_HEAD_DIM), lambda h, b, i: (b * ni + i, h)),
        out_shape=jax.ShapeDtypeStruct((batch * seq, D_NSA), F32),
        scratch_shapes=scratch,
        compiler_params=_cparams(("arbitrary", "arbitrary", "arbitrary")),
        name="slc_attn" if selected else "win_attn",
    )(*args)


def _gate_expand():
    e = np.zeros((3, 2 * LANE, D_NSA), np.float32)
    for br in range(3):
        for hd in range(NSA_HEADS):
            for part in range(2):
                e[br, part * LANE + 2 * ML_HEADS + br * NSA_HEADS + hd,
                  hd * NSA_HEAD_DIM:(hd + 1) * NSA_HEAD_DIM] = 1.0
    return jnp.asarray(e, BF16)


def _outproj_kernel(yml_ref, oc_ref, os_ref, ow_ref, s_ref, e_ref, ng_ref, w_ref, x_ref, o_ref, cat_scr):
    @pl.when(pl.program_id(1) == 0)
    def _():
        sig = jax.nn.sigmoid(s_ref[...])
        hi_lo = jnp.concatenate(_split2(sig), axis=1)
        o = jnp.zeros(oc_ref.shape, F32)
        for br, ref in enumerate((oc_ref, os_ref, ow_ref)):
            o = o + _dot(hi_lo, e_ref[br]) * ref[...]
        o = o * lax.rsqrt(jnp.mean(o * o, axis=-1, keepdims=True) + EPS) * ng_ref[...]
        cat_scr[:, 0:D_ML] = yml_ref[...]
        cat_scr[:, D_ML:D_ML + D_NSA] = o.astype(BF16)

    o_ref[...] = x_ref[...] + _dot(cat_scr[...], w_ref[...])


def _outproj(y_ml, o_c, o_s, o_w, small, nsa_g, w_out_b, x2d, tm, tn):
    n, d = x2d.shape
    rows = lambda i, j: (i, 0)
    return pl.pallas_call(
        _outproj_kernel,
        grid=(n // tm, d // tn),
        in_specs=[pl.BlockSpec((tm, D_ML), rows), pl.BlockSpec((tm, D_NSA), rows),
                  pl.BlockSpec((tm, D_NSA), rows), pl.BlockSpec((tm, D_NSA), rows),
                  pl.BlockSpec((tm, LANE), rows),
                  pl.BlockSpec((3, 2 * LANE, D_NSA), lambda i, j: (0, 0, 0)),
                  pl.BlockSpec((1, D_NSA), lambda i, j: (0, 0)),
                  pl.BlockSpec((D_ML + D_NSA, tn), lambda i, j: (0, j)),
                  pl.BlockSpec((tm, tn), lambda i, j: (i, j))],
        out_specs=pl.BlockSpec((tm, tn), lambda i, j: (i, j)),
        out_shape=jax.ShapeDtypeStruct((n, d), F32),
        scratch_shapes=[pltpu.VMEM((tm, D_ML + D_NSA), BF16)],
        compiler_params=_cparams(("arbitrary", "arbitrary")),
        name="outproj",
    )(y_ml, o_c, o_s, o_w, small, _gate_expand(), nsa_g, w_out_b, x2d)


ROUTE_COL = N_GROUPS
BIG_COL = 1 << 20
SPLIT = D_MODEL // LANE
DMA_UNROLL = 8


def _store_split(ref, val, n):
    for k in range(SPLIT):
        ref[pl.ds(k, n, stride=SPLIT), :] = val[:, k * LANE:(k + 1) * LANE]


def _load_split(ref, n):
    return jnp.concatenate([ref[pl.ds(k, n, stride=SPLIT), :] for k in range(SPLIT)], axis=1)


def _route_kernel(y_ref, g_ref, w_ref, b_ref, h_ref, info_ref, cnt_ref, carry_scr, *, tm):
    @pl.when(pl.program_id(0) == 0)
    def _():
        carry_scr[...] = jnp.zeros(carry_scr.shape, F32)

    x = y_ref[...]
    h = x * lax.rsqrt(jnp.mean(x * x, axis=-1, keepdims=True) + EPS) * g_ref[...]
    _store_split(h_ref, h, tm)
    logit = _dot(h.astype(BF16), w_ref[...]) + b_ref[...]
    col = lax.broadcasted_iota(jnp.int32, logit.shape, 1)
    is_grp = col < N_GROUPS
    gmax = jnp.max(jnp.where(is_grp, logit, -jnp.inf), axis=1, keepdims=True)
    gtop = jnp.min(jnp.where(is_grp & (logit == gmax), col, BIG_COL), axis=1, keepdims=True)
    gsum = jnp.sum(jnp.where(is_grp, jnp.exp(logit - gmax), 0.0), axis=1, keepdims=True)
    first = ROUTE_COL + gtop * EXPERTS_PER_GROUP
    in_grp = (col >= first) & (col < first + EXPERTS_PER_GROUP)
    v1 = jnp.max(jnp.where(in_grp, logit, -jnp.inf), axis=1, keepdims=True)
    i1 = jnp.min(jnp.where(in_grp & (logit == v1), col, BIG_COL), axis=1, keepdims=True)
    rest = in_grp & (col != i1)
    v2 = jnp.max(jnp.where(rest, logit, -jnp.inf), axis=1, keepdims=True)
    i2 = jnp.min(jnp.where(rest & (logit == v2), col, BIG_COL), axis=1, keepdims=True)
    e = jnp.exp(v2 - v1)
    w1 = 1.0 / ((1.0 + e) * gsum)
    w2 = e / ((1.0 + e) * gsum)
    pick1 = col == i1
    pick2 = col == i2
    both = (pick1 | pick2).astype(F32)
    ri = lax.broadcasted_iota(jnp.int32, (tm, tm), 0)
    ci = lax.broadcasted_iota(jnp.int32, (tm, tm), 1)
    before = (ci < ri).astype(BF16)
    cum = _dot(before, both.astype(BF16)) + carry_scr[...]
    r1 = jnp.sum(jnp.where(pick1, cum, 0.0), axis=1, keepdims=True)
    r2 = jnp.sum(jnp.where(pick2, cum, 0.0), axis=1, keepdims=True)
    carry_scr[...] = carry_scr[...] + jnp.sum(both, axis=0, keepdims=True)
    cnt_ref[...] = jnp.broadcast_to(carry_scr[...], cnt_ref.shape)
    info = jnp.where(col == 0, (i1 - ROUTE_COL).astype(F32), 0.0)
    info = jnp.where(col == 1, (i2 - ROUTE_COL).astype(F32), info)
    info = jnp.where(col == 2, w1, info)
    info = jnp.where(col == 3, w2, info)
    info = jnp.where(col == 4, r1, info)
    info_ref[...] = jnp.where(col == 5, r2, info)


def _route(y2d, g, w_rg, b_rg, w_re, b_re, tm):
    n, d = y2d.shape
    wr = jnp.concatenate([w_rg, w_re, jnp.zeros((d, LANE - N_GROUPS - N_EXPERTS), F32)], axis=1).astype(BF16)
    bias = jnp.concatenate([b_rg, b_re, jnp.zeros((LANE - N_GROUPS - N_EXPERTS,), F32)])[None, :]
    rows = lambda i: (i, 0)
    fixed = lambda i: (0, 0)
    return pl.pallas_call(
        functools.partial(_route_kernel, tm=tm),
        grid=(n // tm,),
        in_specs=[pl.BlockSpec((tm, d), rows), pl.BlockSpec((1, d), fixed),
                  pl.BlockSpec((d, LANE), fixed), pl.BlockSpec((1, LANE), fixed)],
        out_specs=[pl.BlockSpec((tm * SPLIT, LANE), rows), pl.BlockSpec((tm, LANE), rows),
                   pl.BlockSpec((8, LANE), fixed)],
        out_shape=[jax.ShapeDtypeStruct((n * SPLIT, LANE), F32), jax.ShapeDtypeStruct((n, LANE), F32),
                   jax.ShapeDtypeStruct((8, LANE), F32)],
        scratch_shapes=[pltpu.VMEM((1, LANE), F32)],
        compiler_params=_cparams(("arbitrary",)),
        name="moe_route",
    )(y2d, g, wr, bias)


W_CHUNKS = 4


def _expert_kernel(te_ref, nu_ref, src_ref, nx_ref, ws_ref, h_hbm, wg_hbm, wu_hbm, wd_hbm, o_ref,
                   xbuf, wg_f, wu_f, wd_f, wg_b, wu_b, wd_b, sem, wsem, *, tme):
    i = pl.program_id(0)
    slot = i % 2

    def weight_copies(expert, wslot):
        copies = []
        for hbm, buf in ((wg_hbm, wg_f), (wu_hbm, wu_f), (wd_hbm, wd_f)):
            step = hbm.shape[1] // W_CHUNKS
            for c in range(W_CHUNKS):
                copies.append(pltpu.make_async_copy(hbm.at[expert, pl.ds(c * step, step)],
                                                    buf.at[wslot, pl.ds(c * step, step)], wsem.at[wslot]))
        return copies

    def row_copy(tile, r, buf_slot):
        src_row = src_ref[tile * tme + r]
        return pltpu.make_async_copy(h_hbm.at[pl.ds(pl.multiple_of(src_row * SPLIT, SPLIT), SPLIT)],
                                     xbuf.at[buf_slot, pl.ds(pl.multiple_of(r * SPLIT, SPLIT), SPLIT)],
                                     sem.at[buf_slot])

    def fetch(tile, buf_slot):
        def body(r, carry):
            row_copy(tile, r, buf_slot).start()
            return carry
        lax.fori_loop(0, tme, body, 0, unroll=DMA_UNROLL)

    @pl.when(i == 0)
    def _():
        fetch(0, 0)
        for cp in weight_copies(te_ref[0], ws_ref[0]):
            cp.start()

    @pl.when(i + 1 < nu_ref[0])
    def _():
        fetch(i + 1, 1 - slot)

    @pl.when(i < nu_ref[0])
    def _():
        prev = te_ref[jnp.maximum(i - 1, 0)]

        @pl.when((i == 0) | (te_ref[i] != prev))
        def _():
            w = ws_ref[i]
            for cp in weight_copies(te_ref[i], w):
                cp.wait()
            wg_b[...] = wg_f[w].astype(BF16)
            wu_b[...] = wu_f[w].astype(BF16)
            wd_b[...] = wd_f[w].astype(BF16)

            @pl.when(nx_ref[i] >= 0)
            def _():
                for cp in weight_copies(nx_ref[i], 1 - w):
                    cp.start()

        def wait_body(r, carry):
            row_copy(i, r, slot).wait()
            return carry
        lax.fori_loop(0, tme, wait_body, 0, unroll=DMA_UNROLL)

        x = _load_split(xbuf.at[slot], tme).astype(BF16)
        a = _dot(x, wg_b[...])
        u = _dot(x, wu_b[...])
        hid = a * jax.nn.sigmoid(a) * u
        _store_split(o_ref, _dot(hid.astype(BF16), wd_b[...]), tme)

    @pl.when(i >= nu_ref[0])
    def _():
        o_ref[...] = jnp.zeros(o_ref.shape, F32)


def _experts(h, src, tile_expert, n_used, next_expert, weight_slot, w_gate, w_up, w_down, tme):
    p = src.shape[0]
    d = D_MODEL
    hbm = pl.BlockSpec(memory_space=pl.ANY)
    return pl.pallas_call(
        functools.partial(_expert_kernel, tme=tme),
        grid_spec=pltpu.PrefetchScalarGridSpec(
            num_scalar_prefetch=5, grid=(p // tme,),
            in_specs=[hbm, hbm, hbm, hbm],
            out_specs=pl.BlockSpec((tme * SPLIT, LANE), lambda i, *_: (i, 0)),
            scratch_shapes=[pltpu.VMEM((2, tme * SPLIT, LANE), F32),
                            pltpu.VMEM((2, d, D_EXPERT), F32), pltpu.VMEM((2, d, D_EXPERT), F32),
                            pltpu.VMEM((2, D_EXPERT, d), F32),
                            pltpu.VMEM((d, D_EXPERT), BF16), pltpu.VMEM((d, D_EXPERT), BF16),
                            pltpu.VMEM((D_EXPERT, d), BF16),
                            pltpu.SemaphoreType.DMA((2,)), pltpu.SemaphoreType.DMA((2,))]),
        out_shape=jax.ShapeDtypeStruct((p * SPLIT, LANE), F32),
        compiler_params=_cparams(("arbitrary",)),
        name="moe_experts",
    )(tile_expert, n_used, src, next_expert, weight_slot, h, w_gate, w_up, w_down)


def _combine_kernel(d1_ref, d2_ref, y_ref, info_ref, ys_hbm, g_ref, o_ref, a_buf, b_buf, sem, *, tm):
    i = pl.program_id(0)
    slot = i % 2

    def copy(r, row, buf, buf_slot):
        return pltpu.make_async_copy(ys_hbm.at[pl.ds(pl.multiple_of(row * SPLIT, SPLIT), SPLIT)],
                                     buf.at[buf_slot, pl.ds(pl.multiple_of(r * SPLIT, SPLIT), SPLIT)],
                                     sem.at[buf_slot])

    def fetch(tile, buf_slot):
        def issue(r, carry):
            copy(r, d1_ref[tile * tm + r], a_buf, buf_slot).start()
            copy(r, d2_ref[tile * tm + r], b_buf, buf_slot).start()
            return carry
        lax.fori_loop(0, tm, issue, 0, unroll=DMA_UNROLL)

    @pl.when(i == 0)
    def _():
        fetch(0, 0)

    @pl.when(i + 1 < pl.num_programs(0))
    def _():
        fetch(i + 1, 1 - slot)

    def drain(r, carry):
        copy(r, 0, a_buf, slot).wait()
        copy(r, 0, b_buf, slot).wait()
        return carry

    lax.fori_loop(0, tm, drain, 0, unroll=DMA_UNROLL)
    w1 = info_ref[:, 2:3]
    w2 = info_ref[:, 3:4]
    y = y_ref[...] + w1 * _load_split(a_buf.at[slot], tm) + w2 * _load_split(b_buf.at[slot], tm)
    o_ref[...] = y * lax.rsqrt(jnp.mean(y * y, axis=-1, keepdims=True) + EPS) * g_ref[...]


def _combine(y2d, info, ys, d1, d2, g, tm):
    n, d = y2d.shape
    return pl.pallas_call(
        functools.partial(_combine_kernel, tm=tm),
        grid_spec=pltpu.PrefetchScalarGridSpec(
            num_scalar_prefetch=2, grid=(n // tm,),
            in_specs=[pl.BlockSpec((tm, d), lambda i, a, b: (i, 0)),
                      pl.BlockSpec((tm, LANE), lambda i, a, b: (i, 0)),
                      pl.BlockSpec(memory_space=pl.ANY),
                      pl.BlockSpec((1, d), lambda i, a, b: (0, 0))],
            out_specs=pl.BlockSpec((tm, d), lambda i, a, b: (i, 0)),
            scratch_shapes=[pltpu.VMEM((2, tm * SPLIT, LANE), F32), pltpu.VMEM((2, tm * SPLIT, LANE), F32),
                            pltpu.SemaphoreType.DMA((2,))]),
        out_shape=jax.ShapeDtypeStruct((n, d), F32),
        compiler_params=_cparams(("arbitrary",)),
        name="moe_combine",
    )(d1, d2, y2d, info, ys, g)


def _moe_final(y2d, ffn_g, w_rg, b_rg, w_re, b_re, w_gate, w_up, w_down, final_g, tm, tme):
    n, d = y2d.shape
    h, info, cnt = _route(y2d, ffn_g, w_rg, b_rg, w_re, b_re, tm)
    e1 = info[:, 0].astype(jnp.int32)
    e2 = info[:, 1].astype(jnp.int32)
    counts = cnt[0, ROUTE_COL:ROUTE_COL + N_EXPERTS].astype(jnp.int32)
    padded = (counts + tme - 1) // tme * tme
    ends = jnp.cumsum(padded)
    offs = ends - padded
    d1 = offs[e1] + info[:, 4].astype(jnp.int32)
    d2 = offs[e2] + info[:, 5].astype(jnp.int32)
    p = 2 * n + N_EXPERTS * tme
    tok = jnp.arange(n, dtype=jnp.int32)
    src = jnp.zeros((p,), jnp.int32).at[jnp.concatenate([d1, d2])].set(jnp.concatenate([tok, tok]))
    n_tiles = p // tme
    n_used = (ends[-1] // tme).astype(jnp.int32).reshape(1)
    tile_start = jnp.arange(n_tiles, dtype=jnp.int32) * tme
    tile_expert = jnp.sum((ends[None, :] <= tile_start[:, None]).astype(jnp.int32), axis=1)
    last = jnp.sum((ends <= ends[-1] - 1).astype(jnp.int32))
    tile_expert = jnp.minimum(tile_expert, last).astype(jnp.int32)
    run = jnp.cumsum(jnp.concatenate([jnp.zeros((1,), jnp.int32),
                                      (tile_expert[1:] != tile_expert[:-1]).astype(jnp.int32)]))
    weight_slot = (run % 2).astype(jnp.int32)
    after = ends[tile_expert] // tme
    next_expert = jnp.where(after < n_used[0], tile_expert[jnp.minimum(after, n_tiles - 1)], -1).astype(jnp.int32)
    ys = _experts(h, src, tile_expert, n_used, next_expert, weight_slot, w_gate, w_up, w_down, tme)
    return _combine(y2d, info, ys, d1, d2, final_g, tm)


GROUP_ROWS = 8
SLC_LANES = 384
SEL_KEYS = 1024
WIN_KEYS = 640


def _sample_cattn_kernel(q_ref, kc_ref, vc_ref, b_ref, ov_ref, o_ref, sel_ref, *, n_cmp, n_slc):
    q = (q_ref[0, 0] * (NSA_HEAD_DIM ** -0.5)).astype(BF16)
    s = _dot_nt(q, kc_ref[0, 0].astype(BF16)) + b_ref[0]
    valid = lax.broadcasted_iota(jnp.int32, s.shape, 1) < n_cmp
    s = jnp.where(valid, s, NEG_BIG)
    e = jnp.exp(s - jnp.max(s, axis=1, keepdims=True))
    p = jnp.where(valid, e / jnp.sum(e, axis=1, keepdims=True), 0.0)
    o_ref[0, 0] = _dot(p.astype(BF16), vc_ref[0, 0].astype(BF16))
    pg = jnp.where(lax.broadcasted_iota(jnp.int32, p.shape, 0) < NSA_GROUP, p, 0.0)
    pc = jnp.broadcast_to(jnp.sum(pg, axis=0, keepdims=True), pg.shape)
    score = _dot(pc.astype(BF16), ov_ref[...])[0:1, :]
    blk = lax.broadcasted_iota(jnp.int32, score.shape, 1)
    cur = n_slc - 1
    forced = (blk == 0) | (blk == cur) | (blk == cur - 1)
    score = jnp.where(forced, FORCE_SCORE, score)
    score = jnp.where(blk > cur, -jnp.inf, score)
    ri = lax.broadcasted_iota(jnp.int32, (SLC_LANES, SLC_LANES), 0)
    ci = lax.broadcasted_iota(jnp.int32, (SLC_LANES, SLC_LANES), 1)
    sb = jnp.broadcast_to(score, (SLC_LANES, SLC_LANES))
    col = jnp.sum(jnp.where(ri == ci, sb, 0.0), axis=1, keepdims=True)
    beats = (col > sb) | ((col == sb) & (ri < ci))
    rank = jnp.sum(beats.astype(jnp.int32), axis=0, keepdims=True)
    sel_ref[0, 0] = jnp.broadcast_to((rank < SLC_TOPN).astype(F32), (GROUP_ROWS, SLC_LANES))


def _sample_cattn(q8, kvc, bias_cs, n_cmp, n_slc):
    bsz, _, nch, _ = kvc.shape
    c0 = np.arange(nch) * CMP_STRIDE
    s0 = np.arange(SLC_LANES) * SLC_BLOCK
    ov = np.minimum(c0[:, None] + CMP_BLOCK, s0[None, :] + SLC_BLOCK) - np.maximum(c0[:, None], s0[None, :])
    ov = np.clip(ov, 0, None).astype(np.float32) / CMP_BLOCK
    ov[n_cmp:, :] = 0.0
    ov[:, n_slc:] = 0.0
    return pl.pallas_call(
        functools.partial(_sample_cattn_kernel, n_cmp=n_cmp, n_slc=n_slc),
        grid=(bsz, NSA_KV_HEADS),
        in_specs=[pl.BlockSpec((1, 1, GROUP_ROWS, NSA_HEAD_DIM), lambda b, h: (b, h, 0, 0)),
                  pl.BlockSpec((1, 1, nch, NSA_HEAD_DIM), lambda b, h: (b, h, 0, 0)),
                  pl.BlockSpec((1, 1, nch, NSA_HEAD_DIM), lambda b, h: (b, NSA_KV_HEADS + h, 0, 0)),
                  pl.BlockSpec((1, GROUP_ROWS, nch), lambda b, h: (h, 0, 0)),
                  pl.BlockSpec((nch, SLC_LANES), lambda b, h: (0, 0))],
        out_specs=[pl.BlockSpec((1, 1, GROUP_ROWS, NSA_HEAD_DIM), lambda b, h: (b, h, 0, 0)),
                   pl.BlockSpec((1, 1, GROUP_ROWS, SLC_LANES), lambda b, h: (b, h, 0, 0))],
        out_shape=[jax.ShapeDtypeStruct((bsz, NSA_KV_HEADS, GROUP_ROWS, NSA_HEAD_DIM), F32),
                   jax.ShapeDtypeStruct((bsz, NSA_KV_HEADS, GROUP_ROWS, SLC_LANES), F32)],
        compiler_params=_cparams(("arbitrary", "arbitrary")),
        name="sample_cmp_attn",
    )(q8, kvc, kvc, bias_cs, jnp.asarray(ov, BF16))


def _sample_attn_kernel(pg_ref, hf_ref, q_ref, cache_hbm, snew_ref, bs_ref, win_ref, wnew_ref, bw_ref,
                        os_ref, ow_ref, kt_buf, vt_buf, sem, *, n_gather, wlen):
    b = pl.program_id(0)
    rows_per_page = 2 * NSA_KV_HEADS * NSA_HEAD_DIM

    def page_copy(h, s, c, buf):
        page = pg_ref[(b * NSA_KV_HEADS + h) * n_gather + s]
        start = pl.multiple_of(page * rows_per_page + (c * NSA_KV_HEADS + h) * NSA_HEAD_DIM, NSA_HEAD_DIM)
        return pltpu.make_async_copy(cache_hbm.at[pl.ds(start, NSA_HEAD_DIM)],
                                     buf.at[h, :, pl.ds(s * PAGE_SIZE, PAGE_SIZE)], sem)

    for h in range(NSA_KV_HEADS):
        for s in range(n_gather):
            page_copy(h, s, 0, kt_buf).start()
            page_copy(h, s, 1, vt_buf).start()

    def attend(q, kts, vts, bias, valid):
        s = jnp.concatenate([_dot(q, kt.astype(BF16)) for kt in kts], axis=1) + bias
        s = jnp.where(valid, s, NEG_BIG)
        e = jnp.exp(s - jnp.max(s, axis=1, keepdims=True))
        p = jnp.where(valid, e / jnp.sum(e, axis=1, keepdims=True), 0.0).astype(BF16)
        out = None
        lo = 0
        for vt in vts:
            n = vt.shape[1]
            part = _dot_nt(p[:, lo:lo + n], vt.astype(BF16))
            out = part if out is None else out + part
            lo += n
        return out

    def head_rows(ref, c, h):
        r0 = (c * NSA_KV_HEADS + h) * NSA_HEAD_DIM
        return ref[0, r0:r0 + NSA_HEAD_DIM, :]

    widx = lax.broadcasted_iota(jnp.int32, (GROUP_ROWS, wlen + PAGE_SIZE), 1)
    wvalid = (widx >= wlen + 1 - WINDOW) & (widx <= wlen)
    for h in range(NSA_KV_HEADS):
        q = (q_ref[0, h] * (NSA_HEAD_DIM ** -0.5)).astype(BF16)
        ow_ref[0, h] = attend(q, [head_rows(win_ref, 0, h), head_rows(wnew_ref, 0, h)],
                              [head_rows(win_ref, 1, h), head_rows(wnew_ref, 1, h)], bw_ref[h], wvalid)

    for h in range(NSA_KV_HEADS):
        for s in range(n_gather):
            page_copy(h, s, 0, kt_buf).wait()
            page_copy(h, s, 1, vt_buf).wait()

    keys = n_gather * PAGE_SIZE
    lane = lax.broadcasted_iota(jnp.int32, (GROUP_ROWS, keys + PAGE_SIZE), 1)
    slot = lane >> 7
    lane_half = (lane >> 6) & 1
    for h in range(NSA_KV_HEADS):
        want = jnp.full(lane.shape, -1, jnp.int32)
        for s in range(n_gather):
            want = jnp.where(slot == s, hf_ref[(b * NSA_KV_HEADS + h) * n_gather + s], want)
        svalid = (lane_half == want) | (lane == keys)
        q = (q_ref[0, h] * (NSA_HEAD_DIM ** -0.5)).astype(BF16)
        os_ref[0, h] = attend(q, [kt_buf[h], head_rows(snew_ref, 0, h)],
                              [vt_buf[h], head_rows(snew_ref, 1, h)], bs_ref[0, h], svalid)


def _sample_attn(pages, halves, q8, cache_t, snew_t, bias_sel, win_t, wnew_t, bias_w, n_gather):
    bsz = q8.shape[0]
    wlen = win_t.shape[2]
    keys = n_gather * PAGE_SIZE
    rows = 2 * NSA_KV_HEADS * NSA_HEAD_DIM
    qspec = pl.BlockSpec((1, NSA_KV_HEADS, GROUP_ROWS, NSA_HEAD_DIM), lambda b, pg, hf: (b, 0, 0, 0))
    newspec = pl.BlockSpec((1, rows, PAGE_SIZE), lambda b, pg, hf: (b, 0, 0))
    return pl.pallas_call(
        functools.partial(_sample_attn_kernel, n_gather=n_gather, wlen=wlen),
        grid_spec=pltpu.PrefetchScalarGridSpec(
            num_scalar_prefetch=2, grid=(bsz,),
            in_specs=[qspec,
                      pl.BlockSpec(memory_space=pl.ANY),
                      newspec,
                      pl.BlockSpec((1, NSA_KV_HEADS, GROUP_ROWS, keys + PAGE_SIZE), lambda b, pg, hf: (b, 0, 0, 0)),
                      pl.BlockSpec((1, rows, wlen), lambda b, pg, hf: (b, 0, 0)),
                      newspec,
                      pl.BlockSpec((NSA_KV_HEADS, GROUP_ROWS, wlen + PAGE_SIZE), lambda b, pg, hf: (0, 0, 0))],
            out_specs=[qspec, qspec],
            scratch_shapes=[pltpu.VMEM((NSA_KV_HEADS, NSA_HEAD_DIM, keys), F32),
                            pltpu.VMEM((NSA_KV_HEADS, NSA_HEAD_DIM, keys), F32),
                            pltpu.SemaphoreType.DMA(())]),
        out_shape=[jax.ShapeDtypeStruct(q8.shape, F32), jax.ShapeDtypeStruct(q8.shape, F32)],
        compiler_params=_cparams(("arbitrary",)),
        name="sample_slc_win_attn",
    )(pages, halves, q8, cache_t, snew_t, bias_sel, win_t, wnew_t, bias_w)


PAGES_PER_STEP = 16


def _paged_pq_kernel(pt_ref, cache_hbm, pos_ref, w_ref, o_ref, pbuf, tok_scr, sem):
    b = pl.program_id(0)
    g = pl.program_id(1)
    ng = pl.num_programs(1)
    lin = b * ng + g
    slot = lin % 2
    rows_per_page = 2 * NSA_KV_HEADS * NSA_HEAD_DIM
    chunks = PAGES_PER_STEP * (PAGE_SIZE // CMP_STRIDE)

    def page_copy(step, p, buf_slot):
        sb = step // ng
        sg = step - sb * ng
        page = pt_ref[sb, sg * PAGES_PER_STEP + p]
        return pltpu.make_async_copy(cache_hbm.at[pl.ds(pl.multiple_of(page * rows_per_page, rows_per_page),
                                                        rows_per_page)],
                                     pbuf.at[buf_slot, p], sem.at[buf_slot])

    def fetch(step, buf_slot):
        for p in range(PAGES_PER_STEP):
            page_copy(step, p, buf_slot).start()

    @pl.when(lin == 0)
    def _():
        fetch(0, 0)

    @pl.when(lin + 1 < pl.num_programs(0) * ng)
    def _():
        fetch(lin + 1, 1 - slot)

    for p in range(PAGES_PER_STEP):
        page_copy(lin, p, slot).wait()

    for c in range(2):
        head_chunks = []
        for pair in (2 * c, 2 * c + 1):
            for p in range(PAGES_PER_STEP):
                tok_scr[p * PAGE_SIZE:(p + 1) * PAGE_SIZE, :] = pbuf[slot, p, pair * LANE:(pair + 1) * LANE, :].T
            toks = [tok_scr[pl.ds(l, chunks, stride=CMP_STRIDE), :] for l in range(CMP_STRIDE)]
            for half in range(2):
                lanes = slice(half * NSA_HEAD_DIM, (half + 1) * NSA_HEAD_DIM)
                head_chunks.append(jnp.concatenate([t[:, lanes] for t in toks], axis=1))
        stacked = jnp.concatenate(head_chunks, axis=0)
        for part in range(2):
            lhs = (stacked + pos_ref[c, part:part + 1, :]).astype(BF16)
            out = _dot(lhs, w_ref[c, part])
            for hh in range(NSA_KV_HEADS):
                o_ref[0, c * NSA_KV_HEADS + hh, :, part * CMP_HIDDEN:(part + 1) * CMP_HIDDEN] = (
                    out[hh * chunks:(hh + 1) * chunks, :])


def _paged_pq(cache_t, page_table, cmp_pos, cmp_w1):
    db, n_pages = page_table.shape
    kdim = CMP_STRIDE * NSA_HEAD_DIM
    nch = n_pages * (PAGE_SIZE // CMP_STRIDE)
    tn = PAGES_PER_STEP * (PAGE_SIZE // CMP_STRIDE)
    rows_per_page = 2 * NSA_KV_HEADS * NSA_HEAD_DIM
    return pl.pallas_call(
        _paged_pq_kernel,
        grid_spec=pltpu.PrefetchScalarGridSpec(
            num_scalar_prefetch=1, grid=(db, n_pages // PAGES_PER_STEP),
            in_specs=[pl.BlockSpec(memory_space=pl.ANY),
                      pl.BlockSpec((2, 2, kdim), lambda b, g, pt: (0, 0, 0)),
                      pl.BlockSpec((2, 2, kdim, CMP_HIDDEN), lambda b, g, pt: (0, 0, 0, 0))],
            out_specs=pl.BlockSpec((1, 8, tn, 2 * CMP_HIDDEN), lambda b, g, pt: (b, 0, g, 0)),
            scratch_shapes=[pltpu.VMEM((2, PAGES_PER_STEP, rows_per_page, PAGE_SIZE), F32),
                            pltpu.VMEM((PAGES_PER_STEP * PAGE_SIZE, LANE), F32),
                            pltpu.SemaphoreType.DMA((2,))]),
        out_shape=jax.ShapeDtypeStruct((db, 8, nch, 2 * CMP_HIDDEN), F32),
        compiler_params=_cparams(("arbitrary", "arbitrary")),
        name="cmp_pq_paged",
    )(page_table, cache_t, cmp_pos.reshape(2, 2, kdim), cmp_w1.reshape(2, 2, kdim, CMP_HIDDEN).astype(BF16))


def _pad_rows(a, n):
    return jnp.concatenate([a, jnp.zeros((n - a.shape[0],) + a.shape[1:], a.dtype)], axis=0)


def _sample_mixer(x_sample, cache_cmp, cache_slc, cache_win, st_c, st_n, st_m, st_conv, page_table,
                  norm_g, wb, gate_bias, conv_w, ml_norm_g, cmp_pos, cmp_w1, cmp_w2, nsa_norm_g, w_out_b, rel_bias):
    db = x_sample.shape[0]
    n_pages = page_table.shape[1]
    past = n_pages * PAGE_SIZE
    tok = 16
    x16 = _pad_rows(x_sample.reshape(db, D_MODEL), tok)
    u = _proj(x16, norm_g, wb, tok, 512)
    small = u[:, COL_SMALL:COL_SMALL + LANE]

    T = 128
    useq = jnp.zeros((db, T, 4 * D_ML), F32)
    useq = useq.at[:, T - CONV_W:T - 1, 0:2 * D_ML].set(st_conv)
    useq = useq.at[:, T - 1, :].set(u[:db, 0:4 * D_ML])
    sseq = jnp.zeros((db, T, LANE), F32).at[:, :T - 1, LANE - 1].set(1.0)
    sseq = sseq.at[:, T - 1, :].set(small[:db])
    y_seq, conv_n, c_n, n_n, m_n = _mlstm(
        useq.reshape(db * T, 4 * D_ML), sseq.reshape(db * T, LANE), jnp.zeros((db, CONV_W - 1, 2 * D_ML), F32),
        st_c, st_n, st_m, conv_w, gate_bias, ml_norm_g, db, T, T)
    y_ml = y_seq.reshape(db, T, D_ML)[:, T - 1]

    n_pool = cache_cmp.shape[0]
    cmp_t = cache_cmp.transpose(0, 2, 3, 4, 1).reshape(n_pool * 2 * D_KV, PAGE_SIZE)
    kvc = _cmp_hid(_paged_pq(cmp_t, page_table, cmp_pos, cmp_w1), cmp_w2)
    nch = past // CMP_STRIDE
    n_cmp = (past + 1) // CMP_STRIDE - CMP_BLOCK // CMP_STRIDE + 1
    n_slc = -(-(past + 1) // SLC_BLOCK)
    q = u[:db, COL_QNSA:COL_QNSA + D_NSA].reshape(db, NSA_KV_HEADS, NSA_GROUP, NSA_HEAD_DIM)
    q8 = jnp.concatenate([q, jnp.zeros_like(q)], axis=2)
    bd = _bias_by_distance(rel_bias, past + 1).reshape(NSA_KV_HEADS, NSA_GROUP, past + 1)
    pad_g = lambda a: jnp.concatenate([a, jnp.zeros_like(a)], axis=1)
    dist_c = np.clip(past - (np.arange(nch) * CMP_STRIDE + CMP_BLOCK - 1), 0, None)
    o_c8, sel = _sample_cattn(q8, kvc, pad_g(bd[:, :, dist_c]), n_cmp, n_slc)

    mask = sel[:, :, 0, :n_slc] > 0.5
    idx = jnp.sort(jnp.where(mask, jnp.arange(n_slc, dtype=jnp.int32), jnp.int32(1 << 20)), axis=-1)
    n_gather = SLC_TOPN - 1
    idx = idx[..., :n_gather]
    pages_per_block = PAGE_SIZE // SLC_BLOCK
    logical_page = idx // pages_per_block
    pages = jnp.take_along_axis(page_table[:, None, :], logical_page, axis=2).reshape(-1).astype(jnp.int32)
    halves = (idx % pages_per_block).reshape(-1).astype(jnp.int32)
    kpos = (logical_page[..., None] * PAGE_SIZE + jnp.arange(PAGE_SIZE, dtype=jnp.int32)).reshape(db, NSA_KV_HEADS, -1)
    hh = jnp.arange(NSA_KV_HEADS)[None, :, None, None]
    gg = jnp.arange(NSA_GROUP)[None, None, :, None]
    bias_sel = bd[hh, gg, (past - kpos)[:, :, None, :]]
    bias_sel = jnp.concatenate([bias_sel, jnp.broadcast_to(bd[None, :, :, 0:1], (db, NSA_KV_HEADS, NSA_GROUP, 1)),
                                jnp.zeros((db, NSA_KV_HEADS, NSA_GROUP, PAGE_SIZE - 1), F32)], axis=-1)
    bias_sel = jnp.concatenate([bias_sel, jnp.zeros_like(bias_sel)], axis=2)
    wlen = cache_win.shape[1]
    dist_w = np.clip(wlen - np.arange(wlen + PAGE_SIZE), 0, None)
    kvs_new = _kv_from_per_head(u[:db, COL_KVS:COL_KVS + 2 * D_KV])
    kvw_new = _kv_from_per_head(u[:db, COL_KVW:COL_KVW + 2 * D_KV])
    lane_pad = lambda a: jnp.pad(a[:, :, None], ((0, 0), (0, 0), (0, PAGE_SIZE - 1)))
    slc_t = cache_slc.transpose(0, 2, 3, 4, 1).reshape(n_pool * 2 * D_KV, PAGE_SIZE)
    win_t = cache_win.transpose(0, 2, 3, 4, 1).reshape(db, 2 * D_KV, wlen)
    o_s8, o_w8 = _sample_attn(pages, halves, q8, slc_t, lane_pad(kvs_new), bias_sel, win_t, lane_pad(kvw_new),
                              pad_g(bd[:, :, dist_w]), n_gather)
    win2d = cache_win.reshape(db, wlen, 2 * D_KV)

    heads = lambda o: _pad_rows(o[:, :, :NSA_GROUP, :].reshape(db, D_NSA), tok)
    y = _outproj(_pad_rows(y_ml, tok), heads(o_c8), heads(o_s8), heads(o_w8), small, nsa_norm_g, w_out_b, x16, tok, 512)
    kvshape = (1, db, 1, 2, NSA_KV_HEADS, NSA_HEAD_DIM)
    new_win = jnp.concatenate([win2d[:, 1:], kvw_new[:, None, :]], axis=1)
    states = (u[:db, COL_KVC:COL_KVC + 2 * D_KV].reshape(kvshape), kvs_new.reshape(kvshape),
              new_win.reshape((1, db, wlen, 2, NSA_KV_HEADS, NSA_HEAD_DIM)),
              c_n[None], n_n[None], m_n[None], conv_n[None])
    return y, states


def kernel(x_prompt, x_sample, cache_cmp_kv, cache_slc_kv, cache_win_kv, state_mlstm_C, state_mlstm_n,
           state_mlstm_m, state_conv, page_table, rel_bias, norm_mix_g, w_in, b_ig, b_fg, conv_w, ml_norm_g,
           cmp_pos, cmp_w1, cmp_w2, nsa_norm_g, w_out, norm_ffn_g, w_router_grp, b_router_grp, w_router_exp,
           b_router_exp, w_gate, w_up, w_down, norm_final_g):
    B, S, D = x_prompt.shape
    wb = _reorder_w_in(w_in[0])
    gate_bias = jnp.zeros((1, LANE), F32).at[0, 0:ML_HEADS].set(b_ig[0]).at[0, ML_HEADS:2 * ML_HEADS].set(b_fg[0])
    w_out_b = w_out[0].astype(BF16)
    yp, st_p = _prompt_mixer(x_prompt, norm_mix_g, wb, gate_bias, conv_w[0], ml_norm_g, cmp_pos[0], cmp_w1[0],
                             cmp_w2[0], nsa_norm_g, w_out_b, rel_bias)
    ys, st_s = _sample_mixer(x_sample, cache_cmp_kv[0], cache_slc_kv[0], cache_win_kv[0], state_mlstm_C[0],
                             state_mlstm_n[0], state_mlstm_m[0], state_conv[0], page_table, norm_mix_g, wb, gate_bias,
                             conv_w[0], ml_norm_g, cmp_pos[0], cmp_w1[0], cmp_w2[0], nsa_norm_g, w_out_b, rel_bias)
    moe_w = (norm_ffn_g, w_router_grp[0], b_router_grp[0], w_router_exp[0], b_router_exp[0],
             w_gate[0], w_up[0], w_down[0], norm_final_g[None, :])
    DB, L, _ = x_sample.shape
    out_p = _moe_final(yp.reshape(B * S, D), *moe_w, 256, 128).reshape(B, S, D)
    out_s = _moe_final(_pad_rows(ys, 128), *moe_w, 128, 16)[:DB].reshape(DB, L, D)
    outs = [out_p, out_s]
    for a, b in zip(st_p, st_s):
        outs += [a, b]
    return tuple(outs)


def _prompt_mixer(x_prompt, norm_g, wb, gate_bias, conv_w, ml_norm_g, cmp_pos, cmp_w1, cmp_w2, nsa_norm_g,
                  w_out_b, rel_bias):
    B, S, D = x_prompt.shape
    x2d = x_prompt.reshape(B * S, D)
    tm = min(1024, B * S)
    u = _proj(x2d, norm_g, wb, tm, 512)
    small = u[:, COL_SMALL:COL_SMALL + LANE]
    y_ml, conv_n, c_n, n_n, m_n = _mlstm(
        u, small, jnp.zeros((B, CONV_W - 1, 2 * D_ML), F32),
        jnp.zeros((B, ML_HEADS, ML_HEAD_DIM, ML_HEAD_DIM), F32), jnp.zeros((B, ML_HEADS, ML_HEAD_DIM), F32),
        jnp.full((B, ML_HEADS), -jnp.inf, F32), conv_w, gate_bias, ml_norm_g, B, S, 256)
    kv_c = u[:, COL_KVC:COL_KVC + 2 * D_KV]
    kv_s = _kv_from_per_head(u[:, COL_KVS:COL_KVS + 2 * D_KV])
    kv_w = _kv_from_per_head(u[:, COL_KVW:COL_KVW + 2 * D_KV])
    nch = S // CMP_STRIDE
    kvc = _compress(u, COL_KVC // (2 * D_KV), B, S, cmp_pos, cmp_w1, cmp_w2, min(S, 4096))
    tq_c = min(S, 256)
    o_c, sel_t = _cattn(u, kvc, rel_bias, B, S, tq_c)
    o_s = _flash(u, COL_KVS, rel_bias, sel_t, B, S)
    o_w = _flash(u, COL_KVW, rel_bias, None, B, S)
    y = _outproj(y_ml, o_c, o_s, o_w, small, nsa_norm_g, w_out_b, x2d, tm, 512)
    kvshape = (1, B, S, 2, NSA_KV_HEADS, NSA_HEAD_DIM)
    win = min(WINDOW, S)
    states = (kv_c.reshape(kvshape), kv_s.reshape(kvshape), kv_w.reshape(kvshape)[:, :, S - win:],
              c_n[None], n_n[None], m_n[None], conv_n[None])
    return y.reshape(B, S, D), states
```

```python
import functools
import math

import numpy as np
import jax
import jax.numpy as jnp
from jax import lax
from jax.experimental import pallas as pl
from jax.experimental.pallas import tpu as pltpu

F32 = jnp.float32
BF16 = jnp.bfloat16

D_MODEL = 2048
ML_HEADS = 4
ML_HEAD_DIM = 256
D_ML = 1024
CONV_W = 4
NSA_HEADS = 16
NSA_HEAD_DIM = 64
D_NSA = 1024
NSA_KV_HEADS = 4
NSA_GROUP = 4
D_KV = 256
CMP_BLOCK = 32
CMP_STRIDE = 16
CMP_HIDDEN = 256
SLC_BLOCK = 64
SLC_TOPN = 16
WINDOW = 512
N_BUCKETS = 32
MAX_DISTANCE = 2048
N_GROUPS = 4
EXPERTS_PER_GROUP = 8
N_EXPERTS = 32
D_EXPERT = 512
PAGE_SIZE = 128
EPS = 1e-6
NEG_BIG = -1e30
FORCE_SCORE = 1e4

LANE = 128
COL_QML, COL_KML, COL_VML, COL_OML = 0, 1024, 2048, 3072
COL_QNSA = 4096
COL_KVC, COL_KVS, COL_KVW = 5120, 5632, 6144
COL_SMALL = 6656
N_PROJ = 7168
VMEM_LIMIT = 56 * 1024 * 1024


def _cparams(sem, vmem=VMEM_LIMIT):
    return pltpu.CompilerParams(dimension_semantics=sem, vmem_limit_bytes=vmem)


def _split2(x):
    hi = x.astype(BF16)
    lo = (x - hi.astype(F32)).astype(BF16)
    return hi, lo


def _split3(x):
    hi = x.astype(BF16)
    r = x - hi.astype(F32)
    mid = r.astype(BF16)
    lo = (r - mid.astype(F32)).astype(BF16)
    return hi, mid, lo


def _dot(a, b):
    return jnp.dot(a, b, preferred_element_type=F32)


def _dot_nt(a, b):
    return lax.dot_general(a, b, (((1,), (1,)), ((), ())), preferred_element_type=F32)


def _dot_tn(a, b):
    return lax.dot_general(a, b, (((0,), (0,)), ((), ())), preferred_element_type=F32)


def _proj_kernel(x_ref, g_ref, w_ref, o_ref, h_scr):
    @pl.when(pl.program_id(1) == 0)
    def _():
        x = x_ref[...]
        ms = jnp.mean(x * x, axis=-1, keepdims=True)
        h_scr[...] = (x * lax.rsqrt(ms + EPS) * g_ref[...]).astype(BF16)

    o_ref[...] = _dot(h_scr[...], w_ref[...])


def _proj(x2d, g, wb, tm, tn):
    n, d = x2d.shape
    nc = wb.shape[1]
    return pl.pallas_call(
        _proj_kernel,
        grid=(n // tm, nc // tn),
        in_specs=[pl.BlockSpec((tm, d), lambda i, j: (i, 0)),
                  pl.BlockSpec((1, d), lambda i, j: (0, 0)),
                  pl.BlockSpec((d, tn), lambda i, j: (0, j))],
        out_specs=pl.BlockSpec((tm, tn), lambda i, j: (i, j)),
        out_shape=jax.ShapeDtypeStruct((n, nc), F32),
        scratch_shapes=[pltpu.VMEM((tm, d), BF16)],
        compiler_params=_cparams(("arbitrary", "arbitrary")),
        name="proj",
    )(x2d, g, wb)


def _reorder_w_in(w_in):
    big = w_in[:, :4 * D_ML]
    small_a = w_in[:, 4 * D_ML:4 * D_ML + 2 * ML_HEADS]
    rest = w_in[:, 4 * D_ML + 2 * ML_HEADS:]
    q_and_cmp = rest[:, :D_NSA + 2 * D_KV]
    gate = rest[:, D_NSA + 6 * D_KV:]
    d = w_in.shape[0]

    def per_head(w):
        return w.reshape(d, 2, NSA_KV_HEADS, NSA_HEAD_DIM).transpose(0, 2, 1, 3).reshape(d, 2 * D_KV)

    kv_s = per_head(rest[:, D_NSA + 2 * D_KV:D_NSA + 4 * D_KV])
    kv_w = per_head(rest[:, D_NSA + 4 * D_KV:D_NSA + 6 * D_KV])
    pad = jnp.zeros((d, N_PROJ - COL_SMALL - 2 * ML_HEADS - 3 * NSA_HEADS), w_in.dtype)
    return jnp.concatenate([big, q_and_cmp, kv_s, kv_w, small_a, gate, pad], axis=1).astype(BF16)


def _kv_from_per_head(kv2d):
    n = kv2d.shape[0]
    return kv2d.reshape(n, NSA_KV_HEADS, 2, NSA_HEAD_DIM).transpose(0, 2, 1, 3).reshape(n, 2 * D_KV)


def _log_sigmoid(x):
    return jnp.minimum(x, 0.0) - jnp.log1p(jnp.exp(-jnp.abs(x)))


def _mlstm_kernel(q_ref, k_ref, v_ref, o_ref, s_ref, cb_ref, c0_ref, n0_ref, m0_ref,
                  cw_ref, gb_ref, ng_ref,
                  y_ref, cbo_ref, co_ref, no_ref, mo_ref,
                  ext_scr, c_scr, n_scr, m_scr, *, T):
    c = pl.program_id(1)
    nc = pl.num_programs(1)

    @pl.when(c == 0)
    def _():
        ext_scr[0:8, :] = jnp.zeros((8, 2 * D_ML), F32)
        ext_scr[5:8, :] = cb_ref[0]
        c_scr[...] = c0_ref[0]
        n_scr[...] = n0_ref[0]
        m_scr[...] = m0_ref[0]

    ext_scr[8:8 + T, 0:D_ML] = q_ref[...]
    ext_scr[8:8 + T, D_ML:2 * D_ML] = k_ref[...]
    conv = ext_scr[5:5 + T, :] * cw_ref[0:1, :]
    for j in range(1, CONV_W):
        conv = conv + ext_scr[5 + j:5 + j + T, :] * cw_ref[j:j + 1, :]
    tail = ext_scr[8 + T - 3:8 + T, :]
    ext_scr[5:8, :] = tail
    cbo_ref[0] = tail
    qk = conv * jax.nn.sigmoid(conv)

    pre = s_ref[...] + gb_ref[...]
    col = lax.broadcasted_iota(jnp.int32, pre.shape, 1)
    padrow = s_ref[:, LANE - 1:LANE] > 0.5
    gates = jnp.where(col < ML_HEADS, pre, _log_sigmoid(pre))
    gates = jnp.where(padrow, jnp.where(col < ML_HEADS, NEG_BIG, 0.0), gates)
    g_r = gates.T
    ti = lax.broadcasted_iota(jnp.int32, (T, T), 0)
    si = lax.broadcasted_iota(jnp.int32, (T, T), 1)
    upper = (ti <= si).astype(BF16)
    g_fin = jnp.where(lax.broadcasted_iota(jnp.int32, g_r.shape, 0) < ML_HEADS, 0.0, g_r)
    hi, mid, lo = _split3(g_fin)
    cum_r = _dot(hi, upper) + _dot(mid, upper) + _dot(lo, upper)
    rowi = lax.broadcasted_iota(jnp.int32, g_r.shape, 0)
    a_r = jnp.where(rowi < ML_HEADS, g_r, cum_r)
    a_c = a_r.T
    causal = si <= ti

    for h in range(ML_HEADS):
        sl = slice(h * ML_HEAD_DIM, (h + 1) * ML_HEAD_DIM)
        q = qk[:, h * ML_HEAD_DIM:(h + 1) * ML_HEAD_DIM]
        k = qk[:, D_ML + h * ML_HEAD_DIM:D_ML + (h + 1) * ML_HEAD_DIM] * (ML_HEAD_DIM ** -0.5)
        v = v_ref[:, sl]
        ig_r = a_r[h:h + 1, :]
        b_r = a_r[ML_HEADS + h:ML_HEADS + h + 1, :]
        ig_c = a_c[:, h:h + 1]
        b_c = a_c[:, ML_HEADS + h:ML_HEADS + h + 1]
        m_prev = m_scr[h:h + 1, 0:1]
        logd = jnp.where(causal, b_c - b_r + ig_r, -jnp.inf)
        inter = b_c + m_prev
        m_t = jnp.maximum(inter, jnp.max(logd, axis=1, keepdims=True))
        w_intra = jnp.exp(logd - m_t)
        w_inter = jnp.exp(inter - m_t)
        qb = q.astype(BF16)
        kb = k.astype(BF16)
        vb = v.astype(BF16)
        sc = _dot_nt(qb, kb) * w_intra
        cmat = c_scr[h]
        nvec = n_scr[h:h + 1, :]
        num = _dot(sc.astype(BF16), vb) + w_inter * _dot(qb, cmat.astype(BF16))
        qn = jnp.sum(qb.astype(F32) * nvec.astype(BF16).astype(F32), axis=1, keepdims=True)
        den = jnp.sum(sc, axis=1, keepdims=True) + w_inter * qn
        hh = num / jnp.maximum(jnp.abs(den), jnp.exp(-m_t))
        m_new = m_t[T - 1:T, :]
        b_last = b_c[T - 1:T, :]
        w_s = jnp.exp(b_last - b_c + ig_c - m_new)
        decay = jnp.exp(b_last + m_prev - m_new)
        kw = k * w_s
        c_new = decay * cmat + _dot_tn(kw.astype(BF16), vb)
        n_new = decay * nvec + jnp.sum(kw, axis=0, keepdims=True)
        c_scr[h] = c_new
        n_scr[h:h + 1, :] = n_new
        m_scr[h:h + 1, :] = jnp.broadcast_to(m_new, (1, LANE))
        hn = hh * lax.rsqrt(jnp.mean(hh * hh, axis=1, keepdims=True) + EPS) * ng_ref[:, sl]
        y_ref[:, sl] = (hn * jax.nn.sigmoid(o_ref[:, sl])).astype(y_ref.dtype)

    @pl.when(c == nc - 1)
    def _():
        co_ref[0] = c_scr[...]
        no_ref[0] = n_scr[...]
        mo_ref[0] = m_scr[...]


def _mlstm(u, small, conv_buf, c0, n0, m0, conv_w, gate_bias, norm_g, batch, seq, T):
    nc = seq // T
    cb = D_ML // 1024
    m0b = jnp.broadcast_to(m0[:, :, None], (batch, ML_HEADS, LANE))
    m0b = jnp.concatenate([m0b, jnp.zeros((batch, 8 - ML_HEADS, LANE), F32)], axis=1)
    n0p = jnp.concatenate([n0, jnp.zeros((batch, 8 - ML_HEADS, ML_HEAD_DIM), F32)], axis=1)
    row = lambda b, c: (b * nc + c, 0)
    outs = pl.pallas_call(
        functools.partial(_mlstm_kernel, T=T),
        grid=(batch, nc),
        in_specs=[pl.BlockSpec((T, D_ML), lambda b, c: (b * nc + c, COL_QML // D_ML)),
                  pl.BlockSpec((T, D_ML), lambda b, c: (b * nc + c, COL_KML // D_ML)),
                  pl.BlockSpec((T, D_ML), lambda b, c: (b * nc + c, COL_VML // D_ML)),
                  pl.BlockSpec((T, D_ML), lambda b, c: (b * nc + c, COL_OML // D_ML)),
                  pl.BlockSpec((T, LANE), row),
                  pl.BlockSpec((1, CONV_W - 1, 2 * D_ML), lambda b, c: (b, 0, 0)),
                  pl.BlockSpec((1, ML_HEADS, ML_HEAD_DIM, ML_HEAD_DIM), lambda b, c: (b, 0, 0, 0)),
                  pl.BlockSpec((1, 8, ML_HEAD_DIM), lambda b, c: (b, 0, 0)),
                  pl.BlockSpec((1, 8, LANE), lambda b, c: (b, 0, 0)),
                  pl.BlockSpec((CONV_W, 2 * D_ML), lambda b, c: (0, 0)),
                  pl.BlockSpec((1, LANE), lambda b, c: (0, 0)),
                  pl.BlockSpec((1, D_ML), lambda b, c: (0, 0))],
        out_specs=[pl.BlockSpec((T, D_ML), row),
                   pl.BlockSpec((1, CONV_W - 1, 2 * D_ML), lambda b, c: (b, 0, 0)),
                   pl.BlockSpec((1, ML_HEADS, ML_HEAD_DIM, ML_HEAD_DIM), lambda b, c: (b, 0, 0, 0)),
                   pl.BlockSpec((1, 8, ML_HEAD_DIM), lambda b, c: (b, 0, 0)),
                   pl.BlockSpec((1, 8, LANE), lambda b, c: (b, 0, 0))],
        out_shape=[jax.ShapeDtypeStruct((batch * seq, D_ML), BF16),
                   jax.ShapeDtypeStruct((batch, CONV_W - 1, 2 * D_ML), F32),
                   jax.ShapeDtypeStruct((batch, ML_HEADS, ML_HEAD_DIM, ML_HEAD_DIM), F32),
                   jax.ShapeDtypeStruct((batch, 8, ML_HEAD_DIM), F32),
                   jax.ShapeDtypeStruct((batch, 8, LANE), F32)],
        scratch_shapes=[pltpu.VMEM((8 + T, 2 * D_ML), F32),
                        pltpu.VMEM((ML_HEADS, ML_HEAD_DIM, ML_HEAD_DIM), F32),
                        pltpu.VMEM((8, ML_HEAD_DIM), F32),
                        pltpu.VMEM((8, LANE), F32)],
        compiler_params=_cparams(("arbitrary", "arbitrary")),
        name="mlstm",
    )(u, u, u, u, small, conv_buf, c0, n0p, m0b, conv_w, gate_bias, norm_g)
    y, cbo, co, no, mo = outs
    return y, cbo, co, no[:, :ML_HEADS], mo[:, :ML_HEADS, 0]


def _bucket_np(dist):
    n = np.maximum(dist, 0)
    max_exact = N_BUCKETS // 2
    nf = np.maximum(n, 1).astype(np.float64)
    large = max_exact + (np.log(nf / max_exact) / math.log(MAX_DISTANCE / max_exact)
                         * (N_BUCKETS - max_exact)).astype(np.int64)
    return np.where(n < max_exact, n, np.minimum(large, N_BUCKETS - 1)).astype(np.int32)


def _bias_by_distance(rel_bias, n):
    return rel_bias.astype(F32)[_bucket_np(np.arange(n))].T


def _overlap_t(n_cmp, nch, n_slc):
    c0 = np.arange(nch) * CMP_STRIDE
    s0 = np.arange(n_slc) * SLC_BLOCK
    ov = np.minimum(c0[None, :] + CMP_BLOCK, s0[:, None] + SLC_BLOCK) - np.maximum(c0[None, :], s0[:, None])
    ov = np.clip(ov, 0, None).astype(np.float32) / CMP_BLOCK
    ov[:, n_cmp:] = 0.0
    return jnp.asarray(ov, BF16)


def _pq_kernel(x0_ref, x1_ref, x2_ref, x3_ref, pos_ref, w_ref, o_ref, *, rows):
    for pair, x_ref in enumerate((x0_ref, x1_ref, x2_ref, x3_ref)):
        toks = [x_ref[pl.ds(l, rows, stride=CMP_STRIDE), :] for l in range(CMP_STRIDE)]
        for half in range(2):
            ch = 2 * pair + half
            c = ch // NSA_KV_HEADS
            lanes = slice(half * NSA_HEAD_DIM, (half + 1) * NSA_HEAD_DIM)
            chunk = jnp.concatenate([t[:, lanes] for t in toks], axis=1)
            for part in range(2):
                lhs = (chunk + pos_ref[c, part:part + 1, :]).astype(BF16)
                o_ref[0, ch, :, part * CMP_HIDDEN:(part + 1) * CMP_HIDDEN] = _dot(lhs, w_ref[c, part])


def _hid_kernel(pq_ref, w2_ref, o_ref, *, nch):
    p = pq_ref[0, 0, :, 0:CMP_HIDDEN]
    q = pltpu.roll(pq_ref[0, 0, :, CMP_HIDDEN:2 * CMP_HIDDEN], nch - 1, 0)
    hid = jax.nn.gelu(p + q, approximate=True)
    o_ref[0, 0] = _dot(hid.astype(BF16), w2_ref[0])


def _compress(kv2d, col_block, bsz, seq, cmp_pos, cmp_w1, cmp_w2, tt):
    nch = seq // CMP_STRIDE
    kdim = CMP_STRIDE * NSA_HEAD_DIM
    nt = seq // tt
    tn = tt // CMP_STRIDE
    w1 = cmp_w1.reshape(2, 2, kdim, CMP_HIDDEN).astype(BF16)
    pos = cmp_pos.reshape(2, 2, kdim)
    pq = pl.pallas_call(
        functools.partial(_pq_kernel, rows=tn),
        grid=(bsz, nt),
        in_specs=[pl.BlockSpec((tt, LANE), functools.partial(lambda b, i, k: (b * nt + i, col_block * 4 + k), k=k))
                  for k in range(4)] +
                 [pl.BlockSpec((2, 2, kdim), lambda b, i: (0, 0, 0)),
                  pl.BlockSpec((2, 2, kdim, CMP_HIDDEN), lambda b, i: (0, 0, 0, 0))],
        out_specs=pl.BlockSpec((1, 8, tn, 2 * CMP_HIDDEN), lambda b, i: (b, 0, i, 0)),
        out_shape=jax.ShapeDtypeStruct((bsz, 8, nch, 2 * CMP_HIDDEN), F32),
        compiler_params=_cparams(("arbitrary", "arbitrary")),
        name="cmp_pq",
    )(kv2d, kv2d, kv2d, kv2d, pos, w1)
    return _cmp_hid(pq, cmp_w2)


def _cmp_hid(pq, cmp_w2):
    bsz, _, nch, _ = pq.shape
    return pl.pallas_call(
        functools.partial(_hid_kernel, nch=nch),
        grid=(bsz, 8),
        in_specs=[pl.BlockSpec((1, 1, nch, 2 * CMP_HIDDEN), lambda b, c: (b, c, 0, 0)),
                  pl.BlockSpec((1, CMP_HIDDEN, NSA_HEAD_DIM), lambda b, c: (c // NSA_KV_HEADS, 0, 0))],
        out_specs=pl.BlockSpec((1, 1, nch, NSA_HEAD_DIM), lambda b, c: (b, c, 0, 0)),
        out_shape=jax.ShapeDtypeStruct((bsz, 8, nch, NSA_HEAD_DIM), F32),
        compiler_params=_cparams(("arbitrary", "arbitrary")),
        name="cmp_hid",
    )(pq, cmp_w2.astype(BF16))


def _top_n_mask(score_t, blk, n_rows):
    rank = jnp.zeros(score_t.shape, jnp.int32)
    for i in range(n_rows):
        row = score_t[i:i + 1, :]
        beats = (row > score_t) | ((row == score_t) & (blk > i))
        rank = rank + beats.astype(jnp.int32)
    return (rank < min(SLC_TOPN, n_rows)).astype(F32)


def _bias_from_buckets(bucket, table_ref, first_head):
    biases = [jnp.zeros(bucket.shape, F32) for _ in range(NSA_GROUP)]
    for k in range(N_BUCKETS):
        hit = bucket == k
        biases = [jnp.where(hit, table_ref[k, first_head + g], biases[g]) for g in range(NSA_GROUP)]
    return biases


def _cattn_kernel(tab_ref, q_ref, kc_ref, vc_ref, bk_ref, ovt_ref, o_ref, sel_ref, bias_scr, strip_scr,
                  *, tq, nch, n_cmp, n_slc):
    h = pl.program_id(0)
    i = pl.program_id(1)

    @pl.when(pl.program_id(2) == 0)
    def _():
        bands = tq // CMP_STRIDE
        for g, strip in enumerate(_bias_from_buckets(bk_ref[0], tab_ref, h * NSA_GROUP)):
            strip_scr[g] = strip
            for a in range(bands):
                off = bands - 1 - a
                bias_scr[g, a * CMP_STRIDE:(a + 1) * CMP_STRIDE, :] = strip_scr[g, :, off:off + nch]

    kc = kc_ref[0, 0].astype(BF16)
    vc = vc_ref[0, 0].astype(BF16)
    t = i * tq + lax.broadcasted_iota(jnp.int32, (tq, nch), 0)
    n = lax.broadcasted_iota(jnp.int32, (tq, nch), 1)
    valid = (t - CMP_STRIDE * n - (CMP_BLOCK - 1) >= 0) & (n < n_cmp)
    pc = jnp.zeros((tq, nch), F32)
    for g in range(NSA_GROUP):
        sl = slice(g * NSA_HEAD_DIM, (g + 1) * NSA_HEAD_DIM)
        qg = (q_ref[:, sl] * (NSA_HEAD_DIM ** -0.5)).astype(BF16)
        s = _dot_nt(qg, kc) + bias_scr[g]
        s = jnp.where(valid, s, NEG_BIG)
        e = jnp.exp(s - jnp.max(s, axis=1, keepdims=True))
        p = jnp.where(valid, e / jnp.sum(e, axis=1, keepdims=True), 0.0)
        o_ref[:, sl] = _dot(p.astype(BF16), vc)
        pc = pc + p
    score_t = _dot_nt(ovt_ref[...], pc.astype(BF16))
    blk = lax.broadcasted_iota(jnp.int32, (n_slc, tq), 0)
    cur = (i * tq + lax.broadcasted_iota(jnp.int32, (n_slc, tq), 1)) // SLC_BLOCK
    forced = (blk == 0) | (blk == cur) | (blk == cur - 1)
    score_t = jnp.where(forced, FORCE_SCORE, score_t)
    score_t = jnp.where(blk > cur, -FORCE_SCORE, score_t)
    sel_ref[0, 0] = _top_n_mask(score_t, blk, n_slc)


def _cattn(u, kvc, rel_bias, batch, seq, tq):
    nch = kvc.shape[2]
    n_cmp = nch - 1
    n_slc = seq // SLC_BLOCK
    ni = seq // tq
    ovt = _overlap_t(n_cmp, nch, n_slc)
    qcol = COL_QNSA // (NSA_GROUP * NSA_HEAD_DIM)
    bands = tq // CMP_STRIDE
    strip_w = -(-(nch + bands - 1) // LANE) * LANE
    dist = (tq * np.arange(ni)[:, None, None] + np.arange(CMP_STRIDE)[None, :, None]
            - CMP_STRIDE * (np.arange(strip_w)[None, None, :] - (bands - 1)) - (CMP_BLOCK - 1))
    buckets = jnp.asarray(_bucket_np(dist))
    return pl.pallas_call(
        functools.partial(_cattn_kernel, tq=tq, nch=nch, n_cmp=n_cmp, n_slc=n_slc),
        grid=(NSA_KV_HEADS, ni, batch),
        in_specs=[pl.BlockSpec(memory_space=pltpu.SMEM),
                  pl.BlockSpec((tq, NSA_GROUP * NSA_HEAD_DIM), lambda h, i, b: (b * ni + i, qcol + h)),
                  pl.BlockSpec((1, 1, nch, NSA_HEAD_DIM), lambda h, i, b: (b, h, 0, 0)),
                  pl.BlockSpec((1, 1, nch, NSA_HEAD_DIM), lambda h, i, b: (b, NSA_KV_HEADS + h, 0, 0)),
                  pl.BlockSpec((1, CMP_STRIDE, strip_w), lambda h, i, b: (i, 0, 0)),
                  pl.BlockSpec((n_slc, nch), lambda h, i, b: (0, 0))],
        out_specs=[pl.BlockSpec((tq, NSA_GROUP * NSA_HEAD_DIM), lambda h, i, b: (b * ni + i, h)),
                   pl.BlockSpec((1, 1, n_slc, tq), lambda h, i, b: (b, h, 0, i))],
        out_shape=[jax.ShapeDtypeStruct((batch * seq, D_NSA), F32),
                   jax.ShapeDtypeStruct((batch, NSA_KV_HEADS, n_slc, seq), F32)],
        scratch_shapes=[pltpu.VMEM((NSA_GROUP, tq, nch), F32), pltpu.VMEM((NSA_GROUP, CMP_STRIDE, strip_w), F32)],
        compiler_params=_cparams(("arbitrary", "arbitrary", "arbitrary")),
        name="cmp_attn",
    )(rel_bias.astype(F32), u, kvc, kvc, buckets, ovt)


TQ = 128
FLASH_GROUP = 8


def _flash_kernel(tab_ref, q_ref, kv_ref, bk_ref, *rest, selected, n_delta):
    if selected:
        sel_ref, o_ref, band_scr, qs_scr, m_scr, l_scr, acc_scr = rest
    else:
        o_ref, band_scr, qs_scr, m_scr, l_scr, acc_scr = rest
    h = pl.program_id(0)
    i = pl.program_id(2)

    @pl.when((pl.program_id(1) == 0) & (i == 0))
    def _():
        def fill(d, carry):
            for g, bias in enumerate(_bias_from_buckets(bk_ref[d], tab_ref, h * NSA_GROUP)):
                band_scr[d, :, g * TQ:(g + 1) * TQ] = bias
            return carry
        lax.fori_loop(0, n_delta, fill, 0)

    for g in range(NSA_GROUP):
        qg = (q_ref[:, g * NSA_HEAD_DIM:(g + 1) * NSA_HEAD_DIM] * (NSA_HEAD_DIM ** -0.5)).astype(BF16)
        qs_scr[g * TQ:(g + 1) * TQ, :] = jnp.concatenate([qg, jnp.zeros_like(qg)], axis=1)
    is_key_lane = lax.broadcasted_iota(jnp.int32, (TQ, LANE), 1) < NSA_HEAD_DIM
    m_scr[...] = jnp.full(m_scr.shape, 0.5 * NEG_BIG, F32)
    l_scr[...] = jnp.zeros(l_scr.shape, F32)
    acc_scr[...] = jnp.zeros(acc_scr.shape, F32)
    key = lax.broadcasted_iota(jnp.int32, (TQ, TQ), 0)
    qry = lax.broadcasted_iota(jnp.int32, (TQ, TQ), 1)
    n_back = WINDOW // TQ

    def scores(j, kind):
        kvj = kv_ref[pl.ds(pl.multiple_of(j * TQ, TQ), TQ), :].astype(BF16)
        s = _dot_nt(kvj, qs_scr[...]) + band_scr[i - j]
        mask = None
        if kind == "diag":
            mask = qry >= key
        elif kind == "far":
            mask = qry < key
        if selected:
            r = sel_ref[0, 0, pl.ds(2 * j, 2), :]
            picked = jnp.where(key < SLC_BLOCK, r[0:1, :], r[1:2, :]) > 0.5
            mask = picked if mask is None else (mask & picked)
        if mask is not None:
            s = jnp.concatenate([jnp.where(mask, s[:, g * TQ:(g + 1) * TQ], NEG_BIG) for g in range(NSA_GROUP)], axis=1)
        return s, ones_and_values(j)

    def ones_and_values(j):
        kvj = kv_ref[pl.ds(pl.multiple_of(j * TQ, TQ), TQ), :].astype(BF16)
        return jnp.where(is_key_lane, jnp.ones_like(kvj), kvj)

    def update(tiles):
        m_old = m_scr[...]
        m_new = m_old
        for s, _ in tiles:
            m_new = jnp.maximum(m_new, jnp.max(s, axis=0, keepdims=True))
        alpha = jnp.exp(m_old - m_new)
        l_new = alpha * l_scr[...]
        acc = alpha * acc_scr[...]
        for s, ones_v in tiles:
            p = jnp.exp(s - m_new)
            pv = _dot_tn(ones_v, p.astype(BF16))
            l_new = l_new + pv[0:1, :]
            acc = acc + pv[NSA_HEAD_DIM:2 * NSA_HEAD_DIM, :]
        m_scr[...] = m_new
        l_scr[...] = l_new
        acc_scr[...] = acc

    def full_tiles(first, count):
        return [scores(first + k, "full") for k in range(count)]

    if selected:
        def group(t, carry):
            update(full_tiles(FLASH_GROUP * t, FLASH_GROUP))
            return carry

        n_groups = i // FLASH_GROUP
        lax.fori_loop(0, n_groups, group, 0)
        for rem in range(FLASH_GROUP):
            @pl.when(i - n_groups * FLASH_GROUP == rem)
            def _():
                update(full_tiles(n_groups * FLASH_GROUP, rem) + [scores(i, "diag")])
    else:
        @pl.when(i >= n_back)
        def _():
            update([scores(i - n_back, "far")] + full_tiles(i - n_back + 1, n_back - 1) + [scores(i, "diag")])

        for rem in range(n_back):
            @pl.when(i == rem)
            def _():
                update(full_tiles(0, rem) + [scores(i, "diag")])

    for g in range(NSA_GROUP):
        cols = slice(g * TQ, (g + 1) * TQ)
        o_ref[:, g * NSA_HEAD_DIM:(g + 1) * NSA_HEAD_DIM] = (acc_scr[:, cols] / l_scr[:, cols]).T


def _flash(u, kv_col, rel_bias, sel_t, batch, seq):
    ni = seq // TQ
    qcol = COL_QNSA // (NSA_GROUP * NSA_HEAD_DIM)
    selected = sel_t is not None
    n_delta = ni if selected else WINDOW // TQ + 1
    kvblk = kv_col // LANE
    delta = np.arange(n_delta)[:, None, None] * TQ + np.arange(TQ)[None, None, :] - np.arange(TQ)[None, :, None]
    buckets = jnp.asarray(_bucket_np(delta))
    in_specs = [pl.BlockSpec(memory_space=pltpu.SMEM),
                pl.BlockSpec((TQ, NSA_GROUP * NSA_HEAD_DIM), lambda h, b, i: (b * ni + i, qcol + h)),
                pl.BlockSpec((seq, LANE), lambda h, b, i: (b, kvblk + h)),
                pl.BlockSpec((n_delta, TQ, TQ), lambda h, b, i: (0, 0, 0))]
    args = [rel_bias.astype(F32), u, u, buckets]
    if selected:
        n_slc = sel_t.shape[2]
        in_specs.append(pl.BlockSpec((1, 1, n_slc, TQ), lambda h, b, i: (b, h, 0, i)))
        args.append(sel_t)
    scratch = [pltpu.VMEM((n_delta, TQ, NSA_GROUP * TQ), F32),
               pltpu.VMEM((NSA_GROUP * TQ, LANE), BF16),
               pltpu.VMEM((1, NSA_GROUP * TQ), F32),
               pltpu.VMEM((1, NSA_GROUP * TQ), F32),
               pltpu.VMEM((NSA_HEAD_DIM, NSA_GROUP * TQ), F32)]
    return pl.pallas_call(
        functools.partial(_flash_kernel, selected=selected, n_delta=n_delta),
        grid=(NSA_KV_HEADS, batch, ni),
        in_specs=in_specs,
        out_specs=pl.BlockSpec((TQ, NSA_GROUP * NSA_HEAD_DIM), lambda h, b, i: (b * ni + i, h)),
        out_shape=jax.ShapeDtypeStruct((batch * seq, D_NSA), F32),
        scratch_shapes=scratch,
        compiler_params=_cparams(("arbitrary", "arbitrary", "arbitrary")),
        name="slc_attn" if selected else "win_attn",
    )(*args)


def _gate_expand():
    e = np.zeros((3, 2 * LANE, D_NSA), np.float32)
    for br in range(3):
        for hd in range(NSA_HEADS):
            for part in range(2):
                e[br, part * LANE + 2 * ML_HEADS + br * NSA_HEADS + hd,
                  hd * NSA_HEAD_DIM:(hd + 1) * NSA_HEAD_DIM] = 1.0
    return jnp.asarray(e, BF16)


def _outproj_kernel(yml_ref, oc_ref, os_ref, ow_ref, s_ref, e_ref, ng_ref, w_ref, x_ref, o_ref, cat_scr):
    @pl.when(pl.program_id(1) == 0)
    def _():
        sig = jax.nn.sigmoid(s_ref[...])
        hi_lo = jnp.concatenate(_split2(sig), axis=1)
        o = jnp.zeros(oc_ref.shape, F32)
        for br, ref in enumerate((oc_ref, os_ref, ow_ref)):
            o = o + _dot(hi_lo, e_ref[br]) * ref[...]
        o = o * lax.rsqrt(jnp.mean(o * o, axis=-1, keepdims=True) + EPS) * ng_ref[...]
        cat_scr[:, 0:D_ML] = yml_ref[...]
        cat_scr[:, D_ML:D_ML + D_NSA] = o.astype(BF16)

    o_ref[...] = x_ref[...] + _dot(cat_scr[...], w_ref[...])


def _outproj(y_ml, o_c, o_s, o_w, small, nsa_g, w_out_b, x2d, tm, tn):
    n, d = x2d.shape
    rows = lambda i, j: (i, 0)
    return pl.pallas_call(
        _outproj_kernel,
        grid=(n // tm, d // tn),
        in_specs=[pl.BlockSpec((tm, D_ML), rows), pl.BlockSpec((tm, D_NSA), rows),
                  pl.BlockSpec((tm, D_NSA), rows), pl.BlockSpec((tm, D_NSA), rows),
                  pl.BlockSpec((tm, LANE), rows),
                  pl.BlockSpec((3, 2 * LANE, D_NSA), lambda i, j: (0, 0, 0)),
                  pl.BlockSpec((1, D_NSA), lambda i, j: (0, 0)),
                  pl.BlockSpec((D_ML + D_NSA, tn), lambda i, j: (0, j)),
                  pl.BlockSpec((tm, tn), lambda i, j: (i, j))],
        out_specs=pl.BlockSpec((tm, tn), lambda i, j: (i, j)),
        out_shape=jax.ShapeDtypeStruct((n, d), F32),
        scratch_shapes=[pltpu.VMEM((tm, D_ML + D_NSA), BF16)],
        compiler_params=_cparams(("arbitrary", "arbitrary")),
        name="outproj",
    )(y_ml, o_c, o_s, o_w, small, _gate_expand(), nsa_g, w_out_b, x2d)


ROUTE_COL = N_GROUPS
BIG_COL = 1 << 20
SPLIT = D_MODEL // LANE
DMA_UNROLL = 8


def _store_split(ref, val, n):
    for k in range(SPLIT):
        ref[pl.ds(k, n, stride=SPLIT), :] = val[:, k * LANE:(k + 1) * LANE]


def _load_split(ref, n):
    return jnp.concatenate([ref[pl.ds(k, n, stride=SPLIT), :] for k in range(SPLIT)], axis=1)


def _route_kernel(y_ref, g_ref, w_ref, b_ref, h_ref, info_ref, cnt_ref, carry_scr, *, tm):
    @pl.when(pl.program_id(0) == 0)
    def _():
        carry_scr[...] = jnp.zeros(carry_scr.shape, F32)

    x = y_ref[...]
    h = x * lax.rsqrt(jnp.mean(x * x, axis=-1, keepdims=True) + EPS) * g_ref[...]
    _store_split(h_ref, h, tm)
    logit = _dot(h.astype(BF16), w_ref[...]) + b_ref[...]
    col = lax.broadcasted_iota(jnp.int32, logit.shape, 1)
    is_grp = col < N_GROUPS
    gmax = jnp.max(jnp.where(is_grp, logit, -jnp.inf), axis=1, keepdims=True)
    gtop = jnp.min(jnp.where(is_grp & (logit == gmax), col, BIG_COL), axis=1, keepdims=True)
    gsum = jnp.sum(jnp.where(is_grp, jnp.exp(logit - gmax), 0.0), axis=1, keepdims=True)
    first = ROUTE_COL + gtop * EXPERTS_PER_GROUP
    in_grp = (col >= first) & (col < first + EXPERTS_PER_GROUP)
    v1 = jnp.max(jnp.where(in_grp, logit, -jnp.inf), axis=1, keepdims=True)
    i1 = jnp.min(jnp.where(in_grp & (logit == v1), col, BIG_COL), axis=1, keepdims=True)
    rest = in_grp & (col != i1)
    v2 = jnp.max(jnp.where(rest, logit, -jnp.inf), axis=1, keepdims=True)
    i2 = jnp.min(jnp.where(rest & (logit == v2), col, BIG_COL), axis=1, keepdims=True)
    e = jnp.exp(v2 - v1)
    w1 = 1.0 / ((1.0 + e) * gsum)
    w2 = e / ((1.0 + e) * gsum)
    pick1 = col == i1
    pick2 = col == i2
    both = (pick1 | pick2).astype(F32)
    ri = lax.broadcasted_iota(jnp.int32, (tm, tm), 0)
    ci = lax.broadcasted_iota(jnp.int32, (tm, tm), 1)
    before = (ci < ri).astype(BF16)
    cum = _dot(before, both.astype(BF16)) + carry_scr[...]
    r1 = jnp.sum(jnp.where(pick1, cum, 0.0), axis=1, keepdims=True)
    r2 = jnp.sum(jnp.where(pick2, cum, 0.0), axis=1, keepdims=True)
    carry_scr[...] = carry_scr[...] + jnp.sum(both, axis=0, keepdims=True)
    cnt_ref[...] = jnp.broadcast_to(carry_scr[...], cnt_ref.shape)
    info = jnp.where(col == 0, (i1 - ROUTE_COL).astype(F32), 0.0)
    info = jnp.where(col == 1, (i2 - ROUTE_COL).astype(F32), info)
    info = jnp.where(col == 2, w1, info)
    info = jnp.where(col == 3, w2, info)
    info = jnp.where(col == 4, r1, info)
    info_ref[...] = jnp.where(col == 5, r2, info)


def _route(y2d, g, w_rg, b_rg, w_re, b_re, tm):
    n, d = y2d.shape
    wr = jnp.concatenate([w_rg, w_re, jnp.zeros((d, LANE - N_GROUPS - N_EXPERTS), F32)], axis=1).astype(BF16)
    bias = jnp.concatenate([b_rg, b_re, jnp.zeros((LANE - N_GROUPS - N_EXPERTS,), F32)])[None, :]
    rows = lambda i: (i, 0)
    fixed = lambda i: (0, 0)
    return pl.pallas_call(
        functools.partial(_route_kernel, tm=tm),
        grid=(n // tm,),
        in_specs=[pl.BlockSpec((tm, d), rows), pl.BlockSpec((1, d), fixed),
                  pl.BlockSpec((d, LANE), fixed), pl.BlockSpec((1, LANE), fixed)],
        out_specs=[pl.BlockSpec((tm * SPLIT, LANE), rows), pl.BlockSpec((tm, LANE), rows),
                   pl.BlockSpec((8, LANE), fixed)],
        out_shape=[jax.ShapeDtypeStruct((n * SPLIT, LANE), F32), jax.ShapeDtypeStruct((n, LANE), F32),
                   jax.ShapeDtypeStruct((8, LANE), F32)],
        scratch_shapes=[pltpu.VMEM((1, LANE), F32)],
        compiler_params=_cparams(("arbitrary",)),
        name="moe_route",
    )(y2d, g, wr, bias)


W_CHUNKS = 4


def _expert_kernel(te_ref, nu_ref, src_ref, nx_ref, ws_ref, h_hbm, wg_hbm, wu_hbm, wd_hbm, o_ref,
                   xbuf, wg_f, wu_f, wd_f, wg_b, wu_b, wd_b, sem, wsem, *, tme):
    i = pl.program_id(0)
    slot = i % 2

    def weight_copies(expert, wslot):
        copies = []
        for hbm, buf in ((wg_hbm, wg_f), (wu_hbm, wu_f), (wd_hbm, wd_f)):
            step = hbm.shape[1] // W_CHUNKS
            for c in range(W_CHUNKS):
                copies.append(pltpu.make_async_copy(hbm.at[expert, pl.ds(c * step, step)],
                                                    buf.at[wslot, pl.ds(c * step, step)], wsem.at[wslot]))
        return copies

    def row_copy(tile, r, buf_slot):
        src_row = src_ref[tile * tme + r]
        return pltpu.make_async_copy(h_hbm.at[pl.ds(pl.multiple_of(src_row * SPLIT, SPLIT), SPLIT)],
                                     xbuf.at[buf_slot, pl.ds(pl.multiple_of(r * SPLIT, SPLIT), SPLIT)],
                                     sem.at[buf_slot])

    def fetch(tile, buf_slot):
        def body(r, carry):
            row_copy(tile, r, buf_slot).start()
            return carry
        lax.fori_loop(0, tme, body, 0, unroll=DMA_UNROLL)

    @pl.when(i == 0)
    def _():
        fetch(0, 0)
        for cp in weight_copies(te_ref[0], ws_ref[0]):
            cp.start()

    @pl.when(i + 1 < nu_ref[0])
    def _():
        fetch(i + 1, 1 - slot)

    @pl.when(i < nu_ref[0])
    def _():
        prev = te_ref[jnp.maximum(i - 1, 0)]

        @pl.when((i == 0) | (te_ref[i] != prev))
        def _():
            w = ws_ref[i]
            for cp in weight_copies(te_ref[i], w):
                cp.wait()
            wg_b[...] = wg_f[w].astype(BF16)
            wu_b[...] = wu_f[w].astype(BF16)
            wd_b[...] = wd_f[w].astype(BF16)

            @pl.when(nx_ref[i] >= 0)
            def _():
                for cp in weight_copies(nx_ref[i], 1 - w):
                    cp.start()

        def wait_body(r, carry):
            row_copy(i, r, slot).wait()
            return carry
        lax.fori_loop(0, tme, wait_body, 0, unroll=DMA_UNROLL)

        x = _load_split(xbuf.at[slot], tme).astype(BF16)
        a = _dot(x, wg_b[...])
        u = _dot(x, wu_b[...])
        hid = a * jax.nn.sigmoid(a) * u
        _store_split(o_ref, _dot(hid.astype(BF16), wd_b[...]), tme)

    @pl.when(i >= nu_ref[0])
    def _():
        o_ref[...] = jnp.zeros(o_ref.shape, F32)


def _experts(h, src, tile_expert, n_used, next_expert, weight_slot, w_gate, w_up, w_down, tme):
    p = src.shape[0]
    d = D_MODEL
    hbm = pl.BlockSpec(memory_space=pl.ANY)
    return pl.pallas_call(
        functools.partial(_expert_kernel, tme=tme),
        grid_spec=pltpu.PrefetchScalarGridSpec(
            num_scalar_prefetch=5, grid=(p // tme,),
            in_specs=[hbm, hbm, hbm, hbm],
            out_specs=pl.BlockSpec((tme * SPLIT, LANE), lambda i, *_: (i, 0)),
            scratch_shapes=[pltpu.VMEM((2, tme * SPLIT, LANE), F32),
                            pltpu.VMEM((2, d, D_EXPERT), F32), pltpu.VMEM((2, d, D_EXPERT), F32),
                            pltpu.VMEM((2, D_EXPERT, d), F32),
                            pltpu.VMEM((d, D_EXPERT), BF16), pltpu.VMEM((d, D_EXPERT), BF16),
                            pltpu.VMEM((D_EXPERT, d), BF16),
                            pltpu.SemaphoreType.DMA((2,)), pltpu.SemaphoreType.DMA((2,))]),
        out_shape=jax.ShapeDtypeStruct((p * SPLIT, LANE), F32),
        compiler_params=_cparams(("arbitrary",)),
        name="moe_experts",
    )(tile_expert, n_used, src, next_expert, weight_slot, h, w_gate, w_up, w_down)


def _combine_kernel(d1_ref, d2_ref, y_ref, info_ref, ys_hbm, g_ref, o_ref, a_buf, b_buf, sem, *, tm):
    i = pl.program_id(0)
    slot = i % 2

    def copy(r, row, buf, buf_slot):
        return pltpu.make_async_copy(ys_hbm.at[pl.ds(pl.multiple_of(row * SPLIT, SPLIT), SPLIT)],
                                     buf.at[buf_slot, pl.ds(pl.multiple_of(r * SPLIT, SPLIT), SPLIT)],
                                     sem.at[buf_slot])

    def fetch(tile, buf_slot):
        def issue(r, carry):
            copy(r, d1_ref[tile * tm + r], a_buf, buf_slot).start()
            copy(r, d2_ref[tile * tm + r], b_buf, buf_slot).start()
            return carry
        lax.fori_loop(0, tm, issue, 0, unroll=DMA_UNROLL)

    @pl.when(i == 0)
    def _():
        fetch(0, 0)

    @pl.when(i + 1 < pl.num_programs(0))
    def _():
        fetch(i + 1, 1 - slot)

    def drain(r, carry):
        copy(r, 0, a_buf, slot).wait()
        copy(r, 0, b_buf, slot).wait()
        return carry

    lax.fori_loop(0, tm, drain, 0, unroll=DMA_UNROLL)
    w1 = info_ref[:, 2:3]
    w2 = info_ref[:, 3:4]
    y = y_ref[...] + w1 * _load_split(a_buf.at[slot], tm) + w2 * _load_split(b_buf.at[slot], tm)
    o_ref[...] = y * lax.rsqrt(jnp.mean(y * y, axis=-1, keepdims=True) + EPS) * g_ref[...]


def _combine(y2d, info, ys, d1, d2, g, tm):
    n, d = y2d.shape
    return pl.pallas_call(
        functools.partial(_combine_kernel, tm=tm),
        grid_spec=pltpu.PrefetchScalarGridSpec(
            num_scalar_prefetch=2, grid=(n // tm,),
            in_specs=[pl.BlockSpec((tm, d), lambda i, a, b: (i, 0)),
                      pl.BlockSpec((tm, LANE), lambda i, a, b: (i, 0)),
                      pl.BlockSpec(memory_space=pl.ANY),
                      pl.BlockSpec((1, d), lambda i, a, b: (0, 0))],
            out_specs=pl.BlockSpec((tm, d), lambda i, a, b: (i, 0)),
            scratch_shapes=[pltpu.VMEM((2, tm * SPLIT, LANE), F32), pltpu.VMEM((2, tm * SPLIT, LANE), F32),
                            pltpu.SemaphoreType.DMA((2,))]),
        out_shape=jax.ShapeDtypeStruct((n, d), F32),
        compiler_params=_cparams(("arbitrary",)),
        name="moe_combine",
    )(d1, d2, y2d, info, ys, g)


def _moe_final(y2d, ffn_g, w_rg, b_rg, w_re, b_re, w_gate, w_up, w_down, final_g, tm, tme):
    n, d = y2d.shape
    h, info, cnt = _route(y2d, ffn_g, w_rg, b_rg, w_re, b_re, tm)
    e1 = info[:, 0].astype(jnp.int32)
    e2 = info[:, 1].astype(jnp.int32)
    counts = cnt[0, ROUTE_COL:ROUTE_COL + N_EXPERTS].astype(jnp.int32)
    padded = (counts + tme - 1) // tme * tme
    ends = jnp.cumsum(padded)
    offs = ends - padded
    d1 = offs[e1] + info[:, 4].astype(jnp.int32)
    d2 = offs[e2] + info[:, 5].astype(jnp.int32)
    p = 2 * n + N_EXPERTS * tme
    tok = jnp.arange(n, dtype=jnp.int32)
    src = jnp.zeros((p,), jnp.int32).at[jnp.concatenate([d1, d2])].set(jnp.concatenate([tok, tok]))
    n_tiles = p // tme
    n_used = (ends[-1] // tme).astype(jnp.int32).reshape(1)
    tile_start = jnp.arange(n_tiles, dtype=jnp.int32) * tme
    tile_expert = jnp.sum((ends[None, :] <= tile_start[:, None]).astype(jnp.int32), axis=1)
    last = jnp.sum((ends <= ends[-1] - 1).astype(jnp.int32))
    tile_expert = jnp.minimum(tile_expert, last).astype(jnp.int32)
    run = jnp.cumsum(jnp.concatenate([jnp.zeros((1,), jnp.int32),
                                      (tile_expert[1:] != tile_expert[:-1]).astype(jnp.int32)]))
    weight_slot = (run % 2).astype(jnp.int32)
    after = ends[tile_expert] // tme
    next_expert = jnp.where(after < n_used[0], tile_expert[jnp.minimum(after, n_tiles - 1)], -1).astype(jnp.int32)
    ys = _experts(h, src, tile_expert, n_used, next_expert, weight_slot, w_gate, w_up, w_down, tme)
    return _combine(y2d, info, ys, d1, d2, final_g, tm)


GROUP_ROWS = 8
SLC_LANES = 384
SEL_KEYS = 1024
WIN_KEYS = 640


def _sample_cattn_kernel(q_ref, kc_ref, vc_ref, b_ref, ov_ref, o_ref, sel_ref, *, n_cmp, n_slc):
    q = (q_ref[0, 0] * (NSA_HEAD_DIM ** -0.5)).astype(BF16)
    s = _dot_nt(q, kc_ref[0, 0].astype(BF16)) + b_ref[0]
    valid = lax.broadcasted_iota(jnp.int32, s.shape, 1) < n_cmp
    s = jnp.where(valid, s, NEG_BIG)
    e = jnp.exp(s - jnp.max(s, axis=1, keepdims=True))
    p = jnp.where(valid, e / jnp.sum(e, axis=1, keepdims=True), 0.0)
    o_ref[0, 0] = _dot(p.astype(BF16), vc_ref[0, 0].astype(BF16))
    pg = jnp.where(lax.broadcasted_iota(jnp.int32, p.shape, 0) < NSA_GROUP, p, 0.0)
    pc = jnp.broadcast_to(jnp.sum(pg, axis=0, keepdims=True), pg.shape)
    score = _dot(pc.astype(BF16), ov_ref[...])[0:1, :]
    blk = lax.broadcasted_iota(jnp.int32, score.shape, 1)
    cur = n_slc - 1
    forced = (blk == 0) | (blk == cur) | (blk == cur - 1)
    score = jnp.where(forced, FORCE_SCORE, score)
    score = jnp.where(blk > cur, -jnp.inf, score)
    ri = lax.broadcasted_iota(jnp.int32, (SLC_LANES, SLC_LANES), 0)
    ci = lax.broadcasted_iota(jnp.int32, (SLC_LANES, SLC_LANES), 1)
    sb = jnp.broadcast_to(score, (SLC_LANES, SLC_LANES))
    col = jnp.sum(jnp.where(ri == ci, sb, 0.0), axis=1, keepdims=True)
    beats = (col > sb) | ((col == sb) & (ri < ci))
    rank = jnp.sum(beats.astype(jnp.int32), axis=0, keepdims=True)
    sel_ref[0, 0] = jnp.broadcast_to((rank < SLC_TOPN).astype(F32), (GROUP_ROWS, SLC_LANES))


def _sample_cattn(q8, kvc, bias_cs, n_cmp, n_slc):
    bsz, _, nch, _ = kvc.shape
    c0 = np.arange(nch) * CMP_STRIDE
    s0 = np.arange(SLC_LANES) * SLC_BLOCK
    ov = np.minimum(c0[:, None] + CMP_BLOCK, s0[None, :] + SLC_BLOCK) - np.maximum(c0[:, None], s0[None, :])
    ov = np.clip(ov, 0, None).astype(np.float32) / CMP_BLOCK
    ov[n_cmp:, :] = 0.0
    ov[:, n_slc:] = 0.0
    return pl.pallas_call(
        functools.partial(_sample_cattn_kernel, n_cmp=n_cmp, n_slc=n_slc),
        grid=(bsz, NSA_KV_HEADS),
        in_specs=[pl.BlockSpec((1, 1, GROUP_ROWS, NSA_HEAD_DIM), lambda b, h: (b, h, 0, 0)),
                  pl.BlockSpec((1, 1, nch, NSA_HEAD_DIM), lambda b, h: (b, h, 0, 0)),
                  pl.BlockSpec((1, 1, nch, NSA_HEAD_DIM), lambda b, h: (b, NSA_KV_HEADS + h, 0, 0)),
                  pl.BlockSpec((1, GROUP_ROWS, nch), lambda b, h: (h, 0, 0)),
                  pl.BlockSpec((nch, SLC_LANES), lambda b, h: (0, 0))],
        out_specs=[pl.BlockSpec((1, 1, GROUP_ROWS, NSA_HEAD_DIM), lambda b, h: (b, h, 0, 0)),
                   pl.BlockSpec((1, 1, GROUP_ROWS, SLC_LANES), lambda b, h: (b, h, 0, 0))],
        out_shape=[jax.ShapeDtypeStruct((bsz, NSA_KV_HEADS, GROUP_ROWS, NSA_HEAD_DIM), F32),
                   jax.ShapeDtypeStruct((bsz, NSA_KV_HEADS, GROUP_ROWS, SLC_LANES), F32)],
        compiler_params=_cparams(("arbitrary", "arbitrary")),
        name="sample_cmp_attn",
    )(q8, kvc, kvc, bias_cs, jnp.asarray(ov, BF16))


def _sample_attn_kernel(pg_ref, hf_ref, q_ref, cache_hbm, snew_ref, bs_ref, win_ref, wnew_ref, bw_ref,
                        os_ref, ow_ref, kt_buf, vt_buf, sem, *, n_gather, wlen):
    b = pl.program_id(0)
    rows_per_page = 2 * NSA_KV_HEADS * NSA_HEAD_DIM

    def page_copy(h, s, c, buf):
        page = pg_ref[(b * NSA_KV_HEADS + h) * n_gather + s]
        start = pl.multiple_of(page * rows_per_page + (c * NSA_KV_HEADS + h) * NSA_HEAD_DIM, NSA_HEAD_DIM)
        return pltpu.make_async_copy(cache_hbm.at[pl.ds(start, NSA_HEAD_DIM)],
                                     buf.at[h, :, pl.ds(s * PAGE_SIZE, PAGE_SIZE)], sem)

    for h in range(NSA_KV_HEADS):
        for s in range(n_gather):
            page_copy(h, s, 0, kt_buf).start()
            page_copy(h, s, 1, vt_buf).start()

    def attend(q, kts, vts, bias, valid):
        s = jnp.concatenate([_dot(q, kt.astype(BF16)) for kt in kts], axis=1) + bias
        s = jnp.where(valid, s, NEG_BIG)
        e = jnp.exp(s - jnp.max(s, axis=1, keepdims=True))
        p = jnp.where(valid, e / jnp.sum(e, axis=1, keepdims=True), 0.0).astype(BF16)
        out = None
        lo = 0
        for vt in vts:
            n = vt.shape[1]
            part = _dot_nt(p[:, lo:lo + n], vt.astype(BF16))
            out = part if out is None else out + part
            lo += n
        return out

    def head_rows(ref, c, h):
        r0 = (c * NSA_KV_HEADS + h) * NSA_HEAD_DIM
        return ref[0, r0:r0 + NSA_HEAD_DIM, :]

    widx = lax.broadcasted_iota(jnp.int32, (GROUP_ROWS, wlen + PAGE_SIZE), 1)
    wvalid = (widx >= wlen + 1 - WINDOW) & (widx <= wlen)
    for h in range(NSA_KV_HEADS):
        q = (q_ref[0, h] * (NSA_HEAD_DIM ** -0.5)).astype(BF16)
        ow_ref[0, h] = attend(q, [head_rows(win_ref, 0, h), head_rows(wnew_ref, 0, h)],
                              [head_rows(win_ref, 1, h), head_rows(wnew_ref, 1, h)], bw_ref[h], wvalid)

    for h in range(NSA_KV_HEADS):
        for s in range(n_gather):
            page_copy(h, s, 0, kt_buf).wait()
            page_copy(h, s, 1, vt_buf).wait()

    keys = n_gather * PAGE_SIZE
    lane = lax.broadcasted_iota(jnp.int32, (GROUP_ROWS, keys + PAGE_SIZE), 1)
    slot = lane >> 7
    lane_half = (lane >> 6) & 1
    for h in range(NSA_KV_HEADS):
        want = jnp.full(lane.shape, -1, jnp.int32)
        for s in range(n_gather):
            want = jnp.where(slot == s, hf_ref[(b * NSA_KV_HEADS + h) * n_gather + s], want)
        svalid = (lane_half == want) | (lane == keys)
        q = (q_ref[0, h] * (NSA_HEAD_DIM ** -0.5)).astype(BF16)
        os_ref[0, h] = attend(q, [kt_buf[h], head_rows(snew_ref, 0, h)],
                              [vt_buf[h], head_rows(snew_ref, 1, h)], bs_ref[0, h], svalid)


def _sample_attn(pages, halves, q8, cache_t, snew_t, bias_sel, win_t, wnew_t, bias_w, n_gather):
    bsz = q8.shape[0]
    wlen = win_t.shape[2]
    keys = n_gather * PAGE_SIZE
    rows = 2 * NSA_KV_HEADS * NSA_HEAD_DIM
    qspec = pl.BlockSpec((1, NSA_KV_HEADS, GROUP_ROWS, NSA_HEAD_DIM), lambda b, pg, hf: (b, 0, 0, 0))
    newspec = pl.BlockSpec((1, rows, PAGE_SIZE), lambda b, pg, hf: (b, 0, 0))
    return pl.pallas_call(
        functools.partial(_sample_attn_kernel, n_gather=n_gather, wlen=wlen),
        grid_spec=pltpu.PrefetchScalarGridSpec(
            num_scalar_prefetch=2, grid=(bsz,),
            in_specs=[qspec,
                      pl.BlockSpec(memory_space=pl.ANY),
                      newspec,
                      pl.BlockSpec((1, NSA_KV_HEADS, GROUP_ROWS, keys + PAGE_SIZE), lambda b, pg, hf: (b, 0, 0, 0)),
                      pl.BlockSpec((1, rows, wlen), lambda b, pg, hf: (b, 0, 0)),
                      newspec,
                      pl.BlockSpec((NSA_KV_HEADS, GROUP_ROWS, wlen + PAGE_SIZE), lambda b, pg, hf: (0, 0, 0))],
            out_specs=[qspec, qspec],
            scratch_shapes=[pltpu.VMEM((NSA_KV_HEADS, NSA_HEAD_DIM, keys), F32),
                            pltpu.VMEM((NSA_KV_HEADS, NSA_HEAD_DIM, keys), F32),
                            pltpu.SemaphoreType.DMA(())]),
        out_shape=[jax.ShapeDtypeStruct(q8.shape, F32), jax.ShapeDtypeStruct(q8.shape, F32)],
        compiler_params=_cparams(("arbitrary",)),
        name="sample_slc_win_attn",
    )(pages, halves, q8, cache_t, snew_t, bias_sel, win_t, wnew_t, bias_w)


PAGES_PER_STEP = 16


def _paged_pq_kernel(pt_ref, cache_hbm, pos_ref, w_ref, o_ref, pbuf, tok_scr, sem):
    b = pl.program_id(0)
    g = pl.program_id(1)
    ng = pl.num_programs(1)
    lin = b * ng + g
    slot = lin % 2
    rows_per_page = 2 * NSA_KV_HEADS * NSA_HEAD_DIM
    chunks = PAGES_PER_STEP * (PAGE_SIZE // CMP_STRIDE)

    def page_copy(step, p, buf_slot):
        sb = step // ng
        sg = step - sb * ng
        page = pt_ref[sb, sg * PAGES_PER_STEP + p]
        return pltpu.make_async_copy(cache_hbm.at[pl.ds(pl.multiple_of(page * rows_per_page, rows_per_page),
                                                        rows_per_page)],
                                     pbuf.at[buf_slot, p], sem.at[buf_slot])

    def fetch(step, buf_slot):
        for p in range(PAGES_PER_STEP):
            page_copy(step, p, buf_slot).start()

    @pl.when(lin == 0)
    def _():
        fetch(0, 0)

    @pl.when(lin + 1 < pl.num_programs(0) * ng)
    def _():
        fetch(lin + 1, 1 - slot)

    for p in range(PAGES_PER_STEP):
        page_copy(lin, p, slot).wait()

    for c in range(2):
        head_chunks = []
        for pair in (2 * c, 2 * c + 1):
            for p in range(PAGES_PER_STEP):
                tok_scr[p * PAGE_SIZE:(p + 1) * PAGE_SIZE, :] = pbuf[slot, p, pair * LANE:(pair + 1) * LANE, :].T
            toks = [tok_scr[pl.ds(l, chunks, stride=CMP_STRIDE), :] for l in range(CMP_STRIDE)]
            for half in range(2):
                lanes = slice(half * NSA_HEAD_DIM, (half + 1) * NSA_HEAD_DIM)
                head_chunks.append(jnp.concatenate([t[:, lanes] for t in toks], axis=1))
        stacked = jnp.concatenate(head_chunks, axis=0)
        for part in range(2):
            lhs = (stacked + pos_ref[c, part:part + 1, :]).astype(BF16)
            out = _dot(lhs, w_ref[c, part])
            for hh in range(NSA_KV_HEADS):
                o_ref[0, c * NSA_KV_HEADS + hh, :, part * CMP_HIDDEN:(part + 1) * CMP_HIDDEN] = (
                    out[hh * chunks:(hh + 1) * chunks, :])


def _paged_pq(cache_t, page_table, cmp_pos, cmp_w1):
    db, n_pages = page_table.shape
    kdim = CMP_STRIDE * NSA_HEAD_DIM
    nch = n_pages * (PAGE_SIZE // CMP_STRIDE)
    tn = PAGES_PER_STEP * (PAGE_SIZE // CMP_STRIDE)
    rows_per_page = 2 * NSA_KV_HEADS * NSA_HEAD_DIM
    return pl.pallas_call(
        _paged_pq_kernel,
        grid_spec=pltpu.PrefetchScalarGridSpec(
            num_scalar_prefetch=1, grid=(db, n_pages // PAGES_PER_STEP),
            in_specs=[pl.BlockSpec(memory_space=pl.ANY),
                      pl.BlockSpec((2, 2, kdim), lambda b, g, pt: (0, 0, 0)),
                      pl.BlockSpec((2, 2, kdim, CMP_HIDDEN), lambda b, g, pt: (0, 0, 0, 0))],
            out_specs=pl.BlockSpec((1, 8, tn, 2 * CMP_HIDDEN), lambda b, g, pt: (b, 0, g, 0)),
            scratch_shapes=[pltpu.VMEM((2, PAGES_PER_STEP, rows_per_page, PAGE_SIZE), F32),
                            pltpu.VMEM((PAGES_PER_STEP * PAGE_SIZE, LANE), F32),
                            pltpu.SemaphoreType.DMA((2,))]),
        out_shape=jax.ShapeDtypeStruct((db, 8, nch, 2 * CMP_HIDDEN), F32),
        compiler_params=_cparams(("arbitrary", "arbitrary")),
        name="cmp_pq_paged",
    )(page_table, cache_t, cmp_pos.reshape(2, 2, kdim), cmp_w1.reshape(2, 2, kdim, CMP_HIDDEN).astype(BF16))


def _pad_rows(a, n):
    return jnp.concatenate([a, jnp.zeros((n - a.shape[0],) + a.shape[1:], a.dtype)], axis=0)


def _sample_mixer(x_sample, cache_cmp, cache_slc, cache_win, st_c, st_n, st_m, st_conv, page_table,
                  norm_g, wb, gate_bias, conv_w, ml_norm_g, cmp_pos, cmp_w1, cmp_w2, nsa_norm_g, w_out_b, rel_bias):
    db = x_sample.shape[0]
    n_pages = page_table.shape[1]
    past = n_pages * PAGE_SIZE
    tok = 16
    x16 = _pad_rows(x_sample.reshape(db, D_MODEL), tok)
    u = _proj(x16, norm_g, wb, tok, 512)
    small = u[:, COL_SMALL:COL_SMALL + LANE]

    T = 128
    useq = jnp.zeros((db, T, 4 * D_ML), F32)
    useq = useq.at[:, T - CONV_W:T - 1, 0:2 * D_ML].set(st_conv)
    useq = useq.at[:, T - 1, :].set(u[:db, 0:4 * D_ML])
    sseq = jnp.zeros((db, T, LANE), F32).at[:, :T - 1, LANE - 1].set(1.0)
    sseq = sseq.at[:, T - 1, :].set(small[:db])
    y_seq, conv_n, c_n, n_n, m_n = _mlstm(
        useq.reshape(db * T, 4 * D_ML), sseq.reshape(db * T, LANE), jnp.zeros((db, CONV_W - 1, 2 * D_ML), F32),
        st_c, st_n, st_m, conv_w, gate_bias, ml_norm_g, db, T, T)
    y_ml = y_seq.reshape(db, T, D_ML)[:, T - 1]

    n_pool = cache_cmp.shape[0]
    cmp_t = cache_cmp.transpose(0, 2, 3, 4, 1).reshape(n_pool * 2 * D_KV, PAGE_SIZE)
    kvc = _cmp_hid(_paged_pq(cmp_t, page_table, cmp_pos, cmp_w1), cmp_w2)
    nch = past // CMP_STRIDE
    n_cmp = (past + 1) // CMP_STRIDE - CMP_BLOCK // CMP_STRIDE + 1
    n_slc = -(-(past + 1) // SLC_BLOCK)
    q = u[:db, COL_QNSA:COL_QNSA + D_NSA].reshape(db, NSA_KV_HEADS, NSA_GROUP, NSA_HEAD_DIM)
    q8 = jnp.concatenate([q, jnp.zeros_like(q)], axis=2)
    bd = _bias_by_distance(rel_bias, past + 1).reshape(NSA_KV_HEADS, NSA_GROUP, past + 1)
    pad_g = lambda a: jnp.concatenate([a, jnp.zeros_like(a)], axis=1)
    dist_c = np.clip(past - (np.arange(nch) * CMP_STRIDE + CMP_BLOCK - 1), 0, None)
    o_c8, sel = _sample_cattn(q8, kvc, pad_g(bd[:, :, dist_c]), n_cmp, n_slc)

    mask = sel[:, :, 0, :n_slc] > 0.5
    idx = jnp.sort(jnp.where(mask, jnp.arange(n_slc, dtype=jnp.int32), jnp.int32(1 << 20)), axis=-1)
    n_gather = SLC_TOPN - 1
    idx = idx[..., :n_gather]
    pages_per_block = PAGE_SIZE // SLC_BLOCK
    logical_page = idx // pages_per_block
    pages = jnp.take_along_axis(page_table[:, None, :], logical_page, axis=2).reshape(-1).astype(jnp.int32)
    halves = (idx % pages_per_block).reshape(-1).astype(jnp.int32)
    kpos = (logical_page[..., None] * PAGE_SIZE + jnp.arange(PAGE_SIZE, dtype=jnp.int32)).reshape(db, NSA_KV_HEADS, -1)
    hh = jnp.arange(NSA_KV_HEADS)[None, :, None, None]
    gg = jnp.arange(NSA_GROUP)[None, None, :, None]
    bias_sel = bd[hh, gg, (past - kpos)[:, :, None, :]]
    bias_sel = jnp.concatenate([bias_sel, jnp.broadcast_to(bd[None, :, :, 0:1], (db, NSA_KV_HEADS, NSA_GROUP, 1)),
                                jnp.zeros((db, NSA_KV_HEADS, NSA_GROUP, PAGE_SIZE - 1), F32)], axis=-1)
    bias_sel = jnp.concatenate([bias_sel, jnp.zeros_like(bias_sel)], axis=2)
    wlen = cache_win.shape[1]
    dist_w = np.clip(wlen - np.arange(wlen + PAGE_SIZE), 0, None)
    kvs_new = _kv_from_per_head(u[:db, COL_KVS:COL_KVS + 2 * D_KV])
    kvw_new = _kv_from_per_head(u[:db, COL_KVW:COL_KVW + 2 * D_KV])
    lane_pad = lambda a: jnp.pad(a[:, :, None], ((0, 0), (0, 0), (0, PAGE_SIZE - 1)))
    slc_t = cache_slc.transpose(0, 2, 3, 4, 1).reshape(n_pool * 2 * D_KV, PAGE_SIZE)
    win_t = cache_win.transpose(0, 2, 3, 4, 1).reshape(db, 2 * D_KV, wlen)
    o_s8, o_w8 = _sample_attn(pages, halves, q8, slc_t, lane_pad(kvs_new), bias_sel, win_t, lane_pad(kvw_new),
                              pad_g(bd[:, :, dist_w]), n_gather)
    win2d = cache_win.reshape(db, wlen, 2 * D_KV)

    heads = lambda o: _pad_rows(o[:, :, :NSA_GROUP, :].reshape(db, D_NSA), tok)
    y = _outproj(_pad_rows(y_ml, tok), heads(o_c8), heads(o_s8), heads(o_w8), small, nsa_norm_g, w_out_b, x16, tok, 512)
    kvshape = (1, db, 1, 2, NSA_KV_HEADS, NSA_HEAD_DIM)
    new_win = jnp.concatenate([win2d[:, 1:], kvw_new[:, None, :]], axis=1)
    states = (u[:db, COL_KVC:COL_KVC + 2 * D_KV].reshape(kvshape), kvs_new.reshape(kvshape),
              new_win.reshape((1, db, wlen, 2, NSA_KV_HEADS, NSA_HEAD_DIM)),
              c_n[None], n_n[None], m_n[None], conv_n[None])
    return y, states


def kernel(x_prompt, x_sample, cache_cmp_kv, cache_slc_kv, cache_win_kv, state_mlstm_C, state_mlstm_n,
           state_mlstm_m, state_conv, page_table, rel_bias, norm_mix_g, w_in, b_ig, b_fg, conv_w, ml_norm_g,
           cmp_pos, cmp_w1, cmp_w2, nsa_norm_g, w_out, norm_ffn_g, w_router_grp, b_router_grp, w_router_exp,
           b_router_exp, w_gate, w_up, w_down, norm_final_g):
    B, S, D = x_prompt.shape
    wb = _reorder_w_in(w_in[0])
    gate_bias = jnp.zeros((1, LANE), F32).at[0, 0:ML_HEADS].set(b_ig[0]).at[0, ML_HEADS:2 * ML_HEADS].set(b_fg[0])
    w_out_b = w_out[0].astype(BF16)
    yp, st_p = _prompt_mixer(x_prompt, norm_mix_g, wb, gate_bias, conv_w[0], ml_norm_g, cmp_pos[0], cmp_w1[0],
                             cmp_w2[0], nsa_norm_g, w_out_b, rel_bias)
    ys, st_s = _sample_mixer(x_sample, cache_cmp_kv[0], cache_slc_kv[0], cache_win_kv[0], state_mlstm_C[0],
                             state_mlstm_n[0], state_mlstm_m[0], state_conv[0], page_table, norm_mix_g, wb, gate_bias,
                             conv_w[0], ml_norm_g, cmp_pos[0], cmp_w1[0], cmp_w2[0], nsa_norm_g, w_out_b, rel_bias)
    moe_w = (norm_ffn_g, w_router_grp[0], b_router_grp[0], w_router_exp[0], b_router_exp[0],
             w_gate[0], w_up[0], w_down[0], norm_final_g[None, :])
    DB, L, _ = x_sample.shape
    out_p = _moe_final(yp.reshape(B * S, D), *moe_w, 256, 128).reshape(B, S, D)
    out_s = _moe_final(_pad_rows(ys, 128), *moe_w, 128, 16)[:DB].reshape(DB, L, D)
    outs = [out_p, out_s]
    for a, b in zip(st_p, st_s):
        outs += [a, b]
    return tuple(outs)


def _prompt_mixer(x_prompt, norm_g, wb, gate_bias, conv_w, ml_norm_g, cmp_pos, cmp_w1, cmp_w2, nsa_norm_g,
                  w_out_b, rel_bias):
    B, S, D = x_prompt.shape
    x2d = x_prompt.reshape(B * S, D)
    tm = min(1024, B * S)
    u = _proj(x2d, norm_g, wb, tm, 512)
    small = u[:, COL_SMALL:COL_SMALL + LANE]
    y_ml, conv_n, c_n, n_n, m_n = _mlstm(
        u, small, jnp.zeros((B, CONV_W - 1, 2 * D_ML), F32),
        jnp.zeros((B, ML_HEADS, ML_HEAD_DIM, ML_HEAD_DIM), F32), jnp.zeros((B, ML_HEADS, ML_HEAD_DIM), F32),
        jnp.full((B, ML_HEADS), -jnp.inf, F32), conv_w, gate_bias, ml_norm_g, B, S, 256)
    kv_c = u[:, COL_KVC:COL_KVC + 2 * D_KV]
    kv_s = _kv_from_per_head(u[:, COL_KVS:COL_KVS + 2 * D_KV])
    kv_w = _kv_from_per_head(u[:, COL_KVW:COL_KVW + 2 * D_KV])
    nch = S // CMP_STRIDE
    kvc = _compress(u, COL_KVC // (2 * D_KV), B, S, cmp_pos, cmp_w1, cmp_w2, min(S, 4096))
    tq_c = min(S, 256)
    o_c, sel_t = _cattn(u, kvc, rel_bias, B, S, tq_c)
    o_s = _flash(u, COL_KVS, rel_bias, sel_t, B, S)
    o_w = _flash(u, COL_KVW, rel_bias, None, B, S)
    y = _outproj(y_ml, o_c, o_s, o_w, small, nsa_norm_g, w_out_b, x2d, tm, 512)
    kvshape = (1, B, S, 2, NSA_KV_HEADS, NSA_HEAD_DIM)
    win = min(WINDOW, S)
    states = (kv_c.reshape(kvshape), kv_s.reshape(kvshape), kv_w.reshape(kvshape)[:, :, S - win:],
              c_n[None], n_n[None], m_n[None], conv_n[None])
    return y.reshape(B, S, D), states
```

```python
import functools
import math

import numpy as np
import jax
import jax.numpy as jnp
from jax import lax
from jax.experimental import pallas as pl
from jax.experimental.pallas import tpu as pltpu

F32 = jnp.float32
BF16 = jnp.bfloat16

D_MODEL = 2048
ML_HEADS = 4
ML_HEAD_DIM = 256
D_ML = 1024
CONV_W = 4
NSA_HEADS = 16
NSA_HEAD_DIM = 64
D_NSA = 1024
NSA_KV_HEADS = 4
NSA_GROUP = 4
D_KV = 256
CMP_BLOCK = 32
CMP_STRIDE = 16
CMP_HIDDEN = 256
SLC_BLOCK = 64
SLC_TOPN = 16
WINDOW = 512
N_BUCKETS = 32
MAX_DISTANCE = 2048
N_GROUPS = 4
EXPERTS_PER_GROUP = 8
N_EXPERTS = 32
D_EXPERT = 512
PAGE_SIZE = 128
EPS = 1e-6
NEG_BIG = -1e30
FORCE_SCORE = 1e4

LANE = 128
COL_QML, COL_KML, COL_VML, COL_OML = 0, 1024, 2048, 3072
COL_QNSA = 4096
COL_KVC, COL_KVS, COL_KVW = 5120, 5632, 6144
COL_SMALL = 6656
N_PROJ = 7168
VMEM_LIMIT = 56 * 1024 * 1024


def _cparams(sem, vmem=VMEM_LIMIT):
    return pltpu.CompilerParams(dimension_semantics=sem, vmem_limit_bytes=vmem)


def _split2(x):
    hi = x.astype(BF16)
    lo = (x - hi.astype(F32)).astype(BF16)
    return hi, lo


def _split3(x):
    hi = x.astype(BF16)
    r = x - hi.astype(F32)
    mid = r.astype(BF16)
    lo = (r - mid.astype(F32)).astype(BF16)
    return hi, mid, lo


def _dot(a, b):
    return jnp.dot(a, b, preferred_element_type=F32)


def _dot_nt(a, b):
    return lax.dot_general(a, b, (((1,), (1,)), ((), ())), preferred_element_type=F32)


def _dot_tn(a, b):
    return lax.dot_general(a, b, (((0,), (0,)), ((), ())), preferred_element_type=F32)


def _proj_kernel(x_ref, g_ref, w_ref, o_ref, *rest, kv_blocks):
    h_scr = rest[-1]
    j = pl.program_id(1)

    @pl.when(j == 0)
    def _():
        x = x_ref[...]
        ms = jnp.mean(x * x, axis=-1, keepdims=True)
        h_scr[...] = (x * lax.rsqrt(ms + EPS) * g_ref[...]).astype(BF16)

    res = _dot(h_scr[...], w_ref[...])
    o_ref[...] = res

    for (block, per_head), t_ref in zip(kv_blocks, rest[:-1]):
        @pl.when(j == block)
        def _():
            res_t = res.T
            for c in range(2):
                for hh in range(NSA_KV_HEADS):
                    piece = hh * 2 + c if per_head else c * NSA_KV_HEADS + hh
                    t_ref[0, c, hh] = res_t[piece * NSA_HEAD_DIM:(piece + 1) * NSA_HEAD_DIM, :]


def _proj(x2d, g, wb, tm, tn, seq=None):
    n, d = x2d.shape
    nc = wb.shape[1]
    kv_blocks = ()
    out_specs = [pl.BlockSpec((tm, tn), lambda i, j: (i, j))]
    out_shape = [jax.ShapeDtypeStruct((n, nc), F32)]
    if seq is not None:
        assert tn == 2 * D_KV and seq % tm == 0
        kv_blocks = ((COL_KVC // tn, False), (COL_KVS // tn, True), (COL_KVW // tn, True))
        per_seq = seq // tm
        for _ in kv_blocks:
            out_specs.append(pl.BlockSpec((1, 2, NSA_KV_HEADS, NSA_HEAD_DIM, tm),
                                          lambda i, j: (i // per_seq, 0, 0, 0, i % per_seq)))
            out_shape.append(jax.ShapeDtypeStruct((n // seq, 2, NSA_KV_HEADS, NSA_HEAD_DIM, seq), F32))
    outs = pl.pallas_call(
        functools.partial(_proj_kernel, kv_blocks=kv_blocks),
        grid=(n // tm, nc // tn),
        in_specs=[pl.BlockSpec((tm, d), lambda i, j: (i, 0)),
                  pl.BlockSpec((1, d), lambda i, j: (0, 0)),
                  pl.BlockSpec((d, tn), lambda i, j: (0, j))],
        out_specs=out_specs,
        out_shape=out_shape,
        scratch_shapes=[pltpu.VMEM((tm, d), BF16)],
        compiler_params=_cparams(("arbitrary", "arbitrary")),
        name="proj",
    )(x2d, g, wb)
    return outs if seq is not None else outs[0]


def _reorder_w_in(w_in):
    big = w_in[:, :4 * D_ML]
    small_a = w_in[:, 4 * D_ML:4 * D_ML + 2 * ML_HEADS]
    rest = w_in[:, 4 * D_ML + 2 * ML_HEADS:]
    q_and_cmp = rest[:, :D_NSA + 2 * D_KV]
    gate = rest[:, D_NSA + 6 * D_KV:]
    d = w_in.shape[0]

    def per_head(w):
        return w.reshape(d, 2, NSA_KV_HEADS, NSA_HEAD_DIM).transpose(0, 2, 1, 3).reshape(d, 2 * D_KV)

    kv_s = per_head(rest[:, D_NSA + 2 * D_KV:D_NSA + 4 * D_KV])
    kv_w = per_head(rest[:, D_NSA + 4 * D_KV:D_NSA + 6 * D_KV])
    pad = jnp.zeros((d, N_PROJ - COL_SMALL - 2 * ML_HEADS - 3 * NSA_HEADS), w_in.dtype)
    return jnp.concatenate([big, q_and_cmp, kv_s, kv_w, small_a, gate, pad], axis=1).astype(BF16)


def _kv_from_per_head(kv2d):
    n = kv2d.shape[0]
    return kv2d.reshape(n, NSA_KV_HEADS, 2, NSA_HEAD_DIM).transpose(0, 2, 1, 3).reshape(n, 2 * D_KV)


def _log_sigmoid(x):
    return jnp.minimum(x, 0.0) - jnp.log1p(jnp.exp(-jnp.abs(x)))


def _mlstm_kernel(q_ref, k_ref, v_ref, o_ref, s_ref, cb_ref, c0_ref, n0_ref, m0_ref,
                  cw_ref, gb_ref, ng_ref,
                  y_ref, cbo_ref, co_ref, no_ref, mo_ref,
                  ext_scr, c_scr, n_scr, m_scr, *, T):
    c = pl.program_id(1)
    nc = pl.num_programs(1)

    @pl.when(c == 0)
    def _():
        ext_scr[0:8, :] = jnp.zeros((8, 2 * D_ML), F32)
        ext_scr[5:8, :] = cb_ref[0]
        c_scr[...] = c0_ref[0]
        n_scr[...] = n0_ref[0]
        m_scr[...] = m0_ref[0]

    ext_scr[8:8 + T, 0:D_ML] = q_ref[...]
    ext_scr[8:8 + T, D_ML:2 * D_ML] = k_ref[...]
    conv = ext_scr[5:5 + T, :] * cw_ref[0:1, :]
    for j in range(1, CONV_W):
        conv = conv + ext_scr[5 + j:5 + j + T, :] * cw_ref[j:j + 1, :]
    tail = ext_scr[8 + T - 3:8 + T, :]
    ext_scr[5:8, :] = tail
    cbo_ref[0] = tail
    qk = conv * jax.nn.sigmoid(conv)

    pre = s_ref[...] + gb_ref[...]
    col = lax.broadcasted_iota(jnp.int32, pre.shape, 1)
    padrow = s_ref[:, LANE - 1:LANE] > 0.5
    gates = jnp.where(col < ML_HEADS, pre, _log_sigmoid(pre))
    gates = jnp.where(padrow, jnp.where(col < ML_HEADS, NEG_BIG, 0.0), gates)
    g_r = gates.T
    ti = lax.broadcasted_iota(jnp.int32, (T, T), 0)
    si = lax.broadcasted_iota(jnp.int32, (T, T), 1)
    upper = (ti <= si).astype(BF16)
    g_fin = jnp.where(lax.broadcasted_iota(jnp.int32, g_r.shape, 0) < ML_HEADS, 0.0, g_r)
    hi, mid, lo = _split3(g_fin)
    cum_r = _dot(hi, upper) + _dot(mid, upper) + _dot(lo, upper)
    rowi = lax.broadcasted_iota(jnp.int32, g_r.shape, 0)
    a_r = jnp.where(rowi < ML_HEADS, g_r, cum_r)
    a_c = a_r.T
    causal = si <= ti

    for h in range(ML_HEADS):
        sl = slice(h * ML_HEAD_DIM, (h + 1) * ML_HEAD_DIM)
        q = qk[:, h * ML_HEAD_DIM:(h + 1) * ML_HEAD_DIM]
        k = qk[:, D_ML + h * ML_HEAD_DIM:D_ML + (h + 1) * ML_HEAD_DIM] * (ML_HEAD_DIM ** -0.5)
        v = v_ref[:, sl]
        ig_r = a_r[h:h + 1, :]
        b_r = a_r[ML_HEADS + h:ML_HEADS + h + 1, :]
        ig_c = a_c[:, h:h + 1]
        b_c = a_c[:, ML_HEADS + h:ML_HEADS + h + 1]
        m_prev = m_scr[h:h + 1, 0:1]
        logd = jnp.where(causal, b_c - b_r + ig_r, -jnp.inf)
        inter = b_c + m_prev
        m_t = jnp.maximum(inter, jnp.max(logd, axis=1, keepdims=True))
        w_intra = jnp.exp(logd - m_t)
        w_inter = jnp.exp(inter - m_t)
        qb = q.astype(BF16)
        kb = k.astype(BF16)
        vb = v.astype(BF16)
        sc = _dot_nt(qb, kb) * w_intra
        cmat = c_scr[h]
        nvec = n_scr[h:h + 1, :]
        num = _dot(sc.astype(BF16), vb) + w_inter * _dot(qb, cmat.astype(BF16))
        qn = jnp.sum(qb.astype(F32) * nvec.astype(BF16).astype(F32), axis=1, keepdims=True)
        den = jnp.sum(sc, axis=1, keepdims=True) + w_inter * qn
        hh = num / jnp.maximum(jnp.abs(den), jnp.exp(-m_t))
        m_new = m_t[T - 1:T, :]
        b_last = b_c[T - 1:T, :]
        w_s = jnp.exp(b_last - b_c + ig_c - m_new)
        decay = jnp.exp(b_last + m_prev - m_new)
        kw = k * w_s
        c_new = decay * cmat + _dot_tn(kw.astype(BF16), vb)
        n_new = decay * nvec + jnp.sum(kw, axis=0, keepdims=True)
        c_scr[h] = c_new
        n_scr[h:h + 1, :] = n_new
        m_scr[h:h + 1, :] = jnp.broadcast_to(m_new, (1, LANE))
        hn = hh * lax.rsqrt(jnp.mean(hh * hh, axis=1, keepdims=True) + EPS) * ng_ref[:, sl]
        y_ref[:, sl] = (hn * jax.nn.sigmoid(o_ref[:, sl])).astype(y_ref.dtype)

    @pl.when(c == nc - 1)
    def _():
        co_ref[0] = c_scr[...]
        no_ref[0] = n_scr[...]
        mo_ref[0] = m_scr[...]


def _mlstm(u, small, conv_buf, c0, n0, m0, conv_w, gate_bias, norm_g, batch, seq, T):
    nc = seq // T
    cb = D_ML // 1024
    m0b = jnp.broadcast_to(m0[:, :, None], (batch, ML_HEADS, LANE))
    m0b = jnp.concatenate([m0b, jnp.zeros((batch, 8 - ML_HEADS, LANE), F32)], axis=1)
    n0p = jnp.concatenate([n0, jnp.zeros((batch, 8 - ML_HEADS, ML_HEAD_DIM), F32)], axis=1)
    row = lambda b, c: (b * nc + c, 0)
    outs = pl.pallas_call(
        functools.partial(_mlstm_kernel, T=T),
        grid=(batch, nc),
        in_specs=[pl.BlockSpec((T, D_ML), lambda b, c: (b * nc + c, COL_QML // D_ML)),
                  pl.BlockSpec((T, D_ML), lambda b, c: (b * nc + c, COL_KML // D_ML)),
                  pl.BlockSpec((T, D_ML), lambda b, c: (b * nc + c, COL_VML // D_ML)),
                  pl.BlockSpec((T, D_ML), lambda b, c: (b * nc + c, COL_OML // D_ML)),
                  pl.BlockSpec((T, LANE), row),
                  pl.BlockSpec((1, CONV_W - 1, 2 * D_ML), lambda b, c: (b, 0, 0)),
                  pl.BlockSpec((1, ML_HEADS, ML_HEAD_DIM, ML_HEAD_DIM), lambda b, c: (b, 0, 0, 0)),
                  pl.BlockSpec((1, 8, ML_HEAD_DIM), lambda b, c: (b, 0, 0)),
                  pl.BlockSpec((1, 8, LANE), lambda b, c: (b, 0, 0)),
                  pl.BlockSpec((CONV_W, 2 * D_ML), lambda b, c: (0, 0)),
                  pl.BlockSpec((1, LANE), lambda b, c: (0, 0)),
                  pl.BlockSpec((1, D_ML), lambda b, c: (0, 0))],
        out_specs=[pl.BlockSpec((T, D_ML), row),
                   pl.BlockSpec((1, CONV_W - 1, 2 * D_ML), lambda b, c: (b, 0, 0)),
                   pl.BlockSpec((1, ML_HEADS, ML_HEAD_DIM, ML_HEAD_DIM), lambda b, c: (b, 0, 0, 0)),
                   pl.BlockSpec((1, 8, ML_HEAD_DIM), lambda b, c: (b, 0, 0)),
                   pl.BlockSpec((1, 8, LANE), lambda b, c: (b, 0, 0))],
        out_shape=[jax.ShapeDtypeStruct((batch * seq, D_ML), BF16),
                   jax.ShapeDtypeStruct((batch, CONV_W - 1, 2 * D_ML), F32),
                   jax.ShapeDtypeStruct((batch, ML_HEADS, ML_HEAD_DIM, ML_HEAD_DIM), F32),
                   jax.ShapeDtypeStruct((batch, 8, ML_HEAD_DIM), F32),
                   jax.ShapeDtypeStruct((batch, 8, LANE), F32)],
        scratch_shapes=[pltpu.VMEM((8 + T, 2 * D_ML), F32),
                        pltpu.VMEM((ML_HEADS, ML_HEAD_DIM, ML_HEAD_DIM), F32),
                        pltpu.VMEM((8, ML_HEAD_DIM), F32),
                        pltpu.VMEM((8, LANE), F32)],
        compiler_params=_cparams(("arbitrary", "arbitrary")),
        name="mlstm",
    )(u, u, u, u, small, conv_buf, c0, n0p, m0b, conv_w, gate_bias, norm_g)
    y, cbo, co, no, mo = outs
    return y, cbo, co, no[:, :ML_HEADS], mo[:, :ML_HEADS, 0]


def _bucket_np(dist):
    n = np.maximum(dist, 0)
    max_exact = N_BUCKETS // 2
    nf = np.maximum(n, 1).astype(np.float64)
    large = max_exact + (np.log(nf / max_exact) / math.log(MAX_DISTANCE / max_exact)
                         * (N_BUCKETS - max_exact)).astype(np.int64)
    return np.where(n < max_exact, n, np.minimum(large, N_BUCKETS - 1)).astype(np.int32)


def _bias_by_distance(rel_bias, n):
    return rel_bias.astype(F32)[_bucket_np(np.arange(n))].T


def _overlap_t(n_cmp, nch, n_slc):
    c0 = np.arange(nch) * CMP_STRIDE
    s0 = np.arange(n_slc) * SLC_BLOCK
    ov = np.minimum(c0[None, :] + CMP_BLOCK, s0[:, None] + SLC_BLOCK) - np.maximum(c0[None, :], s0[:, None])
    ov = np.clip(ov, 0, None).astype(np.float32) / CMP_BLOCK
    ov[:, n_cmp:] = 0.0
    return jnp.asarray(ov, BF16)


def _pq_kernel(x0_ref, x1_ref, x2_ref, x3_ref, pos_ref, w_ref, o_ref, *, rows):
    for pair, x_ref in enumerate((x0_ref, x1_ref, x2_ref, x3_ref)):
        toks = [x_ref[pl.ds(l, rows, stride=CMP_STRIDE), :] for l in range(CMP_STRIDE)]
        for half in range(2):
            ch = 2 * pair + half
            c = ch // NSA_KV_HEADS
            lanes = slice(half * NSA_HEAD_DIM, (half + 1) * NSA_HEAD_DIM)
            chunk = jnp.concatenate([t[:, lanes] for t in toks], axis=1)
            for part in range(2):
                lhs = (chunk + pos_ref[c, part:part + 1, :]).astype(BF16)
                o_ref[0, ch, :, part * CMP_HIDDEN:(part + 1) * CMP_HIDDEN] = _dot(lhs, w_ref[c, part])


def _hid_kernel(pq_ref, w2_ref, o_ref, *, nch):
    p = pq_ref[0, 0, :, 0:CMP_HIDDEN]
    q = pltpu.roll(pq_ref[0, 0, :, CMP_HIDDEN:2 * CMP_HIDDEN], nch - 1, 0)
    hid = jax.nn.gelu(p + q, approximate=True)
    o_ref[0, 0] = _dot(hid.astype(BF16), w2_ref[0])


def _compress(kv2d, col_block, bsz, seq, cmp_pos, cmp_w1, cmp_w2, tt):
    nch = seq // CMP_STRIDE
    kdim = CMP_STRIDE * NSA_HEAD_DIM
    nt = seq // tt
    tn = tt // CMP_STRIDE
    w1 = cmp_w1.reshape(2, 2, kdim, CMP_HIDDEN).astype(BF16)
    pos = cmp_pos.reshape(2, 2, kdim)
    pq = pl.pallas_call(
        functools.partial(_pq_kernel, rows=tn),
        grid=(bsz, nt),
        in_specs=[pl.BlockSpec((tt, LANE), functools.partial(lambda b, i, k: (b * nt + i, col_block * 4 + k), k=k))
                  for k in range(4)] +
                 [pl.BlockSpec((2, 2, kdim), lambda b, i: (0, 0, 0)),
                  pl.BlockSpec((2, 2, kdim, CMP_HIDDEN), lambda b, i: (0, 0, 0, 0))],
        out_specs=pl.BlockSpec((1, 8, tn, 2 * CMP_HIDDEN), lambda b, i: (b, 0, i, 0)),
        out_shape=jax.ShapeDtypeStruct((bsz, 8, nch, 2 * CMP_HIDDEN), F32),
        compiler_params=_cparams(("arbitrary", "arbitrary")),
        name="cmp_pq",
    )(kv2d, kv2d, kv2d, kv2d, pos, w1)
    return _cmp_hid(pq, cmp_w2)


def _cmp_hid(pq, cmp_w2):
    bsz, _, nch, _ = pq.shape
    return pl.pallas_call(
        functools.partial(_hid_kernel, nch=nch),
        grid=(bsz, 8),
        in_specs=[pl.BlockSpec((1, 1, nch, 2 * CMP_HIDDEN), lambda b, c: (b, c, 0, 0)),
                  pl.BlockSpec((1, CMP_HIDDEN, NSA_HEAD_DIM), lambda b, c: (c // NSA_KV_HEADS, 0, 0))],
        out_specs=pl.BlockSpec((1, 1, nch, NSA_HEAD_DIM), lambda b, c: (b, c, 0, 0)),
        out_shape=jax.ShapeDtypeStruct((bsz, 8, nch, NSA_HEAD_DIM), F32),
        compiler_params=_cparams(("arbitrary", "arbitrary")),
        name="cmp_hid",
    )(pq, cmp_w2.astype(BF16))


def _top_n_mask(score_t, blk, n_rows):
    rank = jnp.zeros(score_t.shape, jnp.int32)
    for i in range(n_rows):
        row = score_t[i:i + 1, :]
        beats = (row > score_t) | ((row == score_t) & (blk > i))
        rank = rank + beats.astype(jnp.int32)
    return (rank < min(SLC_TOPN, n_rows)).astype(F32)


def _bias_from_buckets(bucket, table_ref, first_head):
    biases = [jnp.zeros(bucket.shape, F32) for _ in range(NSA_GROUP)]
    for k in range(N_BUCKETS):
        hit = bucket == k
        biases = [jnp.where(hit, table_ref[k, first_head + g], biases[g]) for g in range(NSA_GROUP)]
    return biases


def _cattn_kernel(tab_ref, q_ref, kc_ref, vc_ref, bk_ref, ovt_ref, o_ref, sel_ref, bias_scr, strip_scr,
                  *, tq, nch, n_cmp, n_slc):
    h = pl.program_id(0)
    i = pl.program_id(1)

    @pl.when(pl.program_id(2) == 0)
    def _():
        bands = tq // CMP_STRIDE
        for g, strip in enumerate(_bias_from_buckets(bk_ref[0], tab_ref, h * NSA_GROUP)):
            strip_scr[g] = strip
            for a in range(bands):
                off = bands - 1 - a
                bias_scr[g, a * CMP_STRIDE:(a + 1) * CMP_STRIDE, :] = strip_scr[g, :, off:off + nch]

    kc = kc_ref[0, 0].astype(BF16)
    vc = vc_ref[0, 0].astype(BF16)
    t = i * tq + lax.broadcasted_iota(jnp.int32, (tq, nch), 0)
    n = lax.broadcasted_iota(jnp.int32, (tq, nch), 1)
    valid = (t - CMP_STRIDE * n - (CMP_BLOCK - 1) >= 0) & (n < n_cmp)
    pc = jnp.zeros((tq, nch), F32)
    for g in range(NSA_GROUP):
        sl = slice(g * NSA_HEAD_DIM, (g + 1) * NSA_HEAD_DIM)
        qg = (q_ref[:, sl] * (NSA_HEAD_DIM ** -0.5)).astype(BF16)
        s = _dot_nt(qg, kc) + bias_scr[g]
        s = jnp.where(valid, s, NEG_BIG)
        e = jnp.exp(s - jnp.max(s, axis=1, keepdims=True))
        p = jnp.where(valid, e / jnp.sum(e, axis=1, keepdims=True), 0.0)
        o_ref[:, sl] = _dot(p.astype(BF16), vc)
        pc = pc + p
    score_t = _dot_nt(ovt_ref[...], pc.astype(BF16))
    blk = lax.broadcasted_iota(jnp.int32, (n_slc, tq), 0)
    cur = (i * tq + lax.broadcasted_iota(jnp.int32, (n_slc, tq), 1)) // SLC_BLOCK
    forced = (blk == 0) | (blk == cur) | (blk == cur - 1)
    score_t = jnp.where(forced, FORCE_SCORE, score_t)
    score_t = jnp.where(blk > cur, -FORCE_SCORE, score_t)
    sel_ref[0, 0] = _top_n_mask(score_t, blk, n_slc)


def _cattn(u, kvc, rel_bias, batch, seq, tq):
    nch = kvc.shape[2]
    n_cmp = nch - 1
    n_slc = seq // SLC_BLOCK
    ni = seq // tq
    ovt = _overlap_t(n_cmp, nch, n_slc)
    qcol = COL_QNSA // (NSA_GROUP * NSA_HEAD_DIM)
    bands = tq // CMP_STRIDE
    strip_w = -(-(nch + bands - 1) // LANE) * LANE
    dist = (tq * np.arange(ni)[:, None, None] + np.arange(CMP_STRIDE)[None, :, None]
            - CMP_STRIDE * (np.arange(strip_w)[None, None, :] - (bands - 1)) - (CMP_BLOCK - 1))
    buckets = jnp.asarray(_bucket_np(dist))
    return pl.pallas_call(
        functools.partial(_cattn_kernel, tq=tq, nch=nch, n_cmp=n_cmp, n_slc=n_slc),
        grid=(NSA_KV_HEADS, ni, batch),
        in_specs=[pl.BlockSpec(memory_space=pltpu.SMEM),
                  pl.BlockSpec((tq, NSA_GROUP * NSA_HEAD_DIM), lambda h, i, b: (b * ni + i, qcol + h)),
                  pl.BlockSpec((1, 1, nch, NSA_HEAD_DIM), lambda h, i, b: (b, h, 0, 0)),
                  pl.BlockSpec((1, 1, nch, NSA_HEAD_DIM), lambda h, i, b: (b, NSA_KV_HEADS + h, 0, 0)),
                  pl.BlockSpec((1, CMP_STRIDE, strip_w), lambda h, i, b: (i, 0, 0)),
                  pl.BlockSpec((n_slc, nch), lambda h, i, b: (0, 0))],
        out_specs=[pl.BlockSpec((tq, NSA_GROUP * NSA_HEAD_DIM), lambda h, i, b: (b * ni + i, h)),
                   pl.BlockSpec((1, 1, n_slc, tq), lambda h, i, b: (b, h, 0, i))],
        out_shape=[jax.ShapeDtypeStruct((batch * seq, D_NSA), F32),
                   jax.ShapeDtypeStruct((batch, NSA_KV_HEADS, n_slc, seq), F32)],
        scratch_shapes=[pltpu.VMEM((NSA_GROUP, tq, nch), F32), pltpu.VMEM((NSA_GROUP, CMP_STRIDE, strip_w), F32)],
        compiler_params=_cparams(("arbitrary", "arbitrary", "arbitrary")),
        name="cmp_attn",
    )(rel_bias.astype(F32), u, kvc, kvc, buckets, ovt)


TQ = 128
FLASH_GROUP = 8


def _flash_kernel(tab_ref, q_ref, kv_ref, bk_ref, *rest, selected, n_delta):
    if selected:
        sel_ref, o_ref, band_scr, qs_scr, m_scr, l_scr, acc_scr = rest
    else:
        o_ref, band_scr, qs_scr, m_scr, l_scr, acc_scr = rest
    h = pl.program_id(0)
    i = pl.program_id(2)

    @pl.when((pl.program_id(1) == 0) & (i == 0))
    def _():
        def fill(d, carry):
            for g, bias in enumerate(_bias_from_buckets(bk_ref[d], tab_ref, h * NSA_GROUP)):
                band_scr[d, :, g * TQ:(g + 1) * TQ] = bias
            return carry
        lax.fori_loop(0, n_delta, fill, 0)

    for g in range(NSA_GROUP):
        qg = (q_ref[:, g * NSA_HEAD_DIM:(g + 1) * NSA_HEAD_DIM] * (NSA_HEAD_DIM ** -0.5)).astype(BF16)
        qs_scr[g * TQ:(g + 1) * TQ, :] = jnp.concatenate([qg, jnp.zeros_like(qg)], axis=1)
    is_key_lane = lax.broadcasted_iota(jnp.int32, (TQ, LANE), 1) < NSA_HEAD_DIM
    m_scr[...] = jnp.full(m_scr.shape, 0.5 * NEG_BIG, F32)
    l_scr[...] = jnp.zeros(l_scr.shape, F32)
    acc_scr[...] = jnp.zeros(acc_scr.shape, F32)
    key = lax.broadcasted_iota(jnp.int32, (TQ, TQ), 0)
    qry = lax.broadcasted_iota(jnp.int32, (TQ, TQ), 1)
    n_back = WINDOW // TQ

    def scores(j, kind):
        kvj = kv_ref[pl.ds(pl.multiple_of(j * TQ, TQ), TQ), :].astype(BF16)
        s = _dot_nt(kvj, qs_scr[...]) + band_scr[i - j]
        mask = None
        if kind == "diag":
            mask = qry >= key
        elif kind == "far":
            mask = qry < key
        if selected:
            r = sel_ref[0, 0, pl.ds(2 * j, 2), :]
            picked = jnp.where(key < SLC_BLOCK, r[0:1, :], r[1:2, :]) > 0.5
            mask = picked if mask is None else (mask & picked)
        if mask is not None:
            s = jnp.concatenate([jnp.where(mask, s[:, g * TQ:(g + 1) * TQ], NEG_BIG) for g in range(NSA_GROUP)], axis=1)
        return s, ones_and_values(j)

    def ones_and_values(j):
        kvj = kv_ref[pl.ds(pl.multiple_of(j * TQ, TQ), TQ), :].astype(BF16)
        return jnp.where(is_key_lane, jnp.ones_like(kvj), kvj)

    def update(tiles):
        m_old = m_scr[...]
        m_new = m_old
        for s, _ in tiles:
            m_new = jnp.maximum(m_new, jnp.max(s, axis=0, keepdims=True))
        alpha = jnp.exp(m_old - m_new)
        l_new = alpha * l_scr[...]
        acc = alpha * acc_scr[...]
        for s, ones_v in tiles:
            p = jnp.exp(s - m_new)
            pv = _dot_tn(ones_v, p.astype(BF16))
            l_new = l_new + pv[0:1, :]
            acc = acc + pv[NSA_HEAD_DIM:2 * NSA_HEAD_DIM, :]
        m_scr[...] = m_new
        l_scr[...] = l_new
        acc_scr[...] = acc

    def full_tiles(first, count):
        return [scores(first + k, "full") for k in range(count)]

    if selected:
        def group(t, carry):
            update(full_tiles(FLASH_GROUP * t, FLASH_GROUP))
            return carry

        n_groups = i // FLASH_GROUP
        lax.fori_loop(0, n_groups, group, 0)
        for rem in range(FLASH_GROUP):
            @pl.when(i - n_groups * FLASH_GROUP == rem)
            def _():
                update(full_tiles(n_groups * FLASH_GROUP, rem) + [scores(i, "diag")])
    else:
        @pl.when(i >= n_back)
        def _():
            update([scores(i - n_back, "far")] + full_tiles(i - n_back + 1, n_back - 1) + [scores(i, "diag")])

        for rem in range(n_back):
            @pl.when(i == rem)
            def _():
                update(full_tiles(0, rem) + [scores(i, "diag")])

    for g in range(NSA_GROUP):
        cols = slice(g * TQ, (g + 1) * TQ)
        o_ref[:, g * NSA_HEAD_DIM:(g + 1) * NSA_HEAD_DIM] = (acc_scr[:, cols] / l_scr[:, cols]).T


def _flash(u, kv_col, rel_bias, sel_t, batch, seq):
    ni = seq // TQ
    qcol = COL_QNSA // (NSA_GROUP * NSA_HEAD_DIM)
    selected = sel_t is not None
    n_delta = ni if selected else WINDOW // TQ + 1
    kvblk = kv_col // LANE
    delta = np.arange(n_delta)[:, None, None] * TQ + np.arange(TQ)[None, None, :] - np.arange(TQ)[None, :, None]
    buckets = jnp.asarray(_bucket_np(delta))
    in_specs = [pl.BlockSpec(memory_space=pltpu.SMEM),
                pl.BlockSpec((TQ, NSA_GROUP * NSA_HEAD_DIM), lambda h, b, i: (b * ni + i, qcol + h)),
                pl.BlockSpec((seq, LANE), lambda h, b, i: (b, kvblk + h)),
                pl.BlockSpec((n_delta, TQ, TQ), lambda h, b, i: (0, 0, 0))]
    args = [rel_bias.astype(F32), u, u, buckets]
    if selected:
        n_slc = sel_t.shape[2]
        in_specs.append(pl.BlockSpec((1, 1, n_slc, TQ), lambda h, b, i: (b, h, 0, i)))
        args.append(sel_t)
    scratch = [pltpu.VMEM((n_delta, TQ, NSA_GROUP * TQ), F32),
               pltpu.VMEM((NSA_GROUP * TQ, LANE), BF16),
               pltpu.VMEM((1, NSA_GROUP * TQ), F32),
               pltpu.VMEM((1, NSA_GROUP * TQ), F32),
               pltpu.VMEM((NSA_HEAD_DIM, NSA_GROUP * TQ), F32)]
    return pl.pallas_call(
        functools.partial(_flash_kernel, selected=selected, n_delta=n_delta),
        grid=(NSA_KV_HEADS, batch, ni),
        in_specs=in_specs,
        out_specs=pl.BlockSpec((TQ, NSA_GROUP * NSA_HEAD_DIM), lambda h, b, i: (b * ni + i, h)),
        out_shape=jax.ShapeDtypeStruct((batch * seq, D_NSA), F32),
        scratch_shapes=scratch,
        compiler_params=_cparams(("arbitrary", "arbitrary", "arbitrary")),
        name="slc_attn" if selected else "win_attn",
    )(*args)


def _gate_expand():
    e = np.zeros((3, 2 * LANE, D_NSA), np.float32)
    for br in range(3):
        for hd in range(NSA_HEADS):
            for part in range(2):
                e[br, part * LANE + 2 * ML_HEADS + br * NSA_HEADS + hd,
                  hd * NSA_HEAD_DIM:(hd + 1) * NSA_HEAD_DIM] = 1.0
    return jnp.asarray(e, BF16)


def _outproj_kernel(yml_ref, oc_ref, os_ref, ow_ref, s_ref, e_ref, ng_ref, w_ref, x_ref, o_ref, cat_scr):
    @pl.when(pl.program_id(1) == 0)
    def _():
        sig = jax.nn.sigmoid(s_ref[...])
        hi_lo = jnp.concatenate(_split2(sig), axis=1)
        o = jnp.zeros(oc_ref.shape, F32)
        for br, ref in enumerate((oc_ref, os_ref, ow_ref)):
            o = o + _dot(hi_lo, e_ref[br]) * ref[...]
        o = o * lax.rsqrt(jnp.mean(o * o, axis=-1, keepdims=True) + EPS) * ng_ref[...]
        cat_scr[:, 0:D_ML] = yml_ref[...]
        cat_scr[:, D_ML:D_ML + D_NSA] = o.astype(BF16)

    o_ref[...] = x_ref[...] + _dot(cat_scr[...], w_ref[...])


def _outproj(y_ml, o_c, o_s, o_w, small, nsa_g, w_out_b, x2d, tm, tn):
    n, d = x2d.shape
    rows = lambda i, j: (i, 0)
    return pl.pallas_call(
        _outproj_kernel,
        grid=(n // tm, d // tn),
        in_specs=[pl.BlockSpec((tm, D_ML), rows), pl.BlockSpec((tm, D_NSA), rows),
                  pl.BlockSpec((tm, D_NSA), rows), pl.BlockSpec((tm, D_NSA), rows),
                  pl.BlockSpec((tm, LANE), rows),
                  pl.BlockSpec((3, 2 * LANE, D_NSA), lambda i, j: (0, 0, 0)),
                  pl.BlockSpec((1, D_NSA), lambda i, j: (0, 0)),
                  pl.BlockSpec((D_ML + D_NSA, tn), lambda i, j: (0, j)),
                  pl.BlockSpec((tm, tn), lambda i, j: (i, j))],
        out_specs=pl.BlockSpec((tm, tn), lambda i, j: (i, j)),
        out_shape=jax.ShapeDtypeStruct((n, d), F32),
        scratch_shapes=[pltpu.VMEM((tm, D_ML + D_NSA), BF16)],
        compiler_params=_cparams(("arbitrary", "arbitrary")),
        name="outproj",
    )(y_ml, o_c, o_s, o_w, small, _gate_expand(), nsa_g, w_out_b, x2d)


ROUTE_COL = N_GROUPS
BIG_COL = 1 << 20
SPLIT = D_MODEL // LANE
DMA_UNROLL = 8


def _store_split(ref, val, n):
    for k in range(SPLIT):
        ref[pl.ds(k, n, stride=SPLIT), :] = val[:, k * LANE:(k + 1) * LANE]


def _load_split(ref, n):
    return jnp.concatenate([ref[pl.ds(k, n, stride=SPLIT), :] for k in range(SPLIT)], axis=1)


def _route_kernel(y_ref, g_ref, w_ref, b_ref, h_ref, info_ref, cnt_ref, carry_scr, *, tm):
    @pl.when(pl.program_id(0) == 0)
    def _():
        carry_scr[...] = jnp.zeros(carry_scr.shape, F32)

    x = y_ref[...]
    h = x * lax.rsqrt(jnp.mean(x * x, axis=-1, keepdims=True) + EPS) * g_ref[...]
    _store_split(h_ref, h, tm)
    logit = _dot(h.astype(BF16), w_ref[...]) + b_ref[...]
    col = lax.broadcasted_iota(jnp.int32, logit.shape, 1)
    is_grp = col < N_GROUPS
    gmax = jnp.max(jnp.where(is_grp, logit, -jnp.inf), axis=1, keepdims=True)
    gtop = jnp.min(jnp.where(is_grp & (logit == gmax), col, BIG_COL), axis=1, keepdims=True)
    gsum = jnp.sum(jnp.where(is_grp, jnp.exp(logit - gmax), 0.0), axis=1, keepdims=True)
    first = ROUTE_COL + gtop * EXPERTS_PER_GROUP
    in_grp = (col >= first) & (col < first + EXPERTS_PER_GROUP)
    v1 = jnp.max(jnp.where(in_grp, logit, -jnp.inf), axis=1, keepdims=True)
    i1 = jnp.min(jnp.where(in_grp & (logit == v1), col, BIG_COL), axis=1, keepdims=True)
    rest = in_grp & (col != i1)
    v2 = jnp.max(jnp.where(rest, logit, -jnp.inf), axis=1, keepdims=True)
    i2 = jnp.min(jnp.where(rest & (logit == v2), col, BIG_COL), axis=1, keepdims=True)
    e = jnp.exp(v2 - v1)
    w1 = 1.0 / ((1.0 + e) * gsum)
    w2 = e / ((1.0 + e) * gsum)
    pick1 = col == i1
    pick2 = col == i2
    both = (pick1 | pick2).astype(F32)
    ri = lax.broadcasted_iota(jnp.int32, (tm, tm), 0)
    ci = lax.broadcasted_iota(jnp.int32, (tm, tm), 1)
    before = (ci < ri).astype(BF16)
    cum = _dot(before, both.astype(BF16)) + carry_scr[...]
    r1 = jnp.sum(jnp.where(pick1, cum, 0.0), axis=1, keepdims=True)
    r2 = jnp.sum(jnp.where(pick2, cum, 0.0), axis=1, keepdims=True)
    carry_scr[...] = carry_scr[...] + jnp.sum(both, axis=0, keepdims=True)
    cnt_ref[...] = jnp.broadcast_to(carry_scr[...], cnt_ref.shape)
    info = jnp.where(col == 0, (i1 - ROUTE_COL).astype(F32), 0.0)
    info = jnp.where(col == 1, (i2 - ROUTE_COL).astype(F32), info)
    info = jnp.where(col == 2, w1, info)
    info = jnp.where(col == 3, w2, info)
    info = jnp.where(col == 4, r1, info)
    info_ref[...] = jnp.where(col == 5, r2, info)


def _route(y2d, g, w_rg, b_rg, w_re, b_re, tm):
    n, d = y2d.shape
    wr = jnp.concatenate([w_rg, w_re, jnp.zeros((d, LANE - N_GROUPS - N_EXPERTS), F32)], axis=1).astype(BF16)
    bias = jnp.concatenate([b_rg, b_re, jnp.zeros((LANE - N_GROUPS - N_EXPERTS,), F32)])[None, :]
    rows = lambda i: (i, 0)
    fixed = lambda i: (0, 0)
    return pl.pallas_call(
        functools.partial(_route_kernel, tm=tm),
        grid=(n // tm,),
        in_specs=[pl.BlockSpec((tm, d), rows), pl.BlockSpec((1, d), fixed),
                  pl.BlockSpec((d, LANE), fixed), pl.BlockSpec((1, LANE), fixed)],
        out_specs=[pl.BlockSpec((tm * SPLIT, LANE), rows), pl.BlockSpec((tm, LANE), rows),
                   pl.BlockSpec((8, LANE), fixed)],
        out_shape=[jax.ShapeDtypeStruct((n * SPLIT, LANE), F32), jax.ShapeDtypeStruct((n, LANE), F32),
                   jax.ShapeDtypeStruct((8, LANE), F32)],
        scratch_shapes=[pltpu.VMEM((1, LANE), F32)],
        compiler_params=_cparams(("arbitrary",)),
        name="moe_route",
    )(y2d, g, wr, bias)


W_CHUNKS = 4


def _expert_kernel(te_ref, nu_ref, src_ref, nx_ref, ws_ref, h_hbm, wg_hbm, wu_hbm, wd_hbm, o_ref,
                   xbuf, wg_f, wu_f, wd_f, wg_b, wu_b, wd_b, sem, wsem, *, tme):
    i = pl.program_id(0)
    slot = i % 2

    def weight_copies(expert, wslot):
        copies = []
        for hbm, buf in ((wg_hbm, wg_f), (wu_hbm, wu_f), (wd_hbm, wd_f)):
            step = hbm.shape[1] // W_CHUNKS
            for c in range(W_CHUNKS):
                copies.append(pltpu.make_async_copy(hbm.at[expert, pl.ds(c * step, step)],
                                                    buf.at[wslot, pl.ds(c * step, step)], wsem.at[wslot]))
        return copies

    def row_copy(tile, r, buf_slot):
        src_row = src_ref[tile * tme + r]
        return pltpu.make_async_copy(h_hbm.at[pl.ds(pl.multiple_of(src_row * SPLIT, SPLIT), SPLIT)],
                                     xbuf.at[buf_slot, pl.ds(pl.multiple_of(r * SPLIT, SPLIT), SPLIT)],
                                     sem.at[buf_slot])

    def fetch(tile, buf_slot):
        def body(r, carry):
            row_copy(tile, r, buf_slot).start()
            return carry
        lax.fori_loop(0, tme, body, 0, unroll=DMA_UNROLL)

    @pl.when(i == 0)
    def _():
        fetch(0, 0)
        for cp in weight_copies(te_ref[0], ws_ref[0]):
            cp.start()

    @pl.when(i + 1 < nu_ref[0])
    def _():
        fetch(i + 1, 1 - slot)

    @pl.when(i < nu_ref[0])
    def _():
        prev = te_ref[jnp.maximum(i - 1, 0)]

        @pl.when((i == 0) | (te_ref[i] != prev))
        def _():
            w = ws_ref[i]
            for cp in weight_copies(te_ref[i], w):
                cp.wait()
            wg_b[...] = wg_f[w].astype(BF16)
            wu_b[...] = wu_f[w].astype(BF16)
            wd_b[...] = wd_f[w].astype(BF16)

            @pl.when(nx_ref[i] >= 0)
            def _():
                for cp in weight_copies(nx_ref[i], 1 - w):
                    cp.start()

        def wait_body(r, carry):
            row_copy(i, r, slot).wait()
            return carry
        lax.fori_loop(0, tme, wait_body, 0, unroll=DMA_UNROLL)

        x = _load_split(xbuf.at[slot], tme).astype(BF16)
        a = _dot(x, wg_b[...])
        u = _dot(x, wu_b[...])
        hid = a * jax.nn.sigmoid(a) * u
        _store_split(o_ref, _dot(hid.astype(BF16), wd_b[...]), tme)

    @pl.when(i >= nu_ref[0])
    def _():
        o_ref[...] = jnp.zeros(o_ref.shape, F32)


def _experts(h, src, tile_expert, n_used, next_expert, weight_slot, w_gate, w_up, w_down, tme):
    p = src.shape[0]
    d = D_MODEL
    hbm = pl.BlockSpec(memory_space=pl.ANY)
    return pl.pallas_call(
        functools.partial(_expert_kernel, tme=tme),
        grid_spec=pltpu.PrefetchScalarGridSpec(
            num_scalar_prefetch=5, grid=(p // tme,),
            in_specs=[hbm, hbm, hbm, hbm],
            out_specs=pl.BlockSpec((tme * SPLIT, LANE), lambda i, *_: (i, 0)),
            scratch_shapes=[pltpu.VMEM((2, tme * SPLIT, LANE), F32),
                            pltpu.VMEM((2, d, D_EXPERT), F32), pltpu.VMEM((2, d, D_EXPERT), F32),
                            pltpu.VMEM((2, D_EXPERT, d), F32),
                            pltpu.VMEM((d, D_EXPERT), BF16), pltpu.VMEM((d, D_EXPERT), BF16),
                            pltpu.VMEM((D_EXPERT, d), BF16),
                            pltpu.SemaphoreType.DMA((2,)), pltpu.SemaphoreType.DMA((2,))]),
        out_shape=jax.ShapeDtypeStruct((p * SPLIT, LANE), F32),
        compiler_params=_cparams(("arbitrary",)),
        name="moe_experts",
    )(tile_expert, n_used, src, next_expert, weight_slot, h, w_gate, w_up, w_down)


def _combine_kernel(d1_ref, d2_ref, y_ref, info_ref, ys_hbm, g_ref, o_ref, a_buf, b_buf, sem, *, tm):
    i = pl.program_id(0)
    slot = i % 2

    def copy(r, row, buf, buf_slot):
        return pltpu.make_async_copy(ys_hbm.at[pl.ds(pl.multiple_of(row * SPLIT, SPLIT), SPLIT)],
                                     buf.at[buf_slot, pl.ds(pl.multiple_of(r * SPLIT, SPLIT), SPLIT)],
                                     sem.at[buf_slot])

    def fetch(tile, buf_slot):
        def issue(r, carry):
            copy(r, d1_ref[tile * tm + r], a_buf, buf_slot).start()
            copy(r, d2_ref[tile * tm + r], b_buf, buf_slot).start()
            return carry
        lax.fori_loop(0, tm, issue, 0, unroll=DMA_UNROLL)

    @pl.when(i == 0)
    def _():
        fetch(0, 0)

    @pl.when(i + 1 < pl.num_programs(0))
    def _():
        fetch(i + 1, 1 - slot)

    def drain(r, carry):
        copy(r, 0, a_buf, slot).wait()
        copy(r, 0, b_buf, slot).wait()
        return carry

    lax.fori_loop(0, tm, drain, 0, unroll=DMA_UNROLL)
    w1 = info_ref[:, 2:3]
    w2 = info_ref[:, 3:4]
    y = y_ref[...] + w1 * _load_split(a_buf.at[slot], tm) + w2 * _load_split(b_buf.at[slot], tm)
    o_ref[...] = y * lax.rsqrt(jnp.mean(y * y, axis=-1, keepdims=True) + EPS) * g_ref[...]


def _combine(y2d, info, ys, d1, d2, g, tm):
    n, d = y2d.shape
    return pl.pallas_call(
        functools.partial(_combine_kernel, tm=tm),
        grid_spec=pltpu.PrefetchScalarGridSpec(
            num_scalar_prefetch=2, grid=(n // tm,),
            in_specs=[pl.BlockSpec((tm, d), lambda i, a, b: (i, 0)),
                      pl.BlockSpec((tm, LANE), lambda i, a, b: (i, 0)),
                      pl.BlockSpec(memory_space=pl.ANY),
                      pl.BlockSpec((1, d), lambda i, a, b: (0, 0))],
            out_specs=pl.BlockSpec((tm, d), lambda i, a, b: (i, 0)),
            scratch_shapes=[pltpu.VMEM((2, tm * SPLIT, LANE), F32), pltpu.VMEM((2, tm * SPLIT, LANE), F32),
                            pltpu.SemaphoreType.DMA((2,))]),
        out_shape=jax.ShapeDtypeStruct((n, d), F32),
        compiler_params=_cparams(("arbitrary",)),
        name="moe_combine",
    )(d1, d2, y2d, info, ys, g)


def _moe_final(y2d, ffn_g, w_rg, b_rg, w_re, b_re, w_gate, w_up, w_down, final_g, tm, tme):
    n, d = y2d.shape
    h, info, cnt = _route(y2d, ffn_g, w_rg, b_rg, w_re, b_re, tm)
    e1 = info[:, 0].astype(jnp.int32)
    e2 = info[:, 1].astype(jnp.int32)
    counts = cnt[0, ROUTE_COL:ROUTE_COL + N_EXPERTS].astype(jnp.int32)
    padded = (counts + tme - 1) // tme * tme
    ends = jnp.cumsum(padded)
    offs = ends - padded
    d1 = offs[e1] + info[:, 4].astype(jnp.int32)
    d2 = offs[e2] + info[:, 5].astype(jnp.int32)
    p = 2 * n + N_EXPERTS * tme
    tok = jnp.arange(n, dtype=jnp.int32)
    src = jnp.zeros((p,), jnp.int32).at[jnp.concatenate([d1, d2])].set(jnp.concatenate([tok, tok]))
    n_tiles = p // tme
    n_used = (ends[-1] // tme).astype(jnp.int32).reshape(1)
    tile_start = jnp.arange(n_tiles, dtype=jnp.int32) * tme
    tile_expert = jnp.sum((ends[None, :] <= tile_start[:, None]).astype(jnp.int32), axis=1)
    last = jnp.sum((ends <= ends[-1] - 1).astype(jnp.int32))
    tile_expert = jnp.minimum(tile_expert, last).astype(jnp.int32)
    run = jnp.cumsum(jnp.concatenate([jnp.zeros((1,), jnp.int32),
                                      (tile_expert[1:] != tile_expert[:-1]).astype(jnp.int32)]))
    weight_slot = (run % 2).astype(jnp.int32)
    after = ends[tile_expert] // tme
    next_expert = jnp.where(after < n_used[0], tile_expert[jnp.minimum(after, n_tiles - 1)], -1).astype(jnp.int32)
    ys = _experts(h, src, tile_expert, n_used, next_expert, weight_slot, w_gate, w_up, w_down, tme)
    return _combine(y2d, info, ys, d1, d2, final_g, tm)


GROUP_ROWS = 8
SLC_LANES = 384
SEL_KEYS = 1024
WIN_KEYS = 640


def _sample_cattn_kernel(q_ref, kc_ref, vc_ref, b_ref, ov_ref, o_ref, sel_ref, *, n_cmp, n_slc):
    q = (q_ref[0, 0] * (NSA_HEAD_DIM ** -0.5)).astype(BF16)
    s = _dot_nt(q, kc_ref[0, 0].astype(BF16)) + b_ref[0]
    valid = lax.broadcasted_iota(jnp.int32, s.shape, 1) < n_cmp
    s = jnp.where(valid, s, NEG_BIG)
    e = jnp.exp(s - jnp.max(s, axis=1, keepdims=True))
    p = jnp.where(valid, e / jnp.sum(e, axis=1, keepdims=True), 0.0)
    o_ref[0, 0] = _dot(p.astype(BF16), vc_ref[0, 0].astype(BF16))
    pg = jnp.where(lax.broadcasted_iota(jnp.int32, p.shape, 0) < NSA_GROUP, p, 0.0)
    pc = jnp.broadcast_to(jnp.sum(pg, axis=0, keepdims=True), pg.shape)
    score = _dot(pc.astype(BF16), ov_ref[...])[0:1, :]
    blk = lax.broadcasted_iota(jnp.int32, score.shape, 1)
    cur = n_slc - 1
    forced = (blk == 0) | (blk == cur) | (blk == cur - 1)
    score = jnp.where(forced, FORCE_SCORE, score)
    score = jnp.where(blk > cur, -jnp.inf, score)
    ri = lax.broadcasted_iota(jnp.int32, (SLC_LANES, SLC_LANES), 0)
    ci = lax.broadcasted_iota(jnp.int32, (SLC_LANES, SLC_LANES), 1)
    sb = jnp.broadcast_to(score, (SLC_LANES, SLC_LANES))
    col = jnp.sum(jnp.where(ri == ci, sb, 0.0), axis=1, keepdims=True)
    beats = (col > sb) | ((col == sb) & (ri < ci))
    rank = jnp.sum(beats.astype(jnp.int32), axis=0, keepdims=True)
    sel_ref[0, 0] = jnp.broadcast_to((rank < SLC_TOPN).astype(F32), (GROUP_ROWS, SLC_LANES))


def _sample_cattn(q8, kvc, bias_cs, n_cmp, n_slc):
    bsz, _, nch, _ = kvc.shape
    c0 = np.arange(nch) * CMP_STRIDE
    s0 = np.arange(SLC_LANES) * SLC_BLOCK
    ov = np.minimum(c0[:, None] + CMP_BLOCK, s0[None, :] + SLC_BLOCK) - np.maximum(c0[:, None], s0[None, :])
    ov = np.clip(ov, 0, None).astype(np.float32) / CMP_BLOCK
    ov[n_cmp:, :] = 0.0
    ov[:, n_slc:] = 0.0
    return pl.pallas_call(
        functools.partial(_sample_cattn_kernel, n_cmp=n_cmp, n_slc=n_slc),
        grid=(bsz, NSA_KV_HEADS),
        in_specs=[pl.BlockSpec((1, 1, GROUP_ROWS, NSA_HEAD_DIM), lambda b, h: (b, h, 0, 0)),
                  pl.BlockSpec((1, 1, nch, NSA_HEAD_DIM), lambda b, h: (b, h, 0, 0)),
                  pl.BlockSpec((1, 1, nch, NSA_HEAD_DIM), lambda b, h: (b, NSA_KV_HEADS + h, 0, 0)),
                  pl.BlockSpec((1, GROUP_ROWS, nch), lambda b, h: (h, 0, 0)),
                  pl.BlockSpec((nch, SLC_LANES), lambda b, h: (0, 0))],
        out_specs=[pl.BlockSpec((1, 1, GROUP_ROWS, NSA_HEAD_DIM), lambda b, h: (b, h, 0, 0)),
                   pl.BlockSpec((1, 1, GROUP_ROWS, SLC_LANES), lambda b, h: (b, h, 0, 0))],
        out_shape=[jax.ShapeDtypeStruct((bsz, NSA_KV_HEADS, GROUP_ROWS, NSA_HEAD_DIM), F32),
                   jax.ShapeDtypeStruct((bsz, NSA_KV_HEADS, GROUP_ROWS, SLC_LANES), F32)],
        compiler_params=_cparams(("arbitrary", "arbitrary")),
        name="sample_cmp_attn",
    )(q8, kvc, kvc, bias_cs, jnp.asarray(ov, BF16))


def _sample_attn_kernel(pg_ref, hf_ref, q_ref, cache_hbm, snew_ref, bs_ref, win_ref, wnew_ref, bw_ref,
                        os_ref, ow_ref, kt_buf, vt_buf, sem, *, n_gather, wlen):
    b = pl.program_id(0)
    rows_per_page = 2 * NSA_KV_HEADS * NSA_HEAD_DIM

    def page_copy(h, s, c, buf):
        page = pg_ref[(b * NSA_KV_HEADS + h) * n_gather + s]
        start = pl.multiple_of(page * rows_per_page + (c * NSA_KV_HEADS + h) * NSA_HEAD_DIM, NSA_HEAD_DIM)
        return pltpu.make_async_copy(cache_hbm.at[pl.ds(start, NSA_HEAD_DIM)],
                                     buf.at[h, :, pl.ds(s * PAGE_SIZE, PAGE_SIZE)], sem)

    for h in range(NSA_KV_HEADS):
        for s in range(n_gather):
            page_copy(h, s, 0, kt_buf).start()
            page_copy(h, s, 1, vt_buf).start()

    def attend(q, kts, vts, bias, valid):
        s = jnp.concatenate([_dot(q, kt.astype(BF16)) for kt in kts], axis=1) + bias
        s = jnp.where(valid, s, NEG_BIG)
        e = jnp.exp(s - jnp.max(s, axis=1, keepdims=True))
        p = jnp.where(valid, e / jnp.sum(e, axis=1, keepdims=True), 0.0).astype(BF16)
        out = None
        lo = 0
        for vt in vts:
            n = vt.shape[1]
            part = _dot_nt(p[:, lo:lo + n], vt.astype(BF16))
            out = part if out is None else out + part
            lo += n
        return out

    def head_rows(ref, c, h):
        r0 = (c * NSA_KV_HEADS + h) * NSA_HEAD_DIM
        return ref[0, r0:r0 + NSA_HEAD_DIM, :]

    widx = lax.broadcasted_iota(jnp.int32, (GROUP_ROWS, wlen + PAGE_SIZE), 1)
    wvalid = (widx >= wlen + 1 - WINDOW) & (widx <= wlen)
    for h in range(NSA_KV_HEADS):
        q = (q_ref[0, h] * (NSA_HEAD_DIM ** -0.5)).astype(BF16)
        ow_ref[0, h] = attend(q, [head_rows(win_ref, 0, h), head_rows(wnew_ref, 0, h)],
                              [head_rows(win_ref, 1, h), head_rows(wnew_ref, 1, h)], bw_ref[h], wvalid)

    for h in range(NSA_KV_HEADS):
        for s in range(n_gather):
            page_copy(h, s, 0, kt_buf).wait()
            page_copy(h, s, 1, vt_buf).wait()

    keys = n_gather * PAGE_SIZE
    lane = lax.broadcasted_iota(jnp.int32, (GROUP_ROWS, keys + PAGE_SIZE), 1)
    slot = lane >> 7
    lane_half = (lane >> 6) & 1
    for h in range(NSA_KV_HEADS):
        want = jnp.full(lane.shape, -1, jnp.int32)
        for s in range(n_gather):
            want = jnp.where(slot == s, hf_ref[(b * NSA_KV_HEADS + h) * n_gather + s], want)
        svalid = (lane_half == want) | (lane == keys)
        q = (q_ref[0, h] * (NSA_HEAD_DIM ** -0.5)).astype(BF16)
        os_ref[0, h] = attend(q, [kt_buf[h], head_rows(snew_ref, 0, h)],
                              [vt_buf[h], head_rows(snew_ref, 1, h)], bs_ref[0, h], svalid)


def _sample_attn(pages, halves, q8, cache_t, snew_t, bias_sel, win_t, wnew_t, bias_w, n_gather):
    bsz = q8.shape[0]
    wlen = win_t.shape[2]
    keys = n_gather * PAGE_SIZE
    rows = 2 * NSA_KV_HEADS * NSA_HEAD_DIM
    qspec = pl.BlockSpec((1, NSA_KV_HEADS, GROUP_ROWS, NSA_HEAD_DIM), lambda b, pg, hf: (b, 0, 0, 0))
    newspec = pl.BlockSpec((1, rows, PAGE_SIZE), lambda b, pg, hf: (b, 0, 0))
    return pl.pallas_call(
        functools.partial(_sample_attn_kernel, n_gather=n_gather, wlen=wlen),
        grid_spec=pltpu.PrefetchScalarGridSpec(
            num_scalar_prefetch=2, grid=(bsz,),
            in_specs=[qspec,
                      pl.BlockSpec(memory_space=pl.ANY),
                      newspec,
                      pl.BlockSpec((1, NSA_KV_HEADS, GROUP_ROWS, keys + PAGE_SIZE), lambda b, pg, hf: (b, 0, 0, 0)),
                      pl.BlockSpec((1, rows, wlen), lambda b, pg, hf: (b, 0, 0)),
                      newspec,
                      pl.BlockSpec((NSA_KV_HEADS, GROUP_ROWS, wlen + PAGE_SIZE), lambda b, pg, hf: (0, 0, 0))],
            out_specs=[qspec, qspec],
            scratch_shapes=[pltpu.VMEM((NSA_KV_HEADS, NSA_HEAD_DIM, keys), F32),
                            pltpu.VMEM((NSA_KV_HEADS, NSA_HEAD_DIM, keys), F32),
                            pltpu.SemaphoreType.DMA(())]),
        out_shape=[jax.ShapeDtypeStruct(q8.shape, F32), jax.ShapeDtypeStruct(q8.shape, F32)],
        compiler_params=_cparams(("arbitrary",)),
        name="sample_slc_win_attn",
    )(pages, halves, q8, cache_t, snew_t, bias_sel, win_t, wnew_t, bias_w)


PAGES_PER_STEP = 16


def _paged_pq_kernel(pt_ref, cache_hbm, pos_ref, w_ref, o_ref, pbuf, tok_scr, sem):
    b = pl.program_id(0)
    g = pl.program_id(1)
    ng = pl.num_programs(1)
    lin = b * ng + g
    slot = lin % 2
    rows_per_page = 2 * NSA_KV_HEADS * NSA_HEAD_DIM
    chunks = PAGES_PER_STEP * (PAGE_SIZE // CMP_STRIDE)

    def page_copy(step, p, buf_slot):
        sb = step // ng
        sg = step - sb * ng
        page = pt_ref[sb, sg * PAGES_PER_STEP + p]
        return pltpu.make_async_copy(cache_hbm.at[pl.ds(pl.multiple_of(page * rows_per_page, rows_per_page),
                                                        rows_per_page)],
                                     pbuf.at[buf_slot, p], sem.at[buf_slot])

    def fetch(step, buf_slot):
        for p in range(PAGES_PER_STEP):
            page_copy(step, p, buf_slot).start()

    @pl.when(lin == 0)
    def _():
        fetch(0, 0)

    @pl.when(lin + 1 < pl.num_programs(0) * ng)
    def _():
        fetch(lin + 1, 1 - slot)

    for p in range(PAGES_PER_STEP):
        page_copy(lin, p, slot).wait()

    for c in range(2):
        head_chunks = []
        for pair in (2 * c, 2 * c + 1):
            for p in range(PAGES_PER_STEP):
                tok_scr[p * PAGE_SIZE:(p + 1) * PAGE_SIZE, :] = pbuf[slot, p, pair * LANE:(pair + 1) * LANE, :].T
            toks = [tok_scr[pl.ds(l, chunks, stride=CMP_STRIDE), :] for l in range(CMP_STRIDE)]
            for half in range(2):
                lanes = slice(half * NSA_HEAD_DIM, (half + 1) * NSA_HEAD_DIM)
                head_chunks.append(jnp.concatenate([t[:, lanes] for t in toks], axis=1))
        stacked = jnp.concatenate(head_chunks, axis=0)
        for part in range(2):
            lhs = (stacked + pos_ref[c, part:part + 1, :]).astype(BF16)
            out = _dot(lhs, w_ref[c, part])
            for hh in range(NSA_KV_HEADS):
                o_ref[0, c * NSA_KV_HEADS + hh, :, part * CMP_HIDDEN:(part + 1) * CMP_HIDDEN] = (
                    out[hh * chunks:(hh + 1) * chunks, :])


def _paged_pq(cache_t, page_table, cmp_pos, cmp_w1):
    db, n_pages = page_table.shape
    kdim = CMP_STRIDE * NSA_HEAD_DIM
    nch = n_pages * (PAGE_SIZE // CMP_STRIDE)
    tn = PAGES_PER_STEP * (PAGE_SIZE // CMP_STRIDE)
    rows_per_page = 2 * NSA_KV_HEADS * NSA_HEAD_DIM
    return pl.pallas_call(
        _paged_pq_kernel,
        grid_spec=pltpu.PrefetchScalarGridSpec(
            num_scalar_prefetch=1, grid=(db, n_pages // PAGES_PER_STEP),
            in_specs=[pl.BlockSpec(memory_space=pl.ANY),
                      pl.BlockSpec((2, 2, kdim), lambda b, g, pt: (0, 0, 0)),
                      pl.BlockSpec((2, 2, kdim, CMP_HIDDEN), lambda b, g, pt: (0, 0, 0, 0))],
            out_specs=pl.BlockSpec((1, 8, tn, 2 * CMP_HIDDEN), lambda b, g, pt: (b, 0, g, 0)),
            scratch_shapes=[pltpu.VMEM((2, PAGES_PER_STEP, rows_per_page, PAGE_SIZE), F32),
                            pltpu.VMEM((PAGES_PER_STEP * PAGE_SIZE, LANE), F32),
                            pltpu.SemaphoreType.DMA((2,))]),
        out_shape=jax.ShapeDtypeStruct((db, 8, nch, 2 * CMP_HIDDEN), F32),
        compiler_params=_cparams(("arbitrary", "arbitrary")),
        name="cmp_pq_paged",
    )(page_table, cache_t, cmp_pos.reshape(2, 2, kdim), cmp_w1.reshape(2, 2, kdim, CMP_HIDDEN).astype(BF16))


def _pad_rows(a, n):
    return jnp.concatenate([a, jnp.zeros((n - a.shape[0],) + a.shape[1:], a.dtype)], axis=0)


def _sample_mixer(x_sample, cache_cmp, cache_slc, cache_win, st_c, st_n, st_m, st_conv, page_table,
                  norm_g, wb, gate_bias, conv_w, ml_norm_g, cmp_pos, cmp_w1, cmp_w2, nsa_norm_g, w_out_b, rel_bias):
    db = x_sample.shape[0]
    n_pages = page_table.shape[1]
    past = n_pages * PAGE_SIZE
    tok = 16
    x16 = _pad_rows(x_sample.reshape(db, D_MODEL), tok)
    u = _proj(x16, norm_g, wb, tok, 512)
    small = u[:, COL_SMALL:COL_SMALL + LANE]

    T = 128
    useq = jnp.zeros((db, T, 4 * D_ML), F32)
    useq = useq.at[:, T - CONV_W:T - 1, 0:2 * D_ML].set(st_conv)
    useq = useq.at[:, T - 1, :].set(u[:db, 0:4 * D_ML])
    sseq = jnp.zeros((db, T, LANE), F32).at[:, :T - 1, LANE - 1].set(1.0)
    sseq = sseq.at[:, T - 1, :].set(small[:db])
    y_seq, conv_n, c_n, n_n, m_n = _mlstm(
        useq.reshape(db * T, 4 * D_ML), sseq.reshape(db * T, LANE), jnp.zeros((db, CONV_W - 1, 2 * D_ML), F32),
        st_c, st_n, st_m, conv_w, gate_bias, ml_norm_g, db, T, T)
    y_ml = y_seq.reshape(db, T, D_ML)[:, T - 1]

    n_pool = cache_cmp.shape[0]
    cmp_t = cache_cmp.transpose(0, 2, 3, 4, 1).reshape(n_pool * 2 * D_KV, PAGE_SIZE)
    kvc = _cmp_hid(_paged_pq(cmp_t, page_table, cmp_pos, cmp_w1), cmp_w2)
    nch = past // CMP_STRIDE
    n_cmp = (past + 1) // CMP_STRIDE - CMP_BLOCK // CMP_STRIDE + 1
    n_slc = -(-(past + 1) // SLC_BLOCK)
    q = u[:db, COL_QNSA:COL_QNSA + D_NSA].reshape(db, NSA_KV_HEADS, NSA_GROUP, NSA_HEAD_DIM)
    q8 = jnp.concatenate([q, jnp.zeros_like(q)], axis=2)
    bd = _bias_by_distance(rel_bias, past + 1).reshape(NSA_KV_HEADS, NSA_GROUP, past + 1)
    pad_g = lambda a: jnp.concatenate([a, jnp.zeros_like(a)], axis=1)
    dist_c = np.clip(past - (np.arange(nch) * CMP_STRIDE + CMP_BLOCK - 1), 0, None)
    o_c8, sel = _sample_cattn(q8, kvc, pad_g(bd[:, :, dist_c]), n_cmp, n_slc)

    mask = sel[:, :, 0, :n_slc] > 0.5
    idx = jnp.sort(jnp.where(mask, jnp.arange(n_slc, dtype=jnp.int32), jnp.int32(1 << 20)), axis=-1)
    n_gather = SLC_TOPN - 1
    idx = idx[..., :n_gather]
    pages_per_block = PAGE_SIZE // SLC_BLOCK
    logical_page = idx // pages_per_block
    pages = jnp.take_along_axis(page_table[:, None, :], logical_page, axis=2).reshape(-1).astype(jnp.int32)
    halves = (idx % pages_per_block).reshape(-1).astype(jnp.int32)
    kpos = (logical_page[..., None] * PAGE_SIZE + jnp.arange(PAGE_SIZE, dtype=jnp.int32)).reshape(db, NSA_KV_HEADS, -1)
    hh = jnp.arange(NSA_KV_HEADS)[None, :, None, None]
    gg = jnp.arange(NSA_GROUP)[None, None, :, None]
    bias_sel = bd[hh, gg, (past - kpos)[:, :, None, :]]
    bias_sel = jnp.concatenate([bias_sel, jnp.broadcast_to(bd[None, :, :, 0:1], (db, NSA_KV_HEADS, NSA_GROUP, 1)),
                                jnp.zeros((db, NSA_KV_HEADS, NSA_GROUP, PAGE_SIZE - 1), F32)], axis=-1)
    bias_sel = jnp.concatenate([bias_sel, jnp.zeros_like(bias_sel)], axis=2)
    wlen = cache_win.shape[1]
    dist_w = np.clip(wlen - np.arange(wlen + PAGE_SIZE), 0, None)
    kvs_new = _kv_from_per_head(u[:db, COL_KVS:COL_KVS + 2 * D_KV])
    kvw_new = _kv_from_per_head(u[:db, COL_KVW:COL_KVW + 2 * D_KV])
    lane_pad = lambda a: jnp.pad(a[:, :, None], ((0, 0), (0, 0), (0, PAGE_SIZE - 1)))
    slc_t = cache_slc.transpose(0, 2, 3, 4, 1).reshape(n_pool * 2 * D_KV, PAGE_SIZE)
    win_t = cache_win.transpose(0, 2, 3, 4, 1).reshape(db, 2 * D_KV, wlen)
    o_s8, o_w8 = _sample_attn(pages, halves, q8, slc_t, lane_pad(kvs_new), bias_sel, win_t, lane_pad(kvw_new),
                              pad_g(bd[:, :, dist_w]), n_gather)
    win2d = cache_win.reshape(db, wlen, 2 * D_KV)

    heads = lambda o: _pad_rows(o[:, :, :NSA_GROUP, :].reshape(db, D_NSA), tok)
    y = _outproj(_pad_rows(y_ml, tok), heads(o_c8), heads(o_s8), heads(o_w8), small, nsa_norm_g, w_out_b, x16, tok, 512)
    kvshape = (1, db, 1, 2, NSA_KV_HEADS, NSA_HEAD_DIM)
    new_win = jnp.concatenate([win2d[:, 1:], kvw_new[:, None, :]], axis=1)
    states = (u[:db, COL_KVC:COL_KVC + 2 * D_KV].reshape(kvshape), kvs_new.reshape(kvshape),
              new_win.reshape((1, db, wlen, 2, NSA_KV_HEADS, NSA_HEAD_DIM)),
              c_n[None], n_n[None], m_n[None], conv_n[None])
    return y, states


def kernel(x_prompt, x_sample, cache_cmp_kv, cache_slc_kv, cache_win_kv, state_mlstm_C, state_mlstm_n,
           state_mlstm_m, state_conv, page_table, rel_bias, norm_mix_g, w_in, b_ig, b_fg, conv_w, ml_norm_g,
           cmp_pos, cmp_w1, cmp_w2, nsa_norm_g, w_out, norm_ffn_g, w_router_grp, b_router_grp, w_router_exp,
           b_router_exp, w_gate, w_up, w_down, norm_final_g):
    B, S, D = x_prompt.shape
    wb = _reorder_w_in(w_in[0])
    gate_bias = jnp.zeros((1, LANE), F32).at[0, 0:ML_HEADS].set(b_ig[0]).at[0, ML_HEADS:2 * ML_HEADS].set(b_fg[0])
    w_out_b = w_out[0].astype(BF16)
    yp, st_p = _prompt_mixer(x_prompt, norm_mix_g, wb, gate_bias, conv_w[0], ml_norm_g, cmp_pos[0], cmp_w1[0],
                             cmp_w2[0], nsa_norm_g, w_out_b, rel_bias)
    ys, st_s = _sample_mixer(x_sample, cache_cmp_kv[0], cache_slc_kv[0], cache_win_kv[0], state_mlstm_C[0],
                             state_mlstm_n[0], state_mlstm_m[0], state_conv[0], page_table, norm_mix_g, wb, gate_bias,
                             conv_w[0], ml_norm_g, cmp_pos[0], cmp_w1[0], cmp_w2[0], nsa_norm_g, w_out_b, rel_bias)
    moe_w = (norm_ffn_g, w_router_grp[0], b_router_grp[0], w_router_exp[0], b_router_exp[0],
             w_gate[0], w_up[0], w_down[0], norm_final_g[None, :])
    DB, L, _ = x_sample.shape
    out_p = _moe_final(yp.reshape(B * S, D), *moe_w, 256, 128).reshape(B, S, D)
    out_s = _moe_final(_pad_rows(ys, 128), *moe_w, 128, 16)[:DB].reshape(DB, L, D)
    outs = [out_p, out_s]
    for a, b in zip(st_p, st_s):
        outs += [a, b]
    return tuple(outs)


def _prompt_mixer(x_prompt, norm_g, wb, gate_bias, conv_w, ml_norm_g, cmp_pos, cmp_w1, cmp_w2, nsa_norm_g,
                  w_out_b, rel_bias):
    B, S, D = x_prompt.shape
    x2d = x_prompt.reshape(B * S, D)
    tm = min(1024, S)
    u, kvt_c, kvt_s, kvt_w = _proj(x2d, norm_g, wb, tm, 2 * D_KV, seq=S)
    small = u[:, COL_SMALL:COL_SMALL + LANE]
    y_ml, conv_n, c_n, n_n, m_n = _mlstm(
        u, small, jnp.zeros((B, CONV_W - 1, 2 * D_ML), F32),
        jnp.zeros((B, ML_HEADS, ML_HEAD_DIM, ML_HEAD_DIM), F32), jnp.zeros((B, ML_HEADS, ML_HEAD_DIM), F32),
        jnp.full((B, ML_HEADS), -jnp.inf, F32), conv_w, gate_bias, ml_norm_g, B, S, 256)
    kvc = _compress(u, COL_KVC // (2 * D_KV), B, S, cmp_pos, cmp_w1, cmp_w2, min(S, 4096))
    tq_c = min(S, 256)
    o_c, sel_t = _cattn(u, kvc, rel_bias, B, S, tq_c)
    o_s = _flash(u, COL_KVS, rel_bias, sel_t, B, S)
    o_w = _flash(u, COL_KVW, rel_bias, None, B, S)
    y = _outproj(y_ml, o_c, o_s, o_w, small, nsa_norm_g, w_out_b, x2d, tm, 512)
    win = min(WINDOW, S)
    rows = lambda kvt: kvt.transpose(0, 4, 1, 2, 3)[None]
    states = (rows(kvt_c), rows(kvt_s), rows(kvt_w[..., S - win:]),
              c_n[None], n_n[None], m_n[None], conv_n[None])
    return y.reshape(B, S, D), states
```

```python
import functools
import math

import numpy as np
import jax
import jax.numpy as jnp
from jax import lax
from jax.experimental import pallas as pl
from jax.experimental.pallas import tpu as pltpu

F32 = jnp.float32
BF16 = jnp.bfloat16

D_MODEL = 2048
ML_HEADS = 4
ML_HEAD_DIM = 256
D_ML = 1024
CONV_W = 4
NSA_HEADS = 16
NSA_HEAD_DIM = 64
D_NSA = 1024
NSA_KV_HEADS = 4
NSA_GROUP = 4
D_KV = 256
CMP_BLOCK = 32
CMP_STRIDE = 16
CMP_HIDDEN = 256
SLC_BLOCK = 64
SLC_TOPN = 16
WINDOW = 512
N_BUCKETS = 32
MAX_DISTANCE = 2048
N_GROUPS = 4
EXPERTS_PER_GROUP = 8
N_EXPERTS = 32
D_EXPERT = 512
PAGE_SIZE = 128
EPS = 1e-6
NEG_BIG = -1e30
FORCE_SCORE = 1e4

LANE = 128
COL_QML, COL_KML, COL_VML, COL_OML = 0, 1024, 2048, 3072
COL_QNSA = 4096
COL_KVC, COL_KVS, COL_KVW = 5120, 5632, 6144
COL_SMALL = 6656
N_PROJ = 7168
VMEM_LIMIT = 56 * 1024 * 1024


def _cparams(sem, vmem=VMEM_LIMIT):
    return pltpu.CompilerParams(dimension_semantics=sem, vmem_limit_bytes=vmem)


def _split2(x):
    hi = x.astype(BF16)
    lo = (x - hi.astype(F32)).astype(BF16)
    return hi, lo


def _split3(x):
    hi = x.astype(BF16)
    r = x - hi.astype(F32)
    mid = r.astype(BF16)
    lo = (r - mid.astype(F32)).astype(BF16)
    return hi, mid, lo


def _dot(a, b):
    return jnp.dot(a, b, preferred_element_type=F32)


def _dot_nt(a, b):
    return lax.dot_general(a, b, (((1,), (1,)), ((), ())), preferred_element_type=F32)


def _dot_tn(a, b):
    return lax.dot_general(a, b, (((0,), (0,)), ((), ())), preferred_element_type=F32)


def _proj_kernel(x_ref, g_ref, w_ref, o_ref, *rest, kv_blocks):
    h_scr = rest[-1]
    j = pl.program_id(1)

    @pl.when(j == 0)
    def _():
        x = x_ref[...]
        ms = jnp.mean(x * x, axis=-1, keepdims=True)
        h_scr[...] = (x * lax.rsqrt(ms + EPS) * g_ref[...]).astype(BF16)

    res = _dot(h_scr[...], w_ref[...])
    o_ref[...] = res

    for (block, per_head), t_ref in zip(kv_blocks, rest[:-1]):
        @pl.when(j == block)
        def _():
            res_t = res.T
            for c in range(2):
                for hh in range(NSA_KV_HEADS):
                    piece = hh * 2 + c if per_head else c * NSA_KV_HEADS + hh
                    t_ref[0, c, hh] = res_t[piece * NSA_HEAD_DIM:(piece + 1) * NSA_HEAD_DIM, :]


def _proj(x2d, g, wb, tm, tn, seq=None):
    n, d = x2d.shape
    nc = wb.shape[1]
    kv_blocks = ()
    out_specs = [pl.BlockSpec((tm, tn), lambda i, j: (i, j))]
    out_shape = [jax.ShapeDtypeStruct((n, nc), F32)]
    if seq is not None:
        assert tn == 2 * D_KV and seq % tm == 0
        kv_blocks = ((COL_KVC // tn, False), (COL_KVS // tn, True), (COL_KVW // tn, True))
        per_seq = seq // tm
        for _ in kv_blocks:
            out_specs.append(pl.BlockSpec((1, 2, NSA_KV_HEADS, NSA_HEAD_DIM, tm),
                                          lambda i, j: (i // per_seq, 0, 0, 0, i % per_seq)))
            out_shape.append(jax.ShapeDtypeStruct((n // seq, 2, NSA_KV_HEADS, NSA_HEAD_DIM, seq), F32))
    outs = pl.pallas_call(
        functools.partial(_proj_kernel, kv_blocks=kv_blocks),
        grid=(n // tm, nc // tn),
        in_specs=[pl.BlockSpec((tm, d), lambda i, j: (i, 0)),
                  pl.BlockSpec((1, d), lambda i, j: (0, 0)),
                  pl.BlockSpec((d, tn), lambda i, j: (0, j))],
        out_specs=out_specs,
        out_shape=out_shape,
        scratch_shapes=[pltpu.VMEM((tm, d), BF16)],
        compiler_params=_cparams(("arbitrary", "arbitrary")),
        name="proj",
    )(x2d, g, wb)
    return outs if seq is not None else outs[0]


def _reorder_w_in(w_in):
    big = w_in[:, :4 * D_ML]
    small_a = w_in[:, 4 * D_ML:4 * D_ML + 2 * ML_HEADS]
    rest = w_in[:, 4 * D_ML + 2 * ML_HEADS:]
    q_and_cmp = rest[:, :D_NSA + 2 * D_KV]
    gate = rest[:, D_NSA + 6 * D_KV:]
    d = w_in.shape[0]

    def per_head(w):
        return w.reshape(d, 2, NSA_KV_HEADS, NSA_HEAD_DIM).transpose(0, 2, 1, 3).reshape(d, 2 * D_KV)

    kv_s = per_head(rest[:, D_NSA + 2 * D_KV:D_NSA + 4 * D_KV])
    kv_w = per_head(rest[:, D_NSA + 4 * D_KV:D_NSA + 6 * D_KV])
    pad = jnp.zeros((d, N_PROJ - COL_SMALL - 2 * ML_HEADS - 3 * NSA_HEADS), w_in.dtype)
    return jnp.concatenate([big, q_and_cmp, kv_s, kv_w, small_a, gate, pad], axis=1).astype(BF16)


def _kv_from_per_head(kv2d):
    n = kv2d.shape[0]
    return kv2d.reshape(n, NSA_KV_HEADS, 2, NSA_HEAD_DIM).transpose(0, 2, 1, 3).reshape(n, 2 * D_KV)


def _log_sigmoid(x):
    return jnp.minimum(x, 0.0) - jnp.log1p(jnp.exp(-jnp.abs(x)))


def _mlstm_kernel(q_ref, k_ref, v_ref, o_ref, s_ref, cb_ref, c0_ref, n0_ref, m0_ref,
                  cw_ref, gb_ref, ng_ref,
                  y_ref, cbo_ref, co_ref, no_ref, mo_ref,
                  ext_scr, c_scr, n_scr, m_scr, *, T):
    c = pl.program_id(1)
    nc = pl.num_programs(1)

    @pl.when(c == 0)
    def _():
        ext_scr[0:8, :] = jnp.zeros((8, 2 * D_ML), F32)
        ext_scr[5:8, :] = cb_ref[0]
        c_scr[...] = c0_ref[0]
        n_scr[...] = n0_ref[0]
        m_scr[...] = m0_ref[0]

    ext_scr[8:8 + T, 0:D_ML] = q_ref[...]
    ext_scr[8:8 + T, D_ML:2 * D_ML] = k_ref[...]
    conv = ext_scr[5:5 + T, :] * cw_ref[0:1, :]
    for j in range(1, CONV_W):
        conv = conv + ext_scr[5 + j:5 + j + T, :] * cw_ref[j:j + 1, :]
    tail = ext_scr[8 + T - 3:8 + T, :]
    ext_scr[5:8, :] = tail
    cbo_ref[0] = tail
    qk = conv * jax.nn.sigmoid(conv)

    pre = s_ref[...] + gb_ref[...]
    col = lax.broadcasted_iota(jnp.int32, pre.shape, 1)
    padrow = s_ref[:, LANE - 1:LANE] > 0.5
    gates = jnp.where(col < ML_HEADS, pre, _log_sigmoid(pre))
    gates = jnp.where(padrow, jnp.where(col < ML_HEADS, NEG_BIG, 0.0), gates)
    g_r = gates.T
    ti = lax.broadcasted_iota(jnp.int32, (T, T), 0)
    si = lax.broadcasted_iota(jnp.int32, (T, T), 1)
    upper = (ti <= si).astype(BF16)
    g_fin = jnp.where(lax.broadcasted_iota(jnp.int32, g_r.shape, 0) < ML_HEADS, 0.0, g_r)
    hi, mid, lo = _split3(g_fin)
    cum_r = _dot(hi, upper) + _dot(mid, upper) + _dot(lo, upper)
    rowi = lax.broadcasted_iota(jnp.int32, g_r.shape, 0)
    a_r = jnp.where(rowi < ML_HEADS, g_r, cum_r)
    a_c = a_r.T
    causal = si <= ti

    for h in range(ML_HEADS):
        sl = slice(h * ML_HEAD_DIM, (h + 1) * ML_HEAD_DIM)
        q = qk[:, h * ML_HEAD_DIM:(h + 1) * ML_HEAD_DIM]
        k = qk[:, D_ML + h * ML_HEAD_DIM:D_ML + (h + 1) * ML_HEAD_DIM] * (ML_HEAD_DIM ** -0.5)
        v = v_ref[:, sl]
        ig_r = a_r[h:h + 1, :]
        b_r = a_r[ML_HEADS + h:ML_HEADS + h + 1, :]
        ig_c = a_c[:, h:h + 1]
        b_c = a_c[:, ML_HEADS + h:ML_HEADS + h + 1]
        m_prev = m_scr[h:h + 1, 0:1]
        logd = jnp.where(causal, b_c - b_r + ig_r, -jnp.inf)
        inter = b_c + m_prev
        m_t = jnp.maximum(inter, jnp.max(logd, axis=1, keepdims=True))
        w_intra = jnp.exp(logd - m_t)
        w_inter = jnp.exp(inter - m_t)
        qb = q.astype(BF16)
        kb = k.astype(BF16)
        vb = v.astype(BF16)
        sc = _dot_nt(qb, kb) * w_intra
        cmat = c_scr[h]
        nvec = n_scr[h:h + 1, :]
        num = _dot(sc.astype(BF16), vb) + w_inter * _dot(qb, cmat.astype(BF16))
        qn = jnp.sum(qb.astype(F32) * nvec.astype(BF16).astype(F32), axis=1, keepdims=True)
        den = jnp.sum(sc, axis=1, keepdims=True) + w_inter * qn
        hh = num / jnp.maximum(jnp.abs(den), jnp.exp(-m_t))
        m_new = m_t[T - 1:T, :]
        b_last = b_c[T - 1:T, :]
        w_s = jnp.exp(b_last - b_c + ig_c - m_new)
        decay = jnp.exp(b_last + m_prev - m_new)
        kw = k * w_s
        c_new = decay * cmat + _dot_tn(kw.astype(BF16), vb)
        n_new = decay * nvec + jnp.sum(kw, axis=0, keepdims=True)
        c_scr[h] = c_new
        n_scr[h:h + 1, :] = n_new
        m_scr[h:h + 1, :] = jnp.broadcast_to(m_new, (1, LANE))
        hn = hh * lax.rsqrt(jnp.mean(hh * hh, axis=1, keepdims=True) + EPS) * ng_ref[:, sl]
        y_ref[:, sl] = (hn * jax.nn.sigmoid(o_ref[:, sl])).astype(y_ref.dtype)

    @pl.when(c == nc - 1)
    def _():
        co_ref[0] = c_scr[...]
        no_ref[0] = n_scr[...]
        mo_ref[0] = m_scr[...]


def _mlstm(u, small, conv_buf, c0, n0, m0, conv_w, gate_bias, norm_g, batch, seq, T):
    nc = seq // T
    cb = D_ML // 1024
    m0b = jnp.broadcast_to(m0[:, :, None], (batch, ML_HEADS, LANE))
    m0b = jnp.concatenate([m0b, jnp.zeros((batch, 8 - ML_HEADS, LANE), F32)], axis=1)
    n0p = jnp.concatenate([n0, jnp.zeros((batch, 8 - ML_HEADS, ML_HEAD_DIM), F32)], axis=1)
    row = lambda b, c: (b * nc + c, 0)
    outs = pl.pallas_call(
        functools.partial(_mlstm_kernel, T=T),
        grid=(batch, nc),
        in_specs=[pl.BlockSpec((T, D_ML), lambda b, c: (b * nc + c, COL_QML // D_ML)),
                  pl.BlockSpec((T, D_ML), lambda b, c: (b * nc + c, COL_KML // D_ML)),
                  pl.BlockSpec((T, D_ML), lambda b, c: (b * nc + c, COL_VML // D_ML)),
                  pl.BlockSpec((T, D_ML), lambda b, c: (b * nc + c, COL_OML // D_ML)),
                  pl.BlockSpec((T, LANE), row),
                  pl.BlockSpec((1, CONV_W - 1, 2 * D_ML), lambda b, c: (b, 0, 0)),
                  pl.BlockSpec((1, ML_HEADS, ML_HEAD_DIM, ML_HEAD_DIM), lambda b, c: (b, 0, 0, 0)),
                  pl.BlockSpec((1, 8, ML_HEAD_DIM), lambda b, c: (b, 0, 0)),
                  pl.BlockSpec((1, 8, LANE), lambda b, c: (b, 0, 0)),
                  pl.BlockSpec((CONV_W, 2 * D_ML), lambda b, c: (0, 0)),
                  pl.BlockSpec((1, LANE), lambda b, c: (0, 0)),
                  pl.BlockSpec((1, D_ML), lambda b, c: (0, 0))],
        out_specs=[pl.BlockSpec((T, D_ML), row),
                   pl.BlockSpec((1, CONV_W - 1, 2 * D_ML), lambda b, c: (b, 0, 0)),
                   pl.BlockSpec((1, ML_HEADS, ML_HEAD_DIM, ML_HEAD_DIM), lambda b, c: (b, 0, 0, 0)),
                   pl.BlockSpec((1, 8, ML_HEAD_DIM), lambda b, c: (b, 0, 0)),
                   pl.BlockSpec((1, 8, LANE), lambda b, c: (b, 0, 0))],
        out_shape=[jax.ShapeDtypeStruct((batch * seq, D_ML), BF16),
                   jax.ShapeDtypeStruct((batch, CONV_W - 1, 2 * D_ML), F32),
                   jax.ShapeDtypeStruct((batch, ML_HEADS, ML_HEAD_DIM, ML_HEAD_DIM), F32),
                   jax.ShapeDtypeStruct((batch, 8, ML_HEAD_DIM), F32),
                   jax.ShapeDtypeStruct((batch, 8, LANE), F32)],
        scratch_shapes=[pltpu.VMEM((8 + T, 2 * D_ML), F32),
                        pltpu.VMEM((ML_HEADS, ML_HEAD_DIM, ML_HEAD_DIM), F32),
                        pltpu.VMEM((8, ML_HEAD_DIM), F32),
                        pltpu.VMEM((8, LANE), F32)],
        compiler_params=_cparams(("arbitrary", "arbitrary")),
        name="mlstm",
    )(u, u, u, u, small, conv_buf, c0, n0p, m0b, conv_w, gate_bias, norm_g)
    y, cbo, co, no, mo = outs
    return y, cbo, co, no[:, :ML_HEADS], mo[:, :ML_HEADS, 0]


def _bucket_np(dist):
    n = np.maximum(dist, 0)
    max_exact = N_BUCKETS // 2
    nf = np.maximum(n, 1).astype(np.float64)
    large = max_exact + (np.log(nf / max_exact) / math.log(MAX_DISTANCE / max_exact)
                         * (N_BUCKETS - max_exact)).astype(np.int64)
    return np.where(n < max_exact, n, np.minimum(large, N_BUCKETS - 1)).astype(np.int32)


def _bias_by_distance(rel_bias, n):
    return rel_bias.astype(F32)[_bucket_np(np.arange(n))].T


def _overlap_t(n_cmp, nch, n_slc):
    c0 = np.arange(nch) * CMP_STRIDE
    s0 = np.arange(n_slc) * SLC_BLOCK
    ov = np.minimum(c0[None, :] + CMP_BLOCK, s0[:, None] + SLC_BLOCK) - np.maximum(c0[None, :], s0[:, None])
    ov = np.clip(ov, 0, None).astype(np.float32) / CMP_BLOCK
    ov[:, n_cmp:] = 0.0
    return jnp.asarray(ov, BF16)


def _pq_kernel(x0_ref, x1_ref, x2_ref, x3_ref, pos_ref, w_ref, o_ref, *, rows):
    for pair, x_ref in enumerate((x0_ref, x1_ref, x2_ref, x3_ref)):
        toks = [x_ref[pl.ds(l, rows, stride=CMP_STRIDE), :] for l in range(CMP_STRIDE)]
        for half in range(2):
            ch = 2 * pair + half
            c = ch // NSA_KV_HEADS
            lanes = slice(half * NSA_HEAD_DIM, (half + 1) * NSA_HEAD_DIM)
            chunk = jnp.concatenate([t[:, lanes] for t in toks], axis=1)
            for part in range(2):
                lhs = (chunk + pos_ref[c, part:part + 1, :]).astype(BF16)
                o_ref[0, ch, :, part * CMP_HIDDEN:(part + 1) * CMP_HIDDEN] = _dot(lhs, w_ref[c, part])


def _hid_kernel(pq_ref, w2_ref, o_ref, *, nch):
    p = pq_ref[0, 0, :, 0:CMP_HIDDEN]
    q = pltpu.roll(pq_ref[0, 0, :, CMP_HIDDEN:2 * CMP_HIDDEN], nch - 1, 0)
    hid = jax.nn.gelu(p + q, approximate=True)
    o_ref[0, 0] = _dot(hid.astype(BF16), w2_ref[0])


def _compress(kv2d, col_block, bsz, seq, cmp_pos, cmp_w1, cmp_w2, tt):
    nch = seq // CMP_STRIDE
    kdim = CMP_STRIDE * NSA_HEAD_DIM
    nt = seq // tt
    tn = tt // CMP_STRIDE
    w1 = cmp_w1.reshape(2, 2, kdim, CMP_HIDDEN).astype(BF16)
    pos = cmp_pos.reshape(2, 2, kdim)
    pq = pl.pallas_call(
        functools.partial(_pq_kernel, rows=tn),
        grid=(bsz, nt),
        in_specs=[pl.BlockSpec((tt, LANE), functools.partial(lambda b, i, k: (b * nt + i, col_block * 4 + k), k=k))
                  for k in range(4)] +
                 [pl.BlockSpec((2, 2, kdim), lambda b, i: (0, 0, 0)),
                  pl.BlockSpec((2, 2, kdim, CMP_HIDDEN), lambda b, i: (0, 0, 0, 0))],
        out_specs=pl.BlockSpec((1, 8, tn, 2 * CMP_HIDDEN), lambda b, i: (b, 0, i, 0)),
        out_shape=jax.ShapeDtypeStruct((bsz, 8, nch, 2 * CMP_HIDDEN), F32),
        compiler_params=_cparams(("arbitrary", "arbitrary")),
        name="cmp_pq",
    )(kv2d, kv2d, kv2d, kv2d, pos, w1)
    return _cmp_hid(pq, cmp_w2)


def _cmp_hid(pq, cmp_w2):
    bsz, _, nch, _ = pq.shape
    return pl.pallas_call(
        functools.partial(_hid_kernel, nch=nch),
        grid=(bsz, 8),
        in_specs=[pl.BlockSpec((1, 1, nch, 2 * CMP_HIDDEN), lambda b, c: (b, c, 0, 0)),
                  pl.BlockSpec((1, CMP_HIDDEN, NSA_HEAD_DIM), lambda b, c: (c // NSA_KV_HEADS, 0, 0))],
        out_specs=pl.BlockSpec((1, 1, nch, NSA_HEAD_DIM), lambda b, c: (b, c, 0, 0)),
        out_shape=jax.ShapeDtypeStruct((bsz, 8, nch, NSA_HEAD_DIM), F32),
        compiler_params=_cparams(("arbitrary", "arbitrary")),
        name="cmp_hid",
    )(pq, cmp_w2.astype(BF16))


def _top_n_mask(score_t, blk, n_rows):
    rank = jnp.zeros(score_t.shape, jnp.int32)
    for i in range(n_rows):
        row = score_t[i:i + 1, :]
        beats = (row > score_t) | ((row == score_t) & (blk > i))
        rank = rank + beats.astype(jnp.int32)
    return (rank < min(SLC_TOPN, n_rows)).astype(F32)


def _bias_from_buckets(bucket, table_ref, first_head):
    biases = [jnp.zeros(bucket.shape, F32) for _ in range(NSA_GROUP)]
    for k in range(N_BUCKETS):
        hit = bucket == k
        biases = [jnp.where(hit, table_ref[k, first_head + g], biases[g]) for g in range(NSA_GROUP)]
    return biases


def _cattn_kernel(tab_ref, q_ref, kc_ref, vc_ref, bk_ref, ovt_ref, o_ref, sel_ref, bias_scr, strip_scr,
                  *, tq, nch, n_cmp, n_slc):
    h = pl.program_id(0)
    i = pl.program_id(1)

    @pl.when(pl.program_id(2) == 0)
    def _():
        bands = tq // CMP_STRIDE
        for g, strip in enumerate(_bias_from_buckets(bk_ref[0], tab_ref, h * NSA_GROUP)):
            strip_scr[g] = strip
            for a in range(bands):
                off = bands - 1 - a
                bias_scr[g, a * CMP_STRIDE:(a + 1) * CMP_STRIDE, :] = strip_scr[g, :, off:off + nch]

    kc = kc_ref[0, 0].astype(BF16)
    vc = vc_ref[0, 0].astype(BF16)
    t = i * tq + lax.broadcasted_iota(jnp.int32, (tq, nch), 0)
    n = lax.broadcasted_iota(jnp.int32, (tq, nch), 1)
    valid = (t - CMP_STRIDE * n - (CMP_BLOCK - 1) >= 0) & (n < n_cmp)
    pc = jnp.zeros((tq, nch), F32)
    for g in range(NSA_GROUP):
        sl = slice(g * NSA_HEAD_DIM, (g + 1) * NSA_HEAD_DIM)
        qg = (q_ref[:, sl] * (NSA_HEAD_DIM ** -0.5)).astype(BF16)
        s = _dot_nt(qg, kc) + bias_scr[g]
        s = jnp.where(valid, s, NEG_BIG)
        e = jnp.exp(s - jnp.max(s, axis=1, keepdims=True))
        p = jnp.where(valid, e / jnp.sum(e, axis=1, keepdims=True), 0.0)
        o_ref[:, sl] = _dot(p.astype(BF16), vc)
        pc = pc + p
    score_t = _dot_nt(ovt_ref[...], pc.astype(BF16))
    blk = lax.broadcasted_iota(jnp.int32, (n_slc, tq), 0)
    cur = (i * tq + lax.broadcasted_iota(jnp.int32, (n_slc, tq), 1)) // SLC_BLOCK
    forced = (blk == 0) | (blk == cur) | (blk == cur - 1)
    score_t = jnp.where(forced, FORCE_SCORE, score_t)
    score_t = jnp.where(blk > cur, -FORCE_SCORE, score_t)
    sel_ref[0, 0] = _top_n_mask(score_t, blk, n_slc)


def _cattn(u, kvc, rel_bias, batch, seq, tq):
    nch = kvc.shape[2]
    n_cmp = nch - 1
    n_slc = seq // SLC_BLOCK
    ni = seq // tq
    ovt = _overlap_t(n_cmp, nch, n_slc)
    qcol = COL_QNSA // (NSA_GROUP * NSA_HEAD_DIM)
    bands = tq // CMP_STRIDE
    strip_w = -(-(nch + bands - 1) // LANE) * LANE
    dist = (tq * np.arange(ni)[:, None, None] + np.arange(CMP_STRIDE)[None, :, None]
            - CMP_STRIDE * (np.arange(strip_w)[None, None, :] - (bands - 1)) - (CMP_BLOCK - 1))
    buckets = jnp.asarray(_bucket_np(dist))
    return pl.pallas_call(
        functools.partial(_cattn_kernel, tq=tq, nch=nch, n_cmp=n_cmp, n_slc=n_slc),
        grid=(NSA_KV_HEADS, ni, batch),
        in_specs=[pl.BlockSpec(memory_space=pltpu.SMEM),
                  pl.BlockSpec((tq, NSA_GROUP * NSA_HEAD_DIM), lambda h, i, b: (b * ni + i, qcol + h)),
                  pl.BlockSpec((1, 1, nch, NSA_HEAD_DIM), lambda h, i, b: (b, h, 0, 0)),
                  pl.BlockSpec((1, 1, nch, NSA_HEAD_DIM), lambda h, i, b: (b, NSA_KV_HEADS + h, 0, 0)),
                  pl.BlockSpec((1, CMP_STRIDE, strip_w), lambda h, i, b: (i, 0, 0)),
                  pl.BlockSpec((n_slc, nch), lambda h, i, b: (0, 0))],
        out_specs=[pl.BlockSpec((tq, NSA_GROUP * NSA_HEAD_DIM), lambda h, i, b: (b * ni + i, h)),
                   pl.BlockSpec((1, 1, n_slc, tq), lambda h, i, b: (b, h, 0, i))],
        out_shape=[jax.ShapeDtypeStruct((batch * seq, D_NSA), F32),
                   jax.ShapeDtypeStruct((batch, NSA_KV_HEADS, n_slc, seq), F32)],
        scratch_shapes=[pltpu.VMEM((NSA_GROUP, tq, nch), F32), pltpu.VMEM((NSA_GROUP, CMP_STRIDE, strip_w), F32)],
        compiler_params=_cparams(("arbitrary", "arbitrary", "arbitrary")),
        name="cmp_attn",
    )(rel_bias.astype(F32), u, kvc, kvc, buckets, ovt)


TQ = 128
FLASH_GROUP = 4


def _flash_kernel(tab_ref, q_ref, kv_ref, bk_ref, *rest, selected, n_delta):
    if selected:
        sel_ref, o_ref, band_scr, qs_scr, m_scr, l_scr, acc_scr = rest
    else:
        o_ref, band_scr, qs_scr, m_scr, l_scr, acc_scr = rest
    h = pl.program_id(0)
    i = pl.program_id(1)
    n_batch = q_ref.shape[0]

    @pl.when(i == 0)
    def _():
        def fill(d, carry):
            for g, bias in enumerate(_bias_from_buckets(bk_ref[d], tab_ref, h * NSA_GROUP)):
                band_scr[d, :, g * TQ:(g + 1) * TQ] = bias
            return carry
        lax.fori_loop(0, n_delta, fill, 0)

    for bb in range(n_batch):
        for g in range(NSA_GROUP):
            qg = (q_ref[bb, :, g * NSA_HEAD_DIM:(g + 1) * NSA_HEAD_DIM] * (NSA_HEAD_DIM ** -0.5)).astype(BF16)
            qs_scr[bb, g * TQ:(g + 1) * TQ, :] = jnp.concatenate([qg, jnp.zeros_like(qg)], axis=1)
    is_key_lane = lax.broadcasted_iota(jnp.int32, (TQ, LANE), 1) < NSA_HEAD_DIM
    m_scr[...] = jnp.full(m_scr.shape, 0.5 * NEG_BIG, F32)
    l_scr[...] = jnp.zeros(l_scr.shape, F32)
    acc_scr[...] = jnp.zeros(acc_scr.shape, F32)
    key = lax.broadcasted_iota(jnp.int32, (TQ, TQ), 0)
    qry = lax.broadcasted_iota(jnp.int32, (TQ, TQ), 1)
    n_back = WINDOW // TQ

    def scores(bb, j, kind):
        kvj = kv_ref[bb, pl.ds(pl.multiple_of(j * TQ, TQ), TQ), :].astype(BF16)
        s = _dot_nt(kvj, qs_scr[bb]) + band_scr[i - j]
        mask = None
        if kind == "diag":
            mask = qry >= key
        elif kind == "far":
            mask = qry < key
        if selected:
            r = sel_ref[bb, 0, pl.ds(2 * j, 2), :]
            picked = jnp.where(key < SLC_BLOCK, r[0:1, :], r[1:2, :]) > 0.5
            mask = picked if mask is None else (mask & picked)
        if mask is not None:
            s = jnp.concatenate([jnp.where(mask, s[:, g * TQ:(g + 1) * TQ], NEG_BIG) for g in range(NSA_GROUP)], axis=1)
        return s, jnp.where(is_key_lane, jnp.ones_like(kvj), kvj)

    def update(tile_ids):
        all_tiles = [[scores(bb, j, kind) for j, kind in tile_ids] for bb in range(n_batch)]
        for bb, tiles in enumerate(all_tiles):
            m_old = m_scr[bb]
            m_new = m_old
            for s, _ in tiles:
                m_new = jnp.maximum(m_new, jnp.max(s, axis=0, keepdims=True))
            alpha = jnp.exp(m_old - m_new)
            l_new = alpha * l_scr[bb]
            acc = alpha * acc_scr[bb]
            for s, ones_v in tiles:
                p = jnp.exp(s - m_new)
                pv = _dot_tn(ones_v, p.astype(BF16))
                l_new = l_new + pv[0:1, :]
                acc = acc + pv[NSA_HEAD_DIM:2 * NSA_HEAD_DIM, :]
            m_scr[bb] = m_new
            l_scr[bb] = l_new
            acc_scr[bb] = acc

    def full_tiles(first, count):
        return [(first + k, "full") for k in range(count)]

    if selected:
        def group(t, carry):
            update(full_tiles(FLASH_GROUP * t, FLASH_GROUP))
            return carry

        n_groups = i // FLASH_GROUP
        lax.fori_loop(0, n_groups, group, 0)
        for rem in range(FLASH_GROUP):
            @pl.when(i - n_groups * FLASH_GROUP == rem)
            def _():
                update(full_tiles(n_groups * FLASH_GROUP, rem) + [(i, "diag")])
    else:
        @pl.when(i >= n_back)
        def _():
            update([(i - n_back, "far")] + full_tiles(i - n_back + 1, n_back - 1) + [(i, "diag")])

        for rem in range(n_back):
            @pl.when(i == rem)
            def _():
                update(full_tiles(0, rem) + [(i, "diag")])

    for bb in range(n_batch):
        for g in range(NSA_GROUP):
            cols = slice(g * TQ, (g + 1) * TQ)
            o_ref[bb, :, g * NSA_HEAD_DIM:(g + 1) * NSA_HEAD_DIM] = (acc_scr[bb, :, cols] / l_scr[bb, :, cols]).T


def _flash(u, kv_col, rel_bias, sel_t, batch, seq):
    ni = seq // TQ
    qcol = COL_QNSA // (NSA_GROUP * NSA_HEAD_DIM)
    selected = sel_t is not None
    n_delta = ni if selected else WINDOW // TQ + 1
    kvblk = kv_col // LANE
    delta = np.arange(n_delta)[:, None, None] * TQ + np.arange(TQ)[None, None, :] - np.arange(TQ)[None, :, None]
    buckets = jnp.asarray(_bucket_np(delta))
    u3 = u.reshape(batch, seq, u.shape[1])
    in_specs = [pl.BlockSpec(memory_space=pltpu.SMEM),
                pl.BlockSpec((batch, TQ, NSA_GROUP * NSA_HEAD_DIM), lambda h, i: (0, i, qcol + h)),
                pl.BlockSpec((batch, seq, LANE), lambda h, i: (0, 0, kvblk + h)),
                pl.BlockSpec((n_delta, TQ, TQ), lambda h, i: (0, 0, 0))]
    args = [rel_bias.astype(F32), u3, u3, buckets]
    if selected:
        n_slc = sel_t.shape[2]
        in_specs.append(pl.BlockSpec((batch, 1, n_slc, TQ), lambda h, i: (0, h, 0, i)))
        args.append(sel_t)
    scratch = [pltpu.VMEM((n_delta, TQ, NSA_GROUP * TQ), F32),
               pltpu.VMEM((batch, NSA_GROUP * TQ, LANE), BF16),
               pltpu.VMEM((batch, 1, NSA_GROUP * TQ), F32),
               pltpu.VMEM((batch, 1, NSA_GROUP * TQ), F32),
               pltpu.VMEM((batch, NSA_HEAD_DIM, NSA_GROUP * TQ), F32)]
    out = pl.pallas_call(
        functools.partial(_flash_kernel, selected=selected, n_delta=n_delta),
        grid=(NSA_KV_HEADS, ni),
        in_specs=in_specs,
        out_specs=pl.BlockSpec((batch, TQ, NSA_GROUP * NSA_HEAD_DIM), lambda h, i: (0, i, h)),
        out_shape=jax.ShapeDtypeStruct((batch, seq, D_NSA), F32),
        scratch_shapes=scratch,
        compiler_params=_cparams(("arbitrary", "arbitrary")),
        name="slc_attn" if selected else "win_attn",
    )(*args)
    return out.reshape(batch * seq, D_NSA)


def _gate_expand():
    e = np.zeros((3, 2 * LANE, D_NSA), np.float32)
    for br in range(3):
        for hd in range(NSA_HEADS):
            for part in range(2):
                e[br, part * LANE + 2 * ML_HEADS + br * NSA_HEADS + hd,
                  hd * NSA_HEAD_DIM:(hd + 1) * NSA_HEAD_DIM] = 1.0
    return jnp.asarray(e, BF16)


def _outproj_kernel(yml_ref, oc_ref, os_ref, ow_ref, s_ref, e_ref, ng_ref, w_ref, x_ref, o_ref, cat_scr):
    @pl.when(pl.program_id(1) == 0)
    def _():
        sig = jax.nn.sigmoid(s_ref[...])
        hi_lo = jnp.concatenate(_split2(sig), axis=1)
        o = jnp.zeros(oc_ref.shape, F32)
        for br, ref in enumerate((oc_ref, os_ref, ow_ref)):
            o = o + _dot(hi_lo, e_ref[br]) * ref[...]
        o = o * lax.rsqrt(jnp.mean(o * o, axis=-1, keepdims=True) + EPS) * ng_ref[...]
        cat_scr[:, 0:D_ML] = yml_ref[...]
        cat_scr[:, D_ML:D_ML + D_NSA] = o.astype(BF16)

    o_ref[...] = x_ref[...] + _dot(cat_scr[...], w_ref[...])


def _outproj(y_ml, o_c, o_s, o_w, small, nsa_g, w_out_b, x2d, tm, tn):
    n, d = x2d.shape
    rows = lambda i, j: (i, 0)
    return pl.pallas_call(
        _outproj_kernel,
        grid=(n // tm, d // tn),
        in_specs=[pl.BlockSpec((tm, D_ML), rows), pl.BlockSpec((tm, D_NSA), rows),
                  pl.BlockSpec((tm, D_NSA), rows), pl.BlockSpec((tm, D_NSA), rows),
                  pl.BlockSpec((tm, LANE), rows),
                  pl.BlockSpec((3, 2 * LANE, D_NSA), lambda i, j: (0, 0, 0)),
                  pl.BlockSpec((1, D_NSA), lambda i, j: (0, 0)),
                  pl.BlockSpec((D_ML + D_NSA, tn), lambda i, j: (0, j)),
                  pl.BlockSpec((tm, tn), lambda i, j: (i, j))],
        out_specs=pl.BlockSpec((tm, tn), lambda i, j: (i, j)),
        out_shape=jax.ShapeDtypeStruct((n, d), F32),
        scratch_shapes=[pltpu.VMEM((tm, D_ML + D_NSA), BF16)],
        compiler_params=_cparams(("arbitrary", "arbitrary")),
        name="outproj",
    )(y_ml, o_c, o_s, o_w, small, _gate_expand(), nsa_g, w_out_b, x2d)


ROUTE_COL = N_GROUPS
BIG_COL = 1 << 20
SPLIT = D_MODEL // LANE
DMA_UNROLL = 8


def _store_split(ref, val, n):
    for k in range(SPLIT):
        ref[pl.ds(k, n, stride=SPLIT), :] = val[:, k * LANE:(k + 1) * LANE]


def _load_split(ref, n):
    return jnp.concatenate([ref[pl.ds(k, n, stride=SPLIT), :] for k in range(SPLIT)], axis=1)


def _route_kernel(y_ref, g_ref, w_ref, b_ref, h_ref, info_ref, cnt_ref, carry_scr, *, tm):
    @pl.when(pl.program_id(0) == 0)
    def _():
        carry_scr[...] = jnp.zeros(carry_scr.shape, F32)

    x = y_ref[...]
    h = x * lax.rsqrt(jnp.mean(x * x, axis=-1, keepdims=True) + EPS) * g_ref[...]
    _store_split(h_ref, h, tm)
    logit = _dot(h.astype(BF16), w_ref[...]) + b_ref[...]
    col = lax.broadcasted_iota(jnp.int32, logit.shape, 1)
    is_grp = col < N_GROUPS
    gmax = jnp.max(jnp.where(is_grp, logit, -jnp.inf), axis=1, keepdims=True)
    gtop = jnp.min(jnp.where(is_grp & (logit == gmax), col, BIG_COL), axis=1, keepdims=True)
    gsum = jnp.sum(jnp.where(is_grp, jnp.exp(logit - gmax), 0.0), axis=1, keepdims=True)
    first = ROUTE_COL + gtop * EXPERTS_PER_GROUP
    in_grp = (col >= first) & (col < first + EXPERTS_PER_GROUP)
    v1 = jnp.max(jnp.where(in_grp, logit, -jnp.inf), axis=1, keepdims=True)
    i1 = jnp.min(jnp.where(in_grp & (logit == v1), col, BIG_COL), axis=1, keepdims=True)
    rest = in_grp & (col != i1)
    v2 = jnp.max(jnp.where(rest, logit, -jnp.inf), axis=1, keepdims=True)
    i2 = jnp.min(jnp.where(rest & (logit == v2), col, BIG_COL), axis=1, keepdims=True)
    e = jnp.exp(v2 - v1)
    w1 = 1.0 / ((1.0 + e) * gsum)
    w2 = e / ((1.0 + e) * gsum)
    pick1 = col == i1
    pick2 = col == i2
    both = (pick1 | pick2).astype(F32)
    ri = lax.broadcasted_iota(jnp.int32, (tm, tm), 0)
    ci = lax.broadcasted_iota(jnp.int32, (tm, tm), 1)
    before = (ci < ri).astype(BF16)
    cum = _dot(before, both.astype(BF16)) + carry_scr[...]
    r1 = jnp.sum(jnp.where(pick1, cum, 0.0), axis=1, keepdims=True)
    r2 = jnp.sum(jnp.where(pick2, cum, 0.0), axis=1, keepdims=True)
    carry_scr[...] = carry_scr[...] + jnp.sum(both, axis=0, keepdims=True)
    cnt_ref[...] = jnp.broadcast_to(carry_scr[...], cnt_ref.shape)
    info = jnp.where(col == 0, (i1 - ROUTE_COL).astype(F32), 0.0)
    info = jnp.where(col == 1, (i2 - ROUTE_COL).astype(F32), info)
    info = jnp.where(col == 2, w1, info)
    info = jnp.where(col == 3, w2, info)
    info = jnp.where(col == 4, r1, info)
    info_ref[...] = jnp.where(col == 5, r2, info)


def _route(y2d, g, w_rg, b_rg, w_re, b_re, tm):
    n, d = y2d.shape
    wr = jnp.concatenate([w_rg, w_re, jnp.zeros((d, LANE - N_GROUPS - N_EXPERTS), F32)], axis=1).astype(BF16)
    bias = jnp.concatenate([b_rg, b_re, jnp.zeros((LANE - N_GROUPS - N_EXPERTS,), F32)])[None, :]
    rows = lambda i: (i, 0)
    fixed = lambda i: (0, 0)
    return pl.pallas_call(
        functools.partial(_route_kernel, tm=tm),
        grid=(n // tm,),
        in_specs=[pl.BlockSpec((tm, d), rows), pl.BlockSpec((1, d), fixed),
                  pl.BlockSpec((d, LANE), fixed), pl.BlockSpec((1, LANE), fixed)],
        out_specs=[pl.BlockSpec((tm * SPLIT, LANE), rows), pl.BlockSpec((tm, LANE), rows),
                   pl.BlockSpec((8, LANE), fixed)],
        out_shape=[jax.ShapeDtypeStruct((n * SPLIT, LANE), F32), jax.ShapeDtypeStruct((n, LANE), F32),
                   jax.ShapeDtypeStruct((8, LANE), F32)],
        scratch_shapes=[pltpu.VMEM((1, LANE), F32)],
        compiler_params=_cparams(("arbitrary",)),
        name="moe_route",
    )(y2d, g, wr, bias)


W_CHUNKS = 4


def _expert_kernel(te_ref, nu_ref, src_ref, nx_ref, ws_ref, h_hbm, wg_hbm, wu_hbm, wd_hbm, o_ref,
                   xbuf, wg_f, wu_f, wd_f, wg_b, wu_b, wd_b, sem, wsem, *, tme):
    i = pl.program_id(0)
    slot = i % 2

    def weight_copies(expert, wslot):
        copies = []
        for hbm, buf in ((wg_hbm, wg_f), (wu_hbm, wu_f), (wd_hbm, wd_f)):
            step = hbm.shape[1] // W_CHUNKS
            for c in range(W_CHUNKS):
                copies.append(pltpu.make_async_copy(hbm.at[expert, pl.ds(c * step, step)],
                                                    buf.at[wslot, pl.ds(c * step, step)], wsem.at[wslot]))
        return copies

    def row_copy(tile, r, buf_slot):
        src_row = src_ref[tile * tme + r]
        return pltpu.make_async_copy(h_hbm.at[pl.ds(pl.multiple_of(src_row * SPLIT, SPLIT), SPLIT)],
                                     xbuf.at[buf_slot, pl.ds(pl.multiple_of(r * SPLIT, SPLIT), SPLIT)],
                                     sem.at[buf_slot])

    def fetch(tile, buf_slot):
        def body(r, carry):
            row_copy(tile, r, buf_slot).start()
            return carry
        lax.fori_loop(0, tme, body, 0, unroll=DMA_UNROLL)

    @pl.when(i == 0)
    def _():
        fetch(0, 0)
        for cp in weight_copies(te_ref[0], ws_ref[0]):
            cp.start()

    @pl.when(i + 1 < nu_ref[0])
    def _():
        fetch(i + 1, 1 - slot)

    @pl.when(i < nu_ref[0])
    def _():
        prev = te_ref[jnp.maximum(i - 1, 0)]

        @pl.when((i == 0) | (te_ref[i] != prev))
        def _():
            w = ws_ref[i]
            for cp in weight_copies(te_ref[i], w):
                cp.wait()
            wg_b[...] = wg_f[w].astype(BF16)
            wu_b[...] = wu_f[w].astype(BF16)
            wd_b[...] = wd_f[w].astype(BF16)

            @pl.when(nx_ref[i] >= 0)
            def _():
                for cp in weight_copies(nx_ref[i], 1 - w):
                    cp.start()

        def wait_body(r, carry):
            row_copy(i, r, slot).wait()
            return carry
        lax.fori_loop(0, tme, wait_body, 0, unroll=DMA_UNROLL)

        x = _load_split(xbuf.at[slot], tme).astype(BF16)
        a = _dot(x, wg_b[...])
        u = _dot(x, wu_b[...])
        hid = a * jax.nn.sigmoid(a) * u
        _store_split(o_ref, _dot(hid.astype(BF16), wd_b[...]), tme)

    @pl.when(i >= nu_ref[0])
    def _():
        o_ref[...] = jnp.zeros(o_ref.shape, F32)


def _experts(h, src, tile_expert, n_used, next_expert, weight_slot, w_gate, w_up, w_down, tme):
    p = src.shape[0]
    d = D_MODEL
    hbm = pl.BlockSpec(memory_space=pl.ANY)
    return pl.pallas_call(
        functools.partial(_expert_kernel, tme=tme),
        grid_spec=pltpu.PrefetchScalarGridSpec(
            num_scalar_prefetch=5, grid=(p // tme,),
            in_specs=[hbm, hbm, hbm, hbm],
            out_specs=pl.BlockSpec((tme * SPLIT, LANE), lambda i, *_: (i, 0)),
            scratch_shapes=[pltpu.VMEM((2, tme * SPLIT, LANE), F32),
                            pltpu.VMEM((2, d, D_EXPERT), F32), pltpu.VMEM((2, d, D_EXPERT), F32),
                            pltpu.VMEM((2, D_EXPERT, d), F32),
                            pltpu.VMEM((d, D_EXPERT), BF16), pltpu.VMEM((d, D_EXPERT), BF16),
                            pltpu.VMEM((D_EXPERT, d), BF16),
                            pltpu.SemaphoreType.DMA((2,)), pltpu.SemaphoreType.DMA((2,))]),
        out_shape=jax.ShapeDtypeStruct((p * SPLIT, LANE), F32),
        compiler_params=_cparams(("arbitrary",)),
        name="moe_experts",
    )(tile_expert, n_used, src, next_expert, weight_slot, h, w_gate, w_up, w_down)


def _combine_kernel(d1_ref, d2_ref, y_ref, info_ref, ys_hbm, g_ref, o_ref, a_buf, b_buf, sem, *, tm):
    i = pl.program_id(0)
    slot = i % 2

    def copy(r, row, buf, buf_slot):
        return pltpu.make_async_copy(ys_hbm.at[pl.ds(pl.multiple_of(row * SPLIT, SPLIT), SPLIT)],
                                     buf.at[buf_slot, pl.ds(pl.multiple_of(r * SPLIT, SPLIT), SPLIT)],
                                     sem.at[buf_slot])

    def fetch(tile, buf_slot):
        def issue(r, carry):
            copy(r, d1_ref[tile * tm + r], a_buf, buf_slot).start()
            copy(r, d2_ref[tile * tm + r], b_buf, buf_slot).start()
            return carry
        lax.fori_loop(0, tm, issue, 0, unroll=DMA_UNROLL)

    @pl.when(i == 0)
    def _():
        fetch(0, 0)

    @pl.when(i + 1 < pl.num_programs(0))
    def _():
        fetch(i + 1, 1 - slot)

    def drain(r, carry):
        copy(r, 0, a_buf, slot).wait()
        copy(r, 0, b_buf, slot).wait()
        return carry

    lax.fori_loop(0, tm, drain, 0, unroll=DMA_UNROLL)
    w1 = info_ref[:, 2:3]
    w2 = info_ref[:, 3:4]
    y = y_ref[...] + w1 * _load_split(a_buf.at[slot], tm) + w2 * _load_split(b_buf.at[slot], tm)
    o_ref[...] = y * lax.rsqrt(jnp.mean(y * y, axis=-1, keepdims=True) + EPS) * g_ref[...]


def _combine(y2d, info, ys, d1, d2, g, tm):
    n, d = y2d.shape
    return pl.pallas_call(
        functools.partial(_combine_kernel, tm=tm),
        grid_spec=pltpu.PrefetchScalarGridSpec(
            num_scalar_prefetch=2, grid=(n // tm,),
            in_specs=[pl.BlockSpec((tm, d), lambda i, a, b: (i, 0)),
                      pl.BlockSpec((tm, LANE), lambda i, a, b: (i, 0)),
                      pl.BlockSpec(memory_space=pl.ANY),
                      pl.BlockSpec((1, d), lambda i, a, b: (0, 0))],
            out_specs=pl.BlockSpec((tm, d), lambda i, a, b: (i, 0)),
            scratch_shapes=[pltpu.VMEM((2, tm * SPLIT, LANE), F32), pltpu.VMEM((2, tm * SPLIT, LANE), F32),
                            pltpu.SemaphoreType.DMA((2,))]),
        out_shape=jax.ShapeDtypeStruct((n, d), F32),
        compiler_params=_cparams(("arbitrary",)),
        name="moe_combine",
    )(d1, d2, y2d, info, ys, g)


def _moe_final(y2d, ffn_g, w_rg, b_rg, w_re, b_re, w_gate, w_up, w_down, final_g, tm, tme):
    n, d = y2d.shape
    h, info, cnt = _route(y2d, ffn_g, w_rg, b_rg, w_re, b_re, tm)
    e1 = info[:, 0].astype(jnp.int32)
    e2 = info[:, 1].astype(jnp.int32)
    counts = cnt[0, ROUTE_COL:ROUTE_COL + N_EXPERTS].astype(jnp.int32)
    padded = (counts + tme - 1) // tme * tme
    ends = jnp.cumsum(padded)
    offs = ends - padded
    d1 = offs[e1] + info[:, 4].astype(jnp.int32)
    d2 = offs[e2] + info[:, 5].astype(jnp.int32)
    p = 2 * n + N_EXPERTS * tme
    tok = jnp.arange(n, dtype=jnp.int32)
    src = jnp.zeros((p,), jnp.int32).at[jnp.concatenate([d1, d2])].set(jnp.concatenate([tok, tok]))
    n_tiles = p // tme
    n_used = (ends[-1] // tme).astype(jnp.int32).reshape(1)
    tile_start = jnp.arange(n_tiles, dtype=jnp.int32) * tme
    tile_expert = jnp.sum((ends[None, :] <= tile_start[:, None]).astype(jnp.int32), axis=1)
    last = jnp.sum((ends <= ends[-1] - 1).astype(jnp.int32))
    tile_expert = jnp.minimum(tile_expert, last).astype(jnp.int32)
    run = jnp.cumsum(jnp.concatenate([jnp.zeros((1,), jnp.int32),
                                      (tile_expert[1:] != tile_expert[:-1]).astype(jnp.int32)]))
    weight_slot = (run % 2).astype(jnp.int32)
    after = ends[tile_expert] // tme
    next_expert = jnp.where(after < n_used[0], tile_expert[jnp.minimum(after, n_tiles - 1)], -1).astype(jnp.int32)
    ys = _experts(h, src, tile_expert, n_used, next_expert, weight_slot, w_gate, w_up, w_down, tme)
    return _combine(y2d, info, ys, d1, d2, final_g, tm)


GROUP_ROWS = 8
SLC_LANES = 384
SEL_KEYS = 1024
WIN_KEYS = 640


def _sample_cattn_kernel(q_ref, kc_ref, vc_ref, b_ref, ov_ref, o_ref, sel_ref, *, n_cmp, n_slc):
    q = (q_ref[0, 0] * (NSA_HEAD_DIM ** -0.5)).astype(BF16)
    s = _dot_nt(q, kc_ref[0, 0].astype(BF16)) + b_ref[0]
    valid = lax.broadcasted_iota(jnp.int32, s.shape, 1) < n_cmp
    s = jnp.where(valid, s, NEG_BIG)
    e = jnp.exp(s - jnp.max(s, axis=1, keepdims=True))
    p = jnp.where(valid, e / jnp.sum(e, axis=1, keepdims=True), 0.0)
    o_ref[0, 0] = _dot(p.astype(BF16), vc_ref[0, 0].astype(BF16))
    pg = jnp.where(lax.broadcasted_iota(jnp.int32, p.shape, 0) < NSA_GROUP, p, 0.0)
    pc = jnp.broadcast_to(jnp.sum(pg, axis=0, keepdims=True), pg.shape)
    score = _dot(pc.astype(BF16), ov_ref[...])[0:1, :]
    blk = lax.broadcasted_iota(jnp.int32, score.shape, 1)
    cur = n_slc - 1
    forced = (blk == 0) | (blk == cur) | (blk == cur - 1)
    score = jnp.where(forced, FORCE_SCORE, score)
    score = jnp.where(blk > cur, -jnp.inf, score)
    ri = lax.broadcasted_iota(jnp.int32, (SLC_LANES, SLC_LANES), 0)
    ci = lax.broadcasted_iota(jnp.int32, (SLC_LANES, SLC_LANES), 1)
    sb = jnp.broadcast_to(score, (SLC_LANES, SLC_LANES))
    col = jnp.sum(jnp.where(ri == ci, sb, 0.0), axis=1, keepdims=True)
    beats = (col > sb) | ((col == sb) & (ri < ci))
    rank = jnp.sum(beats.astype(jnp.int32), axis=0, keepdims=True)
    sel_ref[0, 0] = jnp.broadcast_to((rank < SLC_TOPN).astype(F32), (GROUP_ROWS, SLC_LANES))


def _sample_cattn(q8, kvc, bias_cs, n_cmp, n_slc):
    bsz, _, nch, _ = kvc.shape
    c0 = np.arange(nch) * CMP_STRIDE
    s0 = np.arange(SLC_LANES) * SLC_BLOCK
    ov = np.minimum(c0[:, None] + CMP_BLOCK, s0[None, :] + SLC_BLOCK) - np.maximum(c0[:, None], s0[None, :])
    ov = np.clip(ov, 0, None).astype(np.float32) / CMP_BLOCK
    ov[n_cmp:, :] = 0.0
    ov[:, n_slc:] = 0.0
    return pl.pallas_call(
        functools.partial(_sample_cattn_kernel, n_cmp=n_cmp, n_slc=n_slc),
        grid=(bsz, NSA_KV_HEADS),
        in_specs=[pl.BlockSpec((1, 1, GROUP_ROWS, NSA_HEAD_DIM), lambda b, h: (b, h, 0, 0)),
                  pl.BlockSpec((1, 1, nch, NSA_HEAD_DIM), lambda b, h: (b, h, 0, 0)),
                  pl.BlockSpec((1, 1, nch, NSA_HEAD_DIM), lambda b, h: (b, NSA_KV_HEADS + h, 0, 0)),
                  pl.BlockSpec((1, GROUP_ROWS, nch), lambda b, h: (h, 0, 0)),
                  pl.BlockSpec((nch, SLC_LANES), lambda b, h: (0, 0))],
        out_specs=[pl.BlockSpec((1, 1, GROUP_ROWS, NSA_HEAD_DIM), lambda b, h: (b, h, 0, 0)),
                   pl.BlockSpec((1, 1, GROUP_ROWS, SLC_LANES), lambda b, h: (b, h, 0, 0))],
        out_shape=[jax.ShapeDtypeStruct((bsz, NSA_KV_HEADS, GROUP_ROWS, NSA_HEAD_DIM), F32),
                   jax.ShapeDtypeStruct((bsz, NSA_KV_HEADS, GROUP_ROWS, SLC_LANES), F32)],
        compiler_params=_cparams(("arbitrary", "arbitrary")),
        name="sample_cmp_attn",
    )(q8, kvc, kvc, bias_cs, jnp.asarray(ov, BF16))


def _sample_attn_kernel(pg_ref, hf_ref, q_ref, cache_hbm, snew_ref, bs_ref, win_ref, wnew_ref, bw_ref,
                        os_ref, ow_ref, kt_buf, vt_buf, sem, *, n_gather, wlen):
    b = pl.program_id(0)
    rows_per_page = 2 * NSA_KV_HEADS * NSA_HEAD_DIM

    def page_copy(h, s, c, buf):
        page = pg_ref[(b * NSA_KV_HEADS + h) * n_gather + s]
        start = pl.multiple_of(page * rows_per_page + (c * NSA_KV_HEADS + h) * NSA_HEAD_DIM, NSA_HEAD_DIM)
        return pltpu.make_async_copy(cache_hbm.at[pl.ds(start, NSA_HEAD_DIM)],
                                     buf.at[h, :, pl.ds(s * PAGE_SIZE, PAGE_SIZE)], sem)

    for h in range(NSA_KV_HEADS):
        for s in range(n_gather):
            page_copy(h, s, 0, kt_buf).start()
            page_copy(h, s, 1, vt_buf).start()

    def attend(q, kts, vts, bias, valid):
        s = jnp.concatenate([_dot(q, kt.astype(BF16)) for kt in kts], axis=1) + bias
        s = jnp.where(valid, s, NEG_BIG)
        e = jnp.exp(s - jnp.max(s, axis=1, keepdims=True))
        p = jnp.where(valid, e / jnp.sum(e, axis=1, keepdims=True), 0.0).astype(BF16)
        out = None
        lo = 0
        for vt in vts:
            n = vt.shape[1]
            part = _dot_nt(p[:, lo:lo + n], vt.astype(BF16))
            out = part if out is None else out + part
            lo += n
        return out

    def head_rows(ref, c, h):
        r0 = (c * NSA_KV_HEADS + h) * NSA_HEAD_DIM
        return ref[0, r0:r0 + NSA_HEAD_DIM, :]

    widx = lax.broadcasted_iota(jnp.int32, (GROUP_ROWS, wlen + PAGE_SIZE), 1)
    wvalid = (widx >= wlen + 1 - WINDOW) & (widx <= wlen)
    for h in range(NSA_KV_HEADS):
        q = (q_ref[0, h] * (NSA_HEAD_DIM ** -0.5)).astype(BF16)
        ow_ref[0, h] = attend(q, [head_rows(win_ref, 0, h), head_rows(wnew_ref, 0, h)],
                              [head_rows(win_ref, 1, h), head_rows(wnew_ref, 1, h)], bw_ref[h], wvalid)

    for h in range(NSA_KV_HEADS):
        for s in range(n_gather):
            page_copy(h, s, 0, kt_buf).wait()
            page_copy(h, s, 1, vt_buf).wait()

    keys = n_gather * PAGE_SIZE
    lane = lax.broadcasted_iota(jnp.int32, (GROUP_ROWS, keys + PAGE_SIZE), 1)
    slot = lane >> 7
    lane_half = (lane >> 6) & 1
    for h in range(NSA_KV_HEADS):
        want = jnp.full(lane.shape, -1, jnp.int32)
        for s in range(n_gather):
            want = jnp.where(slot == s, hf_ref[(b * NSA_KV_HEADS + h) * n_gather + s], want)
        svalid = (lane_half == want) | (lane == keys)
        q = (q_ref[0, h] * (NSA_HEAD_DIM ** -0.5)).astype(BF16)
        os_ref[0, h] = attend(q, [kt_buf[h], head_rows(snew_ref, 0, h)],
                              [vt_buf[h], head_rows(snew_ref, 1, h)], bs_ref[0, h], svalid)


def _sample_attn(pages, halves, q8, cache_t, snew_t, bias_sel, win_t, wnew_t, bias_w, n_gather):
    bsz = q8.shape[0]
    wlen = win_t.shape[2]
    keys = n_gather * PAGE_SIZE
    rows = 2 * NSA_KV_HEADS * NSA_HEAD_DIM
    qspec = pl.BlockSpec((1, NSA_KV_HEADS, GROUP_ROWS, NSA_HEAD_DIM), lambda b, pg, hf: (b, 0, 0, 0))
    newspec = pl.BlockSpec((1, rows, PAGE_SIZE), lambda b, pg, hf: (b, 0, 0))
    return pl.pallas_call(
        functools.partial(_sample_attn_kernel, n_gather=n_gather, wlen=wlen),
        grid_spec=pltpu.PrefetchScalarGridSpec(
            num_scalar_prefetch=2, grid=(bsz,),
            in_specs=[qspec,
                      pl.BlockSpec(memory_space=pl.ANY),
                      newspec,
                      pl.BlockSpec((1, NSA_KV_HEADS, GROUP_ROWS, keys + PAGE_SIZE), lambda b, pg, hf: (b, 0, 0, 0)),
                      pl.BlockSpec((1, rows, wlen), lambda b, pg, hf: (b, 0, 0)),
                      newspec,
                      pl.BlockSpec((NSA_KV_HEADS, GROUP_ROWS, wlen + PAGE_SIZE), lambda b, pg, hf: (0, 0, 0))],
            out_specs=[qspec, qspec],
            scratch_shapes=[pltpu.VMEM((NSA_KV_HEADS, NSA_HEAD_DIM, keys), F32),
                            pltpu.VMEM((NSA_KV_HEADS, NSA_HEAD_DIM, keys), F32),
                            pltpu.SemaphoreType.DMA(())]),
        out_shape=[jax.ShapeDtypeStruct(q8.shape, F32), jax.ShapeDtypeStruct(q8.shape, F32)],
        compiler_params=_cparams(("arbitrary",)),
        name="sample_slc_win_attn",
    )(pages, halves, q8, cache_t, snew_t, bias_sel, win_t, wnew_t, bias_w)


PAGES_PER_STEP = 16


def _paged_pq_kernel(pt_ref, cache_hbm, pos_ref, w_ref, o_ref, pbuf, tok_scr, sem):
    b = pl.program_id(0)
    g = pl.program_id(1)
    ng = pl.num_programs(1)
    lin = b * ng + g
    slot = lin % 2
    rows_per_page = 2 * NSA_KV_HEADS * NSA_HEAD_DIM
    chunks = PAGES_PER_STEP * (PAGE_SIZE // CMP_STRIDE)

    def page_copy(step, p, buf_slot):
        sb = step // ng
        sg = step - sb * ng
        page = pt_ref[sb, sg * PAGES_PER_STEP + p]
        return pltpu.make_async_copy(cache_hbm.at[pl.ds(pl.multiple_of(page * rows_per_page, rows_per_page),
                                                        rows_per_page)],
                                     pbuf.at[buf_slot, p], sem.at[buf_slot])

    def fetch(step, buf_slot):
        for p in range(PAGES_PER_STEP):
            page_copy(step, p, buf_slot).start()

    @pl.when(lin == 0)
    def _():
        fetch(0, 0)

    @pl.when(lin + 1 < pl.num_programs(0) * ng)
    def _():
        fetch(lin + 1, 1 - slot)

    for p in range(PAGES_PER_STEP):
        page_copy(lin, p, slot).wait()

    for c in range(2):
        head_chunks = []
        for pair in (2 * c, 2 * c + 1):
            for p in range(PAGES_PER_STEP):
                tok_scr[p * PAGE_SIZE:(p + 1) * PAGE_SIZE, :] = pbuf[slot, p, pair * LANE:(pair + 1) * LANE, :].T
            toks = [tok_scr[pl.ds(l, chunks, stride=CMP_STRIDE), :] for l in range(CMP_STRIDE)]
            for half in range(2):
                lanes = slice(half * NSA_HEAD_DIM, (half + 1) * NSA_HEAD_DIM)
                head_chunks.append(jnp.concatenate([t[:, lanes] for t in toks], axis=1))
        stacked = jnp.concatenate(head_chunks, axis=0)
        for part in range(2):
            lhs = (stacked + pos_ref[c, part:part + 1, :]).astype(BF16)
            out = _dot(lhs, w_ref[c, part])
            for hh in range(NSA_KV_HEADS):
                o_ref[0, c * NSA_KV_HEADS + hh, :, part * CMP_HIDDEN:(part + 1) * CMP_HIDDEN] = (
                    out[hh * chunks:(hh + 1) * chunks, :])


def _paged_pq(cache_t, page_table, cmp_pos, cmp_w1):
    db, n_pages = page_table.shape
    kdim = CMP_STRIDE * NSA_HEAD_DIM
    nch = n_pages * (PAGE_SIZE // CMP_STRIDE)
    tn = PAGES_PER_STEP * (PAGE_SIZE // CMP_STRIDE)
    rows_per_page = 2 * NSA_KV_HEADS * NSA_HEAD_DIM
    return pl.pallas_call(
        _paged_pq_kernel,
        grid_spec=pltpu.PrefetchScalarGridSpec(
            num_scalar_prefetch=1, grid=(db, n_pages // PAGES_PER_STEP),
            in_specs=[pl.BlockSpec(memory_space=pl.ANY),
                      pl.BlockSpec((2, 2, kdim), lambda b, g, pt: (0, 0, 0)),
                      pl.BlockSpec((2, 2, kdim, CMP_HIDDEN), lambda b, g, pt: (0, 0, 0, 0))],
            out_specs=pl.BlockSpec((1, 8, tn, 2 * CMP_HIDDEN), lambda b, g, pt: (b, 0, g, 0)),
            scratch_shapes=[pltpu.VMEM((2, PAGES_PER_STEP, rows_per_page, PAGE_SIZE), F32),
                            pltpu.VMEM((PAGES_PER_STEP * PAGE_SIZE, LANE), F32),
                            pltpu.SemaphoreType.DMA((2,))]),
        out_shape=jax.ShapeDtypeStruct((db, 8, nch, 2 * CMP_HIDDEN), F32),
        compiler_params=_cparams(("arbitrary", "arbitrary")),
        name="cmp_pq_paged",
    )(page_table, cache_t, cmp_pos.reshape(2, 2, kdim), cmp_w1.reshape(2, 2, kdim, CMP_HIDDEN).astype(BF16))


def _pad_rows(a, n):
    return jnp.concatenate([a, jnp.zeros((n - a.shape[0],) + a.shape[1:], a.dtype)], axis=0)


def _sample_mixer(x_sample, cache_cmp, cache_slc, cache_win, st_c, st_n, st_m, st_conv, page_table,
                  norm_g, wb, gate_bias, conv_w, ml_norm_g, cmp_pos, cmp_w1, cmp_w2, nsa_norm_g, w_out_b, rel_bias):
    db = x_sample.shape[0]
    n_pages = page_table.shape[1]
    past = n_pages * PAGE_SIZE
    tok = 16
    x16 = _pad_rows(x_sample.reshape(db, D_MODEL), tok)
    u = _proj(x16, norm_g, wb, tok, 512)
    small = u[:, COL_SMALL:COL_SMALL + LANE]

    T = 128
    useq = jnp.zeros((db, T, 4 * D_ML), F32)
    useq = useq.at[:, T - CONV_W:T - 1, 0:2 * D_ML].set(st_conv)
    useq = useq.at[:, T - 1, :].set(u[:db, 0:4 * D_ML])
    sseq = jnp.zeros((db, T, LANE), F32).at[:, :T - 1, LANE - 1].set(1.0)
    sseq = sseq.at[:, T - 1, :].set(small[:db])
    y_seq, conv_n, c_n, n_n, m_n = _mlstm(
        useq.reshape(db * T, 4 * D_ML), sseq.reshape(db * T, LANE), jnp.zeros((db, CONV_W - 1, 2 * D_ML), F32),
        st_c, st_n, st_m, conv_w, gate_bias, ml_norm_g, db, T, T)
    y_ml = y_seq.reshape(db, T, D_ML)[:, T - 1]

    n_pool = cache_cmp.shape[0]
    cmp_t = cache_cmp.transpose(0, 2, 3, 4, 1).reshape(n_pool * 2 * D_KV, PAGE_SIZE)
    kvc = _cmp_hid(_paged_pq(cmp_t, page_table, cmp_pos, cmp_w1), cmp_w2)
    nch = past // CMP_STRIDE
    n_cmp = (past + 1) // CMP_STRIDE - CMP_BLOCK // CMP_STRIDE + 1
    n_slc = -(-(past + 1) // SLC_BLOCK)
    q = u[:db, COL_QNSA:COL_QNSA + D_NSA].reshape(db, NSA_KV_HEADS, NSA_GROUP, NSA_HEAD_DIM)
    q8 = jnp.concatenate([q, jnp.zeros_like(q)], axis=2)
    bd = _bias_by_distance(rel_bias, past + 1).reshape(NSA_KV_HEADS, NSA_GROUP, past + 1)
    pad_g = lambda a: jnp.concatenate([a, jnp.zeros_like(a)], axis=1)
    dist_c = np.clip(past - (np.arange(nch) * CMP_STRIDE + CMP_BLOCK - 1), 0, None)
    o_c8, sel = _sample_cattn(q8, kvc, pad_g(bd[:, :, dist_c]), n_cmp, n_slc)

    mask = sel[:, :, 0, :n_slc] > 0.5
    idx = jnp.sort(jnp.where(mask, jnp.arange(n_slc, dtype=jnp.int32), jnp.int32(1 << 20)), axis=-1)
    n_gather = SLC_TOPN - 1
    idx = idx[..., :n_gather]
    pages_per_block = PAGE_SIZE // SLC_BLOCK
    logical_page = idx // pages_per_block
    pages = jnp.take_along_axis(page_table[:, None, :], logical_page, axis=2).reshape(-1).astype(jnp.int32)
    halves = (idx % pages_per_block).reshape(-1).astype(jnp.int32)
    kpos = (logical_page[..., None] * PAGE_SIZE + jnp.arange(PAGE_SIZE, dtype=jnp.int32)).reshape(db, NSA_KV_HEADS, -1)
    hh = jnp.arange(NSA_KV_HEADS)[None, :, None, None]
    gg = jnp.arange(NSA_GROUP)[None, None, :, None]
    bias_sel = bd[hh, gg, (past - kpos)[:, :, None, :]]
    bias_sel = jnp.concatenate([bias_sel, jnp.broadcast_to(bd[None, :, :, 0:1], (db, NSA_KV_HEADS, NSA_GROUP, 1)),
                                jnp.zeros((db, NSA_KV_HEADS, NSA_GROUP, PAGE_SIZE - 1), F32)], axis=-1)
    bias_sel = jnp.concatenate([bias_sel, jnp.zeros_like(bias_sel)], axis=2)
    wlen = cache_win.shape[1]
    dist_w = np.clip(wlen - np.arange(wlen + PAGE_SIZE), 0, None)
    kvs_new = _kv_from_per_head(u[:db, COL_KVS:COL_KVS + 2 * D_KV])
    kvw_new = _kv_from_per_head(u[:db, COL_KVW:COL_KVW + 2 * D_KV])
    lane_pad = lambda a: jnp.pad(a[:, :, None], ((0, 0), (0, 0), (0, PAGE_SIZE - 1)))
    slc_t = cache_slc.transpose(0, 2, 3, 4, 1).reshape(n_pool * 2 * D_KV, PAGE_SIZE)
    win_t = cache_win.transpose(0, 2, 3, 4, 1).reshape(db, 2 * D_KV, wlen)
    o_s8, o_w8 = _sample_attn(pages, halves, q8, slc_t, lane_pad(kvs_new), bias_sel, win_t, lane_pad(kvw_new),
                              pad_g(bd[:, :, dist_w]), n_gather)
    win2d = cache_win.reshape(db, wlen, 2 * D_KV)

    heads = lambda o: _pad_rows(o[:, :, :NSA_GROUP, :].reshape(db, D_NSA), tok)
    y = _outproj(_pad_rows(y_ml, tok), heads(o_c8), heads(o_s8), heads(o_w8), small, nsa_norm_g, w_out_b, x16, tok, 512)
    kvshape = (1, db, 1, 2, NSA_KV_HEADS, NSA_HEAD_DIM)
    new_win = jnp.concatenate([win2d[:, 1:], kvw_new[:, None, :]], axis=1)
    states = (u[:db, COL_KVC:COL_KVC + 2 * D_KV].reshape(kvshape), kvs_new.reshape(kvshape),
              new_win.reshape((1, db, wlen, 2, NSA_KV_HEADS, NSA_HEAD_DIM)),
              c_n[None], n_n[None], m_n[None], conv_n[None])
    return y, states


def kernel(x_prompt, x_sample, cache_cmp_kv, cache_slc_kv, cache_win_kv, state_mlstm_C, state_mlstm_n,
           state_mlstm_m, state_conv, page_table, rel_bias, norm_mix_g, w_in, b_ig, b_fg, conv_w, ml_norm_g,
           cmp_pos, cmp_w1, cmp_w2, nsa_norm_g, w_out, norm_ffn_g, w_router_grp, b_router_grp, w_router_exp,
           b_router_exp, w_gate, w_up, w_down, norm_final_g):
    B, S, D = x_prompt.shape
    wb = _reorder_w_in(w_in[0])
    gate_bias = jnp.zeros((1, LANE), F32).at[0, 0:ML_HEADS].set(b_ig[0]).at[0, ML_HEADS:2 * ML_HEADS].set(b_fg[0])
    w_out_b = w_out[0].astype(BF16)
    yp, st_p = _prompt_mixer(x_prompt, norm_mix_g, wb, gate_bias, conv_w[0], ml_norm_g, cmp_pos[0], cmp_w1[0],
                             cmp_w2[0], nsa_norm_g, w_out_b, rel_bias)
    ys, st_s = _sample_mixer(x_sample, cache_cmp_kv[0], cache_slc_kv[0], cache_win_kv[0], state_mlstm_C[0],
                             state_mlstm_n[0], state_mlstm_m[0], state_conv[0], page_table, norm_mix_g, wb, gate_bias,
                             conv_w[0], ml_norm_g, cmp_pos[0], cmp_w1[0], cmp_w2[0], nsa_norm_g, w_out_b, rel_bias)
    moe_w = (norm_ffn_g, w_router_grp[0], b_router_grp[0], w_router_exp[0], b_router_exp[0],
             w_gate[0], w_up[0], w_down[0], norm_final_g[None, :])
    DB, L, _ = x_sample.shape
    out_p = _moe_final(yp.reshape(B * S, D), *moe_w, 256, 128).reshape(B, S, D)
    out_s = _moe_final(_pad_rows(ys, 128), *moe_w, 128, 16)[:DB].reshape(DB, L, D)
    outs = [out_p, out_s]
    for a, b in zip(st_p, st_s):
        outs += [a, b]
    return tuple(outs)


def _prompt_mixer(x_prompt, norm_g, wb, gate_bias, conv_w, ml_norm_g, cmp_pos, cmp_w1, cmp_w2, nsa_norm_g,
                  w_out_b, rel_bias):
    B, S, D = x_prompt.shape
    x2d = x_prompt.reshape(B * S, D)
    tm = min(1024, S)
    u, kvt_c, kvt_s, kvt_w = _proj(x2d, norm_g, wb, tm, 2 * D_KV, seq=S)
    small = u[:, COL_SMALL:COL_SMALL + LANE]
    y_ml, conv_n, c_n, n_n, m_n = _mlstm(
        u, small, jnp.zeros((B, CONV_W - 1, 2 * D_ML), F32),
        jnp.zeros((B, ML_HEADS, ML_HEAD_DIM, ML_HEAD_DIM), F32), jnp.zeros((B, ML_HEADS, ML_HEAD_DIM), F32),
        jnp.full((B, ML_HEADS), -jnp.inf, F32), conv_w, gate_bias, ml_norm_g, B, S, 256)
    kvc = _compress(u, COL_KVC // (2 * D_KV), B, S, cmp_pos, cmp_w1, cmp_w2, min(S, 4096))
    tq_c = min(S, 256)
    o_c, sel_t = _cattn(u, kvc, rel_bias, B, S, tq_c)
    o_s = _flash(u, COL_KVS, rel_bias, sel_t, B, S)
    o_w = _flash(u, COL_KVW, rel_bias, None, B, S)
    y = _outproj(y_ml, o_c, o_s, o_w, small, nsa_norm_g, w_out_b, x2d, tm, 512)
    win = min(WINDOW, S)
    rows = lambda kvt: kvt.transpose(0, 4, 1, 2, 3)[None]
    states = (rows(kvt_c), rows(kvt_s), rows(kvt_w[..., S - win:]),
              c_n[None], n_n[None], m_n[None], conv_n[None])
    return y.reshape(B, S, D), states
```

```python
import functools
import math

import numpy as np
import jax
import jax.numpy as jnp
from jax import lax
from jax.experimental import pallas as pl
from jax.experimental.pallas import tpu as pltpu

F32 = jnp.float32
BF16 = jnp.bfloat16

D_MODEL = 2048
ML_HEADS = 4
ML_HEAD_DIM = 256
D_ML = 1024
CONV_W = 4
NSA_HEADS = 16
NSA_HEAD_DIM = 64
D_NSA = 1024
NSA_KV_HEADS = 4
NSA_GROUP = 4
D_KV = 256
CMP_BLOCK = 32
CMP_STRIDE = 16
CMP_HIDDEN = 256
SLC_BLOCK = 64
SLC_TOPN = 16
WINDOW = 512
N_BUCKETS = 32
MAX_DISTANCE = 2048
N_GROUPS = 4
EXPERTS_PER_GROUP = 8
N_EXPERTS = 32
D_EXPERT = 512
PAGE_SIZE = 128
EPS = 1e-6
NEG_BIG = -1e30
FORCE_SCORE = 1e4

LANE = 128
COL_QML, COL_KML, COL_VML, COL_OML = 0, 1024, 2048, 3072
COL_QNSA = 4096
COL_KVC, COL_KVS, COL_KVW = 5120, 5632, 6144
COL_SMALL = 6656
N_PROJ = 7168
VMEM_LIMIT = 56 * 1024 * 1024


def _cparams(sem, vmem=VMEM_LIMIT):
    return pltpu.CompilerParams(dimension_semantics=sem, vmem_limit_bytes=vmem)


def _split2(x):
    hi = x.astype(BF16)
    lo = (x - hi.astype(F32)).astype(BF16)
    return hi, lo


def _split3(x):
    hi = x.astype(BF16)
    r = x - hi.astype(F32)
    mid = r.astype(BF16)
    lo = (r - mid.astype(F32)).astype(BF16)
    return hi, mid, lo


def _dot(a, b):
    return jnp.dot(a, b, preferred_element_type=F32)


def _dot_nt(a, b):
    return lax.dot_general(a, b, (((1,), (1,)), ((), ())), preferred_element_type=F32)


def _dot_tn(a, b):
    return lax.dot_general(a, b, (((0,), (0,)), ((), ())), preferred_element_type=F32)


def _proj_kernel(x_ref, g_ref, w_ref, o_ref, *rest, kv_blocks):
    h_scr = rest[-1]
    j = pl.program_id(1)

    @pl.when(j == 0)
    def _():
        x = x_ref[...]
        ms = jnp.mean(x * x, axis=-1, keepdims=True)
        h_scr[...] = (x * lax.rsqrt(ms + EPS) * g_ref[...]).astype(BF16)

    res = _dot(h_scr[...], w_ref[...])
    o_ref[...] = res

    for (block, per_head), t_ref in zip(kv_blocks, rest[:-1]):
        @pl.when(j == block)
        def _():
            res_t = res.T
            for c in range(2):
                for hh in range(NSA_KV_HEADS):
                    piece = hh * 2 + c if per_head else c * NSA_KV_HEADS + hh
                    t_ref[0, c, hh] = res_t[piece * NSA_HEAD_DIM:(piece + 1) * NSA_HEAD_DIM, :]


def _proj(x2d, g, wb, tm, tn, seq=None):
    n, d = x2d.shape
    nc = wb.shape[1]
    kv_blocks = ()
    out_specs = [pl.BlockSpec((tm, tn), lambda i, j: (i, j))]
    out_shape = [jax.ShapeDtypeStruct((n, nc), F32)]
    if seq is not None:
        assert tn == 2 * D_KV and seq % tm == 0
        kv_blocks = ((COL_KVC // tn, False), (COL_KVS // tn, True), (COL_KVW // tn, True))
        per_seq = seq // tm
        for _ in kv_blocks:
            out_specs.append(pl.BlockSpec((1, 2, NSA_KV_HEADS, NSA_HEAD_DIM, tm),
                                          lambda i, j: (i // per_seq, 0, 0, 0, i % per_seq)))
            out_shape.append(jax.ShapeDtypeStruct((n // seq, 2, NSA_KV_HEADS, NSA_HEAD_DIM, seq), F32))
    outs = pl.pallas_call(
        functools.partial(_proj_kernel, kv_blocks=kv_blocks),
        grid=(n // tm, nc // tn),
        in_specs=[pl.BlockSpec((tm, d), lambda i, j: (i, 0)),
                  pl.BlockSpec((1, d), lambda i, j: (0, 0)),
                  pl.BlockSpec((d, tn), lambda i, j: (0, j))],
        out_specs=out_specs,
        out_shape=out_shape,
        scratch_shapes=[pltpu.VMEM((tm, d), BF16)],
        compiler_params=_cparams(("arbitrary", "arbitrary")),
        name="proj",
    )(x2d, g, wb)
    return outs if seq is not None else outs[0]


def _reorder_w_in(w_in):
    w_in = w_in.astype(BF16)
    big = w_in[:, :4 * D_ML]
    small_a = w_in[:, 4 * D_ML:4 * D_ML + 2 * ML_HEADS]
    rest = w_in[:, 4 * D_ML + 2 * ML_HEADS:]
    q_and_cmp = rest[:, :D_NSA + 2 * D_KV]
    gate = rest[:, D_NSA + 6 * D_KV:]
    d = w_in.shape[0]

    def per_head(w):
        return w.reshape(d, 2, NSA_KV_HEADS, NSA_HEAD_DIM).transpose(0, 2, 1, 3).reshape(d, 2 * D_KV)

    kv_s = per_head(rest[:, D_NSA + 2 * D_KV:D_NSA + 4 * D_KV])
    kv_w = per_head(rest[:, D_NSA + 4 * D_KV:D_NSA + 6 * D_KV])
    pad = jnp.zeros((d, N_PROJ - COL_SMALL - 2 * ML_HEADS - 3 * NSA_HEADS), w_in.dtype)
    return jnp.concatenate([big, q_and_cmp, kv_s, kv_w, small_a, gate, pad], axis=1).astype(BF16)


def _kv_from_per_head(kv2d):
    n = kv2d.shape[0]
    return kv2d.reshape(n, NSA_KV_HEADS, 2, NSA_HEAD_DIM).transpose(0, 2, 1, 3).reshape(n, 2 * D_KV)


def _log_sigmoid(x):
    return jnp.minimum(x, 0.0) - jnp.log1p(jnp.exp(-jnp.abs(x)))


def _mlstm_kernel(q_ref, k_ref, v_ref, o_ref, s_ref, cb_ref, c0_ref, n0_ref, m0_ref,
                  cw_ref, gb_ref, ng_ref,
                  y_ref, cbo_ref, co_ref, no_ref, mo_ref,
                  ext_scr, c_scr, n_scr, m_scr, *, T):
    c = pl.program_id(1)
    nc = pl.num_programs(1)

    @pl.when(c == 0)
    def _():
        ext_scr[0:8, :] = jnp.zeros((8, 2 * D_ML), F32)
        ext_scr[5:8, :] = cb_ref[0]
        c_scr[...] = c0_ref[0]
        n_scr[...] = n0_ref[0]
        m_scr[...] = m0_ref[0]

    ext_scr[8:8 + T, 0:D_ML] = q_ref[...]
    ext_scr[8:8 + T, D_ML:2 * D_ML] = k_ref[...]
    conv = ext_scr[5:5 + T, :] * cw_ref[0:1, :]
    for j in range(1, CONV_W):
        conv = conv + ext_scr[5 + j:5 + j + T, :] * cw_ref[j:j + 1, :]
    tail = ext_scr[8 + T - 3:8 + T, :]
    ext_scr[5:8, :] = tail
    cbo_ref[0] = tail
    qk = conv * jax.nn.sigmoid(conv)

    pre = s_ref[...] + gb_ref[...]
    col = lax.broadcasted_iota(jnp.int32, pre.shape, 1)
    padrow = s_ref[:, LANE - 1:LANE] > 0.5
    gates = jnp.where(col < ML_HEADS, pre, _log_sigmoid(pre))
    gates = jnp.where(padrow, jnp.where(col < ML_HEADS, NEG_BIG, 0.0), gates)
    g_r = gates.T
    ti = lax.broadcasted_iota(jnp.int32, (T, T), 0)
    si = lax.broadcasted_iota(jnp.int32, (T, T), 1)
    upper = (ti <= si).astype(BF16)
    g_fin = jnp.where(lax.broadcasted_iota(jnp.int32, g_r.shape, 0) < ML_HEADS, 0.0, g_r)
    hi, mid, lo = _split3(g_fin)
    cum_r = _dot(hi, upper) + _dot(mid, upper) + _dot(lo, upper)
    rowi = lax.broadcasted_iota(jnp.int32, g_r.shape, 0)
    a_r = jnp.where(rowi < ML_HEADS, g_r, cum_r)
    a_c = a_r.T
    causal = si <= ti

    for h in range(ML_HEADS):
        sl = slice(h * ML_HEAD_DIM, (h + 1) * ML_HEAD_DIM)
        q = qk[:, h * ML_HEAD_DIM:(h + 1) * ML_HEAD_DIM]
        k = qk[:, D_ML + h * ML_HEAD_DIM:D_ML + (h + 1) * ML_HEAD_DIM] * (ML_HEAD_DIM ** -0.5)
        v = v_ref[:, sl]
        ig_r = a_r[h:h + 1, :]
        b_r = a_r[ML_HEADS + h:ML_HEADS + h + 1, :]
        ig_c = a_c[:, h:h + 1]
        b_c = a_c[:, ML_HEADS + h:ML_HEADS + h + 1]
        m_prev = m_scr[h:h + 1, 0:1]
        logd = jnp.where(causal, b_c - b_r + ig_r, -jnp.inf)
        inter = b_c + m_prev
        m_t = jnp.maximum(inter, jnp.max(logd, axis=1, keepdims=True))
        w_intra = jnp.exp(logd - m_t)
        w_inter = jnp.exp(inter - m_t)
        qb = q.astype(BF16)
        kb = k.astype(BF16)
        vb = v.astype(BF16)
        sc = _dot_nt(qb, kb) * w_intra
        cmat = c_scr[h]
        nvec = n_scr[h:h + 1, :]
        num = _dot(sc.astype(BF16), vb) + w_inter * _dot(qb, cmat.astype(BF16))
        qn = jnp.sum(qb.astype(F32) * nvec.astype(BF16).astype(F32), axis=1, keepdims=True)
        den = jnp.sum(sc, axis=1, keepdims=True) + w_inter * qn
        hh = num / jnp.maximum(jnp.abs(den), jnp.exp(-m_t))
        m_new = m_t[T - 1:T, :]
        b_last = b_c[T - 1:T, :]
        w_s = jnp.exp(b_last - b_c + ig_c - m_new)
        decay = jnp.exp(b_last + m_prev - m_new)
        kw = k * w_s
        c_new = decay * cmat + _dot_tn(kw.astype(BF16), vb)
        n_new = decay * nvec + jnp.sum(kw, axis=0, keepdims=True)
        c_scr[h] = c_new
        n_scr[h:h + 1, :] = n_new
        m_scr[h:h + 1, :] = jnp.broadcast_to(m_new, (1, LANE))
        hn = hh * lax.rsqrt(jnp.mean(hh * hh, axis=1, keepdims=True) + EPS) * ng_ref[:, sl]
        y_ref[:, sl] = (hn * jax.nn.sigmoid(o_ref[:, sl])).astype(y_ref.dtype)

    @pl.when(c == nc - 1)
    def _():
        co_ref[0] = c_scr[...]
        no_ref[0] = n_scr[...]
        mo_ref[0] = m_scr[...]


def _mlstm(u, small, conv_buf, c0, n0, m0, conv_w, gate_bias, norm_g, batch, seq, T):
    nc = seq // T
    cb = D_ML // 1024
    m0b = jnp.broadcast_to(m0[:, :, None], (batch, ML_HEADS, LANE))
    m0b = jnp.concatenate([m0b, jnp.zeros((batch, 8 - ML_HEADS, LANE), F32)], axis=1)
    n0p = jnp.concatenate([n0, jnp.zeros((batch, 8 - ML_HEADS, ML_HEAD_DIM), F32)], axis=1)
    row = lambda b, c: (b * nc + c, 0)
    outs = pl.pallas_call(
        functools.partial(_mlstm_kernel, T=T),
        grid=(batch, nc),
        in_specs=[pl.BlockSpec((T, D_ML), lambda b, c: (b * nc + c, COL_QML // D_ML)),
                  pl.BlockSpec((T, D_ML), lambda b, c: (b * nc + c, COL_KML // D_ML)),
                  pl.BlockSpec((T, D_ML), lambda b, c: (b * nc + c, COL_VML // D_ML)),
                  pl.BlockSpec((T, D_ML), lambda b, c: (b * nc + c, COL_OML // D_ML)),
                  pl.BlockSpec((T, LANE), row),
                  pl.BlockSpec((1, CONV_W - 1, 2 * D_ML), lambda b, c: (b, 0, 0)),
                  pl.BlockSpec((1, ML_HEADS, ML_HEAD_DIM, ML_HEAD_DIM), lambda b, c: (b, 0, 0, 0)),
                  pl.BlockSpec((1, 8, ML_HEAD_DIM), lambda b, c: (b, 0, 0)),
                  pl.BlockSpec((1, 8, LANE), lambda b, c: (b, 0, 0)),
                  pl.BlockSpec((CONV_W, 2 * D_ML), lambda b, c: (0, 0)),
                  pl.BlockSpec((1, LANE), lambda b, c: (0, 0)),
                  pl.BlockSpec((1, D_ML), lambda b, c: (0, 0))],
        out_specs=[pl.BlockSpec((T, D_ML), row),
                   pl.BlockSpec((1, CONV_W - 1, 2 * D_ML), lambda b, c: (b, 0, 0)),
                   pl.BlockSpec((1, ML_HEADS, ML_HEAD_DIM, ML_HEAD_DIM), lambda b, c: (b, 0, 0, 0)),
                   pl.BlockSpec((1, 8, ML_HEAD_DIM), lambda b, c: (b, 0, 0)),
                   pl.BlockSpec((1, 8, LANE), lambda b, c: (b, 0, 0))],
        out_shape=[jax.ShapeDtypeStruct((batch * seq, D_ML), BF16),
                   jax.ShapeDtypeStruct((batch, CONV_W - 1, 2 * D_ML), F32),
                   jax.ShapeDtypeStruct((batch, ML_HEADS, ML_HEAD_DIM, ML_HEAD_DIM), F32),
                   jax.ShapeDtypeStruct((batch, 8, ML_HEAD_DIM), F32),
                   jax.ShapeDtypeStruct((batch, 8, LANE), F32)],
        scratch_shapes=[pltpu.VMEM((8 + T, 2 * D_ML), F32),
                        pltpu.VMEM((ML_HEADS, ML_HEAD_DIM, ML_HEAD_DIM), F32),
                        pltpu.VMEM((8, ML_HEAD_DIM), F32),
                        pltpu.VMEM((8, LANE), F32)],
        compiler_params=_cparams(("arbitrary", "arbitrary")),
        name="mlstm",
    )(u, u, u, u, small, conv_buf, c0, n0p, m0b, conv_w, gate_bias, norm_g)
    y, cbo, co, no, mo = outs
    return y, cbo, co, no[:, :ML_HEADS], mo[:, :ML_HEADS, 0]


def _bucket_np(dist):
    n = np.maximum(dist, 0)
    max_exact = N_BUCKETS // 2
    nf = np.maximum(n, 1).astype(np.float64)
    large = max_exact + (np.log(nf / max_exact) / math.log(MAX_DISTANCE / max_exact)
                         * (N_BUCKETS - max_exact)).astype(np.int64)
    return np.where(n < max_exact, n, np.minimum(large, N_BUCKETS - 1)).astype(np.int32)


def _bias_by_distance(rel_bias, n):
    return rel_bias.astype(F32)[_bucket_np(np.arange(n))].T


def _overlap_t(n_cmp, nch, n_slc):
    c0 = np.arange(nch) * CMP_STRIDE
    s0 = np.arange(n_slc) * SLC_BLOCK
    ov = np.minimum(c0[None, :] + CMP_BLOCK, s0[:, None] + SLC_BLOCK) - np.maximum(c0[None, :], s0[:, None])
    ov = np.clip(ov, 0, None).astype(np.float32) / CMP_BLOCK
    ov[:, n_cmp:] = 0.0
    return jnp.asarray(ov, BF16)


def _pq_kernel(x0_ref, x1_ref, x2_ref, x3_ref, pos_ref, w_ref, o_ref, *, rows):
    for pair, x_ref in enumerate((x0_ref, x1_ref, x2_ref, x3_ref)):
        toks = [x_ref[pl.ds(l, rows, stride=CMP_STRIDE), :] for l in range(CMP_STRIDE)]
        for half in range(2):
            ch = 2 * pair + half
            c = ch // NSA_KV_HEADS
            lanes = slice(half * NSA_HEAD_DIM, (half + 1) * NSA_HEAD_DIM)
            chunk = jnp.concatenate([t[:, lanes] for t in toks], axis=1)
            for part in range(2):
                lhs = (chunk + pos_ref[c, part:part + 1, :]).astype(BF16)
                o_ref[0, ch, :, part * CMP_HIDDEN:(part + 1) * CMP_HIDDEN] = _dot(lhs, w_ref[c, part])


def _hid_kernel(pq_ref, w2_ref, o_ref, *, nch):
    p = pq_ref[0, 0, :, 0:CMP_HIDDEN]
    q = pltpu.roll(pq_ref[0, 0, :, CMP_HIDDEN:2 * CMP_HIDDEN], nch - 1, 0)
    hid = jax.nn.gelu(p + q, approximate=True)
    o_ref[0, 0] = _dot(hid.astype(BF16), w2_ref[0])


def _compress(kv2d, col_block, bsz, seq, cmp_pos, cmp_w1, cmp_w2, tt):
    nch = seq // CMP_STRIDE
    kdim = CMP_STRIDE * NSA_HEAD_DIM
    nt = seq // tt
    tn = tt // CMP_STRIDE
    w1 = cmp_w1.reshape(2, 2, kdim, CMP_HIDDEN).astype(BF16)
    pos = cmp_pos.reshape(2, 2, kdim)
    pq = pl.pallas_call(
        functools.partial(_pq_kernel, rows=tn),
        grid=(bsz, nt),
        in_specs=[pl.BlockSpec((tt, LANE), functools.partial(lambda b, i, k: (b * nt + i, col_block * 4 + k), k=k))
                  for k in range(4)] +
                 [pl.BlockSpec((2, 2, kdim), lambda b, i: (0, 0, 0)),
                  pl.BlockSpec((2, 2, kdim, CMP_HIDDEN), lambda b, i: (0, 0, 0, 0))],
        out_specs=pl.BlockSpec((1, 8, tn, 2 * CMP_HIDDEN), lambda b, i: (b, 0, i, 0)),
        out_shape=jax.ShapeDtypeStruct((bsz, 8, nch, 2 * CMP_HIDDEN), F32),
        compiler_params=_cparams(("arbitrary", "arbitrary")),
        name="cmp_pq",
    )(kv2d, kv2d, kv2d, kv2d, pos, w1)
    return _cmp_hid(pq, cmp_w2)


def _cmp_hid(pq, cmp_w2):
    bsz, _, nch, _ = pq.shape
    return pl.pallas_call(
        functools.partial(_hid_kernel, nch=nch),
        grid=(bsz, 8),
        in_specs=[pl.BlockSpec((1, 1, nch, 2 * CMP_HIDDEN), lambda b, c: (b, c, 0, 0)),
                  pl.BlockSpec((1, CMP_HIDDEN, NSA_HEAD_DIM), lambda b, c: (c // NSA_KV_HEADS, 0, 0))],
        out_specs=pl.BlockSpec((1, 1, nch, NSA_HEAD_DIM), lambda b, c: (b, c, 0, 0)),
        out_shape=jax.ShapeDtypeStruct((bsz, 8, nch, NSA_HEAD_DIM), F32),
        compiler_params=_cparams(("arbitrary", "arbitrary")),
        name="cmp_hid",
    )(pq, cmp_w2.astype(BF16))


def _top_n_mask(score_t, blk, n_rows):
    rank = jnp.zeros(score_t.shape, jnp.int32)
    for i in range(n_rows):
        row = score_t[i:i + 1, :]
        beats = (row > score_t) | ((row == score_t) & (blk > i))
        rank = rank + beats.astype(jnp.int32)
    return (rank < min(SLC_TOPN, n_rows)).astype(F32)


def _bias_from_buckets(bucket, table_ref, first_head):
    biases = [jnp.zeros(bucket.shape, F32) for _ in range(NSA_GROUP)]
    for k in range(N_BUCKETS):
        hit = bucket == k
        biases = [jnp.where(hit, table_ref[k, first_head + g], biases[g]) for g in range(NSA_GROUP)]
    return biases


def _cattn_kernel(tab_ref, q_ref, kc_ref, vc_ref, bk_ref, ovt_ref, o_ref, sel_ref, bias_scr, strip_scr,
                  *, tq, nch, n_cmp, n_slc):
    h = pl.program_id(0)
    i = pl.program_id(1)

    bands = tq // CMP_STRIDE
    for g, strip in enumerate(_bias_from_buckets(bk_ref[0], tab_ref, h * NSA_GROUP)):
        strip_scr[g] = strip
        for a in range(bands):
            off = bands - 1 - a
            bias_scr[g, a * CMP_STRIDE:(a + 1) * CMP_STRIDE, :] = strip_scr[g, :, off:off + nch]

    t = i * tq + lax.broadcasted_iota(jnp.int32, (tq, nch), 0)
    n = lax.broadcasted_iota(jnp.int32, (tq, nch), 1)
    valid = (t - CMP_STRIDE * n - (CMP_BLOCK - 1) >= 0) & (n < n_cmp)
    blk = lax.broadcasted_iota(jnp.int32, (n_slc, tq), 0)
    cur = (i * tq + lax.broadcasted_iota(jnp.int32, (n_slc, tq), 1)) // SLC_BLOCK
    forced = (blk == 0) | (blk == cur) | (blk == cur - 1)
    for bb in range(q_ref.shape[0]):
        kc = kc_ref[bb, 0].astype(BF16)
        vc = vc_ref[bb, 0].astype(BF16)
        pc = jnp.zeros((tq, nch), F32)
        for g in range(NSA_GROUP):
            sl = slice(g * NSA_HEAD_DIM, (g + 1) * NSA_HEAD_DIM)
            qg = (q_ref[bb, :, sl] * (NSA_HEAD_DIM ** -0.5)).astype(BF16)
            s = _dot_nt(qg, kc) + bias_scr[g]
            s = jnp.where(valid, s, NEG_BIG)
            e = jnp.exp(s - jnp.max(s, axis=1, keepdims=True))
            p = jnp.where(valid, e / jnp.sum(e, axis=1, keepdims=True), 0.0)
            o_ref[bb, :, sl] = _dot(p.astype(BF16), vc)
            pc = pc + p
        score_t = _dot_nt(ovt_ref[...], pc.astype(BF16))
        score_t = jnp.where(forced, FORCE_SCORE, score_t)
        score_t = jnp.where(blk > cur, -FORCE_SCORE, score_t)
        sel_ref[bb, 0] = _top_n_mask(score_t, blk, n_slc)


def _cattn(u, kvc, rel_bias, batch, seq, tq):
    nch = kvc.shape[2]
    n_cmp = nch - 1
    n_slc = seq // SLC_BLOCK
    ni = seq // tq
    ovt = _overlap_t(n_cmp, nch, n_slc)
    qcol = COL_QNSA // (NSA_GROUP * NSA_HEAD_DIM)
    bands = tq // CMP_STRIDE
    strip_w = -(-(nch + bands - 1) // LANE) * LANE
    dist = (tq * np.arange(ni)[:, None, None] + np.arange(CMP_STRIDE)[None, :, None]
            - CMP_STRIDE * (np.arange(strip_w)[None, None, :] - (bands - 1)) - (CMP_BLOCK - 1))
    buckets = jnp.asarray(_bucket_np(dist))
    o_c, sel_t = pl.pallas_call(
        functools.partial(_cattn_kernel, tq=tq, nch=nch, n_cmp=n_cmp, n_slc=n_slc),
        grid=(NSA_KV_HEADS, ni),
        in_specs=[pl.BlockSpec(memory_space=pltpu.SMEM),
                  pl.BlockSpec((batch, tq, NSA_GROUP * NSA_HEAD_DIM), lambda h, i: (0, i, qcol + h)),
                  pl.BlockSpec((batch, 1, nch, NSA_HEAD_DIM), lambda h, i: (0, h, 0, 0)),
                  pl.BlockSpec((batch, 1, nch, NSA_HEAD_DIM), lambda h, i: (0, NSA_KV_HEADS + h, 0, 0)),
                  pl.BlockSpec((1, CMP_STRIDE, strip_w), lambda h, i: (i, 0, 0)),
                  pl.BlockSpec((n_slc, nch), lambda h, i: (0, 0))],
        out_specs=[pl.BlockSpec((batch, tq, NSA_GROUP * NSA_HEAD_DIM), lambda h, i: (0, i, h)),
                   pl.BlockSpec((batch, 1, n_slc, tq), lambda h, i: (0, h, 0, i))],
        out_shape=[jax.ShapeDtypeStruct((batch, seq, D_NSA), F32),
                   jax.ShapeDtypeStruct((batch, NSA_KV_HEADS, n_slc, seq), F32)],
        scratch_shapes=[pltpu.VMEM((NSA_GROUP, tq, nch), F32), pltpu.VMEM((NSA_GROUP, CMP_STRIDE, strip_w), F32)],
        compiler_params=_cparams(("arbitrary", "arbitrary")),
        name="cmp_attn",
    )(rel_bias.astype(F32), u.reshape(batch, seq, u.shape[1]), kvc, kvc, buckets, ovt)
    return o_c.reshape(batch * seq, D_NSA), sel_t


TQ = 128
FLASH_GROUP = 8


def _flash_kernel(tab_ref, q_ref, kv_ref, bk_ref, *rest, selected, n_delta):
    if selected:
        sel_ref, o_ref, band_scr, qs_scr, m_scr, l_scr, acc_scr = rest
    else:
        o_ref, band_scr, qs_scr, m_scr, l_scr, acc_scr = rest
    h = pl.program_id(0)
    i = pl.program_id(1)
    n_batch = q_ref.shape[0]

    @pl.when(i == 0)
    def _():
        def fill(d, carry):
            for g, bias in enumerate(_bias_from_buckets(bk_ref[d], tab_ref, h * NSA_GROUP)):
                band_scr[d, :, g * TQ:(g + 1) * TQ] = bias
            return carry
        lax.fori_loop(0, n_delta, fill, 0)

    for bb in range(n_batch):
        for g in range(NSA_GROUP):
            qg = (q_ref[bb, :, g * NSA_HEAD_DIM:(g + 1) * NSA_HEAD_DIM] * (NSA_HEAD_DIM ** -0.5)).astype(BF16)
            qs_scr[bb, g * TQ:(g + 1) * TQ, :] = jnp.concatenate([qg, jnp.zeros_like(qg)], axis=1)
    is_key_lane = lax.broadcasted_iota(jnp.int32, (TQ, LANE), 1) < NSA_HEAD_DIM
    m_scr[...] = jnp.full(m_scr.shape, 0.5 * NEG_BIG, F32)
    l_scr[...] = jnp.zeros(l_scr.shape, F32)
    acc_scr[...] = jnp.zeros(acc_scr.shape, F32)
    key = lax.broadcasted_iota(jnp.int32, (TQ, TQ), 0)
    qry = lax.broadcasted_iota(jnp.int32, (TQ, TQ), 1)
    n_back = WINDOW // TQ

    def scores(bb, j, kind):
        kvj = kv_ref[bb, pl.ds(pl.multiple_of(j * TQ, TQ), TQ), :].astype(BF16)
        s = _dot_nt(kvj, qs_scr[bb]) + band_scr[i - j]
        mask = None
        if kind == "diag":
            mask = qry >= key
        elif kind == "far":
            mask = qry < key
        if selected:
            r = sel_ref[bb, 0, pl.ds(2 * j, 2), :]
            picked = jnp.where(key < SLC_BLOCK, r[0:1, :], r[1:2, :]) > 0.5
            mask = picked if mask is None else (mask & picked)
        if mask is not None:
            s = jnp.concatenate([jnp.where(mask, s[:, g * TQ:(g + 1) * TQ], NEG_BIG) for g in range(NSA_GROUP)], axis=1)
        return s, jnp.where(is_key_lane, jnp.ones_like(kvj), kvj)

    def update(tile_ids):
        all_tiles = [[scores(bb, j, kind) for j, kind in tile_ids] for bb in range(n_batch)]
        for bb, tiles in enumerate(all_tiles):
            m_old = m_scr[bb]
            m_new = m_old
            for s, _ in tiles:
                m_new = jnp.maximum(m_new, jnp.max(s, axis=0, keepdims=True))
            alpha = jnp.exp(m_old - m_new)
            l_new = alpha * l_scr[bb]
            acc = alpha * acc_scr[bb]
            for s, ones_v in tiles:
                p = jnp.exp(s - m_new)
                pv = _dot_tn(ones_v, p.astype(BF16))
                l_new = l_new + pv[0:1, :]
                acc = acc + pv[NSA_HEAD_DIM:2 * NSA_HEAD_DIM, :]
            m_scr[bb] = m_new
            l_scr[bb] = l_new
            acc_scr[bb] = acc

    def full_tiles(first, count):
        return [(first + k, "full") for k in range(count)]

    if selected:
        def group(t, carry):
            update(full_tiles(FLASH_GROUP * t, FLASH_GROUP))
            return carry

        n_groups = i // FLASH_GROUP
        lax.fori_loop(0, n_groups, group, 0)
        for rem in range(FLASH_GROUP):
            @pl.when(i - n_groups * FLASH_GROUP == rem)
            def _():
                update(full_tiles(n_groups * FLASH_GROUP, rem) + [(i, "diag")])
    else:
        @pl.when(i >= n_back)
        def _():
            update([(i - n_back, "far")] + full_tiles(i - n_back + 1, n_back - 1) + [(i, "diag")])

        for rem in range(n_back):
            @pl.when(i == rem)
            def _():
                update(full_tiles(0, rem) + [(i, "diag")])

    for bb in range(n_batch):
        for g in range(NSA_GROUP):
            cols = slice(g * TQ, (g + 1) * TQ)
            o_ref[bb, :, g * NSA_HEAD_DIM:(g + 1) * NSA_HEAD_DIM] = (acc_scr[bb, :, cols] / l_scr[bb, :, cols]).T


def _flash(u, kv_col, rel_bias, sel_t, batch, seq):
    ni = seq // TQ
    qcol = COL_QNSA // (NSA_GROUP * NSA_HEAD_DIM)
    selected = sel_t is not None
    n_delta = ni if selected else WINDOW // TQ + 1
    kvblk = kv_col // LANE
    delta = np.arange(n_delta)[:, None, None] * TQ + np.arange(TQ)[None, None, :] - np.arange(TQ)[None, :, None]
    buckets = jnp.asarray(_bucket_np(delta))
    u3 = u.reshape(batch, seq, u.shape[1])
    in_specs = [pl.BlockSpec(memory_space=pltpu.SMEM),
                pl.BlockSpec((batch, TQ, NSA_GROUP * NSA_HEAD_DIM), lambda h, i: (0, i, qcol + h)),
                pl.BlockSpec((batch, seq, LANE), lambda h, i: (0, 0, kvblk + h)),
                pl.BlockSpec((n_delta, TQ, TQ), lambda h, i: (0, 0, 0))]
    args = [rel_bias.astype(F32), u3, u3, buckets]
    if selected:
        n_slc = sel_t.shape[2]
        in_specs.append(pl.BlockSpec((batch, 1, n_slc, TQ), lambda h, i: (0, h, 0, i)))
        args.append(sel_t)
    scratch = [pltpu.VMEM((n_delta, TQ, NSA_GROUP * TQ), F32),
               pltpu.VMEM((batch, NSA_GROUP * TQ, LANE), BF16),
               pltpu.VMEM((batch, 1, NSA_GROUP * TQ), F32),
               pltpu.VMEM((batch, 1, NSA_GROUP * TQ), F32),
               pltpu.VMEM((batch, NSA_HEAD_DIM, NSA_GROUP * TQ), F32)]
    out = pl.pallas_call(
        functools.partial(_flash_kernel, selected=selected, n_delta=n_delta),
        grid=(NSA_KV_HEADS, ni),
        in_specs=in_specs,
        out_specs=pl.BlockSpec((batch, TQ, NSA_GROUP * NSA_HEAD_DIM), lambda h, i: (0, i, h)),
        out_shape=jax.ShapeDtypeStruct((batch, seq, D_NSA), F32),
        scratch_shapes=scratch,
        compiler_params=_cparams(("arbitrary", "arbitrary")),
        name="slc_attn" if selected else "win_attn",
    )(*args)
    return out.reshape(batch * seq, D_NSA)


def _gate_expand():
    e = np.zeros((3, 2 * LANE, D_NSA), np.float32)
    for br in range(3):
        for hd in range(NSA_HEADS):
            for part in range(2):
                e[br, part * LANE + 2 * ML_HEADS + br * NSA_HEADS + hd,
                  hd * NSA_HEAD_DIM:(hd + 1) * NSA_HEAD_DIM] = 1.0
    return jnp.asarray(e, BF16)


def _outproj_kernel(yml_ref, oc_ref, os_ref, ow_ref, s_ref, e_ref, ng_ref, w_ref, x_ref, o_ref, cat_scr):
    @pl.when(pl.program_id(1) == 0)
    def _():
        sig = jax.nn.sigmoid(s_ref[...])
        hi_lo = jnp.concatenate(_split2(sig), axis=1)
        o = jnp.zeros(oc_ref.shape, F32)
        for br, ref in enumerate((oc_ref, os_ref, ow_ref)):
            o = o + _dot(hi_lo, e_ref[br]) * ref[...]
        o = o * lax.rsqrt(jnp.mean(o * o, axis=-1, keepdims=True) + EPS) * ng_ref[...]
        cat_scr[:, 0:D_ML] = yml_ref[...]
        cat_scr[:, D_ML:D_ML + D_NSA] = o.astype(BF16)

    o_ref[...] = x_ref[...] + _dot(cat_scr[...], w_ref[...])


def _outproj(y_ml, o_c, o_s, o_w, small, nsa_g, w_out_b, x2d, tm, tn):
    n, d = x2d.shape
    rows = lambda i, j: (i, 0)
    return pl.pallas_call(
        _outproj_kernel,
        grid=(n // tm, d // tn),
        in_specs=[pl.BlockSpec((tm, D_ML), rows), pl.BlockSpec((tm, D_NSA), rows),
                  pl.BlockSpec((tm, D_NSA), rows), pl.BlockSpec((tm, D_NSA), rows),
                  pl.BlockSpec((tm, LANE), rows),
                  pl.BlockSpec((3, 2 * LANE, D_NSA), lambda i, j: (0, 0, 0)),
                  pl.BlockSpec((1, D_NSA), lambda i, j: (0, 0)),
                  pl.BlockSpec((D_ML + D_NSA, tn), lambda i, j: (0, j)),
                  pl.BlockSpec((tm, tn), lambda i, j: (i, j))],
        out_specs=pl.BlockSpec((tm, tn), lambda i, j: (i, j)),
        out_shape=jax.ShapeDtypeStruct((n, d), F32),
        scratch_shapes=[pltpu.VMEM((tm, D_ML + D_NSA), BF16)],
        compiler_params=_cparams(("arbitrary", "arbitrary")),
        name="outproj",
    )(y_ml, o_c, o_s, o_w, small, _gate_expand(), nsa_g, w_out_b, x2d)


ROUTE_COL = N_GROUPS
BIG_COL = 1 << 20
SPLIT = D_MODEL // LANE
DMA_UNROLL = 8


def _store_split(ref, val, n):
    for k in range(SPLIT):
        ref[pl.ds(k, n, stride=SPLIT), :] = val[:, k * LANE:(k + 1) * LANE]


def _load_split(ref, n):
    return jnp.concatenate([ref[pl.ds(k, n, stride=SPLIT), :] for k in range(SPLIT)], axis=1)


def _route_kernel(y_ref, g_ref, w_ref, b_ref, h_ref, info_ref, cnt_ref, carry_scr, *, tm):
    @pl.when(pl.program_id(0) == 0)
    def _():
        carry_scr[...] = jnp.zeros(carry_scr.shape, F32)

    x = y_ref[...]
    h = x * lax.rsqrt(jnp.mean(x * x, axis=-1, keepdims=True) + EPS) * g_ref[...]
    _store_split(h_ref, h, tm)
    logit = _dot(h.astype(BF16), w_ref[...]) + b_ref[...]
    col = lax.broadcasted_iota(jnp.int32, logit.shape, 1)
    is_grp = col < N_GROUPS
    gmax = jnp.max(jnp.where(is_grp, logit, -jnp.inf), axis=1, keepdims=True)
    gtop = jnp.min(jnp.where(is_grp & (logit == gmax), col, BIG_COL), axis=1, keepdims=True)
    gsum = jnp.sum(jnp.where(is_grp, jnp.exp(logit - gmax), 0.0), axis=1, keepdims=True)
    first = ROUTE_COL + gtop * EXPERTS_PER_GROUP
    in_grp = (col >= first) & (col < first + EXPERTS_PER_GROUP)
    v1 = jnp.max(jnp.where(in_grp, logit, -jnp.inf), axis=1, keepdims=True)
    i1 = jnp.min(jnp.where(in_grp & (logit == v1), col, BIG_COL), axis=1, keepdims=True)
    rest = in_grp & (col != i1)
    v2 = jnp.max(jnp.where(rest, logit, -jnp.inf), axis=1, keepdims=True)
    i2 = jnp.min(jnp.where(rest & (logit == v2), col, BIG_COL), axis=1, keepdims=True)
    e = jnp.exp(v2 - v1)
    w1 = 1.0 / ((1.0 + e) * gsum)
    w2 = e / ((1.0 + e) * gsum)
    pick1 = col == i1
    pick2 = col == i2
    both = (pick1 | pick2).astype(F32)
    ri = lax.broadcasted_iota(jnp.int32, (tm, tm), 0)
    ci = lax.broadcasted_iota(jnp.int32, (tm, tm), 1)
    before = (ci < ri).astype(BF16)
    cum = _dot(before, both.astype(BF16)) + carry_scr[...]
    r1 = jnp.sum(jnp.where(pick1, cum, 0.0), axis=1, keepdims=True)
    r2 = jnp.sum(jnp.where(pick2, cum, 0.0), axis=1, keepdims=True)
    carry_scr[...] = carry_scr[...] + jnp.sum(both, axis=0, keepdims=True)
    cnt_ref[...] = jnp.broadcast_to(carry_scr[...], cnt_ref.shape)
    info = jnp.where(col == 0, (i1 - ROUTE_COL).astype(F32), 0.0)
    info = jnp.where(col == 1, (i2 - ROUTE_COL).astype(F32), info)
    info = jnp.where(col == 2, w1, info)
    info = jnp.where(col == 3, w2, info)
    info = jnp.where(col == 4, r1, info)
    info_ref[...] = jnp.where(col == 5, r2, info)


def _route(y2d, g, w_rg, b_rg, w_re, b_re, tm):
    n, d = y2d.shape
    wr = jnp.concatenate([w_rg, w_re, jnp.zeros((d, LANE - N_GROUPS - N_EXPERTS), F32)], axis=1).astype(BF16)
    bias = jnp.concatenate([b_rg, b_re, jnp.zeros((LANE - N_GROUPS - N_EXPERTS,), F32)])[None, :]
    rows = lambda i: (i, 0)
    fixed = lambda i: (0, 0)
    return pl.pallas_call(
        functools.partial(_route_kernel, tm=tm),
        grid=(n // tm,),
        in_specs=[pl.BlockSpec((tm, d), rows), pl.BlockSpec((1, d), fixed),
                  pl.BlockSpec((d, LANE), fixed), pl.BlockSpec((1, LANE), fixed)],
        out_specs=[pl.BlockSpec((tm * SPLIT, LANE), rows), pl.BlockSpec((tm, LANE), rows),
                   pl.BlockSpec((8, LANE), fixed)],
        out_shape=[jax.ShapeDtypeStruct((n * SPLIT, LANE), F32), jax.ShapeDtypeStruct((n, LANE), F32),
                   jax.ShapeDtypeStruct((8, LANE), F32)],
        scratch_shapes=[pltpu.VMEM((1, LANE), F32)],
        compiler_params=_cparams(("arbitrary",)),
        name="moe_route",
    )(y2d, g, wr, bias)


W_CHUNKS = 4


def _expert_kernel(te_ref, nu_ref, src_ref, nx_ref, ws_ref, h_hbm, wg_hbm, wu_hbm, wd_hbm, o_ref,
                   xbuf, wg_f, wu_f, wd_f, wg_b, wu_b, wd_b, sem, wsem, *, tme):
    i = pl.program_id(0)
    slot = i % 2

    def weight_copies(expert, wslot):
        copies = []
        for hbm, buf in ((wg_hbm, wg_f), (wu_hbm, wu_f), (wd_hbm, wd_f)):
            step = hbm.shape[1] // W_CHUNKS
            for c in range(W_CHUNKS):
                copies.append(pltpu.make_async_copy(hbm.at[expert, pl.ds(c * step, step)],
                                                    buf.at[wslot, pl.ds(c * step, step)], wsem.at[wslot]))
        return copies

    def row_copy(tile, r, buf_slot):
        src_row = src_ref[tile * tme + r]
        return pltpu.make_async_copy(h_hbm.at[pl.ds(pl.multiple_of(src_row * SPLIT, SPLIT), SPLIT)],
                                     xbuf.at[buf_slot, pl.ds(pl.multiple_of(r * SPLIT, SPLIT), SPLIT)],
                                     sem.at[buf_slot])

    def fetch(tile, buf_slot):
        def body(r, carry):
            row_copy(tile, r, buf_slot).start()
            return carry
        lax.fori_loop(0, tme, body, 0, unroll=DMA_UNROLL)

    @pl.when(i == 0)
    def _():
        fetch(0, 0)
        for cp in weight_copies(te_ref[0], ws_ref[0]):
            cp.start()

    @pl.when(i + 1 < nu_ref[0])
    def _():
        fetch(i + 1, 1 - slot)

    @pl.when(i < nu_ref[0])
    def _():
        prev = te_ref[jnp.maximum(i - 1, 0)]

        @pl.when((i == 0) | (te_ref[i] != prev))
        def _():
            w = ws_ref[i]
            for cp in weight_copies(te_ref[i], w):
                cp.wait()
            wg_b[...] = wg_f[w].astype(BF16)
            wu_b[...] = wu_f[w].astype(BF16)
            wd_b[...] = wd_f[w].astype(BF16)

            @pl.when(nx_ref[i] >= 0)
            def _():
                for cp in weight_copies(nx_ref[i], 1 - w):
                    cp.start()

        def wait_body(r, carry):
            row_copy(i, r, slot).wait()
            return carry
        lax.fori_loop(0, tme, wait_body, 0, unroll=DMA_UNROLL)

        x = _load_split(xbuf.at[slot], tme).astype(BF16)
        a = _dot(x, wg_b[...])
        u = _dot(x, wu_b[...])
        hid = a * jax.nn.sigmoid(a) * u
        _store_split(o_ref, _dot(hid.astype(BF16), wd_b[...]), tme)

    @pl.when(i >= nu_ref[0])
    def _():
        o_ref[...] = jnp.zeros(o_ref.shape, F32)


def _experts(h, src, tile_expert, n_used, next_expert, weight_slot, w_gate, w_up, w_down, tme):
    p = src.shape[0]
    d = D_MODEL
    hbm = pl.BlockSpec(memory_space=pl.ANY)
    return pl.pallas_call(
        functools.partial(_expert_kernel, tme=tme),
        grid_spec=pltpu.PrefetchScalarGridSpec(
            num_scalar_prefetch=5, grid=(p // tme,),
            in_specs=[hbm, hbm, hbm, hbm],
            out_specs=pl.BlockSpec((tme * SPLIT, LANE), lambda i, *_: (i, 0)),
            scratch_shapes=[pltpu.VMEM((2, tme * SPLIT, LANE), F32),
                            pltpu.VMEM((2, d, D_EXPERT), F32), pltpu.VMEM((2, d, D_EXPERT), F32),
                            pltpu.VMEM((2, D_EXPERT, d), F32),
                            pltpu.VMEM((d, D_EXPERT), BF16), pltpu.VMEM((d, D_EXPERT), BF16),
                            pltpu.VMEM((D_EXPERT, d), BF16),
                            pltpu.SemaphoreType.DMA((2,)), pltpu.SemaphoreType.DMA((2,))]),
        out_shape=jax.ShapeDtypeStruct((p * SPLIT, LANE), F32),
        compiler_params=_cparams(("arbitrary",)),
        name="moe_experts",
    )(tile_expert, n_used, src, next_expert, weight_slot, h, w_gate, w_up, w_down)


def _combine_kernel(d1_ref, d2_ref, y_ref, info_ref, ys_hbm, g_ref, o_ref, a_buf, b_buf, sem, *, tm):
    i = pl.program_id(0)
    slot = i % 2

    def copy(r, row, buf, buf_slot):
        return pltpu.make_async_copy(ys_hbm.at[pl.ds(pl.multiple_of(row * SPLIT, SPLIT), SPLIT)],
                                     buf.at[buf_slot, pl.ds(pl.multiple_of(r * SPLIT, SPLIT), SPLIT)],
                                     sem.at[buf_slot])

    def fetch(tile, buf_slot):
        def issue(r, carry):
            copy(r, d1_ref[tile * tm + r], a_buf, buf_slot).start()
            copy(r, d2_ref[tile * tm + r], b_buf, buf_slot).start()
            return carry
        lax.fori_loop(0, tm, issue, 0, unroll=DMA_UNROLL)

    @pl.when(i == 0)
    def _():
        fetch(0, 0)

    @pl.when(i + 1 < pl.num_programs(0))
    def _():
        fetch(i + 1, 1 - slot)

    def drain(r, carry):
        copy(r, 0, a_buf, slot).wait()
        copy(r, 0, b_buf, slot).wait()
        return carry

    lax.fori_loop(0, tm, drain, 0, unroll=DMA_UNROLL)
    w1 = info_ref[:, 2:3]
    w2 = info_ref[:, 3:4]
    y = y_ref[...] + w1 * _load_split(a_buf.at[slot], tm) + w2 * _load_split(b_buf.at[slot], tm)
    o_ref[...] = y * lax.rsqrt(jnp.mean(y * y, axis=-1, keepdims=True) + EPS) * g_ref[...]


def _combine(y2d, info, ys, d1, d2, g, tm):
    n, d = y2d.shape
    return pl.pallas_call(
        functools.partial(_combine_kernel, tm=tm),
        grid_spec=pltpu.PrefetchScalarGridSpec(
            num_scalar_prefetch=2, grid=(n // tm,),
            in_specs=[pl.BlockSpec((tm, d), lambda i, a, b: (i, 0)),
                      pl.BlockSpec((tm, LANE), lambda i, a, b: (i, 0)),
                      pl.BlockSpec(memory_space=pl.ANY),
                      pl.BlockSpec((1, d), lambda i, a, b: (0, 0))],
            out_specs=pl.BlockSpec((tm, d), lambda i, a, b: (i, 0)),
            scratch_shapes=[pltpu.VMEM((2, tm * SPLIT, LANE), F32), pltpu.VMEM((2, tm * SPLIT, LANE), F32),
                            pltpu.SemaphoreType.DMA((2,))]),
        out_shape=jax.ShapeDtypeStruct((n, d), F32),
        compiler_params=_cparams(("arbitrary",)),
        name="moe_combine",
    )(d1, d2, y2d, info, ys, g)


def _moe_final(y2d, ffn_g, w_rg, b_rg, w_re, b_re, w_gate, w_up, w_down, final_g, tm, tme):
    n, d = y2d.shape
    h, info, cnt = _route(y2d, ffn_g, w_rg, b_rg, w_re, b_re, tm)
    e1 = info[:, 0].astype(jnp.int32)
    e2 = info[:, 1].astype(jnp.int32)
    counts = cnt[0, ROUTE_COL:ROUTE_COL + N_EXPERTS].astype(jnp.int32)
    padded = (counts + tme - 1) // tme * tme
    ends = jnp.cumsum(padded)
    offs = ends - padded
    d1 = offs[e1] + info[:, 4].astype(jnp.int32)
    d2 = offs[e2] + info[:, 5].astype(jnp.int32)
    p = 2 * n + N_EXPERTS * tme
    tok = jnp.arange(n, dtype=jnp.int32)
    src = jnp.zeros((p,), jnp.int32).at[jnp.concatenate([d1, d2])].set(jnp.concatenate([tok, tok]))
    n_tiles = p // tme
    n_used = (ends[-1] // tme).astype(jnp.int32).reshape(1)
    tile_start = jnp.arange(n_tiles, dtype=jnp.int32) * tme
    tile_expert = jnp.sum((ends[None, :] <= tile_start[:, None]).astype(jnp.int32), axis=1)
    last = jnp.sum((ends <= ends[-1] - 1).astype(jnp.int32))
    tile_expert = jnp.minimum(tile_expert, last).astype(jnp.int32)
    run = jnp.cumsum(jnp.concatenate([jnp.zeros((1,), jnp.int32),
                                      (tile_expert[1:] != tile_expert[:-1]).astype(jnp.int32)]))
    weight_slot = (run % 2).astype(jnp.int32)
    after = ends[tile_expert] // tme
    next_expert = jnp.where(after < n_used[0], tile_expert[jnp.minimum(after, n_tiles - 1)], -1).astype(jnp.int32)
    ys = _experts(h, src, tile_expert, n_used, next_expert, weight_slot, w_gate, w_up, w_down, tme)
    return _combine(y2d, info, ys, d1, d2, final_g, tm)


GROUP_ROWS = 8
SLC_LANES = 384


def _sample_cattn_kernel(q_ref, kc_ref, vc_ref, b_ref, ov_ref, o_ref, sel_ref, *, n_cmp, n_slc):
    q = (q_ref[0, 0] * (NSA_HEAD_DIM ** -0.5)).astype(BF16)
    s = _dot_nt(q, kc_ref[0, 0].astype(BF16)) + b_ref[0]
    valid = lax.broadcasted_iota(jnp.int32, s.shape, 1) < n_cmp
    s = jnp.where(valid, s, NEG_BIG)
    e = jnp.exp(s - jnp.max(s, axis=1, keepdims=True))
    p = jnp.where(valid, e / jnp.sum(e, axis=1, keepdims=True), 0.0)
    o_ref[0, 0] = _dot(p.astype(BF16), vc_ref[0, 0].astype(BF16))
    pg = jnp.where(lax.broadcasted_iota(jnp.int32, p.shape, 0) < NSA_GROUP, p, 0.0)
    pc = jnp.broadcast_to(jnp.sum(pg, axis=0, keepdims=True), pg.shape)
    score = _dot(pc.astype(BF16), ov_ref[...])[0:1, :]
    blk = lax.broadcasted_iota(jnp.int32, score.shape, 1)
    cur = n_slc - 1
    forced = (blk == 0) | (blk == cur) | (blk == cur - 1)
    score = jnp.where(forced, FORCE_SCORE, score)
    score = jnp.where(blk > cur, -jnp.inf, score)
    ri = lax.broadcasted_iota(jnp.int32, (SLC_LANES, SLC_LANES), 0)
    ci = lax.broadcasted_iota(jnp.int32, (SLC_LANES, SLC_LANES), 1)
    sb = jnp.broadcast_to(score, (SLC_LANES, SLC_LANES))
    col = jnp.sum(jnp.where(ri == ci, sb, 0.0), axis=1, keepdims=True)
    beats = (col > sb) | ((col == sb) & (ri < ci))
    rank = jnp.sum(beats.astype(jnp.int32), axis=0, keepdims=True)
    sel_ref[0, 0] = jnp.broadcast_to((rank < SLC_TOPN).astype(F32), (GROUP_ROWS, SLC_LANES))


def _sample_cattn(q8, kvc, bias_cs, n_cmp, n_slc):
    bsz, _, nch, _ = kvc.shape
    c0 = np.arange(nch) * CMP_STRIDE
    s0 = np.arange(SLC_LANES) * SLC_BLOCK
    ov = np.minimum(c0[:, None] + CMP_BLOCK, s0[None, :] + SLC_BLOCK) - np.maximum(c0[:, None], s0[None, :])
    ov = np.clip(ov, 0, None).astype(np.float32) / CMP_BLOCK
    ov[n_cmp:, :] = 0.0
    ov[:, n_slc:] = 0.0
    return pl.pallas_call(
        functools.partial(_sample_cattn_kernel, n_cmp=n_cmp, n_slc=n_slc),
        grid=(bsz, NSA_KV_HEADS),
        in_specs=[pl.BlockSpec((1, 1, GROUP_ROWS, NSA_HEAD_DIM), lambda b, h: (b, h, 0, 0)),
                  pl.BlockSpec((1, 1, nch, NSA_HEAD_DIM), lambda b, h: (b, h, 0, 0)),
                  pl.BlockSpec((1, 1, nch, NSA_HEAD_DIM), lambda b, h: (b, NSA_KV_HEADS + h, 0, 0)),
                  pl.BlockSpec((1, GROUP_ROWS, nch), lambda b, h: (h, 0, 0)),
                  pl.BlockSpec((nch, SLC_LANES), lambda b, h: (0, 0))],
        out_specs=[pl.BlockSpec((1, 1, GROUP_ROWS, NSA_HEAD_DIM), lambda b, h: (b, h, 0, 0)),
                   pl.BlockSpec((1, 1, GROUP_ROWS, SLC_LANES), lambda b, h: (b, h, 0, 0))],
        out_shape=[jax.ShapeDtypeStruct((bsz, NSA_KV_HEADS, GROUP_ROWS, NSA_HEAD_DIM), F32),
                   jax.ShapeDtypeStruct((bsz, NSA_KV_HEADS, GROUP_ROWS, SLC_LANES), F32)],
        compiler_params=_cparams(("arbitrary", "arbitrary")),
        name="sample_cmp_attn",
    )(q8, kvc, kvc, bias_cs, jnp.asarray(ov, BF16))


def _sample_attn_kernel(pg_ref, hf_ref, q_ref, cache_hbm, snew_ref, bs_ref, win_ref, wnew_ref, bw_ref,
                        os_ref, ow_ref, kt_buf, vt_buf, sem, *, n_gather, wlen):
    b = pl.program_id(0)
    rows_per_page = 2 * NSA_KV_HEADS * NSA_HEAD_DIM

    def page_copy(h, s, c, buf):
        page = pg_ref[(b * NSA_KV_HEADS + h) * n_gather + s]
        start = pl.multiple_of(page * rows_per_page + (c * NSA_KV_HEADS + h) * NSA_HEAD_DIM, NSA_HEAD_DIM)
        return pltpu.make_async_copy(cache_hbm.at[pl.ds(start, NSA_HEAD_DIM)],
                                     buf.at[h, :, pl.ds(s * PAGE_SIZE, PAGE_SIZE)], sem)

    for h in range(NSA_KV_HEADS):
        for s in range(n_gather):
            page_copy(h, s, 0, kt_buf).start()
            page_copy(h, s, 1, vt_buf).start()

    def attend(q, kts, vts, bias, valid):
        s = jnp.concatenate([_dot(q, kt.astype(BF16)) for kt in kts], axis=1) + bias
        s = jnp.where(valid, s, NEG_BIG)
        e = jnp.exp(s - jnp.max(s, axis=1, keepdims=True))
        p = jnp.where(valid, e / jnp.sum(e, axis=1, keepdims=True), 0.0).astype(BF16)
        out = None
        lo = 0
        for vt in vts:
            n = vt.shape[1]
            part = _dot_nt(p[:, lo:lo + n], vt.astype(BF16))
            out = part if out is None else out + part
            lo += n
        return out

    def head_rows(ref, c, h):
        r0 = (c * NSA_KV_HEADS + h) * NSA_HEAD_DIM
        return ref[0, r0:r0 + NSA_HEAD_DIM, :]

    widx = lax.broadcasted_iota(jnp.int32, (GROUP_ROWS, wlen + PAGE_SIZE), 1)
    wvalid = (widx >= wlen + 1 - WINDOW) & (widx <= wlen)
    for h in range(NSA_KV_HEADS):
        q = (q_ref[0, h] * (NSA_HEAD_DIM ** -0.5)).astype(BF16)
        ow_ref[0, h] = attend(q, [head_rows(win_ref, 0, h), head_rows(wnew_ref, 0, h)],
                              [head_rows(win_ref, 1, h), head_rows(wnew_ref, 1, h)], bw_ref[h], wvalid)

    for h in range(NSA_KV_HEADS):
        for s in range(n_gather):
            page_copy(h, s, 0, kt_buf).wait()
            page_copy(h, s, 1, vt_buf).wait()

    keys = n_gather * PAGE_SIZE
    lane = lax.broadcasted_iota(jnp.int32, (GROUP_ROWS, keys + PAGE_SIZE), 1)
    slot = lane >> 7
    lane_half = (lane >> 6) & 1
    for h in range(NSA_KV_HEADS):
        want = jnp.full(lane.shape, -1, jnp.int32)
        for s in range(n_gather):
            want = jnp.where(slot == s, hf_ref[(b * NSA_KV_HEADS + h) * n_gather + s], want)
        svalid = (lane_half == want) | (lane == keys)
        q = (q_ref[0, h] * (NSA_HEAD_DIM ** -0.5)).astype(BF16)
        os_ref[0, h] = attend(q, [kt_buf[h], head_rows(snew_ref, 0, h)],
                              [vt_buf[h], head_rows(snew_ref, 1, h)], bs_ref[0, h], svalid)


def _sample_attn(pages, halves, q8, cache_t, snew_t, bias_sel, win_t, wnew_t, bias_w, n_gather):
    bsz = q8.shape[0]
    wlen = win_t.shape[2]
    keys = n_gather * PAGE_SIZE
    rows = 2 * NSA_KV_HEADS * NSA_HEAD_DIM
    qspec = pl.BlockSpec((1, NSA_KV_HEADS, GROUP_ROWS, NSA_HEAD_DIM), lambda b, pg, hf: (b, 0, 0, 0))
    newspec = pl.BlockSpec((1, rows, PAGE_SIZE), lambda b, pg, hf: (b, 0, 0))
    return pl.pallas_call(
        functools.partial(_sample_attn_kernel, n_gather=n_gather, wlen=wlen),
        grid_spec=pltpu.PrefetchScalarGridSpec(
            num_scalar_prefetch=2, grid=(bsz,),
            in_specs=[qspec,
                      pl.BlockSpec(memory_space=pl.ANY),
                      newspec,
                      pl.BlockSpec((1, NSA_KV_HEADS, GROUP_ROWS, keys + PAGE_SIZE), lambda b, pg, hf: (b, 0, 0, 0)),
                      pl.BlockSpec((1, rows, wlen), lambda b, pg, hf: (b, 0, 0)),
                      newspec,
                      pl.BlockSpec((NSA_KV_HEADS, GROUP_ROWS, wlen + PAGE_SIZE), lambda b, pg, hf: (0, 0, 0))],
            out_specs=[qspec, qspec],
            scratch_shapes=[pltpu.VMEM((NSA_KV_HEADS, NSA_HEAD_DIM, keys), F32),
                            pltpu.VMEM((NSA_KV_HEADS, NSA_HEAD_DIM, keys), F32),
                            pltpu.SemaphoreType.DMA(())]),
        out_shape=[jax.ShapeDtypeStruct(q8.shape, F32), jax.ShapeDtypeStruct(q8.shape, F32)],
        compiler_params=_cparams(("arbitrary",)),
        name="sample_slc_win_attn",
    )(pages, halves, q8, cache_t, snew_t, bias_sel, win_t, wnew_t, bias_w)


PAGES_PER_STEP = 16


def _paged_pq_kernel(pt_ref, cache_hbm, pos_ref, w_ref, o_ref, pbuf, tok_scr, sem):
    b = pl.program_id(0)
    g = pl.program_id(1)
    ng = pl.num_programs(1)
    lin = b * ng + g
    slot = lin % 2
    rows_per_page = 2 * NSA_KV_HEADS * NSA_HEAD_DIM
    chunks = PAGES_PER_STEP * (PAGE_SIZE // CMP_STRIDE)

    def page_copy(step, p, buf_slot):
        sb = step // ng
        sg = step - sb * ng
        page = pt_ref[sb, sg * PAGES_PER_STEP + p]
        return pltpu.make_async_copy(cache_hbm.at[pl.ds(pl.multiple_of(page * rows_per_page, rows_per_page),
                                                        rows_per_page)],
                                     pbuf.at[buf_slot, p], sem.at[buf_slot])

    def fetch(step, buf_slot):
        for p in range(PAGES_PER_STEP):
            page_copy(step, p, buf_slot).start()

    @pl.when(lin == 0)
    def _():
        fetch(0, 0)

    @pl.when(lin + 1 < pl.num_programs(0) * ng)
    def _():
        fetch(lin + 1, 1 - slot)

    for p in range(PAGES_PER_STEP):
        page_copy(lin, p, slot).wait()

    for c in range(2):
        head_chunks = []
        for pair in (2 * c, 2 * c + 1):
            for p in range(PAGES_PER_STEP):
                tok_scr[p * PAGE_SIZE:(p + 1) * PAGE_SIZE, :] = pbuf[slot, p, pair * LANE:(pair + 1) * LANE, :].T
            toks = [tok_scr[pl.ds(l, chunks, stride=CMP_STRIDE), :] for l in range(CMP_STRIDE)]
            for half in range(2):
                lanes = slice(half * NSA_HEAD_DIM, (half + 1) * NSA_HEAD_DIM)
                head_chunks.append(jnp.concatenate([t[:, lanes] for t in toks], axis=1))
        stacked = jnp.concatenate(head_chunks, axis=0)
        for part in range(2):
            lhs = (stacked + pos_ref[c, part:part + 1, :]).astype(BF16)
            out = _dot(lhs, w_ref[c, part])
            for hh in range(NSA_KV_HEADS):
                o_ref[0, c * NSA_KV_HEADS + hh, :, part * CMP_HIDDEN:(part + 1) * CMP_HIDDEN] = (
                    out[hh * chunks:(hh + 1) * chunks, :])


def _paged_pq(cache_t, page_table, cmp_pos, cmp_w1):
    db, n_pages = page_table.shape
    kdim = CMP_STRIDE * NSA_HEAD_DIM
    nch = n_pages * (PAGE_SIZE // CMP_STRIDE)
    tn = PAGES_PER_STEP * (PAGE_SIZE // CMP_STRIDE)
    rows_per_page = 2 * NSA_KV_HEADS * NSA_HEAD_DIM
    return pl.pallas_call(
        _paged_pq_kernel,
        grid_spec=pltpu.PrefetchScalarGridSpec(
            num_scalar_prefetch=1, grid=(db, n_pages // PAGES_PER_STEP),
            in_specs=[pl.BlockSpec(memory_space=pl.ANY),
                      pl.BlockSpec((2, 2, kdim), lambda b, g, pt: (0, 0, 0)),
                      pl.BlockSpec((2, 2, kdim, CMP_HIDDEN), lambda b, g, pt: (0, 0, 0, 0))],
            out_specs=pl.BlockSpec((1, 8, tn, 2 * CMP_HIDDEN), lambda b, g, pt: (b, 0, g, 0)),
            scratch_shapes=[pltpu.VMEM((2, PAGES_PER_STEP, rows_per_page, PAGE_SIZE), F32),
                            pltpu.VMEM((PAGES_PER_STEP * PAGE_SIZE, LANE), F32),
                            pltpu.SemaphoreType.DMA((2,))]),
        out_shape=jax.ShapeDtypeStruct((db, 8, nch, 2 * CMP_HIDDEN), F32),
        compiler_params=_cparams(("arbitrary", "arbitrary")),
        name="cmp_pq_paged",
    )(page_table, cache_t, cmp_pos.reshape(2, 2, kdim), cmp_w1.reshape(2, 2, kdim, CMP_HIDDEN).astype(BF16))


def _pad_rows(a, n):
    return jnp.concatenate([a, jnp.zeros((n - a.shape[0],) + a.shape[1:], a.dtype)], axis=0)


def _sample_mixer(x_sample, cache_cmp, cache_slc, cache_win, st_c, st_n, st_m, st_conv, page_table,
                  norm_g, wb, gate_bias, conv_w, ml_norm_g, cmp_pos, cmp_w1, cmp_w2, nsa_norm_g, w_out_b, rel_bias):
    db = x_sample.shape[0]
    n_pages = page_table.shape[1]
    past = n_pages * PAGE_SIZE
    tok = 16
    x16 = _pad_rows(x_sample.reshape(db, D_MODEL), tok)
    u = _proj(x16, norm_g, wb, tok, 512)
    small = u[:, COL_SMALL:COL_SMALL + LANE]

    T = 128
    useq = jnp.zeros((db, T, 4 * D_ML), F32)
    useq = useq.at[:, T - CONV_W:T - 1, 0:2 * D_ML].set(st_conv)
    useq = useq.at[:, T - 1, :].set(u[:db, 0:4 * D_ML])
    sseq = jnp.zeros((db, T, LANE), F32).at[:, :T - 1, LANE - 1].set(1.0)
    sseq = sseq.at[:, T - 1, :].set(small[:db])
    y_seq, conv_n, c_n, n_n, m_n = _mlstm(
        useq.reshape(db * T, 4 * D_ML), sseq.reshape(db * T, LANE), jnp.zeros((db, CONV_W - 1, 2 * D_ML), F32),
        st_c, st_n, st_m, conv_w, gate_bias, ml_norm_g, db, T, T)
    y_ml = y_seq.reshape(db, T, D_ML)[:, T - 1]

    n_pool = cache_cmp.shape[0]
    cmp_t = cache_cmp.transpose(0, 2, 3, 4, 1).reshape(n_pool * 2 * D_KV, PAGE_SIZE)
    kvc = _cmp_hid(_paged_pq(cmp_t, page_table, cmp_pos, cmp_w1), cmp_w2)
    nch = past // CMP_STRIDE
    n_cmp = (past + 1) // CMP_STRIDE - CMP_BLOCK // CMP_STRIDE + 1
    n_slc = -(-(past + 1) // SLC_BLOCK)
    q = u[:db, COL_QNSA:COL_QNSA + D_NSA].reshape(db, NSA_KV_HEADS, NSA_GROUP, NSA_HEAD_DIM)
    q8 = jnp.concatenate([q, jnp.zeros_like(q)], axis=2)
    bd = _bias_by_distance(rel_bias, past + 1).reshape(NSA_KV_HEADS, NSA_GROUP, past + 1)
    pad_g = lambda a: jnp.concatenate([a, jnp.zeros_like(a)], axis=1)
    dist_c = np.clip(past - (np.arange(nch) * CMP_STRIDE + CMP_BLOCK - 1), 0, None)
    o_c8, sel = _sample_cattn(q8, kvc, pad_g(bd[:, :, dist_c]), n_cmp, n_slc)

    mask = sel[:, :, 0, :n_slc] > 0.5
    idx = jnp.sort(jnp.where(mask, jnp.arange(n_slc, dtype=jnp.int32), jnp.int32(1 << 20)), axis=-1)
    n_gather = SLC_TOPN - 1
    idx = idx[..., :n_gather]
    pages_per_block = PAGE_SIZE // SLC_BLOCK
    logical_page = idx // pages_per_block
    pages = jnp.take_along_axis(page_table[:, None, :], logical_page, axis=2).reshape(-1).astype(jnp.int32)
    halves = (idx % pages_per_block).reshape(-1).astype(jnp.int32)
    kpos = (logical_page[..., None] * PAGE_SIZE + jnp.arange(PAGE_SIZE, dtype=jnp.int32)).reshape(db, NSA_KV_HEADS, -1)
    hh = jnp.arange(NSA_KV_HEADS)[None, :, None, None]
    gg = jnp.arange(NSA_GROUP)[None, None, :, None]
    bias_sel = bd[hh, gg, (past - kpos)[:, :, None, :]]
    bias_sel = jnp.concatenate([bias_sel, jnp.broadcast_to(bd[None, :, :, 0:1], (db, NSA_KV_HEADS, NSA_GROUP, 1)),
                                jnp.zeros((db, NSA_KV_HEADS, NSA_GROUP, PAGE_SIZE - 1), F32)], axis=-1)
    bias_sel = jnp.concatenate([bias_sel, jnp.zeros_like(bias_sel)], axis=2)
    wlen = cache_win.shape[1]
    dist_w = np.clip(wlen - np.arange(wlen + PAGE_SIZE), 0, None)
    kvs_new = _kv_from_per_head(u[:db, COL_KVS:COL_KVS + 2 * D_KV])
    kvw_new = _kv_from_per_head(u[:db, COL_KVW:COL_KVW + 2 * D_KV])
    lane_pad = lambda a: jnp.pad(a[:, :, None], ((0, 0), (0, 0), (0, PAGE_SIZE - 1)))
    slc_t = cache_slc.transpose(0, 2, 3, 4, 1).reshape(n_pool * 2 * D_KV, PAGE_SIZE)
    win_t = cache_win.transpose(0, 2, 3, 4, 1).reshape(db, 2 * D_KV, wlen)
    o_s8, o_w8 = _sample_attn(pages, halves, q8, slc_t, lane_pad(kvs_new), bias_sel, win_t, lane_pad(kvw_new),
                              pad_g(bd[:, :, dist_w]), n_gather)
    win2d = cache_win.reshape(db, wlen, 2 * D_KV)

    heads = lambda o: _pad_rows(o[:, :, :NSA_GROUP, :].reshape(db, D_NSA), tok)
    y = _outproj(_pad_rows(y_ml, tok), heads(o_c8), heads(o_s8), heads(o_w8), small, nsa_norm_g, w_out_b, x16, tok, 512)
    kvshape = (1, db, 1, 2, NSA_KV_HEADS, NSA_HEAD_DIM)
    new_win = jnp.concatenate([win2d[:, 1:], kvw_new[:, None, :]], axis=1)
    states = (u[:db, COL_KVC:COL_KVC + 2 * D_KV].reshape(kvshape), kvs_new.reshape(kvshape),
              new_win.reshape((1, db, wlen, 2, NSA_KV_HEADS, NSA_HEAD_DIM)),
              c_n[None], n_n[None], m_n[None], conv_n[None])
    return y, states


def kernel(x_prompt, x_sample, cache_cmp_kv, cache_slc_kv, cache_win_kv, state_mlstm_C, state_mlstm_n,
           state_mlstm_m, state_conv, page_table, rel_bias, norm_mix_g, w_in, b_ig, b_fg, conv_w, ml_norm_g,
           cmp_pos, cmp_w1, cmp_w2, nsa_norm_g, w_out, norm_ffn_g, w_router_grp, b_router_grp, w_router_exp,
           b_router_exp, w_gate, w_up, w_down, norm_final_g):
    B, S, D = x_prompt.shape
    wb = _reorder_w_in(w_in[0])
    gate_bias = jnp.zeros((1, LANE), F32).at[0, 0:ML_HEADS].set(b_ig[0]).at[0, ML_HEADS:2 * ML_HEADS].set(b_fg[0])
    w_out_b = w_out[0].astype(BF16)
    yp, st_p = _prompt_mixer(x_prompt, norm_mix_g, wb, gate_bias, conv_w[0], ml_norm_g, cmp_pos[0], cmp_w1[0],
                             cmp_w2[0], nsa_norm_g, w_out_b, rel_bias)
    ys, st_s = _sample_mixer(x_sample, cache_cmp_kv[0], cache_slc_kv[0], cache_win_kv[0], state_mlstm_C[0],
                             state_mlstm_n[0], state_mlstm_m[0], state_conv[0], page_table, norm_mix_g, wb, gate_bias,
                             conv_w[0], ml_norm_g, cmp_pos[0], cmp_w1[0], cmp_w2[0], nsa_norm_g, w_out_b, rel_bias)
    moe_w = (norm_ffn_g, w_router_grp[0], b_router_grp[0], w_router_exp[0], b_router_exp[0],
             w_gate[0], w_up[0], w_down[0], norm_final_g[None, :])
    DB, L, _ = x_sample.shape
    out_p = _moe_final(yp.reshape(B * S, D), *moe_w, 256, 128).reshape(B, S, D)
    out_s = _moe_final(_pad_rows(ys, 128), *moe_w, 128, 16)[:DB].reshape(DB, L, D)
    outs = [out_p, out_s]
    for a, b in zip(st_p, st_s):
        outs += [a, b]
    return tuple(outs)


def _prompt_mixer(x_prompt, norm_g, wb, gate_bias, conv_w, ml_norm_g, cmp_pos, cmp_w1, cmp_w2, nsa_norm_g,
                  w_out_b, rel_bias):
    B, S, D = x_prompt.shape
    x2d = x_prompt.reshape(B * S, D)
    tm = min(1024, S)
    u, kvt_c, kvt_s, kvt_w = _proj(x2d, norm_g, wb, tm, 2 * D_KV, seq=S)
    small = u[:, COL_SMALL:COL_SMALL + LANE]
    y_ml, conv_n, c_n, n_n, m_n = _mlstm(
        u, small, jnp.zeros((B, CONV_W - 1, 2 * D_ML), F32),
        jnp.zeros((B, ML_HEADS, ML_HEAD_DIM, ML_HEAD_DIM), F32), jnp.zeros((B, ML_HEADS, ML_HEAD_DIM), F32),
        jnp.full((B, ML_HEADS), -jnp.inf, F32), conv_w, gate_bias, ml_norm_g, B, S, 256)
    kvc = _compress(u, COL_KVC // (2 * D_KV), B, S, cmp_pos, cmp_w1, cmp_w2, min(S, 4096))
    tq_c = min(S, 256)
    o_c, sel_t = _cattn(u, kvc, rel_bias, B, S, tq_c)
    o_s = _flash(u, COL_KVS, rel_bias, sel_t, B, S)
    o_w = _flash(u, COL_KVW, rel_bias, None, B, S)
    y = _outproj(y_ml, o_c, o_s, o_w, small, nsa_norm_g, w_out_b, x2d, tm, 512)
    win = min(WINDOW, S)
    rows = lambda kvt: kvt.transpose(0, 4, 1, 2, 3)[None]
    states = (rows(kvt_c), rows(kvt_s), rows(kvt_w[..., S - win:]),
              c_n[None], n_n[None], m_n[None], conv_n[None])
    return y.reshape(B, S, D), states
```

```python
import functools
import math

import numpy as np
import jax
import jax.numpy as jnp
from jax import lax
from jax.experimental import pallas as pl
from jax.experimental.pallas import tpu as pltpu

F32 = jnp.float32
BF16 = jnp.bfloat16

D_MODEL = 2048
ML_HEADS = 4
ML_HEAD_DIM = 256
D_ML = 1024
CONV_W = 4
NSA_HEADS = 16
NSA_HEAD_DIM = 64
D_NSA = 1024
NSA_KV_HEADS = 4
NSA_GROUP = 4
D_KV = 256
CMP_BLOCK = 32
CMP_STRIDE = 16
CMP_HIDDEN = 256
SLC_BLOCK = 64
SLC_TOPN = 16
WINDOW = 512
N_BUCKETS = 32
MAX_DISTANCE = 2048
N_GROUPS = 4
EXPERTS_PER_GROUP = 8
N_EXPERTS = 32
D_EXPERT = 512
PAGE_SIZE = 128
EPS = 1e-6
NEG_BIG = -1e30
FORCE_SCORE = 1e4

LANE = 128
COL_QML, COL_KML, COL_VML, COL_OML = 0, 1024, 2048, 3072
COL_QNSA = 4096
COL_KVC, COL_KVS, COL_KVW = 5120, 5632, 6144
COL_SMALL = 6656
N_PROJ = 7168
VMEM_LIMIT = 56 * 1024 * 1024

PROJ_ROWS = 1024
PROJ_COLS = 2 * D_KV
ML_CHUNK_ROWS = 256
CMP_TOKENS = 4096
CMP_QUERY_ROWS = 256
MOE_TOKEN_ROWS = 256
MOE_EXPERT_ROWS = 128
SAMPLE_TOKEN_ROWS = 16
SAMPLE_SEQ_ROWS = 128
SAMPLE_MOE_ROWS = 128
SAMPLE_EXPERT_ROWS = 16


def _cparams(sem, vmem=VMEM_LIMIT):
    return pltpu.CompilerParams(dimension_semantics=sem, vmem_limit_bytes=vmem)


def _split2(x):
    hi = x.astype(BF16)
    lo = (x - hi.astype(F32)).astype(BF16)
    return hi, lo


def _split3(x):
    hi = x.astype(BF16)
    r = x - hi.astype(F32)
    mid = r.astype(BF16)
    lo = (r - mid.astype(F32)).astype(BF16)
    return hi, mid, lo


def _dot(a, b):
    return jnp.dot(a, b, preferred_element_type=F32)


def _dot_nt(a, b):
    return lax.dot_general(a, b, (((1,), (1,)), ((), ())), preferred_element_type=F32)


def _dot_tn(a, b):
    return lax.dot_general(a, b, (((0,), (0,)), ((), ())), preferred_element_type=F32)


def _proj_kernel(x_ref, g_ref, w_ref, o_ref, *rest, kv_blocks):
    h_scr = rest[-1]
    j = pl.program_id(1)

    @pl.when(j == 0)
    def _():
        x = x_ref[...]
        ms = jnp.mean(x * x, axis=-1, keepdims=True)
        h_scr[...] = (x * lax.rsqrt(ms + EPS) * g_ref[...]).astype(BF16)

    res = _dot(h_scr[...], w_ref[...])
    o_ref[...] = res

    for (block, per_head), t_ref in zip(kv_blocks, rest[:-1]):
        @pl.when(j == block)
        def _():
            res_t = res.T
            for c in range(2):
                for hh in range(NSA_KV_HEADS):
                    piece = hh * 2 + c if per_head else c * NSA_KV_HEADS + hh
                    t_ref[0, c, hh] = res_t[piece * NSA_HEAD_DIM:(piece + 1) * NSA_HEAD_DIM, :]


def _proj(x2d, g, wb, tm, tn, seq=None):
    n, d = x2d.shape
    nc = wb.shape[1]
    kv_blocks = ()
    out_specs = [pl.BlockSpec((tm, tn), lambda i, j: (i, j))]
    out_shape = [jax.ShapeDtypeStruct((n, nc), F32)]
    if seq is not None:
        assert tn == 2 * D_KV and seq % tm == 0
        kv_blocks = ((COL_KVC // tn, False), (COL_KVS // tn, True), (COL_KVW // tn, True))
        per_seq = seq // tm
        for _ in kv_blocks:
            out_specs.append(pl.BlockSpec((1, 2, NSA_KV_HEADS, NSA_HEAD_DIM, tm),
                                          lambda i, j: (i // per_seq, 0, 0, 0, i % per_seq)))
            out_shape.append(jax.ShapeDtypeStruct((n // seq, 2, NSA_KV_HEADS, NSA_HEAD_DIM, seq), F32))
    outs = pl.pallas_call(
        functools.partial(_proj_kernel, kv_blocks=kv_blocks),
        grid=(n // tm, nc // tn),
        in_specs=[pl.BlockSpec((tm, d), lambda i, j: (i, 0)),
                  pl.BlockSpec((1, d), lambda i, j: (0, 0)),
                  pl.BlockSpec((d, tn), lambda i, j: (0, j))],
        out_specs=out_specs,
        out_shape=out_shape,
        scratch_shapes=[pltpu.VMEM((tm, d), BF16)],
        compiler_params=_cparams(("arbitrary", "arbitrary")),
        name="proj",
    )(x2d, g, wb)
    return outs if seq is not None else outs[0]


def _reorder_w_in(w_in):
    w_in = w_in.astype(BF16)
    big = w_in[:, :4 * D_ML]
    small_a = w_in[:, 4 * D_ML:4 * D_ML + 2 * ML_HEADS]
    rest = w_in[:, 4 * D_ML + 2 * ML_HEADS:]
    q_and_cmp = rest[:, :D_NSA + 2 * D_KV]
    gate = rest[:, D_NSA + 6 * D_KV:]
    d = w_in.shape[0]

    def per_head(w):
        return w.reshape(d, 2, NSA_KV_HEADS, NSA_HEAD_DIM).transpose(0, 2, 1, 3).reshape(d, 2 * D_KV)

    kv_s = per_head(rest[:, D_NSA + 2 * D_KV:D_NSA + 4 * D_KV])
    kv_w = per_head(rest[:, D_NSA + 4 * D_KV:D_NSA + 6 * D_KV])
    pad = jnp.zeros((d, N_PROJ - COL_SMALL - 2 * ML_HEADS - 3 * NSA_HEADS), w_in.dtype)
    return jnp.concatenate([big, q_and_cmp, kv_s, kv_w, small_a, gate, pad], axis=1).astype(BF16)


def _kv_from_per_head(kv2d):
    n = kv2d.shape[0]
    return kv2d.reshape(n, NSA_KV_HEADS, 2, NSA_HEAD_DIM).transpose(0, 2, 1, 3).reshape(n, 2 * D_KV)


def _log_sigmoid(x):
    return jnp.minimum(x, 0.0) - jnp.log1p(jnp.exp(-jnp.abs(x)))


def _mlstm_kernel(q_ref, k_ref, v_ref, o_ref, s_ref, cb_ref, c0_ref, n0_ref, m0_ref,
                  cw_ref, gb_ref, ng_ref,
                  y_ref, cbo_ref, co_ref, no_ref, mo_ref,
                  ext_scr, c_scr, n_scr, m_scr, *, T):
    c = pl.program_id(1)
    nc = pl.num_programs(1)

    @pl.when(c == 0)
    def _():
        ext_scr[0:8, :] = jnp.zeros((8, 2 * D_ML), F32)
        ext_scr[5:8, :] = cb_ref[0]
        c_scr[...] = c0_ref[0]
        n_scr[...] = n0_ref[0]
        m_scr[...] = m0_ref[0]

    ext_scr[8:8 + T, 0:D_ML] = q_ref[...]
    ext_scr[8:8 + T, D_ML:2 * D_ML] = k_ref[...]
    conv = ext_scr[5:5 + T, :] * cw_ref[0:1, :]
    for j in range(1, CONV_W):
        conv = conv + ext_scr[5 + j:5 + j + T, :] * cw_ref[j:j + 1, :]
    tail = ext_scr[8 + T - 3:8 + T, :]
    ext_scr[5:8, :] = tail
    cbo_ref[0] = tail
    qk = conv * jax.nn.sigmoid(conv)

    pre = s_ref[...] + gb_ref[...]
    col = lax.broadcasted_iota(jnp.int32, pre.shape, 1)
    padrow = s_ref[:, LANE - 1:LANE] > 0.5
    gates = jnp.where(col < ML_HEADS, pre, _log_sigmoid(pre))
    gates = jnp.where(padrow, jnp.where(col < ML_HEADS, NEG_BIG, 0.0), gates)
    g_r = gates.T
    ti = lax.broadcasted_iota(jnp.int32, (T, T), 0)
    si = lax.broadcasted_iota(jnp.int32, (T, T), 1)
    upper = (ti <= si).astype(BF16)
    g_fin = jnp.where(lax.broadcasted_iota(jnp.int32, g_r.shape, 0) < ML_HEADS, 0.0, g_r)
    hi, mid, lo = _split3(g_fin)
    cum_r = _dot(hi, upper) + _dot(mid, upper) + _dot(lo, upper)
    rowi = lax.broadcasted_iota(jnp.int32, g_r.shape, 0)
    a_r = jnp.where(rowi < ML_HEADS, g_r, cum_r)
    a_c = a_r.T
    causal = si <= ti

    for h in range(ML_HEADS):
        sl = slice(h * ML_HEAD_DIM, (h + 1) * ML_HEAD_DIM)
        q = qk[:, h * ML_HEAD_DIM:(h + 1) * ML_HEAD_DIM]
        k = qk[:, D_ML + h * ML_HEAD_DIM:D_ML + (h + 1) * ML_HEAD_DIM] * (ML_HEAD_DIM ** -0.5)
        v = v_ref[:, sl]
        ig_r = a_r[h:h + 1, :]
        b_r = a_r[ML_HEADS + h:ML_HEADS + h + 1, :]
        ig_c = a_c[:, h:h + 1]
        b_c = a_c[:, ML_HEADS + h:ML_HEADS + h + 1]
        m_prev = m_scr[h:h + 1, 0:1]
        logd = jnp.where(causal, b_c - b_r + ig_r, -jnp.inf)
        inter = b_c + m_prev
        m_t = jnp.maximum(inter, jnp.max(logd, axis=1, keepdims=True))
        w_intra = jnp.exp(logd - m_t)
        w_inter = jnp.exp(inter - m_t)
        qb = q.astype(BF16)
        kb = k.astype(BF16)
        vb = v.astype(BF16)
        sc = _dot_nt(qb, kb) * w_intra
        cmat = c_scr[h]
        nvec = n_scr[h:h + 1, :]
        num = _dot(sc.astype(BF16), vb) + w_inter * _dot(qb, cmat.astype(BF16))
        qn = jnp.sum(qb.astype(F32) * nvec.astype(BF16).astype(F32), axis=1, keepdims=True)
        den = jnp.sum(sc, axis=1, keepdims=True) + w_inter * qn
        hh = num / jnp.maximum(jnp.abs(den), jnp.exp(-m_t))
        m_new = m_t[T - 1:T, :]
        b_last = b_c[T - 1:T, :]
        w_s = jnp.exp(b_last - b_c + ig_c - m_new)
        decay = jnp.exp(b_last + m_prev - m_new)
        kw = k * w_s
        c_new = decay * cmat + _dot_tn(kw.astype(BF16), vb)
        n_new = decay * nvec + jnp.sum(kw, axis=0, keepdims=True)
        c_scr[h] = c_new
        n_scr[h:h + 1, :] = n_new
        m_scr[h:h + 1, :] = jnp.broadcast_to(m_new, (1, LANE))
        hn = hh * lax.rsqrt(jnp.mean(hh * hh, axis=1, keepdims=True) + EPS) * ng_ref[:, sl]
        y_ref[:, sl] = (hn * jax.nn.sigmoid(o_ref[:, sl])).astype(y_ref.dtype)

    @pl.when(c == nc - 1)
    def _():
        co_ref[0] = c_scr[...]
        no_ref[0] = n_scr[...]
        mo_ref[0] = m_scr[...]


def _mlstm(u, small, conv_buf, c0, n0, m0, conv_w, gate_bias, norm_g, batch, seq, T):
    nc = seq // T
    cb = D_ML // 1024
    m0b = jnp.broadcast_to(m0[:, :, None], (batch, ML_HEADS, LANE))
    m0b = jnp.concatenate([m0b, jnp.zeros((batch, 8 - ML_HEADS, LANE), F32)], axis=1)
    n0p = jnp.concatenate([n0, jnp.zeros((batch, 8 - ML_HEADS, ML_HEAD_DIM), F32)], axis=1)
    row = lambda b, c: (b * nc + c, 0)
    outs = pl.pallas_call(
        functools.partial(_mlstm_kernel, T=T),
        grid=(batch, nc),
        in_specs=[pl.BlockSpec((T, D_ML), lambda b, c: (b * nc + c, COL_QML // D_ML)),
                  pl.BlockSpec((T, D_ML), lambda b, c: (b * nc + c, COL_KML // D_ML)),
                  pl.BlockSpec((T, D_ML), lambda b, c: (b * nc + c, COL_VML // D_ML)),
                  pl.BlockSpec((T, D_ML), lambda b, c: (b * nc + c, COL_OML // D_ML)),
                  pl.BlockSpec((T, LANE), row),
                  pl.BlockSpec((1, CONV_W - 1, 2 * D_ML), lambda b, c: (b, 0, 0)),
                  pl.BlockSpec((1, ML_HEADS, ML_HEAD_DIM, ML_HEAD_DIM), lambda b, c: (b, 0, 0, 0)),
                  pl.BlockSpec((1, 8, ML_HEAD_DIM), lambda b, c: (b, 0, 0)),
                  pl.BlockSpec((1, 8, LANE), lambda b, c: (b, 0, 0)),
                  pl.BlockSpec((CONV_W, 2 * D_ML), lambda b, c: (0, 0)),
                  pl.BlockSpec((1, LANE), lambda b, c: (0, 0)),
                  pl.BlockSpec((1, D_ML), lambda b, c: (0, 0))],
        out_specs=[pl.BlockSpec((T, D_ML), row),
                   pl.BlockSpec((1, CONV_W - 1, 2 * D_ML), lambda b, c: (b, 0, 0)),
                   pl.BlockSpec((1, ML_HEADS, ML_HEAD_DIM, ML_HEAD_DIM), lambda b, c: (b, 0, 0, 0)),
                   pl.BlockSpec((1, 8, ML_HEAD_DIM), lambda b, c: (b, 0, 0)),
                   pl.BlockSpec((1, 8, LANE), lambda b, c: (b, 0, 0))],
        out_shape=[jax.ShapeDtypeStruct((batch * seq, D_ML), BF16),
                   jax.ShapeDtypeStruct((batch, CONV_W - 1, 2 * D_ML), F32),
                   jax.ShapeDtypeStruct((batch, ML_HEADS, ML_HEAD_DIM, ML_HEAD_DIM), F32),
                   jax.ShapeDtypeStruct((batch, 8, ML_HEAD_DIM), F32),
                   jax.ShapeDtypeStruct((batch, 8, LANE), F32)],
        scratch_shapes=[pltpu.VMEM((8 + T, 2 * D_ML), F32),
                        pltpu.VMEM((ML_HEADS, ML_HEAD_DIM, ML_HEAD_DIM), F32),
                        pltpu.VMEM((8, ML_HEAD_DIM), F32),
                        pltpu.VMEM((8, LANE), F32)],
        compiler_params=_cparams(("arbitrary", "arbitrary")),
        name="mlstm",
    )(u, u, u, u, small, conv_buf, c0, n0p, m0b, conv_w, gate_bias, norm_g)
    y, cbo, co, no, mo = outs
    return y, cbo, co, no[:, :ML_HEADS], mo[:, :ML_HEADS, 0]


def _bucket_np(dist):
    n = np.maximum(dist, 0)
    max_exact = N_BUCKETS // 2
    nf = np.maximum(n, 1).astype(np.float64)
    large = max_exact + (np.log(nf / max_exact) / math.log(MAX_DISTANCE / max_exact)
                         * (N_BUCKETS - max_exact)).astype(np.int64)
    return np.where(n < max_exact, n, np.minimum(large, N_BUCKETS - 1)).astype(np.int32)


def _bias_by_distance(rel_bias, n):
    return rel_bias.astype(F32)[_bucket_np(np.arange(n))].T


def _overlap_t(n_cmp, nch, n_slc):
    c0 = np.arange(nch) * CMP_STRIDE
    s0 = np.arange(n_slc) * SLC_BLOCK
    ov = np.minimum(c0[None, :] + CMP_BLOCK, s0[:, None] + SLC_BLOCK) - np.maximum(c0[None, :], s0[:, None])
    ov = np.clip(ov, 0, None).astype(np.float32) / CMP_BLOCK
    ov[:, n_cmp:] = 0.0
    return jnp.asarray(ov, BF16)


def _pq_kernel(x0_ref, x1_ref, x2_ref, x3_ref, pos_ref, w_ref, o_ref, *, rows):
    for pair, x_ref in enumerate((x0_ref, x1_ref, x2_ref, x3_ref)):
        toks = [x_ref[pl.ds(l, rows, stride=CMP_STRIDE), :] for l in range(CMP_STRIDE)]
        for half in range(2):
            ch = 2 * pair + half
            c = ch // NSA_KV_HEADS
            lanes = slice(half * NSA_HEAD_DIM, (half + 1) * NSA_HEAD_DIM)
            chunk = jnp.concatenate([t[:, lanes] for t in toks], axis=1)
            for part in range(2):
                lhs = (chunk + pos_ref[c, part:part + 1, :]).astype(BF16)
                o_ref[0, ch, :, part * CMP_HIDDEN:(part + 1) * CMP_HIDDEN] = _dot(lhs, w_ref[c, part])


def _hid_kernel(pq_ref, w2_ref, o_ref, *, nch):
    p = pq_ref[0, 0, :, 0:CMP_HIDDEN]
    q = pltpu.roll(pq_ref[0, 0, :, CMP_HIDDEN:2 * CMP_HIDDEN], nch - 1, 0)
    hid = jax.nn.gelu(p + q, approximate=True)
    o_ref[0, 0] = _dot(hid.astype(BF16), w2_ref[0])


def _compress(kv2d, col_block, bsz, seq, cmp_pos, cmp_w1, cmp_w2, tt):
    nch = seq // CMP_STRIDE
    kdim = CMP_STRIDE * NSA_HEAD_DIM
    nt = seq // tt
    tn = tt // CMP_STRIDE
    w1 = cmp_w1.reshape(2, 2, kdim, CMP_HIDDEN).astype(BF16)
    pos = cmp_pos.reshape(2, 2, kdim)
    pq = pl.pallas_call(
        functools.partial(_pq_kernel, rows=tn),
        grid=(bsz, nt),
        in_specs=[pl.BlockSpec((tt, LANE), functools.partial(lambda b, i, k: (b * nt + i, col_block * 4 + k), k=k))
                  for k in range(4)] +
                 [pl.BlockSpec((2, 2, kdim), lambda b, i: (0, 0, 0)),
                  pl.BlockSpec((2, 2, kdim, CMP_HIDDEN), lambda b, i: (0, 0, 0, 0))],
        out_specs=pl.BlockSpec((1, 8, tn, 2 * CMP_HIDDEN), lambda b, i: (b, 0, i, 0)),
        out_shape=jax.ShapeDtypeStruct((bsz, 8, nch, 2 * CMP_HIDDEN), F32),
        compiler_params=_cparams(("arbitrary", "arbitrary")),
        name="cmp_pq",
    )(kv2d, kv2d, kv2d, kv2d, pos, w1)
    return _cmp_hid(pq, cmp_w2)


def _cmp_hid(pq, cmp_w2):
    bsz, _, nch, _ = pq.shape
    return pl.pallas_call(
        functools.partial(_hid_kernel, nch=nch),
        grid=(bsz, 8),
        in_specs=[pl.BlockSpec((1, 1, nch, 2 * CMP_HIDDEN), lambda b, c: (b, c, 0, 0)),
                  pl.BlockSpec((1, CMP_HIDDEN, NSA_HEAD_DIM), lambda b, c: (c // NSA_KV_HEADS, 0, 0))],
        out_specs=pl.BlockSpec((1, 1, nch, NSA_HEAD_DIM), lambda b, c: (b, c, 0, 0)),
        out_shape=jax.ShapeDtypeStruct((bsz, 8, nch, NSA_HEAD_DIM), F32),
        compiler_params=_cparams(("arbitrary", "arbitrary")),
        name="cmp_hid",
    )(pq, cmp_w2.astype(BF16))


def _top_n_mask(score_t, blk, n_rows):
    rank = jnp.zeros(score_t.shape, jnp.int32)
    for i in range(n_rows):
        row = score_t[i:i + 1, :]
        beats = (row > score_t) | ((row == score_t) & (blk > i))
        rank = rank + beats.astype(jnp.int32)
    return (rank < min(SLC_TOPN, n_rows)).astype(F32)


def _bias_from_buckets(bucket, table_ref, first_head):
    biases = [jnp.zeros(bucket.shape, F32) for _ in range(NSA_GROUP)]
    for k in range(N_BUCKETS):
        hit = bucket == k
        biases = [jnp.where(hit, table_ref[k, first_head + g], biases[g]) for g in range(NSA_GROUP)]
    return biases


def _cattn_kernel(tab_ref, q_ref, kc_ref, vc_ref, bk_ref, ovt_ref, o_ref, sel_ref, bias_scr, strip_scr,
                  *, tq, nch, n_cmp, n_slc):
    h = pl.program_id(0)
    i = pl.program_id(1)

    bands = tq // CMP_STRIDE
    for g, strip in enumerate(_bias_from_buckets(bk_ref[0], tab_ref, h * NSA_GROUP)):
        strip_scr[g] = strip
        for a in range(bands):
            off = bands - 1 - a
            bias_scr[g, a * CMP_STRIDE:(a + 1) * CMP_STRIDE, :] = strip_scr[g, :, off:off + nch]

    t = i * tq + lax.broadcasted_iota(jnp.int32, (tq, nch), 0)
    n = lax.broadcasted_iota(jnp.int32, (tq, nch), 1)
    valid = (t - CMP_STRIDE * n - (CMP_BLOCK - 1) >= 0) & (n < n_cmp)
    blk = lax.broadcasted_iota(jnp.int32, (n_slc, tq), 0)
    cur = (i * tq + lax.broadcasted_iota(jnp.int32, (n_slc, tq), 1)) // SLC_BLOCK
    forced = (blk == 0) | (blk == cur) | (blk == cur - 1)
    for bb in range(q_ref.shape[0]):
        kc = kc_ref[bb, 0].astype(BF16)
        vc = vc_ref[bb, 0].astype(BF16)
        pc = jnp.zeros((tq, nch), F32)
        for g in range(NSA_GROUP):
            sl = slice(g * NSA_HEAD_DIM, (g + 1) * NSA_HEAD_DIM)
            qg = (q_ref[bb, :, sl] * (NSA_HEAD_DIM ** -0.5)).astype(BF16)
            s = _dot_nt(qg, kc) + bias_scr[g]
            s = jnp.where(valid, s, NEG_BIG)
            e = jnp.exp(s - jnp.max(s, axis=1, keepdims=True))
            p = jnp.where(valid, e / jnp.sum(e, axis=1, keepdims=True), 0.0)
            o_ref[bb, :, sl] = _dot(p.astype(BF16), vc)
            pc = pc + p
        score_t = _dot_nt(ovt_ref[...], pc.astype(BF16))
        score_t = jnp.where(forced, FORCE_SCORE, score_t)
        score_t = jnp.where(blk > cur, -FORCE_SCORE, score_t)
        sel_ref[bb, 0] = _top_n_mask(score_t, blk, n_slc)


def _cattn(u, kvc, rel_bias, batch, seq, tq):
    nch = kvc.shape[2]
    n_cmp = nch - 1
    n_slc = seq // SLC_BLOCK
    ni = seq // tq
    ovt = _overlap_t(n_cmp, nch, n_slc)
    qcol = COL_QNSA // (NSA_GROUP * NSA_HEAD_DIM)
    bands = tq // CMP_STRIDE
    strip_w = -(-(nch + bands - 1) // LANE) * LANE
    dist = (tq * np.arange(ni)[:, None, None] + np.arange(CMP_STRIDE)[None, :, None]
            - CMP_STRIDE * (np.arange(strip_w)[None, None, :] - (bands - 1)) - (CMP_BLOCK - 1))
    buckets = jnp.asarray(_bucket_np(dist))
    o_c, sel_t = pl.pallas_call(
        functools.partial(_cattn_kernel, tq=tq, nch=nch, n_cmp=n_cmp, n_slc=n_slc),
        grid=(NSA_KV_HEADS, ni),
        in_specs=[pl.BlockSpec(memory_space=pltpu.SMEM),
                  pl.BlockSpec((batch, tq, NSA_GROUP * NSA_HEAD_DIM), lambda h, i: (0, i, qcol + h)),
                  pl.BlockSpec((batch, 1, nch, NSA_HEAD_DIM), lambda h, i: (0, h, 0, 0)),
                  pl.BlockSpec((batch, 1, nch, NSA_HEAD_DIM), lambda h, i: (0, NSA_KV_HEADS + h, 0, 0)),
                  pl.BlockSpec((1, CMP_STRIDE, strip_w), lambda h, i: (i, 0, 0)),
                  pl.BlockSpec((n_slc, nch), lambda h, i: (0, 0))],
        out_specs=[pl.BlockSpec((batch, tq, NSA_GROUP * NSA_HEAD_DIM), lambda h, i: (0, i, h)),
                   pl.BlockSpec((batch, 1, n_slc, tq), lambda h, i: (0, h, 0, i))],
        out_shape=[jax.ShapeDtypeStruct((batch, seq, D_NSA), F32),
                   jax.ShapeDtypeStruct((batch, NSA_KV_HEADS, n_slc, seq), F32)],
        scratch_shapes=[pltpu.VMEM((NSA_GROUP, tq, nch), F32), pltpu.VMEM((NSA_GROUP, CMP_STRIDE, strip_w), F32)],
        compiler_params=_cparams(("arbitrary", "arbitrary")),
        name="cmp_attn",
    )(rel_bias.astype(F32), u.reshape(batch, seq, u.shape[1]), kvc, kvc, buckets, ovt)
    return o_c.reshape(batch * seq, D_NSA), sel_t


TQ = 128
FLASH_GROUP = 8


def _flash_kernel(tab_ref, q_ref, kv_ref, bk_ref, *rest, selected, n_delta):
    if selected:
        sel_ref, o_ref, band_scr, qs_scr, m_scr, l_scr, acc_scr = rest
    else:
        o_ref, band_scr, qs_scr, m_scr, l_scr, acc_scr = rest
    h = pl.program_id(0)
    i = pl.program_id(1)
    n_batch = q_ref.shape[0]

    @pl.when(i == 0)
    def _():
        def fill(d, carry):
            for g, bias in enumerate(_bias_from_buckets(bk_ref[d], tab_ref, h * NSA_GROUP)):
                band_scr[d, :, g * TQ:(g + 1) * TQ] = bias
            return carry
        lax.fori_loop(0, n_delta, fill, 0)

    for bb in range(n_batch):
        for g in range(NSA_GROUP):
            qg = (q_ref[bb, :, g * NSA_HEAD_DIM:(g + 1) * NSA_HEAD_DIM] * (NSA_HEAD_DIM ** -0.5)).astype(BF16)
            qs_scr[bb, g * TQ:(g + 1) * TQ, :] = jnp.concatenate([qg, jnp.zeros_like(qg)], axis=1)
    is_key_lane = lax.broadcasted_iota(jnp.int32, (TQ, LANE), 1) < NSA_HEAD_DIM
    m_scr[...] = jnp.full(m_scr.shape, 0.5 * NEG_BIG, F32)
    l_scr[...] = jnp.zeros(l_scr.shape, F32)
    acc_scr[...] = jnp.zeros(acc_scr.shape, F32)
    key = lax.broadcasted_iota(jnp.int32, (TQ, TQ), 0)
    qry = lax.broadcasted_iota(jnp.int32, (TQ, TQ), 1)
    n_back = WINDOW // TQ

    def scores(bb, j, kind):
        kvj = kv_ref[bb, pl.ds(pl.multiple_of(j * TQ, TQ), TQ), :].astype(BF16)
        s = _dot_nt(kvj, qs_scr[bb]) + band_scr[i - j]
        mask = None
        if kind == "diag":
            mask = qry >= key
        elif kind == "far":
            mask = qry < key
        if selected:
            r = sel_ref[bb, 0, pl.ds(2 * j, 2), :]
            picked = jnp.where(key < SLC_BLOCK, r[0:1, :], r[1:2, :]) > 0.5
            mask = picked if mask is None else (mask & picked)
        if mask is not None:
            s = jnp.concatenate([jnp.where(mask, s[:, g * TQ:(g + 1) * TQ], NEG_BIG) for g in range(NSA_GROUP)], axis=1)
        return s, jnp.where(is_key_lane, jnp.ones_like(kvj), kvj)

    def update(tile_ids):
        all_tiles = [[scores(bb, j, kind) for j, kind in tile_ids] for bb in range(n_batch)]
        for bb, tiles in enumerate(all_tiles):
            m_old = m_scr[bb]
            m_new = m_old
            for s, _ in tiles:
                m_new = jnp.maximum(m_new, jnp.max(s, axis=0, keepdims=True))
            alpha = jnp.exp(m_old - m_new)
            l_new = alpha * l_scr[bb]
            acc = alpha * acc_scr[bb]
            for s, ones_v in tiles:
                p = jnp.exp(s - m_new)
                pv = _dot_tn(ones_v, p.astype(BF16))
                l_new = l_new + pv[0:1, :]
                acc = acc + pv[NSA_HEAD_DIM:2 * NSA_HEAD_DIM, :]
            m_scr[bb] = m_new
            l_scr[bb] = l_new
            acc_scr[bb] = acc

    def full_tiles(first, count):
        return [(first + k, "full") for k in range(count)]

    if selected:
        def group(t, carry):
            update(full_tiles(FLASH_GROUP * t, FLASH_GROUP))
            return carry

        n_groups = i // FLASH_GROUP
        lax.fori_loop(0, n_groups, group, 0)
        for rem in range(FLASH_GROUP):
            @pl.when(i - n_groups * FLASH_GROUP == rem)
            def _():
                update(full_tiles(n_groups * FLASH_GROUP, rem) + [(i, "diag")])
    else:
        @pl.when(i >= n_back)
        def _():
            update([(i - n_back, "far")] + full_tiles(i - n_back + 1, n_back - 1) + [(i, "diag")])

        for rem in range(n_back):
            @pl.when(i == rem)
            def _():
                update(full_tiles(0, rem) + [(i, "diag")])

    for bb in range(n_batch):
        for g in range(NSA_GROUP):
            cols = slice(g * TQ, (g + 1) * TQ)
            o_ref[bb, :, g * NSA_HEAD_DIM:(g + 1) * NSA_HEAD_DIM] = (acc_scr[bb, :, cols] / l_scr[bb, :, cols]).T


def _flash(u, kv_col, rel_bias, sel_t, batch, seq):
    ni = seq // TQ
    qcol = COL_QNSA // (NSA_GROUP * NSA_HEAD_DIM)
    selected = sel_t is not None
    n_delta = ni if selected else WINDOW // TQ + 1
    kvblk = kv_col // LANE
    delta = np.arange(n_delta)[:, None, None] * TQ + np.arange(TQ)[None, None, :] - np.arange(TQ)[None, :, None]
    buckets = jnp.asarray(_bucket_np(delta))
    u3 = u.reshape(batch, seq, u.shape[1])
    in_specs = [pl.BlockSpec(memory_space=pltpu.SMEM),
                pl.BlockSpec((batch, TQ, NSA_GROUP * NSA_HEAD_DIM), lambda h, i: (0, i, qcol + h)),
                pl.BlockSpec((batch, seq, LANE), lambda h, i: (0, 0, kvblk + h)),
                pl.BlockSpec((n_delta, TQ, TQ), lambda h, i: (0, 0, 0))]
    args = [rel_bias.astype(F32), u3, u3, buckets]
    if selected:
        n_slc = sel_t.shape[2]
        in_specs.append(pl.BlockSpec((batch, 1, n_slc, TQ), lambda h, i: (0, h, 0, i)))
        args.append(sel_t)
    scratch = [pltpu.VMEM((n_delta, TQ, NSA_GROUP * TQ), F32),
               pltpu.VMEM((batch, NSA_GROUP * TQ, LANE), BF16),
               pltpu.VMEM((batch, 1, NSA_GROUP * TQ), F32),
               pltpu.VMEM((batch, 1, NSA_GROUP * TQ), F32),
               pltpu.VMEM((batch, NSA_HEAD_DIM, NSA_GROUP * TQ), F32)]
    out = pl.pallas_call(
        functools.partial(_flash_kernel, selected=selected, n_delta=n_delta),
        grid=(NSA_KV_HEADS, ni),
        in_specs=in_specs,
        out_specs=pl.BlockSpec((batch, TQ, NSA_GROUP * NSA_HEAD_DIM), lambda h, i: (0, i, h)),
        out_shape=jax.ShapeDtypeStruct((batch, seq, D_NSA), F32),
        scratch_shapes=scratch,
        compiler_params=_cparams(("arbitrary", "arbitrary")),
        name="slc_attn" if selected else "win_attn",
    )(*args)
    return out.reshape(batch * seq, D_NSA)


def _gate_expand():
    e = np.zeros((3, 2 * LANE, D_NSA), np.float32)
    for br in range(3):
        for hd in range(NSA_HEADS):
            for part in range(2):
                e[br, part * LANE + 2 * ML_HEADS + br * NSA_HEADS + hd,
                  hd * NSA_HEAD_DIM:(hd + 1) * NSA_HEAD_DIM] = 1.0
    return jnp.asarray(e, BF16)


def _outproj_kernel(yml_ref, oc_ref, os_ref, ow_ref, s_ref, e_ref, ng_ref, w_ref, x_ref, o_ref, cat_scr):
    @pl.when(pl.program_id(1) == 0)
    def _():
        sig = jax.nn.sigmoid(s_ref[...])
        hi_lo = jnp.concatenate(_split2(sig), axis=1)
        o = jnp.zeros(oc_ref.shape, F32)
        for br, ref in enumerate((oc_ref, os_ref, ow_ref)):
            o = o + _dot(hi_lo, e_ref[br]) * ref[...]
        o = o * lax.rsqrt(jnp.mean(o * o, axis=-1, keepdims=True) + EPS) * ng_ref[...]
        cat_scr[:, 0:D_ML] = yml_ref[...]
        cat_scr[:, D_ML:D_ML + D_NSA] = o.astype(BF16)

    o_ref[...] = x_ref[...] + _dot(cat_scr[...], w_ref[...])


def _outproj(y_ml, o_c, o_s, o_w, small, nsa_g, w_out_b, x2d, tm, tn):
    n, d = x2d.shape
    rows = lambda i, j: (i, 0)
    return pl.pallas_call(
        _outproj_kernel,
        grid=(n // tm, d // tn),
        in_specs=[pl.BlockSpec((tm, D_ML), rows), pl.BlockSpec((tm, D_NSA), rows),
                  pl.BlockSpec((tm, D_NSA), rows), pl.BlockSpec((tm, D_NSA), rows),
                  pl.BlockSpec((tm, LANE), rows),
                  pl.BlockSpec((3, 2 * LANE, D_NSA), lambda i, j: (0, 0, 0)),
                  pl.BlockSpec((1, D_NSA), lambda i, j: (0, 0)),
                  pl.BlockSpec((D_ML + D_NSA, tn), lambda i, j: (0, j)),
                  pl.BlockSpec((tm, tn), lambda i, j: (i, j))],
        out_specs=pl.BlockSpec((tm, tn), lambda i, j: (i, j)),
        out_shape=jax.ShapeDtypeStruct((n, d), F32),
        scratch_shapes=[pltpu.VMEM((tm, D_ML + D_NSA), BF16)],
        compiler_params=_cparams(("arbitrary", "arbitrary")),
        name="outproj",
    )(y_ml, o_c, o_s, o_w, small, _gate_expand(), nsa_g, w_out_b, x2d)


ROUTE_COL = N_GROUPS
BIG_COL = 1 << 20
SPLIT = D_MODEL // LANE
DMA_UNROLL = 8


def _store_split(ref, val, n):
    for k in range(SPLIT):
        ref[pl.ds(k, n, stride=SPLIT), :] = val[:, k * LANE:(k + 1) * LANE]


def _load_split(ref, n):
    return jnp.concatenate([ref[pl.ds(k, n, stride=SPLIT), :] for k in range(SPLIT)], axis=1)


def _route_kernel(y_ref, g_ref, w_ref, b_ref, h_ref, info_ref, cnt_ref, carry_scr, *, tm):
    @pl.when(pl.program_id(0) == 0)
    def _():
        carry_scr[...] = jnp.zeros(carry_scr.shape, F32)

    x = y_ref[...]
    h = x * lax.rsqrt(jnp.mean(x * x, axis=-1, keepdims=True) + EPS) * g_ref[...]
    _store_split(h_ref, h, tm)
    logit = _dot(h.astype(BF16), w_ref[...]) + b_ref[...]
    col = lax.broadcasted_iota(jnp.int32, logit.shape, 1)
    is_grp = col < N_GROUPS
    gmax = jnp.max(jnp.where(is_grp, logit, -jnp.inf), axis=1, keepdims=True)
    gtop = jnp.min(jnp.where(is_grp & (logit == gmax), col, BIG_COL), axis=1, keepdims=True)
    gsum = jnp.sum(jnp.where(is_grp, jnp.exp(logit - gmax), 0.0), axis=1, keepdims=True)
    first = ROUTE_COL + gtop * EXPERTS_PER_GROUP
    in_grp = (col >= first) & (col < first + EXPERTS_PER_GROUP)
    v1 = jnp.max(jnp.where(in_grp, logit, -jnp.inf), axis=1, keepdims=True)
    i1 = jnp.min(jnp.where(in_grp & (logit == v1), col, BIG_COL), axis=1, keepdims=True)
    rest = in_grp & (col != i1)
    v2 = jnp.max(jnp.where(rest, logit, -jnp.inf), axis=1, keepdims=True)
    i2 = jnp.min(jnp.where(rest & (logit == v2), col, BIG_COL), axis=1, keepdims=True)
    e = jnp.exp(v2 - v1)
    w1 = 1.0 / ((1.0 + e) * gsum)
    w2 = e / ((1.0 + e) * gsum)
    pick1 = col == i1
    pick2 = col == i2
    both = (pick1 | pick2).astype(F32)
    ri = lax.broadcasted_iota(jnp.int32, (tm, tm), 0)
    ci = lax.broadcasted_iota(jnp.int32, (tm, tm), 1)
    before = (ci < ri).astype(BF16)
    cum = _dot(before, both.astype(BF16)) + carry_scr[...]
    r1 = jnp.sum(jnp.where(pick1, cum, 0.0), axis=1, keepdims=True)
    r2 = jnp.sum(jnp.where(pick2, cum, 0.0), axis=1, keepdims=True)
    carry_scr[...] = carry_scr[...] + jnp.sum(both, axis=0, keepdims=True)
    cnt_ref[...] = jnp.broadcast_to(carry_scr[...], cnt_ref.shape)
    info = jnp.where(col == 0, (i1 - ROUTE_COL).astype(F32), 0.0)
    info = jnp.where(col == 1, (i2 - ROUTE_COL).astype(F32), info)
    info = jnp.where(col == 2, w1, info)
    info = jnp.where(col == 3, w2, info)
    info = jnp.where(col == 4, r1, info)
    info_ref[...] = jnp.where(col == 5, r2, info)


def _route(y2d, g, w_rg, b_rg, w_re, b_re, tm):
    n, d = y2d.shape
    wr = jnp.concatenate([w_rg, w_re, jnp.zeros((d, LANE - N_GROUPS - N_EXPERTS), F32)], axis=1).astype(BF16)
    bias = jnp.concatenate([b_rg, b_re, jnp.zeros((LANE - N_GROUPS - N_EXPERTS,), F32)])[None, :]
    rows = lambda i: (i, 0)
    fixed = lambda i: (0, 0)
    return pl.pallas_call(
        functools.partial(_route_kernel, tm=tm),
        grid=(n // tm,),
        in_specs=[pl.BlockSpec((tm, d), rows), pl.BlockSpec((1, d), fixed),
                  pl.BlockSpec((d, LANE), fixed), pl.BlockSpec((1, LANE), fixed)],
        out_specs=[pl.BlockSpec((tm * SPLIT, LANE), rows), pl.BlockSpec((tm, LANE), rows),
                   pl.BlockSpec((8, LANE), fixed)],
        out_shape=[jax.ShapeDtypeStruct((n * SPLIT, LANE), F32), jax.ShapeDtypeStruct((n, LANE), F32),
                   jax.ShapeDtypeStruct((8, LANE), F32)],
        scratch_shapes=[pltpu.VMEM((1, LANE), F32)],
        compiler_params=_cparams(("arbitrary",)),
        name="moe_route",
    )(y2d, g, wr, bias)


W_CHUNKS = 4


def _expert_kernel(te_ref, nu_ref, src_ref, nx_ref, ws_ref, h_hbm, wg_hbm, wu_hbm, wd_hbm, o_ref,
                   xbuf, wg_f, wu_f, wd_f, wg_b, wu_b, wd_b, sem, wsem, *, tme):
    i = pl.program_id(0)
    slot = i % 2

    def weight_copies(expert, wslot):
        copies = []
        for hbm, buf in ((wg_hbm, wg_f), (wu_hbm, wu_f), (wd_hbm, wd_f)):
            step = hbm.shape[1] // W_CHUNKS
            for c in range(W_CHUNKS):
                copies.append(pltpu.make_async_copy(hbm.at[expert, pl.ds(c * step, step)],
                                                    buf.at[wslot, pl.ds(c * step, step)], wsem.at[wslot]))
        return copies

    def row_copy(tile, r, buf_slot):
        src_row = src_ref[tile * tme + r]
        return pltpu.make_async_copy(h_hbm.at[pl.ds(pl.multiple_of(src_row * SPLIT, SPLIT), SPLIT)],
                                     xbuf.at[buf_slot, pl.ds(pl.multiple_of(r * SPLIT, SPLIT), SPLIT)],
                                     sem.at[buf_slot])

    def fetch(tile, buf_slot):
        def body(r, carry):
            row_copy(tile, r, buf_slot).start()
            return carry
        lax.fori_loop(0, tme, body, 0, unroll=DMA_UNROLL)

    @pl.when(i == 0)
    def _():
        fetch(0, 0)
        for cp in weight_copies(te_ref[0], ws_ref[0]):
            cp.start()

    @pl.when(i + 1 < nu_ref[0])
    def _():
        fetch(i + 1, 1 - slot)

    @pl.when(i < nu_ref[0])
    def _():
        prev = te_ref[jnp.maximum(i - 1, 0)]

        @pl.when((i == 0) | (te_ref[i] != prev))
        def _():
            w = ws_ref[i]
            for cp in weight_copies(te_ref[i], w):
                cp.wait()
            wg_b[...] = wg_f[w].astype(BF16)
            wu_b[...] = wu_f[w].astype(BF16)
            wd_b[...] = wd_f[w].astype(BF16)

            @pl.when(nx_ref[i] >= 0)
            def _():
                for cp in weight_copies(nx_ref[i], 1 - w):
                    cp.start()

        def wait_body(r, carry):
            row_copy(i, r, slot).wait()
            return carry
        lax.fori_loop(0, tme, wait_body, 0, unroll=DMA_UNROLL)

        x = _load_split(xbuf.at[slot], tme).astype(BF16)
        a = _dot(x, wg_b[...])
        u = _dot(x, wu_b[...])
        hid = a * jax.nn.sigmoid(a) * u
        _store_split(o_ref, _dot(hid.astype(BF16), wd_b[...]), tme)

    @pl.when(i >= nu_ref[0])
    def _():
        o_ref[...] = jnp.zeros(o_ref.shape, F32)


def _experts(h, src, tile_expert, n_used, next_expert, weight_slot, w_gate, w_up, w_down, tme):
    p = src.shape[0]
    d = D_MODEL
    hbm = pl.BlockSpec(memory_space=pl.ANY)
    return pl.pallas_call(
        functools.partial(_expert_kernel, tme=tme),
        grid_spec=pltpu.PrefetchScalarGridSpec(
            num_scalar_prefetch=5, grid=(p // tme,),
            in_specs=[hbm, hbm, hbm, hbm],
            out_specs=pl.BlockSpec((tme * SPLIT, LANE), lambda i, *_: (i, 0)),
            scratch_shapes=[pltpu.VMEM((2, tme * SPLIT, LANE), F32),
                            pltpu.VMEM((2, d, D_EXPERT), F32), pltpu.VMEM((2, d, D_EXPERT), F32),
                            pltpu.VMEM((2, D_EXPERT, d), F32),
                            pltpu.VMEM((d, D_EXPERT), BF16), pltpu.VMEM((d, D_EXPERT), BF16),
                            pltpu.VMEM((D_EXPERT, d), BF16),
                            pltpu.SemaphoreType.DMA((2,)), pltpu.SemaphoreType.DMA((2,))]),
        out_shape=jax.ShapeDtypeStruct((p * SPLIT, LANE), F32),
        compiler_params=_cparams(("arbitrary",)),
        name="moe_experts",
    )(tile_expert, n_used, src, next_expert, weight_slot, h, w_gate, w_up, w_down)


def _combine_kernel(d1_ref, d2_ref, y_ref, info_ref, ys_hbm, g_ref, o_ref, a_buf, b_buf, sem, *, tm):
    i = pl.program_id(0)
    slot = i % 2

    def copy(r, row, buf, buf_slot):
        return pltpu.make_async_copy(ys_hbm.at[pl.ds(pl.multiple_of(row * SPLIT, SPLIT), SPLIT)],
                                     buf.at[buf_slot, pl.ds(pl.multiple_of(r * SPLIT, SPLIT), SPLIT)],
                                     sem.at[buf_slot])

    def fetch(tile, buf_slot):
        def issue(r, carry):
            copy(r, d1_ref[tile * tm + r], a_buf, buf_slot).start()
            copy(r, d2_ref[tile * tm + r], b_buf, buf_slot).start()
            return carry
        lax.fori_loop(0, tm, issue, 0, unroll=DMA_UNROLL)

    @pl.when(i == 0)
    def _():
        fetch(0, 0)

    @pl.when(i + 1 < pl.num_programs(0))
    def _():
        fetch(i + 1, 1 - slot)

    def drain(r, carry):
        copy(r, 0, a_buf, slot).wait()
        copy(r, 0, b_buf, slot).wait()
        return carry

    lax.fori_loop(0, tm, drain, 0, unroll=DMA_UNROLL)
    w1 = info_ref[:, 2:3]
    w2 = info_ref[:, 3:4]
    y = y_ref[...] + w1 * _load_split(a_buf.at[slot], tm) + w2 * _load_split(b_buf.at[slot], tm)
    o_ref[...] = y * lax.rsqrt(jnp.mean(y * y, axis=-1, keepdims=True) + EPS) * g_ref[...]


def _combine(y2d, info, ys, d1, d2, g, tm):
    n, d = y2d.shape
    return pl.pallas_call(
        functools.partial(_combine_kernel, tm=tm),
        grid_spec=pltpu.PrefetchScalarGridSpec(
            num_scalar_prefetch=2, grid=(n // tm,),
            in_specs=[pl.BlockSpec((tm, d), lambda i, a, b: (i, 0)),
                      pl.BlockSpec((tm, LANE), lambda i, a, b: (i, 0)),
                      pl.BlockSpec(memory_space=pl.ANY),
                      pl.BlockSpec((1, d), lambda i, a, b: (0, 0))],
            out_specs=pl.BlockSpec((tm, d), lambda i, a, b: (i, 0)),
            scratch_shapes=[pltpu.VMEM((2, tm * SPLIT, LANE), F32), pltpu.VMEM((2, tm * SPLIT, LANE), F32),
                            pltpu.SemaphoreType.DMA((2,))]),
        out_shape=jax.ShapeDtypeStruct((n, d), F32),
        compiler_params=_cparams(("arbitrary",)),
        name="moe_combine",
    )(d1, d2, y2d, info, ys, g)


def _moe_final(y2d, ffn_g, w_rg, b_rg, w_re, b_re, w_gate, w_up, w_down, final_g, tm, tme):
    n, d = y2d.shape
    h, info, cnt = _route(y2d, ffn_g, w_rg, b_rg, w_re, b_re, tm)
    e1 = info[:, 0].astype(jnp.int32)
    e2 = info[:, 1].astype(jnp.int32)
    counts = cnt[0, ROUTE_COL:ROUTE_COL + N_EXPERTS].astype(jnp.int32)
    padded = (counts + tme - 1) // tme * tme
    ends = jnp.cumsum(padded)
    offs = ends - padded
    d1 = offs[e1] + info[:, 4].astype(jnp.int32)
    d2 = offs[e2] + info[:, 5].astype(jnp.int32)
    p = 2 * n + N_EXPERTS * tme
    tok = jnp.arange(n, dtype=jnp.int32)
    src = jnp.zeros((p,), jnp.int32).at[jnp.concatenate([d1, d2])].set(jnp.concatenate([tok, tok]))
    n_tiles = p // tme
    n_used = (ends[-1] // tme).astype(jnp.int32).reshape(1)
    tile_start = jnp.arange(n_tiles, dtype=jnp.int32) * tme
    tile_expert = jnp.sum((ends[None, :] <= tile_start[:, None]).astype(jnp.int32), axis=1)
    last = jnp.sum((ends <= ends[-1] - 1).astype(jnp.int32))
    tile_expert = jnp.minimum(tile_expert, last).astype(jnp.int32)
    run = jnp.cumsum(jnp.concatenate([jnp.zeros((1,), jnp.int32),
                                      (tile_expert[1:] != tile_expert[:-1]).astype(jnp.int32)]))
    weight_slot = (run % 2).astype(jnp.int32)
    after = ends[tile_expert] // tme
    next_expert = jnp.where(after < n_used[0], tile_expert[jnp.minimum(after, n_tiles - 1)], -1).astype(jnp.int32)
    ys = _experts(h, src, tile_expert, n_used, next_expert, weight_slot, w_gate, w_up, w_down, tme)
    return _combine(y2d, info, ys, d1, d2, final_g, tm)


GROUP_ROWS = 8
SLC_LANES = 384


def _sample_cattn_kernel(q_ref, kc_ref, vc_ref, b_ref, ov_ref, o_ref, sel_ref, *, n_cmp, n_slc):
    q = (q_ref[0, 0] * (NSA_HEAD_DIM ** -0.5)).astype(BF16)
    s = _dot_nt(q, kc_ref[0, 0].astype(BF16)) + b_ref[0]
    valid = lax.broadcasted_iota(jnp.int32, s.shape, 1) < n_cmp
    s = jnp.where(valid, s, NEG_BIG)
    e = jnp.exp(s - jnp.max(s, axis=1, keepdims=True))
    p = jnp.where(valid, e / jnp.sum(e, axis=1, keepdims=True), 0.0)
    o_ref[0, 0] = _dot(p.astype(BF16), vc_ref[0, 0].astype(BF16))
    pg = jnp.where(lax.broadcasted_iota(jnp.int32, p.shape, 0) < NSA_GROUP, p, 0.0)
    pc = jnp.broadcast_to(jnp.sum(pg, axis=0, keepdims=True), pg.shape)
    score = _dot(pc.astype(BF16), ov_ref[...])[0:1, :]
    blk = lax.broadcasted_iota(jnp.int32, score.shape, 1)
    cur = n_slc - 1
    forced = (blk == 0) | (blk == cur) | (blk == cur - 1)
    score = jnp.where(forced, FORCE_SCORE, score)
    score = jnp.where(blk > cur, -jnp.inf, score)
    ri = lax.broadcasted_iota(jnp.int32, (SLC_LANES, SLC_LANES), 0)
    ci = lax.broadcasted_iota(jnp.int32, (SLC_LANES, SLC_LANES), 1)
    sb = jnp.broadcast_to(score, (SLC_LANES, SLC_LANES))
    col = jnp.sum(jnp.where(ri == ci, sb, 0.0), axis=1, keepdims=True)
    beats = (col > sb) | ((col == sb) & (ri < ci))
    rank = jnp.sum(beats.astype(jnp.int32), axis=0, keepdims=True)
    sel_ref[0, 0] = jnp.broadcast_to((rank < SLC_TOPN).astype(F32), (GROUP_ROWS, SLC_LANES))


def _sample_cattn(q8, kvc, bias_cs, n_cmp, n_slc):
    bsz, _, nch, _ = kvc.shape
    c0 = np.arange(nch) * CMP_STRIDE
    s0 = np.arange(SLC_LANES) * SLC_BLOCK
    ov = np.minimum(c0[:, None] + CMP_BLOCK, s0[None, :] + SLC_BLOCK) - np.maximum(c0[:, None], s0[None, :])
    ov = np.clip(ov, 0, None).astype(np.float32) / CMP_BLOCK
    ov[n_cmp:, :] = 0.0
    ov[:, n_slc:] = 0.0
    return pl.pallas_call(
        functools.partial(_sample_cattn_kernel, n_cmp=n_cmp, n_slc=n_slc),
        grid=(bsz, NSA_KV_HEADS),
        in_specs=[pl.BlockSpec((1, 1, GROUP_ROWS, NSA_HEAD_DIM), lambda b, h: (b, h, 0, 0)),
                  pl.BlockSpec((1, 1, nch, NSA_HEAD_DIM), lambda b, h: (b, h, 0, 0)),
                  pl.BlockSpec((1, 1, nch, NSA_HEAD_DIM), lambda b, h: (b, NSA_KV_HEADS + h, 0, 0)),
                  pl.BlockSpec((1, GROUP_ROWS, nch), lambda b, h: (h, 0, 0)),
                  pl.BlockSpec((nch, SLC_LANES), lambda b, h: (0, 0))],
        out_specs=[pl.BlockSpec((1, 1, GROUP_ROWS, NSA_HEAD_DIM), lambda b, h: (b, h, 0, 0)),
                   pl.BlockSpec((1, 1, GROUP_ROWS, SLC_LANES), lambda b, h: (b, h, 0, 0))],
        out_shape=[jax.ShapeDtypeStruct((bsz, NSA_KV_HEADS, GROUP_ROWS, NSA_HEAD_DIM), F32),
                   jax.ShapeDtypeStruct((bsz, NSA_KV_HEADS, GROUP_ROWS, SLC_LANES), F32)],
        compiler_params=_cparams(("arbitrary", "arbitrary")),
        name="sample_cmp_attn",
    )(q8, kvc, kvc, bias_cs, jnp.asarray(ov, BF16))


def _sample_attn_kernel(pg_ref, hf_ref, q_ref, cache_hbm, snew_ref, bs_ref, win_ref, wnew_ref, bw_ref,
                        os_ref, ow_ref, kt_buf, vt_buf, sem, *, n_gather, wlen):
    b = pl.program_id(0)
    rows_per_page = 2 * NSA_KV_HEADS * NSA_HEAD_DIM

    def page_copy(h, s, c, buf):
        page = pg_ref[(b * NSA_KV_HEADS + h) * n_gather + s]
        start = pl.multiple_of(page * rows_per_page + (c * NSA_KV_HEADS + h) * NSA_HEAD_DIM, NSA_HEAD_DIM)
        return pltpu.make_async_copy(cache_hbm.at[pl.ds(start, NSA_HEAD_DIM)],
                                     buf.at[h, :, pl.ds(s * PAGE_SIZE, PAGE_SIZE)], sem)

    for h in range(NSA_KV_HEADS):
        for s in range(n_gather):
            page_copy(h, s, 0, kt_buf).start()
            page_copy(h, s, 1, vt_buf).start()

    def attend(q, kts, vts, bias, valid):
        s = jnp.concatenate([_dot(q, kt.astype(BF16)) for kt in kts], axis=1) + bias
        s = jnp.where(valid, s, NEG_BIG)
        e = jnp.exp(s - jnp.max(s, axis=1, keepdims=True))
        p = jnp.where(valid, e / jnp.sum(e, axis=1, keepdims=True), 0.0).astype(BF16)
        out = None
        lo = 0
        for vt in vts:
            n = vt.shape[1]
            part = _dot_nt(p[:, lo:lo + n], vt.astype(BF16))
            out = part if out is None else out + part
            lo += n
        return out

    def head_rows(ref, c, h):
        r0 = (c * NSA_KV_HEADS + h) * NSA_HEAD_DIM
        return ref[0, r0:r0 + NSA_HEAD_DIM, :]

    widx = lax.broadcasted_iota(jnp.int32, (GROUP_ROWS, wlen + PAGE_SIZE), 1)
    wvalid = (widx >= wlen + 1 - WINDOW) & (widx <= wlen)
    for h in range(NSA_KV_HEADS):
        q = (q_ref[0, h] * (NSA_HEAD_DIM ** -0.5)).astype(BF16)
        ow_ref[0, h] = attend(q, [head_rows(win_ref, 0, h), head_rows(wnew_ref, 0, h)],
                              [head_rows(win_ref, 1, h), head_rows(wnew_ref, 1, h)], bw_ref[h], wvalid)

    for h in range(NSA_KV_HEADS):
        for s in range(n_gather):
            page_copy(h, s, 0, kt_buf).wait()
            page_copy(h, s, 1, vt_buf).wait()

    keys = n_gather * PAGE_SIZE
    lane = lax.broadcasted_iota(jnp.int32, (GROUP_ROWS, keys + PAGE_SIZE), 1)
    slot = lane >> 7
    lane_half = (lane >> 6) & 1
    for h in range(NSA_KV_HEADS):
        want = jnp.full(lane.shape, -1, jnp.int32)
        for s in range(n_gather):
            want = jnp.where(slot == s, hf_ref[(b * NSA_KV_HEADS + h) * n_gather + s], want)
        svalid = (lane_half == want) | (lane == keys)
        q = (q_ref[0, h] * (NSA_HEAD_DIM ** -0.5)).astype(BF16)
        os_ref[0, h] = attend(q, [kt_buf[h], head_rows(snew_ref, 0, h)],
                              [vt_buf[h], head_rows(snew_ref, 1, h)], bs_ref[0, h], svalid)


def _sample_attn(pages, halves, q8, cache_t, snew_t, bias_sel, win_t, wnew_t, bias_w, n_gather):
    bsz = q8.shape[0]
    wlen = win_t.shape[2]
    keys = n_gather * PAGE_SIZE
    rows = 2 * NSA_KV_HEADS * NSA_HEAD_DIM
    qspec = pl.BlockSpec((1, NSA_KV_HEADS, GROUP_ROWS, NSA_HEAD_DIM), lambda b, pg, hf: (b, 0, 0, 0))
    newspec = pl.BlockSpec((1, rows, PAGE_SIZE), lambda b, pg, hf: (b, 0, 0))
    return pl.pallas_call(
        functools.partial(_sample_attn_kernel, n_gather=n_gather, wlen=wlen),
        grid_spec=pltpu.PrefetchScalarGridSpec(
            num_scalar_prefetch=2, grid=(bsz,),
            in_specs=[qspec,
                      pl.BlockSpec(memory_space=pl.ANY),
                      newspec,
                      pl.BlockSpec((1, NSA_KV_HEADS, GROUP_ROWS, keys + PAGE_SIZE), lambda b, pg, hf: (b, 0, 0, 0)),
                      pl.BlockSpec((1, rows, wlen), lambda b, pg, hf: (b, 0, 0)),
                      newspec,
                      pl.BlockSpec((NSA_KV_HEADS, GROUP_ROWS, wlen + PAGE_SIZE), lambda b, pg, hf: (0, 0, 0))],
            out_specs=[qspec, qspec],
            scratch_shapes=[pltpu.VMEM((NSA_KV_HEADS, NSA_HEAD_DIM, keys), F32),
                            pltpu.VMEM((NSA_KV_HEADS, NSA_HEAD_DIM, keys), F32),
                            pltpu.SemaphoreType.DMA(())]),
        out_shape=[jax.ShapeDtypeStruct(q8.shape, F32), jax.ShapeDtypeStruct(q8.shape, F32)],
        compiler_params=_cparams(("arbitrary",)),
        name="sample_slc_win_attn",
    )(pages, halves, q8, cache_t, snew_t, bias_sel, win_t, wnew_t, bias_w)


PAGES_PER_STEP = 16


def _paged_pq_kernel(pt_ref, cache_hbm, pos_ref, w_ref, o_ref, pbuf, tok_scr, sem):
    b = pl.program_id(0)
    g = pl.program_id(1)
    ng = pl.num_programs(1)
    lin = b * ng + g
    slot = lin % 2
    rows_per_page = 2 * NSA_KV_HEADS * NSA_HEAD_DIM
    chunks = PAGES_PER_STEP * (PAGE_SIZE // CMP_STRIDE)

    def page_copy(step, p, buf_slot):
        sb = step // ng
        sg = step - sb * ng
        page = pt_ref[sb, sg * PAGES_PER_STEP + p]
        return pltpu.make_async_copy(cache_hbm.at[pl.ds(pl.multiple_of(page * rows_per_page, rows_per_page),
                                                        rows_per_page)],
                                     pbuf.at[buf_slot, p], sem.at[buf_slot])

    def fetch(step, buf_slot):
        for p in range(PAGES_PER_STEP):
            page_copy(step, p, buf_slot).start()

    @pl.when(lin == 0)
    def _():
        fetch(0, 0)

    @pl.when(lin + 1 < pl.num_programs(0) * ng)
    def _():
        fetch(lin + 1, 1 - slot)

    for p in range(PAGES_PER_STEP):
        page_copy(lin, p, slot).wait()

    for c in range(2):
        head_chunks = []
        for pair in (2 * c, 2 * c + 1):
            for p in range(PAGES_PER_STEP):
                tok_scr[p * PAGE_SIZE:(p + 1) * PAGE_SIZE, :] = pbuf[slot, p, pair * LANE:(pair + 1) * LANE, :].T
            toks = [tok_scr[pl.ds(l, chunks, stride=CMP_STRIDE), :] for l in range(CMP_STRIDE)]
            for half in range(2):
                lanes = slice(half * NSA_HEAD_DIM, (half + 1) * NSA_HEAD_DIM)
                head_chunks.append(jnp.concatenate([t[:, lanes] for t in toks], axis=1))
        stacked = jnp.concatenate(head_chunks, axis=0)
        for part in range(2):
            lhs = (stacked + pos_ref[c, part:part + 1, :]).astype(BF16)
            out = _dot(lhs, w_ref[c, part])
            for hh in range(NSA_KV_HEADS):
                o_ref[0, c * NSA_KV_HEADS + hh, :, part * CMP_HIDDEN:(part + 1) * CMP_HIDDEN] = (
                    out[hh * chunks:(hh + 1) * chunks, :])


def _paged_pq(cache_t, page_table, cmp_pos, cmp_w1):
    db, n_pages = page_table.shape
    kdim = CMP_STRIDE * NSA_HEAD_DIM
    nch = n_pages * (PAGE_SIZE // CMP_STRIDE)
    tn = PAGES_PER_STEP * (PAGE_SIZE // CMP_STRIDE)
    rows_per_page = 2 * NSA_KV_HEADS * NSA_HEAD_DIM
    return pl.pallas_call(
        _paged_pq_kernel,
        grid_spec=pltpu.PrefetchScalarGridSpec(
            num_scalar_prefetch=1, grid=(db, n_pages // PAGES_PER_STEP),
            in_specs=[pl.BlockSpec(memory_space=pl.ANY),
                      pl.BlockSpec((2, 2, kdim), lambda b, g, pt: (0, 0, 0)),
                      pl.BlockSpec((2, 2, kdim, CMP_HIDDEN), lambda b, g, pt: (0, 0, 0, 0))],
            out_specs=pl.BlockSpec((1, 8, tn, 2 * CMP_HIDDEN), lambda b, g, pt: (b, 0, g, 0)),
            scratch_shapes=[pltpu.VMEM((2, PAGES_PER_STEP, rows_per_page, PAGE_SIZE), F32),
                            pltpu.VMEM((PAGES_PER_STEP * PAGE_SIZE, LANE), F32),
                            pltpu.SemaphoreType.DMA((2,))]),
        out_shape=jax.ShapeDtypeStruct((db, 8, nch, 2 * CMP_HIDDEN), F32),
        compiler_params=_cparams(("arbitrary", "arbitrary")),
        name="cmp_pq_paged",
    )(page_table, cache_t, cmp_pos.reshape(2, 2, kdim), cmp_w1.reshape(2, 2, kdim, CMP_HIDDEN).astype(BF16))


def _pad_rows(a, n):
    return jnp.concatenate([a, jnp.zeros((n - a.shape[0],) + a.shape[1:], a.dtype)], axis=0)


def _sample_mixer(x_sample, cache_cmp, cache_slc, cache_win, st_c, st_n, st_m, st_conv, page_table,
                  norm_g, wb, gate_bias, conv_w, ml_norm_g, cmp_pos, cmp_w1, cmp_w2, nsa_norm_g, w_out_b, rel_bias):
    db = x_sample.shape[0]
    n_pages = page_table.shape[1]
    past = n_pages * PAGE_SIZE
    tok = SAMPLE_TOKEN_ROWS
    x16 = _pad_rows(x_sample.reshape(db, D_MODEL), tok)
    u = _proj(x16, norm_g, wb, tok, PROJ_COLS)
    small = u[:, COL_SMALL:COL_SMALL + LANE]

    T = SAMPLE_SEQ_ROWS
    useq = jnp.zeros((db, T, 4 * D_ML), F32)
    useq = useq.at[:, T - CONV_W:T - 1, 0:2 * D_ML].set(st_conv)
    useq = useq.at[:, T - 1, :].set(u[:db, 0:4 * D_ML])
    sseq = jnp.zeros((db, T, LANE), F32).at[:, :T - 1, LANE - 1].set(1.0)
    sseq = sseq.at[:, T - 1, :].set(small[:db])
    y_seq, conv_n, c_n, n_n, m_n = _mlstm(
        useq.reshape(db * T, 4 * D_ML), sseq.reshape(db * T, LANE), jnp.zeros((db, CONV_W - 1, 2 * D_ML), F32),
        st_c, st_n, st_m, conv_w, gate_bias, ml_norm_g, db, T, T)
    y_ml = y_seq.reshape(db, T, D_ML)[:, T - 1]

    n_pool = cache_cmp.shape[0]
    cmp_t = cache_cmp.transpose(0, 2, 3, 4, 1).reshape(n_pool * 2 * D_KV, PAGE_SIZE)
    kvc = _cmp_hid(_paged_pq(cmp_t, page_table, cmp_pos, cmp_w1), cmp_w2)
    nch = past // CMP_STRIDE
    n_cmp = (past + 1) // CMP_STRIDE - CMP_BLOCK // CMP_STRIDE + 1
    n_slc = -(-(past + 1) // SLC_BLOCK)
    q = u[:db, COL_QNSA:COL_QNSA + D_NSA].reshape(db, NSA_KV_HEADS, NSA_GROUP, NSA_HEAD_DIM)
    q8 = jnp.concatenate([q, jnp.zeros_like(q)], axis=2)
    bd = _bias_by_distance(rel_bias, past + 1).reshape(NSA_KV_HEADS, NSA_GROUP, past + 1)
    pad_g = lambda a: jnp.concatenate([a, jnp.zeros_like(a)], axis=1)
    dist_c = np.clip(past - (np.arange(nch) * CMP_STRIDE + CMP_BLOCK - 1), 0, None)
    o_c8, sel = _sample_cattn(q8, kvc, pad_g(bd[:, :, dist_c]), n_cmp, n_slc)

    mask = sel[:, :, 0, :n_slc] > 0.5
    idx = jnp.sort(jnp.where(mask, jnp.arange(n_slc, dtype=jnp.int32), jnp.int32(1 << 20)), axis=-1)
    n_gather = SLC_TOPN - 1
    idx = idx[..., :n_gather]
    pages_per_block = PAGE_SIZE // SLC_BLOCK
    logical_page = idx // pages_per_block
    pages = jnp.take_along_axis(page_table[:, None, :], logical_page, axis=2).reshape(-1).astype(jnp.int32)
    halves = (idx % pages_per_block).reshape(-1).astype(jnp.int32)
    kpos = (logical_page[..., None] * PAGE_SIZE + jnp.arange(PAGE_SIZE, dtype=jnp.int32)).reshape(db, NSA_KV_HEADS, -1)
    hh = jnp.arange(NSA_KV_HEADS)[None, :, None, None]
    gg = jnp.arange(NSA_GROUP)[None, None, :, None]
    bias_sel = bd[hh, gg, (past - kpos)[:, :, None, :]]
    bias_sel = jnp.concatenate([bias_sel, jnp.broadcast_to(bd[None, :, :, 0:1], (db, NSA_KV_HEADS, NSA_GROUP, 1)),
                                jnp.zeros((db, NSA_KV_HEADS, NSA_GROUP, PAGE_SIZE - 1), F32)], axis=-1)
    bias_sel = jnp.concatenate([bias_sel, jnp.zeros_like(bias_sel)], axis=2)
    wlen = cache_win.shape[1]
    dist_w = np.clip(wlen - np.arange(wlen + PAGE_SIZE), 0, None)
    kvs_new = _kv_from_per_head(u[:db, COL_KVS:COL_KVS + 2 * D_KV])
    kvw_new = _kv_from_per_head(u[:db, COL_KVW:COL_KVW + 2 * D_KV])
    lane_pad = lambda a: jnp.pad(a[:, :, None], ((0, 0), (0, 0), (0, PAGE_SIZE - 1)))
    slc_t = cache_slc.transpose(0, 2, 3, 4, 1).reshape(n_pool * 2 * D_KV, PAGE_SIZE)
    win_t = cache_win.transpose(0, 2, 3, 4, 1).reshape(db, 2 * D_KV, wlen)
    o_s8, o_w8 = _sample_attn(pages, halves, q8, slc_t, lane_pad(kvs_new), bias_sel, win_t, lane_pad(kvw_new),
                              pad_g(bd[:, :, dist_w]), n_gather)
    win2d = cache_win.reshape(db, wlen, 2 * D_KV)

    heads = lambda o: _pad_rows(o[:, :, :NSA_GROUP, :].reshape(db, D_NSA), tok)
    y = _outproj(_pad_rows(y_ml, tok), heads(o_c8), heads(o_s8), heads(o_w8), small, nsa_norm_g, w_out_b, x16, tok,
                 PROJ_COLS)
    kvshape = (1, db, 1, 2, NSA_KV_HEADS, NSA_HEAD_DIM)
    new_win = jnp.concatenate([win2d[:, 1:], kvw_new[:, None, :]], axis=1)
    states = (u[:db, COL_KVC:COL_KVC + 2 * D_KV].reshape(kvshape), kvs_new.reshape(kvshape),
              new_win.reshape((1, db, wlen, 2, NSA_KV_HEADS, NSA_HEAD_DIM)),
              c_n[None], n_n[None], m_n[None], conv_n[None])
    return y, states


def kernel(x_prompt, x_sample, cache_cmp_kv, cache_slc_kv, cache_win_kv, state_mlstm_C, state_mlstm_n,
           state_mlstm_m, state_conv, page_table, rel_bias, norm_mix_g, w_in, b_ig, b_fg, conv_w, ml_norm_g,
           cmp_pos, cmp_w1, cmp_w2, nsa_norm_g, w_out, norm_ffn_g, w_router_grp, b_router_grp, w_router_exp,
           b_router_exp, w_gate, w_up, w_down, norm_final_g):
    B, S, D = x_prompt.shape
    wb = _reorder_w_in(w_in[0])
    gate_bias = jnp.zeros((1, LANE), F32).at[0, 0:ML_HEADS].set(b_ig[0]).at[0, ML_HEADS:2 * ML_HEADS].set(b_fg[0])
    w_out_b = w_out[0].astype(BF16)
    yp, st_p = _prompt_mixer(x_prompt, norm_mix_g, wb, gate_bias, conv_w[0], ml_norm_g, cmp_pos[0], cmp_w1[0],
                             cmp_w2[0], nsa_norm_g, w_out_b, rel_bias)
    ys, st_s = _sample_mixer(x_sample, cache_cmp_kv[0], cache_slc_kv[0], cache_win_kv[0], state_mlstm_C[0],
                             state_mlstm_n[0], state_mlstm_m[0], state_conv[0], page_table, norm_mix_g, wb, gate_bias,
                             conv_w[0], ml_norm_g, cmp_pos[0], cmp_w1[0], cmp_w2[0], nsa_norm_g, w_out_b, rel_bias)
    moe_w = (norm_ffn_g, w_router_grp[0], b_router_grp[0], w_router_exp[0], b_router_exp[0],
             w_gate[0], w_up[0], w_down[0], norm_final_g[None, :])
    DB, L, _ = x_sample.shape
    out_p = _moe_final(yp.reshape(B * S, D), *moe_w, MOE_TOKEN_ROWS, MOE_EXPERT_ROWS).reshape(B, S, D)
    out_s = _moe_final(_pad_rows(ys, SAMPLE_MOE_ROWS), *moe_w, SAMPLE_MOE_ROWS, SAMPLE_EXPERT_ROWS)[:DB]
    out_s = out_s.reshape(DB, L, D)
    outs = [out_p, out_s]
    for a, b in zip(st_p, st_s):
        outs += [a, b]
    return tuple(outs)


def _prompt_mixer(x_prompt, norm_g, wb, gate_bias, conv_w, ml_norm_g, cmp_pos, cmp_w1, cmp_w2, nsa_norm_g,
                  w_out_b, rel_bias):
    B, S, D = x_prompt.shape
    x2d = x_prompt.reshape(B * S, D)
    tm = min(PROJ_ROWS, S)
    u, kvt_c, kvt_s, kvt_w = _proj(x2d, norm_g, wb, tm, PROJ_COLS, seq=S)
    small = u[:, COL_SMALL:COL_SMALL + LANE]
    y_ml, conv_n, c_n, n_n, m_n = _mlstm(
        u, small, jnp.zeros((B, CONV_W - 1, 2 * D_ML), F32),
        jnp.zeros((B, ML_HEADS, ML_HEAD_DIM, ML_HEAD_DIM), F32), jnp.zeros((B, ML_HEADS, ML_HEAD_DIM), F32),
        jnp.full((B, ML_HEADS), -jnp.inf, F32), conv_w, gate_bias, ml_norm_g, B, S, ML_CHUNK_ROWS)
    kvc = _compress(u, COL_KVC // (2 * D_KV), B, S, cmp_pos, cmp_w1, cmp_w2, min(S, CMP_TOKENS))
    o_c, sel_t = _cattn(u, kvc, rel_bias, B, S, min(S, CMP_QUERY_ROWS))
    o_s = _flash(u, COL_KVS, rel_bias, sel_t, B, S)
    o_w = _flash(u, COL_KVW, rel_bias, None, B, S)
    y = _outproj(y_ml, o_c, o_s, o_w, small, nsa_norm_g, w_out_b, x2d, tm, PROJ_COLS)
    win = min(WINDOW, S)
    rows = lambda kvt: kvt.transpose(0, 4, 1, 2, 3)[None]
    states = (rows(kvt_c), rows(kvt_s), rows(kvt_w[..., S - win:]),
              c_n[None], n_n[None], m_n[None], conv_n[None])
    return y.reshape(B, S, D), states
```

```python
import functools
import math

import numpy as np
import jax
import jax.numpy as jnp
from jax import lax
from jax.experimental import pallas as pl
from jax.experimental.pallas import tpu as pltpu

F32 = jnp.float32
BF16 = jnp.bfloat16

D_MODEL = 2048
ML_HEADS = 4
ML_HEAD_DIM = 256
D_ML = 1024
CONV_W = 4
NSA_HEADS = 16
NSA_HEAD_DIM = 64
D_NSA = 1024
NSA_KV_HEADS = 4
NSA_GROUP = 4
D_KV = 256
CMP_BLOCK = 32
CMP_STRIDE = 16
CMP_HIDDEN = 256
SLC_BLOCK = 64
SLC_TOPN = 16
WINDOW = 512
N_BUCKETS = 32
MAX_DISTANCE = 2048
N_GROUPS = 4
EXPERTS_PER_GROUP = 8
N_EXPERTS = 32
D_EXPERT = 512
PAGE_SIZE = 128
EPS = 1e-6
NEG_BIG = -1e30
FORCE_SCORE = 1e4

LANE = 128
COL_QML, COL_KML, COL_VML, COL_OML = 0, 1024, 2048, 3072
COL_QNSA = 4096
COL_KVC, COL_KVS, COL_KVW = 5120, 5632, 6144
COL_SMALL = 6656
N_PROJ = 7168
VMEM_LIMIT = 56 * 1024 * 1024

PROJ_ROWS = 1024
PROJ_COLS = 2 * D_KV
ML_CHUNK_ROWS = 256
CMP_TOKENS = 4096
CMP_QUERY_ROWS = 256
MOE_TOKEN_ROWS = 256
MOE_EXPERT_ROWS = 128
SAMPLE_TOKEN_ROWS = 16
SAMPLE_SEQ_ROWS = 128
SAMPLE_MOE_ROWS = 128
SAMPLE_EXPERT_ROWS = 16


def _cparams(sem, vmem=VMEM_LIMIT):
    return pltpu.CompilerParams(dimension_semantics=sem, vmem_limit_bytes=vmem)


def _split2(x):
    hi = x.astype(BF16)
    lo = (x - hi.astype(F32)).astype(BF16)
    return hi, lo


def _split3(x):
    hi = x.astype(BF16)
    r = x - hi.astype(F32)
    mid = r.astype(BF16)
    lo = (r - mid.astype(F32)).astype(BF16)
    return hi, mid, lo


def _dot(a, b):
    return jnp.dot(a, b, preferred_element_type=F32)


def _dot_nt(a, b):
    return lax.dot_general(a, b, (((1,), (1,)), ((), ())), preferred_element_type=F32)


def _dot_tn(a, b):
    return lax.dot_general(a, b, (((0,), (0,)), ((), ())), preferred_element_type=F32)


def _proj_kernel(x_ref, g_ref, w_ref, o_ref, *rest, kv_blocks):
    h_scr = rest[-1]
    j = pl.program_id(1)

    @pl.when(j == 0)
    def _():
        x = x_ref[...]
        ms = jnp.mean(x * x, axis=-1, keepdims=True)
        h_scr[...] = (x * lax.rsqrt(ms + EPS) * g_ref[...]).astype(BF16)

    res = _dot(h_scr[...], w_ref[...])
    o_ref[...] = res

    for (block, per_head), t_ref in zip(kv_blocks, rest[:-1]):
        @pl.when(j == block)
        def _():
            res_t = res.T
            for c in range(2):
                for hh in range(NSA_KV_HEADS):
                    piece = hh * 2 + c if per_head else c * NSA_KV_HEADS + hh
                    t_ref[0, c, hh] = res_t[piece * NSA_HEAD_DIM:(piece + 1) * NSA_HEAD_DIM, :]


def _proj(x2d, g, wb, tm, tn, seq=None):
    n, d = x2d.shape
    nc = wb.shape[1]
    kv_blocks = ()
    out_specs = [pl.BlockSpec((tm, tn), lambda i, j: (i, j))]
    out_shape = [jax.ShapeDtypeStruct((n, nc), F32)]
    if seq is not None:
        assert tn == 2 * D_KV and seq % tm == 0
        kv_blocks = ((COL_KVC // tn, False), (COL_KVS // tn, True), (COL_KVW // tn, True))
        per_seq = seq // tm
        for _ in kv_blocks:
            out_specs.append(pl.BlockSpec((1, 2, NSA_KV_HEADS, NSA_HEAD_DIM, tm),
                                          lambda i, j: (i // per_seq, 0, 0, 0, i % per_seq)))
            out_shape.append(jax.ShapeDtypeStruct((n // seq, 2, NSA_KV_HEADS, NSA_HEAD_DIM, seq), F32))
    outs = pl.pallas_call(
        functools.partial(_proj_kernel, kv_blocks=kv_blocks),
        grid=(n // tm, nc // tn),
        in_specs=[pl.BlockSpec((tm, d), lambda i, j: (i, 0)),
                  pl.BlockSpec((1, d), lambda i, j: (0, 0)),
                  pl.BlockSpec((d, tn), lambda i, j: (0, j))],
        out_specs=out_specs,
        out_shape=out_shape,
        scratch_shapes=[pltpu.VMEM((tm, d), BF16)],
        compiler_params=_cparams(("arbitrary", "arbitrary")),
        name="proj",
    )(x2d, g, wb)
    return outs if seq is not None else outs[0]


def _reorder_w_in(w_in):
    w_in = w_in.astype(BF16)
    big = w_in[:, :4 * D_ML]
    small_a = w_in[:, 4 * D_ML:4 * D_ML + 2 * ML_HEADS]
    rest = w_in[:, 4 * D_ML + 2 * ML_HEADS:]
    q_and_cmp = rest[:, :D_NSA + 2 * D_KV]
    gate = rest[:, D_NSA + 6 * D_KV:]
    d = w_in.shape[0]

    def per_head(w):
        return w.reshape(d, 2, NSA_KV_HEADS, NSA_HEAD_DIM).transpose(0, 2, 1, 3).reshape(d, 2 * D_KV)

    kv_s = per_head(rest[:, D_NSA + 2 * D_KV:D_NSA + 4 * D_KV])
    kv_w = per_head(rest[:, D_NSA + 4 * D_KV:D_NSA + 6 * D_KV])
    pad = jnp.zeros((d, N_PROJ - COL_SMALL - 2 * ML_HEADS - 3 * NSA_HEADS), w_in.dtype)
    return jnp.concatenate([big, q_and_cmp, kv_s, kv_w, small_a, gate, pad], axis=1).astype(BF16)


def _kv_from_per_head(kv2d):
    n = kv2d.shape[0]
    return kv2d.reshape(n, NSA_KV_HEADS, 2, NSA_HEAD_DIM).transpose(0, 2, 1, 3).reshape(n, 2 * D_KV)


def _log_sigmoid(x):
    return jnp.minimum(x, 0.0) - jnp.log1p(jnp.exp(-jnp.abs(x)))


def _mlstm_kernel(q_ref, k_ref, v_ref, o_ref, s_ref, cb_ref, c0_ref, n0_ref, m0_ref,
                  cw_ref, gb_ref, ng_ref,
                  y_ref, cbo_ref, co_ref, no_ref, mo_ref,
                  ext_scr, c_scr, n_scr, m_scr, *, T):
    c = pl.program_id(1)
    nc = pl.num_programs(1)

    @pl.when(c == 0)
    def _():
        ext_scr[0:8, :] = jnp.zeros((8, 2 * D_ML), F32)
        ext_scr[5:8, :] = cb_ref[0]
        c_scr[...] = c0_ref[0]
        n_scr[...] = n0_ref[0]
        m_scr[...] = m0_ref[0]

    ext_scr[8:8 + T, 0:D_ML] = q_ref[...]
    ext_scr[8:8 + T, D_ML:2 * D_ML] = k_ref[...]
    conv = ext_scr[5:5 + T, :] * cw_ref[0:1, :]
    for j in range(1, CONV_W):
        conv = conv + ext_scr[5 + j:5 + j + T, :] * cw_ref[j:j + 1, :]
    tail = ext_scr[8 + T - 3:8 + T, :]
    ext_scr[5:8, :] = tail
    cbo_ref[0] = tail
    qk = conv * jax.nn.sigmoid(conv)

    pre = s_ref[...] + gb_ref[...]
    col = lax.broadcasted_iota(jnp.int32, pre.shape, 1)
    padrow = s_ref[:, LANE - 1:LANE] > 0.5
    gates = jnp.where(col < ML_HEADS, pre, _log_sigmoid(pre))
    gates = jnp.where(padrow, jnp.where(col < ML_HEADS, NEG_BIG, 0.0), gates)
    g_r = gates.T
    ti = lax.broadcasted_iota(jnp.int32, (T, T), 0)
    si = lax.broadcasted_iota(jnp.int32, (T, T), 1)
    upper = (ti <= si).astype(BF16)
    g_fin = jnp.where(lax.broadcasted_iota(jnp.int32, g_r.shape, 0) < ML_HEADS, 0.0, g_r)
    hi, mid, lo = _split3(g_fin)
    cum_r = _dot(hi, upper) + _dot(mid, upper) + _dot(lo, upper)
    rowi = lax.broadcasted_iota(jnp.int32, g_r.shape, 0)
    a_r = jnp.where(rowi < ML_HEADS, g_r, cum_r)
    a_c = a_r.T
    causal = si <= ti

    for h in range(ML_HEADS):
        sl = slice(h * ML_HEAD_DIM, (h + 1) * ML_HEAD_DIM)
        q = qk[:, h * ML_HEAD_DIM:(h + 1) * ML_HEAD_DIM]
        k = qk[:, D_ML + h * ML_HEAD_DIM:D_ML + (h + 1) * ML_HEAD_DIM] * (ML_HEAD_DIM ** -0.5)
        v = v_ref[:, sl]
        ig_r = a_r[h:h + 1, :]
        b_r = a_r[ML_HEADS + h:ML_HEADS + h + 1, :]
        ig_c = a_c[:, h:h + 1]
        b_c = a_c[:, ML_HEADS + h:ML_HEADS + h + 1]
        m_prev = m_scr[h:h + 1, 0:1]
        logd = jnp.where(causal, b_c - b_r + ig_r, -jnp.inf)
        inter = b_c + m_prev
        m_t = jnp.maximum(inter, jnp.max(logd, axis=1, keepdims=True))
        w_intra = jnp.exp(logd - m_t)
        w_inter = jnp.exp(inter - m_t)
        qb = q.astype(BF16)
        kb = k.astype(BF16)
        vb = v.astype(BF16)
        sc = _dot_nt(qb, kb) * w_intra
        cmat = c_scr[h]
        nvec = n_scr[h:h + 1, :]
        num = _dot(sc.astype(BF16), vb) + w_inter * _dot(qb, cmat.astype(BF16))
        qn = jnp.sum(qb.astype(F32) * nvec.astype(BF16).astype(F32), axis=1, keepdims=True)
        den = jnp.sum(sc, axis=1, keepdims=True) + w_inter * qn
        hh = num / jnp.maximum(jnp.abs(den), jnp.exp(-m_t))
        m_new = m_t[T - 1:T, :]
        b_last = b_c[T - 1:T, :]
        w_s = jnp.exp(b_last - b_c + ig_c - m_new)
        decay = jnp.exp(b_last + m_prev - m_new)
        kw = k * w_s
        c_new = decay * cmat + _dot_tn(kw.astype(BF16), vb)
        n_new = decay * nvec + jnp.sum(kw, axis=0, keepdims=True)
        c_scr[h] = c_new
        n_scr[h:h + 1, :] = n_new
        m_scr[h:h + 1, :] = jnp.broadcast_to(m_new, (1, LANE))
        hn = hh * lax.rsqrt(jnp.mean(hh * hh, axis=1, keepdims=True) + EPS) * ng_ref[:, sl]
        y_ref[:, sl] = (hn * jax.nn.sigmoid(o_ref[:, sl])).astype(y_ref.dtype)

    @pl.when(c == nc - 1)
    def _():
        co_ref[0] = c_scr[...]
        no_ref[0] = n_scr[...]
        mo_ref[0] = m_scr[...]


def _mlstm(u, small, conv_buf, c0, n0, m0, conv_w, gate_bias, norm_g, batch, seq, T):
    nc = seq // T
    cb = D_ML // 1024
    m0b = jnp.broadcast_to(m0[:, :, None], (batch, ML_HEADS, LANE))
    m0b = jnp.concatenate([m0b, jnp.zeros((batch, 8 - ML_HEADS, LANE), F32)], axis=1)
    n0p = jnp.concatenate([n0, jnp.zeros((batch, 8 - ML_HEADS, ML_HEAD_DIM), F32)], axis=1)
    row = lambda b, c: (b * nc + c, 0)
    outs = pl.pallas_call(
        functools.partial(_mlstm_kernel, T=T),
        grid=(batch, nc),
        in_specs=[pl.BlockSpec((T, D_ML), lambda b, c: (b * nc + c, COL_QML // D_ML)),
                  pl.BlockSpec((T, D_ML), lambda b, c: (b * nc + c, COL_KML // D_ML)),
                  pl.BlockSpec((T, D_ML), lambda b, c: (b * nc + c, COL_VML // D_ML)),
                  pl.BlockSpec((T, D_ML), lambda b, c: (b * nc + c, COL_OML // D_ML)),
                  pl.BlockSpec((T, LANE), row),
                  pl.BlockSpec((1, CONV_W - 1, 2 * D_ML), lambda b, c: (b, 0, 0)),
                  pl.BlockSpec((1, ML_HEADS, ML_HEAD_DIM, ML_HEAD_DIM), lambda b, c: (b, 0, 0, 0)),
                  pl.BlockSpec((1, 8, ML_HEAD_DIM), lambda b, c: (b, 0, 0)),
                  pl.BlockSpec((1, 8, LANE), lambda b, c: (b, 0, 0)),
                  pl.BlockSpec((CONV_W, 2 * D_ML), lambda b, c: (0, 0)),
                  pl.BlockSpec((1, LANE), lambda b, c: (0, 0)),
                  pl.BlockSpec((1, D_ML), lambda b, c: (0, 0))],
        out_specs=[pl.BlockSpec((T, D_ML), row),
                   pl.BlockSpec((1, CONV_W - 1, 2 * D_ML), lambda b, c: (b, 0, 0)),
                   pl.BlockSpec((1, ML_HEADS, ML_HEAD_DIM, ML_HEAD_DIM), lambda b, c: (b, 0, 0, 0)),
                   pl.BlockSpec((1, 8, ML_HEAD_DIM), lambda b, c: (b, 0, 0)),
                   pl.BlockSpec((1, 8, LANE), lambda b, c: (b, 0, 0))],
        out_shape=[jax.ShapeDtypeStruct((batch * seq, D_ML), BF16),
                   jax.ShapeDtypeStruct((batch, CONV_W - 1, 2 * D_ML), F32),
                   jax.ShapeDtypeStruct((batch, ML_HEADS, ML_HEAD_DIM, ML_HEAD_DIM), F32),
                   jax.ShapeDtypeStruct((batch, 8, ML_HEAD_DIM), F32),
                   jax.ShapeDtypeStruct((batch, 8, LANE), F32)],
        scratch_shapes=[pltpu.VMEM((8 + T, 2 * D_ML), F32),
                        pltpu.VMEM((ML_HEADS, ML_HEAD_DIM, ML_HEAD_DIM), F32),
                        pltpu.VMEM((8, ML_HEAD_DIM), F32),
                        pltpu.VMEM((8, LANE), F32)],
        compiler_params=_cparams(("arbitrary", "arbitrary")),
        name="mlstm",
    )(u, u, u, u, small, conv_buf, c0, n0p, m0b, conv_w, gate_bias, norm_g)
    y, cbo, co, no, mo = outs
    return y, cbo, co, no[:, :ML_HEADS], mo[:, :ML_HEADS, 0]


def _bucket_np(dist):
    n = np.maximum(dist, 0)
    max_exact = N_BUCKETS // 2
    nf = np.maximum(n, 1).astype(np.float64)
    large = max_exact + (np.log(nf / max_exact) / math.log(MAX_DISTANCE / max_exact)
                         * (N_BUCKETS - max_exact)).astype(np.int64)
    return np.where(n < max_exact, n, np.minimum(large, N_BUCKETS - 1)).astype(np.int32)


def _bias_by_distance(rel_bias, n):
    return rel_bias.astype(F32)[_bucket_np(np.arange(n))].T


def _overlap_t(n_cmp, nch, n_slc):
    c0 = np.arange(nch) * CMP_STRIDE
    s0 = np.arange(n_slc) * SLC_BLOCK
    ov = np.minimum(c0[None, :] + CMP_BLOCK, s0[:, None] + SLC_BLOCK) - np.maximum(c0[None, :], s0[:, None])
    ov = np.clip(ov, 0, None).astype(np.float32) / CMP_BLOCK
    ov[:, n_cmp:] = 0.0
    return jnp.asarray(ov, BF16)


def _pq_kernel(x0_ref, x1_ref, x2_ref, x3_ref, pos_ref, w_ref, o_ref, *, rows):
    for pair, x_ref in enumerate((x0_ref, x1_ref, x2_ref, x3_ref)):
        toks = [x_ref[pl.ds(l, rows, stride=CMP_STRIDE), :] for l in range(CMP_STRIDE)]
        for half in range(2):
            ch = 2 * pair + half
            c = ch // NSA_KV_HEADS
            lanes = slice(half * NSA_HEAD_DIM, (half + 1) * NSA_HEAD_DIM)
            chunk = jnp.concatenate([t[:, lanes] for t in toks], axis=1)
            for part in range(2):
                lhs = (chunk + pos_ref[c, part:part + 1, :]).astype(BF16)
                o_ref[0, ch, :, part * CMP_HIDDEN:(part + 1) * CMP_HIDDEN] = _dot(lhs, w_ref[c, part])


def _hid_kernel(pq_ref, w2_ref, o_ref, *, nch):
    p = pq_ref[0, 0, :, 0:CMP_HIDDEN]
    q = pltpu.roll(pq_ref[0, 0, :, CMP_HIDDEN:2 * CMP_HIDDEN], nch - 1, 0)
    hid = jax.nn.gelu(p + q, approximate=True)
    o_ref[0, 0] = _dot(hid.astype(BF16), w2_ref[0])


def _compress(kv2d, col_block, bsz, seq, cmp_pos, cmp_w1, cmp_w2, tt):
    nch = seq // CMP_STRIDE
    kdim = CMP_STRIDE * NSA_HEAD_DIM
    nt = seq // tt
    tn = tt // CMP_STRIDE
    w1 = cmp_w1.reshape(2, 2, kdim, CMP_HIDDEN).astype(BF16)
    pos = cmp_pos.reshape(2, 2, kdim)
    pq = pl.pallas_call(
        functools.partial(_pq_kernel, rows=tn),
        grid=(bsz, nt),
        in_specs=[pl.BlockSpec((tt, LANE), functools.partial(lambda b, i, k: (b * nt + i, col_block * 4 + k), k=k))
                  for k in range(4)] +
                 [pl.BlockSpec((2, 2, kdim), lambda b, i: (0, 0, 0)),
                  pl.BlockSpec((2, 2, kdim, CMP_HIDDEN), lambda b, i: (0, 0, 0, 0))],
        out_specs=pl.BlockSpec((1, 8, tn, 2 * CMP_HIDDEN), lambda b, i: (b, 0, i, 0)),
        out_shape=jax.ShapeDtypeStruct((bsz, 8, nch, 2 * CMP_HIDDEN), F32),
        compiler_params=_cparams(("arbitrary", "arbitrary")),
        name="cmp_pq",
    )(kv2d, kv2d, kv2d, kv2d, pos, w1)
    return _cmp_hid(pq, cmp_w2)


def _cmp_hid(pq, cmp_w2):
    bsz, _, nch, _ = pq.shape
    return pl.pallas_call(
        functools.partial(_hid_kernel, nch=nch),
        grid=(bsz, 8),
        in_specs=[pl.BlockSpec((1, 1, nch, 2 * CMP_HIDDEN), lambda b, c: (b, c, 0, 0)),
                  pl.BlockSpec((1, CMP_HIDDEN, NSA_HEAD_DIM), lambda b, c: (c // NSA_KV_HEADS, 0, 0))],
        out_specs=pl.BlockSpec((1, 1, nch, NSA_HEAD_DIM), lambda b, c: (b, c, 0, 0)),
        out_shape=jax.ShapeDtypeStruct((bsz, 8, nch, NSA_HEAD_DIM), F32),
        compiler_params=_cparams(("arbitrary", "arbitrary")),
        name="cmp_hid",
    )(pq, cmp_w2.astype(BF16))


def _top_n_mask(score_t, blk, n_rows):
    rank = jnp.zeros(score_t.shape, jnp.int32)
    for i in range(n_rows):
        row = score_t[i:i + 1, :]
        beats = (row > score_t) | ((row == score_t) & (blk > i))
        rank = rank + beats.astype(jnp.int32)
    return (rank < min(SLC_TOPN, n_rows)).astype(F32)


def _bias_from_buckets(bucket, table_ref, first_head):
    biases = [jnp.zeros(bucket.shape, F32) for _ in range(NSA_GROUP)]
    for k in range(N_BUCKETS):
        hit = bucket == k
        biases = [jnp.where(hit, table_ref[k, first_head + g], biases[g]) for g in range(NSA_GROUP)]
    return biases


def _cattn_kernel(tab_ref, q_ref, kc_ref, vc_ref, bk_ref, ovt_ref, o_ref, sel_ref, bias_scr, strip_scr,
                  *, tq, nch, n_cmp, n_slc):
    h = pl.program_id(0)
    i = pl.program_id(1)

    bands = tq // CMP_STRIDE
    for g, strip in enumerate(_bias_from_buckets(bk_ref[0], tab_ref, h * NSA_GROUP)):
        strip_scr[g] = strip
        for a in range(bands):
            off = bands - 1 - a
            bias_scr[g, a * CMP_STRIDE:(a + 1) * CMP_STRIDE, :] = strip_scr[g, :, off:off + nch]

    t = i * tq + lax.broadcasted_iota(jnp.int32, (tq, nch), 0)
    n = lax.broadcasted_iota(jnp.int32, (tq, nch), 1)
    valid = (t - CMP_STRIDE * n - (CMP_BLOCK - 1) >= 0) & (n < n_cmp)
    blk = lax.broadcasted_iota(jnp.int32, (n_slc, tq), 0)
    cur = (i * tq + lax.broadcasted_iota(jnp.int32, (n_slc, tq), 1)) // SLC_BLOCK
    forced = (blk == 0) | (blk == cur) | (blk == cur - 1)
    for bb in range(q_ref.shape[0]):
        kc = kc_ref[bb, 0].astype(BF16)
        vc = vc_ref[bb, 0].astype(BF16)
        pc = jnp.zeros((tq, nch), F32)
        for g in range(NSA_GROUP):
            sl = slice(g * NSA_HEAD_DIM, (g + 1) * NSA_HEAD_DIM)
            qg = (q_ref[bb, :, sl] * (NSA_HEAD_DIM ** -0.5)).astype(BF16)
            s = _dot_nt(qg, kc) + bias_scr[g]
            s = jnp.where(valid, s, NEG_BIG)
            e = jnp.exp(s - jnp.max(s, axis=1, keepdims=True))
            p = jnp.where(valid, e / jnp.sum(e, axis=1, keepdims=True), 0.0)
            o_ref[bb, :, sl] = _dot(p.astype(BF16), vc)
            pc = pc + p
        score_t = _dot_nt(ovt_ref[...], pc.astype(BF16))
        score_t = jnp.where(forced, FORCE_SCORE, score_t)
        score_t = jnp.where(blk > cur, -FORCE_SCORE, score_t)
        sel_ref[bb, 0] = _top_n_mask(score_t, blk, n_slc)


def _cattn(u, kvc, rel_bias, batch, seq, tq):
    nch = kvc.shape[2]
    n_cmp = nch - 1
    n_slc = seq // SLC_BLOCK
    ni = seq // tq
    ovt = _overlap_t(n_cmp, nch, n_slc)
    qcol = COL_QNSA // (NSA_GROUP * NSA_HEAD_DIM)
    bands = tq // CMP_STRIDE
    strip_w = -(-(nch + bands - 1) // LANE) * LANE
    dist = (tq * np.arange(ni)[:, None, None] + np.arange(CMP_STRIDE)[None, :, None]
            - CMP_STRIDE * (np.arange(strip_w)[None, None, :] - (bands - 1)) - (CMP_BLOCK - 1))
    buckets = jnp.asarray(_bucket_np(dist))
    o_c, sel_t = pl.pallas_call(
        functools.partial(_cattn_kernel, tq=tq, nch=nch, n_cmp=n_cmp, n_slc=n_slc),
        grid=(NSA_KV_HEADS, ni),
        in_specs=[pl.BlockSpec(memory_space=pltpu.SMEM),
                  pl.BlockSpec((batch, tq, NSA_GROUP * NSA_HEAD_DIM), lambda h, i: (0, i, qcol + h)),
                  pl.BlockSpec((batch, 1, nch, NSA_HEAD_DIM), lambda h, i: (0, h, 0, 0)),
                  pl.BlockSpec((batch, 1, nch, NSA_HEAD_DIM), lambda h, i: (0, NSA_KV_HEADS + h, 0, 0)),
                  pl.BlockSpec((1, CMP_STRIDE, strip_w), lambda h, i: (i, 0, 0)),
                  pl.BlockSpec((n_slc, nch), lambda h, i: (0, 0))],
        out_specs=[pl.BlockSpec((batch, tq, NSA_GROUP * NSA_HEAD_DIM), lambda h, i: (0, i, h)),
                   pl.BlockSpec((batch, 1, n_slc, tq), lambda h, i: (0, h, 0, i))],
        out_shape=[jax.ShapeDtypeStruct((batch, seq, D_NSA), F32),
                   jax.ShapeDtypeStruct((batch, NSA_KV_HEADS, n_slc, seq), F32)],
        scratch_shapes=[pltpu.VMEM((NSA_GROUP, tq, nch), F32), pltpu.VMEM((NSA_GROUP, CMP_STRIDE, strip_w), F32)],
        compiler_params=_cparams(("arbitrary", "arbitrary")),
        name="cmp_attn",
    )(rel_bias.astype(F32), u.reshape(batch, seq, u.shape[1]), kvc, kvc, buckets, ovt)
    return o_c.reshape(batch * seq, D_NSA), sel_t


TQ = 128
FLASH_GROUP = 8


def _flash_kernel(tab_ref, q_ref, kv_ref, bk_ref, *rest, selected, n_delta):
    if selected:
        sel_ref, o_ref, band_scr, qs_scr, m_scr, l_scr, acc_scr = rest
    else:
        o_ref, band_scr, qs_scr, m_scr, l_scr, acc_scr = rest
    h = pl.program_id(0)
    i = pl.program_id(1)
    n_batch = q_ref.shape[0]

    @pl.when(i == 0)
    def _():
        def fill(d, carry):
            for g, bias in enumerate(_bias_from_buckets(bk_ref[d], tab_ref, h * NSA_GROUP)):
                band_scr[d, :, g * TQ:(g + 1) * TQ] = bias
            return carry
        lax.fori_loop(0, n_delta, fill, 0)

    for bb in range(n_batch):
        for g in range(NSA_GROUP):
            qg = (q_ref[bb, :, g * NSA_HEAD_DIM:(g + 1) * NSA_HEAD_DIM] * (NSA_HEAD_DIM ** -0.5)).astype(BF16)
            qs_scr[bb, g * TQ:(g + 1) * TQ, :] = jnp.concatenate([qg, jnp.zeros_like(qg)], axis=1)
    is_key_lane = lax.broadcasted_iota(jnp.int32, (TQ, LANE), 1) < NSA_HEAD_DIM
    m_scr[...] = jnp.full(m_scr.shape, 0.5 * NEG_BIG, F32)
    l_scr[...] = jnp.zeros(l_scr.shape, F32)
    acc_scr[...] = jnp.zeros(acc_scr.shape, F32)
    key = lax.broadcasted_iota(jnp.int32, (TQ, TQ), 0)
    qry = lax.broadcasted_iota(jnp.int32, (TQ, TQ), 1)
    n_back = WINDOW // TQ

    def scores(bb, j, kind):
        kvj = kv_ref[bb, pl.ds(pl.multiple_of(j * TQ, TQ), TQ), :].astype(BF16)
        s = _dot_nt(kvj, qs_scr[bb]) + band_scr[i - j]
        mask = None
        if kind == "diag":
            mask = qry >= key
        elif kind == "far":
            mask = qry < key
        if selected:
            r = sel_ref[bb, 0, pl.ds(2 * j, 2), :]
            picked = jnp.where(key < SLC_BLOCK, r[0:1, :], r[1:2, :]) > 0.5
            mask = picked if mask is None else (mask & picked)
        if mask is not None:
            s = jnp.concatenate([jnp.where(mask, s[:, g * TQ:(g + 1) * TQ], NEG_BIG) for g in range(NSA_GROUP)], axis=1)
        return s, jnp.where(is_key_lane, jnp.ones_like(kvj), kvj)

    def update(tile_ids):
        all_tiles = [[scores(bb, j, kind) for j, kind in tile_ids] for bb in range(n_batch)]
        for bb, tiles in enumerate(all_tiles):
            m_old = m_scr[bb]
            m_new = m_old
            for s, _ in tiles:
                m_new = jnp.maximum(m_new, jnp.max(s, axis=0, keepdims=True))
            alpha = jnp.exp(m_old - m_new)
            l_new = alpha * l_scr[bb]
            acc = alpha * acc_scr[bb]
            for s, ones_v in tiles:
                p = jnp.exp(s - m_new)
                pv = _dot_tn(ones_v, p.astype(BF16))
                l_new = l_new + pv[0:1, :]
                acc = acc + pv[NSA_HEAD_DIM:2 * NSA_HEAD_DIM, :]
            m_scr[bb] = m_new
            l_scr[bb] = l_new
            acc_scr[bb] = acc

    def full_tiles(first, count):
        return [(first + k, "full") for k in range(count)]

    if selected:
        def group(t, carry):
            update(full_tiles(FLASH_GROUP * t, FLASH_GROUP))
            return carry

        n_groups = i // FLASH_GROUP
        lax.fori_loop(0, n_groups, group, 0)
        for rem in range(FLASH_GROUP):
            @pl.when(i - n_groups * FLASH_GROUP == rem)
            def _():
                update(full_tiles(n_groups * FLASH_GROUP, rem) + [(i, "diag")])
    else:
        @pl.when(i >= n_back)
        def _():
            update([(i - n_back, "far")] + full_tiles(i - n_back + 1, n_back - 1) + [(i, "diag")])

        for rem in range(n_back):
            @pl.when(i == rem)
            def _():
                update(full_tiles(0, rem) + [(i, "diag")])

    for bb in range(n_batch):
        for g in range(NSA_GROUP):
            cols = slice(g * TQ, (g + 1) * TQ)
            o_ref[bb, :, g * NSA_HEAD_DIM:(g + 1) * NSA_HEAD_DIM] = (acc_scr[bb, :, cols] / l_scr[bb, :, cols]).T


def _flash(u, kv_col, rel_bias, sel_t, batch, seq):
    ni = seq // TQ
    qcol = COL_QNSA // (NSA_GROUP * NSA_HEAD_DIM)
    selected = sel_t is not None
    n_delta = ni if selected else WINDOW // TQ + 1
    kvblk = kv_col // LANE
    delta = np.arange(n_delta)[:, None, None] * TQ + np.arange(TQ)[None, None, :] - np.arange(TQ)[None, :, None]
    buckets = jnp.asarray(_bucket_np(delta))
    u3 = u.reshape(batch, seq, u.shape[1])
    in_specs = [pl.BlockSpec(memory_space=pltpu.SMEM),
                pl.BlockSpec((batch, TQ, NSA_GROUP * NSA_HEAD_DIM), lambda h, i: (0, i, qcol + h)),
                pl.BlockSpec((batch, seq, LANE), lambda h, i: (0, 0, kvblk + h)),
                pl.BlockSpec((n_delta, TQ, TQ), lambda h, i: (0, 0, 0))]
    args = [rel_bias.astype(F32), u3, u3, buckets]
    if selected:
        n_slc = sel_t.shape[2]
        in_specs.append(pl.BlockSpec((batch, 1, n_slc, TQ), lambda h, i: (0, h, 0, i)))
        args.append(sel_t)
    scratch = [pltpu.VMEM((n_delta, TQ, NSA_GROUP * TQ), F32),
               pltpu.VMEM((batch, NSA_GROUP * TQ, LANE), BF16),
               pltpu.VMEM((batch, 1, NSA_GROUP * TQ), F32),
               pltpu.VMEM((batch, 1, NSA_GROUP * TQ), F32),
               pltpu.VMEM((batch, NSA_HEAD_DIM, NSA_GROUP * TQ), F32)]
    out = pl.pallas_call(
        functools.partial(_flash_kernel, selected=selected, n_delta=n_delta),
        grid=(NSA_KV_HEADS, ni),
        in_specs=in_specs,
        out_specs=pl.BlockSpec((batch, TQ, NSA_GROUP * NSA_HEAD_DIM), lambda h, i: (0, i, h)),
        out_shape=jax.ShapeDtypeStruct((batch, seq, D_NSA), F32),
        scratch_shapes=scratch,
        compiler_params=_cparams(("arbitrary", "arbitrary")),
        name="slc_attn" if selected else "win_attn",
    )(*args)
    return out.reshape(batch * seq, D_NSA)


def _gate_expand():
    e = np.zeros((3, 2 * LANE, D_NSA), np.float32)
    for br in range(3):
        for hd in range(NSA_HEADS):
            for part in range(2):
                e[br, part * LANE + 2 * ML_HEADS + br * NSA_HEADS + hd,
                  hd * NSA_HEAD_DIM:(hd + 1) * NSA_HEAD_DIM] = 1.0
    return jnp.asarray(e, BF16)


def _outproj_kernel(yml_ref, oc_ref, os_ref, ow_ref, s_ref, e_ref, ng_ref, w_ref, x_ref, o_ref, cat_scr):
    @pl.when(pl.program_id(1) == 0)
    def _():
        sig = jax.nn.sigmoid(s_ref[...])
        hi_lo = jnp.concatenate(_split2(sig), axis=1)
        o = jnp.zeros(oc_ref.shape, F32)
        for br, ref in enumerate((oc_ref, os_ref, ow_ref)):
            o = o + _dot(hi_lo, e_ref[br]) * ref[...]
        o = o * lax.rsqrt(jnp.mean(o * o, axis=-1, keepdims=True) + EPS) * ng_ref[...]
        cat_scr[:, 0:D_ML] = yml_ref[...]
        cat_scr[:, D_ML:D_ML + D_NSA] = o.astype(BF16)

    o_ref[...] = x_ref[...] + _dot(cat_scr[...], w_ref[...])


def _outproj(y_ml, o_c, o_s, o_w, small, nsa_g, w_out_b, x2d, tm, tn):
    n, d = x2d.shape
    rows = lambda i, j: (i, 0)
    return pl.pallas_call(
        _outproj_kernel,
        grid=(n // tm, d // tn),
        in_specs=[pl.BlockSpec((tm, D_ML), rows), pl.BlockSpec((tm, D_NSA), rows),
                  pl.BlockSpec((tm, D_NSA), rows), pl.BlockSpec((tm, D_NSA), rows),
                  pl.BlockSpec((tm, LANE), rows),
                  pl.BlockSpec((3, 2 * LANE, D_NSA), lambda i, j: (0, 0, 0)),
                  pl.BlockSpec((1, D_NSA), lambda i, j: (0, 0)),
                  pl.BlockSpec((D_ML + D_NSA, tn), lambda i, j: (0, j)),
                  pl.BlockSpec((tm, tn), lambda i, j: (i, j))],
        out_specs=pl.BlockSpec((tm, tn), lambda i, j: (i, j)),
        out_shape=jax.ShapeDtypeStruct((n, d), F32),
        scratch_shapes=[pltpu.VMEM((tm, D_ML + D_NSA), BF16)],
        compiler_params=_cparams(("arbitrary", "arbitrary")),
        name="outproj",
    )(y_ml, o_c, o_s, o_w, small, _gate_expand(), nsa_g, w_out_b, x2d)


ROUTE_COL = N_GROUPS
BIG_COL = 1 << 20
SPLIT = D_MODEL // LANE
DMA_UNROLL = 8


def _store_split(ref, val, n):
    for k in range(SPLIT):
        ref[pl.ds(k, n, stride=SPLIT), :] = val[:, k * LANE:(k + 1) * LANE]


def _load_split(ref, n):
    return jnp.concatenate([ref[pl.ds(k, n, stride=SPLIT), :] for k in range(SPLIT)], axis=1)


def _route_kernel(y_ref, g_ref, w_ref, b_ref, h_ref, info_ref, cnt_ref, carry_scr, *, tm):
    @pl.when(pl.program_id(0) == 0)
    def _():
        carry_scr[...] = jnp.zeros(carry_scr.shape, F32)

    x = y_ref[...]
    h = x * lax.rsqrt(jnp.mean(x * x, axis=-1, keepdims=True) + EPS) * g_ref[...]
    _store_split(h_ref, h, tm)
    logit = _dot(h.astype(BF16), w_ref[...]) + b_ref[...]
    col = lax.broadcasted_iota(jnp.int32, logit.shape, 1)
    is_grp = col < N_GROUPS
    gmax = jnp.max(jnp.where(is_grp, logit, -jnp.inf), axis=1, keepdims=True)
    gtop = jnp.min(jnp.where(is_grp & (logit == gmax), col, BIG_COL), axis=1, keepdims=True)
    gsum = jnp.sum(jnp.where(is_grp, jnp.exp(logit - gmax), 0.0), axis=1, keepdims=True)
    first = ROUTE_COL + gtop * EXPERTS_PER_GROUP
    in_grp = (col >= first) & (col < first + EXPERTS_PER_GROUP)
    v1 = jnp.max(jnp.where(in_grp, logit, -jnp.inf), axis=1, keepdims=True)
    i1 = jnp.min(jnp.where(in_grp & (logit == v1), col, BIG_COL), axis=1, keepdims=True)
    rest = in_grp & (col != i1)
    v2 = jnp.max(jnp.where(rest, logit, -jnp.inf), axis=1, keepdims=True)
    i2 = jnp.min(jnp.where(rest & (logit == v2), col, BIG_COL), axis=1, keepdims=True)
    e = jnp.exp(v2 - v1)
    w1 = 1.0 / ((1.0 + e) * gsum)
    w2 = e / ((1.0 + e) * gsum)
    pick1 = col == i1
    pick2 = col == i2
    both = (pick1 | pick2).astype(F32)
    ri = lax.broadcasted_iota(jnp.int32, (tm, tm), 0)
    ci = lax.broadcasted_iota(jnp.int32, (tm, tm), 1)
    before = (ci < ri).astype(BF16)
    cum = _dot(before, both.astype(BF16)) + carry_scr[...]
    r1 = jnp.sum(jnp.where(pick1, cum, 0.0), axis=1, keepdims=True)
    r2 = jnp.sum(jnp.where(pick2, cum, 0.0), axis=1, keepdims=True)
    carry_scr[...] = carry_scr[...] + jnp.sum(both, axis=0, keepdims=True)
    cnt_ref[...] = jnp.broadcast_to(carry_scr[...], cnt_ref.shape)
    info = jnp.where(col == 0, (i1 - ROUTE_COL).astype(F32), 0.0)
    info = jnp.where(col == 1, (i2 - ROUTE_COL).astype(F32), info)
    info = jnp.where(col == 2, w1, info)
    info = jnp.where(col == 3, w2, info)
    info = jnp.where(col == 4, r1, info)
    info_ref[...] = jnp.where(col == 5, r2, info)


def _route(y2d, g, w_rg, b_rg, w_re, b_re, tm):
    n, d = y2d.shape
    wr = jnp.concatenate([w_rg, w_re, jnp.zeros((d, LANE - N_GROUPS - N_EXPERTS), F32)], axis=1).astype(BF16)
    bias = jnp.concatenate([b_rg, b_re, jnp.zeros((LANE - N_GROUPS - N_EXPERTS,), F32)])[None, :]
    rows = lambda i: (i, 0)
    fixed = lambda i: (0, 0)
    return pl.pallas_call(
        functools.partial(_route_kernel, tm=tm),
        grid=(n // tm,),
        in_specs=[pl.BlockSpec((tm, d), rows), pl.BlockSpec((1, d), fixed),
                  pl.BlockSpec((d, LANE), fixed), pl.BlockSpec((1, LANE), fixed)],
        out_specs=[pl.BlockSpec((tm * SPLIT, LANE), rows), pl.BlockSpec((tm, LANE), rows),
                   pl.BlockSpec((8, LANE), fixed)],
        out_shape=[jax.ShapeDtypeStruct((n * SPLIT, LANE), F32), jax.ShapeDtypeStruct((n, LANE), F32),
                   jax.ShapeDtypeStruct((8, LANE), F32)],
        scratch_shapes=[pltpu.VMEM((1, LANE), F32)],
        compiler_params=_cparams(("arbitrary",)),
        name="moe_route",
    )(y2d, g, wr, bias)


W_CHUNKS = 4


def _expert_kernel(te_ref, nu_ref, src_ref, nx_ref, ws_ref, h_hbm, wg_hbm, wu_hbm, wd_hbm, o_ref,
                   xbuf, wg_f, wu_f, wd_f, wg_b, wu_b, wd_b, sem, wsem, *, tme):
    i = pl.program_id(0)
    slot = i % 2

    def weight_copies(expert, wslot):
        copies = []
        for hbm, buf in ((wg_hbm, wg_f), (wu_hbm, wu_f), (wd_hbm, wd_f)):
            step = hbm.shape[1] // W_CHUNKS
            for c in range(W_CHUNKS):
                copies.append(pltpu.make_async_copy(hbm.at[expert, pl.ds(c * step, step)],
                                                    buf.at[wslot, pl.ds(c * step, step)], wsem.at[wslot]))
        return copies

    def row_copy(tile, r, buf_slot):
        src_row = src_ref[tile * tme + r]
        return pltpu.make_async_copy(h_hbm.at[pl.ds(pl.multiple_of(src_row * SPLIT, SPLIT), SPLIT)],
                                     xbuf.at[buf_slot, pl.ds(pl.multiple_of(r * SPLIT, SPLIT), SPLIT)],
                                     sem.at[buf_slot])

    def fetch(tile, buf_slot):
        def body(r, carry):
            row_copy(tile, r, buf_slot).start()
            return carry
        lax.fori_loop(0, tme, body, 0, unroll=DMA_UNROLL)

    @pl.when(i == 0)
    def _():
        fetch(0, 0)
        for cp in weight_copies(te_ref[0], ws_ref[0]):
            cp.start()

    @pl.when(i + 1 < nu_ref[0])
    def _():
        fetch(i + 1, 1 - slot)

    @pl.when(i < nu_ref[0])
    def _():
        prev = te_ref[jnp.maximum(i - 1, 0)]

        @pl.when((i == 0) | (te_ref[i] != prev))
        def _():
            w = ws_ref[i]
            for cp in weight_copies(te_ref[i], w):
                cp.wait()
            wg_b[...] = wg_f[w].astype(BF16)
            wu_b[...] = wu_f[w].astype(BF16)
            wd_b[...] = wd_f[w].astype(BF16)

            @pl.when(nx_ref[i] >= 0)
            def _():
                for cp in weight_copies(nx_ref[i], 1 - w):
                    cp.start()

        pltpu.make_async_copy(h_hbm.at[pl.ds(0, tme * SPLIT)], xbuf.at[slot], sem.at[slot]).wait()

        x = _load_split(xbuf.at[slot], tme).astype(BF16)
        a = _dot(x, wg_b[...])
        u = _dot(x, wu_b[...])
        hid = a * jax.nn.sigmoid(a) * u
        _store_split(o_ref, _dot(hid.astype(BF16), wd_b[...]), tme)

    @pl.when(i >= nu_ref[0])
    def _():
        o_ref[...] = jnp.zeros(o_ref.shape, F32)


def _experts(h, src, tile_expert, n_used, next_expert, weight_slot, w_gate, w_up, w_down, tme):
    p = src.shape[0]
    d = D_MODEL
    hbm = pl.BlockSpec(memory_space=pl.ANY)
    return pl.pallas_call(
        functools.partial(_expert_kernel, tme=tme),
        grid_spec=pltpu.PrefetchScalarGridSpec(
            num_scalar_prefetch=5, grid=(p // tme,),
            in_specs=[hbm, hbm, hbm, hbm],
            out_specs=pl.BlockSpec((tme * SPLIT, LANE), lambda i, *_: (i, 0)),
            scratch_shapes=[pltpu.VMEM((2, tme * SPLIT, LANE), F32),
                            pltpu.VMEM((2, d, D_EXPERT), F32), pltpu.VMEM((2, d, D_EXPERT), F32),
                            pltpu.VMEM((2, D_EXPERT, d), F32),
                            pltpu.VMEM((d, D_EXPERT), BF16), pltpu.VMEM((d, D_EXPERT), BF16),
                            pltpu.VMEM((D_EXPERT, d), BF16),
                            pltpu.SemaphoreType.DMA((2,)), pltpu.SemaphoreType.DMA((2,))]),
        out_shape=jax.ShapeDtypeStruct((p * SPLIT, LANE), F32),
        compiler_params=_cparams(("arbitrary",)),
        name="moe_experts",
    )(tile_expert, n_used, src, next_expert, weight_slot, h, w_gate, w_up, w_down)


def _combine_kernel(d1_ref, d2_ref, y_ref, info_ref, ys_hbm, g_ref, o_ref, a_buf, b_buf, sem, *, tm):
    i = pl.program_id(0)
    slot = i % 2

    def copy(r, row, buf, buf_slot):
        return pltpu.make_async_copy(ys_hbm.at[pl.ds(pl.multiple_of(row * SPLIT, SPLIT), SPLIT)],
                                     buf.at[buf_slot, pl.ds(pl.multiple_of(r * SPLIT, SPLIT), SPLIT)],
                                     sem.at[buf_slot])

    def fetch(tile, buf_slot):
        def issue(r, carry):
            copy(r, d1_ref[tile * tm + r], a_buf, buf_slot).start()
            copy(r, d2_ref[tile * tm + r], b_buf, buf_slot).start()
            return carry
        lax.fori_loop(0, tm, issue, 0, unroll=DMA_UNROLL)

    @pl.when(i == 0)
    def _():
        fetch(0, 0)

    @pl.when(i + 1 < pl.num_programs(0))
    def _():
        fetch(i + 1, 1 - slot)

    for buf in (a_buf, b_buf):
        pltpu.make_async_copy(ys_hbm.at[pl.ds(0, tm * SPLIT)], buf.at[slot], sem.at[slot]).wait()
    w1 = info_ref[:, 2:3]
    w2 = info_ref[:, 3:4]
    y = y_ref[...] + w1 * _load_split(a_buf.at[slot], tm) + w2 * _load_split(b_buf.at[slot], tm)
    o_ref[...] = y * lax.rsqrt(jnp.mean(y * y, axis=-1, keepdims=True) + EPS) * g_ref[...]


def _combine(y2d, info, ys, d1, d2, g, tm):
    n, d = y2d.shape
    return pl.pallas_call(
        functools.partial(_combine_kernel, tm=tm),
        grid_spec=pltpu.PrefetchScalarGridSpec(
            num_scalar_prefetch=2, grid=(n // tm,),
            in_specs=[pl.BlockSpec((tm, d), lambda i, a, b: (i, 0)),
                      pl.BlockSpec((tm, LANE), lambda i, a, b: (i, 0)),
                      pl.BlockSpec(memory_space=pl.ANY),
                      pl.BlockSpec((1, d), lambda i, a, b: (0, 0))],
            out_specs=pl.BlockSpec((tm, d), lambda i, a, b: (i, 0)),
            scratch_shapes=[pltpu.VMEM((2, tm * SPLIT, LANE), F32), pltpu.VMEM((2, tm * SPLIT, LANE), F32),
                            pltpu.SemaphoreType.DMA((2,))]),
        out_shape=jax.ShapeDtypeStruct((n, d), F32),
        compiler_params=_cparams(("arbitrary",)),
        name="moe_combine",
    )(d1, d2, y2d, info, ys, g)


def _moe_final(y2d, ffn_g, w_rg, b_rg, w_re, b_re, w_gate, w_up, w_down, final_g, tm, tme):
    n, d = y2d.shape
    h, info, cnt = _route(y2d, ffn_g, w_rg, b_rg, w_re, b_re, tm)
    e1 = info[:, 0].astype(jnp.int32)
    e2 = info[:, 1].astype(jnp.int32)
    counts = cnt[0, ROUTE_COL:ROUTE_COL + N_EXPERTS].astype(jnp.int32)
    padded = (counts + tme - 1) // tme * tme
    ends = jnp.cumsum(padded)
    offs = ends - padded
    d1 = offs[e1] + info[:, 4].astype(jnp.int32)
    d2 = offs[e2] + info[:, 5].astype(jnp.int32)
    p = 2 * n + N_EXPERTS * tme
    tok = jnp.arange(n, dtype=jnp.int32)
    src = jnp.zeros((p,), jnp.int32).at[jnp.concatenate([d1, d2])].set(jnp.concatenate([tok, tok]))
    n_tiles = p // tme
    n_used = (ends[-1] // tme).astype(jnp.int32).reshape(1)
    tile_start = jnp.arange(n_tiles, dtype=jnp.int32) * tme
    tile_expert = jnp.sum((ends[None, :] <= tile_start[:, None]).astype(jnp.int32), axis=1)
    last = jnp.sum((ends <= ends[-1] - 1).astype(jnp.int32))
    tile_expert = jnp.minimum(tile_expert, last).astype(jnp.int32)
    run = jnp.cumsum(jnp.concatenate([jnp.zeros((1,), jnp.int32),
                                      (tile_expert[1:] != tile_expert[:-1]).astype(jnp.int32)]))
    weight_slot = (run % 2).astype(jnp.int32)
    after = ends[tile_expert] // tme
    next_expert = jnp.where(after < n_used[0], tile_expert[jnp.minimum(after, n_tiles - 1)], -1).astype(jnp.int32)
    ys = _experts(h, src, tile_expert, n_used, next_expert, weight_slot, w_gate, w_up, w_down, tme)
    return _combine(y2d, info, ys, d1, d2, final_g, tm)


GROUP_ROWS = 8
SLC_LANES = 384


def _sample_cattn_kernel(q_ref, kc_ref, vc_ref, b_ref, ov_ref, o_ref, sel_ref, *, n_cmp, n_slc):
    q = (q_ref[0, 0] * (NSA_HEAD_DIM ** -0.5)).astype(BF16)
    s = _dot_nt(q, kc_ref[0, 0].astype(BF16)) + b_ref[0]
    valid = lax.broadcasted_iota(jnp.int32, s.shape, 1) < n_cmp
    s = jnp.where(valid, s, NEG_BIG)
    e = jnp.exp(s - jnp.max(s, axis=1, keepdims=True))
    p = jnp.where(valid, e / jnp.sum(e, axis=1, keepdims=True), 0.0)
    o_ref[0, 0] = _dot(p.astype(BF16), vc_ref[0, 0].astype(BF16))
    pg = jnp.where(lax.broadcasted_iota(jnp.int32, p.shape, 0) < NSA_GROUP, p, 0.0)
    pc = jnp.broadcast_to(jnp.sum(pg, axis=0, keepdims=True), pg.shape)
    score = _dot(pc.astype(BF16), ov_ref[...])[0:1, :]
    blk = lax.broadcasted_iota(jnp.int32, score.shape, 1)
    cur = n_slc - 1
    forced = (blk == 0) | (blk == cur) | (blk == cur - 1)
    score = jnp.where(forced, FORCE_SCORE, score)
    score = jnp.where(blk > cur, -jnp.inf, score)
    ri = lax.broadcasted_iota(jnp.int32, (SLC_LANES, SLC_LANES), 0)
    ci = lax.broadcasted_iota(jnp.int32, (SLC_LANES, SLC_LANES), 1)
    sb = jnp.broadcast_to(score, (SLC_LANES, SLC_LANES))
    col = jnp.sum(jnp.where(ri == ci, sb, 0.0), axis=1, keepdims=True)
    beats = (col > sb) | ((col == sb) & (ri < ci))
    rank = jnp.sum(beats.astype(jnp.int32), axis=0, keepdims=True)
    sel_ref[0, 0] = jnp.broadcast_to((rank < SLC_TOPN).astype(F32), (GROUP_ROWS, SLC_LANES))


def _sample_cattn(q8, kvc, bias_cs, n_cmp, n_slc):
    bsz, _, nch, _ = kvc.shape
    c0 = np.arange(nch) * CMP_STRIDE
    s0 = np.arange(SLC_LANES) * SLC_BLOCK
    ov = np.minimum(c0[:, None] + CMP_BLOCK, s0[None, :] + SLC_BLOCK) - np.maximum(c0[:, None], s0[None, :])
    ov = np.clip(ov, 0, None).astype(np.float32) / CMP_BLOCK
    ov[n_cmp:, :] = 0.0
    ov[:, n_slc:] = 0.0
    return pl.pallas_call(
        functools.partial(_sample_cattn_kernel, n_cmp=n_cmp, n_slc=n_slc),
        grid=(bsz, NSA_KV_HEADS),
        in_specs=[pl.BlockSpec((1, 1, GROUP_ROWS, NSA_HEAD_DIM), lambda b, h: (b, h, 0, 0)),
                  pl.BlockSpec((1, 1, nch, NSA_HEAD_DIM), lambda b, h: (b, h, 0, 0)),
                  pl.BlockSpec((1, 1, nch, NSA_HEAD_DIM), lambda b, h: (b, NSA_KV_HEADS + h, 0, 0)),
                  pl.BlockSpec((1, GROUP_ROWS, nch), lambda b, h: (h, 0, 0)),
                  pl.BlockSpec((nch, SLC_LANES), lambda b, h: (0, 0))],
        out_specs=[pl.BlockSpec((1, 1, GROUP_ROWS, NSA_HEAD_DIM), lambda b, h: (b, h, 0, 0)),
                   pl.BlockSpec((1, 1, GROUP_ROWS, SLC_LANES), lambda b, h: (b, h, 0, 0))],
        out_shape=[jax.ShapeDtypeStruct((bsz, NSA_KV_HEADS, GROUP_ROWS, NSA_HEAD_DIM), F32),
                   jax.ShapeDtypeStruct((bsz, NSA_KV_HEADS, GROUP_ROWS, SLC_LANES), F32)],
        compiler_params=_cparams(("arbitrary", "arbitrary")),
        name="sample_cmp_attn",
    )(q8, kvc, kvc, bias_cs, jnp.asarray(ov, BF16))


def _sample_attn_kernel(pg_ref, hf_ref, q_ref, cache_hbm, snew_ref, bs_ref, win_ref, wnew_ref, bw_ref,
                        os_ref, ow_ref, kt_buf, vt_buf, sem, *, n_gather, wlen):
    b = pl.program_id(0)
    rows_per_page = 2 * NSA_KV_HEADS * NSA_HEAD_DIM

    def page_copy(h, s, c, buf):
        page = pg_ref[(b * NSA_KV_HEADS + h) * n_gather + s]
        start = pl.multiple_of(page * rows_per_page + (c * NSA_KV_HEADS + h) * NSA_HEAD_DIM, NSA_HEAD_DIM)
        return pltpu.make_async_copy(cache_hbm.at[pl.ds(start, NSA_HEAD_DIM)],
                                     buf.at[h, :, pl.ds(s * PAGE_SIZE, PAGE_SIZE)], sem)

    for h in range(NSA_KV_HEADS):
        for s in range(n_gather):
            page_copy(h, s, 0, kt_buf).start()
            page_copy(h, s, 1, vt_buf).start()

    def attend(q, kts, vts, bias, valid):
        s = jnp.concatenate([_dot(q, kt.astype(BF16)) for kt in kts], axis=1) + bias
        s = jnp.where(valid, s, NEG_BIG)
        e = jnp.exp(s - jnp.max(s, axis=1, keepdims=True))
        p = jnp.where(valid, e / jnp.sum(e, axis=1, keepdims=True), 0.0).astype(BF16)
        out = None
        lo = 0
        for vt in vts:
            n = vt.shape[1]
            part = _dot_nt(p[:, lo:lo + n], vt.astype(BF16))
            out = part if out is None else out + part
            lo += n
        return out

    def head_rows(ref, c, h):
        r0 = (c * NSA_KV_HEADS + h) * NSA_HEAD_DIM
        return ref[0, r0:r0 + NSA_HEAD_DIM, :]

    widx = lax.broadcasted_iota(jnp.int32, (GROUP_ROWS, wlen + PAGE_SIZE), 1)
    wvalid = (widx >= wlen + 1 - WINDOW) & (widx <= wlen)
    for h in range(NSA_KV_HEADS):
        q = (q_ref[0, h] * (NSA_HEAD_DIM ** -0.5)).astype(BF16)
        ow_ref[0, h] = attend(q, [head_rows(win_ref, 0, h), head_rows(wnew_ref, 0, h)],
                              [head_rows(win_ref, 1, h), head_rows(wnew_ref, 1, h)], bw_ref[h], wvalid)

    for h in range(NSA_KV_HEADS):
        for s in range(n_gather):
            page_copy(h, s, 0, kt_buf).wait()
            page_copy(h, s, 1, vt_buf).wait()

    keys = n_gather * PAGE_SIZE
    lane = lax.broadcasted_iota(jnp.int32, (GROUP_ROWS, keys + PAGE_SIZE), 1)
    slot = lane >> 7
    lane_half = (lane >> 6) & 1
    for h in range(NSA_KV_HEADS):
        want = jnp.full(lane.shape, -1, jnp.int32)
        for s in range(n_gather):
            want = jnp.where(slot == s, hf_ref[(b * NSA_KV_HEADS + h) * n_gather + s], want)
        svalid = (lane_half == want) | (lane == keys)
        q = (q_ref[0, h] * (NSA_HEAD_DIM ** -0.5)).astype(BF16)
        os_ref[0, h] = attend(q, [kt_buf[h], head_rows(snew_ref, 0, h)],
                              [vt_buf[h], head_rows(snew_ref, 1, h)], bs_ref[0, h], svalid)


def _sample_attn(pages, halves, q8, cache_t, snew_t, bias_sel, win_t, wnew_t, bias_w, n_gather):
    bsz = q8.shape[0]
    wlen = win_t.shape[2]
    keys = n_gather * PAGE_SIZE
    rows = 2 * NSA_KV_HEADS * NSA_HEAD_DIM
    qspec = pl.BlockSpec((1, NSA_KV_HEADS, GROUP_ROWS, NSA_HEAD_DIM), lambda b, pg, hf: (b, 0, 0, 0))
    newspec = pl.BlockSpec((1, rows, PAGE_SIZE), lambda b, pg, hf: (b, 0, 0))
    return pl.pallas_call(
        functools.partial(_sample_attn_kernel, n_gather=n_gather, wlen=wlen),
        grid_spec=pltpu.PrefetchScalarGridSpec(
            num_scalar_prefetch=2, grid=(bsz,),
            in_specs=[qspec,
                      pl.BlockSpec(memory_space=pl.ANY),
                      newspec,
                      pl.BlockSpec((1, NSA_KV_HEADS, GROUP_ROWS, keys + PAGE_SIZE), lambda b, pg, hf: (b, 0, 0, 0)),
                      pl.BlockSpec((1, rows, wlen), lambda b, pg, hf: (b, 0, 0)),
                      newspec,
                      pl.BlockSpec((NSA_KV_HEADS, GROUP_ROWS, wlen + PAGE_SIZE), lambda b, pg, hf: (0, 0, 0))],
            out_specs=[qspec, qspec],
            scratch_shapes=[pltpu.VMEM((NSA_KV_HEADS, NSA_HEAD_DIM, keys), F32),
                            pltpu.VMEM((NSA_KV_HEADS, NSA_HEAD_DIM, keys), F32),
                            pltpu.SemaphoreType.DMA(())]),
        out_shape=[jax.ShapeDtypeStruct(q8.shape, F32), jax.ShapeDtypeStruct(q8.shape, F32)],
        compiler_params=_cparams(("arbitrary",)),
        name="sample_slc_win_attn",
    )(pages, halves, q8, cache_t, snew_t, bias_sel, win_t, wnew_t, bias_w)


PAGES_PER_STEP = 16


def _paged_pq_kernel(pt_ref, cache_hbm, pos_ref, w_ref, o_ref, pbuf, tok_scr, sem):
    b = pl.program_id(0)
    g = pl.program_id(1)
    ng = pl.num_programs(1)
    lin = b * ng + g
    slot = lin % 2
    rows_per_page = 2 * NSA_KV_HEADS * NSA_HEAD_DIM
    chunks = PAGES_PER_STEP * (PAGE_SIZE // CMP_STRIDE)

    def page_copy(step, p, buf_slot):
        sb = step // ng
        sg = step - sb * ng
        page = pt_ref[sb, sg * PAGES_PER_STEP + p]
        return pltpu.make_async_copy(cache_hbm.at[pl.ds(pl.multiple_of(page * rows_per_page, rows_per_page),
                                                        rows_per_page)],
                                     pbuf.at[buf_slot, p], sem.at[buf_slot])

    def fetch(step, buf_slot):
        for p in range(PAGES_PER_STEP):
            page_copy(step, p, buf_slot).start()

    @pl.when(lin == 0)
    def _():
        fetch(0, 0)

    @pl.when(lin + 1 < pl.num_programs(0) * ng)
    def _():
        fetch(lin + 1, 1 - slot)

    for p in range(PAGES_PER_STEP):
        page_copy(lin, p, slot).wait()

    for c in range(2):
        head_chunks = []
        for pair in (2 * c, 2 * c + 1):
            for p in range(PAGES_PER_STEP):
                tok_scr[p * PAGE_SIZE:(p + 1) * PAGE_SIZE, :] = pbuf[slot, p, pair * LANE:(pair + 1) * LANE, :].T
            toks = [tok_scr[pl.ds(l, chunks, stride=CMP_STRIDE), :] for l in range(CMP_STRIDE)]
            for half in range(2):
                lanes = slice(half * NSA_HEAD_DIM, (half + 1) * NSA_HEAD_DIM)
                head_chunks.append(jnp.concatenate([t[:, lanes] for t in toks], axis=1))
        stacked = jnp.concatenate(head_chunks, axis=0)
        for part in range(2):
            lhs = (stacked + pos_ref[c, part:part + 1, :]).astype(BF16)
            out = _dot(lhs, w_ref[c, part])
            for hh in range(NSA_KV_HEADS):
                o_ref[0, c * NSA_KV_HEADS + hh, :, part * CMP_HIDDEN:(part + 1) * CMP_HIDDEN] = (
                    out[hh * chunks:(hh + 1) * chunks, :])


def _paged_pq(cache_t, page_table, cmp_pos, cmp_w1):
    db, n_pages = page_table.shape
    kdim = CMP_STRIDE * NSA_HEAD_DIM
    nch = n_pages * (PAGE_SIZE // CMP_STRIDE)
    tn = PAGES_PER_STEP * (PAGE_SIZE // CMP_STRIDE)
    rows_per_page = 2 * NSA_KV_HEADS * NSA_HEAD_DIM
    return pl.pallas_call(
        _paged_pq_kernel,
        grid_spec=pltpu.PrefetchScalarGridSpec(
            num_scalar_prefetch=1, grid=(db, n_pages // PAGES_PER_STEP),
            in_specs=[pl.BlockSpec(memory_space=pl.ANY),
                      pl.BlockSpec((2, 2, kdim), lambda b, g, pt: (0, 0, 0)),
                      pl.BlockSpec((2, 2, kdim, CMP_HIDDEN), lambda b, g, pt: (0, 0, 0, 0))],
            out_specs=pl.BlockSpec((1, 8, tn, 2 * CMP_HIDDEN), lambda b, g, pt: (b, 0, g, 0)),
            scratch_shapes=[pltpu.VMEM((2, PAGES_PER_STEP, rows_per_page, PAGE_SIZE), F32),
                            pltpu.VMEM((PAGES_PER_STEP * PAGE_SIZE, LANE), F32),
                            pltpu.SemaphoreType.DMA((2,))]),
        out_shape=jax.ShapeDtypeStruct((db, 8, nch, 2 * CMP_HIDDEN), F32),
        compiler_params=_cparams(("arbitrary", "arbitrary")),
        name="cmp_pq_paged",
    )(page_table, cache_t, cmp_pos.reshape(2, 2, kdim), cmp_w1.reshape(2, 2, kdim, CMP_HIDDEN).astype(BF16))


def _pad_rows(a, n):
    return jnp.concatenate([a, jnp.zeros((n - a.shape[0],) + a.shape[1:], a.dtype)], axis=0)


def _sample_mixer(x_sample, cache_cmp, cache_slc, cache_win, st_c, st_n, st_m, st_conv, page_table,
                  norm_g, wb, gate_bias, conv_w, ml_norm_g, cmp_pos, cmp_w1, cmp_w2, nsa_norm_g, w_out_b, rel_bias):
    db = x_sample.shape[0]
    n_pages = page_table.shape[1]
    past = n_pages * PAGE_SIZE
    tok = SAMPLE_TOKEN_ROWS
    x16 = _pad_rows(x_sample.reshape(db, D_MODEL), tok)
    u = _proj(x16, norm_g, wb, tok, PROJ_COLS)
    small = u[:, COL_SMALL:COL_SMALL + LANE]

    T = SAMPLE_SEQ_ROWS
    useq = jnp.zeros((db, T, 4 * D_ML), F32)
    useq = useq.at[:, T - CONV_W:T - 1, 0:2 * D_ML].set(st_conv)
    useq = useq.at[:, T - 1, :].set(u[:db, 0:4 * D_ML])
    sseq = jnp.zeros((db, T, LANE), F32).at[:, :T - 1, LANE - 1].set(1.0)
    sseq = sseq.at[:, T - 1, :].set(small[:db])
    y_seq, conv_n, c_n, n_n, m_n = _mlstm(
        useq.reshape(db * T, 4 * D_ML), sseq.reshape(db * T, LANE), jnp.zeros((db, CONV_W - 1, 2 * D_ML), F32),
        st_c, st_n, st_m, conv_w, gate_bias, ml_norm_g, db, T, T)
    y_ml = y_seq.reshape(db, T, D_ML)[:, T - 1]

    n_pool = cache_cmp.shape[0]
    cmp_t = cache_cmp.transpose(0, 2, 3, 4, 1).reshape(n_pool * 2 * D_KV, PAGE_SIZE)
    kvc = _cmp_hid(_paged_pq(cmp_t, page_table, cmp_pos, cmp_w1), cmp_w2)
    nch = past // CMP_STRIDE
    n_cmp = (past + 1) // CMP_STRIDE - CMP_BLOCK // CMP_STRIDE + 1
    n_slc = -(-(past + 1) // SLC_BLOCK)
    q = u[:db, COL_QNSA:COL_QNSA + D_NSA].reshape(db, NSA_KV_HEADS, NSA_GROUP, NSA_HEAD_DIM)
    q8 = jnp.concatenate([q, jnp.zeros_like(q)], axis=2)
    bd = _bias_by_distance(rel_bias, past + 1).reshape(NSA_KV_HEADS, NSA_GROUP, past + 1)
    pad_g = lambda a: jnp.concatenate([a, jnp.zeros_like(a)], axis=1)
    dist_c = np.clip(past - (np.arange(nch) * CMP_STRIDE + CMP_BLOCK - 1), 0, None)
    o_c8, sel = _sample_cattn(q8, kvc, pad_g(bd[:, :, dist_c]), n_cmp, n_slc)

    mask = sel[:, :, 0, :n_slc] > 0.5
    idx = jnp.sort(jnp.where(mask, jnp.arange(n_slc, dtype=jnp.int32), jnp.int32(1 << 20)), axis=-1)
    n_gather = SLC_TOPN - 1
    idx = idx[..., :n_gather]
    pages_per_block = PAGE_SIZE // SLC_BLOCK
    logical_page = idx // pages_per_block
    pages = jnp.take_along_axis(page_table[:, None, :], logical_page, axis=2).reshape(-1).astype(jnp.int32)
    halves = (idx % pages_per_block).reshape(-1).astype(jnp.int32)
    kpos = (logical_page[..., None] * PAGE_SIZE + jnp.arange(PAGE_SIZE, dtype=jnp.int32)).reshape(db, NSA_KV_HEADS, -1)
    hh = jnp.arange(NSA_KV_HEADS)[None, :, None, None]
    gg = jnp.arange(NSA_GROUP)[None, None, :, None]
    bias_sel = bd[hh, gg, (past - kpos)[:, :, None, :]]
    bias_sel = jnp.concatenate([bias_sel, jnp.broadcast_to(bd[None, :, :, 0:1], (db, NSA_KV_HEADS, NSA_GROUP, 1)),
                                jnp.zeros((db, NSA_KV_HEADS, NSA_GROUP, PAGE_SIZE - 1), F32)], axis=-1)
    bias_sel = jnp.concatenate([bias_sel, jnp.zeros_like(bias_sel)], axis=2)
    wlen = cache_win.shape[1]
    dist_w = np.clip(wlen - np.arange(wlen + PAGE_SIZE), 0, None)
    kvs_new = _kv_from_per_head(u[:db, COL_KVS:COL_KVS + 2 * D_KV])
    kvw_new = _kv_from_per_head(u[:db, COL_KVW:COL_KVW + 2 * D_KV])
    lane_pad = lambda a: jnp.pad(a[:, :, None], ((0, 0), (0, 0), (0, PAGE_SIZE - 1)))
    slc_t = cache_slc.transpose(0, 2, 3, 4, 1).reshape(n_pool * 2 * D_KV, PAGE_SIZE)
    win_t = cache_win.transpose(0, 2, 3, 4, 1).reshape(db, 2 * D_KV, wlen)
    o_s8, o_w8 = _sample_attn(pages, halves, q8, slc_t, lane_pad(kvs_new), bias_sel, win_t, lane_pad(kvw_new),
                              pad_g(bd[:, :, dist_w]), n_gather)
    win2d = cache_win.reshape(db, wlen, 2 * D_KV)

    heads = lambda o: _pad_rows(o[:, :, :NSA_GROUP, :].reshape(db, D_NSA), tok)
    y = _outproj(_pad_rows(y_ml, tok), heads(o_c8), heads(o_s8), heads(o_w8), small, nsa_norm_g, w_out_b, x16, tok,
                 PROJ_COLS)
    kvshape = (1, db, 1, 2, NSA_KV_HEADS, NSA_HEAD_DIM)
    new_win = jnp.concatenate([win2d[:, 1:], kvw_new[:, None, :]], axis=1)
    states = (u[:db, COL_KVC:COL_KVC + 2 * D_KV].reshape(kvshape), kvs_new.reshape(kvshape),
              new_win.reshape((1, db, wlen, 2, NSA_KV_HEADS, NSA_HEAD_DIM)),
              c_n[None], n_n[None], m_n[None], conv_n[None])
    return y, states


def kernel(x_prompt, x_sample, cache_cmp_kv, cache_slc_kv, cache_win_kv, state_mlstm_C, state_mlstm_n,
           state_mlstm_m, state_conv, page_table, rel_bias, norm_mix_g, w_in, b_ig, b_fg, conv_w, ml_norm_g,
           cmp_pos, cmp_w1, cmp_w2, nsa_norm_g, w_out, norm_ffn_g, w_router_grp, b_router_grp, w_router_exp,
           b_router_exp, w_gate, w_up, w_down, norm_final_g):
    B, S, D = x_prompt.shape
    wb = _reorder_w_in(w_in[0])
    gate_bias = jnp.zeros((1, LANE), F32).at[0, 0:ML_HEADS].set(b_ig[0]).at[0, ML_HEADS:2 * ML_HEADS].set(b_fg[0])
    w_out_b = w_out[0].astype(BF16)
    yp, st_p = _prompt_mixer(x_prompt, norm_mix_g, wb, gate_bias, conv_w[0], ml_norm_g, cmp_pos[0], cmp_w1[0],
                             cmp_w2[0], nsa_norm_g, w_out_b, rel_bias)
    ys, st_s = _sample_mixer(x_sample, cache_cmp_kv[0], cache_slc_kv[0], cache_win_kv[0], state_mlstm_C[0],
                             state_mlstm_n[0], state_mlstm_m[0], state_conv[0], page_table, norm_mix_g, wb, gate_bias,
                             conv_w[0], ml_norm_g, cmp_pos[0], cmp_w1[0], cmp_w2[0], nsa_norm_g, w_out_b, rel_bias)
    moe_w = (norm_ffn_g, w_router_grp[0], b_router_grp[0], w_router_exp[0], b_router_exp[0],
             w_gate[0], w_up[0], w_down[0], norm_final_g[None, :])
    DB, L, _ = x_sample.shape
    out_p = _moe_final(yp.reshape(B * S, D), *moe_w, MOE_TOKEN_ROWS, MOE_EXPERT_ROWS).reshape(B, S, D)
    out_s = _moe_final(_pad_rows(ys, SAMPLE_MOE_ROWS), *moe_w, SAMPLE_MOE_ROWS, SAMPLE_EXPERT_ROWS)[:DB]
    out_s = out_s.reshape(DB, L, D)
    outs = [out_p, out_s]
    for a, b in zip(st_p, st_s):
        outs += [a, b]
    return tuple(outs)


def _prompt_mixer(x_prompt, norm_g, wb, gate_bias, conv_w, ml_norm_g, cmp_pos, cmp_w1, cmp_w2, nsa_norm_g,
                  w_out_b, rel_bias):
    B, S, D = x_prompt.shape
    x2d = x_prompt.reshape(B * S, D)
    tm = min(PROJ_ROWS, S)
    u, kvt_c, kvt_s, kvt_w = _proj(x2d, norm_g, wb, tm, PROJ_COLS, seq=S)
    small = u[:, COL_SMALL:COL_SMALL + LANE]
    y_ml, conv_n, c_n, n_n, m_n = _mlstm(
        u, small, jnp.zeros((B, CONV_W - 1, 2 * D_ML), F32),
        jnp.zeros((B, ML_HEADS, ML_HEAD_DIM, ML_HEAD_DIM), F32), jnp.zeros((B, ML_HEADS, ML_HEAD_DIM), F32),
        jnp.full((B, ML_HEADS), -jnp.inf, F32), conv_w, gate_bias, ml_norm_g, B, S, ML_CHUNK_ROWS)
    kvc = _compress(u, COL_KVC // (2 * D_KV), B, S, cmp_pos, cmp_w1, cmp_w2, min(S, CMP_TOKENS))
    o_c, sel_t = _cattn(u, kvc, rel_bias, B, S, min(S, CMP_QUERY_ROWS))
    o_s = _flash(u, COL_KVS, rel_bias, sel_t, B, S)
    o_w = _flash(u, COL_KVW, rel_bias, None, B, S)
    y = _outproj(y_ml, o_c, o_s, o_w, small, nsa_norm_g, w_out_b, x2d, tm, PROJ_COLS)
    win = min(WINDOW, S)
    rows = lambda kvt: kvt.transpose(0, 4, 1, 2, 3)[None]
    states = (rows(kvt_c), rows(kvt_s), rows(kvt_w[..., S - win:]),
              c_n[None], n_n[None], m_n[None], conv_n[None])
    return y.reshape(B, S, D), states
```

```python
import functools
import math

import numpy as np
import jax
import jax.numpy as jnp
from jax import lax
from jax.experimental import pallas as pl
from jax.experimental.pallas import tpu as pltpu

F32 = jnp.float32
BF16 = jnp.bfloat16

D_MODEL = 2048
ML_HEADS = 4
ML_HEAD_DIM = 256
D_ML = 1024
CONV_W = 4
NSA_HEADS = 16
NSA_HEAD_DIM = 64
D_NSA = 1024
NSA_KV_HEADS = 4
NSA_GROUP = 4
D_KV = 256
CMP_BLOCK = 32
CMP_STRIDE = 16
CMP_HIDDEN = 256
SLC_BLOCK = 64
SLC_TOPN = 16
WINDOW = 512
N_BUCKETS = 32
MAX_DISTANCE = 2048
N_GROUPS = 4
EXPERTS_PER_GROUP = 8
N_EXPERTS = 32
D_EXPERT = 512
PAGE_SIZE = 128
EPS = 1e-6
NEG_BIG = -1e30
FORCE_SCORE = 1e4

LANE = 128
COL_QML, COL_KML, COL_VML, COL_OML = 0, 1024, 2048, 3072
COL_QNSA = 4096
COL_KVC, COL_KVS, COL_KVW = 5120, 5632, 6144
COL_SMALL = 6656
N_PROJ = 7168
VMEM_LIMIT = 56 * 1024 * 1024

PROJ_ROWS = 1024
PROJ_COLS = 2 * D_KV
ML_CHUNK_ROWS = 256
CMP_TOKENS = 4096
CMP_QUERY_ROWS = 256
MOE_TOKEN_ROWS = 256
MOE_EXPERT_ROWS = 128
SAMPLE_TOKEN_ROWS = 16
SAMPLE_SEQ_ROWS = 128
SAMPLE_MOE_ROWS = 128
SAMPLE_EXPERT_ROWS = 16


def _cparams(sem, vmem=VMEM_LIMIT):
    return pltpu.CompilerParams(dimension_semantics=sem, vmem_limit_bytes=vmem)


def _split2(x):
    hi = x.astype(BF16)
    lo = (x - hi.astype(F32)).astype(BF16)
    return hi, lo


def _split3(x):
    hi = x.astype(BF16)
    r = x - hi.astype(F32)
    mid = r.astype(BF16)
    lo = (r - mid.astype(F32)).astype(BF16)
    return hi, mid, lo


def _dot(a, b):
    return jnp.dot(a, b, preferred_element_type=F32)


def _dot_nt(a, b):
    return lax.dot_general(a, b, (((1,), (1,)), ((), ())), preferred_element_type=F32)


def _dot_tn(a, b):
    return lax.dot_general(a, b, (((0,), (0,)), ((), ())), preferred_element_type=F32)


def _proj_kernel(x_ref, g_ref, w_ref, o_ref, *rest, kv_blocks):
    h_scr = rest[-1]
    j = pl.program_id(1)

    @pl.when(j == 0)
    def _():
        x = x_ref[...]
        ms = jnp.mean(x * x, axis=-1, keepdims=True)
        h_scr[...] = (x * lax.rsqrt(ms + EPS) * g_ref[...]).astype(BF16)

    res = _dot(h_scr[...], w_ref[...])
    o_ref[...] = res

    for (block, per_head), t_ref in zip(kv_blocks, rest[:-1]):
        @pl.when(j == block)
        def _():
            res_t = res.T
            for c in range(2):
                for hh in range(NSA_KV_HEADS):
                    piece = hh * 2 + c if per_head else c * NSA_KV_HEADS + hh
                    t_ref[0, c, hh] = res_t[piece * NSA_HEAD_DIM:(piece + 1) * NSA_HEAD_DIM, :]


def _proj(x2d, g, wb, tm, tn, seq=None):
    n, d = x2d.shape
    nc = wb.shape[1]
    kv_blocks = ()
    out_specs = [pl.BlockSpec((tm, tn), lambda i, j: (i, j))]
    out_shape = [jax.ShapeDtypeStruct((n, nc), F32)]
    if seq is not None:
        assert tn == 2 * D_KV and seq % tm == 0
        kv_blocks = ((COL_KVC // tn, False), (COL_KVS // tn, True), (COL_KVW // tn, True))
        per_seq = seq // tm
        for _ in kv_blocks:
            out_specs.append(pl.BlockSpec((1, 2, NSA_KV_HEADS, NSA_HEAD_DIM, tm),
                                          lambda i, j: (i // per_seq, 0, 0, 0, i % per_seq)))
            out_shape.append(jax.ShapeDtypeStruct((n // seq, 2, NSA_KV_HEADS, NSA_HEAD_DIM, seq), F32))
    outs = pl.pallas_call(
        functools.partial(_proj_kernel, kv_blocks=kv_blocks),
        grid=(n // tm, nc // tn),
        in_specs=[pl.BlockSpec((tm, d), lambda i, j: (i, 0)),
                  pl.BlockSpec((1, d), lambda i, j: (0, 0)),
                  pl.BlockSpec((d, tn), lambda i, j: (0, j))],
        out_specs=out_specs,
        out_shape=out_shape,
        scratch_shapes=[pltpu.VMEM((tm, d), BF16)],
        compiler_params=_cparams(("arbitrary", "arbitrary")),
        name="proj",
    )(x2d, g, wb)
    return outs if seq is not None else outs[0]


def _reorder_w_in(w_in):
    w_in = w_in.astype(BF16)
    big = w_in[:, :4 * D_ML]
    small_a = w_in[:, 4 * D_ML:4 * D_ML + 2 * ML_HEADS]
    rest = w_in[:, 4 * D_ML + 2 * ML_HEADS:]
    q_and_cmp = rest[:, :D_NSA + 2 * D_KV]
    gate = rest[:, D_NSA + 6 * D_KV:]
    d = w_in.shape[0]

    def per_head(w):
        return w.reshape(d, 2, NSA_KV_HEADS, NSA_HEAD_DIM).transpose(0, 2, 1, 3).reshape(d, 2 * D_KV)

    kv_s = per_head(rest[:, D_NSA + 2 * D_KV:D_NSA + 4 * D_KV])
    kv_w = per_head(rest[:, D_NSA + 4 * D_KV:D_NSA + 6 * D_KV])
    pad = jnp.zeros((d, N_PROJ - COL_SMALL - 2 * ML_HEADS - 3 * NSA_HEADS), w_in.dtype)
    return jnp.concatenate([big, q_and_cmp, kv_s, kv_w, small_a, gate, pad], axis=1).astype(BF16)


def _kv_from_per_head(kv2d):
    n = kv2d.shape[0]
    return kv2d.reshape(n, NSA_KV_HEADS, 2, NSA_HEAD_DIM).transpose(0, 2, 1, 3).reshape(n, 2 * D_KV)


def _log_sigmoid(x):
    return jnp.minimum(x, 0.0) - jnp.log1p(jnp.exp(-jnp.abs(x)))


def _mlstm_kernel(q_ref, k_ref, v_ref, o_ref, s_ref, cb_ref, c0_ref, n0_ref, m0_ref,
                  cw_ref, gb_ref, ng_ref,
                  y_ref, cbo_ref, co_ref, no_ref, mo_ref,
                  ext_scr, c_scr, n_scr, m_scr, *, T):
    c = pl.program_id(1)
    nc = pl.num_programs(1)

    @pl.when(c == 0)
    def _():
        ext_scr[0:8, :] = jnp.zeros((8, 2 * D_ML), F32)
        ext_scr[5:8, :] = cb_ref[0]
        c_scr[...] = c0_ref[0]
        n_scr[...] = n0_ref[0]
        m_scr[...] = m0_ref[0]

    ext_scr[8:8 + T, 0:D_ML] = q_ref[...]
    ext_scr[8:8 + T, D_ML:2 * D_ML] = k_ref[...]
    conv = ext_scr[5:5 + T, :] * cw_ref[0:1, :]
    for j in range(1, CONV_W):
        conv = conv + ext_scr[5 + j:5 + j + T, :] * cw_ref[j:j + 1, :]
    tail = ext_scr[8 + T - 3:8 + T, :]
    ext_scr[5:8, :] = tail
    cbo_ref[0] = tail
    qk = conv * jax.nn.sigmoid(conv)

    pre = s_ref[...] + gb_ref[...]
    col = lax.broadcasted_iota(jnp.int32, pre.shape, 1)
    padrow = s_ref[:, LANE - 1:LANE] > 0.5
    gates = jnp.where(col < ML_HEADS, pre, _log_sigmoid(pre))
    gates = jnp.where(padrow, jnp.where(col < ML_HEADS, NEG_BIG, 0.0), gates)
    g_r = gates.T
    ti = lax.broadcasted_iota(jnp.int32, (T, T), 0)
    si = lax.broadcasted_iota(jnp.int32, (T, T), 1)
    upper = (ti <= si).astype(BF16)
    g_fin = jnp.where(lax.broadcasted_iota(jnp.int32, g_r.shape, 0) < ML_HEADS, 0.0, g_r)
    hi, mid, lo = _split3(g_fin)
    cum_r = _dot(hi, upper) + _dot(mid, upper) + _dot(lo, upper)
    rowi = lax.broadcasted_iota(jnp.int32, g_r.shape, 0)
    a_r = jnp.where(rowi < ML_HEADS, g_r, cum_r)
    a_c = a_r.T
    causal = si <= ti

    for h in range(ML_HEADS):
        sl = slice(h * ML_HEAD_DIM, (h + 1) * ML_HEAD_DIM)
        q = qk[:, h * ML_HEAD_DIM:(h + 1) * ML_HEAD_DIM]
        k = qk[:, D_ML + h * ML_HEAD_DIM:D_ML + (h + 1) * ML_HEAD_DIM] * (ML_HEAD_DIM ** -0.5)
        v = v_ref[:, sl]
        ig_r = a_r[h:h + 1, :]
        b_r = a_r[ML_HEADS + h:ML_HEADS + h + 1, :]
        ig_c = a_c[:, h:h + 1]
        b_c = a_c[:, ML_HEADS + h:ML_HEADS + h + 1]
        m_prev = m_scr[h:h + 1, 0:1]
        logd = jnp.where(causal, b_c - b_r + ig_r, -jnp.inf)
        inter = b_c + m_prev
        m_t = jnp.maximum(inter, jnp.max(logd, axis=1, keepdims=True))
        w_intra = jnp.exp(logd - m_t)
        w_inter = jnp.exp(inter - m_t)
        qb = q.astype(BF16)
        kb = k.astype(BF16)
        vb = v.astype(BF16)
        sc = _dot_nt(qb, kb) * w_intra
        cmat = c_scr[h]
        nvec = n_scr[h:h + 1, :]
        num = _dot(sc.astype(BF16), vb) + w_inter * _dot(qb, cmat.astype(BF16))
        qn = jnp.sum(qb.astype(F32) * nvec.astype(BF16).astype(F32), axis=1, keepdims=True)
        den = jnp.sum(sc, axis=1, keepdims=True) + w_inter * qn
        hh = num / jnp.maximum(jnp.abs(den), jnp.exp(-m_t))
        m_new = m_t[T - 1:T, :]
        b_last = b_c[T - 1:T, :]
        w_s = jnp.exp(b_last - b_c + ig_c - m_new)
        decay = jnp.exp(b_last + m_prev - m_new)
        kw = k * w_s
        c_new = decay * cmat + _dot_tn(kw.astype(BF16), vb)
        n_new = decay * nvec + jnp.sum(kw, axis=0, keepdims=True)
        c_scr[h] = c_new
        n_scr[h:h + 1, :] = n_new
        m_scr[h:h + 1, :] = jnp.broadcast_to(m_new, (1, LANE))
        hn = hh * lax.rsqrt(jnp.mean(hh * hh, axis=1, keepdims=True) + EPS) * ng_ref[:, sl]
        y_ref[:, sl] = (hn * jax.nn.sigmoid(o_ref[:, sl])).astype(y_ref.dtype)

    @pl.when(c == nc - 1)
    def _():
        co_ref[0] = c_scr[...]
        no_ref[0] = n_scr[...]
        mo_ref[0] = m_scr[...]


def _mlstm(u, small, conv_buf, c0, n0, m0, conv_w, gate_bias, norm_g, batch, seq, T):
    nc = seq // T
    cb = D_ML // 1024
    m0b = jnp.broadcast_to(m0[:, :, None], (batch, ML_HEADS, LANE))
    m0b = jnp.concatenate([m0b, jnp.zeros((batch, 8 - ML_HEADS, LANE), F32)], axis=1)
    n0p = jnp.concatenate([n0, jnp.zeros((batch, 8 - ML_HEADS, ML_HEAD_DIM), F32)], axis=1)
    row = lambda b, c: (b * nc + c, 0)
    outs = pl.pallas_call(
        functools.partial(_mlstm_kernel, T=T),
        grid=(batch, nc),
        in_specs=[pl.BlockSpec((T, D_ML), lambda b, c: (b * nc + c, COL_QML // D_ML)),
                  pl.BlockSpec((T, D_ML), lambda b, c: (b * nc + c, COL_KML // D_ML)),
                  pl.BlockSpec((T, D_ML), lambda b, c: (b * nc + c, COL_VML // D_ML)),
                  pl.BlockSpec((T, D_ML), lambda b, c: (b * nc + c, COL_OML // D_ML)),
                  pl.BlockSpec((T, LANE), row),
                  pl.BlockSpec((1, CONV_W - 1, 2 * D_ML), lambda b, c: (b, 0, 0)),
                  pl.BlockSpec((1, ML_HEADS, ML_HEAD_DIM, ML_HEAD_DIM), lambda b, c: (b, 0, 0, 0)),
                  pl.BlockSpec((1, 8, ML_HEAD_DIM), lambda b, c: (b, 0, 0)),
                  pl.BlockSpec((1, 8, LANE), lambda b, c: (b, 0, 0)),
                  pl.BlockSpec((CONV_W, 2 * D_ML), lambda b, c: (0, 0)),
                  pl.BlockSpec((1, LANE), lambda b, c: (0, 0)),
                  pl.BlockSpec((1, D_ML), lambda b, c: (0, 0))],
        out_specs=[pl.BlockSpec((T, D_ML), row),
                   pl.BlockSpec((1, CONV_W - 1, 2 * D_ML), lambda b, c: (b, 0, 0)),
                   pl.BlockSpec((1, ML_HEADS, ML_HEAD_DIM, ML_HEAD_DIM), lambda b, c: (b, 0, 0, 0)),
                   pl.BlockSpec((1, 8, ML_HEAD_DIM), lambda b, c: (b, 0, 0)),
                   pl.BlockSpec((1, 8, LANE), lambda b, c: (b, 0, 0))],
        out_shape=[jax.ShapeDtypeStruct((batch * seq, D_ML), BF16),
                   jax.ShapeDtypeStruct((batch, CONV_W - 1, 2 * D_ML), F32),
                   jax.ShapeDtypeStruct((batch, ML_HEADS, ML_HEAD_DIM, ML_HEAD_DIM), F32),
                   jax.ShapeDtypeStruct((batch, 8, ML_HEAD_DIM), F32),
                   jax.ShapeDtypeStruct((batch, 8, LANE), F32)],
        scratch_shapes=[pltpu.VMEM((8 + T, 2 * D_ML), F32),
                        pltpu.VMEM((ML_HEADS, ML_HEAD_DIM, ML_HEAD_DIM), F32),
                        pltpu.VMEM((8, ML_HEAD_DIM), F32),
                        pltpu.VMEM((8, LANE), F32)],
        compiler_params=_cparams(("arbitrary", "arbitrary")),
        name="mlstm",
    )(u, u, u, u, small, conv_buf, c0, n0p, m0b, conv_w, gate_bias, norm_g)
    y, cbo, co, no, mo = outs
    return y, cbo, co, no[:, :ML_HEADS], mo[:, :ML_HEADS, 0]


def _bucket_np(dist):
    n = np.maximum(dist, 0)
    max_exact = N_BUCKETS // 2
    nf = np.maximum(n, 1).astype(np.float64)
    large = max_exact + (np.log(nf / max_exact) / math.log(MAX_DISTANCE / max_exact)
                         * (N_BUCKETS - max_exact)).astype(np.int64)
    return np.where(n < max_exact, n, np.minimum(large, N_BUCKETS - 1)).astype(np.int32)


def _bias_by_distance(rel_bias, n):
    return rel_bias.astype(F32)[_bucket_np(np.arange(n))].T


def _overlap_t(n_cmp, nch, n_slc):
    c0 = np.arange(nch) * CMP_STRIDE
    s0 = np.arange(n_slc) * SLC_BLOCK
    ov = np.minimum(c0[None, :] + CMP_BLOCK, s0[:, None] + SLC_BLOCK) - np.maximum(c0[None, :], s0[:, None])
    ov = np.clip(ov, 0, None).astype(np.float32) / CMP_BLOCK
    ov[:, n_cmp:] = 0.0
    return jnp.asarray(ov, BF16)


def _pq_kernel(x0_ref, x1_ref, x2_ref, x3_ref, pos_ref, w_ref, o_ref, *, rows):
    for pair, x_ref in enumerate((x0_ref, x1_ref, x2_ref, x3_ref)):
        toks = [x_ref[pl.ds(l, rows, stride=CMP_STRIDE), :] for l in range(CMP_STRIDE)]
        for half in range(2):
            ch = 2 * pair + half
            c = ch // NSA_KV_HEADS
            lanes = slice(half * NSA_HEAD_DIM, (half + 1) * NSA_HEAD_DIM)
            chunk = jnp.concatenate([t[:, lanes] for t in toks], axis=1)
            for part in range(2):
                lhs = (chunk + pos_ref[c, part:part + 1, :]).astype(BF16)
                o_ref[0, ch, :, part * CMP_HIDDEN:(part + 1) * CMP_HIDDEN] = _dot(lhs, w_ref[c, part])


def _hid_kernel(pq_ref, w2_ref, o_ref, *, nch):
    p = pq_ref[0, 0, :, 0:CMP_HIDDEN]
    q = pltpu.roll(pq_ref[0, 0, :, CMP_HIDDEN:2 * CMP_HIDDEN], nch - 1, 0)
    hid = jax.nn.gelu(p + q, approximate=True)
    o_ref[0, 0] = _dot(hid.astype(BF16), w2_ref[0])


def _compress(kv2d, col_block, bsz, seq, cmp_pos, cmp_w1, cmp_w2, tt):
    nch = seq // CMP_STRIDE
    kdim = CMP_STRIDE * NSA_HEAD_DIM
    nt = seq // tt
    tn = tt // CMP_STRIDE
    w1 = cmp_w1.reshape(2, 2, kdim, CMP_HIDDEN).astype(BF16)
    pos = cmp_pos.reshape(2, 2, kdim)
    pq = pl.pallas_call(
        functools.partial(_pq_kernel, rows=tn),
        grid=(bsz, nt),
        in_specs=[pl.BlockSpec((tt, LANE), functools.partial(lambda b, i, k: (b * nt + i, col_block * 4 + k), k=k))
                  for k in range(4)] +
                 [pl.BlockSpec((2, 2, kdim), lambda b, i: (0, 0, 0)),
                  pl.BlockSpec((2, 2, kdim, CMP_HIDDEN), lambda b, i: (0, 0, 0, 0))],
        out_specs=pl.BlockSpec((1, 8, tn, 2 * CMP_HIDDEN), lambda b, i: (b, 0, i, 0)),
        out_shape=jax.ShapeDtypeStruct((bsz, 8, nch, 2 * CMP_HIDDEN), F32),
        compiler_params=_cparams(("arbitrary", "arbitrary")),
        name="cmp_pq",
    )(kv2d, kv2d, kv2d, kv2d, pos, w1)
    return _cmp_hid(pq, cmp_w2)


def _cmp_hid(pq, cmp_w2):
    bsz, _, nch, _ = pq.shape
    return pl.pallas_call(
        functools.partial(_hid_kernel, nch=nch),
        grid=(bsz, 8),
        in_specs=[pl.BlockSpec((1, 1, nch, 2 * CMP_HIDDEN), lambda b, c: (b, c, 0, 0)),
                  pl.BlockSpec((1, CMP_HIDDEN, NSA_HEAD_DIM), lambda b, c: (c // NSA_KV_HEADS, 0, 0))],
        out_specs=pl.BlockSpec((1, 1, nch, NSA_HEAD_DIM), lambda b, c: (b, c, 0, 0)),
        out_shape=jax.ShapeDtypeStruct((bsz, 8, nch, NSA_HEAD_DIM), F32),
        compiler_params=_cparams(("arbitrary", "arbitrary")),
        name="cmp_hid",
    )(pq, cmp_w2.astype(BF16))


def _top_n_mask(score_t, blk, n_rows):
    rank = jnp.zeros(score_t.shape, jnp.int32)
    for i in range(n_rows):
        row = score_t[i:i + 1, :]
        beats = (row > score_t) | ((row == score_t) & (blk > i))
        rank = rank + beats.astype(jnp.int32)
    return (rank < min(SLC_TOPN, n_rows)).astype(F32)


def _bias_from_buckets(bucket, table_ref, first_head):
    biases = [jnp.zeros(bucket.shape, F32) for _ in range(NSA_GROUP)]
    for k in range(N_BUCKETS):
        hit = bucket == k
        biases = [jnp.where(hit, table_ref[k, first_head + g], biases[g]) for g in range(NSA_GROUP)]
    return biases


def _cattn_kernel(tab_ref, q_ref, kc_ref, vc_ref, bk_ref, ovt_ref, o_ref, sel_ref, bias_scr, strip_scr,
                  *, tq, nch, n_cmp, n_slc):
    h = pl.program_id(0)
    i = pl.program_id(1)

    bands = tq // CMP_STRIDE
    for g, strip in enumerate(_bias_from_buckets(bk_ref[0], tab_ref, h * NSA_GROUP)):
        strip_scr[g] = strip
        for a in range(bands):
            off = bands - 1 - a
            bias_scr[g, a * CMP_STRIDE:(a + 1) * CMP_STRIDE, :] = strip_scr[g, :, off:off + nch]

    t = i * tq + lax.broadcasted_iota(jnp.int32, (tq, nch), 0)
    n = lax.broadcasted_iota(jnp.int32, (tq, nch), 1)
    valid = (t - CMP_STRIDE * n - (CMP_BLOCK - 1) >= 0) & (n < n_cmp)
    blk = lax.broadcasted_iota(jnp.int32, (n_slc, tq), 0)
    cur = (i * tq + lax.broadcasted_iota(jnp.int32, (n_slc, tq), 1)) // SLC_BLOCK
    forced = (blk == 0) | (blk == cur) | (blk == cur - 1)
    for bb in range(q_ref.shape[0]):
        kc = kc_ref[bb, 0].astype(BF16)
        vc = vc_ref[bb, 0].astype(BF16)
        pc = jnp.zeros((tq, nch), F32)
        for g in range(NSA_GROUP):
            sl = slice(g * NSA_HEAD_DIM, (g + 1) * NSA_HEAD_DIM)
            qg = (q_ref[bb, :, sl] * (NSA_HEAD_DIM ** -0.5)).astype(BF16)
            s = _dot_nt(qg, kc) + bias_scr[g]
            s = jnp.where(valid, s, NEG_BIG)
            e = jnp.exp(s - jnp.max(s, axis=1, keepdims=True))
            p = jnp.where(valid, e / jnp.sum(e, axis=1, keepdims=True), 0.0)
            o_ref[bb, :, sl] = _dot(p.astype(BF16), vc)
            pc = pc + p
        score_t = _dot_nt(ovt_ref[...], pc.astype(BF16))
        score_t = jnp.where(forced, FORCE_SCORE, score_t)
        score_t = jnp.where(blk > cur, -FORCE_SCORE, score_t)
        sel_ref[bb, 0] = _top_n_mask(score_t, blk, n_slc)


def _cattn(u, kvc, rel_bias, batch, seq, tq):
    nch = kvc.shape[2]
    n_cmp = nch - 1
    n_slc = seq // SLC_BLOCK
    ni = seq // tq
    ovt = _overlap_t(n_cmp, nch, n_slc)
    qcol = COL_QNSA // (NSA_GROUP * NSA_HEAD_DIM)
    bands = tq // CMP_STRIDE
    strip_w = -(-(nch + bands - 1) // LANE) * LANE
    dist = (tq * np.arange(ni)[:, None, None] + np.arange(CMP_STRIDE)[None, :, None]
            - CMP_STRIDE * (np.arange(strip_w)[None, None, :] - (bands - 1)) - (CMP_BLOCK - 1))
    buckets = jnp.asarray(_bucket_np(dist))
    o_c, sel_t = pl.pallas_call(
        functools.partial(_cattn_kernel, tq=tq, nch=nch, n_cmp=n_cmp, n_slc=n_slc),
        grid=(NSA_KV_HEADS, ni),
        in_specs=[pl.BlockSpec(memory_space=pltpu.SMEM),
                  pl.BlockSpec((batch, tq, NSA_GROUP * NSA_HEAD_DIM), lambda h, i: (0, i, qcol + h)),
                  pl.BlockSpec((batch, 1, nch, NSA_HEAD_DIM), lambda h, i: (0, h, 0, 0)),
                  pl.BlockSpec((batch, 1, nch, NSA_HEAD_DIM), lambda h, i: (0, NSA_KV_HEADS + h, 0, 0)),
                  pl.BlockSpec((1, CMP_STRIDE, strip_w), lambda h, i: (i, 0, 0)),
                  pl.BlockSpec((n_slc, nch), lambda h, i: (0, 0))],
        out_specs=[pl.BlockSpec((batch, tq, NSA_GROUP * NSA_HEAD_DIM), lambda h, i: (0, i, h)),
                   pl.BlockSpec((batch, 1, n_slc, tq), lambda h, i: (0, h, 0, i))],
        out_shape=[jax.ShapeDtypeStruct((batch, seq, D_NSA), F32),
                   jax.ShapeDtypeStruct((batch, NSA_KV_HEADS, n_slc, seq), F32)],
        scratch_shapes=[pltpu.VMEM((NSA_GROUP, tq, nch), F32), pltpu.VMEM((NSA_GROUP, CMP_STRIDE, strip_w), F32)],
        compiler_params=_cparams(("arbitrary", "arbitrary")),
        name="cmp_attn",
    )(rel_bias.astype(F32), u.reshape(batch, seq, u.shape[1]), kvc, kvc, buckets, ovt)
    return o_c.reshape(batch * seq, D_NSA), sel_t


TQ = 128
FLASH_GROUP = 8


def _flash_kernel(tab_ref, q_ref, kv_ref, bk_ref, *rest, selected, n_delta):
    if selected:
        sel_ref, o_ref, band_scr, qs_scr, m_scr, l_scr, acc_scr = rest
    else:
        o_ref, band_scr, qs_scr, m_scr, l_scr, acc_scr = rest
    h = pl.program_id(0)
    i = pl.program_id(1)
    n_batch = q_ref.shape[0]

    @pl.when(i == 0)
    def _():
        def fill(d, carry):
            for g, bias in enumerate(_bias_from_buckets(bk_ref[d], tab_ref, h * NSA_GROUP)):
                band_scr[d, :, g * TQ:(g + 1) * TQ] = bias
            return carry
        lax.fori_loop(0, n_delta, fill, 0)

    for bb in range(n_batch):
        for g in range(NSA_GROUP):
            qg = (q_ref[bb, :, g * NSA_HEAD_DIM:(g + 1) * NSA_HEAD_DIM] * (NSA_HEAD_DIM ** -0.5)).astype(BF16)
            qs_scr[bb, g * TQ:(g + 1) * TQ, :] = jnp.concatenate([qg, jnp.zeros_like(qg)], axis=1)
    is_key_lane = lax.broadcasted_iota(jnp.int32, (TQ, LANE), 1) < NSA_HEAD_DIM
    m_scr[...] = jnp.full(m_scr.shape, 0.5 * NEG_BIG, F32)
    l_scr[...] = jnp.zeros(l_scr.shape, F32)
    acc_scr[...] = jnp.zeros(acc_scr.shape, F32)
    key = lax.broadcasted_iota(jnp.int32, (TQ, TQ), 0)
    qry = lax.broadcasted_iota(jnp.int32, (TQ, TQ), 1)
    n_back = WINDOW // TQ

    def scores(bb, j, kind):
        kvj = kv_ref[bb, pl.ds(pl.multiple_of(j * TQ, TQ), TQ), :].astype(BF16)
        s = _dot_nt(kvj, qs_scr[bb]) + band_scr[i - j]
        mask = None
        if kind == "diag":
            mask = qry >= key
        elif kind == "far":
            mask = qry < key
        if selected:
            r = sel_ref[bb, 0, pl.ds(2 * j, 2), :]
            picked = jnp.where(key < SLC_BLOCK, r[0:1, :], r[1:2, :]) > 0.5
            mask = picked if mask is None else (mask & picked)
        if mask is not None:
            s = jnp.concatenate([jnp.where(mask, s[:, g * TQ:(g + 1) * TQ], NEG_BIG) for g in range(NSA_GROUP)], axis=1)
        return s, jnp.where(is_key_lane, jnp.ones_like(kvj), kvj)

    def update(tile_ids):
        all_tiles = [[scores(bb, j, kind) for j, kind in tile_ids] for bb in range(n_batch)]
        for bb, tiles in enumerate(all_tiles):
            m_old = m_scr[bb]
            m_new = m_old
            for s, _ in tiles:
                m_new = jnp.maximum(m_new, jnp.max(s, axis=0, keepdims=True))
            alpha = jnp.exp(m_old - m_new)
            l_new = alpha * l_scr[bb]
            acc = alpha * acc_scr[bb]
            for s, ones_v in tiles:
                p = jnp.exp(s - m_new)
                pv = _dot_tn(ones_v, p.astype(BF16))
                l_new = l_new + pv[0:1, :]
                acc = acc + pv[NSA_HEAD_DIM:2 * NSA_HEAD_DIM, :]
            m_scr[bb] = m_new
            l_scr[bb] = l_new
            acc_scr[bb] = acc

    def full_tiles(first, count):
        return [(first + k, "full") for k in range(count)]

    if selected:
        def group(t, carry):
            update(full_tiles(FLASH_GROUP * t, FLASH_GROUP))
            return carry

        n_groups = i // FLASH_GROUP
        lax.fori_loop(0, n_groups, group, 0)
        for rem in range(FLASH_GROUP):
            @pl.when(i - n_groups * FLASH_GROUP == rem)
            def _():
                update(full_tiles(n_groups * FLASH_GROUP, rem) + [(i, "diag")])
    else:
        @pl.when(i >= n_back)
        def _():
            update([(i - n_back, "far")] + full_tiles(i - n_back + 1, n_back - 1) + [(i, "diag")])

        for rem in range(n_back):
            @pl.when(i == rem)
            def _():
                update(full_tiles(0, rem) + [(i, "diag")])

    for bb in range(n_batch):
        for g in range(NSA_GROUP):
            cols = slice(g * TQ, (g + 1) * TQ)
            o_ref[bb, :, g * NSA_HEAD_DIM:(g + 1) * NSA_HEAD_DIM] = (acc_scr[bb, :, cols] / l_scr[bb, :, cols]).T


def _flash(u, kv_col, rel_bias, sel_t, batch, seq):
    ni = seq // TQ
    qcol = COL_QNSA // (NSA_GROUP * NSA_HEAD_DIM)
    selected = sel_t is not None
    n_delta = ni if selected else WINDOW // TQ + 1
    kvblk = kv_col // LANE
    delta = np.arange(n_delta)[:, None, None] * TQ + np.arange(TQ)[None, None, :] - np.arange(TQ)[None, :, None]
    buckets = jnp.asarray(_bucket_np(delta))
    u3 = u.reshape(batch, seq, u.shape[1])
    in_specs = [pl.BlockSpec(memory_space=pltpu.SMEM),
                pl.BlockSpec((batch, TQ, NSA_GROUP * NSA_HEAD_DIM), lambda h, i: (0, i, qcol + h)),
                pl.BlockSpec((batch, seq, LANE), lambda h, i: (0, 0, kvblk + h)),
                pl.BlockSpec((n_delta, TQ, TQ), lambda h, i: (0, 0, 0))]
    args = [rel_bias.astype(F32), u3, u3, buckets]
    if selected:
        n_slc = sel_t.shape[2]
        in_specs.append(pl.BlockSpec((batch, 1, n_slc, TQ), lambda h, i: (0, h, 0, i)))
        args.append(sel_t)
    scratch = [pltpu.VMEM((n_delta, TQ, NSA_GROUP * TQ), F32),
               pltpu.VMEM((batch, NSA_GROUP * TQ, LANE), BF16),
               pltpu.VMEM((batch, 1, NSA_GROUP * TQ), F32),
               pltpu.VMEM((batch, 1, NSA_GROUP * TQ), F32),
               pltpu.VMEM((batch, NSA_HEAD_DIM, NSA_GROUP * TQ), F32)]
    out = pl.pallas_call(
        functools.partial(_flash_kernel, selected=selected, n_delta=n_delta),
        grid=(NSA_KV_HEADS, ni),
        in_specs=in_specs,
        out_specs=pl.BlockSpec((batch, TQ, NSA_GROUP * NSA_HEAD_DIM), lambda h, i: (0, i, h)),
        out_shape=jax.ShapeDtypeStruct((batch, seq, D_NSA), F32),
        scratch_shapes=scratch,
        compiler_params=_cparams(("arbitrary", "arbitrary")),
        name="slc_attn" if selected else "win_attn",
    )(*args)
    return out.reshape(batch * seq, D_NSA)


def _gate_expand():
    e = np.zeros((3, 2 * LANE, D_NSA), np.float32)
    for br in range(3):
        for hd in range(NSA_HEADS):
            for part in range(2):
                e[br, part * LANE + 2 * ML_HEADS + br * NSA_HEADS + hd,
                  hd * NSA_HEAD_DIM:(hd + 1) * NSA_HEAD_DIM] = 1.0
    return jnp.asarray(e, BF16)


def _outproj_kernel(yml_ref, oc_ref, os_ref, ow_ref, s_ref, e_ref, ng_ref, w_ref, x_ref, o_ref, cat_scr):
    @pl.when(pl.program_id(1) == 0)
    def _():
        sig = jax.nn.sigmoid(s_ref[...])
        hi_lo = jnp.concatenate(_split2(sig), axis=1)
        o = jnp.zeros(oc_ref.shape, F32)
        for br, ref in enumerate((oc_ref, os_ref, ow_ref)):
            o = o + _dot(hi_lo, e_ref[br]) * ref[...]
        o = o * lax.rsqrt(jnp.mean(o * o, axis=-1, keepdims=True) + EPS) * ng_ref[...]
        cat_scr[:, 0:D_ML] = yml_ref[...]
        cat_scr[:, D_ML:D_ML + D_NSA] = o.astype(BF16)

    o_ref[...] = x_ref[...] + _dot(cat_scr[...], w_ref[...])


def _outproj(y_ml, o_c, o_s, o_w, small, nsa_g, w_out_b, x2d, tm, tn):
    n, d = x2d.shape
    rows = lambda i, j: (i, 0)
    return pl.pallas_call(
        _outproj_kernel,
        grid=(n // tm, d // tn),
        in_specs=[pl.BlockSpec((tm, D_ML), rows), pl.BlockSpec((tm, D_NSA), rows),
                  pl.BlockSpec((tm, D_NSA), rows), pl.BlockSpec((tm, D_NSA), rows),
                  pl.BlockSpec((tm, LANE), rows),
                  pl.BlockSpec((3, 2 * LANE, D_NSA), lambda i, j: (0, 0, 0)),
                  pl.BlockSpec((1, D_NSA), lambda i, j: (0, 0)),
                  pl.BlockSpec((D_ML + D_NSA, tn), lambda i, j: (0, j)),
                  pl.BlockSpec((tm, tn), lambda i, j: (i, j))],
        out_specs=pl.BlockSpec((tm, tn), lambda i, j: (i, j)),
        out_shape=jax.ShapeDtypeStruct((n, d), F32),
        scratch_shapes=[pltpu.VMEM((tm, D_ML + D_NSA), BF16)],
        compiler_params=_cparams(("arbitrary", "arbitrary")),
        name="outproj",
    )(y_ml, o_c, o_s, o_w, small, _gate_expand(), nsa_g, w_out_b, x2d)


ROUTE_COL = N_GROUPS
BIG_COL = 1 << 20
SPLIT = D_MODEL // LANE
DMA_UNROLL = 8


def _store_split(ref, val, n):
    for k in range(SPLIT):
        ref[pl.ds(k, n, stride=SPLIT), :] = val[:, k * LANE:(k + 1) * LANE]


def _load_split(ref, n):
    return jnp.concatenate([ref[pl.ds(k, n, stride=SPLIT), :] for k in range(SPLIT)], axis=1)


PACKED_SPLIT = SPLIT // 2
HIGH_HALF = 0xFFFF0000


def _store_packed(ref, val, n):
    bits = pltpu.bitcast(val.astype(BF16).astype(F32), jnp.uint32)
    half = D_MODEL // 2
    packed = (bits[:, half:] & jnp.uint32(HIGH_HALF)) | (bits[:, :half] >> 16)
    for k in range(PACKED_SPLIT):
        ref[pl.ds(k, n, stride=PACKED_SPLIT), :] = packed[:, k * LANE:(k + 1) * LANE]


def _load_packed(ref, n):
    packed = jnp.concatenate([ref[pl.ds(k, n, stride=PACKED_SPLIT), :] for k in range(PACKED_SPLIT)], axis=1)
    low = pltpu.bitcast(packed << 16, F32)
    high = pltpu.bitcast(packed & jnp.uint32(HIGH_HALF), F32)
    return jnp.concatenate([low, high], axis=1).astype(BF16)


def _route_kernel(y_ref, g_ref, w_ref, b_ref, h_ref, info_ref, cnt_ref, carry_scr, *, tm):
    @pl.when(pl.program_id(0) == 0)
    def _():
        carry_scr[...] = jnp.zeros(carry_scr.shape, F32)

    x = y_ref[...]
    h = x * lax.rsqrt(jnp.mean(x * x, axis=-1, keepdims=True) + EPS) * g_ref[...]
    _store_packed(h_ref, h, tm)
    logit = _dot(h.astype(BF16), w_ref[...]) + b_ref[...]
    col = lax.broadcasted_iota(jnp.int32, logit.shape, 1)
    is_grp = col < N_GROUPS
    gmax = jnp.max(jnp.where(is_grp, logit, -jnp.inf), axis=1, keepdims=True)
    gtop = jnp.min(jnp.where(is_grp & (logit == gmax), col, BIG_COL), axis=1, keepdims=True)
    gsum = jnp.sum(jnp.where(is_grp, jnp.exp(logit - gmax), 0.0), axis=1, keepdims=True)
    first = ROUTE_COL + gtop * EXPERTS_PER_GROUP
    in_grp = (col >= first) & (col < first + EXPERTS_PER_GROUP)
    v1 = jnp.max(jnp.where(in_grp, logit, -jnp.inf), axis=1, keepdims=True)
    i1 = jnp.min(jnp.where(in_grp & (logit == v1), col, BIG_COL), axis=1, keepdims=True)
    rest = in_grp & (col != i1)
    v2 = jnp.max(jnp.where(rest, logit, -jnp.inf), axis=1, keepdims=True)
    i2 = jnp.min(jnp.where(rest & (logit == v2), col, BIG_COL), axis=1, keepdims=True)
    e = jnp.exp(v2 - v1)
    w1 = 1.0 / ((1.0 + e) * gsum)
    w2 = e / ((1.0 + e) * gsum)
    pick1 = col == i1
    pick2 = col == i2
    both = (pick1 | pick2).astype(F32)
    ri = lax.broadcasted_iota(jnp.int32, (tm, tm), 0)
    ci = lax.broadcasted_iota(jnp.int32, (tm, tm), 1)
    before = (ci < ri).astype(BF16)
    cum = _dot(before, both.astype(BF16)) + carry_scr[...]
    r1 = jnp.sum(jnp.where(pick1, cum, 0.0), axis=1, keepdims=True)
    r2 = jnp.sum(jnp.where(pick2, cum, 0.0), axis=1, keepdims=True)
    carry_scr[...] = carry_scr[...] + jnp.sum(both, axis=0, keepdims=True)
    cnt_ref[...] = jnp.broadcast_to(carry_scr[...], cnt_ref.shape)
    info = jnp.where(col == 0, (i1 - ROUTE_COL).astype(F32), 0.0)
    info = jnp.where(col == 1, (i2 - ROUTE_COL).astype(F32), info)
    info = jnp.where(col == 2, w1, info)
    info = jnp.where(col == 3, w2, info)
    info = jnp.where(col == 4, r1, info)
    info_ref[...] = jnp.where(col == 5, r2, info)


def _route(y2d, g, w_rg, b_rg, w_re, b_re, tm):
    n, d = y2d.shape
    wr = jnp.concatenate([w_rg, w_re, jnp.zeros((d, LANE - N_GROUPS - N_EXPERTS), F32)], axis=1).astype(BF16)
    bias = jnp.concatenate([b_rg, b_re, jnp.zeros((LANE - N_GROUPS - N_EXPERTS,), F32)])[None, :]
    rows = lambda i: (i, 0)
    fixed = lambda i: (0, 0)
    return pl.pallas_call(
        functools.partial(_route_kernel, tm=tm),
        grid=(n // tm,),
        in_specs=[pl.BlockSpec((tm, d), rows), pl.BlockSpec((1, d), fixed),
                  pl.BlockSpec((d, LANE), fixed), pl.BlockSpec((1, LANE), fixed)],
        out_specs=[pl.BlockSpec((tm * PACKED_SPLIT, LANE), rows), pl.BlockSpec((tm, LANE), rows),
                   pl.BlockSpec((8, LANE), fixed)],
        out_shape=[jax.ShapeDtypeStruct((n * PACKED_SPLIT, LANE), jnp.uint32), jax.ShapeDtypeStruct((n, LANE), F32),
                   jax.ShapeDtypeStruct((8, LANE), F32)],
        scratch_shapes=[pltpu.VMEM((1, LANE), F32)],
        compiler_params=_cparams(("arbitrary",)),
        name="moe_route",
    )(y2d, g, wr, bias)


W_CHUNKS = 4


def _expert_kernel(te_ref, nu_ref, src_ref, nx_ref, ws_ref, h_hbm, wg_hbm, wu_hbm, wd_hbm, o_ref,
                   xbuf, wg_f, wu_f, wd_f, wg_b, wu_b, wd_b, sem, wsem, *, tme):
    i = pl.program_id(0)
    slot = i % 2

    def weight_copies(expert, wslot):
        copies = []
        for hbm, buf in ((wg_hbm, wg_f), (wu_hbm, wu_f), (wd_hbm, wd_f)):
            step = hbm.shape[1] // W_CHUNKS
            for c in range(W_CHUNKS):
                copies.append(pltpu.make_async_copy(hbm.at[expert, pl.ds(c * step, step)],
                                                    buf.at[wslot, pl.ds(c * step, step)], wsem.at[wslot]))
        return copies

    def row_copy(tile, r, buf_slot):
        src_row = src_ref[tile * tme + r]
        return pltpu.make_async_copy(
            h_hbm.at[pl.ds(pl.multiple_of(src_row * PACKED_SPLIT, PACKED_SPLIT), PACKED_SPLIT)],
            xbuf.at[buf_slot, pl.ds(pl.multiple_of(r * PACKED_SPLIT, PACKED_SPLIT), PACKED_SPLIT)],
            sem.at[buf_slot])

    def fetch(tile, buf_slot):
        def body(r, carry):
            row_copy(tile, r, buf_slot).start()
            return carry
        lax.fori_loop(0, tme, body, 0, unroll=DMA_UNROLL)

    @pl.when(i == 0)
    def _():
        fetch(0, 0)
        for cp in weight_copies(te_ref[0], ws_ref[0]):
            cp.start()

    @pl.when(i + 1 < nu_ref[0])
    def _():
        fetch(i + 1, 1 - slot)

    @pl.when(i < nu_ref[0])
    def _():
        prev = te_ref[jnp.maximum(i - 1, 0)]

        @pl.when((i == 0) | (te_ref[i] != prev))
        def _():
            w = ws_ref[i]
            for cp in weight_copies(te_ref[i], w):
                cp.wait()
            wg_b[...] = wg_f[w].astype(BF16)
            wu_b[...] = wu_f[w].astype(BF16)
            wd_b[...] = wd_f[w].astype(BF16)

            @pl.when(nx_ref[i] >= 0)
            def _():
                for cp in weight_copies(nx_ref[i], 1 - w):
                    cp.start()

        pltpu.make_async_copy(h_hbm.at[pl.ds(0, tme * PACKED_SPLIT)], xbuf.at[slot], sem.at[slot]).wait()

        x = _load_packed(xbuf.at[slot], tme)
        a = _dot(x, wg_b[...])
        u = _dot(x, wu_b[...])
        hid = a * jax.nn.sigmoid(a) * u
        _store_split(o_ref, _dot(hid.astype(BF16), wd_b[...]), tme)

    @pl.when(i >= nu_ref[0])
    def _():
        o_ref[...] = jnp.zeros(o_ref.shape, F32)


def _experts(h, src, tile_expert, n_used, next_expert, weight_slot, w_gate, w_up, w_down, tme):
    p = src.shape[0]
    d = D_MODEL
    hbm = pl.BlockSpec(memory_space=pl.ANY)
    return pl.pallas_call(
        functools.partial(_expert_kernel, tme=tme),
        grid_spec=pltpu.PrefetchScalarGridSpec(
            num_scalar_prefetch=5, grid=(p // tme,),
            in_specs=[hbm, hbm, hbm, hbm],
            out_specs=pl.BlockSpec((tme * SPLIT, LANE), lambda i, *_: (i, 0)),
            scratch_shapes=[pltpu.VMEM((2, tme * PACKED_SPLIT, LANE), jnp.uint32),
                            pltpu.VMEM((2, d, D_EXPERT), F32), pltpu.VMEM((2, d, D_EXPERT), F32),
                            pltpu.VMEM((2, D_EXPERT, d), F32),
                            pltpu.VMEM((d, D_EXPERT), BF16), pltpu.VMEM((d, D_EXPERT), BF16),
                            pltpu.VMEM((D_EXPERT, d), BF16),
                            pltpu.SemaphoreType.DMA((2,)), pltpu.SemaphoreType.DMA((2,))]),
        out_shape=jax.ShapeDtypeStruct((p * SPLIT, LANE), F32),
        compiler_params=_cparams(("arbitrary",)),
        name="moe_experts",
    )(tile_expert, n_used, src, next_expert, weight_slot, h, w_gate, w_up, w_down)


def _combine_kernel(d1_ref, d2_ref, y_ref, info_ref, ys_hbm, g_ref, o_ref, a_buf, b_buf, sem, *, tm):
    i = pl.program_id(0)
    slot = i % 2

    def copy(r, row, buf, buf_slot):
        return pltpu.make_async_copy(ys_hbm.at[pl.ds(pl.multiple_of(row * SPLIT, SPLIT), SPLIT)],
                                     buf.at[buf_slot, pl.ds(pl.multiple_of(r * SPLIT, SPLIT), SPLIT)],
                                     sem.at[buf_slot])

    def fetch(tile, buf_slot):
        def issue(r, carry):
            copy(r, d1_ref[tile * tm + r], a_buf, buf_slot).start()
            copy(r, d2_ref[tile * tm + r], b_buf, buf_slot).start()
            return carry
        lax.fori_loop(0, tm, issue, 0, unroll=DMA_UNROLL)

    @pl.when(i == 0)
    def _():
        fetch(0, 0)

    @pl.when(i + 1 < pl.num_programs(0))
    def _():
        fetch(i + 1, 1 - slot)

    for buf in (a_buf, b_buf):
        pltpu.make_async_copy(ys_hbm.at[pl.ds(0, tm * SPLIT)], buf.at[slot], sem.at[slot]).wait()
    w1 = info_ref[:, 2:3]
    w2 = info_ref[:, 3:4]
    y = y_ref[...] + w1 * _load_split(a_buf.at[slot], tm) + w2 * _load_split(b_buf.at[slot], tm)
    o_ref[...] = y * lax.rsqrt(jnp.mean(y * y, axis=-1, keepdims=True) + EPS) * g_ref[...]


def _combine(y2d, info, ys, d1, d2, g, tm):
    n, d = y2d.shape
    return pl.pallas_call(
        functools.partial(_combine_kernel, tm=tm),
        grid_spec=pltpu.PrefetchScalarGridSpec(
            num_scalar_prefetch=2, grid=(n // tm,),
            in_specs=[pl.BlockSpec((tm, d), lambda i, a, b: (i, 0)),
                      pl.BlockSpec((tm, LANE), lambda i, a, b: (i, 0)),
                      pl.BlockSpec(memory_space=pl.ANY),
                      pl.BlockSpec((1, d), lambda i, a, b: (0, 0))],
            out_specs=pl.BlockSpec((tm, d), lambda i, a, b: (i, 0)),
            scratch_shapes=[pltpu.VMEM((2, tm * SPLIT, LANE), F32), pltpu.VMEM((2, tm * SPLIT, LANE), F32),
                            pltpu.SemaphoreType.DMA((2,))]),
        out_shape=jax.ShapeDtypeStruct((n, d), F32),
        compiler_params=_cparams(("arbitrary",)),
        name="moe_combine",
    )(d1, d2, y2d, info, ys, g)


def _moe_final(y2d, ffn_g, w_rg, b_rg, w_re, b_re, w_gate, w_up, w_down, final_g, tm, tme):
    n, d = y2d.shape
    h, info, cnt = _route(y2d, ffn_g, w_rg, b_rg, w_re, b_re, tm)
    e1 = info[:, 0].astype(jnp.int32)
    e2 = info[:, 1].astype(jnp.int32)
    counts = cnt[0, ROUTE_COL:ROUTE_COL + N_EXPERTS].astype(jnp.int32)
    padded = (counts + tme - 1) // tme * tme
    ends = jnp.cumsum(padded)
    offs = ends - padded
    d1 = offs[e1] + info[:, 4].astype(jnp.int32)
    d2 = offs[e2] + info[:, 5].astype(jnp.int32)
    p = 2 * n + N_EXPERTS * tme
    tok = jnp.arange(n, dtype=jnp.int32)
    src = jnp.zeros((p,), jnp.int32).at[jnp.concatenate([d1, d2])].set(jnp.concatenate([tok, tok]))
    n_tiles = p // tme
    n_used = (ends[-1] // tme).astype(jnp.int32).reshape(1)
    tile_start = jnp.arange(n_tiles, dtype=jnp.int32) * tme
    tile_expert = jnp.sum((ends[None, :] <= tile_start[:, None]).astype(jnp.int32), axis=1)
    last = jnp.sum((ends <= ends[-1] - 1).astype(jnp.int32))
    tile_expert = jnp.minimum(tile_expert, last).astype(jnp.int32)
    run = jnp.cumsum(jnp.concatenate([jnp.zeros((1,), jnp.int32),
                                      (tile_expert[1:] != tile_expert[:-1]).astype(jnp.int32)]))
    weight_slot = (run % 2).astype(jnp.int32)
    after = ends[tile_expert] // tme
    next_expert = jnp.where(after < n_used[0], tile_expert[jnp.minimum(after, n_tiles - 1)], -1).astype(jnp.int32)
    ys = _experts(h, src, tile_expert, n_used, next_expert, weight_slot, w_gate, w_up, w_down, tme)
    return _combine(y2d, info, ys, d1, d2, final_g, tm)


GROUP_ROWS = 8
SLC_LANES = 384


def _sample_cattn_kernel(q_ref, kc_ref, vc_ref, b_ref, ov_ref, o_ref, sel_ref, *, n_cmp, n_slc):
    q = (q_ref[0, 0] * (NSA_HEAD_DIM ** -0.5)).astype(BF16)
    s = _dot_nt(q, kc_ref[0, 0].astype(BF16)) + b_ref[0]
    valid = lax.broadcasted_iota(jnp.int32, s.shape, 1) < n_cmp
    s = jnp.where(valid, s, NEG_BIG)
    e = jnp.exp(s - jnp.max(s, axis=1, keepdims=True))
    p = jnp.where(valid, e / jnp.sum(e, axis=1, keepdims=True), 0.0)
    o_ref[0, 0] = _dot(p.astype(BF16), vc_ref[0, 0].astype(BF16))
    pg = jnp.where(lax.broadcasted_iota(jnp.int32, p.shape, 0) < NSA_GROUP, p, 0.0)
    pc = jnp.broadcast_to(jnp.sum(pg, axis=0, keepdims=True), pg.shape)
    score = _dot(pc.astype(BF16), ov_ref[...])[0:1, :]
    blk = lax.broadcasted_iota(jnp.int32, score.shape, 1)
    cur = n_slc - 1
    forced = (blk == 0) | (blk == cur) | (blk == cur - 1)
    score = jnp.where(forced, FORCE_SCORE, score)
    score = jnp.where(blk > cur, -jnp.inf, score)
    ri = lax.broadcasted_iota(jnp.int32, (SLC_LANES, SLC_LANES), 0)
    ci = lax.broadcasted_iota(jnp.int32, (SLC_LANES, SLC_LANES), 1)
    sb = jnp.broadcast_to(score, (SLC_LANES, SLC_LANES))
    col = jnp.sum(jnp.where(ri == ci, sb, 0.0), axis=1, keepdims=True)
    beats = (col > sb) | ((col == sb) & (ri < ci))
    rank = jnp.sum(beats.astype(jnp.int32), axis=0, keepdims=True)
    sel_ref[0, 0] = jnp.broadcast_to((rank < SLC_TOPN).astype(F32), (GROUP_ROWS, SLC_LANES))


def _sample_cattn(q8, kvc, bias_cs, n_cmp, n_slc):
    bsz, _, nch, _ = kvc.shape
    c0 = np.arange(nch) * CMP_STRIDE
    s0 = np.arange(SLC_LANES) * SLC_BLOCK
    ov = np.minimum(c0[:, None] + CMP_BLOCK, s0[None, :] + SLC_BLOCK) - np.maximum(c0[:, None], s0[None, :])
    ov = np.clip(ov, 0, None).astype(np.float32) / CMP_BLOCK
    ov[n_cmp:, :] = 0.0
    ov[:, n_slc:] = 0.0
    return pl.pallas_call(
        functools.partial(_sample_cattn_kernel, n_cmp=n_cmp, n_slc=n_slc),
        grid=(bsz, NSA_KV_HEADS),
        in_specs=[pl.BlockSpec((1, 1, GROUP_ROWS, NSA_HEAD_DIM), lambda b, h: (b, h, 0, 0)),
                  pl.BlockSpec((1, 1, nch, NSA_HEAD_DIM), lambda b, h: (b, h, 0, 0)),
                  pl.BlockSpec((1, 1, nch, NSA_HEAD_DIM), lambda b, h: (b, NSA_KV_HEADS + h, 0, 0)),
                  pl.BlockSpec((1, GROUP_ROWS, nch), lambda b, h: (h, 0, 0)),
                  pl.BlockSpec((nch, SLC_LANES), lambda b, h: (0, 0))],
        out_specs=[pl.BlockSpec((1, 1, GROUP_ROWS, NSA_HEAD_DIM), lambda b, h: (b, h, 0, 0)),
                   pl.BlockSpec((1, 1, GROUP_ROWS, SLC_LANES), lambda b, h: (b, h, 0, 0))],
        out_shape=[jax.ShapeDtypeStruct((bsz, NSA_KV_HEADS, GROUP_ROWS, NSA_HEAD_DIM), F32),
                   jax.ShapeDtypeStruct((bsz, NSA_KV_HEADS, GROUP_ROWS, SLC_LANES), F32)],
        compiler_params=_cparams(("arbitrary", "arbitrary")),
        name="sample_cmp_attn",
    )(q8, kvc, kvc, bias_cs, jnp.asarray(ov, BF16))


def _sample_attn_kernel(pg_ref, hf_ref, q_ref, cache_hbm, snew_ref, bs_ref, win_ref, wnew_ref, bw_ref,
                        os_ref, ow_ref, kt_buf, vt_buf, sem, *, n_gather, wlen):
    b = pl.program_id(0)
    rows_per_page = 2 * NSA_KV_HEADS * NSA_HEAD_DIM

    def page_copy(h, s, c, buf):
        page = pg_ref[(b * NSA_KV_HEADS + h) * n_gather + s]
        start = pl.multiple_of(page * rows_per_page + (c * NSA_KV_HEADS + h) * NSA_HEAD_DIM, NSA_HEAD_DIM)
        return pltpu.make_async_copy(cache_hbm.at[pl.ds(start, NSA_HEAD_DIM)],
                                     buf.at[h, :, pl.ds(s * PAGE_SIZE, PAGE_SIZE)], sem)

    for h in range(NSA_KV_HEADS):
        for s in range(n_gather):
            page_copy(h, s, 0, kt_buf).start()
            page_copy(h, s, 1, vt_buf).start()

    def attend(q, kts, vts, bias, valid):
        s = jnp.concatenate([_dot(q, kt.astype(BF16)) for kt in kts], axis=1) + bias
        s = jnp.where(valid, s, NEG_BIG)
        e = jnp.exp(s - jnp.max(s, axis=1, keepdims=True))
        p = jnp.where(valid, e / jnp.sum(e, axis=1, keepdims=True), 0.0).astype(BF16)
        out = None
        lo = 0
        for vt in vts:
            n = vt.shape[1]
            part = _dot_nt(p[:, lo:lo + n], vt.astype(BF16))
            out = part if out is None else out + part
            lo += n
        return out

    def head_rows(ref, c, h):
        r0 = (c * NSA_KV_HEADS + h) * NSA_HEAD_DIM
        return ref[0, r0:r0 + NSA_HEAD_DIM, :]

    widx = lax.broadcasted_iota(jnp.int32, (GROUP_ROWS, wlen + PAGE_SIZE), 1)
    wvalid = (widx >= wlen + 1 - WINDOW) & (widx <= wlen)
    for h in range(NSA_KV_HEADS):
        q = (q_ref[0, h] * (NSA_HEAD_DIM ** -0.5)).astype(BF16)
        ow_ref[0, h] = attend(q, [head_rows(win_ref, 0, h), head_rows(wnew_ref, 0, h)],
                              [head_rows(win_ref, 1, h), head_rows(wnew_ref, 1, h)], bw_ref[h], wvalid)

    for h in range(NSA_KV_HEADS):
        for s in range(n_gather):
            page_copy(h, s, 0, kt_buf).wait()
            page_copy(h, s, 1, vt_buf).wait()

    keys = n_gather * PAGE_SIZE
    lane = lax.broadcasted_iota(jnp.int32, (GROUP_ROWS, keys + PAGE_SIZE), 1)
    slot = lane >> 7
    lane_half = (lane >> 6) & 1
    for h in range(NSA_KV_HEADS):
        want = jnp.full(lane.shape, -1, jnp.int32)
        for s in range(n_gather):
            want = jnp.where(slot == s, hf_ref[(b * NSA_KV_HEADS + h) * n_gather + s], want)
        svalid = (lane_half == want) | (lane == keys)
        q = (q_ref[0, h] * (NSA_HEAD_DIM ** -0.5)).astype(BF16)
        os_ref[0, h] = attend(q, [kt_buf[h], head_rows(snew_ref, 0, h)],
                              [vt_buf[h], head_rows(snew_ref, 1, h)], bs_ref[0, h], svalid)


def _sample_attn(pages, halves, q8, cache_t, snew_t, bias_sel, win_t, wnew_t, bias_w, n_gather):
    bsz = q8.shape[0]
    wlen = win_t.shape[2]
    keys = n_gather * PAGE_SIZE
    rows = 2 * NSA_KV_HEADS * NSA_HEAD_DIM
    qspec = pl.BlockSpec((1, NSA_KV_HEADS, GROUP_ROWS, NSA_HEAD_DIM), lambda b, pg, hf: (b, 0, 0, 0))
    newspec = pl.BlockSpec((1, rows, PAGE_SIZE), lambda b, pg, hf: (b, 0, 0))
    return pl.pallas_call(
        functools.partial(_sample_attn_kernel, n_gather=n_gather, wlen=wlen),
        grid_spec=pltpu.PrefetchScalarGridSpec(
            num_scalar_prefetch=2, grid=(bsz,),
            in_specs=[qspec,
                      pl.BlockSpec(memory_space=pl.ANY),
                      newspec,
                      pl.BlockSpec((1, NSA_KV_HEADS, GROUP_ROWS, keys + PAGE_SIZE), lambda b, pg, hf: (b, 0, 0, 0)),
                      pl.BlockSpec((1, rows, wlen), lambda b, pg, hf: (b, 0, 0)),
                      newspec,
                      pl.BlockSpec((NSA_KV_HEADS, GROUP_ROWS, wlen + PAGE_SIZE), lambda b, pg, hf: (0, 0, 0))],
            out_specs=[qspec, qspec],
            scratch_shapes=[pltpu.VMEM((NSA_KV_HEADS, NSA_HEAD_DIM, keys), F32),
                            pltpu.VMEM((NSA_KV_HEADS, NSA_HEAD_DIM, keys), F32),
                            pltpu.SemaphoreType.DMA(())]),
        out_shape=[jax.ShapeDtypeStruct(q8.shape, F32), jax.ShapeDtypeStruct(q8.shape, F32)],
        compiler_params=_cparams(("arbitrary",)),
        name="sample_slc_win_attn",
    )(pages, halves, q8, cache_t, snew_t, bias_sel, win_t, wnew_t, bias_w)


PAGES_PER_STEP = 16


def _paged_pq_kernel(pt_ref, cache_hbm, pos_ref, w_ref, o_ref, pbuf, tok_scr, sem):
    b = pl.program_id(0)
    g = pl.program_id(1)
    ng = pl.num_programs(1)
    lin = b * ng + g
    slot = lin % 2
    rows_per_page = 2 * NSA_KV_HEADS * NSA_HEAD_DIM
    chunks = PAGES_PER_STEP * (PAGE_SIZE // CMP_STRIDE)

    def page_copy(step, p, buf_slot):
        sb = step // ng
        sg = step - sb * ng
        page = pt_ref[sb, sg * PAGES_PER_STEP + p]
        return pltpu.make_async_copy(cache_hbm.at[pl.ds(pl.multiple_of(page * rows_per_page, rows_per_page),
                                                        rows_per_page)],
                                     pbuf.at[buf_slot, p], sem.at[buf_slot])

    def fetch(step, buf_slot):
        for p in range(PAGES_PER_STEP):
            page_copy(step, p, buf_slot).start()

    @pl.when(lin == 0)
    def _():
        fetch(0, 0)

    @pl.when(lin + 1 < pl.num_programs(0) * ng)
    def _():
        fetch(lin + 1, 1 - slot)

    for p in range(PAGES_PER_STEP):
        page_copy(lin, p, slot).wait()

    for c in range(2):
        head_chunks = []
        for pair in (2 * c, 2 * c + 1):
            for p in range(PAGES_PER_STEP):
                tok_scr[p * PAGE_SIZE:(p + 1) * PAGE_SIZE, :] = pbuf[slot, p, pair * LANE:(pair + 1) * LANE, :].T
            toks = [tok_scr[pl.ds(l, chunks, stride=CMP_STRIDE), :] for l in range(CMP_STRIDE)]
            for half in range(2):
                lanes = slice(half * NSA_HEAD_DIM, (half + 1) * NSA_HEAD_DIM)
                head_chunks.append(jnp.concatenate([t[:, lanes] for t in toks], axis=1))
        stacked = jnp.concatenate(head_chunks, axis=0)
        for part in range(2):
            lhs = (stacked + pos_ref[c, part:part + 1, :]).astype(BF16)
            out = _dot(lhs, w_ref[c, part])
            for hh in range(NSA_KV_HEADS):
                o_ref[0, c * NSA_KV_HEADS + hh, :, part * CMP_HIDDEN:(part + 1) * CMP_HIDDEN] = (
                    out[hh * chunks:(hh + 1) * chunks, :])


def _paged_pq(cache_t, page_table, cmp_pos, cmp_w1):
    db, n_pages = page_table.shape
    kdim = CMP_STRIDE * NSA_HEAD_DIM
    nch = n_pages * (PAGE_SIZE // CMP_STRIDE)
    tn = PAGES_PER_STEP * (PAGE_SIZE // CMP_STRIDE)
    rows_per_page = 2 * NSA_KV_HEADS * NSA_HEAD_DIM
    return pl.pallas_call(
        _paged_pq_kernel,
        grid_spec=pltpu.PrefetchScalarGridSpec(
            num_scalar_prefetch=1, grid=(db, n_pages // PAGES_PER_STEP),
            in_specs=[pl.BlockSpec(memory_space=pl.ANY),
                      pl.BlockSpec((2, 2, kdim), lambda b, g, pt: (0, 0, 0)),
                      pl.BlockSpec((2, 2, kdim, CMP_HIDDEN), lambda b, g, pt: (0, 0, 0, 0))],
            out_specs=pl.BlockSpec((1, 8, tn, 2 * CMP_HIDDEN), lambda b, g, pt: (b, 0, g, 0)),
            scratch_shapes=[pltpu.VMEM((2, PAGES_PER_STEP, rows_per_page, PAGE_SIZE), F32),
                            pltpu.VMEM((PAGES_PER_STEP * PAGE_SIZE, LANE), F32),
                            pltpu.SemaphoreType.DMA((2,))]),
        out_shape=jax.ShapeDtypeStruct((db, 8, nch, 2 * CMP_HIDDEN), F32),
        compiler_params=_cparams(("arbitrary", "arbitrary")),
        name="cmp_pq_paged",
    )(page_table, cache_t, cmp_pos.reshape(2, 2, kdim), cmp_w1.reshape(2, 2, kdim, CMP_HIDDEN).astype(BF16))


def _pad_rows(a, n):
    return jnp.concatenate([a, jnp.zeros((n - a.shape[0],) + a.shape[1:], a.dtype)], axis=0)


def _sample_mixer(x_sample, cache_cmp, cache_slc, cache_win, st_c, st_n, st_m, st_conv, page_table,
                  norm_g, wb, gate_bias, conv_w, ml_norm_g, cmp_pos, cmp_w1, cmp_w2, nsa_norm_g, w_out_b, rel_bias):
    db = x_sample.shape[0]
    n_pages = page_table.shape[1]
    past = n_pages * PAGE_SIZE
    tok = SAMPLE_TOKEN_ROWS
    x16 = _pad_rows(x_sample.reshape(db, D_MODEL), tok)
    u = _proj(x16, norm_g, wb, tok, PROJ_COLS)
    small = u[:, COL_SMALL:COL_SMALL + LANE]

    T = SAMPLE_SEQ_ROWS
    useq = jnp.zeros((db, T, 4 * D_ML), F32)
    useq = useq.at[:, T - CONV_W:T - 1, 0:2 * D_ML].set(st_conv)
    useq = useq.at[:, T - 1, :].set(u[:db, 0:4 * D_ML])
    sseq = jnp.zeros((db, T, LANE), F32).at[:, :T - 1, LANE - 1].set(1.0)
    sseq = sseq.at[:, T - 1, :].set(small[:db])
    y_seq, conv_n, c_n, n_n, m_n = _mlstm(
        useq.reshape(db * T, 4 * D_ML), sseq.reshape(db * T, LANE), jnp.zeros((db, CONV_W - 1, 2 * D_ML), F32),
        st_c, st_n, st_m, conv_w, gate_bias, ml_norm_g, db, T, T)
    y_ml = y_seq.reshape(db, T, D_ML)[:, T - 1]

    n_pool = cache_cmp.shape[0]
    cmp_t = cache_cmp.transpose(0, 2, 3, 4, 1).reshape(n_pool * 2 * D_KV, PAGE_SIZE)
    kvc = _cmp_hid(_paged_pq(cmp_t, page_table, cmp_pos, cmp_w1), cmp_w2)
    nch = past // CMP_STRIDE
    n_cmp = (past + 1) // CMP_STRIDE - CMP_BLOCK // CMP_STRIDE + 1
    n_slc = -(-(past + 1) // SLC_BLOCK)
    q = u[:db, COL_QNSA:COL_QNSA + D_NSA].reshape(db, NSA_KV_HEADS, NSA_GROUP, NSA_HEAD_DIM)
    q8 = jnp.concatenate([q, jnp.zeros_like(q)], axis=2)
    bd = _bias_by_distance(rel_bias, past + 1).reshape(NSA_KV_HEADS, NSA_GROUP, past + 1)
    pad_g = lambda a: jnp.concatenate([a, jnp.zeros_like(a)], axis=1)
    dist_c = np.clip(past - (np.arange(nch) * CMP_STRIDE + CMP_BLOCK - 1), 0, None)
    o_c8, sel = _sample_cattn(q8, kvc, pad_g(bd[:, :, dist_c]), n_cmp, n_slc)

    mask = sel[:, :, 0, :n_slc] > 0.5
    idx = jnp.sort(jnp.where(mask, jnp.arange(n_slc, dtype=jnp.int32), jnp.int32(1 << 20)), axis=-1)
    n_gather = SLC_TOPN - 1
    idx = idx[..., :n_gather]
    pages_per_block = PAGE_SIZE // SLC_BLOCK
    logical_page = idx // pages_per_block
    pages = jnp.take_along_axis(page_table[:, None, :], logical_page, axis=2).reshape(-1).astype(jnp.int32)
    halves = (idx % pages_per_block).reshape(-1).astype(jnp.int32)
    kpos = (logical_page[..., None] * PAGE_SIZE + jnp.arange(PAGE_SIZE, dtype=jnp.int32)).reshape(db, NSA_KV_HEADS, -1)
    hh = jnp.arange(NSA_KV_HEADS)[None, :, None, None]
    gg = jnp.arange(NSA_GROUP)[None, None, :, None]
    bias_sel = bd[hh, gg, (past - kpos)[:, :, None, :]]
    bias_sel = jnp.concatenate([bias_sel, jnp.broadcast_to(bd[None, :, :, 0:1], (db, NSA_KV_HEADS, NSA_GROUP, 1)),
                                jnp.zeros((db, NSA_KV_HEADS, NSA_GROUP, PAGE_SIZE - 1), F32)], axis=-1)
    bias_sel = jnp.concatenate([bias_sel, jnp.zeros_like(bias_sel)], axis=2)
    wlen = cache_win.shape[1]
    dist_w = np.clip(wlen - np.arange(wlen + PAGE_SIZE), 0, None)
    kvs_new = _kv_from_per_head(u[:db, COL_KVS:COL_KVS + 2 * D_KV])
    kvw_new = _kv_from_per_head(u[:db, COL_KVW:COL_KVW + 2 * D_KV])
    lane_pad = lambda a: jnp.pad(a[:, :, None], ((0, 0), (0, 0), (0, PAGE_SIZE - 1)))
    slc_t = cache_slc.transpose(0, 2, 3, 4, 1).reshape(n_pool * 2 * D_KV, PAGE_SIZE)
    win_t = cache_win.transpose(0, 2, 3, 4, 1).reshape(db, 2 * D_KV, wlen)
    o_s8, o_w8 = _sample_attn(pages, halves, q8, slc_t, lane_pad(kvs_new), bias_sel, win_t, lane_pad(kvw_new),
                              pad_g(bd[:, :, dist_w]), n_gather)
    win2d = cache_win.reshape(db, wlen, 2 * D_KV)

    heads = lambda o: _pad_rows(o[:, :, :NSA_GROUP, :].reshape(db, D_NSA), tok)
    y = _outproj(_pad_rows(y_ml, tok), heads(o_c8), heads(o_s8), heads(o_w8), small, nsa_norm_g, w_out_b, x16, tok,
                 PROJ_COLS)
    kvshape = (1, db, 1, 2, NSA_KV_HEADS, NSA_HEAD_DIM)
    new_win = jnp.concatenate([win2d[:, 1:], kvw_new[:, None, :]], axis=1)
    states = (u[:db, COL_KVC:COL_KVC + 2 * D_KV].reshape(kvshape), kvs_new.reshape(kvshape),
              new_win.reshape((1, db, wlen, 2, NSA_KV_HEADS, NSA_HEAD_DIM)),
              c_n[None], n_n[None], m_n[None], conv_n[None])
    return y, states


def kernel(x_prompt, x_sample, cache_cmp_kv, cache_slc_kv, cache_win_kv, state_mlstm_C, state_mlstm_n,
           state_mlstm_m, state_conv, page_table, rel_bias, norm_mix_g, w_in, b_ig, b_fg, conv_w, ml_norm_g,
           cmp_pos, cmp_w1, cmp_w2, nsa_norm_g, w_out, norm_ffn_g, w_router_grp, b_router_grp, w_router_exp,
           b_router_exp, w_gate, w_up, w_down, norm_final_g):
    B, S, D = x_prompt.shape
    wb = _reorder_w_in(w_in[0])
    gate_bias = jnp.zeros((1, LANE), F32).at[0, 0:ML_HEADS].set(b_ig[0]).at[0, ML_HEADS:2 * ML_HEADS].set(b_fg[0])
    w_out_b = w_out[0].astype(BF16)
    yp, st_p = _prompt_mixer(x_prompt, norm_mix_g, wb, gate_bias, conv_w[0], ml_norm_g, cmp_pos[0], cmp_w1[0],
                             cmp_w2[0], nsa_norm_g, w_out_b, rel_bias)
    ys, st_s = _sample_mixer(x_sample, cache_cmp_kv[0], cache_slc_kv[0], cache_win_kv[0], state_mlstm_C[0],
                             state_mlstm_n[0], state_mlstm_m[0], state_conv[0], page_table, norm_mix_g, wb, gate_bias,
                             conv_w[0], ml_norm_g, cmp_pos[0], cmp_w1[0], cmp_w2[0], nsa_norm_g, w_out_b, rel_bias)
    moe_w = (norm_ffn_g, w_router_grp[0], b_router_grp[0], w_router_exp[0], b_router_exp[0],
             w_gate[0], w_up[0], w_down[0], norm_final_g[None, :])
    DB, L, _ = x_sample.shape
    out_p = _moe_final(yp.reshape(B * S, D), *moe_w, MOE_TOKEN_ROWS, MOE_EXPERT_ROWS).reshape(B, S, D)
    out_s = _moe_final(_pad_rows(ys, SAMPLE_MOE_ROWS), *moe_w, SAMPLE_MOE_ROWS, SAMPLE_EXPERT_ROWS)[:DB]
    out_s = out_s.reshape(DB, L, D)
    outs = [out_p, out_s]
    for a, b in zip(st_p, st_s):
        outs += [a, b]
    return tuple(outs)


def _prompt_mixer(x_prompt, norm_g, wb, gate_bias, conv_w, ml_norm_g, cmp_pos, cmp_w1, cmp_w2, nsa_norm_g,
                  w_out_b, rel_bias):
    B, S, D = x_prompt.shape
    x2d = x_prompt.reshape(B * S, D)
    tm = min(PROJ_ROWS, S)
    u, kvt_c, kvt_s, kvt_w = _proj(x2d, norm_g, wb, tm, PROJ_COLS, seq=S)
    small = u[:, COL_SMALL:COL_SMALL + LANE]
    y_ml, conv_n, c_n, n_n, m_n = _mlstm(
        u, small, jnp.zeros((B, CONV_W - 1, 2 * D_ML), F32),
        jnp.zeros((B, ML_HEADS, ML_HEAD_DIM, ML_HEAD_DIM), F32), jnp.zeros((B, ML_HEADS, ML_HEAD_DIM), F32),
        jnp.full((B, ML_HEADS), -jnp.inf, F32), conv_w, gate_bias, ml_norm_g, B, S, ML_CHUNK_ROWS)
    kvc = _compress(u, COL_KVC // (2 * D_KV), B, S, cmp_pos, cmp_w1, cmp_w2, min(S, CMP_TOKENS))
    o_c, sel_t = _cattn(u, kvc, rel_bias, B, S, min(S, CMP_QUERY_ROWS))
    o_s = _flash(u, COL_KVS, rel_bias, sel_t, B, S)
    o_w = _flash(u, COL_KVW, rel_bias, None, B, S)
    y = _outproj(y_ml, o_c, o_s, o_w, small, nsa_norm_g, w_out_b, x2d, tm, PROJ_COLS)
    win = min(WINDOW, S)
    rows = lambda kvt: kvt.transpose(0, 4, 1, 2, 3)[None]
    states = (rows(kvt_c), rows(kvt_s), rows(kvt_w[..., S - win:]),
              c_n[None], n_n[None], m_n[None], conv_n[None])
    return y.reshape(B, S, D), states
```

```python
import functools
import math

import numpy as np
import jax
import jax.numpy as jnp
from jax import lax
from jax.experimental import pallas as pl
from jax.experimental.pallas import tpu as pltpu

F32 = jnp.float32
BF16 = jnp.bfloat16

D_MODEL = 2048
ML_HEADS = 4
ML_HEAD_DIM = 256
D_ML = 1024
CONV_W = 4
NSA_HEADS = 16
NSA_HEAD_DIM = 64
D_NSA = 1024
NSA_KV_HEADS = 4
NSA_GROUP = 4
D_KV = 256
CMP_BLOCK = 32
CMP_STRIDE = 16
CMP_HIDDEN = 256
SLC_BLOCK = 64
SLC_TOPN = 16
WINDOW = 512
N_BUCKETS = 32
MAX_DISTANCE = 2048
N_GROUPS = 4
EXPERTS_PER_GROUP = 8
N_EXPERTS = 32
D_EXPERT = 512
PAGE_SIZE = 128
EPS = 1e-6
NEG_BIG = -1e30
FORCE_SCORE = 1e4

LANE = 128
COL_QML, COL_KML, COL_VML, COL_OML = 0, 1024, 2048, 3072
COL_QNSA = 4096
COL_KVC, COL_KVS, COL_KVW = 5120, 5632, 6144
COL_SMALL = 6656
N_PROJ = 7168
VMEM_LIMIT = 56 * 1024 * 1024

PROJ_ROWS = 1024
PROJ_COLS = 2 * D_KV
ML_CHUNK_ROWS = 256
CMP_TOKENS = 4096
CMP_QUERY_ROWS = 256
MOE_TOKEN_ROWS = 512
MOE_EXPERT_ROWS = 128
SAMPLE_TOKEN_ROWS = 16
SAMPLE_SEQ_ROWS = 128
SAMPLE_MOE_ROWS = 128
SAMPLE_EXPERT_ROWS = 16


def _cparams(sem, vmem=VMEM_LIMIT):
    return pltpu.CompilerParams(dimension_semantics=sem, vmem_limit_bytes=vmem)


def _split2(x):
    hi = x.astype(BF16)
    lo = (x - hi.astype(F32)).astype(BF16)
    return hi, lo


def _split3(x):
    hi = x.astype(BF16)
    r = x - hi.astype(F32)
    mid = r.astype(BF16)
    lo = (r - mid.astype(F32)).astype(BF16)
    return hi, mid, lo


def _dot(a, b):
    return jnp.dot(a, b, preferred_element_type=F32)


def _dot_nt(a, b):
    return lax.dot_general(a, b, (((1,), (1,)), ((), ())), preferred_element_type=F32)


def _dot_tn(a, b):
    return lax.dot_general(a, b, (((0,), (0,)), ((), ())), preferred_element_type=F32)


def _proj_kernel(x_ref, g_ref, w_ref, o_ref, *rest, kv_blocks):
    h_scr = rest[-1]
    j = pl.program_id(1)

    @pl.when(j == 0)
    def _():
        x = x_ref[...]
        ms = jnp.mean(x * x, axis=-1, keepdims=True)
        h_scr[...] = (x * lax.rsqrt(ms + EPS) * g_ref[...]).astype(BF16)

    res = _dot(h_scr[...], w_ref[...])
    o_ref[...] = res

    for (block, per_head), t_ref in zip(kv_blocks, rest[:-1]):
        @pl.when(j == block)
        def _():
            res_t = res.T
            for c in range(2):
                for hh in range(NSA_KV_HEADS):
                    piece = hh * 2 + c if per_head else c * NSA_KV_HEADS + hh
                    t_ref[0, c, hh] = res_t[piece * NSA_HEAD_DIM:(piece + 1) * NSA_HEAD_DIM, :]


def _proj(x2d, g, wb, tm, tn, seq=None):
    n, d = x2d.shape
    nc = wb.shape[1]
    kv_blocks = ()
    out_specs = [pl.BlockSpec((tm, tn), lambda i, j: (i, j))]
    out_shape = [jax.ShapeDtypeStruct((n, nc), F32)]
    if seq is not None:
        assert tn == 2 * D_KV and seq % tm == 0
        kv_blocks = ((COL_KVC // tn, False), (COL_KVS // tn, True), (COL_KVW // tn, True))
        per_seq = seq // tm
        for _ in kv_blocks:
            out_specs.append(pl.BlockSpec((1, 2, NSA_KV_HEADS, NSA_HEAD_DIM, tm),
                                          lambda i, j: (i // per_seq, 0, 0, 0, i % per_seq)))
            out_shape.append(jax.ShapeDtypeStruct((n // seq, 2, NSA_KV_HEADS, NSA_HEAD_DIM, seq), F32))
    outs = pl.pallas_call(
        functools.partial(_proj_kernel, kv_blocks=kv_blocks),
        grid=(n // tm, nc // tn),
        in_specs=[pl.BlockSpec((tm, d), lambda i, j: (i, 0)),
                  pl.BlockSpec((1, d), lambda i, j: (0, 0)),
                  pl.BlockSpec((d, tn), lambda i, j: (0, j))],
        out_specs=out_specs,
        out_shape=out_shape,
        scratch_shapes=[pltpu.VMEM((tm, d), BF16)],
        compiler_params=_cparams(("arbitrary", "arbitrary")),
        name="proj",
    )(x2d, g, wb)
    return outs if seq is not None else outs[0]


def _reorder_w_in(w_in):
    w_in = w_in.astype(BF16)
    big = w_in[:, :4 * D_ML]
    small_a = w_in[:, 4 * D_ML:4 * D_ML + 2 * ML_HEADS]
    rest = w_in[:, 4 * D_ML + 2 * ML_HEADS:]
    q_and_cmp = rest[:, :D_NSA + 2 * D_KV]
    gate = rest[:, D_NSA + 6 * D_KV:]
    d = w_in.shape[0]

    def per_head(w):
        return w.reshape(d, 2, NSA_KV_HEADS, NSA_HEAD_DIM).transpose(0, 2, 1, 3).reshape(d, 2 * D_KV)

    kv_s = per_head(rest[:, D_NSA + 2 * D_KV:D_NSA + 4 * D_KV])
    kv_w = per_head(rest[:, D_NSA + 4 * D_KV:D_NSA + 6 * D_KV])
    pad = jnp.zeros((d, N_PROJ - COL_SMALL - 2 * ML_HEADS - 3 * NSA_HEADS), w_in.dtype)
    return jnp.concatenate([big, q_and_cmp, kv_s, kv_w, small_a, gate, pad], axis=1).astype(BF16)


def _kv_from_per_head(kv2d):
    n = kv2d.shape[0]
    return kv2d.reshape(n, NSA_KV_HEADS, 2, NSA_HEAD_DIM).transpose(0, 2, 1, 3).reshape(n, 2 * D_KV)


def _log_sigmoid(x):
    return jnp.minimum(x, 0.0) - jnp.log1p(jnp.exp(-jnp.abs(x)))


def _mlstm_kernel(q_ref, k_ref, v_ref, o_ref, s_ref, cb_ref, c0_ref, n0_ref, m0_ref,
                  cw_ref, gb_ref, ng_ref,
                  y_ref, cbo_ref, co_ref, no_ref, mo_ref,
                  ext_scr, c_scr, n_scr, m_scr, *, T):
    c = pl.program_id(1)
    nc = pl.num_programs(1)

    @pl.when(c == 0)
    def _():
        ext_scr[0:8, :] = jnp.zeros((8, 2 * D_ML), F32)
        ext_scr[5:8, :] = cb_ref[0]
        c_scr[...] = c0_ref[0]
        n_scr[...] = n0_ref[0]
        m_scr[...] = m0_ref[0]

    ext_scr[8:8 + T, 0:D_ML] = q_ref[...]
    ext_scr[8:8 + T, D_ML:2 * D_ML] = k_ref[...]
    conv = ext_scr[5:5 + T, :] * cw_ref[0:1, :]
    for j in range(1, CONV_W):
        conv = conv + ext_scr[5 + j:5 + j + T, :] * cw_ref[j:j + 1, :]
    tail = ext_scr[8 + T - 3:8 + T, :]
    ext_scr[5:8, :] = tail
    cbo_ref[0] = tail
    qk = conv * jax.nn.sigmoid(conv)

    pre = s_ref[...] + gb_ref[...]
    col = lax.broadcasted_iota(jnp.int32, pre.shape, 1)
    padrow = s_ref[:, LANE - 1:LANE] > 0.5
    gates = jnp.where(col < ML_HEADS, pre, _log_sigmoid(pre))
    gates = jnp.where(padrow, jnp.where(col < ML_HEADS, NEG_BIG, 0.0), gates)
    g_r = gates.T
    ti = lax.broadcasted_iota(jnp.int32, (T, T), 0)
    si = lax.broadcasted_iota(jnp.int32, (T, T), 1)
    upper = (ti <= si).astype(BF16)
    g_fin = jnp.where(lax.broadcasted_iota(jnp.int32, g_r.shape, 0) < ML_HEADS, 0.0, g_r)
    hi, mid, lo = _split3(g_fin)
    cum_r = _dot(hi, upper) + _dot(mid, upper) + _dot(lo, upper)
    rowi = lax.broadcasted_iota(jnp.int32, g_r.shape, 0)
    a_r = jnp.where(rowi < ML_HEADS, g_r, cum_r)
    a_c = a_r.T
    causal = si <= ti

    for h in range(ML_HEADS):
        sl = slice(h * ML_HEAD_DIM, (h + 1) * ML_HEAD_DIM)
        q = qk[:, h * ML_HEAD_DIM:(h + 1) * ML_HEAD_DIM]
        k = qk[:, D_ML + h * ML_HEAD_DIM:D_ML + (h + 1) * ML_HEAD_DIM] * (ML_HEAD_DIM ** -0.5)
        v = v_ref[:, sl]
        ig_r = a_r[h:h + 1, :]
        b_r = a_r[ML_HEADS + h:ML_HEADS + h + 1, :]
        ig_c = a_c[:, h:h + 1]
        b_c = a_c[:, ML_HEADS + h:ML_HEADS + h + 1]
        m_prev = m_scr[h:h + 1, 0:1]
        logd = jnp.where(causal, b_c - b_r + ig_r, -jnp.inf)
        inter = b_c + m_prev
        m_t = jnp.maximum(inter, jnp.max(logd, axis=1, keepdims=True))
        w_intra = jnp.exp(logd - m_t)
        w_inter = jnp.exp(inter - m_t)
        qb = q.astype(BF16)
        kb = k.astype(BF16)
        vb = v.astype(BF16)
        sc = _dot_nt(qb, kb) * w_intra
        cmat = c_scr[h]
        nvec = n_scr[h:h + 1, :]
        num = _dot(sc.astype(BF16), vb) + w_inter * _dot(qb, cmat.astype(BF16))
        qn = jnp.sum(qb.astype(F32) * nvec.astype(BF16).astype(F32), axis=1, keepdims=True)
        den = jnp.sum(sc, axis=1, keepdims=True) + w_inter * qn
        hh = num / jnp.maximum(jnp.abs(den), jnp.exp(-m_t))
        m_new = m_t[T - 1:T, :]
        b_last = b_c[T - 1:T, :]
        w_s = jnp.exp(b_last - b_c + ig_c - m_new)
        decay = jnp.exp(b_last + m_prev - m_new)
        kw = k * w_s
        c_new = decay * cmat + _dot_tn(kw.astype(BF16), vb)
        n_new = decay * nvec + jnp.sum(kw, axis=0, keepdims=True)
        c_scr[h] = c_new
        n_scr[h:h + 1, :] = n_new
        m_scr[h:h + 1, :] = jnp.broadcast_to(m_new, (1, LANE))
        hn = hh * lax.rsqrt(jnp.mean(hh * hh, axis=1, keepdims=True) + EPS) * ng_ref[:, sl]
        y_ref[:, sl] = (hn * jax.nn.sigmoid(o_ref[:, sl])).astype(y_ref.dtype)

    @pl.when(c == nc - 1)
    def _():
        co_ref[0] = c_scr[...]
        no_ref[0] = n_scr[...]
        mo_ref[0] = m_scr[...]


def _mlstm(u, small, conv_buf, c0, n0, m0, conv_w, gate_bias, norm_g, batch, seq, T):
    nc = seq // T
    cb = D_ML // 1024
    m0b = jnp.broadcast_to(m0[:, :, None], (batch, ML_HEADS, LANE))
    m0b = jnp.concatenate([m0b, jnp.zeros((batch, 8 - ML_HEADS, LANE), F32)], axis=1)
    n0p = jnp.concatenate([n0, jnp.zeros((batch, 8 - ML_HEADS, ML_HEAD_DIM), F32)], axis=1)
    row = lambda b, c: (b * nc + c, 0)
    outs = pl.pallas_call(
        functools.partial(_mlstm_kernel, T=T),
        grid=(batch, nc),
        in_specs=[pl.BlockSpec((T, D_ML), lambda b, c: (b * nc + c, COL_QML // D_ML)),
                  pl.BlockSpec((T, D_ML), lambda b, c: (b * nc + c, COL_KML // D_ML)),
                  pl.BlockSpec((T, D_ML), lambda b, c: (b * nc + c, COL_VML // D_ML)),
                  pl.BlockSpec((T, D_ML), lambda b, c: (b * nc + c, COL_OML // D_ML)),
                  pl.BlockSpec((T, LANE), row),
                  pl.BlockSpec((1, CONV_W - 1, 2 * D_ML), lambda b, c: (b, 0, 0)),
                  pl.BlockSpec((1, ML_HEADS, ML_HEAD_DIM, ML_HEAD_DIM), lambda b, c: (b, 0, 0, 0)),
                  pl.BlockSpec((1, 8, ML_HEAD_DIM), lambda b, c: (b, 0, 0)),
                  pl.BlockSpec((1, 8, LANE), lambda b, c: (b, 0, 0)),
                  pl.BlockSpec((CONV_W, 2 * D_ML), lambda b, c: (0, 0)),
                  pl.BlockSpec((1, LANE), lambda b, c: (0, 0)),
                  pl.BlockSpec((1, D_ML), lambda b, c: (0, 0))],
        out_specs=[pl.BlockSpec((T, D_ML), row),
                   pl.BlockSpec((1, CONV_W - 1, 2 * D_ML), lambda b, c: (b, 0, 0)),
                   pl.BlockSpec((1, ML_HEADS, ML_HEAD_DIM, ML_HEAD_DIM), lambda b, c: (b, 0, 0, 0)),
                   pl.BlockSpec((1, 8, ML_HEAD_DIM), lambda b, c: (b, 0, 0)),
                   pl.BlockSpec((1, 8, LANE), lambda b, c: (b, 0, 0))],
        out_shape=[jax.ShapeDtypeStruct((batch * seq, D_ML), BF16),
                   jax.ShapeDtypeStruct((batch, CONV_W - 1, 2 * D_ML), F32),
                   jax.ShapeDtypeStruct((batch, ML_HEADS, ML_HEAD_DIM, ML_HEAD_DIM), F32),
                   jax.ShapeDtypeStruct((batch, 8, ML_HEAD_DIM), F32),
                   jax.ShapeDtypeStruct((batch, 8, LANE), F32)],
        scratch_shapes=[pltpu.VMEM((8 + T, 2 * D_ML), F32),
                        pltpu.VMEM((ML_HEADS, ML_HEAD_DIM, ML_HEAD_DIM), F32),
                        pltpu.VMEM((8, ML_HEAD_DIM), F32),
                        pltpu.VMEM((8, LANE), F32)],
        compiler_params=_cparams(("arbitrary", "arbitrary")),
        name="mlstm",
    )(u, u, u, u, small, conv_buf, c0, n0p, m0b, conv_w, gate_bias, norm_g)
    y, cbo, co, no, mo = outs
    return y, cbo, co, no[:, :ML_HEADS], mo[:, :ML_HEADS, 0]


def _bucket_np(dist):
    n = np.maximum(dist, 0)
    max_exact = N_BUCKETS // 2
    nf = np.maximum(n, 1).astype(np.float64)
    large = max_exact + (np.log(nf / max_exact) / math.log(MAX_DISTANCE / max_exact)
                         * (N_BUCKETS - max_exact)).astype(np.int64)
    return np.where(n < max_exact, n, np.minimum(large, N_BUCKETS - 1)).astype(np.int32)


def _bias_by_distance(rel_bias, n):
    return rel_bias.astype(F32)[_bucket_np(np.arange(n))].T


def _overlap_t(n_cmp, nch, n_slc):
    c0 = np.arange(nch) * CMP_STRIDE
    s0 = np.arange(n_slc) * SLC_BLOCK
    ov = np.minimum(c0[None, :] + CMP_BLOCK, s0[:, None] + SLC_BLOCK) - np.maximum(c0[None, :], s0[:, None])
    ov = np.clip(ov, 0, None).astype(np.float32) / CMP_BLOCK
    ov[:, n_cmp:] = 0.0
    return jnp.asarray(ov, BF16)


def _pq_kernel(x0_ref, x1_ref, x2_ref, x3_ref, pos_ref, w_ref, o_ref, *, rows):
    for pair, x_ref in enumerate((x0_ref, x1_ref, x2_ref, x3_ref)):
        toks = [x_ref[pl.ds(l, rows, stride=CMP_STRIDE), :] for l in range(CMP_STRIDE)]
        for half in range(2):
            ch = 2 * pair + half
            c = ch // NSA_KV_HEADS
            lanes = slice(half * NSA_HEAD_DIM, (half + 1) * NSA_HEAD_DIM)
            chunk = jnp.concatenate([t[:, lanes] for t in toks], axis=1)
            for part in range(2):
                lhs = (chunk + pos_ref[c, part:part + 1, :]).astype(BF16)
                o_ref[0, ch, :, part * CMP_HIDDEN:(part + 1) * CMP_HIDDEN] = _dot(lhs, w_ref[c, part])


def _hid_kernel(pq_ref, w2_ref, o_ref, *, nch):
    p = pq_ref[0, 0, :, 0:CMP_HIDDEN]
    q = pltpu.roll(pq_ref[0, 0, :, CMP_HIDDEN:2 * CMP_HIDDEN], nch - 1, 0)
    hid = jax.nn.gelu(p + q, approximate=True)
    o_ref[0, 0] = _dot(hid.astype(BF16), w2_ref[0])


def _compress(kv2d, col_block, bsz, seq, cmp_pos, cmp_w1, cmp_w2, tt):
    nch = seq // CMP_STRIDE
    kdim = CMP_STRIDE * NSA_HEAD_DIM
    nt = seq // tt
    tn = tt // CMP_STRIDE
    w1 = cmp_w1.reshape(2, 2, kdim, CMP_HIDDEN).astype(BF16)
    pos = cmp_pos.reshape(2, 2, kdim)
    pq = pl.pallas_call(
        functools.partial(_pq_kernel, rows=tn),
        grid=(bsz, nt),
        in_specs=[pl.BlockSpec((tt, LANE), functools.partial(lambda b, i, k: (b * nt + i, col_block * 4 + k), k=k))
                  for k in range(4)] +
                 [pl.BlockSpec((2, 2, kdim), lambda b, i: (0, 0, 0)),
                  pl.BlockSpec((2, 2, kdim, CMP_HIDDEN), lambda b, i: (0, 0, 0, 0))],
        out_specs=pl.BlockSpec((1, 8, tn, 2 * CMP_HIDDEN), lambda b, i: (b, 0, i, 0)),
        out_shape=jax.ShapeDtypeStruct((bsz, 8, nch, 2 * CMP_HIDDEN), F32),
        compiler_params=_cparams(("arbitrary", "arbitrary")),
        name="cmp_pq",
    )(kv2d, kv2d, kv2d, kv2d, pos, w1)
    return _cmp_hid(pq, cmp_w2)


def _cmp_hid(pq, cmp_w2):
    bsz, _, nch, _ = pq.shape
    return pl.pallas_call(
        functools.partial(_hid_kernel, nch=nch),
        grid=(bsz, 8),
        in_specs=[pl.BlockSpec((1, 1, nch, 2 * CMP_HIDDEN), lambda b, c: (b, c, 0, 0)),
                  pl.BlockSpec((1, CMP_HIDDEN, NSA_HEAD_DIM), lambda b, c: (c // NSA_KV_HEADS, 0, 0))],
        out_specs=pl.BlockSpec((1, 1, nch, NSA_HEAD_DIM), lambda b, c: (b, c, 0, 0)),
        out_shape=jax.ShapeDtypeStruct((bsz, 8, nch, NSA_HEAD_DIM), F32),
        compiler_params=_cparams(("arbitrary", "arbitrary")),
        name="cmp_hid",
    )(pq, cmp_w2.astype(BF16))


def _top_n_mask(score_t, blk, n_rows):
    rank = jnp.zeros(score_t.shape, jnp.int32)
    for i in range(n_rows):
        row = score_t[i:i + 1, :]
        beats = (row > score_t) | ((row == score_t) & (blk > i))
        rank = rank + beats.astype(jnp.int32)
    return (rank < min(SLC_TOPN, n_rows)).astype(F32)


def _bias_from_buckets(bucket, table_ref, first_head):
    biases = [jnp.zeros(bucket.shape, F32) for _ in range(NSA_GROUP)]
    for k in range(N_BUCKETS):
        hit = bucket == k
        biases = [jnp.where(hit, table_ref[k, first_head + g], biases[g]) for g in range(NSA_GROUP)]
    return biases


def _cattn_kernel(tab_ref, q_ref, kc_ref, vc_ref, bk_ref, ovt_ref, o_ref, sel_ref, bias_scr, strip_scr,
                  *, tq, nch, n_cmp, n_slc):
    h = pl.program_id(0)
    i = pl.program_id(1)

    bands = tq // CMP_STRIDE
    for g, strip in enumerate(_bias_from_buckets(bk_ref[0], tab_ref, h * NSA_GROUP)):
        strip_scr[g] = strip
        for a in range(bands):
            off = bands - 1 - a
            bias_scr[g, a * CMP_STRIDE:(a + 1) * CMP_STRIDE, :] = strip_scr[g, :, off:off + nch]

    t = i * tq + lax.broadcasted_iota(jnp.int32, (tq, nch), 0)
    n = lax.broadcasted_iota(jnp.int32, (tq, nch), 1)
    valid = (t - CMP_STRIDE * n - (CMP_BLOCK - 1) >= 0) & (n < n_cmp)
    blk = lax.broadcasted_iota(jnp.int32, (n_slc, tq), 0)
    cur = (i * tq + lax.broadcasted_iota(jnp.int32, (n_slc, tq), 1)) // SLC_BLOCK
    forced = (blk == 0) | (blk == cur) | (blk == cur - 1)
    for bb in range(q_ref.shape[0]):
        kc = kc_ref[bb, 0].astype(BF16)
        vc = vc_ref[bb, 0].astype(BF16)
        pc = jnp.zeros((tq, nch), F32)
        for g in range(NSA_GROUP):
            sl = slice(g * NSA_HEAD_DIM, (g + 1) * NSA_HEAD_DIM)
            qg = (q_ref[bb, :, sl] * (NSA_HEAD_DIM ** -0.5)).astype(BF16)
            s = _dot_nt(qg, kc) + bias_scr[g]
            s = jnp.where(valid, s, NEG_BIG)
            e = jnp.exp(s - jnp.max(s, axis=1, keepdims=True))
            p = jnp.where(valid, e / jnp.sum(e, axis=1, keepdims=True), 0.0)
            o_ref[bb, :, sl] = _dot(p.astype(BF16), vc)
            pc = pc + p
        score_t = _dot_nt(ovt_ref[...], pc.astype(BF16))
        score_t = jnp.where(forced, FORCE_SCORE, score_t)
        score_t = jnp.where(blk > cur, -FORCE_SCORE, score_t)
        sel_ref[bb, 0] = _top_n_mask(score_t, blk, n_slc)


def _cattn(u, kvc, rel_bias, batch, seq, tq):
    nch = kvc.shape[2]
    n_cmp = nch - 1
    n_slc = seq // SLC_BLOCK
    ni = seq // tq
    ovt = _overlap_t(n_cmp, nch, n_slc)
    qcol = COL_QNSA // (NSA_GROUP * NSA_HEAD_DIM)
    bands = tq // CMP_STRIDE
    strip_w = -(-(nch + bands - 1) // LANE) * LANE
    dist = (tq * np.arange(ni)[:, None, None] + np.arange(CMP_STRIDE)[None, :, None]
            - CMP_STRIDE * (np.arange(strip_w)[None, None, :] - (bands - 1)) - (CMP_BLOCK - 1))
    buckets = jnp.asarray(_bucket_np(dist))
    o_c, sel_t = pl.pallas_call(
        functools.partial(_cattn_kernel, tq=tq, nch=nch, n_cmp=n_cmp, n_slc=n_slc),
        grid=(NSA_KV_HEADS, ni),
        in_specs=[pl.BlockSpec(memory_space=pltpu.SMEM),
                  pl.BlockSpec((batch, tq, NSA_GROUP * NSA_HEAD_DIM), lambda h, i: (0, i, qcol + h)),
                  pl.BlockSpec((batch, 1, nch, NSA_HEAD_DIM), lambda h, i: (0, h, 0, 0)),
                  pl.BlockSpec((batch, 1, nch, NSA_HEAD_DIM), lambda h, i: (0, NSA_KV_HEADS + h, 0, 0)),
                  pl.BlockSpec((1, CMP_STRIDE, strip_w), lambda h, i: (i, 0, 0)),
                  pl.BlockSpec((n_slc, nch), lambda h, i: (0, 0))],
        out_specs=[pl.BlockSpec((batch, tq, NSA_GROUP * NSA_HEAD_DIM), lambda h, i: (0, i, h)),
                   pl.BlockSpec((batch, 1, n_slc, tq), lambda h, i: (0, h, 0, i))],
        out_shape=[jax.ShapeDtypeStruct((batch, seq, D_NSA), F32),
                   jax.ShapeDtypeStruct((batch, NSA_KV_HEADS, n_slc, seq), F32)],
        scratch_shapes=[pltpu.VMEM((NSA_GROUP, tq, nch), F32), pltpu.VMEM((NSA_GROUP, CMP_STRIDE, strip_w), F32)],
        compiler_params=_cparams(("arbitrary", "arbitrary")),
        name="cmp_attn",
    )(rel_bias.astype(F32), u.reshape(batch, seq, u.shape[1]), kvc, kvc, buckets, ovt)
    return o_c.reshape(batch * seq, D_NSA), sel_t


TQ = 128
FLASH_GROUP = 8


def _flash_kernel(tab_ref, q_ref, kv_ref, bk_ref, *rest, selected, n_delta):
    if selected:
        sel_ref, o_ref, band_scr, qs_scr, m_scr, l_scr, acc_scr = rest
    else:
        o_ref, band_scr, qs_scr, m_scr, l_scr, acc_scr = rest
    h = pl.program_id(0)
    i = pl.program_id(1)
    n_batch = q_ref.shape[0]

    @pl.when(i == 0)
    def _():
        def fill(d, carry):
            for g, bias in enumerate(_bias_from_buckets(bk_ref[d], tab_ref, h * NSA_GROUP)):
                band_scr[d, :, g * TQ:(g + 1) * TQ] = bias
            return carry
        lax.fori_loop(0, n_delta, fill, 0)

    for bb in range(n_batch):
        for g in range(NSA_GROUP):
            qg = (q_ref[bb, :, g * NSA_HEAD_DIM:(g + 1) * NSA_HEAD_DIM] * (NSA_HEAD_DIM ** -0.5)).astype(BF16)
            qs_scr[bb, g * TQ:(g + 1) * TQ, :] = jnp.concatenate([qg, jnp.zeros_like(qg)], axis=1)
    is_key_lane = lax.broadcasted_iota(jnp.int32, (TQ, LANE), 1) < NSA_HEAD_DIM
    m_scr[...] = jnp.full(m_scr.shape, 0.5 * NEG_BIG, F32)
    l_scr[...] = jnp.zeros(l_scr.shape, F32)
    acc_scr[...] = jnp.zeros(acc_scr.shape, F32)
    key = lax.broadcasted_iota(jnp.int32, (TQ, TQ), 0)
    qry = lax.broadcasted_iota(jnp.int32, (TQ, TQ), 1)
    n_back = WINDOW // TQ

    def scores(bb, j, kind):
        kvj = kv_ref[bb, pl.ds(pl.multiple_of(j * TQ, TQ), TQ), :].astype(BF16)
        s = _dot_nt(kvj, qs_scr[bb]) + band_scr[i - j]
        mask = None
        if kind == "diag":
            mask = qry >= key
        elif kind == "far":
            mask = qry < key
        if selected:
            r = sel_ref[bb, 0, pl.ds(2 * j, 2), :]
            picked = jnp.where(key < SLC_BLOCK, r[0:1, :], r[1:2, :]) > 0.5
            mask = picked if mask is None else (mask & picked)
        if mask is not None:
            s = jnp.concatenate([jnp.where(mask, s[:, g * TQ:(g + 1) * TQ], NEG_BIG) for g in range(NSA_GROUP)], axis=1)
        return s, jnp.where(is_key_lane, jnp.ones_like(kvj), kvj)

    def update(tile_ids):
        all_tiles = [[scores(bb, j, kind) for j, kind in tile_ids] for bb in range(n_batch)]
        for bb, tiles in enumerate(all_tiles):
            m_old = m_scr[bb]
            m_new = m_old
            for s, _ in tiles:
                m_new = jnp.maximum(m_new, jnp.max(s, axis=0, keepdims=True))
            alpha = jnp.exp(m_old - m_new)
            l_new = alpha * l_scr[bb]
            acc = alpha * acc_scr[bb]
            for s, ones_v in tiles:
                p = jnp.exp(s - m_new)
                pv = _dot_tn(ones_v, p.astype(BF16))
                l_new = l_new + pv[0:1, :]
                acc = acc + pv[NSA_HEAD_DIM:2 * NSA_HEAD_DIM, :]
            m_scr[bb] = m_new
            l_scr[bb] = l_new
            acc_scr[bb] = acc

    def full_tiles(first, count):
        return [(first + k, "full") for k in range(count)]

    if selected:
        def group(t, carry):
            update(full_tiles(FLASH_GROUP * t, FLASH_GROUP))
            return carry

        n_groups = i // FLASH_GROUP
        lax.fori_loop(0, n_groups, group, 0)
        for rem in range(FLASH_GROUP):
            @pl.when(i - n_groups * FLASH_GROUP == rem)
            def _():
                update(full_tiles(n_groups * FLASH_GROUP, rem) + [(i, "diag")])
    else:
        @pl.when(i >= n_back)
        def _():
            update([(i - n_back, "far")] + full_tiles(i - n_back + 1, n_back - 1) + [(i, "diag")])

        for rem in range(n_back):
            @pl.when(i == rem)
            def _():
                update(full_tiles(0, rem) + [(i, "diag")])

    for bb in range(n_batch):
        for g in range(NSA_GROUP):
            cols = slice(g * TQ, (g + 1) * TQ)
            o_ref[bb, :, g * NSA_HEAD_DIM:(g + 1) * NSA_HEAD_DIM] = (acc_scr[bb, :, cols] / l_scr[bb, :, cols]).T


def _flash(u, kv_col, rel_bias, sel_t, batch, seq):
    ni = seq // TQ
    qcol = COL_QNSA // (NSA_GROUP * NSA_HEAD_DIM)
    selected = sel_t is not None
    n_delta = ni if selected else WINDOW // TQ + 1
    kvblk = kv_col // LANE
    delta = np.arange(n_delta)[:, None, None] * TQ + np.arange(TQ)[None, None, :] - np.arange(TQ)[None, :, None]
    buckets = jnp.asarray(_bucket_np(delta))
    u3 = u.reshape(batch, seq, u.shape[1])
    in_specs = [pl.BlockSpec(memory_space=pltpu.SMEM),
                pl.BlockSpec((batch, TQ, NSA_GROUP * NSA_HEAD_DIM), lambda h, i: (0, i, qcol + h)),
                pl.BlockSpec((batch, seq, LANE), lambda h, i: (0, 0, kvblk + h)),
                pl.BlockSpec((n_delta, TQ, TQ), lambda h, i: (0, 0, 0))]
    args = [rel_bias.astype(F32), u3, u3, buckets]
    if selected:
        n_slc = sel_t.shape[2]
        in_specs.append(pl.BlockSpec((batch, 1, n_slc, TQ), lambda h, i: (0, h, 0, i)))
        args.append(sel_t)
    scratch = [pltpu.VMEM((n_delta, TQ, NSA_GROUP * TQ), F32),
               pltpu.VMEM((batch, NSA_GROUP * TQ, LANE), BF16),
               pltpu.VMEM((batch, 1, NSA_GROUP * TQ), F32),
               pltpu.VMEM((batch, 1, NSA_GROUP * TQ), F32),
               pltpu.VMEM((batch, NSA_HEAD_DIM, NSA_GROUP * TQ), F32)]
    out = pl.pallas_call(
        functools.partial(_flash_kernel, selected=selected, n_delta=n_delta),
        grid=(NSA_KV_HEADS, ni),
        in_specs=in_specs,
        out_specs=pl.BlockSpec((batch, TQ, NSA_GROUP * NSA_HEAD_DIM), lambda h, i: (0, i, h)),
        out_shape=jax.ShapeDtypeStruct((batch, seq, D_NSA), F32),
        scratch_shapes=scratch,
        compiler_params=_cparams(("arbitrary", "arbitrary")),
        name="slc_attn" if selected else "win_attn",
    )(*args)
    return out.reshape(batch * seq, D_NSA)


def _gate_expand():
    e = np.zeros((3, 2 * LANE, D_NSA), np.float32)
    for br in range(3):
        for hd in range(NSA_HEADS):
            for part in range(2):
                e[br, part * LANE + 2 * ML_HEADS + br * NSA_HEADS + hd,
                  hd * NSA_HEAD_DIM:(hd + 1) * NSA_HEAD_DIM] = 1.0
    return jnp.asarray(e, BF16)


def _outproj_kernel(yml_ref, oc_ref, os_ref, ow_ref, s_ref, e_ref, ng_ref, w_ref, x_ref, o_ref, cat_scr):
    @pl.when(pl.program_id(1) == 0)
    def _():
        sig = jax.nn.sigmoid(s_ref[...])
        hi_lo = jnp.concatenate(_split2(sig), axis=1)
        o = jnp.zeros(oc_ref.shape, F32)
        for br, ref in enumerate((oc_ref, os_ref, ow_ref)):
            o = o + _dot(hi_lo, e_ref[br]) * ref[...]
        o = o * lax.rsqrt(jnp.mean(o * o, axis=-1, keepdims=True) + EPS) * ng_ref[...]
        cat_scr[:, 0:D_ML] = yml_ref[...]
        cat_scr[:, D_ML:D_ML + D_NSA] = o.astype(BF16)

    o_ref[...] = x_ref[...] + _dot(cat_scr[...], w_ref[...])


def _outproj(y_ml, o_c, o_s, o_w, small, nsa_g, w_out_b, x2d, tm, tn):
    n, d = x2d.shape
    rows = lambda i, j: (i, 0)
    return pl.pallas_call(
        _outproj_kernel,
        grid=(n // tm, d // tn),
        in_specs=[pl.BlockSpec((tm, D_ML), rows), pl.BlockSpec((tm, D_NSA), rows),
                  pl.BlockSpec((tm, D_NSA), rows), pl.BlockSpec((tm, D_NSA), rows),
                  pl.BlockSpec((tm, LANE), rows),
                  pl.BlockSpec((3, 2 * LANE, D_NSA), lambda i, j: (0, 0, 0)),
                  pl.BlockSpec((1, D_NSA), lambda i, j: (0, 0)),
                  pl.BlockSpec((D_ML + D_NSA, tn), lambda i, j: (0, j)),
                  pl.BlockSpec((tm, tn), lambda i, j: (i, j))],
        out_specs=pl.BlockSpec((tm, tn), lambda i, j: (i, j)),
        out_shape=jax.ShapeDtypeStruct((n, d), F32),
        scratch_shapes=[pltpu.VMEM((tm, D_ML + D_NSA), BF16)],
        compiler_params=_cparams(("arbitrary", "arbitrary")),
        name="outproj",
    )(y_ml, o_c, o_s, o_w, small, _gate_expand(), nsa_g, w_out_b, x2d)


ROUTE_COL = N_GROUPS
BIG_COL = 1 << 20
SPLIT = D_MODEL // LANE
DMA_UNROLL = 8


def _store_split(ref, val, n):
    for k in range(SPLIT):
        ref[pl.ds(k, n, stride=SPLIT), :] = val[:, k * LANE:(k + 1) * LANE]


def _load_split(ref, n):
    return jnp.concatenate([ref[pl.ds(k, n, stride=SPLIT), :] for k in range(SPLIT)], axis=1)


PACKED_SPLIT = SPLIT // 2
HIGH_HALF = 0xFFFF0000


def _store_packed(ref, val, n):
    bits = pltpu.bitcast(val.astype(BF16).astype(F32), jnp.uint32)
    half = D_MODEL // 2
    packed = (bits[:, half:] & jnp.uint32(HIGH_HALF)) | (bits[:, :half] >> 16)
    for k in range(PACKED_SPLIT):
        ref[pl.ds(k, n, stride=PACKED_SPLIT), :] = packed[:, k * LANE:(k + 1) * LANE]


def _load_packed(ref, n):
    packed = jnp.concatenate([ref[pl.ds(k, n, stride=PACKED_SPLIT), :] for k in range(PACKED_SPLIT)], axis=1)
    low = pltpu.bitcast(packed << 16, F32)
    high = pltpu.bitcast(packed & jnp.uint32(HIGH_HALF), F32)
    return jnp.concatenate([low, high], axis=1).astype(BF16)


def _route_kernel(y_ref, g_ref, w_ref, b_ref, h_ref, info_ref, cnt_ref, carry_scr, *, tm):
    @pl.when(pl.program_id(0) == 0)
    def _():
        carry_scr[...] = jnp.zeros(carry_scr.shape, F32)

    x = y_ref[...]
    h = x * lax.rsqrt(jnp.mean(x * x, axis=-1, keepdims=True) + EPS) * g_ref[...]
    _store_packed(h_ref, h, tm)
    logit = _dot(h.astype(BF16), w_ref[...]) + b_ref[...]
    col = lax.broadcasted_iota(jnp.int32, logit.shape, 1)
    is_grp = col < N_GROUPS
    gmax = jnp.max(jnp.where(is_grp, logit, -jnp.inf), axis=1, keepdims=True)
    gtop = jnp.min(jnp.where(is_grp & (logit == gmax), col, BIG_COL), axis=1, keepdims=True)
    gsum = jnp.sum(jnp.where(is_grp, jnp.exp(logit - gmax), 0.0), axis=1, keepdims=True)
    first = ROUTE_COL + gtop * EXPERTS_PER_GROUP
    in_grp = (col >= first) & (col < first + EXPERTS_PER_GROUP)
    v1 = jnp.max(jnp.where(in_grp, logit, -jnp.inf), axis=1, keepdims=True)
    i1 = jnp.min(jnp.where(in_grp & (logit == v1), col, BIG_COL), axis=1, keepdims=True)
    rest = in_grp & (col != i1)
    v2 = jnp.max(jnp.where(rest, logit, -jnp.inf), axis=1, keepdims=True)
    i2 = jnp.min(jnp.where(rest & (logit == v2), col, BIG_COL), axis=1, keepdims=True)
    e = jnp.exp(v2 - v1)
    w1 = 1.0 / ((1.0 + e) * gsum)
    w2 = e / ((1.0 + e) * gsum)
    pick1 = col == i1
    pick2 = col == i2
    both = (pick1 | pick2).astype(F32)
    ri = lax.broadcasted_iota(jnp.int32, (tm, tm), 0)
    ci = lax.broadcasted_iota(jnp.int32, (tm, tm), 1)
    before = (ci < ri).astype(BF16)
    cum = _dot(before, both.astype(BF16)) + carry_scr[...]
    r1 = jnp.sum(jnp.where(pick1, cum, 0.0), axis=1, keepdims=True)
    r2 = jnp.sum(jnp.where(pick2, cum, 0.0), axis=1, keepdims=True)
    carry_scr[...] = carry_scr[...] + jnp.sum(both, axis=0, keepdims=True)
    cnt_ref[...] = jnp.broadcast_to(carry_scr[...], cnt_ref.shape)
    info = jnp.where(col == 0, (i1 - ROUTE_COL).astype(F32), 0.0)
    info = jnp.where(col == 1, (i2 - ROUTE_COL).astype(F32), info)
    info = jnp.where(col == 2, w1, info)
    info = jnp.where(col == 3, w2, info)
    info = jnp.where(col == 4, r1, info)
    info_ref[...] = jnp.where(col == 5, r2, info)


def _route(y2d, g, w_rg, b_rg, w_re, b_re, tm):
    n, d = y2d.shape
    wr = jnp.concatenate([w_rg, w_re, jnp.zeros((d, LANE - N_GROUPS - N_EXPERTS), F32)], axis=1).astype(BF16)
    bias = jnp.concatenate([b_rg, b_re, jnp.zeros((LANE - N_GROUPS - N_EXPERTS,), F32)])[None, :]
    rows = lambda i: (i, 0)
    fixed = lambda i: (0, 0)
    return pl.pallas_call(
        functools.partial(_route_kernel, tm=tm),
        grid=(n // tm,),
        in_specs=[pl.BlockSpec((tm, d), rows), pl.BlockSpec((1, d), fixed),
                  pl.BlockSpec((d, LANE), fixed), pl.BlockSpec((1, LANE), fixed)],
        out_specs=[pl.BlockSpec((tm * PACKED_SPLIT, LANE), rows), pl.BlockSpec((tm, LANE), rows),
                   pl.BlockSpec((8, LANE), fixed)],
        out_shape=[jax.ShapeDtypeStruct((n * PACKED_SPLIT, LANE), jnp.uint32), jax.ShapeDtypeStruct((n, LANE), F32),
                   jax.ShapeDtypeStruct((8, LANE), F32)],
        scratch_shapes=[pltpu.VMEM((1, LANE), F32)],
        compiler_params=_cparams(("arbitrary",)),
        name="moe_route",
    )(y2d, g, wr, bias)


W_CHUNKS = 4


def _expert_kernel(te_ref, nu_ref, src_ref, nx_ref, ws_ref, h_hbm, wg_hbm, wu_hbm, wd_hbm, o_ref,
                   xbuf, wg_f, wu_f, wd_f, wg_b, wu_b, wd_b, sem, wsem, *, tme):
    i = pl.program_id(0)
    slot = i % 2

    def weight_copies(expert, wslot):
        copies = []
        for hbm, buf in ((wg_hbm, wg_f), (wu_hbm, wu_f), (wd_hbm, wd_f)):
            step = hbm.shape[1] // W_CHUNKS
            for c in range(W_CHUNKS):
                copies.append(pltpu.make_async_copy(hbm.at[expert, pl.ds(c * step, step)],
                                                    buf.at[wslot, pl.ds(c * step, step)], wsem.at[wslot]))
        return copies

    def row_copy(tile, r, buf_slot):
        src_row = src_ref[tile * tme + r]
        return pltpu.make_async_copy(
            h_hbm.at[pl.ds(pl.multiple_of(src_row * PACKED_SPLIT, PACKED_SPLIT), PACKED_SPLIT)],
            xbuf.at[buf_slot, pl.ds(pl.multiple_of(r * PACKED_SPLIT, PACKED_SPLIT), PACKED_SPLIT)],
            sem.at[buf_slot])

    def fetch(tile, buf_slot):
        def body(r, carry):
            row_copy(tile, r, buf_slot).start()
            return carry
        lax.fori_loop(0, tme, body, 0, unroll=DMA_UNROLL)

    @pl.when(i == 0)
    def _():
        fetch(0, 0)
        for cp in weight_copies(te_ref[0], ws_ref[0]):
            cp.start()

    @pl.when(i + 1 < nu_ref[0])
    def _():
        fetch(i + 1, 1 - slot)

    @pl.when(i < nu_ref[0])
    def _():
        prev = te_ref[jnp.maximum(i - 1, 0)]

        @pl.when((i == 0) | (te_ref[i] != prev))
        def _():
            w = ws_ref[i]
            for cp in weight_copies(te_ref[i], w):
                cp.wait()
            wg_b[...] = wg_f[w].astype(BF16)
            wu_b[...] = wu_f[w].astype(BF16)
            wd_b[...] = wd_f[w].astype(BF16)

            @pl.when(nx_ref[i] >= 0)
            def _():
                for cp in weight_copies(nx_ref[i], 1 - w):
                    cp.start()

        pltpu.make_async_copy(h_hbm.at[pl.ds(0, tme * PACKED_SPLIT)], xbuf.at[slot], sem.at[slot]).wait()

        x = _load_packed(xbuf.at[slot], tme)
        a = _dot(x, wg_b[...])
        u = _dot(x, wu_b[...])
        hid = a * jax.nn.sigmoid(a) * u
        _store_split(o_ref, _dot(hid.astype(BF16), wd_b[...]), tme)

    @pl.when(i >= nu_ref[0])
    def _():
        o_ref[...] = jnp.zeros(o_ref.shape, F32)


def _experts(h, src, tile_expert, n_used, next_expert, weight_slot, w_gate, w_up, w_down, tme):
    p = src.shape[0]
    d = D_MODEL
    hbm = pl.BlockSpec(memory_space=pl.ANY)
    return pl.pallas_call(
        functools.partial(_expert_kernel, tme=tme),
        grid_spec=pltpu.PrefetchScalarGridSpec(
            num_scalar_prefetch=5, grid=(p // tme,),
            in_specs=[hbm, hbm, hbm, hbm],
            out_specs=pl.BlockSpec((tme * SPLIT, LANE), lambda i, *_: (i, 0)),
            scratch_shapes=[pltpu.VMEM((2, tme * PACKED_SPLIT, LANE), jnp.uint32),
                            pltpu.VMEM((2, d, D_EXPERT), F32), pltpu.VMEM((2, d, D_EXPERT), F32),
                            pltpu.VMEM((2, D_EXPERT, d), F32),
                            pltpu.VMEM((d, D_EXPERT), BF16), pltpu.VMEM((d, D_EXPERT), BF16),
                            pltpu.VMEM((D_EXPERT, d), BF16),
                            pltpu.SemaphoreType.DMA((2,)), pltpu.SemaphoreType.DMA((2,))]),
        out_shape=jax.ShapeDtypeStruct((p * SPLIT, LANE), F32),
        compiler_params=_cparams(("arbitrary",)),
        name="moe_experts",
    )(tile_expert, n_used, src, next_expert, weight_slot, h, w_gate, w_up, w_down)


def _combine_kernel(d1_ref, d2_ref, y_ref, info_ref, ys_hbm, g_ref, o_ref, a_buf, b_buf, sem, *, tm):
    i = pl.program_id(0)
    slot = i % 2

    def copy(r, row, buf, buf_slot):
        return pltpu.make_async_copy(ys_hbm.at[pl.ds(pl.multiple_of(row * SPLIT, SPLIT), SPLIT)],
                                     buf.at[buf_slot, pl.ds(pl.multiple_of(r * SPLIT, SPLIT), SPLIT)],
                                     sem.at[buf_slot])

    def fetch(tile, buf_slot):
        def issue(r, carry):
            copy(r, d1_ref[tile * tm + r], a_buf, buf_slot).start()
            copy(r, d2_ref[tile * tm + r], b_buf, buf_slot).start()
            return carry
        lax.fori_loop(0, tm, issue, 0, unroll=DMA_UNROLL)

    @pl.when(i == 0)
    def _():
        fetch(0, 0)

    @pl.when(i + 1 < pl.num_programs(0))
    def _():
        fetch(i + 1, 1 - slot)

    for buf in (a_buf, b_buf):
        pltpu.make_async_copy(ys_hbm.at[pl.ds(0, tm * SPLIT)], buf.at[slot], sem.at[slot]).wait()
    w1 = info_ref[:, 2:3]
    w2 = info_ref[:, 3:4]
    y = y_ref[...] + w1 * _load_split(a_buf.at[slot], tm) + w2 * _load_split(b_buf.at[slot], tm)
    o_ref[...] = y * lax.rsqrt(jnp.mean(y * y, axis=-1, keepdims=True) + EPS) * g_ref[...]


def _combine(y2d, info, ys, d1, d2, g, tm):
    n, d = y2d.shape
    return pl.pallas_call(
        functools.partial(_combine_kernel, tm=tm),
        grid_spec=pltpu.PrefetchScalarGridSpec(
            num_scalar_prefetch=2, grid=(n // tm,),
            in_specs=[pl.BlockSpec((tm, d), lambda i, a, b: (i, 0)),
                      pl.BlockSpec((tm, LANE), lambda i, a, b: (i, 0)),
                      pl.BlockSpec(memory_space=pl.ANY),
                      pl.BlockSpec((1, d), lambda i, a, b: (0, 0))],
            out_specs=pl.BlockSpec((tm, d), lambda i, a, b: (i, 0)),
            scratch_shapes=[pltpu.VMEM((2, tm * SPLIT, LANE), F32), pltpu.VMEM((2, tm * SPLIT, LANE), F32),
                            pltpu.SemaphoreType.DMA((2,))]),
        out_shape=jax.ShapeDtypeStruct((n, d), F32),
        compiler_params=_cparams(("arbitrary",)),
        name="moe_combine",
    )(d1, d2, y2d, info, ys, g)


def _moe_final(y2d, ffn_g, w_rg, b_rg, w_re, b_re, w_gate, w_up, w_down, final_g, tm, tme):
    n, d = y2d.shape
    h, info, cnt = _route(y2d, ffn_g, w_rg, b_rg, w_re, b_re, tm)
    e1 = info[:, 0].astype(jnp.int32)
    e2 = info[:, 1].astype(jnp.int32)
    counts = cnt[0, ROUTE_COL:ROUTE_COL + N_EXPERTS].astype(jnp.int32)
    padded = (counts + tme - 1) // tme * tme
    ends = jnp.cumsum(padded)
    offs = ends - padded
    d1 = offs[e1] + info[:, 4].astype(jnp.int32)
    d2 = offs[e2] + info[:, 5].astype(jnp.int32)
    p = 2 * n + N_EXPERTS * tme
    tok = jnp.arange(n, dtype=jnp.int32)
    src = jnp.zeros((p,), jnp.int32).at[jnp.concatenate([d1, d2])].set(jnp.concatenate([tok, tok]))
    n_tiles = p // tme
    n_used = (ends[-1] // tme).astype(jnp.int32).reshape(1)
    tile_start = jnp.arange(n_tiles, dtype=jnp.int32) * tme
    tile_expert = jnp.sum((ends[None, :] <= tile_start[:, None]).astype(jnp.int32), axis=1)
    last = jnp.sum((ends <= ends[-1] - 1).astype(jnp.int32))
    tile_expert = jnp.minimum(tile_expert, last).astype(jnp.int32)
    run = jnp.cumsum(jnp.concatenate([jnp.zeros((1,), jnp.int32),
                                      (tile_expert[1:] != tile_expert[:-1]).astype(jnp.int32)]))
    weight_slot = (run % 2).astype(jnp.int32)
    after = ends[tile_expert] // tme
    next_expert = jnp.where(after < n_used[0], tile_expert[jnp.minimum(after, n_tiles - 1)], -1).astype(jnp.int32)
    ys = _experts(h, src, tile_expert, n_used, next_expert, weight_slot, w_gate, w_up, w_down, tme)
    return _combine(y2d, info, ys, d1, d2, final_g, tm)


GROUP_ROWS = 8
SLC_LANES = 384


def _sample_cattn_kernel(q_ref, kc_ref, vc_ref, b_ref, ov_ref, o_ref, sel_ref, *, n_cmp, n_slc):
    q = (q_ref[0, 0] * (NSA_HEAD_DIM ** -0.5)).astype(BF16)
    s = _dot_nt(q, kc_ref[0, 0].astype(BF16)) + b_ref[0]
    valid = lax.broadcasted_iota(jnp.int32, s.shape, 1) < n_cmp
    s = jnp.where(valid, s, NEG_BIG)
    e = jnp.exp(s - jnp.max(s, axis=1, keepdims=True))
    p = jnp.where(valid, e / jnp.sum(e, axis=1, keepdims=True), 0.0)
    o_ref[0, 0] = _dot(p.astype(BF16), vc_ref[0, 0].astype(BF16))
    pg = jnp.where(lax.broadcasted_iota(jnp.int32, p.shape, 0) < NSA_GROUP, p, 0.0)
    pc = jnp.broadcast_to(jnp.sum(pg, axis=0, keepdims=True), pg.shape)
    score = _dot(pc.astype(BF16), ov_ref[...])[0:1, :]
    blk = lax.broadcasted_iota(jnp.int32, score.shape, 1)
    cur = n_slc - 1
    forced = (blk == 0) | (blk == cur) | (blk == cur - 1)
    score = jnp.where(forced, FORCE_SCORE, score)
    score = jnp.where(blk > cur, -jnp.inf, score)
    ri = lax.broadcasted_iota(jnp.int32, (SLC_LANES, SLC_LANES), 0)
    ci = lax.broadcasted_iota(jnp.int32, (SLC_LANES, SLC_LANES), 1)
    sb = jnp.broadcast_to(score, (SLC_LANES, SLC_LANES))
    col = jnp.sum(jnp.where(ri == ci, sb, 0.0), axis=1, keepdims=True)
    beats = (col > sb) | ((col == sb) & (ri < ci))
    rank = jnp.sum(beats.astype(jnp.int32), axis=0, keepdims=True)
    sel_ref[0, 0] = jnp.broadcast_to((rank < SLC_TOPN).astype(F32), (GROUP_ROWS, SLC_LANES))


def _sample_cattn(q8, kvc, bias_cs, n_cmp, n_slc):
    bsz, _, nch, _ = kvc.shape
    c0 = np.arange(nch) * CMP_STRIDE
    s0 = np.arange(SLC_LANES) * SLC_BLOCK
    ov = np.minimum(c0[:, None] + CMP_BLOCK, s0[None, :] + SLC_BLOCK) - np.maximum(c0[:, None], s0[None, :])
    ov = np.clip(ov, 0, None).astype(np.float32) / CMP_BLOCK
    ov[n_cmp:, :] = 0.0
    ov[:, n_slc:] = 0.0
    return pl.pallas_call(
        functools.partial(_sample_cattn_kernel, n_cmp=n_cmp, n_slc=n_slc),
        grid=(bsz, NSA_KV_HEADS),
        in_specs=[pl.BlockSpec((1, 1, GROUP_ROWS, NSA_HEAD_DIM), lambda b, h: (b, h, 0, 0)),
                  pl.BlockSpec((1, 1, nch, NSA_HEAD_DIM), lambda b, h: (b, h, 0, 0)),
                  pl.BlockSpec((1, 1, nch, NSA_HEAD_DIM), lambda b, h: (b, NSA_KV_HEADS + h, 0, 0)),
                  pl.BlockSpec((1, GROUP_ROWS, nch), lambda b, h: (h, 0, 0)),
                  pl.BlockSpec((nch, SLC_LANES), lambda b, h: (0, 0))],
        out_specs=[pl.BlockSpec((1, 1, GROUP_ROWS, NSA_HEAD_DIM), lambda b, h: (b, h, 0, 0)),
                   pl.BlockSpec((1, 1, GROUP_ROWS, SLC_LANES), lambda b, h: (b, h, 0, 0))],
        out_shape=[jax.ShapeDtypeStruct((bsz, NSA_KV_HEADS, GROUP_ROWS, NSA_HEAD_DIM), F32),
                   jax.ShapeDtypeStruct((bsz, NSA_KV_HEADS, GROUP_ROWS, SLC_LANES), F32)],
        compiler_params=_cparams(("arbitrary", "arbitrary")),
        name="sample_cmp_attn",
    )(q8, kvc, kvc, bias_cs, jnp.asarray(ov, BF16))


def _sample_attn_kernel(pg_ref, hf_ref, q_ref, cache_hbm, snew_ref, bs_ref, win_ref, wnew_ref, bw_ref,
                        os_ref, ow_ref, kt_buf, vt_buf, sem, *, n_gather, wlen):
    b = pl.program_id(0)
    rows_per_page = 2 * NSA_KV_HEADS * NSA_HEAD_DIM

    def page_copy(h, s, c, buf):
        page = pg_ref[(b * NSA_KV_HEADS + h) * n_gather + s]
        start = pl.multiple_of(page * rows_per_page + (c * NSA_KV_HEADS + h) * NSA_HEAD_DIM, NSA_HEAD_DIM)
        return pltpu.make_async_copy(cache_hbm.at[pl.ds(start, NSA_HEAD_DIM)],
                                     buf.at[h, :, pl.ds(s * PAGE_SIZE, PAGE_SIZE)], sem)

    for h in range(NSA_KV_HEADS):
        for s in range(n_gather):
            page_copy(h, s, 0, kt_buf).start()
            page_copy(h, s, 1, vt_buf).start()

    def attend(q, kts, vts, bias, valid):
        s = jnp.concatenate([_dot(q, kt.astype(BF16)) for kt in kts], axis=1) + bias
        s = jnp.where(valid, s, NEG_BIG)
        e = jnp.exp(s - jnp.max(s, axis=1, keepdims=True))
        p = jnp.where(valid, e / jnp.sum(e, axis=1, keepdims=True), 0.0).astype(BF16)
        out = None
        lo = 0
        for vt in vts:
            n = vt.shape[1]
            part = _dot_nt(p[:, lo:lo + n], vt.astype(BF16))
            out = part if out is None else out + part
            lo += n
        return out

    def head_rows(ref, c, h):
        r0 = (c * NSA_KV_HEADS + h) * NSA_HEAD_DIM
        return ref[0, r0:r0 + NSA_HEAD_DIM, :]

    widx = lax.broadcasted_iota(jnp.int32, (GROUP_ROWS, wlen + PAGE_SIZE), 1)
    wvalid = (widx >= wlen + 1 - WINDOW) & (widx <= wlen)
    for h in range(NSA_KV_HEADS):
        q = (q_ref[0, h] * (NSA_HEAD_DIM ** -0.5)).astype(BF16)
        ow_ref[0, h] = attend(q, [head_rows(win_ref, 0, h), head_rows(wnew_ref, 0, h)],
                              [head_rows(win_ref, 1, h), head_rows(wnew_ref, 1, h)], bw_ref[h], wvalid)

    for h in range(NSA_KV_HEADS):
        for s in range(n_gather):
            page_copy(h, s, 0, kt_buf).wait()
            page_copy(h, s, 1, vt_buf).wait()

    keys = n_gather * PAGE_SIZE
    lane = lax.broadcasted_iota(jnp.int32, (GROUP_ROWS, keys + PAGE_SIZE), 1)
    slot = lane >> 7
    lane_half = (lane >> 6) & 1
    for h in range(NSA_KV_HEADS):
        want = jnp.full(lane.shape, -1, jnp.int32)
        for s in range(n_gather):
            want = jnp.where(slot == s, hf_ref[(b * NSA_KV_HEADS + h) * n_gather + s], want)
        svalid = (lane_half == want) | (lane == keys)
        q = (q_ref[0, h] * (NSA_HEAD_DIM ** -0.5)).astype(BF16)
        os_ref[0, h] = attend(q, [kt_buf[h], head_rows(snew_ref, 0, h)],
                              [vt_buf[h], head_rows(snew_ref, 1, h)], bs_ref[0, h], svalid)


def _sample_attn(pages, halves, q8, cache_t, snew_t, bias_sel, win_t, wnew_t, bias_w, n_gather):
    bsz = q8.shape[0]
    wlen = win_t.shape[2]
    keys = n_gather * PAGE_SIZE
    rows = 2 * NSA_KV_HEADS * NSA_HEAD_DIM
    qspec = pl.BlockSpec((1, NSA_KV_HEADS, GROUP_ROWS, NSA_HEAD_DIM), lambda b, pg, hf: (b, 0, 0, 0))
    newspec = pl.BlockSpec((1, rows, PAGE_SIZE), lambda b, pg, hf: (b, 0, 0))
    return pl.pallas_call(
        functools.partial(_sample_attn_kernel, n_gather=n_gather, wlen=wlen),
        grid_spec=pltpu.PrefetchScalarGridSpec(
            num_scalar_prefetch=2, grid=(bsz,),
            in_specs=[qspec,
                      pl.BlockSpec(memory_space=pl.ANY),
                      newspec,
                      pl.BlockSpec((1, NSA_KV_HEADS, GROUP_ROWS, keys + PAGE_SIZE), lambda b, pg, hf: (b, 0, 0, 0)),
                      pl.BlockSpec((1, rows, wlen), lambda b, pg, hf: (b, 0, 0)),
                      newspec,
                      pl.BlockSpec((NSA_KV_HEADS, GROUP_ROWS, wlen + PAGE_SIZE), lambda b, pg, hf: (0, 0, 0))],
            out_specs=[qspec, qspec],
            scratch_shapes=[pltpu.VMEM((NSA_KV_HEADS, NSA_HEAD_DIM, keys), F32),
                            pltpu.VMEM((NSA_KV_HEADS, NSA_HEAD_DIM, keys), F32),
                            pltpu.SemaphoreType.DMA(())]),
        out_shape=[jax.ShapeDtypeStruct(q8.shape, F32), jax.ShapeDtypeStruct(q8.shape, F32)],
        compiler_params=_cparams(("arbitrary",)),
        name="sample_slc_win_attn",
    )(pages, halves, q8, cache_t, snew_t, bias_sel, win_t, wnew_t, bias_w)


PAGES_PER_STEP = 16


def _paged_pq_kernel(pt_ref, cache_hbm, pos_ref, w_ref, o_ref, pbuf, tok_scr, sem):
    b = pl.program_id(0)
    g = pl.program_id(1)
    ng = pl.num_programs(1)
    lin = b * ng + g
    slot = lin % 2
    rows_per_page = 2 * NSA_KV_HEADS * NSA_HEAD_DIM
    chunks = PAGES_PER_STEP * (PAGE_SIZE // CMP_STRIDE)

    def page_copy(step, p, buf_slot):
        sb = step // ng
        sg = step - sb * ng
        page = pt_ref[sb, sg * PAGES_PER_STEP + p]
        return pltpu.make_async_copy(cache_hbm.at[pl.ds(pl.multiple_of(page * rows_per_page, rows_per_page),
                                                        rows_per_page)],
                                     pbuf.at[buf_slot, p], sem.at[buf_slot])

    def fetch(step, buf_slot):
        for p in range(PAGES_PER_STEP):
            page_copy(step, p, buf_slot).start()

    @pl.when(lin == 0)
    def _():
        fetch(0, 0)

    @pl.when(lin + 1 < pl.num_programs(0) * ng)
    def _():
        fetch(lin + 1, 1 - slot)

    for p in range(PAGES_PER_STEP):
        page_copy(lin, p, slot).wait()

    for c in range(2):
        head_chunks = []
        for pair in (2 * c, 2 * c + 1):
            for p in range(PAGES_PER_STEP):
                tok_scr[p * PAGE_SIZE:(p + 1) * PAGE_SIZE, :] = pbuf[slot, p, pair * LANE:(pair + 1) * LANE, :].T
            toks = [tok_scr[pl.ds(l, chunks, stride=CMP_STRIDE), :] for l in range(CMP_STRIDE)]
            for half in range(2):
                lanes = slice(half * NSA_HEAD_DIM, (half + 1) * NSA_HEAD_DIM)
                head_chunks.append(jnp.concatenate([t[:, lanes] for t in toks], axis=1))
        stacked = jnp.concatenate(head_chunks, axis=0)
        for part in range(2):
            lhs = (stacked + pos_ref[c, part:part + 1, :]).astype(BF16)
            out = _dot(lhs, w_ref[c, part])
            for hh in range(NSA_KV_HEADS):
                o_ref[0, c * NSA_KV_HEADS + hh, :, part * CMP_HIDDEN:(part + 1) * CMP_HIDDEN] = (
                    out[hh * chunks:(hh + 1) * chunks, :])


def _paged_pq(cache_t, page_table, cmp_pos, cmp_w1):
    db, n_pages = page_table.shape
    kdim = CMP_STRIDE * NSA_HEAD_DIM
    nch = n_pages * (PAGE_SIZE // CMP_STRIDE)
    tn = PAGES_PER_STEP * (PAGE_SIZE // CMP_STRIDE)
    rows_per_page = 2 * NSA_KV_HEADS * NSA_HEAD_DIM
    return pl.pallas_call(
        _paged_pq_kernel,
        grid_spec=pltpu.PrefetchScalarGridSpec(
            num_scalar_prefetch=1, grid=(db, n_pages // PAGES_PER_STEP),
            in_specs=[pl.BlockSpec(memory_space=pl.ANY),
                      pl.BlockSpec((2, 2, kdim), lambda b, g, pt: (0, 0, 0)),
                      pl.BlockSpec((2, 2, kdim, CMP_HIDDEN), lambda b, g, pt: (0, 0, 0, 0))],
            out_specs=pl.BlockSpec((1, 8, tn, 2 * CMP_HIDDEN), lambda b, g, pt: (b, 0, g, 0)),
            scratch_shapes=[pltpu.VMEM((2, PAGES_PER_STEP, rows_per_page, PAGE_SIZE), F32),
                            pltpu.VMEM((PAGES_PER_STEP * PAGE_SIZE, LANE), F32),
                            pltpu.SemaphoreType.DMA((2,))]),
        out_shape=jax.ShapeDtypeStruct((db, 8, nch, 2 * CMP_HIDDEN), F32),
        compiler_params=_cparams(("arbitrary", "arbitrary")),
        name="cmp_pq_paged",
    )(page_table, cache_t, cmp_pos.reshape(2, 2, kdim), cmp_w1.reshape(2, 2, kdim, CMP_HIDDEN).astype(BF16))


def _pad_rows(a, n):
    return jnp.concatenate([a, jnp.zeros((n - a.shape[0],) + a.shape[1:], a.dtype)], axis=0)


def _sample_mixer(x_sample, cache_cmp, cache_slc, cache_win, st_c, st_n, st_m, st_conv, page_table,
                  norm_g, wb, gate_bias, conv_w, ml_norm_g, cmp_pos, cmp_w1, cmp_w2, nsa_norm_g, w_out_b, rel_bias):
    db = x_sample.shape[0]
    n_pages = page_table.shape[1]
    past = n_pages * PAGE_SIZE
    tok = SAMPLE_TOKEN_ROWS
    x16 = _pad_rows(x_sample.reshape(db, D_MODEL), tok)
    u = _proj(x16, norm_g, wb, tok, PROJ_COLS)
    small = u[:, COL_SMALL:COL_SMALL + LANE]

    T = SAMPLE_SEQ_ROWS
    useq = jnp.zeros((db, T, 4 * D_ML), F32)
    useq = useq.at[:, T - CONV_W:T - 1, 0:2 * D_ML].set(st_conv)
    useq = useq.at[:, T - 1, :].set(u[:db, 0:4 * D_ML])
    sseq = jnp.zeros((db, T, LANE), F32).at[:, :T - 1, LANE - 1].set(1.0)
    sseq = sseq.at[:, T - 1, :].set(small[:db])
    y_seq, conv_n, c_n, n_n, m_n = _mlstm(
        useq.reshape(db * T, 4 * D_ML), sseq.reshape(db * T, LANE), jnp.zeros((db, CONV_W - 1, 2 * D_ML), F32),
        st_c, st_n, st_m, conv_w, gate_bias, ml_norm_g, db, T, T)
    y_ml = y_seq.reshape(db, T, D_ML)[:, T - 1]

    n_pool = cache_cmp.shape[0]
    cmp_t = cache_cmp.transpose(0, 2, 3, 4, 1).reshape(n_pool * 2 * D_KV, PAGE_SIZE)
    kvc = _cmp_hid(_paged_pq(cmp_t, page_table, cmp_pos, cmp_w1), cmp_w2)
    nch = past // CMP_STRIDE
    n_cmp = (past + 1) // CMP_STRIDE - CMP_BLOCK // CMP_STRIDE + 1
    n_slc = -(-(past + 1) // SLC_BLOCK)
    q = u[:db, COL_QNSA:COL_QNSA + D_NSA].reshape(db, NSA_KV_HEADS, NSA_GROUP, NSA_HEAD_DIM)
    q8 = jnp.concatenate([q, jnp.zeros_like(q)], axis=2)
    bd = _bias_by_distance(rel_bias, past + 1).reshape(NSA_KV_HEADS, NSA_GROUP, past + 1)
    pad_g = lambda a: jnp.concatenate([a, jnp.zeros_like(a)], axis=1)
    dist_c = np.clip(past - (np.arange(nch) * CMP_STRIDE + CMP_BLOCK - 1), 0, None)
    o_c8, sel = _sample_cattn(q8, kvc, pad_g(bd[:, :, dist_c]), n_cmp, n_slc)

    mask = sel[:, :, 0, :n_slc] > 0.5
    idx = jnp.sort(jnp.where(mask, jnp.arange(n_slc, dtype=jnp.int32), jnp.int32(1 << 20)), axis=-1)
    n_gather = SLC_TOPN - 1
    idx = idx[..., :n_gather]
    pages_per_block = PAGE_SIZE // SLC_BLOCK
    logical_page = idx // pages_per_block
    pages = jnp.take_along_axis(page_table[:, None, :], logical_page, axis=2).reshape(-1).astype(jnp.int32)
    halves = (idx % pages_per_block).reshape(-1).astype(jnp.int32)
    kpos = (logical_page[..., None] * PAGE_SIZE + jnp.arange(PAGE_SIZE, dtype=jnp.int32)).reshape(db, NSA_KV_HEADS, -1)
    hh = jnp.arange(NSA_KV_HEADS)[None, :, None, None]
    gg = jnp.arange(NSA_GROUP)[None, None, :, None]
    bias_sel = bd[hh, gg, (past - kpos)[:, :, None, :]]
    bias_sel = jnp.concatenate([bias_sel, jnp.broadcast_to(bd[None, :, :, 0:1], (db, NSA_KV_HEADS, NSA_GROUP, 1)),
                                jnp.zeros((db, NSA_KV_HEADS, NSA_GROUP, PAGE_SIZE - 1), F32)], axis=-1)
    bias_sel = jnp.concatenate([bias_sel, jnp.zeros_like(bias_sel)], axis=2)
    wlen = cache_win.shape[1]
    dist_w = np.clip(wlen - np.arange(wlen + PAGE_SIZE), 0, None)
    kvs_new = _kv_from_per_head(u[:db, COL_KVS:COL_KVS + 2 * D_KV])
    kvw_new = _kv_from_per_head(u[:db, COL_KVW:COL_KVW + 2 * D_KV])
    lane_pad = lambda a: jnp.pad(a[:, :, None], ((0, 0), (0, 0), (0, PAGE_SIZE - 1)))
    slc_t = cache_slc.transpose(0, 2, 3, 4, 1).reshape(n_pool * 2 * D_KV, PAGE_SIZE)
    win_t = cache_win.transpose(0, 2, 3, 4, 1).reshape(db, 2 * D_KV, wlen)
    o_s8, o_w8 = _sample_attn(pages, halves, q8, slc_t, lane_pad(kvs_new), bias_sel, win_t, lane_pad(kvw_new),
                              pad_g(bd[:, :, dist_w]), n_gather)
    win2d = cache_win.reshape(db, wlen, 2 * D_KV)

    heads = lambda o: _pad_rows(o[:, :, :NSA_GROUP, :].reshape(db, D_NSA), tok)
    y = _outproj(_pad_rows(y_ml, tok), heads(o_c8), heads(o_s8), heads(o_w8), small, nsa_norm_g, w_out_b, x16, tok,
                 PROJ_COLS)
    kvshape = (1, db, 1, 2, NSA_KV_HEADS, NSA_HEAD_DIM)
    new_win = jnp.concatenate([win2d[:, 1:], kvw_new[:, None, :]], axis=1)
    states = (u[:db, COL_KVC:COL_KVC + 2 * D_KV].reshape(kvshape), kvs_new.reshape(kvshape),
              new_win.reshape((1, db, wlen, 2, NSA_KV_HEADS, NSA_HEAD_DIM)),
              c_n[None], n_n[None], m_n[None], conv_n[None])
    return y, states


def kernel(x_prompt, x_sample, cache_cmp_kv, cache_slc_kv, cache_win_kv, state_mlstm_C, state_mlstm_n,
           state_mlstm_m, state_conv, page_table, rel_bias, norm_mix_g, w_in, b_ig, b_fg, conv_w, ml_norm_g,
           cmp_pos, cmp_w1, cmp_w2, nsa_norm_g, w_out, norm_ffn_g, w_router_grp, b_router_grp, w_router_exp,
           b_router_exp, w_gate, w_up, w_down, norm_final_g):
    B, S, D = x_prompt.shape
    wb = _reorder_w_in(w_in[0])
    gate_bias = jnp.zeros((1, LANE), F32).at[0, 0:ML_HEADS].set(b_ig[0]).at[0, ML_HEADS:2 * ML_HEADS].set(b_fg[0])
    w_out_b = w_out[0].astype(BF16)
    yp, st_p = _prompt_mixer(x_prompt, norm_mix_g, wb, gate_bias, conv_w[0], ml_norm_g, cmp_pos[0], cmp_w1[0],
                             cmp_w2[0], nsa_norm_g, w_out_b, rel_bias)
    ys, st_s = _sample_mixer(x_sample, cache_cmp_kv[0], cache_slc_kv[0], cache_win_kv[0], state_mlstm_C[0],
                             state_mlstm_n[0], state_mlstm_m[0], state_conv[0], page_table, norm_mix_g, wb, gate_bias,
                             conv_w[0], ml_norm_g, cmp_pos[0], cmp_w1[0], cmp_w2[0], nsa_norm_g, w_out_b, rel_bias)
    moe_w = (norm_ffn_g, w_router_grp[0], b_router_grp[0], w_router_exp[0], b_router_exp[0],
             w_gate[0], w_up[0], w_down[0], norm_final_g[None, :])
    DB, L, _ = x_sample.shape
    out_p = _moe_final(yp.reshape(B * S, D), *moe_w, MOE_TOKEN_ROWS, MOE_EXPERT_ROWS).reshape(B, S, D)
    out_s = _moe_final(_pad_rows(ys, SAMPLE_MOE_ROWS), *moe_w, SAMPLE_MOE_ROWS, SAMPLE_EXPERT_ROWS)[:DB]
    out_s = out_s.reshape(DB, L, D)
    outs = [out_p, out_s]
    for a, b in zip(st_p, st_s):
        outs += [a, b]
    return tuple(outs)


def _prompt_mixer(x_prompt, norm_g, wb, gate_bias, conv_w, ml_norm_g, cmp_pos, cmp_w1, cmp_w2, nsa_norm_g,
                  w_out_b, rel_bias):
    B, S, D = x_prompt.shape
    x2d = x_prompt.reshape(B * S, D)
    tm = min(PROJ_ROWS, S)
    u, kvt_c, kvt_s, kvt_w = _proj(x2d, norm_g, wb, tm, PROJ_COLS, seq=S)
    small = u[:, COL_SMALL:COL_SMALL + LANE]
    y_ml, conv_n, c_n, n_n, m_n = _mlstm(
        u, small, jnp.zeros((B, CONV_W - 1, 2 * D_ML), F32),
        jnp.zeros((B, ML_HEADS, ML_HEAD_DIM, ML_HEAD_DIM), F32), jnp.zeros((B, ML_HEADS, ML_HEAD_DIM), F32),
        jnp.full((B, ML_HEADS), -jnp.inf, F32), conv_w, gate_bias, ml_norm_g, B, S, ML_CHUNK_ROWS)
    kvc = _compress(u, COL_KVC // (2 * D_KV), B, S, cmp_pos, cmp_w1, cmp_w2, min(S, CMP_TOKENS))
    o_c, sel_t = _cattn(u, kvc, rel_bias, B, S, min(S, CMP_QUERY_ROWS))
    o_s = _flash(u, COL_KVS, rel_bias, sel_t, B, S)
    o_w = _flash(u, COL_KVW, rel_bias, None, B, S)
    y = _outproj(y_ml, o_c, o_s, o_w, small, nsa_norm_g, w_out_b, x2d, tm, PROJ_COLS)
    win = min(WINDOW, S)
    rows = lambda kvt: kvt.transpose(0, 4, 1, 2, 3)[None]
    states = (rows(kvt_c), rows(kvt_s), rows(kvt_w[..., S - win:]),
              c_n[None], n_n[None], m_n[None], conv_n[None])
    return y.reshape(B, S, D), states
```

```python
import functools
import math

import numpy as np
import jax
import jax.numpy as jnp
from jax import lax
from jax.experimental import pallas as pl
from jax.experimental.pallas import tpu as pltpu

F32 = jnp.float32
BF16 = jnp.bfloat16

D_MODEL = 2048
ML_HEADS = 4
ML_HEAD_DIM = 256
D_ML = 1024
CONV_W = 4
NSA_HEADS = 16
NSA_HEAD_DIM = 64
D_NSA = 1024
NSA_KV_HEADS = 4
NSA_GROUP = 4
D_KV = 256
CMP_BLOCK = 32
CMP_STRIDE = 16
CMP_HIDDEN = 256
SLC_BLOCK = 64
SLC_TOPN = 16
WINDOW = 512
N_BUCKETS = 32
MAX_DISTANCE = 2048
N_GROUPS = 4
EXPERTS_PER_GROUP = 8
N_EXPERTS = 32
D_EXPERT = 512
PAGE_SIZE = 128
EPS = 1e-6
NEG_BIG = -1e30
FORCE_SCORE = 1e4

LANE = 128
COL_QML, COL_KML, COL_VML, COL_OML = 0, 1024, 2048, 3072
COL_QNSA = 4096
COL_KVC, COL_KVS, COL_KVW = 5120, 5632, 6144
COL_SMALL = 6656
N_PROJ = 7168
VMEM_LIMIT = 56 * 1024 * 1024

PROJ_ROWS = 1024
PROJ_COLS = 2 * D_KV
ML_CHUNK_ROWS = 256
CMP_TOKENS = 4096
CMP_QUERY_ROWS = 256
MOE_TOKEN_ROWS = 256
MOE_EXPERT_ROWS = 128
SAMPLE_TOKEN_ROWS = 16
SAMPLE_SEQ_ROWS = 128
SAMPLE_MOE_ROWS = 128
SAMPLE_EXPERT_ROWS = 16


def _cparams(sem, vmem=VMEM_LIMIT):
    return pltpu.CompilerParams(dimension_semantics=sem, vmem_limit_bytes=vmem)


def _split2(x):
    hi = x.astype(BF16)
    lo = (x - hi.astype(F32)).astype(BF16)
    return hi, lo


def _split3(x):
    hi = x.astype(BF16)
    r = x - hi.astype(F32)
    mid = r.astype(BF16)
    lo = (r - mid.astype(F32)).astype(BF16)
    return hi, mid, lo


def _dot(a, b):
    return jnp.dot(a, b, preferred_element_type=F32)


def _dot_nt(a, b):
    return lax.dot_general(a, b, (((1,), (1,)), ((), ())), preferred_element_type=F32)


def _dot_tn(a, b):
    return lax.dot_general(a, b, (((0,), (0,)), ((), ())), preferred_element_type=F32)


def _proj_kernel(x_ref, g_ref, w_ref, o_ref, *rest, kv_blocks):
    h_scr = rest[-1]
    j = pl.program_id(1)

    @pl.when(j == 0)
    def _():
        x = x_ref[...]
        ms = jnp.mean(x * x, axis=-1, keepdims=True)
        h_scr[...] = (x * lax.rsqrt(ms + EPS) * g_ref[...]).astype(BF16)

    res = _dot(h_scr[...], w_ref[...])
    o_ref[...] = res

    for (block, per_head), t_ref in zip(kv_blocks, rest[:-1]):
        @pl.when(j == block)
        def _():
            res_t = res.T
            for c in range(2):
                for hh in range(NSA_KV_HEADS):
                    piece = hh * 2 + c if per_head else c * NSA_KV_HEADS + hh
                    t_ref[0, c, hh] = res_t[piece * NSA_HEAD_DIM:(piece + 1) * NSA_HEAD_DIM, :]


def _proj(x2d, g, wb, tm, tn, seq=None):
    n, d = x2d.shape
    nc = wb.shape[1]
    kv_blocks = ()
    out_specs = [pl.BlockSpec((tm, tn), lambda i, j: (i, j))]
    out_shape = [jax.ShapeDtypeStruct((n, nc), F32)]
    if seq is not None:
        assert tn == 2 * D_KV and seq % tm == 0
        kv_blocks = ((COL_KVC // tn, False), (COL_KVS // tn, True), (COL_KVW // tn, True))
        per_seq = seq // tm
        for _ in kv_blocks:
            out_specs.append(pl.BlockSpec((1, 2, NSA_KV_HEADS, NSA_HEAD_DIM, tm),
                                          lambda i, j: (i // per_seq, 0, 0, 0, i % per_seq)))
            out_shape.append(jax.ShapeDtypeStruct((n // seq, 2, NSA_KV_HEADS, NSA_HEAD_DIM, seq), F32))
    outs = pl.pallas_call(
        functools.partial(_proj_kernel, kv_blocks=kv_blocks),
        grid=(n // tm, nc // tn),
        in_specs=[pl.BlockSpec((tm, d), lambda i, j: (i, 0)),
                  pl.BlockSpec((1, d), lambda i, j: (0, 0)),
                  pl.BlockSpec((d, tn), lambda i, j: (0, j))],
        out_specs=out_specs,
        out_shape=out_shape,
        scratch_shapes=[pltpu.VMEM((tm, d), BF16)],
        compiler_params=_cparams(("arbitrary", "arbitrary")),
        name="proj",
    )(x2d, g, wb)
    return outs if seq is not None else outs[0]


def _reorder_w_in(w_in):
    w_in = w_in.astype(BF16)
    big = w_in[:, :4 * D_ML]
    small_a = w_in[:, 4 * D_ML:4 * D_ML + 2 * ML_HEADS]
    rest = w_in[:, 4 * D_ML + 2 * ML_HEADS:]
    q_and_cmp = rest[:, :D_NSA + 2 * D_KV]
    gate = rest[:, D_NSA + 6 * D_KV:]
    d = w_in.shape[0]

    def per_head(w):
        return w.reshape(d, 2, NSA_KV_HEADS, NSA_HEAD_DIM).transpose(0, 2, 1, 3).reshape(d, 2 * D_KV)

    kv_s = per_head(rest[:, D_NSA + 2 * D_KV:D_NSA + 4 * D_KV])
    kv_w = per_head(rest[:, D_NSA + 4 * D_KV:D_NSA + 6 * D_KV])
    pad = jnp.zeros((d, N_PROJ - COL_SMALL - 2 * ML_HEADS - 3 * NSA_HEADS), w_in.dtype)
    return jnp.concatenate([big, q_and_cmp, kv_s, kv_w, small_a, gate, pad], axis=1).astype(BF16)


def _kv_from_per_head(kv2d):
    n = kv2d.shape[0]
    return kv2d.reshape(n, NSA_KV_HEADS, 2, NSA_HEAD_DIM).transpose(0, 2, 1, 3).reshape(n, 2 * D_KV)


def _log_sigmoid(x):
    return jnp.minimum(x, 0.0) - jnp.log1p(jnp.exp(-jnp.abs(x)))


def _mlstm_kernel(q_ref, k_ref, v_ref, o_ref, s_ref, cb_ref, c0_ref, n0_ref, m0_ref,
                  cw_ref, gb_ref, ng_ref,
                  y_ref, cbo_ref, co_ref, no_ref, mo_ref,
                  ext_scr, c_scr, n_scr, m_scr, *, T):
    c = pl.program_id(1)
    nc = pl.num_programs(1)

    @pl.when(c == 0)
    def _():
        ext_scr[0:8, :] = jnp.zeros((8, 2 * D_ML), F32)
        ext_scr[5:8, :] = cb_ref[0]
        c_scr[...] = c0_ref[0]
        n_scr[...] = n0_ref[0]
        m_scr[...] = m0_ref[0]

    ext_scr[8:8 + T, 0:D_ML] = q_ref[...]
    ext_scr[8:8 + T, D_ML:2 * D_ML] = k_ref[...]
    conv = ext_scr[5:5 + T, :] * cw_ref[0:1, :]
    for j in range(1, CONV_W):
        conv = conv + ext_scr[5 + j:5 + j + T, :] * cw_ref[j:j + 1, :]
    tail = ext_scr[8 + T - 3:8 + T, :]
    ext_scr[5:8, :] = tail
    cbo_ref[0] = tail
    qk = conv * jax.nn.sigmoid(conv)

    pre = s_ref[...] + gb_ref[...]
    col = lax.broadcasted_iota(jnp.int32, pre.shape, 1)
    padrow = s_ref[:, LANE - 1:LANE] > 0.5
    gates = jnp.where(col < ML_HEADS, pre, _log_sigmoid(pre))
    gates = jnp.where(padrow, jnp.where(col < ML_HEADS, NEG_BIG, 0.0), gates)
    g_r = gates.T
    ti = lax.broadcasted_iota(jnp.int32, (T, T), 0)
    si = lax.broadcasted_iota(jnp.int32, (T, T), 1)
    upper = (ti <= si).astype(BF16)
    g_fin = jnp.where(lax.broadcasted_iota(jnp.int32, g_r.shape, 0) < ML_HEADS, 0.0, g_r)
    hi, mid, lo = _split3(g_fin)
    cum_r = _dot(hi, upper) + _dot(mid, upper) + _dot(lo, upper)
    rowi = lax.broadcasted_iota(jnp.int32, g_r.shape, 0)
    a_r = jnp.where(rowi < ML_HEADS, g_r, cum_r)
    a_c = a_r.T
    causal = si <= ti

    for h in range(ML_HEADS):
        sl = slice(h * ML_HEAD_DIM, (h + 1) * ML_HEAD_DIM)
        q = qk[:, h * ML_HEAD_DIM:(h + 1) * ML_HEAD_DIM]
        k = qk[:, D_ML + h * ML_HEAD_DIM:D_ML + (h + 1) * ML_HEAD_DIM] * (ML_HEAD_DIM ** -0.5)
        v = v_ref[:, sl]
        ig_r = a_r[h:h + 1, :]
        b_r = a_r[ML_HEADS + h:ML_HEADS + h + 1, :]
        ig_c = a_c[:, h:h + 1]
        b_c = a_c[:, ML_HEADS + h:ML_HEADS + h + 1]
        m_prev = m_scr[h:h + 1, 0:1]
        logd = jnp.where(causal, b_c - b_r + ig_r, -jnp.inf)
        inter = b_c + m_prev
        m_t = jnp.maximum(inter, jnp.max(logd, axis=1, keepdims=True))
        w_intra = jnp.exp(logd - m_t)
        w_inter = jnp.exp(inter - m_t)
        qb = q.astype(BF16)
        kb = k.astype(BF16)
        vb = v.astype(BF16)
        sc = _dot_nt(qb, kb) * w_intra
        cmat = c_scr[h]
        nvec = n_scr[h:h + 1, :]
        num = _dot(sc.astype(BF16), vb) + w_inter * _dot(qb, cmat.astype(BF16))
        qn = jnp.sum(qb.astype(F32) * nvec.astype(BF16).astype(F32), axis=1, keepdims=True)
        den = jnp.sum(sc, axis=1, keepdims=True) + w_inter * qn
        hh = num / jnp.maximum(jnp.abs(den), jnp.exp(-m_t))
        m_new = m_t[T - 1:T, :]
        b_last = b_c[T - 1:T, :]
        w_s = jnp.exp(b_last - b_c + ig_c - m_new)
        decay = jnp.exp(b_last + m_prev - m_new)
        kw = k * w_s
        c_new = decay * cmat + _dot_tn(kw.astype(BF16), vb)
        n_new = decay * nvec + jnp.sum(kw, axis=0, keepdims=True)
        c_scr[h] = c_new
        n_scr[h:h + 1, :] = n_new
        m_scr[h:h + 1, :] = jnp.broadcast_to(m_new, (1, LANE))
        hn = hh * lax.rsqrt(jnp.mean(hh * hh, axis=1, keepdims=True) + EPS) * ng_ref[:, sl]
        y_ref[:, sl] = (hn * jax.nn.sigmoid(o_ref[:, sl])).astype(y_ref.dtype)

    @pl.when(c == nc - 1)
    def _():
        co_ref[0] = c_scr[...]
        no_ref[0] = n_scr[...]
        mo_ref[0] = m_scr[...]


def _mlstm(u, small, conv_buf, c0, n0, m0, conv_w, gate_bias, norm_g, batch, seq, T):
    nc = seq // T
    cb = D_ML // 1024
    m0b = jnp.broadcast_to(m0[:, :, None], (batch, ML_HEADS, LANE))
    m0b = jnp.concatenate([m0b, jnp.zeros((batch, 8 - ML_HEADS, LANE), F32)], axis=1)
    n0p = jnp.concatenate([n0, jnp.zeros((batch, 8 - ML_HEADS, ML_HEAD_DIM), F32)], axis=1)
    row = lambda b, c: (b * nc + c, 0)
    outs = pl.pallas_call(
        functools.partial(_mlstm_kernel, T=T),
        grid=(batch, nc),
        in_specs=[pl.BlockSpec((T, D_ML), lambda b, c: (b * nc + c, COL_QML // D_ML)),
                  pl.BlockSpec((T, D_ML), lambda b, c: (b * nc + c, COL_KML // D_ML)),
                  pl.BlockSpec((T, D_ML), lambda b, c: (b * nc + c, COL_VML // D_ML)),
                  pl.BlockSpec((T, D_ML), lambda b, c: (b * nc + c, COL_OML // D_ML)),
                  pl.BlockSpec((T, LANE), row),
                  pl.BlockSpec((1, CONV_W - 1, 2 * D_ML), lambda b, c: (b, 0, 0)),
                  pl.BlockSpec((1, ML_HEADS, ML_HEAD_DIM, ML_HEAD_DIM), lambda b, c: (b, 0, 0, 0)),
                  pl.BlockSpec((1, 8, ML_HEAD_DIM), lambda b, c: (b, 0, 0)),
                  pl.BlockSpec((1, 8, LANE), lambda b, c: (b, 0, 0)),
                  pl.BlockSpec((CONV_W, 2 * D_ML), lambda b, c: (0, 0)),
                  pl.BlockSpec((1, LANE), lambda b, c: (0, 0)),
                  pl.BlockSpec((1, D_ML), lambda b, c: (0, 0))],
        out_specs=[pl.BlockSpec((T, D_ML), row),
                   pl.BlockSpec((1, CONV_W - 1, 2 * D_ML), lambda b, c: (b, 0, 0)),
                   pl.BlockSpec((1, ML_HEADS, ML_HEAD_DIM, ML_HEAD_DIM), lambda b, c: (b, 0, 0, 0)),
                   pl.BlockSpec((1, 8, ML_HEAD_DIM), lambda b, c: (b, 0, 0)),
                   pl.BlockSpec((1, 8, LANE), lambda b, c: (b, 0, 0))],
        out_shape=[jax.ShapeDtypeStruct((batch * seq, D_ML), BF16),
                   jax.ShapeDtypeStruct((batch, CONV_W - 1, 2 * D_ML), F32),
                   jax.ShapeDtypeStruct((batch, ML_HEADS, ML_HEAD_DIM, ML_HEAD_DIM), F32),
                   jax.ShapeDtypeStruct((batch, 8, ML_HEAD_DIM), F32),
                   jax.ShapeDtypeStruct((batch, 8, LANE), F32)],
        scratch_shapes=[pltpu.VMEM((8 + T, 2 * D_ML), F32),
                        pltpu.VMEM((ML_HEADS, ML_HEAD_DIM, ML_HEAD_DIM), F32),
                        pltpu.VMEM((8, ML_HEAD_DIM), F32),
                        pltpu.VMEM((8, LANE), F32)],
        compiler_params=_cparams(("arbitrary", "arbitrary")),
        name="mlstm",
    )(u, u, u, u, small, conv_buf, c0, n0p, m0b, conv_w, gate_bias, norm_g)
    y, cbo, co, no, mo = outs
    return y, cbo, co, no[:, :ML_HEADS], mo[:, :ML_HEADS, 0]


def _bucket_np(dist):
    n = np.maximum(dist, 0)
    max_exact = N_BUCKETS // 2
    nf = np.maximum(n, 1).astype(np.float64)
    large = max_exact + (np.log(nf / max_exact) / math.log(MAX_DISTANCE / max_exact)
                         * (N_BUCKETS - max_exact)).astype(np.int64)
    return np.where(n < max_exact, n, np.minimum(large, N_BUCKETS - 1)).astype(np.int32)


def _bias_by_distance(rel_bias, n):
    return rel_bias.astype(F32)[_bucket_np(np.arange(n))].T


def _overlap_t(n_cmp, nch, n_slc):
    c0 = np.arange(nch) * CMP_STRIDE
    s0 = np.arange(n_slc) * SLC_BLOCK
    ov = np.minimum(c0[None, :] + CMP_BLOCK, s0[:, None] + SLC_BLOCK) - np.maximum(c0[None, :], s0[:, None])
    ov = np.clip(ov, 0, None).astype(np.float32) / CMP_BLOCK
    ov[:, n_cmp:] = 0.0
    return jnp.asarray(ov, BF16)


def _pq_kernel(x0_ref, x1_ref, x2_ref, x3_ref, pos_ref, w_ref, o_ref, *, rows):
    for pair, x_ref in enumerate((x0_ref, x1_ref, x2_ref, x3_ref)):
        toks = [x_ref[pl.ds(l, rows, stride=CMP_STRIDE), :] for l in range(CMP_STRIDE)]
        for half in range(2):
            ch = 2 * pair + half
            c = ch // NSA_KV_HEADS
            lanes = slice(half * NSA_HEAD_DIM, (half + 1) * NSA_HEAD_DIM)
            chunk = jnp.concatenate([t[:, lanes] for t in toks], axis=1)
            for part in range(2):
                lhs = (chunk + pos_ref[c, part:part + 1, :]).astype(BF16)
                o_ref[0, ch, :, part * CMP_HIDDEN:(part + 1) * CMP_HIDDEN] = _dot(lhs, w_ref[c, part])


def _hid_kernel(pq_ref, w2_ref, o_ref, *, nch):
    p = pq_ref[0, 0, :, 0:CMP_HIDDEN]
    q = pltpu.roll(pq_ref[0, 0, :, CMP_HIDDEN:2 * CMP_HIDDEN], nch - 1, 0)
    hid = jax.nn.gelu(p + q, approximate=True)
    o_ref[0, 0] = _dot(hid.astype(BF16), w2_ref[0])


def _compress(kv2d, col_block, bsz, seq, cmp_pos, cmp_w1, cmp_w2, tt):
    nch = seq // CMP_STRIDE
    kdim = CMP_STRIDE * NSA_HEAD_DIM
    nt = seq // tt
    tn = tt // CMP_STRIDE
    w1 = cmp_w1.reshape(2, 2, kdim, CMP_HIDDEN).astype(BF16)
    pos = cmp_pos.reshape(2, 2, kdim)
    pq = pl.pallas_call(
        functools.partial(_pq_kernel, rows=tn),
        grid=(bsz, nt),
        in_specs=[pl.BlockSpec((tt, LANE), functools.partial(lambda b, i, k: (b * nt + i, col_block * 4 + k), k=k))
                  for k in range(4)] +
                 [pl.BlockSpec((2, 2, kdim), lambda b, i: (0, 0, 0)),
                  pl.BlockSpec((2, 2, kdim, CMP_HIDDEN), lambda b, i: (0, 0, 0, 0))],
        out_specs=pl.BlockSpec((1, 8, tn, 2 * CMP_HIDDEN), lambda b, i: (b, 0, i, 0)),
        out_shape=jax.ShapeDtypeStruct((bsz, 8, nch, 2 * CMP_HIDDEN), F32),
        compiler_params=_cparams(("arbitrary", "arbitrary")),
        name="cmp_pq",
    )(kv2d, kv2d, kv2d, kv2d, pos, w1)
    return _cmp_hid(pq, cmp_w2)


def _cmp_hid(pq, cmp_w2):
    bsz, _, nch, _ = pq.shape
    return pl.pallas_call(
        functools.partial(_hid_kernel, nch=nch),
        grid=(bsz, 8),
        in_specs=[pl.BlockSpec((1, 1, nch, 2 * CMP_HIDDEN), lambda b, c: (b, c, 0, 0)),
                  pl.BlockSpec((1, CMP_HIDDEN, NSA_HEAD_DIM), lambda b, c: (c // NSA_KV_HEADS, 0, 0))],
        out_specs=pl.BlockSpec((1, 1, nch, NSA_HEAD_DIM), lambda b, c: (b, c, 0, 0)),
        out_shape=jax.ShapeDtypeStruct((bsz, 8, nch, NSA_HEAD_DIM), F32),
        compiler_params=_cparams(("arbitrary", "arbitrary")),
        name="cmp_hid",
    )(pq, cmp_w2.astype(BF16))


def _top_n_mask(score_t, blk, n_rows):
    rank = jnp.zeros(score_t.shape, jnp.int32)
    for i in range(n_rows):
        row = score_t[i:i + 1, :]
        beats = (row > score_t) | ((row == score_t) & (blk > i))
        rank = rank + beats.astype(jnp.int32)
    return (rank < min(SLC_TOPN, n_rows)).astype(F32)


def _bias_from_buckets(bucket, table_ref, first_head):
    biases = [jnp.zeros(bucket.shape, F32) for _ in range(NSA_GROUP)]
    for k in range(N_BUCKETS):
        hit = bucket == k
        biases = [jnp.where(hit, table_ref[k, first_head + g], biases[g]) for g in range(NSA_GROUP)]
    return biases


def _cattn_kernel(tab_ref, q_ref, kc_ref, vc_ref, bk_ref, ovt_ref, o_ref, sel_ref, bias_scr, strip_scr,
                  *, tq, nch, n_cmp, n_slc):
    h = pl.program_id(0)
    i = pl.program_id(1)

    bands = tq // CMP_STRIDE
    for g, strip in enumerate(_bias_from_buckets(bk_ref[0], tab_ref, h * NSA_GROUP)):
        strip_scr[g] = strip
        for a in range(bands):
            off = bands - 1 - a
            bias_scr[g, a * CMP_STRIDE:(a + 1) * CMP_STRIDE, :] = strip_scr[g, :, off:off + nch]

    t = i * tq + lax.broadcasted_iota(jnp.int32, (tq, nch), 0)
    n = lax.broadcasted_iota(jnp.int32, (tq, nch), 1)
    valid = (t - CMP_STRIDE * n - (CMP_BLOCK - 1) >= 0) & (n < n_cmp)
    blk = lax.broadcasted_iota(jnp.int32, (n_slc, tq), 0)
    cur = (i * tq + lax.broadcasted_iota(jnp.int32, (n_slc, tq), 1)) // SLC_BLOCK
    forced = (blk == 0) | (blk == cur) | (blk == cur - 1)
    for bb in range(q_ref.shape[0]):
        kc = kc_ref[bb, 0].astype(BF16)
        vc = vc_ref[bb, 0].astype(BF16)
        pc = jnp.zeros((tq, nch), F32)
        for g in range(NSA_GROUP):
            sl = slice(g * NSA_HEAD_DIM, (g + 1) * NSA_HEAD_DIM)
            qg = (q_ref[bb, :, sl] * (NSA_HEAD_DIM ** -0.5)).astype(BF16)
            s = _dot_nt(qg, kc) + bias_scr[g]
            s = jnp.where(valid, s, NEG_BIG)
            e = jnp.exp(s - jnp.max(s, axis=1, keepdims=True))
            p = jnp.where(valid, e / jnp.sum(e, axis=1, keepdims=True), 0.0)
            o_ref[bb, :, sl] = _dot(p.astype(BF16), vc)
            pc = pc + p
        score_t = _dot_nt(ovt_ref[...], pc.astype(BF16))
        score_t = jnp.where(forced, FORCE_SCORE, score_t)
        score_t = jnp.where(blk > cur, -FORCE_SCORE, score_t)
        sel_ref[bb, 0] = _top_n_mask(score_t, blk, n_slc)


def _cattn(u, kvc, rel_bias, batch, seq, tq):
    nch = kvc.shape[2]
    n_cmp = nch - 1
    n_slc = seq // SLC_BLOCK
    ni = seq // tq
    ovt = _overlap_t(n_cmp, nch, n_slc)
    qcol = COL_QNSA // (NSA_GROUP * NSA_HEAD_DIM)
    bands = tq // CMP_STRIDE
    strip_w = -(-(nch + bands - 1) // LANE) * LANE
    dist = (tq * np.arange(ni)[:, None, None] + np.arange(CMP_STRIDE)[None, :, None]
            - CMP_STRIDE * (np.arange(strip_w)[None, None, :] - (bands - 1)) - (CMP_BLOCK - 1))
    buckets = jnp.asarray(_bucket_np(dist))
    o_c, sel_t = pl.pallas_call(
        functools.partial(_cattn_kernel, tq=tq, nch=nch, n_cmp=n_cmp, n_slc=n_slc),
        grid=(NSA_KV_HEADS, ni),
        in_specs=[pl.BlockSpec(memory_space=pltpu.SMEM),
                  pl.BlockSpec((batch, tq, NSA_GROUP * NSA_HEAD_DIM), lambda h, i: (0, i, qcol + h)),
                  pl.BlockSpec((batch, 1, nch, NSA_HEAD_DIM), lambda h, i: (0, h, 0, 0)),
                  pl.BlockSpec((batch, 1, nch, NSA_HEAD_DIM), lambda h, i: (0, NSA_KV_HEADS + h, 0, 0)),
                  pl.BlockSpec((1, CMP_STRIDE, strip_w), lambda h, i: (i, 0, 0)),
                  pl.BlockSpec((n_slc, nch), lambda h, i: (0, 0))],
        out_specs=[pl.BlockSpec((batch, tq, NSA_GROUP * NSA_HEAD_DIM), lambda h, i: (0, i, h)),
                   pl.BlockSpec((batch, 1, n_slc, tq), lambda h, i: (0, h, 0, i))],
        out_shape=[jax.ShapeDtypeStruct((batch, seq, D_NSA), F32),
                   jax.ShapeDtypeStruct((batch, NSA_KV_HEADS, n_slc, seq), F32)],
        scratch_shapes=[pltpu.VMEM((NSA_GROUP, tq, nch), F32), pltpu.VMEM((NSA_GROUP, CMP_STRIDE, strip_w), F32)],
        compiler_params=_cparams(("arbitrary", "arbitrary")),
        name="cmp_attn",
    )(rel_bias.astype(F32), u.reshape(batch, seq, u.shape[1]), kvc, kvc, buckets, ovt)
    return o_c.reshape(batch * seq, D_NSA), sel_t


TQ = 128
FLASH_GROUP = 8


def _flash_kernel(tab_ref, q_ref, kv_ref, bk_ref, *rest, selected, n_delta):
    if selected:
        sel_ref, o_ref, band_scr, qs_scr, m_scr, l_scr, acc_scr = rest
    else:
        o_ref, band_scr, qs_scr, m_scr, l_scr, acc_scr = rest
    h = pl.program_id(0)
    i = pl.program_id(1)
    n_batch = q_ref.shape[0]

    @pl.when(i == 0)
    def _():
        def fill(d, carry):
            for g, bias in enumerate(_bias_from_buckets(bk_ref[d], tab_ref, h * NSA_GROUP)):
                band_scr[d, :, g * TQ:(g + 1) * TQ] = bias
            return carry
        lax.fori_loop(0, n_delta, fill, 0)

    for bb in range(n_batch):
        for g in range(NSA_GROUP):
            qg = (q_ref[bb, :, g * NSA_HEAD_DIM:(g + 1) * NSA_HEAD_DIM] * (NSA_HEAD_DIM ** -0.5)).astype(BF16)
            qs_scr[bb, g * TQ:(g + 1) * TQ, :] = jnp.concatenate([qg, jnp.zeros_like(qg)], axis=1)
    is_key_lane = lax.broadcasted_iota(jnp.int32, (TQ, LANE), 1) < NSA_HEAD_DIM
    m_scr[...] = jnp.full(m_scr.shape, 0.5 * NEG_BIG, F32)
    l_scr[...] = jnp.zeros(l_scr.shape, F32)
    acc_scr[...] = jnp.zeros(acc_scr.shape, F32)
    key = lax.broadcasted_iota(jnp.int32, (TQ, TQ), 0)
    qry = lax.broadcasted_iota(jnp.int32, (TQ, TQ), 1)
    n_back = WINDOW // TQ

    def scores(bb, j, kind):
        kvj = kv_ref[bb, pl.ds(pl.multiple_of(j * TQ, TQ), TQ), :].astype(BF16)
        s = _dot_nt(kvj, qs_scr[bb]) + band_scr[i - j]
        mask = None
        if kind == "diag":
            mask = qry >= key
        elif kind == "far":
            mask = qry < key
        if selected:
            r = sel_ref[bb, 0, pl.ds(2 * j, 2), :]
            picked = jnp.where(key < SLC_BLOCK, r[0:1, :], r[1:2, :]) > 0.5
            mask = picked if mask is None else (mask & picked)
        if mask is not None:
            s = jnp.concatenate([jnp.where(mask, s[:, g * TQ:(g + 1) * TQ], NEG_BIG) for g in range(NSA_GROUP)], axis=1)
        return s, jnp.where(is_key_lane, jnp.ones_like(kvj), kvj)

    def update(tile_ids):
        all_tiles = [[scores(bb, j, kind) for j, kind in tile_ids] for bb in range(n_batch)]
        for bb, tiles in enumerate(all_tiles):
            m_old = m_scr[bb]
            m_new = m_old
            for s, _ in tiles:
                m_new = jnp.maximum(m_new, jnp.max(s, axis=0, keepdims=True))
            alpha = jnp.exp(m_old - m_new)
            l_new = alpha * l_scr[bb]
            acc = alpha * acc_scr[bb]
            for s, ones_v in tiles:
                p = jnp.exp(s - m_new)
                pv = _dot_tn(ones_v, p.astype(BF16))
                l_new = l_new + pv[0:1, :]
                acc = acc + pv[NSA_HEAD_DIM:2 * NSA_HEAD_DIM, :]
            m_scr[bb] = m_new
            l_scr[bb] = l_new
            acc_scr[bb] = acc

    def full_tiles(first, count):
        return [(first + k, "full") for k in range(count)]

    if selected:
        def group(t, carry):
            update(full_tiles(FLASH_GROUP * t, FLASH_GROUP))
            return carry

        n_groups = i // FLASH_GROUP
        lax.fori_loop(0, n_groups, group, 0)
        for rem in range(FLASH_GROUP):
            @pl.when(i - n_groups * FLASH_GROUP == rem)
            def _():
                update(full_tiles(n_groups * FLASH_GROUP, rem) + [(i, "diag")])
    else:
        @pl.when(i >= n_back)
        def _():
            update([(i - n_back, "far")] + full_tiles(i - n_back + 1, n_back - 1) + [(i, "diag")])

        for rem in range(n_back):
            @pl.when(i == rem)
            def _():
                update(full_tiles(0, rem) + [(i, "diag")])

    for bb in range(n_batch):
        for g in range(NSA_GROUP):
            cols = slice(g * TQ, (g + 1) * TQ)
            o_ref[bb, :, g * NSA_HEAD_DIM:(g + 1) * NSA_HEAD_DIM] = (acc_scr[bb, :, cols] / l_scr[bb, :, cols]).T


def _flash(u, kv_col, rel_bias, sel_t, batch, seq):
    ni = seq // TQ
    qcol = COL_QNSA // (NSA_GROUP * NSA_HEAD_DIM)
    selected = sel_t is not None
    n_delta = ni if selected else WINDOW // TQ + 1
    kvblk = kv_col // LANE
    delta = np.arange(n_delta)[:, None, None] * TQ + np.arange(TQ)[None, None, :] - np.arange(TQ)[None, :, None]
    buckets = jnp.asarray(_bucket_np(delta))
    u3 = u.reshape(batch, seq, u.shape[1])
    in_specs = [pl.BlockSpec(memory_space=pltpu.SMEM),
                pl.BlockSpec((batch, TQ, NSA_GROUP * NSA_HEAD_DIM), lambda h, i: (0, i, qcol + h)),
                pl.BlockSpec((batch, seq, LANE), lambda h, i: (0, 0, kvblk + h)),
                pl.BlockSpec((n_delta, TQ, TQ), lambda h, i: (0, 0, 0))]
    args = [rel_bias.astype(F32), u3, u3, buckets]
    if selected:
        n_slc = sel_t.shape[2]
        in_specs.append(pl.BlockSpec((batch, 1, n_slc, TQ), lambda h, i: (0, h, 0, i)))
        args.append(sel_t)
    scratch = [pltpu.VMEM((n_delta, TQ, NSA_GROUP * TQ), F32),
               pltpu.VMEM((batch, NSA_GROUP * TQ, LANE), BF16),
               pltpu.VMEM((batch, 1, NSA_GROUP * TQ), F32),
               pltpu.VMEM((batch, 1, NSA_GROUP * TQ), F32),
               pltpu.VMEM((batch, NSA_HEAD_DIM, NSA_GROUP * TQ), F32)]
    out = pl.pallas_call(
        functools.partial(_flash_kernel, selected=selected, n_delta=n_delta),
        grid=(NSA_KV_HEADS, ni),
        in_specs=in_specs,
        out_specs=pl.BlockSpec((batch, TQ, NSA_GROUP * NSA_HEAD_DIM), lambda h, i: (0, i, h)),
        out_shape=jax.ShapeDtypeStruct((batch, seq, D_NSA), F32),
        scratch_shapes=scratch,
        compiler_params=_cparams(("arbitrary", "arbitrary")),
        name="slc_attn" if selected else "win_attn",
    )(*args)
    return out.reshape(batch * seq, D_NSA)


def _gate_expand():
    e = np.zeros((3, 2 * LANE, D_NSA), np.float32)
    for br in range(3):
        for hd in range(NSA_HEADS):
            for part in range(2):
                e[br, part * LANE + 2 * ML_HEADS + br * NSA_HEADS + hd,
                  hd * NSA_HEAD_DIM:(hd + 1) * NSA_HEAD_DIM] = 1.0
    return jnp.asarray(e, BF16)


def _outproj_kernel(yml_ref, oc_ref, os_ref, ow_ref, s_ref, e_ref, ng_ref, w_ref, x_ref, o_ref, cat_scr):
    @pl.when(pl.program_id(1) == 0)
    def _():
        sig = jax.nn.sigmoid(s_ref[...])
        hi_lo = jnp.concatenate(_split2(sig), axis=1)
        o = jnp.zeros(oc_ref.shape, F32)
        for br, ref in enumerate((oc_ref, os_ref, ow_ref)):
            o = o + _dot(hi_lo, e_ref[br]) * ref[...]
        o = o * lax.rsqrt(jnp.mean(o * o, axis=-1, keepdims=True) + EPS) * ng_ref[...]
        cat_scr[:, 0:D_ML] = yml_ref[...]
        cat_scr[:, D_ML:D_ML + D_NSA] = o.astype(BF16)

    o_ref[...] = x_ref[...] + _dot(cat_scr[...], w_ref[...])


def _outproj(y_ml, o_c, o_s, o_w, small, nsa_g, w_out_b, x2d, tm, tn):
    n, d = x2d.shape
    rows = lambda i, j: (i, 0)
    return pl.pallas_call(
        _outproj_kernel,
        grid=(n // tm, d // tn),
        in_specs=[pl.BlockSpec((tm, D_ML), rows), pl.BlockSpec((tm, D_NSA), rows),
                  pl.BlockSpec((tm, D_NSA), rows), pl.BlockSpec((tm, D_NSA), rows),
                  pl.BlockSpec((tm, LANE), rows),
                  pl.BlockSpec((3, 2 * LANE, D_NSA), lambda i, j: (0, 0, 0)),
                  pl.BlockSpec((1, D_NSA), lambda i, j: (0, 0)),
                  pl.BlockSpec((D_ML + D_NSA, tn), lambda i, j: (0, j)),
                  pl.BlockSpec((tm, tn), lambda i, j: (i, j))],
        out_specs=pl.BlockSpec((tm, tn), lambda i, j: (i, j)),
        out_shape=jax.ShapeDtypeStruct((n, d), F32),
        scratch_shapes=[pltpu.VMEM((tm, D_ML + D_NSA), BF16)],
        compiler_params=_cparams(("arbitrary", "arbitrary")),
        name="outproj",
    )(y_ml, o_c, o_s, o_w, small, _gate_expand(), nsa_g, w_out_b, x2d)


ROUTE_COL = N_GROUPS
BIG_COL = 1 << 20
SPLIT = D_MODEL // LANE
DMA_UNROLL = 8


def _store_split(ref, val, n):
    for k in range(SPLIT):
        ref[pl.ds(k, n, stride=SPLIT), :] = val[:, k * LANE:(k + 1) * LANE]


def _load_split(ref, n):
    return jnp.concatenate([ref[pl.ds(k, n, stride=SPLIT), :] for k in range(SPLIT)], axis=1)


PACKED_SPLIT = SPLIT // 2
HIGH_HALF = 0xFFFF0000


def _store_packed(ref, val, n):
    bits = pltpu.bitcast(val.astype(BF16).astype(F32), jnp.uint32)
    half = D_MODEL // 2
    packed = (bits[:, half:] & jnp.uint32(HIGH_HALF)) | (bits[:, :half] >> 16)
    for k in range(PACKED_SPLIT):
        ref[pl.ds(k, n, stride=PACKED_SPLIT), :] = packed[:, k * LANE:(k + 1) * LANE]


def _load_packed(ref, n):
    packed = jnp.concatenate([ref[pl.ds(k, n, stride=PACKED_SPLIT), :] for k in range(PACKED_SPLIT)], axis=1)
    low = pltpu.bitcast(packed << 16, F32)
    high = pltpu.bitcast(packed & jnp.uint32(HIGH_HALF), F32)
    return jnp.concatenate([low, high], axis=1).astype(BF16)


def _route_kernel(y_ref, g_ref, w_ref, b_ref, h_ref, info_ref, cnt_ref, carry_scr, *, tm):
    @pl.when(pl.program_id(0) == 0)
    def _():
        carry_scr[...] = jnp.zeros(carry_scr.shape, F32)

    x = y_ref[...]
    h = x * lax.rsqrt(jnp.mean(x * x, axis=-1, keepdims=True) + EPS) * g_ref[...]
    _store_packed(h_ref, h, tm)
    logit = _dot(h.astype(BF16), w_ref[...]) + b_ref[...]
    col = lax.broadcasted_iota(jnp.int32, logit.shape, 1)
    is_grp = col < N_GROUPS
    gmax = jnp.max(jnp.where(is_grp, logit, -jnp.inf), axis=1, keepdims=True)
    gtop = jnp.min(jnp.where(is_grp & (logit == gmax), col, BIG_COL), axis=1, keepdims=True)
    gsum = jnp.sum(jnp.where(is_grp, jnp.exp(logit - gmax), 0.0), axis=1, keepdims=True)
    first = ROUTE_COL + gtop * EXPERTS_PER_GROUP
    in_grp = (col >= first) & (col < first + EXPERTS_PER_GROUP)
    v1 = jnp.max(jnp.where(in_grp, logit, -jnp.inf), axis=1, keepdims=True)
    i1 = jnp.min(jnp.where(in_grp & (logit == v1), col, BIG_COL), axis=1, keepdims=True)
    rest = in_grp & (col != i1)
    v2 = jnp.max(jnp.where(rest, logit, -jnp.inf), axis=1, keepdims=True)
    i2 = jnp.min(jnp.where(rest & (logit == v2), col, BIG_COL), axis=1, keepdims=True)
    e = jnp.exp(v2 - v1)
    w1 = 1.0 / ((1.0 + e) * gsum)
    w2 = e / ((1.0 + e) * gsum)
    pick1 = col == i1
    pick2 = col == i2
    both = (pick1 | pick2).astype(F32)
    ri = lax.broadcasted_iota(jnp.int32, (tm, tm), 0)
    ci = lax.broadcasted_iota(jnp.int32, (tm, tm), 1)
    before = (ci < ri).astype(BF16)
    cum = _dot(before, both.astype(BF16)) + carry_scr[...]
    r1 = jnp.sum(jnp.where(pick1, cum, 0.0), axis=1, keepdims=True)
    r2 = jnp.sum(jnp.where(pick2, cum, 0.0), axis=1, keepdims=True)
    carry_scr[...] = carry_scr[...] + jnp.sum(both, axis=0, keepdims=True)
    cnt_ref[...] = jnp.broadcast_to(carry_scr[...], cnt_ref.shape)
    info = jnp.where(col == 0, (i1 - ROUTE_COL).astype(F32), 0.0)
    info = jnp.where(col == 1, (i2 - ROUTE_COL).astype(F32), info)
    info = jnp.where(col == 2, w1, info)
    info = jnp.where(col == 3, w2, info)
    info = jnp.where(col == 4, r1, info)
    info_ref[...] = jnp.where(col == 5, r2, info)


def _route(y2d, g, w_rg, b_rg, w_re, b_re, tm):
    n, d = y2d.shape
    wr = jnp.concatenate([w_rg, w_re, jnp.zeros((d, LANE - N_GROUPS - N_EXPERTS), F32)], axis=1).astype(BF16)
    bias = jnp.concatenate([b_rg, b_re, jnp.zeros((LANE - N_GROUPS - N_EXPERTS,), F32)])[None, :]
    rows = lambda i: (i, 0)
    fixed = lambda i: (0, 0)
    return pl.pallas_call(
        functools.partial(_route_kernel, tm=tm),
        grid=(n // tm,),
        in_specs=[pl.BlockSpec((tm, d), rows), pl.BlockSpec((1, d), fixed),
                  pl.BlockSpec((d, LANE), fixed), pl.BlockSpec((1, LANE), fixed)],
        out_specs=[pl.BlockSpec((tm * PACKED_SPLIT, LANE), rows), pl.BlockSpec((tm, LANE), rows),
                   pl.BlockSpec((8, LANE), fixed)],
        out_shape=[jax.ShapeDtypeStruct((n * PACKED_SPLIT, LANE), jnp.uint32), jax.ShapeDtypeStruct((n, LANE), F32),
                   jax.ShapeDtypeStruct((8, LANE), F32)],
        scratch_shapes=[pltpu.VMEM((1, LANE), F32)],
        compiler_params=_cparams(("arbitrary",)),
        name="moe_route",
    )(y2d, g, wr, bias)


W_CHUNKS = 4


def _expert_kernel(te_ref, nu_ref, src_ref, nx_ref, ws_ref, h_hbm, wg_hbm, wu_hbm, wd_hbm, o_ref,
                   xbuf, wg_f, wu_f, wd_f, wg_b, wu_b, wd_b, sem, wsem, *, tme):
    i = pl.program_id(0)
    slot = i % 2

    def weight_copies(expert, wslot):
        copies = []
        for hbm, buf in ((wg_hbm, wg_f), (wu_hbm, wu_f), (wd_hbm, wd_f)):
            step = hbm.shape[1] // W_CHUNKS
            for c in range(W_CHUNKS):
                copies.append(pltpu.make_async_copy(hbm.at[expert, pl.ds(c * step, step)],
                                                    buf.at[wslot, pl.ds(c * step, step)], wsem.at[wslot]))
        return copies

    def row_copy(tile, r, buf_slot):
        src_row = src_ref[tile * tme + r]
        return pltpu.make_async_copy(
            h_hbm.at[pl.ds(pl.multiple_of(src_row * PACKED_SPLIT, PACKED_SPLIT), PACKED_SPLIT)],
            xbuf.at[buf_slot, pl.ds(pl.multiple_of(r * PACKED_SPLIT, PACKED_SPLIT), PACKED_SPLIT)],
            sem.at[buf_slot])

    def fetch(tile, buf_slot):
        def body(r, carry):
            row_copy(tile, r, buf_slot).start()
            return carry
        lax.fori_loop(0, tme, body, 0, unroll=DMA_UNROLL)

    @pl.when(i == 0)
    def _():
        fetch(0, 0)
        for cp in weight_copies(te_ref[0], ws_ref[0]):
            cp.start()

    @pl.when(i + 1 < nu_ref[0])
    def _():
        fetch(i + 1, 1 - slot)

    @pl.when(i < nu_ref[0])
    def _():
        prev = te_ref[jnp.maximum(i - 1, 0)]

        @pl.when((i == 0) | (te_ref[i] != prev))
        def _():
            w = ws_ref[i]
            for cp in weight_copies(te_ref[i], w):
                cp.wait()
            wg_b[...] = wg_f[w].astype(BF16)
            wu_b[...] = wu_f[w].astype(BF16)
            wd_b[...] = wd_f[w].astype(BF16)

            @pl.when(nx_ref[i] >= 0)
            def _():
                for cp in weight_copies(nx_ref[i], 1 - w):
                    cp.start()

        pltpu.make_async_copy(h_hbm.at[pl.ds(0, tme * PACKED_SPLIT)], xbuf.at[slot], sem.at[slot]).wait()

        x = _load_packed(xbuf.at[slot], tme)
        a = _dot(x, wg_b[...])
        u = _dot(x, wu_b[...])
        hid = (a * jax.nn.sigmoid(a) * u).astype(BF16)
        for c in range(SPLIT // 2):
            part = _dot(hid, wd_b[:, c * 2 * LANE:(c + 1) * 2 * LANE])
            for kk in range(2):
                o_ref[pl.ds(2 * c + kk, tme, stride=SPLIT), :] = part[:, kk * LANE:(kk + 1) * LANE]

    @pl.when(i >= nu_ref[0])
    def _():
        o_ref[...] = jnp.zeros(o_ref.shape, F32)


def _experts(h, src, tile_expert, n_used, next_expert, weight_slot, w_gate, w_up, w_down, tme):
    p = src.shape[0]
    d = D_MODEL
    hbm = pl.BlockSpec(memory_space=pl.ANY)
    return pl.pallas_call(
        functools.partial(_expert_kernel, tme=tme),
        grid_spec=pltpu.PrefetchScalarGridSpec(
            num_scalar_prefetch=5, grid=(p // tme,),
            in_specs=[hbm, hbm, hbm, hbm],
            out_specs=pl.BlockSpec((tme * SPLIT, LANE), lambda i, *_: (i, 0)),
            scratch_shapes=[pltpu.VMEM((2, tme * PACKED_SPLIT, LANE), jnp.uint32),
                            pltpu.VMEM((2, d, D_EXPERT), F32), pltpu.VMEM((2, d, D_EXPERT), F32),
                            pltpu.VMEM((2, D_EXPERT, d), F32),
                            pltpu.VMEM((d, D_EXPERT), BF16), pltpu.VMEM((d, D_EXPERT), BF16),
                            pltpu.VMEM((D_EXPERT, d), BF16),
                            pltpu.SemaphoreType.DMA((2,)), pltpu.SemaphoreType.DMA((2,))]),
        out_shape=jax.ShapeDtypeStruct((p * SPLIT, LANE), F32),
        compiler_params=_cparams(("arbitrary",)),
        name="moe_experts",
    )(tile_expert, n_used, src, next_expert, weight_slot, h, w_gate, w_up, w_down)


def _combine_kernel(d1_ref, d2_ref, y_ref, info_ref, ys_hbm, g_ref, o_ref, a_buf, b_buf, sem, *, tm):
    i = pl.program_id(0)
    slot = i % 2

    def copy(r, row, buf, buf_slot):
        return pltpu.make_async_copy(ys_hbm.at[pl.ds(pl.multiple_of(row * SPLIT, SPLIT), SPLIT)],
                                     buf.at[buf_slot, pl.ds(pl.multiple_of(r * SPLIT, SPLIT), SPLIT)],
                                     sem.at[buf_slot])

    def fetch(tile, buf_slot):
        def issue(r, carry):
            copy(r, d1_ref[tile * tm + r], a_buf, buf_slot).start()
            copy(r, d2_ref[tile * tm + r], b_buf, buf_slot).start()
            return carry
        lax.fori_loop(0, tm, issue, 0, unroll=DMA_UNROLL)

    @pl.when(i == 0)
    def _():
        fetch(0, 0)

    @pl.when(i + 1 < pl.num_programs(0))
    def _():
        fetch(i + 1, 1 - slot)

    for buf in (a_buf, b_buf):
        pltpu.make_async_copy(ys_hbm.at[pl.ds(0, tm * SPLIT)], buf.at[slot], sem.at[slot]).wait()
    w1 = info_ref[:, 2:3]
    w2 = info_ref[:, 3:4]
    y = y_ref[...] + w1 * _load_split(a_buf.at[slot], tm) + w2 * _load_split(b_buf.at[slot], tm)
    o_ref[...] = y * lax.rsqrt(jnp.mean(y * y, axis=-1, keepdims=True) + EPS) * g_ref[...]


def _combine(y2d, info, ys, d1, d2, g, tm):
    n, d = y2d.shape
    return pl.pallas_call(
        functools.partial(_combine_kernel, tm=tm),
        grid_spec=pltpu.PrefetchScalarGridSpec(
            num_scalar_prefetch=2, grid=(n // tm,),
            in_specs=[pl.BlockSpec((tm, d), lambda i, a, b: (i, 0)),
                      pl.BlockSpec((tm, LANE), lambda i, a, b: (i, 0)),
                      pl.BlockSpec(memory_space=pl.ANY),
                      pl.BlockSpec((1, d), lambda i, a, b: (0, 0))],
            out_specs=pl.BlockSpec((tm, d), lambda i, a, b: (i, 0)),
            scratch_shapes=[pltpu.VMEM((2, tm * SPLIT, LANE), F32), pltpu.VMEM((2, tm * SPLIT, LANE), F32),
                            pltpu.SemaphoreType.DMA((2,))]),
        out_shape=jax.ShapeDtypeStruct((n, d), F32),
        compiler_params=_cparams(("arbitrary",)),
        name="moe_combine",
    )(d1, d2, y2d, info, ys, g)


def _moe_final(y2d, ffn_g, w_rg, b_rg, w_re, b_re, w_gate, w_up, w_down, final_g, tm, tme):
    n, d = y2d.shape
    h, info, cnt = _route(y2d, ffn_g, w_rg, b_rg, w_re, b_re, tm)
    e1 = info[:, 0].astype(jnp.int32)
    e2 = info[:, 1].astype(jnp.int32)
    counts = cnt[0, ROUTE_COL:ROUTE_COL + N_EXPERTS].astype(jnp.int32)
    padded = (counts + tme - 1) // tme * tme
    ends = jnp.cumsum(padded)
    offs = ends - padded
    d1 = offs[e1] + info[:, 4].astype(jnp.int32)
    d2 = offs[e2] + info[:, 5].astype(jnp.int32)
    p = 2 * n + N_EXPERTS * tme
    tok = jnp.arange(n, dtype=jnp.int32)
    src = jnp.zeros((p,), jnp.int32).at[jnp.concatenate([d1, d2])].set(jnp.concatenate([tok, tok]))
    n_tiles = p // tme
    n_used = (ends[-1] // tme).astype(jnp.int32).reshape(1)
    tile_start = jnp.arange(n_tiles, dtype=jnp.int32) * tme
    tile_expert = jnp.sum((ends[None, :] <= tile_start[:, None]).astype(jnp.int32), axis=1)
    last = jnp.sum((ends <= ends[-1] - 1).astype(jnp.int32))
    tile_expert = jnp.minimum(tile_expert, last).astype(jnp.int32)
    run = jnp.cumsum(jnp.concatenate([jnp.zeros((1,), jnp.int32),
                                      (tile_expert[1:] != tile_expert[:-1]).astype(jnp.int32)]))
    weight_slot = (run % 2).astype(jnp.int32)
    after = ends[tile_expert] // tme
    next_expert = jnp.where(after < n_used[0], tile_expert[jnp.minimum(after, n_tiles - 1)], -1).astype(jnp.int32)
    ys = _experts(h, src, tile_expert, n_used, next_expert, weight_slot, w_gate, w_up, w_down, tme)
    return _combine(y2d, info, ys, d1, d2, final_g, tm)


GROUP_ROWS = 8
SLC_LANES = 384


def _sample_cattn_kernel(q_ref, kc_ref, vc_ref, b_ref, ov_ref, o_ref, sel_ref, *, n_cmp, n_slc):
    q = (q_ref[0, 0] * (NSA_HEAD_DIM ** -0.5)).astype(BF16)
    s = _dot_nt(q, kc_ref[0, 0].astype(BF16)) + b_ref[0]
    valid = lax.broadcasted_iota(jnp.int32, s.shape, 1) < n_cmp
    s = jnp.where(valid, s, NEG_BIG)
    e = jnp.exp(s - jnp.max(s, axis=1, keepdims=True))
    p = jnp.where(valid, e / jnp.sum(e, axis=1, keepdims=True), 0.0)
    o_ref[0, 0] = _dot(p.astype(BF16), vc_ref[0, 0].astype(BF16))
    pg = jnp.where(lax.broadcasted_iota(jnp.int32, p.shape, 0) < NSA_GROUP, p, 0.0)
    pc = jnp.broadcast_to(jnp.sum(pg, axis=0, keepdims=True), pg.shape)
    score = _dot(pc.astype(BF16), ov_ref[...])[0:1, :]
    blk = lax.broadcasted_iota(jnp.int32, score.shape, 1)
    cur = n_slc - 1
    forced = (blk == 0) | (blk == cur) | (blk == cur - 1)
    score = jnp.where(forced, FORCE_SCORE, score)
    score = jnp.where(blk > cur, -jnp.inf, score)
    ri = lax.broadcasted_iota(jnp.int32, (SLC_LANES, SLC_LANES), 0)
    ci = lax.broadcasted_iota(jnp.int32, (SLC_LANES, SLC_LANES), 1)
    sb = jnp.broadcast_to(score, (SLC_LANES, SLC_LANES))
    col = jnp.sum(jnp.where(ri == ci, sb, 0.0), axis=1, keepdims=True)
    beats = (col > sb) | ((col == sb) & (ri < ci))
    rank = jnp.sum(beats.astype(jnp.int32), axis=0, keepdims=True)
    sel_ref[0, 0] = jnp.broadcast_to((rank < SLC_TOPN).astype(F32), (GROUP_ROWS, SLC_LANES))


def _sample_cattn(q8, kvc, bias_cs, n_cmp, n_slc):
    bsz, _, nch, _ = kvc.shape
    c0 = np.arange(nch) * CMP_STRIDE
    s0 = np.arange(SLC_LANES) * SLC_BLOCK
    ov = np.minimum(c0[:, None] + CMP_BLOCK, s0[None, :] + SLC_BLOCK) - np.maximum(c0[:, None], s0[None, :])
    ov = np.clip(ov, 0, None).astype(np.float32) / CMP_BLOCK
    ov[n_cmp:, :] = 0.0
    ov[:, n_slc:] = 0.0
    return pl.pallas_call(
        functools.partial(_sample_cattn_kernel, n_cmp=n_cmp, n_slc=n_slc),
        grid=(bsz, NSA_KV_HEADS),
        in_specs=[pl.BlockSpec((1, 1, GROUP_ROWS, NSA_HEAD_DIM), lambda b, h: (b, h, 0, 0)),
                  pl.BlockSpec((1, 1, nch, NSA_HEAD_DIM), lambda b, h: (b, h, 0, 0)),
                  pl.BlockSpec((1, 1, nch, NSA_HEAD_DIM), lambda b, h: (b, NSA_KV_HEADS + h, 0, 0)),
                  pl.BlockSpec((1, GROUP_ROWS, nch), lambda b, h: (h, 0, 0)),
                  pl.BlockSpec((nch, SLC_LANES), lambda b, h: (0, 0))],
        out_specs=[pl.BlockSpec((1, 1, GROUP_ROWS, NSA_HEAD_DIM), lambda b, h: (b, h, 0, 0)),
                   pl.BlockSpec((1, 1, GROUP_ROWS, SLC_LANES), lambda b, h: (b, h, 0, 0))],
        out_shape=[jax.ShapeDtypeStruct((bsz, NSA_KV_HEADS, GROUP_ROWS, NSA_HEAD_DIM), F32),
                   jax.ShapeDtypeStruct((bsz, NSA_KV_HEADS, GROUP_ROWS, SLC_LANES), F32)],
        compiler_params=_cparams(("arbitrary", "arbitrary")),
        name="sample_cmp_attn",
    )(q8, kvc, kvc, bias_cs, jnp.asarray(ov, BF16))


def _sample_attn_kernel(pg_ref, hf_ref, q_ref, cache_hbm, snew_ref, bs_ref, win_ref, wnew_ref, bw_ref,
                        os_ref, ow_ref, kt_buf, vt_buf, sem, *, n_gather, wlen):
    b = pl.program_id(0)
    rows_per_page = 2 * NSA_KV_HEADS * NSA_HEAD_DIM

    def page_copy(h, s, c, buf):
        page = pg_ref[(b * NSA_KV_HEADS + h) * n_gather + s]
        start = pl.multiple_of(page * rows_per_page + (c * NSA_KV_HEADS + h) * NSA_HEAD_DIM, NSA_HEAD_DIM)
        return pltpu.make_async_copy(cache_hbm.at[pl.ds(start, NSA_HEAD_DIM)],
                                     buf.at[h, :, pl.ds(s * PAGE_SIZE, PAGE_SIZE)], sem)

    for h in range(NSA_KV_HEADS):
        for s in range(n_gather):
            page_copy(h, s, 0, kt_buf).start()
            page_copy(h, s, 1, vt_buf).start()

    def attend(q, kts, vts, bias, valid):
        s = jnp.concatenate([_dot(q, kt.astype(BF16)) for kt in kts], axis=1) + bias
        s = jnp.where(valid, s, NEG_BIG)
        e = jnp.exp(s - jnp.max(s, axis=1, keepdims=True))
        p = jnp.where(valid, e / jnp.sum(e, axis=1, keepdims=True), 0.0).astype(BF16)
        out = None
        lo = 0
        for vt in vts:
            n = vt.shape[1]
            part = _dot_nt(p[:, lo:lo + n], vt.astype(BF16))
            out = part if out is None else out + part
            lo += n
        return out

    def head_rows(ref, c, h):
        r0 = (c * NSA_KV_HEADS + h) * NSA_HEAD_DIM
        return ref[0, r0:r0 + NSA_HEAD_DIM, :]

    widx = lax.broadcasted_iota(jnp.int32, (GROUP_ROWS, wlen + PAGE_SIZE), 1)
    wvalid = (widx >= wlen + 1 - WINDOW) & (widx <= wlen)
    for h in range(NSA_KV_HEADS):
        q = (q_ref[0, h] * (NSA_HEAD_DIM ** -0.5)).astype(BF16)
        ow_ref[0, h] = attend(q, [head_rows(win_ref, 0, h), head_rows(wnew_ref, 0, h)],
                              [head_rows(win_ref, 1, h), head_rows(wnew_ref, 1, h)], bw_ref[h], wvalid)

    for h in range(NSA_KV_HEADS):
        for s in range(n_gather):
            page_copy(h, s, 0, kt_buf).wait()
            page_copy(h, s, 1, vt_buf).wait()

    keys = n_gather * PAGE_SIZE
    lane = lax.broadcasted_iota(jnp.int32, (GROUP_ROWS, keys + PAGE_SIZE), 1)
    slot = lane >> 7
    lane_half = (lane >> 6) & 1
    for h in range(NSA_KV_HEADS):
        want = jnp.full(lane.shape, -1, jnp.int32)
        for s in range(n_gather):
            want = jnp.where(slot == s, hf_ref[(b * NSA_KV_HEADS + h) * n_gather + s], want)
        svalid = (lane_half == want) | (lane == keys)
        q = (q_ref[0, h] * (NSA_HEAD_DIM ** -0.5)).astype(BF16)
        os_ref[0, h] = attend(q, [kt_buf[h], head_rows(snew_ref, 0, h)],
                              [vt_buf[h], head_rows(snew_ref, 1, h)], bs_ref[0, h], svalid)


def _sample_attn(pages, halves, q8, cache_t, snew_t, bias_sel, win_t, wnew_t, bias_w, n_gather):
    bsz = q8.shape[0]
    wlen = win_t.shape[2]
    keys = n_gather * PAGE_SIZE
    rows = 2 * NSA_KV_HEADS * NSA_HEAD_DIM
    qspec = pl.BlockSpec((1, NSA_KV_HEADS, GROUP_ROWS, NSA_HEAD_DIM), lambda b, pg, hf: (b, 0, 0, 0))
    newspec = pl.BlockSpec((1, rows, PAGE_SIZE), lambda b, pg, hf: (b, 0, 0))
    return pl.pallas_call(
        functools.partial(_sample_attn_kernel, n_gather=n_gather, wlen=wlen),
        grid_spec=pltpu.PrefetchScalarGridSpec(
            num_scalar_prefetch=2, grid=(bsz,),
            in_specs=[qspec,
                      pl.BlockSpec(memory_space=pl.ANY),
                      newspec,
                      pl.BlockSpec((1, NSA_KV_HEADS, GROUP_ROWS, keys + PAGE_SIZE), lambda b, pg, hf: (b, 0, 0, 0)),
                      pl.BlockSpec((1, rows, wlen), lambda b, pg, hf: (b, 0, 0)),
                      newspec,
                      pl.BlockSpec((NSA_KV_HEADS, GROUP_ROWS, wlen + PAGE_SIZE), lambda b, pg, hf: (0, 0, 0))],
            out_specs=[qspec, qspec],
            scratch_shapes=[pltpu.VMEM((NSA_KV_HEADS, NSA_HEAD_DIM, keys), F32),
                            pltpu.VMEM((NSA_KV_HEADS, NSA_HEAD_DIM, keys), F32),
                            pltpu.SemaphoreType.DMA(())]),
        out_shape=[jax.ShapeDtypeStruct(q8.shape, F32), jax.ShapeDtypeStruct(q8.shape, F32)],
        compiler_params=_cparams(("arbitrary",)),
        name="sample_slc_win_attn",
    )(pages, halves, q8, cache_t, snew_t, bias_sel, win_t, wnew_t, bias_w)


PAGES_PER_STEP = 16


def _paged_pq_kernel(pt_ref, cache_hbm, pos_ref, w_ref, o_ref, pbuf, tok_scr, sem):
    b = pl.program_id(0)
    g = pl.program_id(1)
    ng = pl.num_programs(1)
    lin = b * ng + g
    slot = lin % 2
    rows_per_page = 2 * NSA_KV_HEADS * NSA_HEAD_DIM
    chunks = PAGES_PER_STEP * (PAGE_SIZE // CMP_STRIDE)

    def page_copy(step, p, buf_slot):
        sb = step // ng
        sg = step - sb * ng
        page = pt_ref[sb, sg * PAGES_PER_STEP + p]
        return pltpu.make_async_copy(cache_hbm.at[pl.ds(pl.multiple_of(page * rows_per_page, rows_per_page),
                                                        rows_per_page)],
                                     pbuf.at[buf_slot, p], sem.at[buf_slot])

    def fetch(step, buf_slot):
        for p in range(PAGES_PER_STEP):
            page_copy(step, p, buf_slot).start()

    @pl.when(lin == 0)
    def _():
        fetch(0, 0)

    @pl.when(lin + 1 < pl.num_programs(0) * ng)
    def _():
        fetch(lin + 1, 1 - slot)

    for p in range(PAGES_PER_STEP):
        page_copy(lin, p, slot).wait()

    for c in range(2):
        head_chunks = []
        for pair in (2 * c, 2 * c + 1):
            for p in range(PAGES_PER_STEP):
                tok_scr[p * PAGE_SIZE:(p + 1) * PAGE_SIZE, :] = pbuf[slot, p, pair * LANE:(pair + 1) * LANE, :].T
            toks = [tok_scr[pl.ds(l, chunks, stride=CMP_STRIDE), :] for l in range(CMP_STRIDE)]
            for half in range(2):
                lanes = slice(half * NSA_HEAD_DIM, (half + 1) * NSA_HEAD_DIM)
                head_chunks.append(jnp.concatenate([t[:, lanes] for t in toks], axis=1))
        stacked = jnp.concatenate(head_chunks, axis=0)
        for part in range(2):
            lhs = (stacked + pos_ref[c, part:part + 1, :]).astype(BF16)
            out = _dot(lhs, w_ref[c, part])
            for hh in range(NSA_KV_HEADS):
                o_ref[0, c * NSA_KV_HEADS + hh, :, part * CMP_HIDDEN:(part + 1) * CMP_HIDDEN] = (
                    out[hh * chunks:(hh + 1) * chunks, :])


def _paged_pq(cache_t, page_table, cmp_pos, cmp_w1):
    db, n_pages = page_table.shape
    kdim = CMP_STRIDE * NSA_HEAD_DIM
    nch = n_pages * (PAGE_SIZE // CMP_STRIDE)
    tn = PAGES_PER_STEP * (PAGE_SIZE // CMP_STRIDE)
    rows_per_page = 2 * NSA_KV_HEADS * NSA_HEAD_DIM
    return pl.pallas_call(
        _paged_pq_kernel,
        grid_spec=pltpu.PrefetchScalarGridSpec(
            num_scalar_prefetch=1, grid=(db, n_pages // PAGES_PER_STEP),
            in_specs=[pl.BlockSpec(memory_space=pl.ANY),
                      pl.BlockSpec((2, 2, kdim), lambda b, g, pt: (0, 0, 0)),
                      pl.BlockSpec((2, 2, kdim, CMP_HIDDEN), lambda b, g, pt: (0, 0, 0, 0))],
            out_specs=pl.BlockSpec((1, 8, tn, 2 * CMP_HIDDEN), lambda b, g, pt: (b, 0, g, 0)),
            scratch_shapes=[pltpu.VMEM((2, PAGES_PER_STEP, rows_per_page, PAGE_SIZE), F32),
                            pltpu.VMEM((PAGES_PER_STEP * PAGE_SIZE, LANE), F32),
                            pltpu.SemaphoreType.DMA((2,))]),
        out_shape=jax.ShapeDtypeStruct((db, 8, nch, 2 * CMP_HIDDEN), F32),
        compiler_params=_cparams(("arbitrary", "arbitrary")),
        name="cmp_pq_paged",
    )(page_table, cache_t, cmp_pos.reshape(2, 2, kdim), cmp_w1.reshape(2, 2, kdim, CMP_HIDDEN).astype(BF16))


def _pad_rows(a, n):
    return jnp.concatenate([a, jnp.zeros((n - a.shape[0],) + a.shape[1:], a.dtype)], axis=0)


def _sample_mixer(x_sample, cache_cmp, cache_slc, cache_win, st_c, st_n, st_m, st_conv, page_table,
                  norm_g, wb, gate_bias, conv_w, ml_norm_g, cmp_pos, cmp_w1, cmp_w2, nsa_norm_g, w_out_b, rel_bias):
    db = x_sample.shape[0]
    n_pages = page_table.shape[1]
    past = n_pages * PAGE_SIZE
    tok = SAMPLE_TOKEN_ROWS
    x16 = _pad_rows(x_sample.reshape(db, D_MODEL), tok)
    u = _proj(x16, norm_g, wb, tok, PROJ_COLS)
    small = u[:, COL_SMALL:COL_SMALL + LANE]

    T = SAMPLE_SEQ_ROWS
    useq = jnp.zeros((db, T, 4 * D_ML), F32)
    useq = useq.at[:, T - CONV_W:T - 1, 0:2 * D_ML].set(st_conv)
    useq = useq.at[:, T - 1, :].set(u[:db, 0:4 * D_ML])
    sseq = jnp.zeros((db, T, LANE), F32).at[:, :T - 1, LANE - 1].set(1.0)
    sseq = sseq.at[:, T - 1, :].set(small[:db])
    y_seq, conv_n, c_n, n_n, m_n = _mlstm(
        useq.reshape(db * T, 4 * D_ML), sseq.reshape(db * T, LANE), jnp.zeros((db, CONV_W - 1, 2 * D_ML), F32),
        st_c, st_n, st_m, conv_w, gate_bias, ml_norm_g, db, T, T)
    y_ml = y_seq.reshape(db, T, D_ML)[:, T - 1]

    n_pool = cache_cmp.shape[0]
    cmp_t = cache_cmp.transpose(0, 2, 3, 4, 1).reshape(n_pool * 2 * D_KV, PAGE_SIZE)
    kvc = _cmp_hid(_paged_pq(cmp_t, page_table, cmp_pos, cmp_w1), cmp_w2)
    nch = past // CMP_STRIDE
    n_cmp = (past + 1) // CMP_STRIDE - CMP_BLOCK // CMP_STRIDE + 1
    n_slc = -(-(past + 1) // SLC_BLOCK)
    q = u[:db, COL_QNSA:COL_QNSA + D_NSA].reshape(db, NSA_KV_HEADS, NSA_GROUP, NSA_HEAD_DIM)
    q8 = jnp.concatenate([q, jnp.zeros_like(q)], axis=2)
    bd = _bias_by_distance(rel_bias, past + 1).reshape(NSA_KV_HEADS, NSA_GROUP, past + 1)
    pad_g = lambda a: jnp.concatenate([a, jnp.zeros_like(a)], axis=1)
    dist_c = np.clip(past - (np.arange(nch) * CMP_STRIDE + CMP_BLOCK - 1), 0, None)
    o_c8, sel = _sample_cattn(q8, kvc, pad_g(bd[:, :, dist_c]), n_cmp, n_slc)

    mask = sel[:, :, 0, :n_slc] > 0.5
    idx = jnp.sort(jnp.where(mask, jnp.arange(n_slc, dtype=jnp.int32), jnp.int32(1 << 20)), axis=-1)
    n_gather = SLC_TOPN - 1
    idx = idx[..., :n_gather]
    pages_per_block = PAGE_SIZE // SLC_BLOCK
    logical_page = idx // pages_per_block
    pages = jnp.take_along_axis(page_table[:, None, :], logical_page, axis=2).reshape(-1).astype(jnp.int32)
    halves = (idx % pages_per_block).reshape(-1).astype(jnp.int32)
    kpos = (logical_page[..., None] * PAGE_SIZE + jnp.arange(PAGE_SIZE, dtype=jnp.int32)).reshape(db, NSA_KV_HEADS, -1)
    hh = jnp.arange(NSA_KV_HEADS)[None, :, None, None]
    gg = jnp.arange(NSA_GROUP)[None, None, :, None]
    bias_sel = bd[hh, gg, (past - kpos)[:, :, None, :]]
    bias_sel = jnp.concatenate([bias_sel, jnp.broadcast_to(bd[None, :, :, 0:1], (db, NSA_KV_HEADS, NSA_GROUP, 1)),
                                jnp.zeros((db, NSA_KV_HEADS, NSA_GROUP, PAGE_SIZE - 1), F32)], axis=-1)
    bias_sel = jnp.concatenate([bias_sel, jnp.zeros_like(bias_sel)], axis=2)
    wlen = cache_win.shape[1]
    dist_w = np.clip(wlen - np.arange(wlen + PAGE_SIZE), 0, None)
    kvs_new = _kv_from_per_head(u[:db, COL_KVS:COL_KVS + 2 * D_KV])
    kvw_new = _kv_from_per_head(u[:db, COL_KVW:COL_KVW + 2 * D_KV])
    lane_pad = lambda a: jnp.pad(a[:, :, None], ((0, 0), (0, 0), (0, PAGE_SIZE - 1)))
    slc_t = cache_slc.transpose(0, 2, 3, 4, 1).reshape(n_pool * 2 * D_KV, PAGE_SIZE)
    win_t = cache_win.transpose(0, 2, 3, 4, 1).reshape(db, 2 * D_KV, wlen)
    o_s8, o_w8 = _sample_attn(pages, halves, q8, slc_t, lane_pad(kvs_new), bias_sel, win_t, lane_pad(kvw_new),
                              pad_g(bd[:, :, dist_w]), n_gather)
    win2d = cache_win.reshape(db, wlen, 2 * D_KV)

    heads = lambda o: _pad_rows(o[:, :, :NSA_GROUP, :].reshape(db, D_NSA), tok)
    y = _outproj(_pad_rows(y_ml, tok), heads(o_c8), heads(o_s8), heads(o_w8), small, nsa_norm_g, w_out_b, x16, tok,
                 PROJ_COLS)
    kvshape = (1, db, 1, 2, NSA_KV_HEADS, NSA_HEAD_DIM)
    new_win = jnp.concatenate([win2d[:, 1:], kvw_new[:, None, :]], axis=1)
    states = (u[:db, COL_KVC:COL_KVC + 2 * D_KV].reshape(kvshape), kvs_new.reshape(kvshape),
              new_win.reshape((1, db, wlen, 2, NSA_KV_HEADS, NSA_HEAD_DIM)),
              c_n[None], n_n[None], m_n[None], conv_n[None])
    return y, states


def kernel(x_prompt, x_sample, cache_cmp_kv, cache_slc_kv, cache_win_kv, state_mlstm_C, state_mlstm_n,
           state_mlstm_m, state_conv, page_table, rel_bias, norm_mix_g, w_in, b_ig, b_fg, conv_w, ml_norm_g,
           cmp_pos, cmp_w1, cmp_w2, nsa_norm_g, w_out, norm_ffn_g, w_router_grp, b_router_grp, w_router_exp,
           b_router_exp, w_gate, w_up, w_down, norm_final_g):
    B, S, D = x_prompt.shape
    wb = _reorder_w_in(w_in[0])
    gate_bias = jnp.zeros((1, LANE), F32).at[0, 0:ML_HEADS].set(b_ig[0]).at[0, ML_HEADS:2 * ML_HEADS].set(b_fg[0])
    w_out_b = w_out[0].astype(BF16)
    yp, st_p = _prompt_mixer(x_prompt, norm_mix_g, wb, gate_bias, conv_w[0], ml_norm_g, cmp_pos[0], cmp_w1[0],
                             cmp_w2[0], nsa_norm_g, w_out_b, rel_bias)
    ys, st_s = _sample_mixer(x_sample, cache_cmp_kv[0], cache_slc_kv[0], cache_win_kv[0], state_mlstm_C[0],
                             state_mlstm_n[0], state_mlstm_m[0], state_conv[0], page_table, norm_mix_g, wb, gate_bias,
                             conv_w[0], ml_norm_g, cmp_pos[0], cmp_w1[0], cmp_w2[0], nsa_norm_g, w_out_b, rel_bias)
    moe_w = (norm_ffn_g, w_router_grp[0], b_router_grp[0], w_router_exp[0], b_router_exp[0],
             w_gate[0], w_up[0], w_down[0], norm_final_g[None, :])
    DB, L, _ = x_sample.shape
    out_p = _moe_final(yp.reshape(B * S, D), *moe_w, MOE_TOKEN_ROWS, MOE_EXPERT_ROWS).reshape(B, S, D)
    out_s = _moe_final(_pad_rows(ys, SAMPLE_MOE_ROWS), *moe_w, SAMPLE_MOE_ROWS, SAMPLE_EXPERT_ROWS)[:DB]
    out_s = out_s.reshape(DB, L, D)
    outs = [out_p, out_s]
    for a, b in zip(st_p, st_s):
        outs += [a, b]
    return tuple(outs)


def _prompt_mixer(x_prompt, norm_g, wb, gate_bias, conv_w, ml_norm_g, cmp_pos, cmp_w1, cmp_w2, nsa_norm_g,
                  w_out_b, rel_bias):
    B, S, D = x_prompt.shape
    x2d = x_prompt.reshape(B * S, D)
    tm = min(PROJ_ROWS, S)
    u, kvt_c, kvt_s, kvt_w = _proj(x2d, norm_g, wb, tm, PROJ_COLS, seq=S)
    small = u[:, COL_SMALL:COL_SMALL + LANE]
    y_ml, conv_n, c_n, n_n, m_n = _mlstm(
        u, small, jnp.zeros((B, CONV_W - 1, 2 * D_ML), F32),
        jnp.zeros((B, ML_HEADS, ML_HEAD_DIM, ML_HEAD_DIM), F32), jnp.zeros((B, ML_HEADS, ML_HEAD_DIM), F32),
        jnp.full((B, ML_HEADS), -jnp.inf, F32), conv_w, gate_bias, ml_norm_g, B, S, ML_CHUNK_ROWS)
    kvc = _compress(u, COL_KVC // (2 * D_KV), B, S, cmp_pos, cmp_w1, cmp_w2, min(S, CMP_TOKENS))
    o_c, sel_t = _cattn(u, kvc, rel_bias, B, S, min(S, CMP_QUERY_ROWS))
    o_s = _flash(u, COL_KVS, rel_bias, sel_t, B, S)
    o_w = _flash(u, COL_KVW, rel_bias, None, B, S)
    y = _outproj(y_ml, o_c, o_s, o_w, small, nsa_norm_g, w_out_b, x2d, tm, PROJ_COLS)
    win = min(WINDOW, S)
    rows = lambda kvt: kvt.transpose(0, 4, 1, 2, 3)[None]
    states = (rows(kvt_c), rows(kvt_s), rows(kvt_w[..., S - win:]),
              c_n[None], n_n[None], m_n[None], conv_n[None])
    return y.reshape(B, S, D), states
```
